```python
import math
import jax, jax.numpy as jnp
from jax import lax
import numpy as np

D_MODEL = 1024
BATCH = 16
SEQ = 2048
DEPTH = 1

HEAD_DIM = 64
MIX_WIDTH = D_MODEL
ATT_HEADS = 8
GDN_HEADS = 8
ATT_WIDTH = ATT_HEADS * HEAD_DIM
GDN_WIDTH = GDN_HEADS * HEAD_DIM
MOBA_BLOCK = 256
MOBA_TOPK = 3
MOBA_QCHUNK = 32
GDN_CHUNK = 64
CONV_WIDTH = 4
D_FF = 4 * D_MODEL
REL_BUCKETS = 32
REL_MAX_EXACT = 16
REL_MAX_DIST = 128
EPS = 1e-6
IN_SIZES = [ATT_WIDTH, ATT_WIDTH, ATT_WIDTH, 3 * GDN_WIDTH, GDN_WIDTH, GDN_HEADS, GDN_HEADS]
IN_COLS = sum(IN_SIZES)

kernel_name = "hymba_moba_gdn_sandwich_layer"


def rms_norm(x, w):
    xf = x.astype(jnp.float32)
    y = xf * lax.rsqrt(jnp.mean(xf * xf, axis=-1, keepdims=True) + EPS)
    return (y * w.astype(jnp.float32)).astype(x.dtype)


def l2_normalize(x):
    xf = x.astype(jnp.float32)
    return xf * lax.rsqrt(jnp.sum(xf * xf, axis=-1, keepdims=True) + EPS)


def rel_bucket(dist):
    d = jnp.maximum(dist, 0)
    large = REL_MAX_EXACT + (
        jnp.log(jnp.maximum(d, 1).astype(jnp.float32) / REL_MAX_EXACT)
        / math.log(REL_MAX_DIST / REL_MAX_EXACT)
        * (REL_BUCKETS - REL_MAX_EXACT)
    ).astype(jnp.int32)
    large = jnp.minimum(large, REL_BUCKETS - 1)
    return jnp.where(d < REL_MAX_EXACT, d, large)


def moba_attention(q, k, v, rel_bias):
    B, S, H, dh = q.shape
    nb = -(-S // MOBA_BLOCK)
    s_pad = nb * MOBA_BLOCK
    qh = q.transpose(0, 2, 1, 3)
    pad = ((0, 0), (0, 0), (0, s_pad - S), (0, 0))
    kh = jnp.pad(k.transpose(0, 2, 1, 3), pad)
    vh = jnp.pad(v.transpose(0, 2, 1, 3), pad)
    kb = kh.reshape(B, H, nb, MOBA_BLOCK, dh)
    vb = vh.reshape(B, H, nb, MOBA_BLOCK, dh)
    kmean = jnp.mean(kb.astype(jnp.float32), axis=3)
    rel_bias = rel_bias.astype(jnp.float32)

    gate = jnp.einsum('bhsd,bhnd->bhsn', qh.astype(jnp.float32), kmean)
    pos = jnp.arange(S)
    n_past = pos // MOBA_BLOCK
    past = jnp.arange(nb)[None, :] < n_past[:, None]
    gate = jnp.where(past, gate, -jnp.inf)
    kk = min(MOBA_TOPK, nb)
    _, sel = lax.top_k(gate, kk)
    sel_valid = jnp.arange(kk)[None, :] < jnp.minimum(n_past, MOBA_TOPK)[:, None]

    nc = S // MOBA_QCHUNK
    q_c = qh.reshape(B, H, nc, MOBA_QCHUNK, dh).transpose(2, 0, 1, 3, 4)
    sel_c = sel.reshape(B, H, nc, MOBA_QCHUNK, kk).transpose(2, 0, 1, 3, 4)
    valid_c = sel_valid.reshape(nc, MOBA_QCHUNK, kk)
    scale = HEAD_DIM ** -0.5
    off = jnp.arange(MOBA_BLOCK)
    head_idx = jnp.arange(H)[:, None, None, None]
    gather_blocks = jax.vmap(jax.vmap(lambda blocks, idx: blocks[idx]))

    def chunk_fn(args):
        c, qc, sc, vc = args
        q0 = c * MOBA_QCHUNK
        qpos = q0 + jnp.arange(MOBA_QCHUNK)
        own = q0 // MOBA_BLOCK
        k_own = lax.dynamic_slice_in_dim(kb, own, 1, axis=2)[:, :, 0]
        v_own = lax.dynamic_slice_in_dim(vb, own, 1, axis=2)[:, :, 0]
        d_own = qpos[:, None] - (own * MOBA_BLOCK + off)[None, :]
        l_own = jnp.einsum('bhqd,bhkd->bhqk', qc, k_own).astype(jnp.float32) * scale
        l_own = l_own + rel_bias[:, rel_bucket(d_own)]
        l_own = jnp.where(d_own >= 0, l_own, -jnp.inf)
        k_sel = gather_blocks(kb, sc)
        v_sel = gather_blocks(vb, sc)
        d_sel = qpos[:, None, None] - (sc[..., None] * MOBA_BLOCK + off)
        l_sel = jnp.einsum('bhqd,bhqnkd->bhqnk', qc, k_sel).astype(jnp.float32) * scale
        l_sel = l_sel + rel_bias[head_idx, rel_bucket(d_sel)]
        l_sel = jnp.where(vc[:, :, None], l_sel, -jnp.inf)
        logits = jnp.concatenate(
            [l_own, l_sel.reshape(B, H, MOBA_QCHUNK, kk * MOBA_BLOCK)], axis=-1)
        p = jax.nn.softmax(logits, axis=-1)
        p_own = p[..., :MOBA_BLOCK].astype(v.dtype)
        p_sel = p[..., MOBA_BLOCK:].reshape(B, H, MOBA_QCHUNK, kk, MOBA_BLOCK).astype(v.dtype)
        return (jnp.einsum('bhqk,bhkd->bhqd', p_own, v_own)
                + jnp.einsum('bhqnk,bhqnkd->bhqd', p_sel, v_sel))

    o = lax.map(chunk_fn, (jnp.arange(nc), q_c, sel_c, valid_c))
    return o.transpose(1, 0, 3, 2, 4).reshape(B, S, H * dh)


def causal_depthwise_conv_silu(x, w):
    C = x.shape[-1]
    y = lax.conv_general_dilated(
        x, w[:, None, :].astype(x.dtype), window_strides=(1,),
        padding=[(CONV_WIDTH - 1, 0)], dimension_numbers=('NWC', 'WIO', 'NWC'),
        feature_group_count=C)
    return jax.nn.silu(y)


def gated_deltanet(qkv, z, a, b, conv_w, A_log, dt_bias, norm_w):
    B, S, _ = qkv.shape
    H, dh, C = GDN_HEADS, HEAD_DIM, GDN_CHUNK
    nc = S // C
    qkv = causal_depthwise_conv_silu(qkv, conv_w)
    q, k, v = jnp.split(qkv, 3, axis=-1)
    q = l2_normalize(q.reshape(B, S, H, dh)) * (dh ** -0.5)
    k = l2_normalize(k.reshape(B, S, H, dh))
    v = v.reshape(B, S, H, dh).astype(jnp.float32)
    beta = jax.nn.sigmoid(b.astype(jnp.float32))
    g = -jnp.exp(A_log.astype(jnp.float32)) * jax.nn.softplus(
        a.astype(jnp.float32) + dt_bias.astype(jnp.float32))

    def to_chunks(t):
        t = jnp.moveaxis(t, 2, 1)
        return t.reshape((B, H, nc, C) + t.shape[3:])

    qc, kc, vc = to_chunks(q), to_chunks(k), to_chunks(v)
    bc = to_chunks(beta)
    g_cum = jnp.cumsum(to_chunks(g), axis=-1)
    causal = jnp.tril(jnp.ones((C, C), dtype=bool))
    strict = jnp.tril(jnp.ones((C, C), dtype=bool), k=-1)
    decay = jnp.exp(jnp.where(causal, g_cum[..., :, None] - g_cum[..., None, :], -jnp.inf))
    k_beta = kc * bc[..., None]
    m = jnp.where(strict, jnp.einsum('bhnid,bhnjd->bhnij', k_beta, kc) * decay, 0.0)
    a_mat = m + jnp.eye(C, dtype=jnp.float32)
    rhs = jnp.concatenate([vc * bc[..., None], k_beta * jnp.exp(g_cum)[..., None]], axis=-1)
    uw = lax.linalg.triangular_solve(a_mat, rhs, left_side=True, lower=True, unit_diagonal=True)
    u, w = uw[..., :dh], uw[..., dh:]
    attn_intra = jnp.einsum('bhnid,bhnjd->bhnij', qc, kc) * decay
    q_dec = qc * jnp.exp(g_cum)[..., None]
    k_dec = kc * jnp.exp(g_cum[..., -1:] - g_cum)[..., None]
    g_last = jnp.exp(g_cum[..., -1])

    def step(state, xs):
        u_c, w_c, q_c, k_c, a_c, gl = xs
        v_new = u_c - jnp.einsum('bhck,bhkv->bhcv', w_c, state)
        o = jnp.einsum('bhck,bhkv->bhcv', q_c, state) + jnp.einsum('bhcs,bhsv->bhcv', a_c, v_new)
        state = state * gl[..., None, None] + jnp.einsum('bhck,bhcv->bhkv', k_c, v_new)
        return state, o

    xs = tuple(jnp.moveaxis(t, 2, 0) for t in (u, w, q_dec, k_dec, attn_intra, g_last))
    state0 = jnp.zeros((B, H, dh, dh), dtype=jnp.float32)
    _, o = lax.scan(step, state0, xs)
    o = jnp.moveaxis(o, 0, 2).reshape(B, H, S, dh).transpose(0, 2, 1, 3)
    o = o * lax.rsqrt(jnp.mean(o * o, axis=-1, keepdims=True) + EPS) * norm_w.astype(jnp.float32)
    o = o * jax.nn.silu(z.reshape(B, S, H, dh).astype(jnp.float32))
    return o.reshape(B, S, GDN_WIDTH).astype(qkv.dtype)


def setup_inputs(seed: int = 0) -> dict:
    key = jax.random.key(seed)
    ks = jax.random.split(key, 16)
    f32 = jnp.float32
    x = jax.random.normal(ks[0], (BATCH, SEQ, D_MODEL), f32)
    w_in = jax.random.normal(ks[1], (DEPTH, D_MODEL, IN_COLS), f32) * D_MODEL ** -0.5
    w_out = jax.random.normal(ks[2], (DEPTH, MIX_WIDTH, D_MODEL), f32) * MIX_WIDTH ** -0.5
    conv_w = jax.random.normal(ks[3], (DEPTH, CONV_WIDTH, 3 * GDN_WIDTH), f32) * CONV_WIDTH ** -0.5
    A_log = jnp.log(jax.random.uniform(ks[4], (DEPTH, GDN_HEADS), f32, 1.0, 16.0))
    dt = jnp.exp(jax.random.uniform(ks[5], (DEPTH, GDN_HEADS), f32, math.log(1e-3), math.log(1e-1)))
    dt_bias = dt + jnp.log(-jnp.expm1(-dt))
    gdn_norm_w = 1.0 + 0.05 * jax.random.normal(ks[6], (DEPTH, HEAD_DIM), f32)
    rel_bias = 0.5 * jax.random.normal(ks[7], (ATT_HEADS, REL_BUCKETS), f32)
    pre_mix_norm = 1.0 + 0.05 * jax.random.normal(ks[8], (DEPTH, D_MODEL), f32)
    post_mix_norm = 1.0 + 0.05 * jax.random.normal(ks[9], (DEPTH, D_MODEL), f32)
    pre_mlp_norm = 1.0 + 0.05 * jax.random.normal(ks[10], (DEPTH, D_MODEL), f32)
    post_mlp_norm = 1.0 + 0.05 * jax.random.normal(ks[11], (DEPTH, D_MODEL), f32)
    w_up = jax.random.normal(ks[12], (DEPTH, D_MODEL, D_FF), f32) * D_MODEL ** -0.5
    w_down = jax.random.normal(ks[13], (DEPTH, D_FF, D_MODEL), f32) * D_FF ** -0.5
    return {"x": x, "w_in": w_in, "w_out": w_out, "conv_w": conv_w, "A_log": A_log,
            "dt_bias": dt_bias, "gdn_norm_w": gdn_norm_w, "rel_bias": rel_bias,
            "pre_mix_norm": pre_mix_norm, "post_mix_norm": post_mix_norm,
            "pre_mlp_norm": pre_mlp_norm, "post_mlp_norm": post_mlp_norm,
            "w_up": w_up, "w_down": w_down}


def reference(x, w_in, w_out, conv_w, A_log, dt_bias, gdn_norm_w, rel_bias,
              pre_mix_norm, post_mix_norm, pre_mlp_norm, post_mlp_norm, w_up, w_down):
    B, S, _ = x.shape
    split_points = np.cumsum(IN_SIZES)[:-1].tolist()
    for l in range(DEPTH):
        h = rms_norm(x, pre_mix_norm[l])
        proj = jnp.einsum('bsd,dc->bsc', h, w_in[l])
        att_q, att_k, att_v, gdn_qkv, gdn_z, gdn_a, gdn_b = jnp.split(proj, split_points, axis=-1)
        o_att = moba_attention(att_q.reshape(B, S, ATT_HEADS, HEAD_DIM),
                               att_k.reshape(B, S, ATT_HEADS, HEAD_DIM),
                               att_v.reshape(B, S, ATT_HEADS, HEAD_DIM), rel_bias)
        o_gdn = gated_deltanet(gdn_qkv, gdn_z, gdn_a, gdn_b, conv_w[l], A_log[l],
                               dt_bias[l], gdn_norm_w[l])
        mix = jnp.einsum('bsc,cd->bsd', jnp.concatenate([o_att, o_gdn], axis=-1), w_out[l])
        x = x + rms_norm(mix, post_mix_norm[l])
        h = rms_norm(x, pre_mlp_norm[l])
        m = jnp.einsum('bsf,fd->bsd', jnp.square(jax.nn.relu(jnp.einsum('bsd,df->bsf', h, w_up[l]))), w_down[l])
        x = x + rms_norm(m, post_mlp_norm[l])
    return x
```

```python
import functools
import math

import jax
import jax.numpy as jnp
from jax import lax
from jax.experimental import pallas as pl
from jax.experimental.pallas import tpu as pltpu

F32 = jnp.float32
BF16 = jnp.bfloat16
HI = lax.Precision.HIGHEST

D_MODEL = 1024
HEAD_DIM = 64
ATT_HEADS = 8
GDN_HEADS = 8
ATT_WIDTH = ATT_HEADS * HEAD_DIM
GDN_WIDTH = GDN_HEADS * HEAD_DIM
MOBA_BLOCK = 256
MOBA_TOPK = 3
GDN_CHUNK = 64
CONV_WIDTH = 4
D_FF = 4 * D_MODEL
REL_BUCKETS = 32
REL_MAX_EXACT = 16
REL_MAX_DIST = 128
EPS = 1e-6
NEG = -1e30

LANES = 128
VMEM_LIMIT = 56 * 1024 * 1024
ROW_TILE = 512
FF_TILE = 1024

NT = (((1,), (1,)), ((), ()))
TN = (((0,), (0,)), ((), ()))


def _bucket_lower_bounds():
    def bucket(d):
        if d < REL_MAX_EXACT:
            return d
        t = math.log(d / REL_MAX_EXACT) / math.log(REL_MAX_DIST / REL_MAX_EXACT)
        t = t * (REL_BUCKETS - REL_MAX_EXACT)
        assert d in (REL_MAX_EXACT, REL_MAX_DIST) or abs(t - round(t)) > 1e-6
        return min(REL_MAX_EXACT + int(t + 1e-9), REL_BUCKETS - 1)
    lower = []
    for b in range(REL_BUCKETS):
        d = 0
        while bucket(d) < b:
            d += 1
        lower.append(d)
    return lower


BUCKET_LOWER = _bucket_lower_bounds()


def _sigmoid(x):
    return 1.0 / (1.0 + jnp.exp(-x))


def _rms(x, w):
    return x * lax.rsqrt(jnp.mean(x * x, axis=-1, keepdims=True) + EPS) * w


def _inproj_kernel(x_ref, nw_ref, wqT_ref, wk_ref, wvT_ref, wg_ref, wz_ref, wab_ref,
                   qT_ref, k_ref, vT_ref, g_ref, z_ref, ab_ref):
    h = _rms(x_ref[...], nw_ref[...]).astype(BF16)
    qT = lax.dot_general(wqT_ref[...], h, NT, preferred_element_type=F32)
    vT = lax.dot_general(wvT_ref[...], h, NT, preferred_element_type=F32)
    for t in range(ROW_TILE // MOBA_BLOCK):
        qT_ref[0, t] = qT[:, t * MOBA_BLOCK:(t + 1) * MOBA_BLOCK]
        vT_ref[0, t] = vT[:, t * MOBA_BLOCK:(t + 1) * MOBA_BLOCK].astype(BF16)
    k_ref[...] = jnp.dot(h, wk_ref[...], preferred_element_type=F32)
    g_ref[...] = jnp.dot(h, wg_ref[...], preferred_element_type=F32)
    z_ref[...] = jnp.dot(h, wz_ref[...], preferred_element_type=F32)
    ab_ref[...] = jnp.dot(h, wab_ref[...], preferred_element_type=F32)


def _inproj(xf, nw, wqT, wk, wvT, wg, wz, wab, B, S):
    T = B * S
    nblk = S // MOBA_BLOCK
    tiles_per_seq = S // ROW_TILE
    blk_per_tile = ROW_TILE // MOBA_BLOCK
    const = lambda i: (0, 0)
    row = lambda i: (i, 0)
    tr = lambda i: (i // tiles_per_seq, i % tiles_per_seq, 0, 0)
    return pl.pallas_call(
        _inproj_kernel,
        grid=(T // ROW_TILE,),
        in_specs=[
            pl.BlockSpec((ROW_TILE, D_MODEL), row),
            pl.BlockSpec((1, D_MODEL), const),
            pl.BlockSpec(wqT.shape, const),
            pl.BlockSpec(wk.shape, const),
            pl.BlockSpec(wvT.shape, const),
            pl.BlockSpec(wg.shape, const),
            pl.BlockSpec(wz.shape, const),
            pl.BlockSpec(wab.shape, const),
        ],
        out_specs=[
            pl.BlockSpec((1, blk_per_tile, ATT_WIDTH, MOBA_BLOCK), tr),
            pl.BlockSpec((ROW_TILE, ATT_WIDTH), row),
            pl.BlockSpec((1, blk_per_tile, ATT_WIDTH, MOBA_BLOCK), tr),
            pl.BlockSpec((ROW_TILE, 3 * GDN_WIDTH), row),
            pl.BlockSpec((ROW_TILE, GDN_WIDTH), row),
            pl.BlockSpec((ROW_TILE, LANES), row),
        ],
        out_shape=[
            jax.ShapeDtypeStruct((B, nblk, ATT_WIDTH, MOBA_BLOCK), F32),
            jax.ShapeDtypeStruct((T, ATT_WIDTH), F32),
            jax.ShapeDtypeStruct((B, nblk, ATT_WIDTH, MOBA_BLOCK), BF16),
            jax.ShapeDtypeStruct((T, 3 * GDN_WIDTH), F32),
            jax.ShapeDtypeStruct((T, GDN_WIDTH), F32),
            jax.ShapeDtypeStruct((T, LANES), F32),
        ],
        compiler_params=pltpu.CompilerParams(
            dimension_semantics=("arbitrary",), vmem_limit_bytes=VMEM_LIMIT),
        name="inproj",
    )(xf, nw, wqT, wk, wvT, wg, wz, wab)


def _attn_kernel(relb_ref, qT_ref, k_ref, vT_ref, oT_ref,
                 kb_ref, km_ref, bias_ref, addm_ref, m_ref, l_ref, acc_ref, *, nblk):
    hp = pl.program_id(0)
    b = pl.program_id(1)
    i = pl.program_id(2)
    BLK = MOBA_BLOCK

    @pl.when((b == 0) & (i == 0))
    def _():
        kk = lax.broadcasted_iota(jnp.int32, (BLK, BLK), 0)
        qq = lax.broadcasted_iota(jnp.int32, (BLK, BLK), 1)
        for hh in range(2):
            h = 2 * hp + hh
            for kind in range(2):
                d = qq - kk + kind * BLK
                val = jnp.full((BLK, BLK), relb_ref[h, REL_BUCKETS - 1], F32)
                for bkt in range(REL_BUCKETS - 2, -1, -1):
                    val = jnp.where(d < BUCKET_LOWER[bkt + 1], relb_ref[h, bkt], val)
                if kind == 0:
                    val = jnp.where(d >= 0, val, NEG)
                bias_ref[hh, kind] = val

    @pl.when(i == 0)
    def _():
        lane = lax.broadcasted_iota(jnp.int32, (1, LANES), 1)
        for j in range(nblk):
            kj = k_ref[0, j * BLK:(j + 1) * BLK, :]
            kb_ref[j * BLK:(j + 1) * BLK, :] = kj.astype(BF16)
            kmj = jnp.sum(kj, axis=0, keepdims=True) * (1.0 / BLK)
            km_ref[j:j + 1, :] = jnp.where(lane < HEAD_DIM, kmj, 0.0)
            km_ref[nblk + j:nblk + j + 1, :] = jnp.where(lane >= HEAD_DIM, kmj, 0.0)

    qT = qT_ref[0, 0]
    gT = jnp.dot(km_ref[...], qT, precision=HI, preferred_element_type=F32)
    ridx = lax.broadcasted_iota(jnp.int32, (nblk, BLK), 0)
    past = ridx < i
    for hh in range(2):
        gm = jnp.where(past, gT[nblk * hh:nblk * (hh + 1)], -jnp.inf)
        cnt = jnp.zeros((nblk, BLK), F32)
        for jp in range(nblk):
            row = gm[jp:jp + 1, :]
            beats = (row > gm) | ((row == gm) & (ridx > jp))
            cnt = cnt + jnp.where(beats, 1.0, 0.0)
        addm_ref[nblk * hh:nblk * (hh + 1), :] = jnp.where(past & (cnt < MOBA_TOPK), 0.0, NEG)

    sub = lax.broadcasted_iota(jnp.int32, (LANES, BLK), 0)
    scale = HEAD_DIM ** -0.5
    for hh in range(2):
        in_head = (sub >= HEAD_DIM * hh) & (sub < HEAD_DIM * (hh + 1))
        qh = jnp.where(in_head, qT * scale, 0.0).astype(BF16)

        def scores(j, qh=qh):
            kj = kb_ref[pl.ds(pl.multiple_of(j * BLK, BLK), BLK), :]
            return jnp.dot(kj, qh, preferred_element_type=F32)

        def pv(j, p, hh=hh):
            vj = vT_ref[0, j, HEAD_DIM * hh:HEAD_DIM * (hh + 1), :]
            return jnp.dot(vj, p.astype(BF16), preferred_element_type=F32)

        def update(j, lg):
            m_old = m_ref[...]
            m_new = jnp.maximum(m_old, jnp.max(lg, axis=0, keepdims=True))
            alpha = jnp.exp(m_old - m_new)
            p = jnp.exp(lg - m_new)
            l_ref[...] = alpha * l_ref[...] + jnp.sum(p, axis=0, keepdims=True)
            acc_ref[...] = alpha * acc_ref[...] + pv(j, p)
            m_ref[...] = m_new

        lg = scores(i) + bias_ref[hh, 0]
        m0 = jnp.max(lg, axis=0, keepdims=True)
        p0 = jnp.exp(lg - m0)
        m_ref[...] = m0
        l_ref[...] = jnp.sum(p0, axis=0, keepdims=True)
        acc_ref[...] = pv(i, p0)

        @pl.when(i >= 1)
        def _(hh=hh, scores=scores, update=update):
            j = i - 1
            lg1 = scores(j) + bias_ref[hh, 1] + addm_ref[pl.ds(nblk * hh + j, 1), :]
            update(j, lg1)

        def far(j, carry, hh=hh, scores=scores, update=update):
            rowb = addm_ref[pl.ds(nblk * hh + j, 1), :] + relb_ref[2 * hp + hh, REL_BUCKETS - 1]
            update(j, scores(j) + rowb)
            return carry

        lax.fori_loop(0, jnp.maximum(i - 1, 0), far, 0)
        oT_ref[0, 0, HEAD_DIM * hh:HEAD_DIM * (hh + 1), :] = acc_ref[...] / l_ref[...]


def _attention(rel_bias, qT, k3, vT, B, S):
    nblk = S // MOBA_BLOCK
    assert BUCKET_LOWER[REL_BUCKETS - 1] <= MOBA_BLOCK + 1
    return pl.pallas_call(
        functools.partial(_attn_kernel, nblk=nblk),
        grid=(ATT_HEADS // 2, B, nblk),
        in_specs=[
            pl.BlockSpec(memory_space=pltpu.SMEM),
            pl.BlockSpec((1, 1, LANES, MOBA_BLOCK), lambda hp, b, i: (b, i, hp, 0)),
            pl.BlockSpec((1, S, LANES), lambda hp, b, i: (b, 0, hp)),
            pl.BlockSpec((1, nblk, LANES, MOBA_BLOCK), lambda hp, b, i: (b, 0, hp, 0)),
        ],
        out_specs=pl.BlockSpec((1, 1, LANES, MOBA_BLOCK), lambda hp, b, i: (b, i, hp, 0)),
        out_shape=jax.ShapeDtypeStruct((B, nblk, ATT_WIDTH, MOBA_BLOCK), F32),
        scratch_shapes=[
            pltpu.VMEM((S, LANES), BF16),
            pltpu.VMEM((2 * nblk, LANES), F32),
            pltpu.VMEM((2, 2, MOBA_BLOCK, MOBA_BLOCK), F32),
            pltpu.VMEM((2 * nblk, MOBA_BLOCK), F32),
            pltpu.VMEM((1, MOBA_BLOCK), F32),
            pltpu.VMEM((1, MOBA_BLOCK), F32),
            pltpu.VMEM((HEAD_DIM, MOBA_BLOCK), F32),
        ],
        compiler_params=pltpu.CompilerParams(
            dimension_semantics=("arbitrary", "arbitrary", "arbitrary"),
            vmem_limit_bytes=VMEM_LIMIT),
        name="moba_attn",
    )(rel_bias, qT, k3, vT)


GDN_GROUP = 4 * HEAD_DIM
CONV_TILE = 256


def _gdn_kernel(gq_ref, gk_ref, gv_ref, z_ref, ab_ref, cwq_ref, cwk_ref, cwv_ref,
                alog_ref, dtb_ref, nw_ref, out_ref,
                qn_ref, kn_ref, vc_ref, beta_ref, g_ref, *, S):
    grp = pl.program_id(1)
    C = GDN_CHUNK
    W = GDN_GROUP

    r_w = lax.broadcasted_iota(jnp.int32, (W, W), 0)
    c_w = lax.broadcasted_iota(jnp.int32, (W, W), 1)
    head_ones = jnp.where((r_w // HEAD_DIM) == (c_w // HEAD_DIM), 1.0, 0.0)
    r_e = lax.broadcasted_iota(jnp.int32, (LANES, W), 0)
    c_e = lax.broadcasted_iota(jnp.int32, (LANES, W), 1)
    head_of_col = grp * (W // HEAD_DIM) + c_e // HEAD_DIM
    sel_a = jnp.where(r_e == head_of_col, 1.0, 0.0)
    sel_b = jnp.where(r_e == GDN_HEADS + head_of_col, 1.0, 0.0)
    trow8 = lax.broadcasted_iota(jnp.int32, (8, W), 0)

    def conv_silu(taps, cw_ref):
        acc = taps[0] * cw_ref[CONV_WIDTH - 1:CONV_WIDTH, :]
        for s in range(1, CONV_WIDTH):
            acc = acc + taps[s] * cw_ref[CONV_WIDTH - 1 - s:CONV_WIDTH - s, :]
        return acc * _sigmoid(acc)

    def finish_tile(r0, yq, yk, yv):
        ssq = jnp.dot(yq * yq, head_ones, precision=HI, preferred_element_type=F32)
        ssk = jnp.dot(yk * yk, head_ones, precision=HI, preferred_element_type=F32)
        rows = pl.ds(r0, CONV_TILE)
        qn_ref[rows, :] = yq * lax.rsqrt(ssq + EPS) * (HEAD_DIM ** -0.5)
        kn_ref[rows, :] = yk * lax.rsqrt(ssk + EPS)
        vc_ref[rows, :] = yv
        abt = ab_ref[rows, :]
        a_b = jnp.dot(abt, sel_a, precision=HI, preferred_element_type=F32)
        b_b = jnp.dot(abt, sel_b, precision=HI, preferred_element_type=F32)
        beta_ref[rows, :] = _sigmoid(b_b)
        xs = a_b + dtb_ref[...]
        softplus = jnp.maximum(xs, 0.0) + jnp.log1p(jnp.exp(-jnp.abs(xs)))
        g_ref[rows, :] = -jnp.exp(alog_ref[...]) * softplus

    def taps(cur, prev8):
        out = [cur]
        for s in range(1, CONV_WIDTH):
            rolled = pltpu.roll(cur, s, 0)
            top = jnp.where(trow8 < s, pltpu.roll(prev8, s, 0), rolled[0:8])
            out.append(jnp.concatenate([top, rolled[8:]], axis=0))
        return out

    zeros8 = jnp.zeros((8, W), F32)
    finish_tile(0, *[conv_silu(taps(x_ref[0:CONV_TILE, :], zeros8), cw_ref)
                     for x_ref, cw_ref in ((gq_ref, cwq_ref), (gk_ref, cwk_ref), (gv_ref, cwv_ref))])

    def tile_body(r, carry):
        r0 = pl.multiple_of(r * CONV_TILE, CONV_TILE)
        finish_tile(r0, *[conv_silu(taps(x_ref[pl.ds(r0, CONV_TILE), :], x_ref[pl.ds(r0 - 8, 8), :]), cw_ref)
                          for x_ref, cw_ref in ((gq_ref, cwq_ref), (gk_ref, cwk_ref), (gv_ref, cwv_ref))])
        return carry

    lax.fori_loop(1, S // CONV_TILE, tile_body, 0)

    lane = lax.broadcasted_iota(jnp.int32, (C, LANES), 1)
    rowi = lax.broadcasted_iota(jnp.int32, (C, LANES), 0)
    colj = lane % HEAD_DIM
    first_head = lane < HEAD_DIM
    causal = rowi >= colj
    strict = rowi > colj
    lane2 = lax.broadcasted_iota(jnp.int32, (C, 2 * LANES), 1)
    first_head2 = (lane2 % LANES) < HEAD_DIM
    r_c = lax.broadcasted_iota(jnp.int32, (C, C), 0)
    c_c = lax.broadcasted_iota(jnp.int32, (C, C), 1)
    ltri = jnp.where(c_c <= r_c, 1.0, 0.0)
    r_l = lax.broadcasted_iota(jnp.int32, (LANES, LANES), 0)
    c_l = lax.broadcasted_iota(jnp.int32, (LANES, LANES), 1)
    same_head = (r_l // HEAD_DIM) == (c_l // HEAD_DIM)
    pair_ones = jnp.where(same_head, 1.0, 0.0)
    ones_cl = jnp.ones((C, LANES), F32)

    def stack(x, mask):
        return jnp.concatenate([jnp.where(mask, x, 0.0), jnp.where(mask, 0.0, x)], axis=0)

    def chunk_body(c, states):
        r0 = pl.multiple_of(c * C, C)
        rows = pl.ds(r0, C)
        new_states = []
        for p in range(W // LANES):
            ls = slice(LANES * p, LANES * (p + 1))
            q2 = qn_ref[rows, ls]
            k2 = kn_ref[rows, ls]
            v2 = vc_ref[rows, ls]
            b2 = beta_ref[rows, ls]
            g2 = g_ref[rows, ls]
            gc = jnp.dot(ltri, g2, precision=HI, preferred_element_type=F32)
            sg = jnp.concatenate([jnp.where(lane == 0, gc, 0.0),
                                  jnp.where(lane == HEAD_DIM, gc, 0.0)], axis=0)
            rowm = lax.dot_general(ones_cl, sg, NT, precision=HI, preferred_element_type=F32)
            decay = jnp.where(causal, jnp.exp(jnp.where(causal, gc - rowm, 0.0)), 0.0)
            kq = lax.dot_general(jnp.concatenate([k2, q2], axis=0).astype(BF16),
                                 stack(k2, first_head).astype(BF16), NT,
                                 preferred_element_type=F32)
            m2 = jnp.where(strict, kq[0:C] * b2 * decay, 0.0)
            a2 = kq[C:2 * C] * decay
            egc = jnp.exp(gc)
            kb2 = k2 * b2
            x = jnp.concatenate([v2 * b2, kb2 * egc], axis=1)
            pk = -m2
            nsteps = int(math.log2(C))
            for t in range(nsteps):
                x = x + jnp.dot(pk, stack(x, first_head2), precision=HI, preferred_element_type=F32)
                if t + 1 < nsteps:
                    pk = jnp.dot(pk, stack(pk, first_head), precision=HI, preferred_element_type=F32)
            u = x[:, :LANES]
            w = x[:, LANES:]
            g_last = gc[C - 1:C, :]
            q_dec = q2 * egc
            k_dec = k2 * jnp.exp(g_last - gc)
            st = states[p]
            wq = jnp.dot(jnp.concatenate([w, q_dec], axis=0).astype(BF16), st.astype(BF16),
                         preferred_element_type=F32)
            v_new = u - wq[0:C]
            o = wq[C:2 * C] + jnp.dot(a2.astype(BF16), stack(v_new, first_head).astype(BF16),
                                      preferred_element_type=F32)
            kv = lax.dot_general(k_dec.astype(BF16), v_new.astype(BF16), TN,
                                 preferred_element_type=F32)
            new_states.append(st * jnp.exp(g_last) + jnp.where(same_head, kv, 0.0))
            ms = jnp.dot(o * o, pair_ones, precision=HI, preferred_element_type=F32) * (1.0 / HEAD_DIM)
            zz = z_ref[rows, ls]
            out_ref[rows, ls] = o * lax.rsqrt(ms + EPS) * nw_ref[:, ls] * (zz * _sigmoid(zz))
        return tuple(new_states)

    init = tuple(jnp.zeros((LANES, LANES), F32) for _ in range(W // LANES))
    lax.fori_loop(0, S // C, chunk_body, init)


def _gdn(gqkv, z, ab, conv_w, alog_row, dtb_row, nw_row, B, S):
    T = B * S
    W = GDN_GROUP
    ngrp = GDN_WIDTH // W
    tiles = S // S
    del tiles
    col = lambda off: (lambda b, g: (b, off + g))
    ccol = lambda off: (lambda b, g: (0, off + g))
    return pl.pallas_call(
        functools.partial(_gdn_kernel, S=S),
        grid=(B, ngrp),
        in_specs=[
            pl.BlockSpec((S, W), col(0)),
            pl.BlockSpec((S, W), col(ngrp)),
            pl.BlockSpec((S, W), col(2 * ngrp)),
            pl.BlockSpec((S, W), col(0)),
            pl.BlockSpec((S, LANES), lambda b, g: (b, 0)),
            pl.BlockSpec((CONV_WIDTH, W), ccol(0)),
            pl.BlockSpec((CONV_WIDTH, W), ccol(ngrp)),
            pl.BlockSpec((CONV_WIDTH, W), ccol(2 * ngrp)),
            pl.BlockSpec((1, W), ccol(0)),
            pl.BlockSpec((1, W), ccol(0)),
            pl.BlockSpec((1, W), ccol(0)),
        ],
        out_specs=pl.BlockSpec((S, W), col(0)),
        out_shape=jax.ShapeDtypeStruct((T, GDN_WIDTH), F32),
        scratch_shapes=[pltpu.VMEM((S, W), F32) for _ in range(5)],
        compiler_params=pltpu.CompilerParams(
            dimension_semantics=("arbitrary", "arbitrary"), vmem_limit_bytes=VMEM_LIMIT),
        name="gdn",
    )(gqkv, gqkv, gqkv, z, ab, conv_w, conv_w, conv_w, alog_row, dtb_row, nw_row)


def _out_mlp_kernel(x_ref, oT_ref, og_ref, woa_ref, wog_ref, pmn_ref, pre_ref, post_ref,
                    wup_ref, wdn_ref, out_ref):
    oT = jnp.concatenate([oT_ref[0, t] for t in range(ROW_TILE // MOBA_BLOCK)], axis=1)
    o_att = oT.T.astype(BF16)
    mix = jnp.dot(o_att, woa_ref[...], preferred_element_type=F32)
    mix = mix + jnp.dot(og_ref[...].astype(BF16), wog_ref[...], preferred_element_type=F32)
    x1 = x_ref[...] + _rms(mix, pmn_ref[...])
    h = _rms(x1, pre_ref[...]).astype(BF16)
    acc = jnp.zeros((ROW_TILE, D_MODEL), F32)
    for c in range(D_FF // FF_TILE):
        up = jnp.dot(h, wup_ref[:, c * FF_TILE:(c + 1) * FF_TILE], preferred_element_type=F32)
        act = jnp.square(jnp.maximum(up, 0.0)).astype(BF16)
        acc = acc + jnp.dot(act, wdn_ref[c * FF_TILE:(c + 1) * FF_TILE, :], preferred_element_type=F32)
    out_ref[...] = x1 + _rms(acc, post_ref[...])


def _out_mlp(xf, oT, og, woa, wog, pmn, pre, post, wup, wdn, B, S):
    T = B * S
    tiles_per_seq = S // ROW_TILE
    blk_per_tile = ROW_TILE // MOBA_BLOCK
    const = lambda i: (0, 0)
    row = lambda i: (i, 0)
    tr = lambda i: (i // tiles_per_seq, i % tiles_per_seq, 0, 0)
    single = dict(pipeline_mode=pl.Buffered(1))
    return pl.pallas_call(
        _out_mlp_kernel,
        grid=(T // ROW_TILE,),
        in_specs=[
            pl.BlockSpec((ROW_TILE, D_MODEL), row),
            pl.BlockSpec((1, blk_per_tile, ATT_WIDTH, MOBA_BLOCK), tr),
            pl.BlockSpec((ROW_TILE, GDN_WIDTH), row),
            pl.BlockSpec(woa.shape, const, **single),
            pl.BlockSpec(wog.shape, const, **single),
            pl.BlockSpec((1, D_MODEL), const),
            pl.BlockSpec((1, D_MODEL), const),
            pl.BlockSpec((1, D_MODEL), const),
            pl.BlockSpec(wup.shape, const, **single),
            pl.BlockSpec(wdn.shape, const, **single),
        ],
        out_specs=pl.BlockSpec((ROW_TILE, D_MODEL), row),
        out_shape=jax.ShapeDtypeStruct((T, D_MODEL), F32),
        compiler_params=pltpu.CompilerParams(
            dimension_semantics=("arbitrary",), vmem_limit_bytes=VMEM_LIMIT),
        name="out_mlp",
    )(xf, oT, og, woa, wog, pmn, pre, post, wup, wdn)


def kernel(x, w_in, w_out, conv_w, A_log, dt_bias, gdn_norm_w, rel_bias, pre_mix_norm,
           post_mix_norm, pre_mlp_norm, post_mlp_norm, w_up, w_down):
    B, S, D = x.shape
    assert D == D_MODEL and S % ROW_TILE == 0 and S % MOBA_BLOCK == 0
    T = B * S
    depth = w_in.shape[0]
    xf = x.reshape(T, D)
    o0, o1, o2, o3, o4 = 0, ATT_WIDTH, 2 * ATT_WIDTH, 3 * ATT_WIDTH, 3 * ATT_WIDTH + 3 * GDN_WIDTH
    o5 = o4 + GDN_WIDTH
    for l in range(depth):
        wi = w_in[l]
        wqT = wi[:, o0:o1].T.astype(BF16)
        wk = wi[:, o1:o2].astype(BF16)
        wvT = wi[:, o2:o3].T.astype(BF16)
        wg = wi[:, o3:o4].astype(BF16)
        wz = wi[:, o4:o5].astype(BF16)
        wab = jnp.pad(wi[:, o5:], ((0, 0), (0, LANES - 2 * GDN_HEADS))).astype(BF16)
        qT, k, vT, gqkv, z, ab = _inproj(xf, pre_mix_norm[l][None, :], wqT, wk, wvT, wg, wz, wab, B, S)
        oT = _attention(rel_bias.astype(F32), qT, k.reshape(B, S, ATT_WIDTH), vT, B, S)
        og = _gdn(gqkv, z, ab, conv_w[l],
                  jnp.repeat(A_log[l], HEAD_DIM)[None, :], jnp.repeat(dt_bias[l], HEAD_DIM)[None, :],
                  jnp.tile(gdn_norm_w[l], GDN_HEADS)[None, :], B, S)
        wo = w_out[l].astype(BF16)
        xf = _out_mlp(xf, oT, og, wo[:ATT_WIDTH], wo[ATT_WIDTH:], post_mix_norm[l][None, :],
                      pre_mlp_norm[l][None, :], post_mlp_norm[l][None, :],
                      w_up[l].astype(BF16), w_down[l].astype(BF16), B, S)
    return xf.reshape(B, S, D)
```

```python
import functools
import math

import jax
import jax.numpy as jnp
from jax import lax
from jax.experimental import pallas as pl
from jax.experimental.pallas import tpu as pltpu

F32 = jnp.float32
BF16 = jnp.bfloat16
HI = lax.Precision.HIGHEST

D_MODEL = 1024
HEAD_DIM = 64
ATT_HEADS = 8
GDN_HEADS = 8
ATT_WIDTH = ATT_HEADS * HEAD_DIM
GDN_WIDTH = GDN_HEADS * HEAD_DIM
MOBA_BLOCK = 256
MOBA_TOPK = 3
GDN_CHUNK = 64
CONV_WIDTH = 4
D_FF = 4 * D_MODEL
REL_BUCKETS = 32
REL_MAX_EXACT = 16
REL_MAX_DIST = 128
EPS = 1e-6
NEG = -1e30

LANES = 128
SUBLANES = 8
VMEM_LIMIT = 56 * 1024 * 1024
ROW_TILE = 512
FF_TILE = 1024

NT = (((1,), (1,)), ((), ()))
TN = (((0,), (0,)), ((), ()))


def _bucket_lower_bounds():
    def bucket(d):
        if d < REL_MAX_EXACT:
            return d
        t = math.log(d / REL_MAX_EXACT) / math.log(REL_MAX_DIST / REL_MAX_EXACT)
        t = t * (REL_BUCKETS - REL_MAX_EXACT)
        assert d in (REL_MAX_EXACT, REL_MAX_DIST) or abs(t - round(t)) > 1e-6
        return min(REL_MAX_EXACT + int(t + 1e-9), REL_BUCKETS - 1)
    lower = []
    for b in range(REL_BUCKETS):
        d = 0
        while bucket(d) < b:
            d += 1
        lower.append(d)
    return lower


BUCKET_LOWER = _bucket_lower_bounds()


def _sigmoid(x):
    return 1.0 / (1.0 + jnp.exp(-x))


def _rms(x, w):
    return x * lax.rsqrt(jnp.mean(x * x, axis=-1, keepdims=True) + EPS) * w


def _split_bf16(x, parts):
    out = []
    for _ in range(parts):
        h = x.astype(BF16)
        out.append(h)
        x = x - h.astype(F32)
    return out


def _dot_split_lhs(x, c, parts):
    acc = None
    for h in _split_bf16(x, parts):
        d = jnp.dot(h, c, preferred_element_type=F32)
        acc = d if acc is None else acc + d
    return acc


def _dot_split_rhs(c, x, parts):
    acc = None
    for h in _split_bf16(x, parts):
        d = jnp.dot(c, h, preferred_element_type=F32)
        acc = d if acc is None else acc + d
    return acc


def _inproj_kernel(x_ref, nw_ref, wqT_ref, wk_ref, wvT_ref, wg_ref, wz_ref, wab_ref,
                   qT_ref, k_ref, vT_ref, g_ref, z_ref, ab_ref):
    h = _rms(x_ref[...], nw_ref[...]).astype(BF16)
    qT = lax.dot_general(wqT_ref[...], h, NT, preferred_element_type=F32)
    vT = lax.dot_general(wvT_ref[...], h, NT, preferred_element_type=F32)
    for t in range(ROW_TILE // MOBA_BLOCK):
        qT_ref[0, t] = qT[:, t * MOBA_BLOCK:(t + 1) * MOBA_BLOCK]
        vT_ref[0, t] = vT[:, t * MOBA_BLOCK:(t + 1) * MOBA_BLOCK].astype(BF16)
    k_ref[...] = jnp.dot(h, wk_ref[...], preferred_element_type=F32)
    g_ref[...] = jnp.dot(h, wg_ref[...], preferred_element_type=F32)
    z_ref[...] = jnp.dot(h, wz_ref[...], preferred_element_type=F32)
    ab_ref[...] = jnp.dot(h, wab_ref[...], preferred_element_type=F32)


def _inproj(xf, nw, wqT, wk, wvT, wg, wz, wab, B, S):
    T = B * S
    nblk = S // MOBA_BLOCK
    tiles_per_seq = S // ROW_TILE
    blk_per_tile = ROW_TILE // MOBA_BLOCK
    const = lambda i: (0, 0)
    row = lambda i: (i, 0)
    tr = lambda i: (i // tiles_per_seq, i % tiles_per_seq, 0, 0)
    return pl.pallas_call(
        _inproj_kernel,
        grid=(T // ROW_TILE,),
        in_specs=[
            pl.BlockSpec((ROW_TILE, D_MODEL), row),
            pl.BlockSpec((1, D_MODEL), const),
            pl.BlockSpec(wqT.shape, const),
            pl.BlockSpec(wk.shape, const),
            pl.BlockSpec(wvT.shape, const),
            pl.BlockSpec(wg.shape, const),
            pl.BlockSpec(wz.shape, const),
            pl.BlockSpec(wab.shape, const),
        ],
        out_specs=[
            pl.BlockSpec((1, blk_per_tile, ATT_WIDTH, MOBA_BLOCK), tr),
            pl.BlockSpec((ROW_TILE, ATT_WIDTH), row),
            pl.BlockSpec((1, blk_per_tile, ATT_WIDTH, MOBA_BLOCK), tr),
            pl.BlockSpec((ROW_TILE, 3 * GDN_WIDTH), row),
            pl.BlockSpec((ROW_TILE, GDN_WIDTH), row),
            pl.BlockSpec((ROW_TILE, LANES), row),
        ],
        out_shape=[
            jax.ShapeDtypeStruct((B, nblk, ATT_WIDTH, MOBA_BLOCK), F32),
            jax.ShapeDtypeStruct((T, ATT_WIDTH), F32),
            jax.ShapeDtypeStruct((B, nblk, ATT_WIDTH, MOBA_BLOCK), BF16),
            jax.ShapeDtypeStruct((T, 3 * GDN_WIDTH), F32),
            jax.ShapeDtypeStruct((T, GDN_WIDTH), F32),
            jax.ShapeDtypeStruct((T, LANES), F32),
        ],
        compiler_params=pltpu.CompilerParams(
            dimension_semantics=("arbitrary",), vmem_limit_bytes=VMEM_LIMIT),
        name="inproj",
    )(xf, nw, wqT, wk, wvT, wg, wz, wab)


def _attn_kernel(relb_ref, qlo_ref, qhi_ref, k_ref, vT_ref, oT_ref,
                 kb_ref, km_ref, bias_ref, addm_ref, qh_ref, lg_ref, *, nblk):
    hp = pl.program_id(0)
    b = pl.program_id(1)
    t = pl.program_id(2)
    BLK = MOBA_BLOCK
    ntile = nblk + 1

    @pl.when((b == 0) & (t == 0))
    def _():
        kk = lax.broadcasted_iota(jnp.int32, (BLK, BLK), 0)
        qq = lax.broadcasted_iota(jnp.int32, (BLK, BLK), 1)
        for hh in range(2):
            h = 2 * hp + hh
            for kind in range(2):
                d = qq - kk + kind * BLK
                val = jnp.full((BLK, BLK), relb_ref[h, REL_BUCKETS - 1], F32)
                for bkt in range(REL_BUCKETS - 2, -1, -1):
                    val = jnp.where(d < BUCKET_LOWER[bkt + 1], relb_ref[h, bkt], val)
                if kind == 0:
                    val = jnp.where(d >= 0, val, NEG)
                bias_ref[hh, kind] = val
            bias_ref[hh, 2] = jnp.full((BLK, BLK), relb_ref[h, REL_BUCKETS - 1], F32)

    @pl.when(t == 0)
    def _():
        lane = lax.broadcasted_iota(jnp.int32, (1, LANES), 1)
        for j in range(nblk):
            kj = k_ref[0, j * BLK:(j + 1) * BLK, :]
            kb_ref[j * BLK:(j + 1) * BLK, :] = kj.astype(BF16)
            kmj = jnp.sum(kj, axis=0, keepdims=True) * (1.0 / BLK)
            km_ref[j:j + 1, :] = jnp.where(lane < HEAD_DIM, kmj, 0.0)
            km_ref[nblk + j:nblk + j + 1, :] = jnp.where(lane >= HEAD_DIM, kmj, 0.0)

    ridx = lax.broadcasted_iota(jnp.int32, (nblk, BLK), 0)
    sub = lax.broadcasted_iota(jnp.int32, (LANES, BLK), 0)
    scale = HEAD_DIM ** -0.5
    q_blocks = (t, nblk - 1 - t)
    for s, (q_ref, qi) in enumerate(zip((qlo_ref, qhi_ref), q_blocks)):
        qT = q_ref[0, 0]
        gT = jnp.dot(km_ref[...], qT, precision=HI, preferred_element_type=F32)
        past = ridx < qi
        for hh in range(2):
            gm = jnp.where(past, gT[nblk * hh:nblk * (hh + 1)], -jnp.inf)
            cnt = jnp.zeros((nblk, BLK), F32)
            for jp in range(nblk):
                row = gm[jp:jp + 1, :]
                beats = (row > gm) | ((row == gm) & (ridx > jp))
                cnt = cnt + jnp.where(beats, 1.0, 0.0)
            visible = (past & (cnt < MOBA_TOPK)) | (ridx == qi)
            addm_ref[s, nblk * hh:nblk * (hh + 1), :] = jnp.where(visible, 0.0, NEG)
            in_head = (sub >= HEAD_DIM * hh) & (sub < HEAD_DIM * (hh + 1))
            qh_ref[s, hh] = jnp.where(in_head, qT * scale, 0.0).astype(BF16)

    is_lo = [n <= t for n in range(ntile)]
    slot = [jnp.where(is_lo[n], 0, 1) for n in range(ntile)]
    kblk = [jnp.where(is_lo[n], n, n - t - 1) for n in range(ntile)]
    kind = [jnp.minimum(jnp.where(is_lo[n], t, nblk - 1 - t) - kblk[n], 2) for n in range(ntile)]

    cmax = [[None] * ntile for _ in range(2)]
    for n in range(ntile):
        kj = kb_ref[pl.ds(pl.multiple_of(kblk[n] * BLK, BLK), BLK), :]
        for hh in range(2):
            lg = jnp.dot(kj, qh_ref[slot[n], hh], preferred_element_type=F32)
            lg = lg + bias_ref[hh, kind[n]] + addm_ref[slot[n], pl.ds(nblk * hh + kblk[n], 1), :]
            lg_ref[hh, n] = lg
            cmax[hh][n] = jnp.max(lg, axis=0, keepdims=True)

    for hh in range(2):
        m_lo = cmax[hh][0]
        m_hi = cmax[hh][ntile - 1]
        for n in range(1, ntile - 1):
            m_lo = jnp.maximum(m_lo, jnp.where(is_lo[n], cmax[hh][n], -jnp.inf))
            m_hi = jnp.maximum(m_hi, jnp.where(is_lo[n], -jnp.inf, cmax[hh][n]))
        l_lo = jnp.zeros((1, BLK), F32)
        l_hi = jnp.zeros((1, BLK), F32)
        acc_lo = jnp.zeros((HEAD_DIM, BLK), F32)
        acc_hi = jnp.zeros((HEAD_DIM, BLK), F32)
        for n in range(ntile):
            p = jnp.exp(lg_ref[hh, n] - jnp.where(is_lo[n], m_lo, m_hi))
            cs = jnp.sum(p, axis=0, keepdims=True)
            vj = vT_ref[0, kblk[n], HEAD_DIM * hh:HEAD_DIM * (hh + 1), :]
            pvn = jnp.dot(vj, p.astype(BF16), preferred_element_type=F32)
            l_lo = l_lo + jnp.where(is_lo[n], cs, 0.0)
            l_hi = l_hi + jnp.where(is_lo[n], 0.0, cs)
            acc_lo = acc_lo + jnp.where(is_lo[n], pvn, 0.0)
            acc_hi = acc_hi + jnp.where(is_lo[n], 0.0, pvn)
        oT_ref[0, 0, HEAD_DIM * hh:HEAD_DIM * (hh + 1), :] = acc_lo / l_lo
        oT_ref[0, 1, HEAD_DIM * hh:HEAD_DIM * (hh + 1), :] = acc_hi / l_hi


def _paired_pos(i, nblk):
    return jnp.where(i < nblk // 2, 2 * i, 2 * (nblk - 1 - i) + 1)


def _attention(rel_bias, qT, k3, vT, B, S):
    nblk = S // MOBA_BLOCK
    assert nblk % 2 == 0
    assert BUCKET_LOWER[REL_BUCKETS - 1] <= MOBA_BLOCK + 1
    return pl.pallas_call(
        functools.partial(_attn_kernel, nblk=nblk),
        grid=(ATT_HEADS // 2, B, nblk // 2),
        in_specs=[
            pl.BlockSpec(memory_space=pltpu.SMEM),
            pl.BlockSpec((1, 1, LANES, MOBA_BLOCK), lambda hp, b, t: (b, t, hp, 0)),
            pl.BlockSpec((1, 1, LANES, MOBA_BLOCK), lambda hp, b, t: (b, nblk - 1 - t, hp, 0)),
            pl.BlockSpec((1, S, LANES), lambda hp, b, t: (b, 0, hp)),
            pl.BlockSpec((1, nblk, LANES, MOBA_BLOCK), lambda hp, b, t: (b, 0, hp, 0)),
        ],
        out_specs=pl.BlockSpec((1, 2, LANES, MOBA_BLOCK), lambda hp, b, t: (b, t, hp, 0)),
        out_shape=jax.ShapeDtypeStruct((B, nblk, ATT_WIDTH, MOBA_BLOCK), F32),
        scratch_shapes=[
            pltpu.VMEM((S, LANES), BF16),
            pltpu.VMEM((2 * nblk, LANES), F32),
            pltpu.VMEM((2, 3, MOBA_BLOCK, MOBA_BLOCK), F32),
            pltpu.VMEM((2, 2 * nblk, MOBA_BLOCK), F32),
            pltpu.VMEM((2, 2, LANES, MOBA_BLOCK), BF16),
            pltpu.VMEM((2, nblk + 1, MOBA_BLOCK, MOBA_BLOCK), F32),
        ],
        compiler_params=pltpu.CompilerParams(
            dimension_semantics=("arbitrary", "arbitrary", "arbitrary"),
            vmem_limit_bytes=VMEM_LIMIT),
        name="moba_attn",
    )(rel_bias, qT, qT, k3, vT)


GDN_GROUP = 4 * HEAD_DIM
GDN_TILE = 256


def _gdn_kernel(gq_ref, gk_ref, gv_ref, z_ref, ab_ref, cwq_ref, cwk_ref, cwv_ref,
                alog_ref, dtb_ref, nw_ref, out_ref,
                u_ref, wq_ref, a_ref, kd_ref, gl_ref, *, S):
    grp = pl.program_id(1)
    C = GDN_CHUNK
    W = GDN_GROUP
    TILE = GDN_TILE
    npair = W // LANES

    r_w = lax.broadcasted_iota(jnp.int32, (W, W), 0)
    c_w = lax.broadcasted_iota(jnp.int32, (W, W), 1)
    head_ones = jnp.where((r_w // HEAD_DIM) == (c_w // HEAD_DIM), 1.0, 0.0).astype(BF16)
    ltri_bd = jnp.where(((r_w // C) == (c_w // C)) & (c_w <= r_w), 1.0, 0.0).astype(BF16)
    r_e = lax.broadcasted_iota(jnp.int32, (LANES, W), 0)
    c_e = lax.broadcasted_iota(jnp.int32, (LANES, W), 1)
    head_of_col = grp * (W // HEAD_DIM) + c_e // HEAD_DIM
    sel_a = jnp.where(r_e == head_of_col, 1.0, 0.0).astype(BF16)
    sel_b = jnp.where(r_e == GDN_HEADS + head_of_col, 1.0, 0.0).astype(BF16)
    trow8 = lax.broadcasted_iota(jnp.int32, (SUBLANES, W), 0)
    tok = lax.broadcasted_iota(jnp.int32, (TILE, W), 0) % C
    col = lax.broadcasted_iota(jnp.int32, (TILE, W), 1) % HEAD_DIM
    causal_t = tok >= col
    strict_t = tok > col

    lane = lax.broadcasted_iota(jnp.int32, (C, LANES), 1)
    rowi = lax.broadcasted_iota(jnp.int32, (C, LANES), 0)
    first_head = lane < HEAD_DIM
    strict = rowi > (lane % HEAD_DIM)
    eye2 = jnp.where(rowi == (lane % HEAD_DIM), 1.0, 0.0)
    lane2 = lax.broadcasted_iota(jnp.int32, (C, 2 * LANES), 1)
    first_head2 = (lane2 % LANES) < HEAD_DIM
    r_l = lax.broadcasted_iota(jnp.int32, (LANES, LANES), 0)
    c_l = lax.broadcasted_iota(jnp.int32, (LANES, LANES), 1)
    same_head = (r_l // HEAD_DIM) == (c_l // HEAD_DIM)
    pair_ones = jnp.where(same_head, 1.0, 0.0).astype(BF16)

    def stack(x, mask):
        return jnp.concatenate([jnp.where(mask, x, 0.0), jnp.where(mask, 0.0, x)], axis=0)

    def conv_silu(x_ref, cw_ref, r, r0):
        cur = x_ref[pl.ds(r0, TILE), :]
        prev8 = x_ref[pl.ds(pl.multiple_of(jnp.maximum(r0 - SUBLANES, 0), SUBLANES), SUBLANES), :]
        prev8 = jnp.where(r > 0, prev8, 0.0)
        acc = cur * cw_ref[CONV_WIDTH - 1:CONV_WIDTH, :]
        for s in range(1, CONV_WIDTH):
            rolled = pltpu.roll(cur, s, 0)
            top = jnp.where(trow8 < s, pltpu.roll(prev8, s, 0), rolled[0:SUBLANES])
            tap = jnp.concatenate([top, rolled[SUBLANES:]], axis=0)
            acc = acc + tap * cw_ref[CONV_WIDTH - 1 - s:CONV_WIDTH - s, :]
        return acc * _sigmoid(acc)

    def solve(m2, rhs):
        dot = functools.partial(jnp.dot, preferred_element_type=F32)
        ns = [-m for m in m2]
        sps = [stack(n, first_head).astype(BF16) for n in ns]
        ps = ns
        tinvs = [eye2 + n for n in ns]
        for _ in range(int(math.log2(C)) - 1):
            ps = [dot(p.astype(BF16), sp) for p, sp in zip(ps, sps)]
            sps = [stack(p, first_head).astype(BF16) for p in ps]
            tinvs = [t + dot(t.astype(BF16), sp) for t, sp in zip(tinvs, sps)]
        tbs = [t.astype(BF16) for t in tinvs]
        x0s = [dot(tb, stack(x, first_head2).astype(BF16)) for tb, x in zip(tbs, rhs)]
        nsplit = [_split_bf16(n, 2) for n in ns]
        xsplit = [_split_bf16(stack(x0, first_head2), 2) for x0 in x0s]
        nxs = [dot(nh, xh) for (nh, _), (xh, _) in zip(nsplit, xsplit)]
        nxs = [a + dot(nl, xh) for a, (_, nl), (xh, _) in zip(nxs, nsplit, xsplit)]
        nxs = [a + dot(nh, xl) for a, (nh, _), (_, xl) in zip(nxs, nsplit, xsplit)]
        resid = [x - x0 + nx for x, x0, nx in zip(rhs, x0s, nxs)]
        return [x0 + dot(tb, stack(rr, first_head2).astype(BF16))
                for x0, tb, rr in zip(x0s, tbs, resid)]

    def tile_body(r, carry):
        r0 = pl.multiple_of(r * TILE, TILE)
        yq = conv_silu(gq_ref, cwq_ref, r, r0)
        yk = conv_silu(gk_ref, cwk_ref, r, r0)
        yv = conv_silu(gv_ref, cwv_ref, r, r0)
        ssq = _dot_split_lhs(yq * yq, head_ones, 2)
        ssk = _dot_split_lhs(yk * yk, head_ones, 2)
        qn = yq * lax.rsqrt(ssq + EPS) * (HEAD_DIM ** -0.5)
        kn = yk * lax.rsqrt(ssk + EPS)
        abt = ab_ref[pl.ds(r0, TILE), :]
        abs3 = _split_bf16(abt, 3)
        a_b = sum(jnp.dot(h, sel_a, preferred_element_type=F32) for h in abs3)
        b_b = sum(jnp.dot(h, sel_b, preferred_element_type=F32) for h in abs3)
        beta = _sigmoid(b_b)
        xs = a_b + dtb_ref[...]
        g = -jnp.exp(alog_ref[...]) * (jnp.maximum(xs, 0.0) + jnp.log1p(jnp.exp(-jnp.abs(xs))))
        gcd = _dot_split_rhs(ltri_bd, jnp.concatenate([g, jnp.where(strict_t, g, 0.0)], axis=1), 3)
        gc = gcd[:, :W]
        decay = jnp.where(causal_t, jnp.exp(jnp.where(causal_t, gcd[:, W:], 0.0)), 0.0)
        egc = jnp.exp(gc)
        kb = kn * beta
        rv = yv * beta
        rk = kb * egc
        qd = qn * egc
        for cc in range(TILE // C):
            rs = slice(cc * C, (cc + 1) * C)
            ci = r * (TILE // C) + cc
            g_last = gc[(cc + 1) * C - 1:(cc + 1) * C, :]
            kd_ref[pl.ds(pl.multiple_of(r0 + cc * C, C), C), :] = (
                kn[rs] * jnp.exp(g_last - gc[rs])).astype(BF16)
            gl_ref[pl.ds(pl.multiple_of(ci * SUBLANES, SUBLANES), SUBLANES), :] = (
                jnp.broadcast_to(jnp.exp(g_last), (SUBLANES, W)))
        units = [(slice(cc * C, (cc + 1) * C), slice(LANES * p, LANES * (p + 1)), cc)
                 for cc in range(TILE // C) for p in range(npair)]
        kqs = [lax.dot_general(jnp.concatenate([kn[rs, ls], qn[rs, ls]], axis=0).astype(BF16),
                               stack(kn[rs, ls], first_head).astype(BF16), NT,
                               preferred_element_type=F32) for rs, ls, _ in units]
        m2s = [jnp.where(strict, kq[0:C] * beta[rs, ls] * decay[rs, ls], 0.0)
               for kq, (rs, ls, _) in zip(kqs, units)]
        xs_ = solve(m2s, [jnp.concatenate([rv[rs, ls], rk[rs, ls]], axis=1) for rs, ls, _ in units])
        for x, kq, (rs, ls, cc) in zip(xs_, kqs, units):
            ci = r * (TILE // C) + cc
            crow = pl.ds(pl.multiple_of(r0 + cc * C, C), C)
            u_ref[crow, ls] = x[:, :LANES]
            wq_ref[ci, 0:C, ls] = x[:, LANES:].astype(BF16)
            wq_ref[ci, C:2 * C, ls] = qd[rs, ls].astype(BF16)
            a_ref[crow, ls] = (kq[C:2 * C] * decay[rs, ls]).astype(BF16)
        return carry

    lax.fori_loop(0, S // TILE, tile_body, 0)

    def chunk_body(c, states):
        r0 = pl.multiple_of(c * C, C)
        rows = pl.ds(r0, C)
        lss = [slice(LANES * p, LANES * (p + 1)) for p in range(npair)]
        wqs = [jnp.dot(wq_ref[c, :, ls], st.astype(BF16), preferred_element_type=F32)
               for ls, st in zip(lss, states)]
        v_news = [u_ref[rows, ls] - wq[0:C] for ls, wq in zip(lss, wqs)]
        kvs = [lax.dot_general(kd_ref[rows, ls], v.astype(BF16), TN, preferred_element_type=F32)
               for ls, v in zip(lss, v_news)]
        os_ = [wq[C:2 * C] + jnp.dot(a_ref[rows, ls], stack(v, first_head).astype(BF16),
                                     preferred_element_type=F32)
               for ls, wq, v in zip(lss, wqs, v_news)]
        gl_row = pl.ds(pl.multiple_of(c * SUBLANES, SUBLANES), 1)
        new_states = [st * gl_ref[gl_row, ls] + jnp.where(same_head, kv, 0.0)
                      for ls, st, kv in zip(lss, states, kvs)]
        for ls, o in zip(lss, os_):
            ms = _dot_split_lhs(o * o, pair_ones, 2) * (1.0 / HEAD_DIM)
            zz = z_ref[rows, ls]
            out_ref[rows, ls] = o * lax.rsqrt(ms + EPS) * nw_ref[:, ls] * (zz * _sigmoid(zz))
        return tuple(new_states)

    init = tuple(jnp.zeros((LANES, LANES), F32) for _ in range(npair))
    lax.fori_loop(0, S // C, chunk_body, init)


def _gdn(gqkv, z, ab, conv_w, alog_row, dtb_row, nw_row, B, S):
    T = B * S
    W = GDN_GROUP
    ngrp = GDN_WIDTH // W
    nchunk = S // GDN_CHUNK
    col = lambda off: (lambda b, g: (b, off + g))
    ccol = lambda off: (lambda b, g: (0, off + g))
    return pl.pallas_call(
        functools.partial(_gdn_kernel, S=S),
        grid=(B, ngrp),
        in_specs=[
            pl.BlockSpec((S, W), col(0)),
            pl.BlockSpec((S, W), col(ngrp)),
            pl.BlockSpec((S, W), col(2 * ngrp)),
            pl.BlockSpec((S, W), col(0)),
            pl.BlockSpec((S, LANES), lambda b, g: (b, 0)),
            pl.BlockSpec((CONV_WIDTH, W), ccol(0)),
            pl.BlockSpec((CONV_WIDTH, W), ccol(ngrp)),
            pl.BlockSpec((CONV_WIDTH, W), ccol(2 * ngrp)),
            pl.BlockSpec((1, W), ccol(0)),
            pl.BlockSpec((1, W), ccol(0)),
            pl.BlockSpec((1, W), ccol(0)),
        ],
        out_specs=pl.BlockSpec((S, W), col(0)),
        out_shape=jax.ShapeDtypeStruct((T, GDN_WIDTH), F32),
        scratch_shapes=[
            pltpu.VMEM((S, W), F32),
            pltpu.VMEM((nchunk, 2 * GDN_CHUNK, W), BF16),
            pltpu.VMEM((S, W), BF16),
            pltpu.VMEM((S, W), BF16),
            pltpu.VMEM((nchunk * SUBLANES, W), F32),
        ],
        compiler_params=pltpu.CompilerParams(
            dimension_semantics=("arbitrary", "arbitrary"), vmem_limit_bytes=VMEM_LIMIT),
        name="gdn",
    )(gqkv, gqkv, gqkv, z, ab, conv_w, conv_w, conv_w, alog_row, dtb_row, nw_row)


def _out_mlp_kernel(x_ref, oTa_ref, oTb_ref, og_ref, woa_ref, wog_ref, pmn_ref, pre_ref, post_ref,
                    wup_ref, wdn_ref, out_ref):
    oT = jnp.concatenate([oTa_ref[0, 0], oTb_ref[0, 0]], axis=1)
    o_att = oT.T.astype(BF16)
    mix = jnp.dot(o_att, woa_ref[...], preferred_element_type=F32)
    mix = mix + jnp.dot(og_ref[...].astype(BF16), wog_ref[...], preferred_element_type=F32)
    x1 = x_ref[...] + _rms(mix, pmn_ref[...])
    h = _rms(x1, pre_ref[...]).astype(BF16)
    acc = jnp.zeros((ROW_TILE, D_MODEL), F32)
    for c in range(D_FF // FF_TILE):
        up = jnp.dot(h, wup_ref[:, c * FF_TILE:(c + 1) * FF_TILE], preferred_element_type=F32)
        act = jnp.square(jnp.maximum(up, 0.0)).astype(BF16)
        acc = acc + jnp.dot(act, wdn_ref[c * FF_TILE:(c + 1) * FF_TILE, :], preferred_element_type=F32)
    out_ref[...] = x1 + _rms(acc, post_ref[...])


def _out_mlp(xf, oT, og, woa, wog, pmn, pre, post, wup, wdn, B, S):
    T = B * S
    nblk = S // MOBA_BLOCK
    tiles_per_seq = S // ROW_TILE
    assert ROW_TILE == 2 * MOBA_BLOCK
    const = lambda i: (0, 0)
    row = lambda i: (i, 0)

    def att_block(which):
        def index(i):
            blk = 2 * (i % tiles_per_seq) + which
            return (i // tiles_per_seq, _paired_pos(blk, nblk), 0, 0)
        return index

    single = dict(pipeline_mode=pl.Buffered(1))
    return pl.pallas_call(
        _out_mlp_kernel,
        grid=(T // ROW_TILE,),
        in_specs=[
            pl.BlockSpec((ROW_TILE, D_MODEL), row),
            pl.BlockSpec((1, 1, ATT_WIDTH, MOBA_BLOCK), att_block(0)),
            pl.BlockSpec((1, 1, ATT_WIDTH, MOBA_BLOCK), att_block(1)),
            pl.BlockSpec((ROW_TILE, GDN_WIDTH), row),
            pl.BlockSpec(woa.shape, const, **single),
            pl.BlockSpec(wog.shape, const, **single),
            pl.BlockSpec((1, D_MODEL), const),
            pl.BlockSpec((1, D_MODEL), const),
            pl.BlockSpec((1, D_MODEL), const),
            pl.BlockSpec(wup.shape, const, **single),
            pl.BlockSpec(wdn.shape, const, **single),
        ],
        out_specs=pl.BlockSpec((ROW_TILE, D_MODEL), row),
        out_shape=jax.ShapeDtypeStruct((T, D_MODEL), F32),
        compiler_params=pltpu.CompilerParams(
            dimension_semantics=("arbitrary",), vmem_limit_bytes=VMEM_LIMIT),
        name="out_mlp",
    )(xf, oT, oT, og, woa, wog, pmn, pre, post, wup, wdn)


def kernel(x, w_in, w_out, conv_w, A_log, dt_bias, gdn_norm_w, rel_bias, pre_mix_norm,
           post_mix_norm, pre_mlp_norm, post_mlp_norm, w_up, w_down):
    B, S, D = x.shape
    assert D == D_MODEL and S % ROW_TILE == 0 and S % MOBA_BLOCK == 0
    T = B * S
    depth = w_in.shape[0]
    xf = x.reshape(T, D)
    o0, o1, o2, o3, o4 = 0, ATT_WIDTH, 2 * ATT_WIDTH, 3 * ATT_WIDTH, 3 * ATT_WIDTH + 3 * GDN_WIDTH
    o5 = o4 + GDN_WIDTH
    for l in range(depth):
        wi = w_in[l]
        wqT = wi[:, o0:o1].T.astype(BF16)
        wk = wi[:, o1:o2].astype(BF16)
        wvT = wi[:, o2:o3].T.astype(BF16)
        wg = wi[:, o3:o4].astype(BF16)
        wz = wi[:, o4:o5].astype(BF16)
        wab = jnp.pad(wi[:, o5:], ((0, 0), (0, LANES - 2 * GDN_HEADS))).astype(BF16)
        qT, k, vT, gqkv, z, ab = _inproj(xf, pre_mix_norm[l][None, :], wqT, wk, wvT, wg, wz, wab, B, S)
        oT = _attention(rel_bias.astype(F32), qT, k.reshape(B, S, ATT_WIDTH), vT, B, S)
        og = _gdn(gqkv, z, ab, conv_w[l],
                  jnp.repeat(A_log[l], HEAD_DIM)[None, :], jnp.repeat(dt_bias[l], HEAD_DIM)[None, :],
                  jnp.tile(gdn_norm_w[l], GDN_HEADS)[None, :], B, S)
        wo = w_out[l].astype(BF16)
        xf = _out_mlp(xf, oT, og, wo[:ATT_WIDTH], wo[ATT_WIDTH:], post_mix_norm[l][None, :],
                      pre_mlp_norm[l][None, :], post_mlp_norm[l][None, :],
                      w_up[l].astype(BF16), w_down[l].astype(BF16), B, S)
    return xf.reshape(B, S, D)
```

```python
import functools
import math

import jax
import jax.numpy as jnp
from jax import lax
from jax.experimental import pallas as pl
from jax.experimental.pallas import tpu as pltpu

F32 = jnp.float32
BF16 = jnp.bfloat16
HI = lax.Precision.HIGHEST

D_MODEL = 1024
HEAD_DIM = 64
ATT_HEADS = 8
GDN_HEADS = 8
ATT_WIDTH = ATT_HEADS * HEAD_DIM
GDN_WIDTH = GDN_HEADS * HEAD_DIM
MOBA_BLOCK = 256
MOBA_TOPK = 3
GDN_CHUNK = 64
CONV_WIDTH = 4
D_FF = 4 * D_MODEL
REL_BUCKETS = 32
REL_MAX_EXACT = 16
REL_MAX_DIST = 128
EPS = 1e-6
NEG = -1e30

LANES = 128
SUBLANES = 8
VMEM_LIMIT = 56 * 1024 * 1024
ROW_TILE = 512
FF_TILE = 1024

NT = (((1,), (1,)), ((), ()))
TN = (((0,), (0,)), ((), ()))


def _bucket_lower_bounds():
    def bucket(d):
        if d < REL_MAX_EXACT:
            return d
        t = math.log(d / REL_MAX_EXACT) / math.log(REL_MAX_DIST / REL_MAX_EXACT)
        t = t * (REL_BUCKETS - REL_MAX_EXACT)
        assert d in (REL_MAX_EXACT, REL_MAX_DIST) or abs(t - round(t)) > 1e-6
        return min(REL_MAX_EXACT + int(t + 1e-9), REL_BUCKETS - 1)
    lower = []
    for b in range(REL_BUCKETS):
        d = 0
        while bucket(d) < b:
            d += 1
        lower.append(d)
    return lower


BUCKET_LOWER = _bucket_lower_bounds()


def _sigmoid(x):
    return 0.5 * jnp.tanh(0.5 * x) + 0.5


def _interleave(*stage_generators):
    live = list(stage_generators)
    while live:
        for gen in list(live):
            try:
                next(gen)
            except StopIteration:
                live.remove(gen)


def _rms(x, w):
    return x * lax.rsqrt(jnp.mean(x * x, axis=-1, keepdims=True) + EPS) * w


def _split_bf16(x, parts):
    out = []
    for _ in range(parts):
        h = x.astype(BF16)
        out.append(h)
        x = x - h.astype(F32)
    return out


def _dot_split_lhs(x, c, parts):
    acc = None
    for h in _split_bf16(x, parts):
        d = jnp.dot(h, c, preferred_element_type=F32)
        acc = d if acc is None else acc + d
    return acc


def _dot_split_rhs(c, x, parts):
    acc = None
    for h in _split_bf16(x, parts):
        d = jnp.dot(c, h, preferred_element_type=F32)
        acc = d if acc is None else acc + d
    return acc


CONV_COLS = 512


def _inproj_kernel(x_ref, xp_ref, nw_ref, wqT_ref, wk_ref, wvT_ref, wg_ref, wz_ref, wab_ref,
                   cw_ref, alog_ref, dtb_ref,
                   qT_ref, k_ref, vT_ref, g_ref, z_ref, gb_ref, *, tiles_per_seq):
    h = _rms(x_ref[...], nw_ref[...]).astype(BF16)
    qT = lax.dot_general(wqT_ref[...], h, NT, preferred_element_type=F32)
    vT = lax.dot_general(wvT_ref[...], h, NT, preferred_element_type=F32)
    for t in range(ROW_TILE // MOBA_BLOCK):
        qT_ref[0, t] = qT[:, t * MOBA_BLOCK:(t + 1) * MOBA_BLOCK]
        vT_ref[0, t] = vT[:, t * MOBA_BLOCK:(t + 1) * MOBA_BLOCK].astype(BF16)
    k_ref[...] = jnp.dot(h, wk_ref[...], preferred_element_type=F32)

    hp = _rms(xp_ref[...], nw_ref[...]).astype(BF16)
    seq_start = (pl.program_id(0) % tiles_per_seq) == 0
    trow8 = lax.broadcasted_iota(jnp.int32, (SUBLANES, CONV_COLS), 0)
    for c in range(3 * GDN_WIDTH // CONV_COLS):
        cols = slice(c * CONV_COLS, (c + 1) * CONV_COLS)
        cur = jnp.dot(h, wg_ref[:, cols], preferred_element_type=F32)
        prev8 = jnp.dot(hp, wg_ref[:, cols], preferred_element_type=F32)
        prev8 = jnp.where(seq_start, 0.0, prev8)
        acc = cur * cw_ref[CONV_WIDTH - 1:CONV_WIDTH, cols]
        for s in range(1, CONV_WIDTH):
            rolled = pltpu.roll(cur, s, 0)
            top = jnp.where(trow8 < s, pltpu.roll(prev8, s, 0), rolled[0:SUBLANES])
            tap = jnp.concatenate([top, rolled[SUBLANES:]], axis=0)
            acc = acc + tap * cw_ref[CONV_WIDTH - 1 - s:CONV_WIDTH - s, cols]
        g_ref[:, cols] = acc * _sigmoid(acc)

    z = jnp.dot(h, wz_ref[...], preferred_element_type=F32)
    z_ref[...] = z * _sigmoid(z)
    ab = jnp.dot(h, wab_ref[...], preferred_element_type=F32)
    xs = ab + dtb_ref[...]
    log_decay = -jnp.exp(alog_ref[...]) * (jnp.maximum(xs, 0.0) + jnp.log1p(jnp.exp(-jnp.abs(xs))))
    lane = lax.broadcasted_iota(jnp.int32, ab.shape, 1)
    gb_ref[...] = jnp.where(lane < GDN_HEADS, log_decay, _sigmoid(ab))


def _inproj(xf, nw, wqT, wk, wvT, wg, wz, wab, conv_w, alog_pad, dtb_pad, B, S):
    T = B * S
    nblk = S // MOBA_BLOCK
    tiles_per_seq = S // ROW_TILE
    blk_per_tile = ROW_TILE // MOBA_BLOCK
    const = lambda i: (0, 0)
    row = lambda i: (i, 0)
    tr = lambda i: (i // tiles_per_seq, i % tiles_per_seq, 0, 0)
    prev_rows = lambda i: (jnp.maximum(i * (ROW_TILE // SUBLANES) - 1, 0), 0)
    return pl.pallas_call(
        functools.partial(_inproj_kernel, tiles_per_seq=tiles_per_seq),
        grid=(T // ROW_TILE,),
        in_specs=[
            pl.BlockSpec((ROW_TILE, D_MODEL), row),
            pl.BlockSpec((SUBLANES, D_MODEL), prev_rows),
            pl.BlockSpec((1, D_MODEL), const),
            pl.BlockSpec(wqT.shape, const),
            pl.BlockSpec(wk.shape, const),
            pl.BlockSpec(wvT.shape, const),
            pl.BlockSpec(wg.shape, const),
            pl.BlockSpec(wz.shape, const),
            pl.BlockSpec(wab.shape, const),
            pl.BlockSpec(conv_w.shape, const),
            pl.BlockSpec((1, LANES), const),
            pl.BlockSpec((1, LANES), const),
        ],
        out_specs=[
            pl.BlockSpec((1, blk_per_tile, ATT_WIDTH, MOBA_BLOCK), tr),
            pl.BlockSpec((ROW_TILE, ATT_WIDTH), row),
            pl.BlockSpec((1, blk_per_tile, ATT_WIDTH, MOBA_BLOCK), tr),
            pl.BlockSpec((ROW_TILE, 3 * GDN_WIDTH), row),
            pl.BlockSpec((ROW_TILE, GDN_WIDTH), row),
            pl.BlockSpec((ROW_TILE, LANES), row),
        ],
        out_shape=[
            jax.ShapeDtypeStruct((B, nblk, ATT_WIDTH, MOBA_BLOCK), F32),
            jax.ShapeDtypeStruct((T, ATT_WIDTH), F32),
            jax.ShapeDtypeStruct((B, nblk, ATT_WIDTH, MOBA_BLOCK), BF16),
            jax.ShapeDtypeStruct((T, 3 * GDN_WIDTH), F32),
            jax.ShapeDtypeStruct((T, GDN_WIDTH), F32),
            jax.ShapeDtypeStruct((T, LANES), F32),
        ],
        compiler_params=pltpu.CompilerParams(
            dimension_semantics=("arbitrary",), vmem_limit_bytes=VMEM_LIMIT),
        name="inproj",
    )(xf, xf, nw, wqT, wk, wvT, wg, wz, wab, conv_w, alog_pad, dtb_pad)


def _attn_kernel(relb_ref, qlo_ref, qhi_ref, k_ref, vT_ref, oT_ref,
                 kb_ref, km_ref, bias_ref, addm_ref, qh_ref, lg_ref, *, nblk):
    hp = pl.program_id(0)
    b = pl.program_id(1)
    t = pl.program_id(2)
    BLK = MOBA_BLOCK
    ntile = nblk + 1

    @pl.when((b == 0) & (t == 0))
    def _():
        kk = lax.broadcasted_iota(jnp.int32, (BLK, BLK), 0)
        qq = lax.broadcasted_iota(jnp.int32, (BLK, BLK), 1)
        for hh in range(2):
            h = 2 * hp + hh
            for kind in range(2):
                d = qq - kk + kind * BLK
                val = jnp.full((BLK, BLK), relb_ref[h, REL_BUCKETS - 1], F32)
                for bkt in range(REL_BUCKETS - 2, -1, -1):
                    val = jnp.where(d < BUCKET_LOWER[bkt + 1], relb_ref[h, bkt], val)
                if kind == 0:
                    val = jnp.where(d >= 0, val, NEG)
                bias_ref[hh, kind] = val
            bias_ref[hh, 2] = jnp.full((BLK, BLK), relb_ref[h, REL_BUCKETS - 1], F32)

    @pl.when(t == 0)
    def _():
        lane = lax.broadcasted_iota(jnp.int32, (1, LANES), 1)
        for j in range(nblk):
            kj = k_ref[0, j * BLK:(j + 1) * BLK, :]
            kb_ref[j * BLK:(j + 1) * BLK, :] = kj.astype(BF16)
            kmj = jnp.sum(kj, axis=0, keepdims=True) * (1.0 / BLK)
            km_ref[j:j + 1, :] = jnp.where(lane < HEAD_DIM, kmj, 0.0)
            km_ref[nblk + j:nblk + j + 1, :] = jnp.where(lane >= HEAD_DIM, kmj, 0.0)

    ridx = lax.broadcasted_iota(jnp.int32, (nblk, BLK), 0)
    sub = lax.broadcasted_iota(jnp.int32, (LANES, BLK), 0)
    scale = HEAD_DIM ** -0.5
    q_blocks = (t, nblk - 1 - t)
    for s, (q_ref, qi) in enumerate(zip((qlo_ref, qhi_ref), q_blocks)):
        qT = q_ref[0, 0]
        gT = jnp.dot(km_ref[...], qT, precision=HI, preferred_element_type=F32)
        past = ridx < qi
        for hh in range(2):
            gm = jnp.where(past, gT[nblk * hh:nblk * (hh + 1)], -jnp.inf)
            cnt = jnp.zeros((nblk, BLK), F32)
            for jp in range(nblk):
                row = gm[jp:jp + 1, :]
                beats = (row > gm) | ((row == gm) & (ridx > jp))
                cnt = cnt + jnp.where(beats, 1.0, 0.0)
            visible = (past & (cnt < MOBA_TOPK)) | (ridx == qi)
            addm_ref[s, nblk * hh:nblk * (hh + 1), :] = jnp.where(visible, 0.0, NEG)
            in_head = (sub >= HEAD_DIM * hh) & (sub < HEAD_DIM * (hh + 1))
            qh_ref[s, hh] = jnp.where(in_head, qT * scale, 0.0).astype(BF16)

    is_lo = [n <= t for n in range(ntile)]
    slot = [jnp.where(is_lo[n], 0, 1) for n in range(ntile)]
    kblk = [jnp.where(is_lo[n], n, n - t - 1) for n in range(ntile)]
    kind = [jnp.minimum(jnp.where(is_lo[n], t, nblk - 1 - t) - kblk[n], 2) for n in range(ntile)]

    cmax = [[None] * ntile for _ in range(2)]
    for n in range(ntile):
        kj = kb_ref[pl.ds(pl.multiple_of(kblk[n] * BLK, BLK), BLK), :]
        for hh in range(2):
            lg = jnp.dot(kj, qh_ref[slot[n], hh], preferred_element_type=F32)
            lg = lg + bias_ref[hh, kind[n]] + addm_ref[slot[n], pl.ds(nblk * hh + kblk[n], 1), :]
            lg_ref[hh, n] = lg
            cmax[hh][n] = jnp.max(lg, axis=0, keepdims=True)

    for hh in range(2):
        m_lo = cmax[hh][0]
        m_hi = cmax[hh][ntile - 1]
        for n in range(1, ntile - 1):
            m_lo = jnp.maximum(m_lo, jnp.where(is_lo[n], cmax[hh][n], -jnp.inf))
            m_hi = jnp.maximum(m_hi, jnp.where(is_lo[n], -jnp.inf, cmax[hh][n]))
        l_lo = jnp.zeros((1, BLK), F32)
        l_hi = jnp.zeros((1, BLK), F32)
        acc_lo = jnp.zeros((HEAD_DIM, BLK), F32)
        acc_hi = jnp.zeros((HEAD_DIM, BLK), F32)
        for n in range(ntile):
            p = jnp.exp(lg_ref[hh, n] - jnp.where(is_lo[n], m_lo, m_hi))
            cs = jnp.sum(p, axis=0, keepdims=True)
            vj = vT_ref[0, kblk[n], HEAD_DIM * hh:HEAD_DIM * (hh + 1), :]
            pvn = jnp.dot(vj, p.astype(BF16), preferred_element_type=F32)
            l_lo = l_lo + jnp.where(is_lo[n], cs, 0.0)
            l_hi = l_hi + jnp.where(is_lo[n], 0.0, cs)
            acc_lo = acc_lo + jnp.where(is_lo[n], pvn, 0.0)
            acc_hi = acc_hi + jnp.where(is_lo[n], 0.0, pvn)
        oT_ref[0, 0, HEAD_DIM * hh:HEAD_DIM * (hh + 1), :] = acc_lo / l_lo
        oT_ref[0, 1, HEAD_DIM * hh:HEAD_DIM * (hh + 1), :] = acc_hi / l_hi


def _paired_pos(i, nblk):
    return jnp.where(i < nblk // 2, 2 * i, 2 * (nblk - 1 - i) + 1)


def _attention(rel_bias, qT, k3, vT, B, S):
    nblk = S // MOBA_BLOCK
    assert nblk % 2 == 0
    assert BUCKET_LOWER[REL_BUCKETS - 1] <= MOBA_BLOCK + 1
    return pl.pallas_call(
        functools.partial(_attn_kernel, nblk=nblk),
        grid=(ATT_HEADS // 2, B, nblk // 2),
        in_specs=[
            pl.BlockSpec(memory_space=pltpu.SMEM),
            pl.BlockSpec((1, 1, LANES, MOBA_BLOCK), lambda hp, b, t: (b, t, hp, 0)),
            pl.BlockSpec((1, 1, LANES, MOBA_BLOCK), lambda hp, b, t: (b, nblk - 1 - t, hp, 0)),
            pl.BlockSpec((1, S, LANES), lambda hp, b, t: (b, 0, hp)),
            pl.BlockSpec((1, nblk, LANES, MOBA_BLOCK), lambda hp, b, t: (b, 0, hp, 0)),
        ],
        out_specs=pl.BlockSpec((1, 2, LANES, MOBA_BLOCK), lambda hp, b, t: (b, t, hp, 0)),
        out_shape=jax.ShapeDtypeStruct((B, nblk, ATT_WIDTH, MOBA_BLOCK), F32),
        scratch_shapes=[
            pltpu.VMEM((S, LANES), BF16),
            pltpu.VMEM((2 * nblk, LANES), F32),
            pltpu.VMEM((2, 3, MOBA_BLOCK, MOBA_BLOCK), F32),
            pltpu.VMEM((2, 2 * nblk, MOBA_BLOCK), F32),
            pltpu.VMEM((2, 2, LANES, MOBA_BLOCK), BF16),
            pltpu.VMEM((2, nblk + 1, MOBA_BLOCK, MOBA_BLOCK), F32),
        ],
        compiler_params=pltpu.CompilerParams(
            dimension_semantics=("arbitrary", "arbitrary", "arbitrary"),
            vmem_limit_bytes=VMEM_LIMIT),
        name="moba_attn",
    )(rel_bias, qT, qT, k3, vT)


GDN_GROUP = 4 * HEAD_DIM
GDN_TILE = 256


def _gdn_kernel(yq_ref, yk_ref, yv_ref, sz_ref, gb_ref, nw_ref, out_ref,
                u_ref, wq_ref, a_ref, kd_ref, gl_ref, *, S):
    grp = pl.program_id(1)
    C = GDN_CHUNK
    W = GDN_GROUP
    TILE = GDN_TILE
    npair = W // LANES

    r_w = lax.broadcasted_iota(jnp.int32, (W, W), 0)
    c_w = lax.broadcasted_iota(jnp.int32, (W, W), 1)
    head_ones = jnp.where((r_w // HEAD_DIM) == (c_w // HEAD_DIM), 1.0, 0.0).astype(BF16)
    ltri_bd = jnp.where(((r_w // C) == (c_w // C)) & (c_w <= r_w), 1.0, 0.0).astype(BF16)
    r_e = lax.broadcasted_iota(jnp.int32, (LANES, W), 0)
    c_e = lax.broadcasted_iota(jnp.int32, (LANES, W), 1)
    head_of_col = grp * (W // HEAD_DIM) + c_e // HEAD_DIM
    sel_g = jnp.where(r_e == head_of_col, 1.0, 0.0).astype(BF16)
    sel_beta = jnp.where(r_e == GDN_HEADS + head_of_col, 1.0, 0.0).astype(BF16)
    tok = lax.broadcasted_iota(jnp.int32, (TILE, W), 0) % C
    col = lax.broadcasted_iota(jnp.int32, (TILE, W), 1) % HEAD_DIM
    causal_t = tok >= col
    strict_t = tok > col

    lane = lax.broadcasted_iota(jnp.int32, (C, LANES), 1)
    rowi = lax.broadcasted_iota(jnp.int32, (C, LANES), 0)
    first_head = lane < HEAD_DIM
    strict = rowi > (lane % HEAD_DIM)
    eye2 = jnp.where(rowi == (lane % HEAD_DIM), 1.0, 0.0)
    lane2 = lax.broadcasted_iota(jnp.int32, (C, 2 * LANES), 1)
    first_head2 = (lane2 % LANES) < HEAD_DIM
    r_l = lax.broadcasted_iota(jnp.int32, (LANES, LANES), 0)
    c_l = lax.broadcasted_iota(jnp.int32, (LANES, LANES), 1)
    same_head = (r_l // HEAD_DIM) == (c_l // HEAD_DIM)
    pair_ones = jnp.where(same_head, 1.0, 0.0).astype(BF16)

    def stack(x, mask):
        return jnp.concatenate([jnp.where(mask, x, 0.0), jnp.where(mask, 0.0, x)], axis=0)

    dot = functools.partial(jnp.dot, preferred_element_type=F32)

    def solve_stages(r):
        yield
        r0 = pl.multiple_of(r * TILE, TILE)
        yq = yq_ref[pl.ds(r0, TILE), :]
        yk = yk_ref[pl.ds(r0, TILE), :]
        yv = yv_ref[pl.ds(r0, TILE), :]
        ssq = _dot_split_lhs(yq * yq, head_ones, 2)
        ssk = _dot_split_lhs(yk * yk, head_ones, 2)
        qn = yq * lax.rsqrt(ssq + EPS) * (HEAD_DIM ** -0.5)
        kn = yk * lax.rsqrt(ssk + EPS)
        gb3 = _split_bf16(gb_ref[pl.ds(r0, TILE), :], 3)
        g = sum(dot(h, sel_g) for h in gb3)
        beta = sum(dot(h, sel_beta) for h in gb3)
        gcd = _dot_split_rhs(ltri_bd, jnp.concatenate([g, jnp.where(strict_t, g, 0.0)], axis=1), 3)
        gc = gcd[:, :W]
        decay = jnp.where(causal_t, jnp.exp(jnp.where(causal_t, gcd[:, W:], 0.0)), 0.0)
        egc = jnp.exp(gc)
        kb = kn * beta
        rv = yv * beta
        rk = kb * egc
        qd = qn * egc
        for cc in range(TILE // C):
            rs = slice(cc * C, (cc + 1) * C)
            ci = r * (TILE // C) + cc
            g_last = gc[(cc + 1) * C - 1:(cc + 1) * C, :]
            kd_ref[pl.ds(pl.multiple_of(r0 + cc * C, C), C), :] = (
                kn[rs] * jnp.exp(g_last - gc[rs])).astype(BF16)
            gl_ref[pl.ds(pl.multiple_of(ci * SUBLANES, SUBLANES), SUBLANES), :] = (
                jnp.broadcast_to(jnp.exp(g_last), (SUBLANES, W)))
        units = [(slice(cc * C, (cc + 1) * C), slice(LANES * p, LANES * (p + 1)), cc)
                 for cc in range(TILE // C) for p in range(npair)]
        kqs = [lax.dot_general(jnp.concatenate([kn[rs, ls], qn[rs, ls]], axis=0).astype(BF16),
                               stack(kn[rs, ls], first_head).astype(BF16), NT,
                               preferred_element_type=F32) for rs, ls, _ in units]
        yield
        ns = [-jnp.where(strict, kq[0:C] * beta[rs, ls] * decay[rs, ls], 0.0)
              for kq, (rs, ls, _) in zip(kqs, units)]
        sps = [stack(n, first_head).astype(BF16) for n in ns]
        ps = ns
        tinvs = [eye2 + n for n in ns]
        for _ in range(int(math.log2(C)) - 1):
            ps = [dot(p.astype(BF16), sp) for p, sp in zip(ps, sps)]
            sps = [stack(p, first_head).astype(BF16) for p in ps]
            tinvs = [t + dot(t.astype(BF16), sp) for t, sp in zip(tinvs, sps)]
            yield
        xs = [dot(t.astype(BF16),
                  stack(jnp.concatenate([rv[rs, ls], rk[rs, ls]], axis=1), first_head2).astype(BF16))
              for t, (rs, ls, _) in zip(tinvs, units)]
        yield
        for x, kq, (rs, ls, cc) in zip(xs, kqs, units):
            ci = r * (TILE // C) + cc
            crow = pl.ds(pl.multiple_of(r0 + cc * C, C), C)
            u_ref[crow, ls] = x[:, :LANES]
            wq_ref[ci, 0:C, ls] = x[:, LANES:].astype(BF16)
            wq_ref[ci, C:2 * C, ls] = qd[rs, ls].astype(BF16)
            a_ref[crow, ls] = (kq[C:2 * C] * decay[rs, ls]).astype(BF16)

    lss = [slice(LANES * p, LANES * (p + 1)) for p in range(npair)]

    def recurrence_stages(r, states):
        pending = None

        def finish(rows, os_):
            for ls, o in zip(lss, os_):
                ms = _dot_split_lhs(o * o, pair_ones, 2) * (1.0 / HEAD_DIM)
                out_ref[rows, ls] = o * lax.rsqrt(ms + EPS) * nw_ref[:, ls] * sz_ref[rows, ls]

        for cc in range(TILE // C):
            c = r * (TILE // C) + cc
            rows = pl.ds(pl.multiple_of(c * C, C), C)
            wqs = [dot(wq_ref[c, :, ls], st.astype(BF16)) for ls, st in zip(lss, states)]
            if pending is not None:
                finish(*pending)
            yield
            v_news = [u_ref[rows, ls] - wq[0:C] for ls, wq in zip(lss, wqs)]
            kvs = [lax.dot_general(kd_ref[rows, ls], v.astype(BF16), TN, preferred_element_type=F32)
                   for ls, v in zip(lss, v_news)]
            os_ = [wq[C:2 * C] + dot(a_ref[rows, ls], stack(v, first_head).astype(BF16))
                   for ls, wq, v in zip(lss, wqs, v_news)]
            gl_row = pl.ds(pl.multiple_of(c * SUBLANES, SUBLANES), 1)
            states[:] = [st * gl_ref[gl_row, ls] + jnp.where(same_head, kv, 0.0)
                         for ls, st, kv in zip(lss, states, kvs)]
            pending = (rows, os_)
            yield
        finish(*pending)

    ntile = S // TILE
    _interleave(solve_stages(0))

    def tile_body(r, states):
        states = list(states)
        _interleave(recurrence_stages(r - 1, states), solve_stages(r))
        return tuple(states)

    init = tuple(jnp.zeros((LANES, LANES), F32) for _ in range(npair))
    states = list(lax.fori_loop(1, ntile, tile_body, init))
    _interleave(recurrence_stages(ntile - 1, states))


def _gdn(gqkv, sz, gb, nw_row, B, S):
    T = B * S
    W = GDN_GROUP
    ngrp = GDN_WIDTH // W
    nchunk = S // GDN_CHUNK
    col = lambda off: (lambda b, g: (b, off + g))
    ccol = lambda off: (lambda b, g: (0, off + g))
    return pl.pallas_call(
        functools.partial(_gdn_kernel, S=S),
        grid=(B, ngrp),
        in_specs=[
            pl.BlockSpec((S, W), col(0)),
            pl.BlockSpec((S, W), col(ngrp)),
            pl.BlockSpec((S, W), col(2 * ngrp)),
            pl.BlockSpec((S, W), col(0)),
            pl.BlockSpec((S, LANES), lambda b, g: (b, 0)),
            pl.BlockSpec((1, W), ccol(0)),
        ],
        out_specs=pl.BlockSpec((S, W), col(0)),
        out_shape=jax.ShapeDtypeStruct((T, GDN_WIDTH), F32),
        scratch_shapes=[
            pltpu.VMEM((S, W), F32),
            pltpu.VMEM((nchunk, 2 * GDN_CHUNK, W), BF16),
            pltpu.VMEM((S, W), BF16),
            pltpu.VMEM((S, W), BF16),
            pltpu.VMEM((nchunk * SUBLANES, W), F32),
        ],
        compiler_params=pltpu.CompilerParams(
            dimension_semantics=("arbitrary", "arbitrary"), vmem_limit_bytes=VMEM_LIMIT),
        name="gdn",
    )(gqkv, gqkv, gqkv, sz, gb, nw_row)


def _out_mlp_kernel(x_ref, oTa_ref, oTb_ref, og_ref, woa_ref, wog_ref, pmn_ref, pre_ref, post_ref,
                    wup_ref, wdn_ref, out_ref):
    oT = jnp.concatenate([oTa_ref[0, 0], oTb_ref[0, 0]], axis=1)
    o_att = oT.T.astype(BF16)
    mix = jnp.dot(o_att, woa_ref[...], preferred_element_type=F32)
    mix = mix + jnp.dot(og_ref[...].astype(BF16), wog_ref[...], preferred_element_type=F32)
    x1 = x_ref[...] + _rms(mix, pmn_ref[...])
    h = _rms(x1, pre_ref[...]).astype(BF16)
    acc = jnp.zeros((ROW_TILE, D_MODEL), F32)
    for c in range(D_FF // FF_TILE):
        up = jnp.dot(h, wup_ref[:, c * FF_TILE:(c + 1) * FF_TILE], preferred_element_type=F32)
        act = jnp.square(jnp.maximum(up, 0.0)).astype(BF16)
        acc = acc + jnp.dot(act, wdn_ref[c * FF_TILE:(c + 1) * FF_TILE, :], preferred_element_type=F32)
    out_ref[...] = x1 + _rms(acc, post_ref[...])


def _out_mlp(xf, oT, og, woa, wog, pmn, pre, post, wup, wdn, B, S):
    T = B * S
    nblk = S // MOBA_BLOCK
    tiles_per_seq = S // ROW_TILE
    assert ROW_TILE == 2 * MOBA_BLOCK
    const = lambda i: (0, 0)
    row = lambda i: (i, 0)

    def att_block(which):
        def index(i):
            blk = 2 * (i % tiles_per_seq) + which
            return (i // tiles_per_seq, _paired_pos(blk, nblk), 0, 0)
        return index

    single = dict(pipeline_mode=pl.Buffered(1))
    return pl.pallas_call(
        _out_mlp_kernel,
        grid=(T // ROW_TILE,),
        in_specs=[
            pl.BlockSpec((ROW_TILE, D_MODEL), row),
            pl.BlockSpec((1, 1, ATT_WIDTH, MOBA_BLOCK), att_block(0)),
            pl.BlockSpec((1, 1, ATT_WIDTH, MOBA_BLOCK), att_block(1)),
            pl.BlockSpec((ROW_TILE, GDN_WIDTH), row),
            pl.BlockSpec(woa.shape, const, **single),
            pl.BlockSpec(wog.shape, const, **single),
            pl.BlockSpec((1, D_MODEL), const),
            pl.BlockSpec((1, D_MODEL), const),
            pl.BlockSpec((1, D_MODEL), const),
            pl.BlockSpec(wup.shape, const, **single),
            pl.BlockSpec(wdn.shape, const, **single),
        ],
        out_specs=pl.BlockSpec((ROW_TILE, D_MODEL), row),
        out_shape=jax.ShapeDtypeStruct((T, D_MODEL), F32),
        compiler_params=pltpu.CompilerParams(
            dimension_semantics=("arbitrary",), vmem_limit_bytes=VMEM_LIMIT),
        name="out_mlp",
    )(xf, oT, oT, og, woa, wog, pmn, pre, post, wup, wdn)


def kernel(x, w_in, w_out, conv_w, A_log, dt_bias, gdn_norm_w, rel_bias, pre_mix_norm,
           post_mix_norm, pre_mlp_norm, post_mlp_norm, w_up, w_down):
    B, S, D = x.shape
    assert D == D_MODEL and S % ROW_TILE == 0 and S % MOBA_BLOCK == 0
    T = B * S
    depth = w_in.shape[0]
    xf = x.reshape(T, D)
    o0, o1, o2, o3, o4 = 0, ATT_WIDTH, 2 * ATT_WIDTH, 3 * ATT_WIDTH, 3 * ATT_WIDTH + 3 * GDN_WIDTH
    o5 = o4 + GDN_WIDTH
    for l in range(depth):
        wi = w_in[l]
        wqT = wi[:, o0:o1].T.astype(BF16)
        wk = wi[:, o1:o2].astype(BF16)
        wvT = wi[:, o2:o3].T.astype(BF16)
        wg = wi[:, o3:o4].astype(BF16)
        wz = wi[:, o4:o5].astype(BF16)
        wab = jnp.pad(wi[:, o5:], ((0, 0), (0, LANES - 2 * GDN_HEADS))).astype(BF16)
        pad8 = lambda v: jnp.pad(v.astype(F32), (0, LANES - GDN_HEADS))[None, :]
        qT, k, vT, gqkv, sz, gb = _inproj(xf, pre_mix_norm[l][None, :], wqT, wk, wvT, wg, wz, wab,
                                          conv_w[l], pad8(A_log[l]), pad8(dt_bias[l]), B, S)
        oT = _attention(rel_bias.astype(F32), qT, k.reshape(B, S, ATT_WIDTH), vT, B, S)
        og = _gdn(gqkv, sz, gb, jnp.tile(gdn_norm_w[l], GDN_HEADS)[None, :], B, S)
        wo = w_out[l].astype(BF16)
        xf = _out_mlp(xf, oT, og, wo[:ATT_WIDTH], wo[ATT_WIDTH:], post_mix_norm[l][None, :],
                      pre_mlp_norm[l][None, :], post_mlp_norm[l][None, :],
                      w_up[l].astype(BF16), w_down[l].astype(BF16), B, S)
    return xf.reshape(B, S, D)
```

```python
import functools
import math

import jax
import jax.numpy as jnp
from jax import lax
from jax.experimental import pallas as pl
from jax.experimental.pallas import tpu as pltpu

F32 = jnp.float32
BF16 = jnp.bfloat16
HI = lax.Precision.HIGHEST

D_MODEL = 1024
HEAD_DIM = 64
ATT_HEADS = 8
GDN_HEADS = 8
ATT_WIDTH = ATT_HEADS * HEAD_DIM
GDN_WIDTH = GDN_HEADS * HEAD_DIM
MOBA_BLOCK = 256
MOBA_TOPK = 3
GDN_CHUNK = 64
CONV_WIDTH = 4
D_FF = 4 * D_MODEL
REL_BUCKETS = 32
REL_MAX_EXACT = 16
REL_MAX_DIST = 128
EPS = 1e-6
NEG = -1e30
LOG2E = math.log2(math.e)

LANES = 128
SUBLANES = 8
VMEM_LIMIT = 56 * 1024 * 1024
ROW_TILE = 512
FF_TILE = 1024

NT = (((1,), (1,)), ((), ()))
TN = (((0,), (0,)), ((), ()))


def _bucket_lower_bounds():
    def bucket(d):
        if d < REL_MAX_EXACT:
            return d
        t = math.log(d / REL_MAX_EXACT) / math.log(REL_MAX_DIST / REL_MAX_EXACT)
        t = t * (REL_BUCKETS - REL_MAX_EXACT)
        assert d in (REL_MAX_EXACT, REL_MAX_DIST) or abs(t - round(t)) > 1e-6
        return min(REL_MAX_EXACT + int(t + 1e-9), REL_BUCKETS - 1)
    lower = []
    for b in range(REL_BUCKETS):
        d = 0
        while bucket(d) < b:
            d += 1
        lower.append(d)
    return lower


BUCKET_LOWER = _bucket_lower_bounds()


def _sigmoid(x):
    return 0.5 * jnp.tanh(0.5 * x) + 0.5


def _interleave(*stage_generators):
    live = list(stage_generators)
    while live:
        for gen in list(live):
            try:
                next(gen)
            except StopIteration:
                live.remove(gen)


def _rms(x, w):
    return x * lax.rsqrt(jnp.mean(x * x, axis=-1, keepdims=True) + EPS) * w


def _split_bf16(x, parts):
    out = []
    for _ in range(parts):
        h = x.astype(BF16)
        out.append(h)
        x = x - h.astype(F32)
    return out


def _dot_split_lhs(x, c, parts):
    acc = None
    for h in _split_bf16(x, parts):
        d = jnp.dot(h, c, preferred_element_type=F32)
        acc = d if acc is None else acc + d
    return acc


def _dot_split_rhs(c, x, parts):
    acc = None
    for h in _split_bf16(x, parts):
        d = jnp.dot(c, h, preferred_element_type=F32)
        acc = d if acc is None else acc + d
    return acc


CONV_COLS = 512


def _inproj_kernel(x_ref, xp_ref, nw_ref, wqT_ref, wk_ref, wvT_ref, wg_ref, wz_ref, wab_ref,
                   cw_ref, alog_ref, dtb_ref,
                   qT_ref, k_ref, vT_ref, g_ref, z_ref, gb_ref, *, tiles_per_seq):
    h = _rms(x_ref[...], nw_ref[...]).astype(BF16)

    hp = _rms(xp_ref[...], nw_ref[...]).astype(BF16)
    seq_start = (pl.program_id(0) % tiles_per_seq) == 0
    trow8 = lax.broadcasted_iota(jnp.int32, (SUBLANES, CONV_COLS), 0)
    for c in range(3 * GDN_WIDTH // CONV_COLS):
        cols = slice(c * CONV_COLS, (c + 1) * CONV_COLS)
        cur = jnp.dot(h, wg_ref[:, cols], preferred_element_type=F32)
        prev8 = jnp.dot(hp, wg_ref[:, cols], preferred_element_type=F32)
        prev8 = jnp.where(seq_start, 0.0, prev8)
        acc = cur * cw_ref[CONV_WIDTH - 1:CONV_WIDTH, cols]
        for s in range(1, CONV_WIDTH):
            rolled = pltpu.roll(cur, s, 0)
            top = jnp.where(trow8 < s, pltpu.roll(prev8, s, 0), rolled[0:SUBLANES])
            tap = jnp.concatenate([top, rolled[SUBLANES:]], axis=0)
            acc = acc + tap * cw_ref[CONV_WIDTH - 1 - s:CONV_WIDTH - s, cols]
        g_ref[:, cols] = acc * _sigmoid(acc)

    z = jnp.dot(h, wz_ref[...], preferred_element_type=F32)
    z_ref[...] = z * _sigmoid(z)
    ab = jnp.dot(h, wab_ref[...], preferred_element_type=F32)
    xs = ab + dtb_ref[...]
    log_decay = -jnp.exp(alog_ref[...]) * (jnp.maximum(xs, 0.0) + jnp.log1p(jnp.exp(-jnp.abs(xs))))
    lane = lax.broadcasted_iota(jnp.int32, ab.shape, 1)
    gb_ref[...] = jnp.where(lane < GDN_HEADS, log_decay, _sigmoid(ab))

    qT = lax.dot_general(wqT_ref[...], h, NT, preferred_element_type=F32)
    vT = lax.dot_general(wvT_ref[...], h, NT, preferred_element_type=F32)
    for t in range(ROW_TILE // MOBA_BLOCK):
        qT_ref[0, t] = qT[:, t * MOBA_BLOCK:(t + 1) * MOBA_BLOCK]
        vT_ref[0, t] = vT[:, t * MOBA_BLOCK:(t + 1) * MOBA_BLOCK].astype(BF16)
    k_ref[...] = jnp.dot(h, wk_ref[...], preferred_element_type=F32)


def _inproj(xf, nw, wqT, wk, wvT, wg, wz, wab, conv_w, alog_pad, dtb_pad, B, S):
    T = B * S
    nblk = S // MOBA_BLOCK
    tiles_per_seq = S // ROW_TILE
    blk_per_tile = ROW_TILE // MOBA_BLOCK
    const = lambda i: (0, 0)
    row = lambda i: (i, 0)
    tr = lambda i: (i // tiles_per_seq, i % tiles_per_seq, 0, 0)
    prev_rows = lambda i: (jnp.maximum(i * (ROW_TILE // SUBLANES) - 1, 0), 0)
    return pl.pallas_call(
        functools.partial(_inproj_kernel, tiles_per_seq=tiles_per_seq),
        grid=(T // ROW_TILE,),
        in_specs=[
            pl.BlockSpec((ROW_TILE, D_MODEL), row),
            pl.BlockSpec((SUBLANES, D_MODEL), prev_rows),
            pl.BlockSpec((1, D_MODEL), const),
            pl.BlockSpec(wqT.shape, const),
            pl.BlockSpec(wk.shape, const),
            pl.BlockSpec(wvT.shape, const),
            pl.BlockSpec(wg.shape, const),
            pl.BlockSpec(wz.shape, const),
            pl.BlockSpec(wab.shape, const),
            pl.BlockSpec(conv_w.shape, const),
            pl.BlockSpec((1, LANES), const),
            pl.BlockSpec((1, LANES), const),
        ],
        out_specs=[
            pl.BlockSpec((1, blk_per_tile, ATT_WIDTH, MOBA_BLOCK), tr),
            pl.BlockSpec((ROW_TILE, ATT_WIDTH), row),
            pl.BlockSpec((1, blk_per_tile, ATT_WIDTH, MOBA_BLOCK), tr),
            pl.BlockSpec((ROW_TILE, 3 * GDN_WIDTH), row),
            pl.BlockSpec((ROW_TILE, GDN_WIDTH), row),
            pl.BlockSpec((ROW_TILE, LANES), row),
        ],
        out_shape=[
            jax.ShapeDtypeStruct((B, nblk, ATT_WIDTH, MOBA_BLOCK), F32),
            jax.ShapeDtypeStruct((T, ATT_WIDTH), F32),
            jax.ShapeDtypeStruct((B, nblk, ATT_WIDTH, MOBA_BLOCK), BF16),
            jax.ShapeDtypeStruct((T, 3 * GDN_WIDTH), F32),
            jax.ShapeDtypeStruct((T, GDN_WIDTH), F32),
            jax.ShapeDtypeStruct((T, LANES), F32),
        ],
        compiler_params=pltpu.CompilerParams(
            dimension_semantics=("arbitrary",), vmem_limit_bytes=VMEM_LIMIT),
        name="inproj",
    )(xf, xf, nw, wqT, wk, wvT, wg, wz, wab, conv_w, alog_pad, dtb_pad)


V_ROWS = HEAD_DIM + 16


def _attn_kernel(relb_ref, qlo_ref, qhi_ref, k_ref, vT_ref, oT_ref,
                 kb_ref, km_ref, va_ref, bias_ref, addm_ref, qh_ref, lg_ref, *, nblk):
    hp = pl.program_id(0)
    b = pl.program_id(1)
    t = pl.program_id(2)
    BLK = MOBA_BLOCK

    @pl.when((b == 0) & (t == 0))
    def _():
        kk = lax.broadcasted_iota(jnp.int32, (BLK, BLK), 0)
        qq = lax.broadcasted_iota(jnp.int32, (BLK, BLK), 1)
        for hh in range(2):
            h = 2 * hp + hh
            for kind in range(2):
                d = qq - kk + kind * BLK
                val = jnp.full((BLK, BLK), relb_ref[h, REL_BUCKETS - 1], F32)
                for bkt in range(REL_BUCKETS - 2, -1, -1):
                    val = jnp.where(d < BUCKET_LOWER[bkt + 1], relb_ref[h, bkt], val)
                val = val * LOG2E
                if kind == 0:
                    val = jnp.where(d >= 0, val, NEG)
                bias_ref[hh, kind] = val
            bias_ref[hh, 2] = jnp.full((BLK, BLK), relb_ref[h, REL_BUCKETS - 1] * LOG2E, F32)

    @pl.when(t == 0)
    def _():
        lane = lax.broadcasted_iota(jnp.int32, (1, LANES), 1)
        ones_row = jnp.where(lax.broadcasted_iota(jnp.int32, (V_ROWS - HEAD_DIM, BLK), 0) == 0,
                             1.0, 0.0).astype(BF16)
        for j in range(nblk):
            kj = k_ref[0, j * BLK:(j + 1) * BLK, :]
            kb_ref[j * BLK:(j + 1) * BLK, :] = kj.astype(BF16)
            kmj = jnp.sum(kj, axis=0, keepdims=True) * (1.0 / BLK)
            km_ref[j:j + 1, :] = jnp.where(lane < HEAD_DIM, kmj, 0.0)
            km_ref[nblk + j:nblk + j + 1, :] = jnp.where(lane >= HEAD_DIM, kmj, 0.0)
            for hh in range(2):
                va_ref[j, hh, 0:HEAD_DIM, :] = vT_ref[0, j, HEAD_DIM * hh:HEAD_DIM * (hh + 1), :]
                va_ref[j, hh, HEAD_DIM:V_ROWS, :] = ones_row

    ridx = lax.broadcasted_iota(jnp.int32, (nblk, BLK), 0)
    sub = lax.broadcasted_iota(jnp.int32, (LANES, BLK), 0)
    scale = HEAD_DIM ** -0.5 * LOG2E
    q_blocks = (t, nblk - 1 - t)
    for s, (q_ref, qi) in enumerate(zip((qlo_ref, qhi_ref), q_blocks)):
        qT = q_ref[0, 0]
        gT = jnp.dot(km_ref[...], qT, precision=HI, preferred_element_type=F32)
        past = ridx < qi
        for hh in range(2):
            gm = jnp.where(past, gT[nblk * hh:nblk * (hh + 1)], -jnp.inf)
            cnt = jnp.zeros((nblk, BLK), F32)
            for jp in range(nblk):
                row = gm[jp:jp + 1, :]
                beats = (row > gm) | ((row == gm) & (ridx > jp))
                cnt = cnt + jnp.where(beats, 1.0, 0.0)
            visible = (past & (cnt < MOBA_TOPK)) | (ridx == qi)
            addm_ref[s, nblk * hh:nblk * (hh + 1), :] = jnp.where(visible, 0.0, NEG)
            in_head = (sub >= HEAD_DIM * hh) & (sub < HEAD_DIM * (hh + 1))
            qh_ref[s, hh] = jnp.where(in_head, qT * scale, 0.0).astype(BF16)

    i_hi = nblk - 1 - t
    has_lo_prev = t >= 1
    n_lo_far = jnp.maximum(t - 1, 0)
    tiles = [(0, t, 0, False), (1, i_hi, 0, False), (1, i_hi - 1, 1, False),
             (jnp.where(has_lo_prev, 0, 1), jnp.where(has_lo_prev, t - 1, i_hi - 2),
              jnp.where(has_lo_prev, 1, 2), False)]
    for f in range(nblk - 3):
        f_lo = f < n_lo_far
        tiles.append((jnp.where(f_lo, 0, 1), jnp.where(f_lo, f, f - n_lo_far), None, True))
    ntile = len(tiles)

    cmax = [[None] * ntile for _ in range(2)]
    rowoff = [[None] * ntile for _ in range(2)]
    for n, (slot, kblk, bias_kind, far) in enumerate(tiles):
        kj = kb_ref[pl.ds(pl.multiple_of(kblk * BLK, BLK), BLK), :]
        for hh in range(2):
            lg = jnp.dot(kj, qh_ref[slot, hh], preferred_element_type=F32)
            if bias_kind is not None:
                lg = lg + bias_ref[hh, bias_kind]
            lg_ref[hh, n] = lg
            cm = jnp.max(lg, axis=0, keepdims=True)
            if n >= 2:
                off = addm_ref[slot, pl.ds(nblk * hh + kblk, 1), :]
                if far:
                    off = off + relb_ref[2 * hp + hh, REL_BUCKETS - 1] * LOG2E
                rowoff[hh][n] = off
                cm = cm + off
            cmax[hh][n] = cm

    def to_lo(n, x, other):
        slot = tiles[n][0]
        if isinstance(slot, int):
            return x if slot == 0 else other
        return jnp.where(slot == 0, x, other)

    for hh in range(2):
        m_lo = cmax[hh][0]
        m_hi = cmax[hh][1]
        for n in range(2, ntile):
            m_lo = jnp.maximum(m_lo, to_lo(n, cmax[hh][n], -jnp.inf))
            m_hi = jnp.maximum(m_hi, to_lo(n, -jnp.inf, cmax[hh][n]))
        acc_lo = None
        acc_hi = None
        for n, (slot, kblk, _, _) in enumerate(tiles):
            m_n = to_lo(n, m_lo, m_hi)
            if rowoff[hh][n] is not None:
                m_n = m_n - rowoff[hh][n]
            p = jnp.exp2(lg_ref[hh, n] - m_n)
            pvn = jnp.dot(va_ref[kblk, hh], p.astype(BF16), preferred_element_type=F32)
            if isinstance(slot, int):
                if slot == 0:
                    acc_lo = pvn if acc_lo is None else acc_lo + pvn
                else:
                    acc_hi = pvn if acc_hi is None else acc_hi + pvn
            else:
                acc_lo = acc_lo + jnp.where(slot == 0, pvn, 0.0)
                acc_hi = acc_hi + jnp.where(slot == 0, 0.0, pvn)
        oT_ref[0, 0, HEAD_DIM * hh:HEAD_DIM * (hh + 1), :] = (
            acc_lo[0:HEAD_DIM] / acc_lo[HEAD_DIM:HEAD_DIM + 1])
        oT_ref[0, 1, HEAD_DIM * hh:HEAD_DIM * (hh + 1), :] = (
            acc_hi[0:HEAD_DIM] / acc_hi[HEAD_DIM:HEAD_DIM + 1])


def _paired_pos(i, nblk):
    return jnp.where(i < nblk // 2, 2 * i, 2 * (nblk - 1 - i) + 1)


def _attention(rel_bias, qT, k3, vT, B, S):
    nblk = S // MOBA_BLOCK
    assert nblk % 2 == 0 and nblk >= 4
    assert BUCKET_LOWER[REL_BUCKETS - 1] <= MOBA_BLOCK + 1
    return pl.pallas_call(
        functools.partial(_attn_kernel, nblk=nblk),
        grid=(ATT_HEADS // 2, B, nblk // 2),
        in_specs=[
            pl.BlockSpec(memory_space=pltpu.SMEM),
            pl.BlockSpec((1, 1, LANES, MOBA_BLOCK), lambda hp, b, t: (b, t, hp, 0)),
            pl.BlockSpec((1, 1, LANES, MOBA_BLOCK), lambda hp, b, t: (b, nblk - 1 - t, hp, 0)),
            pl.BlockSpec((1, S, LANES), lambda hp, b, t: (b, 0, hp)),
            pl.BlockSpec((1, nblk, LANES, MOBA_BLOCK), lambda hp, b, t: (b, 0, hp, 0)),
        ],
        out_specs=pl.BlockSpec((1, 2, LANES, MOBA_BLOCK), lambda hp, b, t: (b, t, hp, 0)),
        out_shape=jax.ShapeDtypeStruct((B, nblk, ATT_WIDTH, MOBA_BLOCK), F32),
        scratch_shapes=[
            pltpu.VMEM((S, LANES), BF16),
            pltpu.VMEM((2 * nblk, LANES), F32),
            pltpu.VMEM((nblk, 2, V_ROWS, MOBA_BLOCK), BF16),
            pltpu.VMEM((2, 3, MOBA_BLOCK, MOBA_BLOCK), F32),
            pltpu.VMEM((2, 2 * nblk, MOBA_BLOCK), F32),
            pltpu.VMEM((2, 2, LANES, MOBA_BLOCK), BF16),
            pltpu.VMEM((2, nblk + 1, MOBA_BLOCK, MOBA_BLOCK), F32),
        ],
        compiler_params=pltpu.CompilerParams(
            dimension_semantics=("arbitrary", "arbitrary", "arbitrary"),
            vmem_limit_bytes=VMEM_LIMIT),
        name="moba_attn",
    )(rel_bias, qT, qT, k3, vT)


GDN_GROUP = 4 * HEAD_DIM
GDN_TILE = 256


def _gdn_kernel(yq_ref, yk_ref, yv_ref, sz_ref, gb_ref, nw_ref, out_ref,
                u_ref, wq_ref, a_ref, kd_ref, gl_ref, *, S):
    grp = pl.program_id(1)
    C = GDN_CHUNK
    W = GDN_GROUP
    TILE = GDN_TILE
    npair = W // LANES

    r_w = lax.broadcasted_iota(jnp.int32, (W, W), 0)
    c_w = lax.broadcasted_iota(jnp.int32, (W, W), 1)
    head_ones = jnp.where((r_w // HEAD_DIM) == (c_w // HEAD_DIM), 1.0, 0.0).astype(BF16)
    ltri_bd = jnp.where(((r_w // C) == (c_w // C)) & (c_w <= r_w), 1.0, 0.0).astype(BF16)
    r_e = lax.broadcasted_iota(jnp.int32, (LANES, W), 0)
    c_e = lax.broadcasted_iota(jnp.int32, (LANES, W), 1)
    head_of_col = grp * (W // HEAD_DIM) + c_e // HEAD_DIM
    sel_g = jnp.where(r_e == head_of_col, 1.0, 0.0).astype(BF16)
    sel_beta = jnp.where(r_e == GDN_HEADS + head_of_col, 1.0, 0.0).astype(BF16)
    tok = lax.broadcasted_iota(jnp.int32, (TILE, W), 0) % C
    col = lax.broadcasted_iota(jnp.int32, (TILE, W), 1) % HEAD_DIM
    causal_t = tok >= col
    strict_t = tok > col

    lane = lax.broadcasted_iota(jnp.int32, (C, LANES), 1)
    rowi = lax.broadcasted_iota(jnp.int32, (C, LANES), 0)
    first_head = lane < HEAD_DIM
    strict = rowi > (lane % HEAD_DIM)
    eye2 = jnp.where(rowi == (lane % HEAD_DIM), 1.0, 0.0)
    lane2 = lax.broadcasted_iota(jnp.int32, (C, 2 * LANES), 1)
    first_head2 = (lane2 % LANES) < HEAD_DIM
    r_l = lax.broadcasted_iota(jnp.int32, (LANES, LANES), 0)
    c_l = lax.broadcasted_iota(jnp.int32, (LANES, LANES), 1)
    same_head = (r_l // HEAD_DIM) == (c_l // HEAD_DIM)
    pair_ones = jnp.where(same_head, 1.0, 0.0).astype(BF16)

    def stack(x, mask):
        return jnp.concatenate([jnp.where(mask, x, 0.0), jnp.where(mask, 0.0, x)], axis=0)

    dot = functools.partial(jnp.dot, preferred_element_type=F32)

    def solve_stages(r):
        yield
        r0 = pl.multiple_of(r * TILE, TILE)
        yq = yq_ref[pl.ds(r0, TILE), :]
        yk = yk_ref[pl.ds(r0, TILE), :]
        yv = yv_ref[pl.ds(r0, TILE), :]
        ssq = _dot_split_lhs(yq * yq, head_ones, 2)
        ssk = _dot_split_lhs(yk * yk, head_ones, 2)
        qn = yq * lax.rsqrt(ssq + EPS) * (HEAD_DIM ** -0.5)
        kn = yk * lax.rsqrt(ssk + EPS)
        gb3 = _split_bf16(gb_ref[pl.ds(r0, TILE), :], 3)
        g = sum(dot(h, sel_g) for h in gb3)
        beta = sum(dot(h, sel_beta) for h in gb3)
        gcd = _dot_split_rhs(ltri_bd, jnp.concatenate([g, jnp.where(strict_t, g, 0.0)], axis=1), 3)
        gc = gcd[:, :W]
        decay = jnp.where(causal_t, jnp.exp(jnp.where(causal_t, gcd[:, W:], 0.0)), 0.0)
        egc = jnp.exp(gc)
        kb = kn * beta
        rv = yv * beta
        rk = kb * egc
        qd = qn * egc
        for cc in range(TILE // C):
            rs = slice(cc * C, (cc + 1) * C)
            ci = r * (TILE // C) + cc
            g_last = gc[(cc + 1) * C - 1:(cc + 1) * C, :]
            kd_ref[pl.ds(pl.multiple_of(r0 + cc * C, C), C), :] = (
                kn[rs] * jnp.exp(g_last - gc[rs])).astype(BF16)
            gl_ref[pl.ds(pl.multiple_of(ci * SUBLANES, SUBLANES), SUBLANES), :] = (
                jnp.broadcast_to(jnp.exp(g_last), (SUBLANES, W)))
        units = [(slice(cc * C, (cc + 1) * C), slice(LANES * p, LANES * (p + 1)), cc)
                 for cc in range(TILE // C) for p in range(npair)]
        kqs = [lax.dot_general(jnp.concatenate([kn[rs, ls], qn[rs, ls]], axis=0).astype(BF16),
                               stack(kn[rs, ls], first_head).astype(BF16), NT,
                               preferred_element_type=F32) for rs, ls, _ in units]
        yield
        ns = [-jnp.where(strict, kq[0:C] * beta[rs, ls] * decay[rs, ls], 0.0)
              for kq, (rs, ls, _) in zip(kqs, units)]
        sps = [stack(n, first_head).astype(BF16) for n in ns]
        ps = ns
        tinvs = [eye2 + n for n in ns]
        for _ in range(int(math.log2(C)) - 1):
            ps = [dot(p.astype(BF16), sp) for p, sp in zip(ps, sps)]
            sps = [stack(p, first_head).astype(BF16) for p in ps]
            tinvs = [t + dot(t.astype(BF16), sp) for t, sp in zip(tinvs, sps)]
            yield
        xs = [dot(t.astype(BF16),
                  stack(jnp.concatenate([rv[rs, ls], rk[rs, ls]], axis=1), first_head2).astype(BF16))
              for t, (rs, ls, _) in zip(tinvs, units)]
        yield
        for x, kq, (rs, ls, cc) in zip(xs, kqs, units):
            ci = r * (TILE // C) + cc
            crow = pl.ds(pl.multiple_of(r0 + cc * C, C), C)
            u_ref[crow, ls] = x[:, :LANES]
            wq_ref[ci, 0:C, ls] = x[:, LANES:].astype(BF16)
            wq_ref[ci, C:2 * C, ls] = qd[rs, ls].astype(BF16)
            a_ref[crow, ls] = (kq[C:2 * C] * decay[rs, ls]).astype(BF16)

    lss = [slice(LANES * p, LANES * (p + 1)) for p in range(npair)]

    def recurrence_stages(r, states):
        pending = None

        def finish(rows, os_):
            for ls, o in zip(lss, os_):
                ms = _dot_split_lhs(o * o, pair_ones, 2) * (1.0 / HEAD_DIM)
                out_ref[rows, ls] = o * lax.rsqrt(ms + EPS) * nw_ref[:, ls] * sz_ref[rows, ls]

        for cc in range(TILE // C):
            c = r * (TILE // C) + cc
            rows = pl.ds(pl.multiple_of(c * C, C), C)
            wqs = [dot(wq_ref[c, :, ls], st.astype(BF16)) for ls, st in zip(lss, states)]
            if pending is not None:
                finish(*pending)
            yield
            v_news = [u_ref[rows, ls] - wq[0:C] for ls, wq in zip(lss, wqs)]
            kvs = [lax.dot_general(kd_ref[rows, ls], v.astype(BF16), TN, preferred_element_type=F32)
                   for ls, v in zip(lss, v_news)]
            os_ = [wq[C:2 * C] + dot(a_ref[rows, ls], stack(v, first_head).astype(BF16))
                   for ls, wq, v in zip(lss, wqs, v_news)]
            gl_row = pl.ds(pl.multiple_of(c * SUBLANES, SUBLANES), 1)
            states[:] = [st * gl_ref[gl_row, ls] + jnp.where(same_head, kv, 0.0)
                         for ls, st, kv in zip(lss, states, kvs)]
            pending = (rows, os_)
            yield
        finish(*pending)

    ntile = S // TILE
    _interleave(solve_stages(0))

    def tile_body(r, states):
        states = list(states)
        _interleave(recurrence_stages(r - 1, states), solve_stages(r))
        return tuple(states)

    init = tuple(jnp.zeros((LANES, LANES), F32) for _ in range(npair))
    states = list(lax.fori_loop(1, ntile, tile_body, init))
    _interleave(recurrence_stages(ntile - 1, states))


def _gdn(gqkv, sz, gb, nw_row, B, S):
    T = B * S
    W = GDN_GROUP
    ngrp = GDN_WIDTH // W
    nchunk = S // GDN_CHUNK
    col = lambda off: (lambda b, g: (b, off + g))
    ccol = lambda off: (lambda b, g: (0, off + g))
    return pl.pallas_call(
        functools.partial(_gdn_kernel, S=S),
        grid=(B, ngrp),
        in_specs=[
            pl.BlockSpec((S, W), col(0)),
            pl.BlockSpec((S, W), col(ngrp)),
            pl.BlockSpec((S, W), col(2 * ngrp)),
            pl.BlockSpec((S, W), col(0)),
            pl.BlockSpec((S, LANES), lambda b, g: (b, 0)),
            pl.BlockSpec((1, W), ccol(0)),
        ],
        out_specs=pl.BlockSpec((S, W), col(0)),
        out_shape=jax.ShapeDtypeStruct((T, GDN_WIDTH), F32),
        scratch_shapes=[
            pltpu.VMEM((S, W), F32),
            pltpu.VMEM((nchunk, 2 * GDN_CHUNK, W), BF16),
            pltpu.VMEM((S, W), BF16),
            pltpu.VMEM((S, W), BF16),
            pltpu.VMEM((nchunk * SUBLANES, W), F32),
        ],
        compiler_params=pltpu.CompilerParams(
            dimension_semantics=("arbitrary", "arbitrary"), vmem_limit_bytes=VMEM_LIMIT),
        name="gdn",
    )(gqkv, gqkv, gqkv, sz, gb, nw_row)


def _out_mlp_kernel(x_ref, oTa_ref, oTb_ref, og_ref, woa_ref, wog_ref, pmn_ref, pre_ref, post_ref,
                    wup_ref, wdn_ref, out_ref):
    oT = jnp.concatenate([oTa_ref[0, 0], oTb_ref[0, 0]], axis=1)
    o_att = oT.T.astype(BF16)
    mix = jnp.dot(o_att, woa_ref[...], preferred_element_type=F32)
    mix = mix + jnp.dot(og_ref[...].astype(BF16), wog_ref[...], preferred_element_type=F32)
    x1 = x_ref[...] + _rms(mix, pmn_ref[...])
    h = _rms(x1, pre_ref[...]).astype(BF16)
    acc = jnp.zeros((ROW_TILE, D_MODEL), F32)
    for c in range(D_FF // FF_TILE):
        up = jnp.dot(h, wup_ref[:, c * FF_TILE:(c + 1) * FF_TILE], preferred_element_type=F32)
        act = jnp.square(jnp.maximum(up, 0.0)).astype(BF16)
        acc = acc + jnp.dot(act, wdn_ref[c * FF_TILE:(c + 1) * FF_TILE, :], preferred_element_type=F32)
    out_ref[...] = x1 + _rms(acc, post_ref[...])


def _out_mlp(xf, oT, og, woa, wog, pmn, pre, post, wup, wdn, B, S):
    T = B * S
    nblk = S // MOBA_BLOCK
    tiles_per_seq = S // ROW_TILE
    assert ROW_TILE == 2 * MOBA_BLOCK
    const = lambda i: (0, 0)
    row = lambda i: (i, 0)

    def att_block(which):
        def index(i):
            blk = 2 * (i % tiles_per_seq) + which
            return (i // tiles_per_seq, _paired_pos(blk, nblk), 0, 0)
        return index

    single = dict(pipeline_mode=pl.Buffered(1))
    return pl.pallas_call(
        _out_mlp_kernel,
        grid=(T // ROW_TILE,),
        in_specs=[
            pl.BlockSpec((ROW_TILE, D_MODEL), row),
            pl.BlockSpec((1, 1, ATT_WIDTH, MOBA_BLOCK), att_block(0)),
            pl.BlockSpec((1, 1, ATT_WIDTH, MOBA_BLOCK), att_block(1)),
            pl.BlockSpec((ROW_TILE, GDN_WIDTH), row),
            pl.BlockSpec(woa.shape, const, **single),
            pl.BlockSpec(wog.shape, const, **single),
            pl.BlockSpec((1, D_MODEL), const),
            pl.BlockSpec((1, D_MODEL), const),
            pl.BlockSpec((1, D_MODEL), const),
            pl.BlockSpec(wup.shape, const, **single),
            pl.BlockSpec(wdn.shape, const, **single),
        ],
        out_specs=pl.BlockSpec((ROW_TILE, D_MODEL), row),
        out_shape=jax.ShapeDtypeStruct((T, D_MODEL), F32),
        compiler_params=pltpu.CompilerParams(
            dimension_semantics=("arbitrary",), vmem_limit_bytes=VMEM_LIMIT),
        name="out_mlp",
    )(xf, oT, oT, og, woa, wog, pmn, pre, post, wup, wdn)


def kernel(x, w_in, w_out, conv_w, A_log, dt_bias, gdn_norm_w, rel_bias, pre_mix_norm,
           post_mix_norm, pre_mlp_norm, post_mlp_norm, w_up, w_down):
    B, S, D = x.shape
    assert D == D_MODEL and S % ROW_TILE == 0 and S % MOBA_BLOCK == 0
    T = B * S
    depth = w_in.shape[0]
    xf = x.reshape(T, D)
    o0, o1, o2, o3, o4 = 0, ATT_WIDTH, 2 * ATT_WIDTH, 3 * ATT_WIDTH, 3 * ATT_WIDTH + 3 * GDN_WIDTH
    o5 = o4 + GDN_WIDTH
    for l in range(depth):
        wi = w_in[l]
        wqT = wi[:, o0:o1].T.astype(BF16)
        wk = wi[:, o1:o2].astype(BF16)
        wvT = wi[:, o2:o3].T.astype(BF16)
        wg = wi[:, o3:o4].astype(BF16)
        wz = wi[:, o4:o5].astype(BF16)
        wab = jnp.pad(wi[:, o5:], ((0, 0), (0, LANES - 2 * GDN_HEADS))).astype(BF16)
        pad8 = lambda v: jnp.pad(v.astype(F32), (0, LANES - GDN_HEADS))[None, :]
        qT, k, vT, gqkv, sz, gb = _inproj(xf, pre_mix_norm[l][None, :], wqT, wk, wvT, wg, wz, wab,
                                          conv_w[l], pad8(A_log[l]), pad8(dt_bias[l]), B, S)
        oT = _attention(rel_bias.astype(F32), qT, k.reshape(B, S, ATT_WIDTH), vT, B, S)
        og = _gdn(gqkv, sz, gb, jnp.tile(gdn_norm_w[l], GDN_HEADS)[None, :], B, S)
        wo = w_out[l].astype(BF16)
        xf = _out_mlp(xf, oT, og, wo[:ATT_WIDTH], wo[ATT_WIDTH:], post_mix_norm[l][None, :],
                      pre_mlp_norm[l][None, :], post_mlp_norm[l][None, :],
                      w_up[l].astype(BF16), w_down[l].astype(BF16), B, S)
    return xf.reshape(B, S, D)
```

```python
import functools
import math

import jax
import jax.numpy as jnp
from jax import lax
from jax.experimental import pallas as pl
from jax.experimental.pallas import tpu as pltpu

F32 = jnp.float32
BF16 = jnp.bfloat16
HI = lax.Precision.HIGHEST

D_MODEL = 1024
HEAD_DIM = 64
ATT_HEADS = 8
GDN_HEADS = 8
ATT_WIDTH = ATT_HEADS * HEAD_DIM
GDN_WIDTH = GDN_HEADS * HEAD_DIM
MOBA_BLOCK = 256
MOBA_TOPK = 3
GDN_CHUNK = 64
CONV_WIDTH = 4
D_FF = 4 * D_MODEL
REL_BUCKETS = 32
REL_MAX_EXACT = 16
REL_MAX_DIST = 128
EPS = 1e-6
NEG = -1e30
LOG2E = math.log2(math.e)

LANES = 128
SUBLANES = 8
VMEM_LIMIT = 56 * 1024 * 1024
ROW_TILE = 512
FF_TILE = 1024

NT = (((1,), (1,)), ((), ()))
TN = (((0,), (0,)), ((), ()))


def _bucket_lower_bounds():
    def bucket(d):
        if d < REL_MAX_EXACT:
            return d
        t = math.log(d / REL_MAX_EXACT) / math.log(REL_MAX_DIST / REL_MAX_EXACT)
        t = t * (REL_BUCKETS - REL_MAX_EXACT)
        assert d in (REL_MAX_EXACT, REL_MAX_DIST) or abs(t - round(t)) > 1e-6
        return min(REL_MAX_EXACT + int(t + 1e-9), REL_BUCKETS - 1)
    lower = []
    for b in range(REL_BUCKETS):
        d = 0
        while bucket(d) < b:
            d += 1
        lower.append(d)
    return lower


BUCKET_LOWER = _bucket_lower_bounds()


def _sigmoid(x):
    return 0.5 * jnp.tanh(0.5 * x) + 0.5


def _interleave(*stage_generators):
    live = list(stage_generators)
    while live:
        for gen in list(live):
            try:
                next(gen)
            except StopIteration:
                live.remove(gen)


def _rms(x, w):
    return x * lax.rsqrt(jnp.mean(x * x, axis=-1, keepdims=True) + EPS) * w


def _split_bf16(x, parts):
    out = []
    for _ in range(parts):
        h = x.astype(BF16)
        out.append(h)
        x = x - h.astype(F32)
    return out


def _dot_split_lhs(x, c, parts):
    acc = None
    for h in _split_bf16(x, parts):
        d = jnp.dot(h, c, preferred_element_type=F32)
        acc = d if acc is None else acc + d
    return acc


def _dot_split_rhs(c, x, parts):
    acc = None
    for h in _split_bf16(x, parts):
        d = jnp.dot(c, h, preferred_element_type=F32)
        acc = d if acc is None else acc + d
    return acc


CONV_COLS = 512


def _inproj_kernel(x_ref, xp_ref, nw_ref, wqT_ref, wk_ref, wvT_ref, wg_ref, wz_ref, wab_ref,
                   cw_ref, alog_ref, dtb_ref,
                   qT_ref, k_ref, vT_ref, g_ref, z_ref, gb_ref, *, tiles_per_seq):
    h = _rms(x_ref[...], nw_ref[...]).astype(BF16)

    hp = _rms(xp_ref[...], nw_ref[...]).astype(BF16)
    seq_start = (pl.program_id(0) % tiles_per_seq) == 0
    trow8 = lax.broadcasted_iota(jnp.int32, (SUBLANES, CONV_COLS), 0)
    for c in range(3 * GDN_WIDTH // CONV_COLS):
        cols = slice(c * CONV_COLS, (c + 1) * CONV_COLS)
        cur = jnp.dot(h, wg_ref[:, cols], preferred_element_type=F32)
        prev8 = jnp.dot(hp, wg_ref[:, cols], preferred_element_type=F32)
        prev8 = jnp.where(seq_start, 0.0, prev8)
        acc = cur * cw_ref[CONV_WIDTH - 1:CONV_WIDTH, cols]
        for s in range(1, CONV_WIDTH):
            rolled = pltpu.roll(cur, s, 0)
            top = jnp.where(trow8 < s, pltpu.roll(prev8, s, 0), rolled[0:SUBLANES])
            tap = jnp.concatenate([top, rolled[SUBLANES:]], axis=0)
            acc = acc + tap * cw_ref[CONV_WIDTH - 1 - s:CONV_WIDTH - s, cols]
        g_ref[:, cols] = acc * _sigmoid(acc)

    z = jnp.dot(h, wz_ref[...], preferred_element_type=F32)
    z_ref[...] = z * _sigmoid(z)
    ab = jnp.dot(h, wab_ref[...], preferred_element_type=F32)
    xs = ab + dtb_ref[...]
    log_decay = -jnp.exp(alog_ref[...]) * (jnp.maximum(xs, 0.0) + jnp.log1p(jnp.exp(-jnp.abs(xs))))
    lane = lax.broadcasted_iota(jnp.int32, ab.shape, 1)
    gb_ref[...] = jnp.where(lane < GDN_HEADS, log_decay, _sigmoid(ab))

    qT = lax.dot_general(wqT_ref[...], h, NT, preferred_element_type=F32)
    vT = lax.dot_general(wvT_ref[...], h, NT, preferred_element_type=F32)
    for t in range(ROW_TILE // MOBA_BLOCK):
        qT_ref[0, t] = qT[:, t * MOBA_BLOCK:(t + 1) * MOBA_BLOCK]
        vT_ref[0, t] = vT[:, t * MOBA_BLOCK:(t + 1) * MOBA_BLOCK].astype(BF16)
    k_ref[...] = jnp.dot(h, wk_ref[...], preferred_element_type=F32)


def _inproj(xf, nw, wqT, wk, wvT, wg, wz, wab, conv_w, alog_pad, dtb_pad, B, S):
    T = B * S
    nblk = S // MOBA_BLOCK
    tiles_per_seq = S // ROW_TILE
    blk_per_tile = ROW_TILE // MOBA_BLOCK
    const = lambda i: (0, 0)
    row = lambda i: (i, 0)
    tr = lambda i: (i // tiles_per_seq, i % tiles_per_seq, 0, 0)
    prev_rows = lambda i: (jnp.maximum(i * (ROW_TILE // SUBLANES) - 1, 0), 0)
    return pl.pallas_call(
        functools.partial(_inproj_kernel, tiles_per_seq=tiles_per_seq),
        grid=(T // ROW_TILE,),
        in_specs=[
            pl.BlockSpec((ROW_TILE, D_MODEL), row),
            pl.BlockSpec((SUBLANES, D_MODEL), prev_rows),
            pl.BlockSpec((1, D_MODEL), const),
            pl.BlockSpec(wqT.shape, const),
            pl.BlockSpec(wk.shape, const),
            pl.BlockSpec(wvT.shape, const),
            pl.BlockSpec(wg.shape, const),
            pl.BlockSpec(wz.shape, const),
            pl.BlockSpec(wab.shape, const),
            pl.BlockSpec(conv_w.shape, const),
            pl.BlockSpec((1, LANES), const),
            pl.BlockSpec((1, LANES), const),
        ],
        out_specs=[
            pl.BlockSpec((1, blk_per_tile, ATT_WIDTH, MOBA_BLOCK), tr),
            pl.BlockSpec((ROW_TILE, ATT_WIDTH), row),
            pl.BlockSpec((1, blk_per_tile, ATT_WIDTH, MOBA_BLOCK), tr),
            pl.BlockSpec((ROW_TILE, 3 * GDN_WIDTH), row),
            pl.BlockSpec((ROW_TILE, GDN_WIDTH), row),
            pl.BlockSpec((ROW_TILE, LANES), row),
        ],
        out_shape=[
            jax.ShapeDtypeStruct((B, nblk, ATT_WIDTH, MOBA_BLOCK), F32),
            jax.ShapeDtypeStruct((T, ATT_WIDTH), F32),
            jax.ShapeDtypeStruct((B, nblk, ATT_WIDTH, MOBA_BLOCK), BF16),
            jax.ShapeDtypeStruct((T, 3 * GDN_WIDTH), F32),
            jax.ShapeDtypeStruct((T, GDN_WIDTH), F32),
            jax.ShapeDtypeStruct((T, LANES), F32),
        ],
        compiler_params=pltpu.CompilerParams(
            dimension_semantics=("arbitrary",), vmem_limit_bytes=VMEM_LIMIT),
        name="inproj",
    )(xf, xf, nw, wqT, wk, wvT, wg, wz, wab, conv_w, alog_pad, dtb_pad)


V_ROWS = HEAD_DIM + 16


def _attn_kernel(relb_ref, qlo_ref, qhi_ref, k_ref, vT_ref, oT_ref,
                 kb_ref, km_ref, va_ref, bias_ref, addm_ref, qh_ref, lg_ref, *, nblk):
    hp = pl.program_id(0)
    b = pl.program_id(1)
    t = pl.program_id(2)
    BLK = MOBA_BLOCK

    @pl.when((b == 0) & (t == 0))
    def _():
        kk = lax.broadcasted_iota(jnp.int32, (BLK, BLK), 0)
        qq = lax.broadcasted_iota(jnp.int32, (BLK, BLK), 1)
        for hh in range(2):
            h = 2 * hp + hh
            for kind in range(2):
                d = qq - kk + kind * BLK
                val = jnp.full((BLK, BLK), relb_ref[h, REL_BUCKETS - 1], F32)
                for bkt in range(REL_BUCKETS - 2, -1, -1):
                    val = jnp.where(d < BUCKET_LOWER[bkt + 1], relb_ref[h, bkt], val)
                val = val * LOG2E
                if kind == 0:
                    val = jnp.where(d >= 0, val, NEG)
                bias_ref[hh, kind] = val
            bias_ref[hh, 2] = jnp.full((BLK, BLK), relb_ref[h, REL_BUCKETS - 1] * LOG2E, F32)

    @pl.when(t == 0)
    def _():
        lane = lax.broadcasted_iota(jnp.int32, (1, LANES), 1)
        ones_row = jnp.where(lax.broadcasted_iota(jnp.int32, (V_ROWS - HEAD_DIM, BLK), 0) == 0,
                             1.0, 0.0).astype(BF16)
        for j in range(nblk):
            kj = k_ref[0, j * BLK:(j + 1) * BLK, :]
            kb_ref[j * BLK:(j + 1) * BLK, :] = kj.astype(BF16)
            kmj = jnp.sum(kj, axis=0, keepdims=True) * (1.0 / BLK)
            km_ref[j:j + 1, :] = jnp.where(lane < HEAD_DIM, kmj, 0.0)
            km_ref[nblk + j:nblk + j + 1, :] = jnp.where(lane >= HEAD_DIM, kmj, 0.0)
            for hh in range(2):
                va_ref[j, hh, 0:HEAD_DIM, :] = vT_ref[0, j, HEAD_DIM * hh:HEAD_DIM * (hh + 1), :]
                va_ref[j, hh, HEAD_DIM:V_ROWS, :] = ones_row

    ridx = lax.broadcasted_iota(jnp.int32, (nblk, BLK), 0)
    sub = lax.broadcasted_iota(jnp.int32, (LANES, BLK), 0)
    scale = HEAD_DIM ** -0.5 * LOG2E
    q_blocks = (t, nblk - 1 - t)
    for s, (q_ref, qi) in enumerate(zip((qlo_ref, qhi_ref), q_blocks)):
        qT = q_ref[0, 0]
        gT = jnp.dot(km_ref[...], qT, precision=HI, preferred_element_type=F32)
        past = ridx < qi
        for hh in range(2):
            gm = jnp.where(past, gT[nblk * hh:nblk * (hh + 1)], -jnp.inf)
            cnt = jnp.zeros((nblk, BLK), F32)
            for jp in range(nblk):
                row = gm[jp:jp + 1, :]
                beats = (row > gm) | ((row == gm) & (ridx > jp))
                cnt = cnt + jnp.where(beats, 1.0, 0.0)
            visible = (past & (cnt < MOBA_TOPK)) | (ridx == qi)
            addm_ref[s, nblk * hh:nblk * (hh + 1), :] = jnp.where(visible, 0.0, NEG)
            in_head = (sub >= HEAD_DIM * hh) & (sub < HEAD_DIM * (hh + 1))
            qh_ref[s, hh] = jnp.where(in_head, qT * scale, 0.0).astype(BF16)

    i_hi = nblk - 1 - t
    has_lo_prev = t >= 1
    n_lo_far = jnp.maximum(t - 1, 0)
    tiles = [(0, t, 0, False), (1, i_hi, 0, False), (1, i_hi - 1, 1, False),
             (jnp.where(has_lo_prev, 0, 1), jnp.where(has_lo_prev, t - 1, i_hi - 2),
              jnp.where(has_lo_prev, 1, 2), False)]
    for f in range(nblk - 3):
        f_lo = f < n_lo_far
        tiles.append((jnp.where(f_lo, 0, 1), jnp.where(f_lo, f, f - n_lo_far), None, True))
    ntile = len(tiles)

    cmax = [[None] * ntile for _ in range(2)]
    rowoff = [[None] * ntile for _ in range(2)]
    for hh in range(2):
        for n, (slot, kblk, bias_kind, far) in enumerate(tiles):
            kj = kb_ref[pl.ds(pl.multiple_of(kblk * BLK, BLK), BLK), :]
            lg = jnp.dot(kj, qh_ref[slot, hh], preferred_element_type=F32)
            if bias_kind is not None:
                lg = lg + bias_ref[hh, bias_kind]
            lg_ref[hh, n] = lg
            cm = jnp.max(lg, axis=0, keepdims=True)
            if n >= 2:
                off = addm_ref[slot, pl.ds(nblk * hh + kblk, 1), :]
                if far:
                    off = off + relb_ref[2 * hp + hh, REL_BUCKETS - 1] * LOG2E
                rowoff[hh][n] = off
                cm = cm + off
            cmax[hh][n] = cm

    def to_lo(n, x, other):
        slot = tiles[n][0]
        if isinstance(slot, int):
            return x if slot == 0 else other
        return jnp.where(slot == 0, x, other)

    for hh in range(2):
        m_lo = cmax[hh][0]
        m_hi = cmax[hh][1]
        for n in range(2, ntile):
            m_lo = jnp.maximum(m_lo, to_lo(n, cmax[hh][n], -jnp.inf))
            m_hi = jnp.maximum(m_hi, to_lo(n, -jnp.inf, cmax[hh][n]))
        acc_lo = None
        acc_hi = None
        for n, (slot, kblk, _, _) in enumerate(tiles):
            m_n = to_lo(n, m_lo, m_hi)
            if rowoff[hh][n] is not None:
                m_n = m_n - rowoff[hh][n]
            p = jnp.exp2(lg_ref[hh, n] - m_n)
            pvn = jnp.dot(va_ref[kblk, hh], p.astype(BF16), preferred_element_type=F32)
            if isinstance(slot, int):
                if slot == 0:
                    acc_lo = pvn if acc_lo is None else acc_lo + pvn
                else:
                    acc_hi = pvn if acc_hi is None else acc_hi + pvn
            else:
                acc_lo = acc_lo + jnp.where(slot == 0, pvn, 0.0)
                acc_hi = acc_hi + jnp.where(slot == 0, 0.0, pvn)
        oT_ref[0, 0, HEAD_DIM * hh:HEAD_DIM * (hh + 1), :] = (
            acc_lo[0:HEAD_DIM] / acc_lo[HEAD_DIM:HEAD_DIM + 1])
        oT_ref[0, 1, HEAD_DIM * hh:HEAD_DIM * (hh + 1), :] = (
            acc_hi[0:HEAD_DIM] / acc_hi[HEAD_DIM:HEAD_DIM + 1])


def _paired_pos(i, nblk):
    return jnp.where(i < nblk // 2, 2 * i, 2 * (nblk - 1 - i) + 1)


def _attention(rel_bias, qT, k3, vT, B, S):
    nblk = S // MOBA_BLOCK
    assert nblk % 2 == 0 and nblk >= 4
    assert BUCKET_LOWER[REL_BUCKETS - 1] <= MOBA_BLOCK + 1
    return pl.pallas_call(
        functools.partial(_attn_kernel, nblk=nblk),
        grid=(ATT_HEADS // 2, B, nblk // 2),
        in_specs=[
            pl.BlockSpec(memory_space=pltpu.SMEM),
            pl.BlockSpec((1, 1, LANES, MOBA_BLOCK), lambda hp, b, t: (b, t, hp, 0)),
            pl.BlockSpec((1, 1, LANES, MOBA_BLOCK), lambda hp, b, t: (b, nblk - 1 - t, hp, 0)),
            pl.BlockSpec((1, S, LANES), lambda hp, b, t: (b, 0, hp)),
            pl.BlockSpec((1, nblk, LANES, MOBA_BLOCK), lambda hp, b, t: (b, 0, hp, 0)),
        ],
        out_specs=pl.BlockSpec((1, 2, LANES, MOBA_BLOCK), lambda hp, b, t: (b, t, hp, 0)),
        out_shape=jax.ShapeDtypeStruct((B, nblk, ATT_WIDTH, MOBA_BLOCK), F32),
        scratch_shapes=[
            pltpu.VMEM((S, LANES), BF16),
            pltpu.VMEM((2 * nblk, LANES), F32),
            pltpu.VMEM((nblk, 2, V_ROWS, MOBA_BLOCK), BF16),
            pltpu.VMEM((2, 3, MOBA_BLOCK, MOBA_BLOCK), F32),
            pltpu.VMEM((2, 2 * nblk, MOBA_BLOCK), F32),
            pltpu.VMEM((2, 2, LANES, MOBA_BLOCK), BF16),
            pltpu.VMEM((2, nblk + 1, MOBA_BLOCK, MOBA_BLOCK), F32),
        ],
        compiler_params=pltpu.CompilerParams(
            dimension_semantics=("arbitrary", "arbitrary", "arbitrary"),
            vmem_limit_bytes=VMEM_LIMIT),
        name="moba_attn",
    )(rel_bias, qT, qT, k3, vT)


GDN_GROUP = 4 * HEAD_DIM
GDN_TILE = 256


def _gdn_kernel(yq_ref, yk_ref, yv_ref, sz_ref, gb_ref, nw_ref, out_ref,
                u_ref, wq_ref, a_ref, kd_ref, gl_ref, *, S):
    grp = pl.program_id(1)
    C = GDN_CHUNK
    W = GDN_GROUP
    TILE = GDN_TILE
    npair = W // LANES

    r_w = lax.broadcasted_iota(jnp.int32, (W, W), 0)
    c_w = lax.broadcasted_iota(jnp.int32, (W, W), 1)
    head_ones = jnp.where((r_w // HEAD_DIM) == (c_w // HEAD_DIM), 1.0, 0.0).astype(BF16)
    ltri_bd = jnp.where(((r_w // C) == (c_w // C)) & (c_w <= r_w), 1.0, 0.0).astype(BF16)
    r_e = lax.broadcasted_iota(jnp.int32, (LANES, W), 0)
    c_e = lax.broadcasted_iota(jnp.int32, (LANES, W), 1)
    head_of_col = grp * (W // HEAD_DIM) + c_e // HEAD_DIM
    sel_g = jnp.where(r_e == head_of_col, 1.0, 0.0).astype(BF16)
    sel_beta = jnp.where(r_e == GDN_HEADS + head_of_col, 1.0, 0.0).astype(BF16)
    tok = lax.broadcasted_iota(jnp.int32, (TILE, W), 0) % C
    col = lax.broadcasted_iota(jnp.int32, (TILE, W), 1) % HEAD_DIM
    causal_t = tok >= col
    strict_t = tok > col

    lane = lax.broadcasted_iota(jnp.int32, (C, LANES), 1)
    rowi = lax.broadcasted_iota(jnp.int32, (C, LANES), 0)
    first_head = lane < HEAD_DIM
    strict = rowi > (lane % HEAD_DIM)
    lane2 = lax.broadcasted_iota(jnp.int32, (C, 2 * LANES), 1)
    first_head2 = (lane2 % LANES) < HEAD_DIM
    r_l = lax.broadcasted_iota(jnp.int32, (LANES, LANES), 0)
    c_l = lax.broadcasted_iota(jnp.int32, (LANES, LANES), 1)
    same_head = (r_l // HEAD_DIM) == (c_l // HEAD_DIM)
    pair_ones = jnp.where(same_head, 1.0, 0.0).astype(BF16)

    def stack(x, mask):
        return jnp.concatenate([jnp.where(mask, x, 0.0), jnp.where(mask, 0.0, x)], axis=0)

    dot = functools.partial(jnp.dot, preferred_element_type=F32)

    def solve_stages(r):
        yield
        r0 = pl.multiple_of(r * TILE, TILE)
        yq = yq_ref[pl.ds(r0, TILE), :]
        yk = yk_ref[pl.ds(r0, TILE), :]
        yv = yv_ref[pl.ds(r0, TILE), :]
        ssq = _dot_split_lhs(yq * yq, head_ones, 1)
        ssk = _dot_split_lhs(yk * yk, head_ones, 1)
        qn = yq * lax.rsqrt(ssq + EPS) * (HEAD_DIM ** -0.5)
        kn = yk * lax.rsqrt(ssk + EPS)
        gb2 = _split_bf16(gb_ref[pl.ds(r0, TILE), :], 2)
        g = sum(dot(h, sel_g) for h in gb2)
        beta = sum(dot(h, sel_beta) for h in gb2)
        gcd = _dot_split_rhs(ltri_bd, jnp.concatenate([g, jnp.where(strict_t, g, 0.0)], axis=1), 2)
        gc = gcd[:, :W]
        decay = jnp.where(causal_t, jnp.exp(jnp.where(causal_t, gcd[:, W:], 0.0)), 0.0)
        egc = jnp.exp(gc)
        kb = kn * beta
        rv = yv * beta
        rk = kb * egc
        qd = qn * egc
        for cc in range(TILE // C):
            rs = slice(cc * C, (cc + 1) * C)
            ci = r * (TILE // C) + cc
            g_last = gc[(cc + 1) * C - 1:(cc + 1) * C, :]
            kd_ref[pl.ds(pl.multiple_of(r0 + cc * C, C), C), :] = (
                kn[rs] * jnp.exp(g_last - gc[rs])).astype(BF16)
            gl_ref[pl.ds(pl.multiple_of(ci * SUBLANES, SUBLANES), SUBLANES), :] = (
                jnp.broadcast_to(jnp.exp(g_last), (SUBLANES, W)))
        units = [(slice(cc * C, (cc + 1) * C), slice(LANES * p, LANES * (p + 1)), cc)
                 for cc in range(TILE // C) for p in range(npair)]
        kqs = [lax.dot_general(jnp.concatenate([kn[rs, ls], qn[rs, ls]], axis=0).astype(BF16),
                               stack(kn[rs, ls], first_head).astype(BF16), NT,
                               preferred_element_type=F32) for rs, ls, _ in units]
        yield
        ps = [-jnp.where(strict, kq[0:C] * beta[rs, ls] * decay[rs, ls], 0.0)
              for kq, (rs, ls, _) in zip(kqs, units)]
        xs = [jnp.concatenate([rv[rs, ls], rk[rs, ls]], axis=1) for rs, ls, _ in units]
        nround = int(math.log2(C))
        for k in range(nround):
            rhs = [stack(x, first_head2).astype(BF16) for x in xs]
            if k + 1 < nround:
                rhs = [jnp.concatenate([stack(p, first_head).astype(BF16), sx], axis=1)
                       for p, sx in zip(ps, rhs)]
            outs = [dot(p.astype(BF16), sx) for p, sx in zip(ps, rhs)]
            if k + 1 < nround:
                ps = [o[:, :LANES] for o in outs]
                xs = [x + o[:, LANES:] for x, o in zip(xs, outs)]
            else:
                xs = [x + o for x, o in zip(xs, outs)]
            yield
        for x, kq, (rs, ls, cc) in zip(xs, kqs, units):
            ci = r * (TILE // C) + cc
            crow = pl.ds(pl.multiple_of(r0 + cc * C, C), C)
            u_ref[crow, ls] = x[:, :LANES]
            wq_ref[ci, 0:C, ls] = x[:, LANES:].astype(BF16)
            wq_ref[ci, C:2 * C, ls] = qd[rs, ls].astype(BF16)
            a_ref[crow, ls] = (kq[C:2 * C] * decay[rs, ls]).astype(BF16)

    lss = [slice(LANES * p, LANES * (p + 1)) for p in range(npair)]

    def recurrence_stages(r, states):
        pending = None

        def finish(rows, os_):
            for ls, o in zip(lss, os_):
                ms = _dot_split_lhs(o * o, pair_ones, 1) * (1.0 / HEAD_DIM)
                out_ref[rows, ls] = o * lax.rsqrt(ms + EPS) * nw_ref[:, ls] * sz_ref[rows, ls]

        for cc in range(TILE // C):
            c = r * (TILE // C) + cc
            rows = pl.ds(pl.multiple_of(c * C, C), C)
            wqs = [dot(wq_ref[c, :, ls], st.astype(BF16)) for ls, st in zip(lss, states)]
            if pending is not None:
                finish(*pending)
            yield
            v_news = [u_ref[rows, ls] - wq[0:C] for ls, wq in zip(lss, wqs)]
            kvs = [lax.dot_general(kd_ref[rows, ls], v.astype(BF16), TN, preferred_element_type=F32)
                   for ls, v in zip(lss, v_news)]
            os_ = [wq[C:2 * C] + dot(a_ref[rows, ls], stack(v, first_head).astype(BF16))
                   for ls, wq, v in zip(lss, wqs, v_news)]
            gl_row = pl.ds(pl.multiple_of(c * SUBLANES, SUBLANES), 1)
            states[:] = [st * gl_ref[gl_row, ls] + jnp.where(same_head, kv, 0.0)
                         for ls, st, kv in zip(lss, states, kvs)]
            pending = (rows, os_)
            yield
        finish(*pending)

    ntile = S // TILE
    _interleave(solve_stages(0))

    def tile_body(r, states):
        states = list(states)
        _interleave(recurrence_stages(r - 1, states), solve_stages(r))
        return tuple(states)

    init = tuple(jnp.zeros((LANES, LANES), F32) for _ in range(npair))
    states = list(lax.fori_loop(1, ntile, tile_body, init))
    _interleave(recurrence_stages(ntile - 1, states))


def _gdn(gqkv, sz, gb, nw_row, B, S):
    T = B * S
    W = GDN_GROUP
    ngrp = GDN_WIDTH // W
    nchunk = S // GDN_CHUNK
    col = lambda off: (lambda b, g: (b, off + g))
    ccol = lambda off: (lambda b, g: (0, off + g))
    return pl.pallas_call(
        functools.partial(_gdn_kernel, S=S),
        grid=(B, ngrp),
        in_specs=[
            pl.BlockSpec((S, W), col(0)),
            pl.BlockSpec((S, W), col(ngrp)),
            pl.BlockSpec((S, W), col(2 * ngrp)),
            pl.BlockSpec((S, W), col(0)),
            pl.BlockSpec((S, LANES), lambda b, g: (b, 0)),
            pl.BlockSpec((1, W), ccol(0)),
        ],
        out_specs=pl.BlockSpec((S, W), col(0)),
        out_shape=jax.ShapeDtypeStruct((T, GDN_WIDTH), F32),
        scratch_shapes=[
            pltpu.VMEM((S, W), F32),
            pltpu.VMEM((nchunk, 2 * GDN_CHUNK, W), BF16),
            pltpu.VMEM((S, W), BF16),
            pltpu.VMEM((S, W), BF16),
            pltpu.VMEM((nchunk * SUBLANES, W), F32),
        ],
        compiler_params=pltpu.CompilerParams(
            dimension_semantics=("arbitrary", "arbitrary"), vmem_limit_bytes=VMEM_LIMIT),
        name="gdn",
    )(gqkv, gqkv, gqkv, sz, gb, nw_row)


def _out_mlp_kernel(x_ref, oTa_ref, oTb_ref, og_ref, woa_ref, wog_ref, pmn_ref, pre_ref, post_ref,
                    wup_ref, wdn_ref, out_ref):
    oT = jnp.concatenate([oTa_ref[0, 0], oTb_ref[0, 0]], axis=1)
    o_att = oT.T.astype(BF16)
    mix = jnp.dot(o_att, woa_ref[...], preferred_element_type=F32)
    mix = mix + jnp.dot(og_ref[...].astype(BF16), wog_ref[...], preferred_element_type=F32)
    x1 = x_ref[...] + _rms(mix, pmn_ref[...])
    h = _rms(x1, pre_ref[...]).astype(BF16)
    acc = jnp.zeros((ROW_TILE, D_MODEL), F32)
    for c in range(D_FF // FF_TILE):
        up = jnp.dot(h, wup_ref[:, c * FF_TILE:(c + 1) * FF_TILE], preferred_element_type=F32)
        act = jnp.square(jnp.maximum(up, 0.0)).astype(BF16)
        acc = acc + jnp.dot(act, wdn_ref[c * FF_TILE:(c + 1) * FF_TILE, :], preferred_element_type=F32)
    out_ref[...] = x1 + _rms(acc, post_ref[...])


def _out_mlp(xf, oT, og, woa, wog, pmn, pre, post, wup, wdn, B, S):
    T = B * S
    nblk = S // MOBA_BLOCK
    tiles_per_seq = S // ROW_TILE
    assert ROW_TILE == 2 * MOBA_BLOCK
    const = lambda i: (0, 0)
    row = lambda i: (i, 0)

    def att_block(which):
        def index(i):
            blk = 2 * (i % tiles_per_seq) + which
            return (i // tiles_per_seq, _paired_pos(blk, nblk), 0, 0)
        return index

    single = dict(pipeline_mode=pl.Buffered(1))
    return pl.pallas_call(
        _out_mlp_kernel,
        grid=(T // ROW_TILE,),
        in_specs=[
            pl.BlockSpec((ROW_TILE, D_MODEL), row),
            pl.BlockSpec((1, 1, ATT_WIDTH, MOBA_BLOCK), att_block(0)),
            pl.BlockSpec((1, 1, ATT_WIDTH, MOBA_BLOCK), att_block(1)),
            pl.BlockSpec((ROW_TILE, GDN_WIDTH), row),
            pl.BlockSpec(woa.shape, const, **single),
            pl.BlockSpec(wog.shape, const, **single),
            pl.BlockSpec((1, D_MODEL), const),
            pl.BlockSpec((1, D_MODEL), const),
            pl.BlockSpec((1, D_MODEL), const),
            pl.BlockSpec(wup.shape, const, **single),
            pl.BlockSpec(wdn.shape, const, **single),
        ],
        out_specs=pl.BlockSpec((ROW_TILE, D_MODEL), row),
        out_shape=jax.ShapeDtypeStruct((T, D_MODEL), F32),
        compiler_params=pltpu.CompilerParams(
            dimension_semantics=("arbitrary",), vmem_limit_bytes=VMEM_LIMIT),
        name="out_mlp",
    )(xf, oT, oT, og, woa, wog, pmn, pre, post, wup, wdn)


def kernel(x, w_in, w_out, conv_w, A_log, dt_bias, gdn_norm_w, rel_bias, pre_mix_norm,
           post_mix_norm, pre_mlp_norm, post_mlp_norm, w_up, w_down):
    B, S, D = x.shape
    assert D == D_MODEL and S % ROW_TILE == 0 and S % MOBA_BLOCK == 0
    T = B * S
    depth = w_in.shape[0]
    xf = x.reshape(T, D)
    o0, o1, o2, o3, o4 = 0, ATT_WIDTH, 2 * ATT_WIDTH, 3 * ATT_WIDTH, 3 * ATT_WIDTH + 3 * GDN_WIDTH
    o5 = o4 + GDN_WIDTH
    for l in range(depth):
        wi = w_in[l]
        wqT = wi[:, o0:o1].T.astype(BF16)
        wk = wi[:, o1:o2].astype(BF16)
        wvT = wi[:, o2:o3].T.astype(BF16)
        wg = wi[:, o3:o4].astype(BF16)
        wz = wi[:, o4:o5].astype(BF16)
        wab = jnp.pad(wi[:, o5:], ((0, 0), (0, LANES - 2 * GDN_HEADS))).astype(BF16)
        pad8 = lambda v: jnp.pad(v.astype(F32), (0, LANES - GDN_HEADS))[None, :]
        qT, k, vT, gqkv, sz, gb = _inproj(xf, pre_mix_norm[l][None, :], wqT, wk, wvT, wg, wz, wab,
                                          conv_w[l], pad8(A_log[l]), pad8(dt_bias[l]), B, S)
        oT = _attention(rel_bias.astype(F32), qT, k.reshape(B, S, ATT_WIDTH), vT, B, S)
        og = _gdn(gqkv, sz, gb, jnp.tile(gdn_norm_w[l], GDN_HEADS)[None, :], B, S)
        wo = w_out[l].astype(BF16)
        xf = _out_mlp(xf, oT, og, wo[:ATT_WIDTH], wo[ATT_WIDTH:], post_mix_norm[l][None, :],
                      pre_mlp_norm[l][None, :], post_mlp_norm[l][None, :],
                      w_up[l].astype(BF16), w_down[l].astype(BF16), B, S)
    return xf.reshape(B, S, D)
```

```python
import functools
import math

import jax
import jax.numpy as jnp
from jax import lax
from jax.experimental import pallas as pl
from jax.experimental.pallas import tpu as pltpu

F32 = jnp.float32
BF16 = jnp.bfloat16
HI = lax.Precision.HIGHEST

D_MODEL = 1024
HEAD_DIM = 64
ATT_HEADS = 8
GDN_HEADS = 8
ATT_WIDTH = ATT_HEADS * HEAD_DIM
GDN_WIDTH = GDN_HEADS * HEAD_DIM
MOBA_BLOCK = 256
MOBA_TOPK = 3
GDN_CHUNK = 64
CONV_WIDTH = 4
D_FF = 4 * D_MODEL
REL_BUCKETS = 32
REL_MAX_EXACT = 16
REL_MAX_DIST = 128
EPS = 1e-6
NEG = -1e30
LOG2E = math.log2(math.e)

LANES = 128
SUBLANES = 8
VMEM_LIMIT = 56 * 1024 * 1024
ROW_TILE = 512
FF_TILE = 1024

NT = (((1,), (1,)), ((), ()))
TN = (((0,), (0,)), ((), ()))


def _bucket_lower_bounds():
    def bucket(d):
        if d < REL_MAX_EXACT:
            return d
        t = math.log(d / REL_MAX_EXACT) / math.log(REL_MAX_DIST / REL_MAX_EXACT)
        t = t * (REL_BUCKETS - REL_MAX_EXACT)
        assert d in (REL_MAX_EXACT, REL_MAX_DIST) or abs(t - round(t)) > 1e-6
        return min(REL_MAX_EXACT + int(t + 1e-9), REL_BUCKETS - 1)
    lower = []
    for b in range(REL_BUCKETS):
        d = 0
        while bucket(d) < b:
            d += 1
        lower.append(d)
    return lower


BUCKET_LOWER = _bucket_lower_bounds()


def _sigmoid(x):
    return 0.5 * jnp.tanh(0.5 * x) + 0.5


def _interleave(*stage_generators):
    live = list(stage_generators)
    while live:
        for gen in list(live):
            try:
                next(gen)
            except StopIteration:
                live.remove(gen)


def _rms(x, w):
    return x * lax.rsqrt(jnp.mean(x * x, axis=-1, keepdims=True) + EPS) * w


def _split_bf16(x, parts):
    out = []
    for _ in range(parts):
        h = x.astype(BF16)
        out.append(h)
        x = x - h.astype(F32)
    return out


def _dot_split_lhs(x, c, parts):
    acc = None
    for h in _split_bf16(x, parts):
        d = jnp.dot(h, c, preferred_element_type=F32)
        acc = d if acc is None else acc + d
    return acc


def _dot_split_rhs(c, x, parts):
    acc = None
    for h in _split_bf16(x, parts):
        d = jnp.dot(c, h, preferred_element_type=F32)
        acc = d if acc is None else acc + d
    return acc


CONV_COLS = 512


def _inproj_kernel(x_ref, xp_ref, nw_ref, wqT_ref, wk_ref, wvT_ref, wg_ref, wz_ref, wab_ref,
                   cw_ref, alog_ref, dtb_ref,
                   qT_ref, k_ref, vT_ref, g_ref, z_ref, gb_ref, *, tiles_per_seq):
    h = _rms(x_ref[...], nw_ref[...]).astype(BF16)

    hp = _rms(xp_ref[...], nw_ref[...]).astype(BF16)
    seq_start = (pl.program_id(0) % tiles_per_seq) == 0
    trow8 = lax.broadcasted_iota(jnp.int32, (SUBLANES, CONV_COLS), 0)
    for c in range(3 * GDN_WIDTH // CONV_COLS):
        cols = slice(c * CONV_COLS, (c + 1) * CONV_COLS)
        cur = jnp.dot(h, wg_ref[:, cols], preferred_element_type=F32)
        prev8 = jnp.dot(hp, wg_ref[:, cols], preferred_element_type=F32)
        prev8 = jnp.where(seq_start, 0.0, prev8)
        acc = cur * cw_ref[CONV_WIDTH - 1:CONV_WIDTH, cols]
        for s in range(1, CONV_WIDTH):
            rolled = pltpu.roll(cur, s, 0)
            top = jnp.where(trow8 < s, pltpu.roll(prev8, s, 0), rolled[0:SUBLANES])
            tap = jnp.concatenate([top, rolled[SUBLANES:]], axis=0)
            acc = acc + tap * cw_ref[CONV_WIDTH - 1 - s:CONV_WIDTH - s, cols]
        g_ref[:, cols] = acc * _sigmoid(acc)

    z = jnp.dot(h, wz_ref[...], preferred_element_type=F32)
    z_ref[...] = z * _sigmoid(z)
    ab = jnp.dot(h, wab_ref[...], preferred_element_type=F32)
    xs = ab + dtb_ref[...]
    log_decay = -jnp.exp(alog_ref[...]) * (jnp.maximum(xs, 0.0) + jnp.log1p(jnp.exp(-jnp.abs(xs))))
    lane = lax.broadcasted_iota(jnp.int32, ab.shape, 1)
    gb_ref[...] = jnp.where(lane < GDN_HEADS, log_decay, _sigmoid(ab))

    qT = lax.dot_general(wqT_ref[...], h, NT, preferred_element_type=F32)
    vT = lax.dot_general(wvT_ref[...], h, NT, preferred_element_type=F32)
    for t in range(ROW_TILE // MOBA_BLOCK):
        qT_ref[0, t] = qT[:, t * MOBA_BLOCK:(t + 1) * MOBA_BLOCK]
        vT_ref[0, t] = vT[:, t * MOBA_BLOCK:(t + 1) * MOBA_BLOCK].astype(BF16)
    k_ref[...] = jnp.dot(h, wk_ref[...], preferred_element_type=F32)


def _inproj(xf, nw, wqT, wk, wvT, wg, wz, wab, conv_w, alog_pad, dtb_pad, B, S):
    T = B * S
    nblk = S // MOBA_BLOCK
    tiles_per_seq = S // ROW_TILE
    blk_per_tile = ROW_TILE // MOBA_BLOCK
    const = lambda i: (0, 0)
    row = lambda i: (i, 0)
    tr = lambda i: (i // tiles_per_seq, i % tiles_per_seq, 0, 0)
    prev_rows = lambda i: (jnp.maximum(i * (ROW_TILE // SUBLANES) - 1, 0), 0)
    return pl.pallas_call(
        functools.partial(_inproj_kernel, tiles_per_seq=tiles_per_seq),
        grid=(T // ROW_TILE,),
        in_specs=[
            pl.BlockSpec((ROW_TILE, D_MODEL), row),
            pl.BlockSpec((SUBLANES, D_MODEL), prev_rows),
            pl.BlockSpec((1, D_MODEL), const),
            pl.BlockSpec(wqT.shape, const),
            pl.BlockSpec(wk.shape, const),
            pl.BlockSpec(wvT.shape, const),
            pl.BlockSpec(wg.shape, const),
            pl.BlockSpec(wz.shape, const),
            pl.BlockSpec(wab.shape, const),
            pl.BlockSpec(conv_w.shape, const),
            pl.BlockSpec((1, LANES), const),
            pl.BlockSpec((1, LANES), const),
        ],
        out_specs=[
            pl.BlockSpec((1, blk_per_tile, ATT_WIDTH, MOBA_BLOCK), tr),
            pl.BlockSpec((ROW_TILE, ATT_WIDTH), row),
            pl.BlockSpec((1, blk_per_tile, ATT_WIDTH, MOBA_BLOCK), tr),
            pl.BlockSpec((ROW_TILE, 3 * GDN_WIDTH), row),
            pl.BlockSpec((ROW_TILE, GDN_WIDTH), row),
            pl.BlockSpec((ROW_TILE, LANES), row),
        ],
        out_shape=[
            jax.ShapeDtypeStruct((B, nblk, ATT_WIDTH, MOBA_BLOCK), F32),
            jax.ShapeDtypeStruct((T, ATT_WIDTH), F32),
            jax.ShapeDtypeStruct((B, nblk, ATT_WIDTH, MOBA_BLOCK), BF16),
            jax.ShapeDtypeStruct((T, 3 * GDN_WIDTH), F32),
            jax.ShapeDtypeStruct((T, GDN_WIDTH), F32),
            jax.ShapeDtypeStruct((T, LANES), F32),
        ],
        compiler_params=pltpu.CompilerParams(
            dimension_semantics=("arbitrary",), vmem_limit_bytes=VMEM_LIMIT),
        name="inproj",
    )(xf, xf, nw, wqT, wk, wvT, wg, wz, wab, conv_w, alog_pad, dtb_pad)


V_ROWS = HEAD_DIM + 16


def _attn_kernel(relb_ref, qlo_ref, qhi_ref, k_ref, vT_ref, oT_ref,
                 kb_ref, km_ref, va_ref, bias_ref, addm_ref, qh_ref, lg_ref, *, nblk):
    hp = pl.program_id(0)
    b = pl.program_id(1)
    t = pl.program_id(2)
    BLK = MOBA_BLOCK

    @pl.when((b == 0) & (t == 0))
    def _():
        kk = lax.broadcasted_iota(jnp.int32, (BLK, BLK), 0)
        qq = lax.broadcasted_iota(jnp.int32, (BLK, BLK), 1)
        for hh in range(2):
            h = 2 * hp + hh
            for kind in range(2):
                d = qq - kk + kind * BLK
                val = jnp.full((BLK, BLK), relb_ref[h, REL_BUCKETS - 1], F32)
                for bkt in range(REL_BUCKETS - 2, -1, -1):
                    val = jnp.where(d < BUCKET_LOWER[bkt + 1], relb_ref[h, bkt], val)
                val = val * LOG2E
                if kind == 0:
                    val = jnp.where(d >= 0, val, NEG)
                bias_ref[hh, kind] = val
            bias_ref[hh, 2] = jnp.full((BLK, BLK), relb_ref[h, REL_BUCKETS - 1] * LOG2E, F32)

    @pl.when(t == 0)
    def _():
        lane = lax.broadcasted_iota(jnp.int32, (1, LANES), 1)
        ones_row = jnp.where(lax.broadcasted_iota(jnp.int32, (V_ROWS - HEAD_DIM, BLK), 0) == 0,
                             1.0, 0.0).astype(BF16)
        for j in range(nblk):
            kj = k_ref[0, j * BLK:(j + 1) * BLK, :]
            kb_ref[j * BLK:(j + 1) * BLK, :] = kj.astype(BF16)
            kmj = jnp.sum(kj, axis=0, keepdims=True) * (1.0 / BLK)
            km_ref[j:j + 1, :] = jnp.where(lane < HEAD_DIM, kmj, 0.0)
            km_ref[nblk + j:nblk + j + 1, :] = jnp.where(lane >= HEAD_DIM, kmj, 0.0)
            for hh in range(2):
                va_ref[j, hh, 0:HEAD_DIM, :] = vT_ref[0, j, HEAD_DIM * hh:HEAD_DIM * (hh + 1), :]
                va_ref[j, hh, HEAD_DIM:V_ROWS, :] = ones_row

    ridx = lax.broadcasted_iota(jnp.int32, (nblk, BLK), 0)
    sub = lax.broadcasted_iota(jnp.int32, (LANES, BLK), 0)
    scale = HEAD_DIM ** -0.5 * LOG2E
    q_blocks = (t, nblk - 1 - t)
    for s, (q_ref, qi) in enumerate(zip((qlo_ref, qhi_ref), q_blocks)):
        qT = q_ref[0, 0]
        gT = jnp.dot(km_ref[...], qT, precision=HI, preferred_element_type=F32)
        past = ridx < qi
        for hh in range(2):
            gm = jnp.where(past, gT[nblk * hh:nblk * (hh + 1)], -jnp.inf)
            cnt = jnp.zeros((nblk, BLK), F32)
            for jp in range(nblk):
                row = gm[jp:jp + 1, :]
                beats = (row > gm) | ((row == gm) & (ridx > jp))
                cnt = cnt + jnp.where(beats, 1.0, 0.0)
            visible = (past & (cnt < MOBA_TOPK)) | (ridx == qi)
            addm_ref[s, nblk * hh:nblk * (hh + 1), :] = jnp.where(visible, 0.0, NEG)
            in_head = (sub >= HEAD_DIM * hh) & (sub < HEAD_DIM * (hh + 1))
            qh_ref[s, hh] = jnp.where(in_head, qT * scale, 0.0).astype(BF16)

    i_hi = nblk - 1 - t
    has_lo_prev = t >= 1
    n_lo_far = jnp.maximum(t - 1, 0)
    tiles = [(0, t, 0, False), (1, i_hi, 0, False), (1, i_hi - 1, 1, False),
             (jnp.where(has_lo_prev, 0, 1), jnp.where(has_lo_prev, t - 1, i_hi - 2),
              jnp.where(has_lo_prev, 1, 2), False)]
    for f in range(nblk - 3):
        f_lo = f < n_lo_far
        tiles.append((jnp.where(f_lo, 0, 1), jnp.where(f_lo, f, f - n_lo_far), None, True))
    ntile = len(tiles)

    cmax = [[None] * ntile for _ in range(2)]
    rowoff = [[None] * ntile for _ in range(2)]
    for hh in range(2):
        for n, (slot, kblk, bias_kind, far) in enumerate(tiles):
            kj = kb_ref[pl.ds(pl.multiple_of(kblk * BLK, BLK), BLK), :]
            lg = jnp.dot(kj, qh_ref[slot, hh], preferred_element_type=F32)
            if bias_kind is not None:
                lg = lg + bias_ref[hh, bias_kind]
            lg_ref[hh, n] = lg
            cm = jnp.max(lg, axis=0, keepdims=True)
            if n >= 2:
                off = addm_ref[slot, pl.ds(nblk * hh + kblk, 1), :]
                if far:
                    off = off + relb_ref[2 * hp + hh, REL_BUCKETS - 1] * LOG2E
                rowoff[hh][n] = off
                cm = cm + off
            cmax[hh][n] = cm

    def to_lo(n, x, other):
        slot = tiles[n][0]
        if isinstance(slot, int):
            return x if slot == 0 else other
        return jnp.where(slot == 0, x, other)

    for hh in range(2):
        m_lo = cmax[hh][0]
        m_hi = cmax[hh][1]
        for n in range(2, ntile):
            m_lo = jnp.maximum(m_lo, to_lo(n, cmax[hh][n], -jnp.inf))
            m_hi = jnp.maximum(m_hi, to_lo(n, -jnp.inf, cmax[hh][n]))
        acc_lo = None
        acc_hi = None
        for n, (slot, kblk, _, _) in enumerate(tiles):
            m_n = to_lo(n, m_lo, m_hi)
            if rowoff[hh][n] is not None:
                m_n = m_n - rowoff[hh][n]
            p = jnp.exp2(lg_ref[hh, n] - m_n)
            pvn = jnp.dot(va_ref[kblk, hh], p.astype(BF16), preferred_element_type=F32)
            if isinstance(slot, int):
                if slot == 0:
                    acc_lo = pvn if acc_lo is None else acc_lo + pvn
                else:
                    acc_hi = pvn if acc_hi is None else acc_hi + pvn
            else:
                acc_lo = acc_lo + jnp.where(slot == 0, pvn, 0.0)
                acc_hi = acc_hi + jnp.where(slot == 0, 0.0, pvn)
        oT_ref[0, 0, HEAD_DIM * hh:HEAD_DIM * (hh + 1), :] = (
            acc_lo[0:HEAD_DIM] / acc_lo[HEAD_DIM:HEAD_DIM + 1])
        oT_ref[0, 1, HEAD_DIM * hh:HEAD_DIM * (hh + 1), :] = (
            acc_hi[0:HEAD_DIM] / acc_hi[HEAD_DIM:HEAD_DIM + 1])


def _paired_pos(i, nblk):
    return jnp.where(i < nblk // 2, 2 * i, 2 * (nblk - 1 - i) + 1)


def _attention(rel_bias, qT, k3, vT, B, S):
    nblk = S // MOBA_BLOCK
    assert nblk % 2 == 0 and nblk >= 4
    assert BUCKET_LOWER[REL_BUCKETS - 1] <= MOBA_BLOCK + 1
    return pl.pallas_call(
        functools.partial(_attn_kernel, nblk=nblk),
        grid=(ATT_HEADS // 2, B, nblk // 2),
        in_specs=[
            pl.BlockSpec(memory_space=pltpu.SMEM),
            pl.BlockSpec((1, 1, LANES, MOBA_BLOCK), lambda hp, b, t: (b, t, hp, 0)),
            pl.BlockSpec((1, 1, LANES, MOBA_BLOCK), lambda hp, b, t: (b, nblk - 1 - t, hp, 0)),
            pl.BlockSpec((1, S, LANES), lambda hp, b, t: (b, 0, hp)),
            pl.BlockSpec((1, nblk, LANES, MOBA_BLOCK), lambda hp, b, t: (b, 0, hp, 0)),
        ],
        out_specs=pl.BlockSpec((1, 2, LANES, MOBA_BLOCK), lambda hp, b, t: (b, t, hp, 0)),
        out_shape=jax.ShapeDtypeStruct((B, nblk, ATT_WIDTH, MOBA_BLOCK), F32),
        scratch_shapes=[
            pltpu.VMEM((S, LANES), BF16),
            pltpu.VMEM((2 * nblk, LANES), F32),
            pltpu.VMEM((nblk, 2, V_ROWS, MOBA_BLOCK), BF16),
            pltpu.VMEM((2, 3, MOBA_BLOCK, MOBA_BLOCK), F32),
            pltpu.VMEM((2, 2 * nblk, MOBA_BLOCK), F32),
            pltpu.VMEM((2, 2, LANES, MOBA_BLOCK), BF16),
            pltpu.VMEM((2, nblk + 1, MOBA_BLOCK, MOBA_BLOCK), F32),
        ],
        compiler_params=pltpu.CompilerParams(
            dimension_semantics=("arbitrary", "arbitrary", "arbitrary"),
            vmem_limit_bytes=VMEM_LIMIT),
        name="moba_attn",
    )(rel_bias, qT, qT, k3, vT)


GDN_GROUP = 4 * HEAD_DIM
GDN_TILE = 256


def _gdn_kernel(yq_ref, yk_ref, yv_ref, sz_ref, gb_ref, nw_ref, out_ref,
                u_ref, wq_ref, a_ref, kd_ref, gl_ref, *, S):
    grp = pl.program_id(1)
    C = GDN_CHUNK
    W = GDN_GROUP
    TILE = GDN_TILE
    npair = W // LANES

    r_w = lax.broadcasted_iota(jnp.int32, (W, W), 0)
    c_w = lax.broadcasted_iota(jnp.int32, (W, W), 1)
    head_ones = jnp.where((r_w // HEAD_DIM) == (c_w // HEAD_DIM), 1.0, 0.0).astype(BF16)
    ltri_bd = jnp.where(((r_w // C) == (c_w // C)) & (c_w <= r_w), 1.0, 0.0).astype(BF16)
    r_e = lax.broadcasted_iota(jnp.int32, (LANES, W), 0)
    c_e = lax.broadcasted_iota(jnp.int32, (LANES, W), 1)
    head_of_col = grp * (W // HEAD_DIM) + c_e // HEAD_DIM
    sel_g = jnp.where(r_e == head_of_col, 1.0, 0.0).astype(BF16)
    sel_beta = jnp.where(r_e == GDN_HEADS + head_of_col, 1.0, 0.0).astype(BF16)
    tok = lax.broadcasted_iota(jnp.int32, (TILE, W), 0) % C
    col = lax.broadcasted_iota(jnp.int32, (TILE, W), 1) % HEAD_DIM
    causal_t = tok >= col
    strict_t = tok > col

    lane = lax.broadcasted_iota(jnp.int32, (C, LANES), 1)
    rowi = lax.broadcasted_iota(jnp.int32, (C, LANES), 0)
    first_head = lane < HEAD_DIM
    strict = rowi > (lane % HEAD_DIM)
    eye2 = jnp.where(rowi == (lane % HEAD_DIM), 1.0, 0.0)
    lane2 = lax.broadcasted_iota(jnp.int32, (C, 2 * LANES), 1)
    first_head2 = (lane2 % LANES) < HEAD_DIM
    r_l = lax.broadcasted_iota(jnp.int32, (LANES, LANES), 0)
    c_l = lax.broadcasted_iota(jnp.int32, (LANES, LANES), 1)
    same_head = (r_l // HEAD_DIM) == (c_l // HEAD_DIM)
    pair_ones = jnp.where(same_head, 1.0, 0.0).astype(BF16)

    def stack(x, mask):
        return jnp.concatenate([jnp.where(mask, x, 0.0), jnp.where(mask, 0.0, x)], axis=0)

    dot = functools.partial(jnp.dot, preferred_element_type=F32)

    def solve_stages(r):
        yield
        r0 = pl.multiple_of(r * TILE, TILE)
        yq = yq_ref[pl.ds(r0, TILE), :]
        yk = yk_ref[pl.ds(r0, TILE), :]
        yv = yv_ref[pl.ds(r0, TILE), :]
        ssq = _dot_split_lhs(yq * yq, head_ones, 1)
        ssk = _dot_split_lhs(yk * yk, head_ones, 1)
        qn = yq * lax.rsqrt(ssq + EPS) * (HEAD_DIM ** -0.5)
        kn = yk * lax.rsqrt(ssk + EPS)
        gb2 = _split_bf16(gb_ref[pl.ds(r0, TILE), :], 2)
        g = sum(dot(h, sel_g) for h in gb2)
        beta = sum(dot(h, sel_beta) for h in gb2)
        gcd = _dot_split_rhs(ltri_bd, jnp.concatenate([g, jnp.where(strict_t, g, 0.0)], axis=1), 2)
        gc = gcd[:, :W]
        decay = jnp.where(causal_t, jnp.exp(jnp.where(causal_t, gcd[:, W:], 0.0)), 0.0)
        egc = jnp.exp(gc)
        kb = kn * beta
        rv = yv * beta
        rk = kb * egc
        qd = qn * egc
        for cc in range(TILE // C):
            rs = slice(cc * C, (cc + 1) * C)
            ci = r * (TILE // C) + cc
            g_last = gc[(cc + 1) * C - 1:(cc + 1) * C, :]
            kd_ref[pl.ds(pl.multiple_of(r0 + cc * C, C), C), :] = (
                kn[rs] * jnp.exp(g_last - gc[rs])).astype(BF16)
            gl_ref[pl.ds(pl.multiple_of(ci * SUBLANES, SUBLANES), SUBLANES), :] = (
                jnp.broadcast_to(jnp.exp(g_last), (SUBLANES, W)))
        units = [(slice(cc * C, (cc + 1) * C), slice(LANES * p, LANES * (p + 1)), cc)
                 for cc in range(TILE // C) for p in range(npair)]
        kqs = [lax.dot_general(jnp.concatenate([kn[rs, ls], qn[rs, ls]], axis=0).astype(BF16),
                               stack(kn[rs, ls], first_head).astype(BF16), NT,
                               preferred_element_type=F32) for rs, ls, _ in units]
        yield
        ps = [-jnp.where(strict, kq[0:C] * beta[rs, ls] * decay[rs, ls], 0.0)
              for kq, (rs, ls, _) in zip(kqs, units)]
        ss = [eye2 + p for p in ps]
        ps = [dot(p.astype(BF16), stack(p, first_head).astype(BF16)) for p in ps]
        yield
        nround = int(math.log2(C))
        for k in range(1, nround):
            rhs = [stack(s_, first_head).astype(BF16) for s_ in ss]
            if k + 1 < nround:
                rhs = [jnp.concatenate([stack(p, first_head).astype(BF16), sx], axis=1)
                       for p, sx in zip(ps, rhs)]
            outs = [dot(p.astype(BF16), sx) for p, sx in zip(ps, rhs)]
            if k + 1 < nround:
                ps = [o[:, :LANES] for o in outs]
                ss = [s_ + o[:, LANES:] for s_, o in zip(ss, outs)]
            else:
                ss = [s_ + o for s_, o in zip(ss, outs)]
            yield
        xs = [dot(s_.astype(BF16),
                  stack(jnp.concatenate([rv[rs, ls], rk[rs, ls]], axis=1), first_head2).astype(BF16))
              for s_, (rs, ls, _) in zip(ss, units)]
        yield
        for x, kq, (rs, ls, cc) in zip(xs, kqs, units):
            ci = r * (TILE // C) + cc
            crow = pl.ds(pl.multiple_of(r0 + cc * C, C), C)
            u_ref[crow, ls] = x[:, :LANES]
            wq_ref[ci, 0:C, ls] = x[:, LANES:].astype(BF16)
            wq_ref[ci, C:2 * C, ls] = qd[rs, ls].astype(BF16)
            a_ref[crow, ls] = (kq[C:2 * C] * decay[rs, ls]).astype(BF16)

    lss = [slice(LANES * p, LANES * (p + 1)) for p in range(npair)]

    def recurrence_stages(r, states):
        pending = None

        def finish(rows, os_):
            for ls, o in zip(lss, os_):
                ms = _dot_split_lhs(o * o, pair_ones, 1) * (1.0 / HEAD_DIM)
                out_ref[rows, ls] = o * lax.rsqrt(ms + EPS) * nw_ref[:, ls] * sz_ref[rows, ls]

        for cc in range(TILE // C):
            c = r * (TILE // C) + cc
            rows = pl.ds(pl.multiple_of(c * C, C), C)
            wqs = [dot(wq_ref[c, :, ls], st.astype(BF16)) for ls, st in zip(lss, states)]
            if pending is not None:
                finish(*pending)
            yield
            v_news = [u_ref[rows, ls] - wq[0:C] for ls, wq in zip(lss, wqs)]
            kvs = [lax.dot_general(kd_ref[rows, ls], v.astype(BF16), TN, preferred_element_type=F32)
                   for ls, v in zip(lss, v_news)]
            os_ = [wq[C:2 * C] + dot(a_ref[rows, ls], stack(v, first_head).astype(BF16))
                   for ls, wq, v in zip(lss, wqs, v_news)]
            gl_row = pl.ds(pl.multiple_of(c * SUBLANES, SUBLANES), 1)
            states[:] = [st * gl_ref[gl_row, ls] + jnp.where(same_head, kv, 0.0)
                         for ls, st, kv in zip(lss, states, kvs)]
            pending = (rows, os_)
            yield
        finish(*pending)

    ntile = S // TILE
    _interleave(solve_stages(0))

    def tile_body(r, states):
        states = list(states)
        _interleave(recurrence_stages(r - 1, states), solve_stages(r))
        return tuple(states)

    init = tuple(jnp.zeros((LANES, LANES), F32) for _ in range(npair))
    states = list(lax.fori_loop(1, ntile, tile_body, init))
    _interleave(recurrence_stages(ntile - 1, states))


def _gdn(gqkv, sz, gb, nw_row, B, S):
    T = B * S
    W = GDN_GROUP
    ngrp = GDN_WIDTH // W
    nchunk = S // GDN_CHUNK
    col = lambda off: (lambda b, g: (b, off + g))
    ccol = lambda off: (lambda b, g: (0, off + g))
    return pl.pallas_call(
        functools.partial(_gdn_kernel, S=S),
        grid=(B, ngrp),
        in_specs=[
            pl.BlockSpec((S, W), col(0)),
            pl.BlockSpec((S, W), col(ngrp)),
            pl.BlockSpec((S, W), col(2 * ngrp)),
            pl.BlockSpec((S, W), col(0)),
            pl.BlockSpec((S, LANES), lambda b, g: (b, 0)),
            pl.BlockSpec((1, W), ccol(0)),
        ],
        out_specs=pl.BlockSpec((S, W), col(0)),
        out_shape=jax.ShapeDtypeStruct((T, GDN_WIDTH), F32),
        scratch_shapes=[
            pltpu.VMEM((S, W), F32),
            pltpu.VMEM((nchunk, 2 * GDN_CHUNK, W), BF16),
            pltpu.VMEM((S, W), BF16),
            pltpu.VMEM((S, W), BF16),
            pltpu.VMEM((nchunk * SUBLANES, W), F32),
        ],
        compiler_params=pltpu.CompilerParams(
            dimension_semantics=("arbitrary", "arbitrary"), vmem_limit_bytes=VMEM_LIMIT),
        name="gdn",
    )(gqkv, gqkv, gqkv, sz, gb, nw_row)


def _out_mlp_kernel(x_ref, oTa_ref, oTb_ref, og_ref, woa_ref, wog_ref, pmn_ref, pre_ref, post_ref,
                    wup_ref, wdn_ref, out_ref):
    oT = jnp.concatenate([oTa_ref[0, 0], oTb_ref[0, 0]], axis=1)
    o_att = oT.T.astype(BF16)
    mix = jnp.dot(o_att, woa_ref[...], preferred_element_type=F32)
    mix = mix + jnp.dot(og_ref[...].astype(BF16), wog_ref[...], preferred_element_type=F32)
    x1 = x_ref[...] + _rms(mix, pmn_ref[...])
    h = _rms(x1, pre_ref[...]).astype(BF16)
    acc = jnp.zeros((ROW_TILE, D_MODEL), F32)
    for c in range(D_FF // FF_TILE):
        up = jnp.dot(h, wup_ref[:, c * FF_TILE:(c + 1) * FF_TILE], preferred_element_type=F32)
        act = jnp.square(jnp.maximum(up, 0.0)).astype(BF16)
        acc = acc + jnp.dot(act, wdn_ref[c * FF_TILE:(c + 1) * FF_TILE, :], preferred_element_type=F32)
    out_ref[...] = x1 + _rms(acc, post_ref[...])


def _out_mlp(xf, oT, og, woa, wog, pmn, pre, post, wup, wdn, B, S):
    T = B * S
    nblk = S // MOBA_BLOCK
    tiles_per_seq = S // ROW_TILE
    assert ROW_TILE == 2 * MOBA_BLOCK
    const = lambda i: (0, 0)
    row = lambda i: (i, 0)

    def att_block(which):
        def index(i):
            blk = 2 * (i % tiles_per_seq) + which
            return (i // tiles_per_seq, _paired_pos(blk, nblk), 0, 0)
        return index

    single = dict(pipeline_mode=pl.Buffered(1))
    return pl.pallas_call(
        _out_mlp_kernel,
        grid=(T // ROW_TILE,),
        in_specs=[
            pl.BlockSpec((ROW_TILE, D_MODEL), row),
            pl.BlockSpec((1, 1, ATT_WIDTH, MOBA_BLOCK), att_block(0)),
            pl.BlockSpec((1, 1, ATT_WIDTH, MOBA_BLOCK), att_block(1)),
            pl.BlockSpec((ROW_TILE, GDN_WIDTH), row),
            pl.BlockSpec(woa.shape, const, **single),
            pl.BlockSpec(wog.shape, const, **single),
            pl.BlockSpec((1, D_MODEL), const),
            pl.BlockSpec((1, D_MODEL), const),
            pl.BlockSpec((1, D_MODEL), const),
            pl.BlockSpec(wup.shape, const, **single),
            pl.BlockSpec(wdn.shape, const, **single),
        ],
        out_specs=pl.BlockSpec((ROW_TILE, D_MODEL), row),
        out_shape=jax.ShapeDtypeStruct((T, D_MODEL), F32),
        compiler_params=pltpu.CompilerParams(
            dimension_semantics=("arbitrary",), vmem_limit_bytes=VMEM_LIMIT),
        name="out_mlp",
    )(xf, oT, oT, og, woa, wog, pmn, pre, post, wup, wdn)


def kernel(x, w_in, w_out, conv_w, A_log, dt_bias, gdn_norm_w, rel_bias, pre_mix_norm,
           post_mix_norm, pre_mlp_norm, post_mlp_norm, w_up, w_down):
    B, S, D = x.shape
    assert D == D_MODEL and S % ROW_TILE == 0 and S % MOBA_BLOCK == 0
    T = B * S
    depth = w_in.shape[0]
    xf = x.reshape(T, D)
    o0, o1, o2, o3, o4 = 0, ATT_WIDTH, 2 * ATT_WIDTH, 3 * ATT_WIDTH, 3 * ATT_WIDTH + 3 * GDN_WIDTH
    o5 = o4 + GDN_WIDTH
    for l in range(depth):
        wi = w_in[l]
        wqT = wi[:, o0:o1].T.astype(BF16)
        wk = wi[:, o1:o2].astype(BF16)
        wvT = wi[:, o2:o3].T.astype(BF16)
        wg = wi[:, o3:o4].astype(BF16)
        wz = wi[:, o4:o5].astype(BF16)
        wab = jnp.pad(wi[:, o5:], ((0, 0), (0, LANES - 2 * GDN_HEADS))).astype(BF16)
        pad8 = lambda v: jnp.pad(v.astype(F32), (0, LANES - GDN_HEADS))[None, :]
        qT, k, vT, gqkv, sz, gb = _inproj(xf, pre_mix_norm[l][None, :], wqT, wk, wvT, wg, wz, wab,
                                          conv_w[l], pad8(A_log[l]), pad8(dt_bias[l]), B, S)
        oT = _attention(rel_bias.astype(F32), qT, k.reshape(B, S, ATT_WIDTH), vT, B, S)
        og = _gdn(gqkv, sz, gb, jnp.tile(gdn_norm_w[l], GDN_HEADS)[None, :], B, S)
        wo = w_out[l].astype(BF16)
        xf = _out_mlp(xf, oT, og, wo[:ATT_WIDTH], wo[ATT_WIDTH:], post_mix_norm[l][None, :],
                      pre_mlp_norm[l][None, :], post_mlp_norm[l][None, :],
                      w_up[l].astype(BF16), w_down[l].astype(BF16), B, S)
    return xf.reshape(B, S, D)
```

```python
import functools
import math

import jax
import jax.numpy as jnp
from jax import lax
from jax.experimental import pallas as pl
from jax.experimental.pallas import tpu as pltpu

F32 = jnp.float32
BF16 = jnp.bfloat16
HI = lax.Precision.HIGHEST

D_MODEL = 1024
HEAD_DIM = 64
ATT_HEADS = 8
GDN_HEADS = 8
ATT_WIDTH = ATT_HEADS * HEAD_DIM
GDN_WIDTH = GDN_HEADS * HEAD_DIM
MOBA_BLOCK = 256
MOBA_TOPK = 3
GDN_CHUNK = 64
CONV_WIDTH = 4
D_FF = 4 * D_MODEL
REL_BUCKETS = 32
REL_MAX_EXACT = 16
REL_MAX_DIST = 128
EPS = 1e-6
NEG = -1e30
LOG2E = math.log2(math.e)

LANES = 128
SUBLANES = 8
VMEM_LIMIT = 56 * 1024 * 1024
ROW_TILE = 512
FF_TILE = 1024

NT = (((1,), (1,)), ((), ()))
TN = (((0,), (0,)), ((), ()))


def _bucket_lower_bounds():
    def bucket(d):
        if d < REL_MAX_EXACT:
            return d
        t = math.log(d / REL_MAX_EXACT) / math.log(REL_MAX_DIST / REL_MAX_EXACT)
        t = t * (REL_BUCKETS - REL_MAX_EXACT)
        assert d in (REL_MAX_EXACT, REL_MAX_DIST) or abs(t - round(t)) > 1e-6
        return min(REL_MAX_EXACT + int(t + 1e-9), REL_BUCKETS - 1)
    lower = []
    for b in range(REL_BUCKETS):
        d = 0
        while bucket(d) < b:
            d += 1
        lower.append(d)
    return lower


BUCKET_LOWER = _bucket_lower_bounds()


def _sigmoid(x):
    return 0.5 * jnp.tanh(0.5 * x) + 0.5


def _interleave(*stage_generators):
    live = list(stage_generators)
    while live:
        for gen in list(live):
            try:
                next(gen)
            except StopIteration:
                live.remove(gen)


def _rms(x, w):
    return x * lax.rsqrt(jnp.mean(x * x, axis=-1, keepdims=True) + EPS) * w


def _split_bf16(x, parts):
    out = []
    for _ in range(parts):
        h = x.astype(BF16)
        out.append(h)
        x = x - h.astype(F32)
    return out


def _dot_split_lhs(x, c, parts):
    acc = None
    for h in _split_bf16(x, parts):
        d = jnp.dot(h, c, preferred_element_type=F32)
        acc = d if acc is None else acc + d
    return acc


def _dot_split_rhs(c, x, parts):
    acc = None
    for h in _split_bf16(x, parts):
        d = jnp.dot(c, h, preferred_element_type=F32)
        acc = d if acc is None else acc + d
    return acc


CONV_COLS = 512


def _inproj_kernel(x_ref, xp_ref, nw_ref, wqT_ref, wk_ref, wvT_ref, wg_ref, wz_ref, wab_ref,
                   cw_ref, alog_ref, dtb_ref,
                   qT_ref, k_ref, vT_ref, g_ref, z_ref, gb_ref, *, tiles_per_seq):
    h = _rms(x_ref[...], nw_ref[...]).astype(BF16)

    hp = _rms(xp_ref[...], nw_ref[...]).astype(BF16)
    seq_start = (pl.program_id(0) % tiles_per_seq) == 0
    trow8 = lax.broadcasted_iota(jnp.int32, (SUBLANES, CONV_COLS), 0)
    for c in range(3 * GDN_WIDTH // CONV_COLS):
        cols = slice(c * CONV_COLS, (c + 1) * CONV_COLS)
        cur = jnp.dot(h, wg_ref[:, cols], preferred_element_type=F32)
        prev8 = jnp.dot(hp, wg_ref[:, cols], preferred_element_type=F32)
        prev8 = jnp.where(seq_start, 0.0, prev8)
        acc = cur * cw_ref[CONV_WIDTH - 1:CONV_WIDTH, cols]
        for s in range(1, CONV_WIDTH):
            rolled = pltpu.roll(cur, s, 0)
            top = jnp.where(trow8 < s, pltpu.roll(prev8, s, 0), rolled[0:SUBLANES])
            tap = jnp.concatenate([top, rolled[SUBLANES:]], axis=0)
            acc = acc + tap * cw_ref[CONV_WIDTH - 1 - s:CONV_WIDTH - s, cols]
        g_ref[:, cols] = acc * _sigmoid(acc)

    z = jnp.dot(h, wz_ref[...], preferred_element_type=F32)
    z_ref[...] = z * _sigmoid(z)
    ab = jnp.dot(h, wab_ref[...], preferred_element_type=F32)
    xs = ab + dtb_ref[...]
    log_decay = -jnp.exp(alog_ref[...]) * (jnp.maximum(xs, 0.0) + jnp.log1p(jnp.exp(-jnp.abs(xs))))
    lane = lax.broadcasted_iota(jnp.int32, ab.shape, 1)
    gb_ref[...] = jnp.where(lane < GDN_HEADS, log_decay, _sigmoid(ab))

    qT = lax.dot_general(wqT_ref[...], h, NT, preferred_element_type=F32)
    vT = lax.dot_general(wvT_ref[...], h, NT, preferred_element_type=F32)
    for t in range(ROW_TILE // MOBA_BLOCK):
        qT_ref[0, t] = qT[:, t * MOBA_BLOCK:(t + 1) * MOBA_BLOCK]
        vT_ref[0, t] = vT[:, t * MOBA_BLOCK:(t + 1) * MOBA_BLOCK].astype(BF16)
    k_ref[...] = jnp.dot(h, wk_ref[...], preferred_element_type=F32)


def _inproj(xf, nw, wqT, wk, wvT, wg, wz, wab, conv_w, alog_pad, dtb_pad, B, S):
    T = B * S
    nblk = S // MOBA_BLOCK
    tiles_per_seq = S // ROW_TILE
    blk_per_tile = ROW_TILE // MOBA_BLOCK
    const = lambda i: (0, 0)
    row = lambda i: (i, 0)
    tr = lambda i: (i // tiles_per_seq, i % tiles_per_seq, 0, 0)
    prev_rows = lambda i: (jnp.maximum(i * (ROW_TILE // SUBLANES) - 1, 0), 0)
    return pl.pallas_call(
        functools.partial(_inproj_kernel, tiles_per_seq=tiles_per_seq),
        grid=(T // ROW_TILE,),
        in_specs=[
            pl.BlockSpec((ROW_TILE, D_MODEL), row),
            pl.BlockSpec((SUBLANES, D_MODEL), prev_rows),
            pl.BlockSpec((1, D_MODEL), const),
            pl.BlockSpec(wqT.shape, const),
            pl.BlockSpec(wk.shape, const),
            pl.BlockSpec(wvT.shape, const),
            pl.BlockSpec(wg.shape, const),
            pl.BlockSpec(wz.shape, const),
            pl.BlockSpec(wab.shape, const),
            pl.BlockSpec(conv_w.shape, const),
            pl.BlockSpec((1, LANES), const),
            pl.BlockSpec((1, LANES), const),
        ],
        out_specs=[
            pl.BlockSpec((1, blk_per_tile, ATT_WIDTH, MOBA_BLOCK), tr),
            pl.BlockSpec((ROW_TILE, ATT_WIDTH), row),
            pl.BlockSpec((1, blk_per_tile, ATT_WIDTH, MOBA_BLOCK), tr),
            pl.BlockSpec((ROW_TILE, 3 * GDN_WIDTH), row),
            pl.BlockSpec((ROW_TILE, GDN_WIDTH), row),
            pl.BlockSpec((ROW_TILE, LANES), row),
        ],
        out_shape=[
            jax.ShapeDtypeStruct((B, nblk, ATT_WIDTH, MOBA_BLOCK), F32),
            jax.ShapeDtypeStruct((T, ATT_WIDTH), F32),
            jax.ShapeDtypeStruct((B, nblk, ATT_WIDTH, MOBA_BLOCK), BF16),
            jax.ShapeDtypeStruct((T, 3 * GDN_WIDTH), F32),
            jax.ShapeDtypeStruct((T, GDN_WIDTH), F32),
            jax.ShapeDtypeStruct((T, LANES), F32),
        ],
        compiler_params=pltpu.CompilerParams(
            dimension_semantics=("arbitrary",), vmem_limit_bytes=VMEM_LIMIT),
        name="inproj",
    )(xf, xf, nw, wqT, wk, wvT, wg, wz, wab, conv_w, alog_pad, dtb_pad)


V_ROWS = HEAD_DIM + 16


def _attn_kernel(relb_ref, qlo_ref, qhi_ref, k_ref, vT_ref, oT_ref,
                 kb_ref, km_ref, va_ref, bias_ref, addm_ref, qh_ref, lg_ref, *, nblk):
    hp = pl.program_id(0)
    b = pl.program_id(1)
    t = pl.program_id(2)
    BLK = MOBA_BLOCK

    @pl.when((b == 0) & (t == 0))
    def _():
        kk = lax.broadcasted_iota(jnp.int32, (BLK, BLK), 0)
        qq = lax.broadcasted_iota(jnp.int32, (BLK, BLK), 1)
        for hh in range(2):
            h = 2 * hp + hh
            for kind in range(2):
                d = qq - kk + kind * BLK
                val = jnp.full((BLK, BLK), relb_ref[h, REL_BUCKETS - 1], F32)
                for bkt in range(REL_BUCKETS - 2, -1, -1):
                    val = jnp.where(d < BUCKET_LOWER[bkt + 1], relb_ref[h, bkt], val)
                val = val * LOG2E
                if kind == 0:
                    val = jnp.where(d >= 0, val, NEG)
                bias_ref[hh, kind] = val
            bias_ref[hh, 2] = jnp.full((BLK, BLK), relb_ref[h, REL_BUCKETS - 1] * LOG2E, F32)

    @pl.when(t == 0)
    def _():
        lane = lax.broadcasted_iota(jnp.int32, (1, LANES), 1)
        ones_row = jnp.where(lax.broadcasted_iota(jnp.int32, (V_ROWS - HEAD_DIM, BLK), 0) == 0,
                             1.0, 0.0).astype(BF16)
        for j in range(nblk):
            kj = k_ref[0, j * BLK:(j + 1) * BLK, :]
            kb_ref[j * BLK:(j + 1) * BLK, :] = kj.astype(BF16)
            kmj = jnp.sum(kj, axis=0, keepdims=True) * (1.0 / BLK)
            km_ref[j:j + 1, :] = jnp.where(lane < HEAD_DIM, kmj, 0.0)
            km_ref[nblk + j:nblk + j + 1, :] = jnp.where(lane >= HEAD_DIM, kmj, 0.0)
            for hh in range(2):
                va_ref[j, hh, 0:HEAD_DIM, :] = vT_ref[0, j, HEAD_DIM * hh:HEAD_DIM * (hh + 1), :]
                va_ref[j, hh, HEAD_DIM:V_ROWS, :] = ones_row

    ridx = lax.broadcasted_iota(jnp.int32, (nblk, BLK), 0)
    sub = lax.broadcasted_iota(jnp.int32, (LANES, BLK), 0)
    scale = HEAD_DIM ** -0.5 * LOG2E
    q_blocks = (t, nblk - 1 - t)
    for s, (q_ref, qi) in enumerate(zip((qlo_ref, qhi_ref), q_blocks)):
        qT = q_ref[0, 0]
        gT = jnp.dot(km_ref[...], qT, precision=HI, preferred_element_type=F32)
        past = ridx < qi
        for hh in range(2):
            gm = jnp.where(past, gT[nblk * hh:nblk * (hh + 1)], -jnp.inf)
            cnt = jnp.zeros((nblk, BLK), F32)
            for jp in range(nblk):
                row = gm[jp:jp + 1, :]
                beats = (row > gm) | ((row == gm) & (ridx > jp))
                cnt = cnt + jnp.where(beats, 1.0, 0.0)
            visible = (past & (cnt < MOBA_TOPK)) | (ridx == qi)
            addm_ref[s, nblk * hh:nblk * (hh + 1), :] = jnp.where(visible, 0.0, NEG)
            in_head = (sub >= HEAD_DIM * hh) & (sub < HEAD_DIM * (hh + 1))
            qh_ref[s, hh] = jnp.where(in_head, qT * scale, 0.0).astype(BF16)

    i_hi = nblk - 1 - t
    has_lo_prev = t >= 1
    n_lo_far = jnp.maximum(t - 1, 0)
    tiles = [(0, t, 0, False), (1, i_hi, 0, False), (1, i_hi - 1, 1, False),
             (jnp.where(has_lo_prev, 0, 1), jnp.where(has_lo_prev, t - 1, i_hi - 2),
              jnp.where(has_lo_prev, 1, 2), False)]
    for f in range(nblk - 3):
        f_lo = f < n_lo_far
        tiles.append((jnp.where(f_lo, 0, 1), jnp.where(f_lo, f, f - n_lo_far), None, True))
    ntile = len(tiles)

    cmax = [[None] * ntile for _ in range(2)]
    rowoff = [[None] * ntile for _ in range(2)]
    for hh in range(2):
        for n, (slot, kblk, bias_kind, far) in enumerate(tiles):
            kj = kb_ref[pl.ds(pl.multiple_of(kblk * BLK, BLK), BLK), :]
            lg = jnp.dot(kj, qh_ref[slot, hh], preferred_element_type=F32)
            if bias_kind is not None:
                lg = lg + bias_ref[hh, bias_kind]
            lg_ref[hh, n] = lg
            cm = jnp.max(lg, axis=0, keepdims=True)
            if n >= 2:
                off = addm_ref[slot, pl.ds(nblk * hh + kblk, 1), :]
                if far:
                    off = off + relb_ref[2 * hp + hh, REL_BUCKETS - 1] * LOG2E
                rowoff[hh][n] = off
                cm = cm + off
            cmax[hh][n] = cm

    def to_lo(n, x, other):
        slot = tiles[n][0]
        if isinstance(slot, int):
            return x if slot == 0 else other
        return jnp.where(slot == 0, x, other)

    for hh in range(2):
        m_lo = cmax[hh][0]
        m_hi = cmax[hh][1]
        for n in range(2, ntile):
            m_lo = jnp.maximum(m_lo, to_lo(n, cmax[hh][n], -jnp.inf))
            m_hi = jnp.maximum(m_hi, to_lo(n, -jnp.inf, cmax[hh][n]))
        acc_lo = None
        acc_hi = None
        for n, (slot, kblk, _, _) in enumerate(tiles):
            m_n = to_lo(n, m_lo, m_hi)
            if rowoff[hh][n] is not None:
                m_n = m_n - rowoff[hh][n]
            p = jnp.exp2(lg_ref[hh, n] - m_n)
            pvn = jnp.dot(va_ref[kblk, hh], p.astype(BF16), preferred_element_type=F32)
            if isinstance(slot, int):
                if slot == 0:
                    acc_lo = pvn if acc_lo is None else acc_lo + pvn
                else:
                    acc_hi = pvn if acc_hi is None else acc_hi + pvn
            else:
                acc_lo = acc_lo + jnp.where(slot == 0, pvn, 0.0)
                acc_hi = acc_hi + jnp.where(slot == 0, 0.0, pvn)
        oT_ref[0, 0, HEAD_DIM * hh:HEAD_DIM * (hh + 1), :] = (
            acc_lo[0:HEAD_DIM] / acc_lo[HEAD_DIM:HEAD_DIM + 1])
        oT_ref[0, 1, HEAD_DIM * hh:HEAD_DIM * (hh + 1), :] = (
            acc_hi[0:HEAD_DIM] / acc_hi[HEAD_DIM:HEAD_DIM + 1])


def _paired_pos(i, nblk):
    return jnp.where(i < nblk // 2, 2 * i, 2 * (nblk - 1 - i) + 1)


def _attention(rel_bias, qT, k3, vT, B, S):
    nblk = S // MOBA_BLOCK
    assert nblk % 2 == 0 and nblk >= 4
    assert BUCKET_LOWER[REL_BUCKETS - 1] <= MOBA_BLOCK + 1
    return pl.pallas_call(
        functools.partial(_attn_kernel, nblk=nblk),
        grid=(ATT_HEADS // 2, B, nblk // 2),
        in_specs=[
            pl.BlockSpec(memory_space=pltpu.SMEM),
            pl.BlockSpec((1, 1, LANES, MOBA_BLOCK), lambda hp, b, t: (b, t, hp, 0)),
            pl.BlockSpec((1, 1, LANES, MOBA_BLOCK), lambda hp, b, t: (b, nblk - 1 - t, hp, 0)),
            pl.BlockSpec((1, S, LANES), lambda hp, b, t: (b, 0, hp)),
            pl.BlockSpec((1, nblk, LANES, MOBA_BLOCK), lambda hp, b, t: (b, 0, hp, 0)),
        ],
        out_specs=pl.BlockSpec((1, 2, LANES, MOBA_BLOCK), lambda hp, b, t: (b, t, hp, 0)),
        out_shape=jax.ShapeDtypeStruct((B, nblk, ATT_WIDTH, MOBA_BLOCK), F32),
        scratch_shapes=[
            pltpu.VMEM((S, LANES), BF16),
            pltpu.VMEM((2 * nblk, LANES), F32),
            pltpu.VMEM((nblk, 2, V_ROWS, MOBA_BLOCK), BF16),
            pltpu.VMEM((2, 3, MOBA_BLOCK, MOBA_BLOCK), F32),
            pltpu.VMEM((2, 2 * nblk, MOBA_BLOCK), F32),
            pltpu.VMEM((2, 2, LANES, MOBA_BLOCK), BF16),
            pltpu.VMEM((2, nblk + 1, MOBA_BLOCK, MOBA_BLOCK), F32),
        ],
        compiler_params=pltpu.CompilerParams(
            dimension_semantics=("arbitrary", "arbitrary", "arbitrary"),
            vmem_limit_bytes=VMEM_LIMIT),
        name="moba_attn",
    )(rel_bias, qT, qT, k3, vT)


GDN_TILE = 256
GDN_HALF = 2 * LANES


def _gdn_kernel(yq_ref, yk_ref, yv_ref, gb_ref, sz_ref, nw_ref, out_ref,
                u_ref, wq_ref, a_ref, kd_ref, gl_ref, st_ref, *, tiles_per_seq, nsteps):
    step = pl.program_id(0)
    C = GDN_CHUNK
    W = GDN_WIDTH
    TILE = GDN_TILE
    npair = W // LANES

    r_w = lax.broadcasted_iota(jnp.int32, (GDN_HALF, GDN_HALF), 0)
    c_w = lax.broadcasted_iota(jnp.int32, (GDN_HALF, GDN_HALF), 1)
    head_ones = jnp.where((r_w // HEAD_DIM) == (c_w // HEAD_DIM), 1.0, 0.0).astype(BF16)
    ltri_bd = jnp.where(((r_w // C) == (c_w // C)) & (c_w <= r_w), 1.0, 0.0).astype(BF16)
    r_e = lax.broadcasted_iota(jnp.int32, (LANES, W), 0)
    c_e = lax.broadcasted_iota(jnp.int32, (LANES, W), 1)
    head_of_col = c_e // HEAD_DIM
    sel_g = jnp.where(r_e == head_of_col, 1.0, 0.0).astype(BF16)
    sel_beta = jnp.where(r_e == GDN_HEADS + head_of_col, 1.0, 0.0).astype(BF16)
    tok = lax.broadcasted_iota(jnp.int32, (TILE, W), 0) % C
    col = lax.broadcasted_iota(jnp.int32, (TILE, W), 1) % HEAD_DIM
    causal_t = tok >= col
    strict_t = tok > col

    lane = lax.broadcasted_iota(jnp.int32, (C, LANES), 1)
    rowi = lax.broadcasted_iota(jnp.int32, (C, LANES), 0)
    first_head = lane < HEAD_DIM
    strict = rowi > (lane % HEAD_DIM)
    eye2 = jnp.where(rowi == (lane % HEAD_DIM), 1.0, 0.0)
    lane2 = lax.broadcasted_iota(jnp.int32, (C, 2 * LANES), 1)
    first_head2 = (lane2 % LANES) < HEAD_DIM
    r_l = lax.broadcasted_iota(jnp.int32, (LANES, LANES), 0)
    c_l = lax.broadcasted_iota(jnp.int32, (LANES, LANES), 1)
    same_head = (r_l // HEAD_DIM) == (c_l // HEAD_DIM)
    pair_ones = jnp.where(same_head, 1.0, 0.0).astype(BF16)

    def stack(x, mask):
        return jnp.concatenate([jnp.where(mask, x, 0.0), jnp.where(mask, 0.0, x)], axis=0)

    dot = functools.partial(jnp.dot, preferred_element_type=F32)

    def head_sumsq(y):
        y2 = (y * y).astype(BF16)
        return jnp.concatenate([dot(y2[:, h:h + GDN_HALF], head_ones) for h in range(0, W, GDN_HALF)],
                               axis=1)

    def solve_stages(slot):
        yield
        yq = yq_ref[...]
        yk = yk_ref[...]
        yv = yv_ref[...]
        qn = yq * lax.rsqrt(head_sumsq(yq) + EPS) * (HEAD_DIM ** -0.5)
        kn = yk * lax.rsqrt(head_sumsq(yk) + EPS)
        gb2 = _split_bf16(gb_ref[...], 2)
        g = sum(dot(h, sel_g) for h in gb2)
        beta = sum(dot(h, sel_beta) for h in gb2)
        gcd = _dot_split_rhs(ltri_bd, jnp.concatenate([g, jnp.where(strict_t, g, 0.0)], axis=1), 2)
        gc = gcd[:, :W]
        decay = jnp.where(causal_t, jnp.exp(jnp.where(causal_t, gcd[:, W:], 0.0)), 0.0)
        egc = jnp.exp(gc)
        kb = kn * beta
        rv = yv * beta
        rk = kb * egc
        qd = qn * egc
        for cc in range(TILE // C):
            rs = slice(cc * C, (cc + 1) * C)
            g_last = gc[(cc + 1) * C - 1:(cc + 1) * C, :]
            kd_ref[slot, rs, :] = (kn[rs] * jnp.exp(g_last - gc[rs])).astype(BF16)
            gl_ref[slot, cc * SUBLANES:(cc + 1) * SUBLANES, :] = (
                jnp.broadcast_to(jnp.exp(g_last), (SUBLANES, W)))
        units = [(slice(cc * C, (cc + 1) * C), slice(LANES * p, LANES * (p + 1)), cc)
                 for cc in range(TILE // C) for p in range(npair)]
        kqs = [lax.dot_general(jnp.concatenate([kn[rs, ls], qn[rs, ls]], axis=0).astype(BF16),
                               stack(kn[rs, ls], first_head).astype(BF16), NT,
                               preferred_element_type=F32) for rs, ls, _ in units]
        yield
        ps = [-jnp.where(strict, kq[0:C] * beta[rs, ls] * decay[rs, ls], 0.0)
              for kq, (rs, ls, _) in zip(kqs, units)]
        ss = [eye2 + p for p in ps]
        ps = [dot(p.astype(BF16), stack(p, first_head).astype(BF16)) for p in ps]
        yield
        nround = int(math.log2(C))
        for k in range(1, nround):
            rhs = [stack(s_, first_head).astype(BF16) for s_ in ss]
            if k + 1 < nround:
                rhs = [jnp.concatenate([stack(p, first_head).astype(BF16), sx], axis=1)
                       for p, sx in zip(ps, rhs)]
            outs = [dot(p.astype(BF16), sx) for p, sx in zip(ps, rhs)]
            if k + 1 < nround:
                ps = [o[:, :LANES] for o in outs]
                ss = [s_ + o[:, LANES:] for s_, o in zip(ss, outs)]
            else:
                ss = [s_ + o for s_, o in zip(ss, outs)]
            yield
        xs = [dot(s_.astype(BF16),
                  stack(jnp.concatenate([rv[rs, ls], rk[rs, ls]], axis=1), first_head2).astype(BF16))
              for s_, (rs, ls, _) in zip(ss, units)]
        yield
        for x, kq, (rs, ls, cc) in zip(xs, kqs, units):
            u_ref[slot, rs, ls] = x[:, :LANES]
            wq_ref[slot, cc, 0:C, ls] = x[:, LANES:].astype(BF16)
            wq_ref[slot, cc, C:2 * C, ls] = qd[rs, ls].astype(BF16)
            a_ref[slot, rs, ls] = (kq[C:2 * C] * decay[rs, ls]).astype(BF16)

    lss = [slice(LANES * p, LANES * (p + 1)) for p in range(npair)]

    def recurrence_stages(slot, seq_start):
        states = [jnp.where(seq_start, 0.0, st_ref[p]) for p in range(npair)]
        pending = None

        def finish(rs, os_):
            for ls, o in zip(lss, os_):
                ms = dot((o * o).astype(BF16), pair_ones) * (1.0 / HEAD_DIM)
                out_ref[rs, ls] = o * lax.rsqrt(ms + EPS) * nw_ref[:, ls] * sz_ref[rs, ls]

        for cc in range(TILE // C):
            rs = slice(cc * C, (cc + 1) * C)
            wqs = [dot(wq_ref[slot, cc, :, ls], st.astype(BF16)) for ls, st in zip(lss, states)]
            if pending is not None:
                finish(*pending)
            yield
            v_news = [u_ref[slot, rs, ls] - wq[0:C] for ls, wq in zip(lss, wqs)]
            kvs = [lax.dot_general(kd_ref[slot, rs, ls], v.astype(BF16), TN, preferred_element_type=F32)
                   for ls, v in zip(lss, v_news)]
            os_ = [wq[C:2 * C] + dot(a_ref[slot, rs, ls], stack(v, first_head).astype(BF16))
                   for ls, wq, v in zip(lss, wqs, v_news)]
            states = [st * gl_ref[slot, cc * SUBLANES:cc * SUBLANES + 1, ls] + jnp.where(same_head, kv, 0.0)
                      for ls, st, kv in zip(lss, states, kvs)]
            pending = (rs, os_)
            yield
        finish(*pending)
        for p in range(npair):
            st_ref[p] = states[p]

    cur = step % 2
    prev = 1 - cur
    seq_start = ((step - 1) % tiles_per_seq) == 0

    @pl.when(step == 0)
    def _():
        st_ref[...] = jnp.zeros(st_ref.shape, F32)
        _interleave(solve_stages(cur))

    @pl.when((step > 0) & (step < nsteps - 1))
    def _():
        _interleave(recurrence_stages(prev, seq_start), solve_stages(cur))

    @pl.when(step == nsteps - 1)
    def _():
        _interleave(recurrence_stages(prev, seq_start))


def _gdn(gqkv, sz, gb, nw_row, B, S):
    T = B * S
    W = GDN_WIDTH
    TILE = GDN_TILE
    assert TILE == GDN_HALF and S % TILE == 0
    nchunk = TILE // GDN_CHUNK
    ntiles = T // TILE
    nsteps = ntiles + 1
    cur_tile = lambda off: (lambda s: (jnp.minimum(s, ntiles - 1), off))
    prev_tile = lambda s: (jnp.maximum(s - 1, 0), 0)
    return pl.pallas_call(
        functools.partial(_gdn_kernel, tiles_per_seq=S // TILE, nsteps=nsteps),
        grid=(nsteps,),
        in_specs=[
            pl.BlockSpec((TILE, W), cur_tile(0)),
            pl.BlockSpec((TILE, W), cur_tile(1)),
            pl.BlockSpec((TILE, W), cur_tile(2)),
            pl.BlockSpec((TILE, LANES), cur_tile(0)),
            pl.BlockSpec((TILE, W), prev_tile),
            pl.BlockSpec((1, W), lambda s: (0, 0)),
        ],
        out_specs=pl.BlockSpec((TILE, W), prev_tile),
        out_shape=jax.ShapeDtypeStruct((T, GDN_WIDTH), F32),
        scratch_shapes=[
            pltpu.VMEM((2, TILE, W), F32),
            pltpu.VMEM((2, nchunk, 2 * GDN_CHUNK, W), BF16),
            pltpu.VMEM((2, TILE, W), BF16),
            pltpu.VMEM((2, TILE, W), BF16),
            pltpu.VMEM((2, nchunk * SUBLANES, W), F32),
            pltpu.VMEM((W // LANES, LANES, LANES), F32),
        ],
        compiler_params=pltpu.CompilerParams(
            dimension_semantics=("arbitrary",), vmem_limit_bytes=VMEM_LIMIT),
        name="gdn",
    )(gqkv, gqkv, gqkv, gb, sz, nw_row)


def _out_mlp_kernel(x_ref, oTa_ref, oTb_ref, og_ref, woa_ref, wog_ref, pmn_ref, pre_ref, post_ref,
                    wup_ref, wdn_ref, out_ref):
    oT = jnp.concatenate([oTa_ref[0, 0], oTb_ref[0, 0]], axis=1)
    o_att = oT.T.astype(BF16)
    mix = jnp.dot(o_att, woa_ref[...], preferred_element_type=F32)
    mix = mix + jnp.dot(og_ref[...].astype(BF16), wog_ref[...], preferred_element_type=F32)
    x1 = x_ref[...] + _rms(mix, pmn_ref[...])
    h = _rms(x1, pre_ref[...]).astype(BF16)
    acc = jnp.zeros((ROW_TILE, D_MODEL), F32)
    for c in range(D_FF // FF_TILE):
        up = jnp.dot(h, wup_ref[:, c * FF_TILE:(c + 1) * FF_TILE], preferred_element_type=F32)
        act = jnp.square(jnp.maximum(up, 0.0)).astype(BF16)
        acc = acc + jnp.dot(act, wdn_ref[c * FF_TILE:(c + 1) * FF_TILE, :], preferred_element_type=F32)
    out_ref[...] = x1 + _rms(acc, post_ref[...])


def _out_mlp(xf, oT, og, woa, wog, pmn, pre, post, wup, wdn, B, S):
    T = B * S
    nblk = S // MOBA_BLOCK
    tiles_per_seq = S // ROW_TILE
    assert ROW_TILE == 2 * MOBA_BLOCK
    const = lambda i: (0, 0)
    row = lambda i: (i, 0)

    def att_block(which):
        def index(i):
            blk = 2 * (i % tiles_per_seq) + which
            return (i // tiles_per_seq, _paired_pos(blk, nblk), 0, 0)
        return index

    single = dict(pipeline_mode=pl.Buffered(1))
    return pl.pallas_call(
        _out_mlp_kernel,
        grid=(T // ROW_TILE,),
        in_specs=[
            pl.BlockSpec((ROW_TILE, D_MODEL), row),
            pl.BlockSpec((1, 1, ATT_WIDTH, MOBA_BLOCK), att_block(0)),
            pl.BlockSpec((1, 1, ATT_WIDTH, MOBA_BLOCK), att_block(1)),
            pl.BlockSpec((ROW_TILE, GDN_WIDTH), row),
            pl.BlockSpec(woa.shape, const, **single),
            pl.BlockSpec(wog.shape, const, **single),
            pl.BlockSpec((1, D_MODEL), const),
            pl.BlockSpec((1, D_MODEL), const),
            pl.BlockSpec((1, D_MODEL), const),
            pl.BlockSpec(wup.shape, const, **single),
            pl.BlockSpec(wdn.shape, const, **single),
        ],
        out_specs=pl.BlockSpec((ROW_TILE, D_MODEL), row),
        out_shape=jax.ShapeDtypeStruct((T, D_MODEL), F32),
        compiler_params=pltpu.CompilerParams(
            dimension_semantics=("arbitrary",), vmem_limit_bytes=VMEM_LIMIT),
        name="out_mlp",
    )(xf, oT, oT, og, woa, wog, pmn, pre, post, wup, wdn)


def kernel(x, w_in, w_out, conv_w, A_log, dt_bias, gdn_norm_w, rel_bias, pre_mix_norm,
           post_mix_norm, pre_mlp_norm, post_mlp_norm, w_up, w_down):
    B, S, D = x.shape
    assert D == D_MODEL and S % ROW_TILE == 0 and S % MOBA_BLOCK == 0
    T = B * S
    depth = w_in.shape[0]
    xf = x.reshape(T, D)
    o0, o1, o2, o3, o4 = 0, ATT_WIDTH, 2 * ATT_WIDTH, 3 * ATT_WIDTH, 3 * ATT_WIDTH + 3 * GDN_WIDTH
    o5 = o4 + GDN_WIDTH
    for l in range(depth):
        wi = w_in[l]
        wqT = wi[:, o0:o1].T.astype(BF16)
        wk = wi[:, o1:o2].astype(BF16)
        wvT = wi[:, o2:o3].T.astype(BF16)
        wg = wi[:, o3:o4].astype(BF16)
        wz = wi[:, o4:o5].astype(BF16)
        wab = jnp.pad(wi[:, o5:], ((0, 0), (0, LANES - 2 * GDN_HEADS))).astype(BF16)
        pad8 = lambda v: jnp.pad(v.astype(F32), (0, LANES - GDN_HEADS))[None, :]
        qT, k, vT, gqkv, sz, gb = _inproj(xf, pre_mix_norm[l][None, :], wqT, wk, wvT, wg, wz, wab,
                                          conv_w[l], pad8(A_log[l]), pad8(dt_bias[l]), B, S)
        oT = _attention(rel_bias.astype(F32), qT, k.reshape(B, S, ATT_WIDTH), vT, B, S)
        og = _gdn(gqkv, sz, gb, jnp.tile(gdn_norm_w[l], GDN_HEADS)[None, :], B, S)
        wo = w_out[l].astype(BF16)
        xf = _out_mlp(xf, oT, og, wo[:ATT_WIDTH], wo[ATT_WIDTH:], post_mix_norm[l][None, :],
                      pre_mlp_norm[l][None, :], post_mlp_norm[l][None, :],
                      w_up[l].astype(BF16), w_down[l].astype(BF16), B, S)
    return xf.reshape(B, S, D)
```

```python
import functools
import math

import jax
import jax.numpy as jnp
from jax import lax
from jax.experimental import pallas as pl
from jax.experimental.pallas import tpu as pltpu

F32 = jnp.float32
BF16 = jnp.bfloat16
HI = lax.Precision.HIGHEST

D_MODEL = 1024
HEAD_DIM = 64
ATT_HEADS = 8
GDN_HEADS = 8
ATT_WIDTH = ATT_HEADS * HEAD_DIM
GDN_WIDTH = GDN_HEADS * HEAD_DIM
MOBA_BLOCK = 256
MOBA_TOPK = 3
GDN_CHUNK = 64
CONV_WIDTH = 4
D_FF = 4 * D_MODEL
REL_BUCKETS = 32
REL_MAX_EXACT = 16
REL_MAX_DIST = 128
EPS = 1e-6
NEG = -1e30
LOG2E = math.log2(math.e)

LANES = 128
SUBLANES = 8
VMEM_LIMIT = 56 * 1024 * 1024
ROW_TILE = 512
FF_TILE = 1024

NT = (((1,), (1,)), ((), ()))
TN = (((0,), (0,)), ((), ()))


def _bucket_lower_bounds():
    def bucket(d):
        if d < REL_MAX_EXACT:
            return d
        t = math.log(d / REL_MAX_EXACT) / math.log(REL_MAX_DIST / REL_MAX_EXACT)
        t = t * (REL_BUCKETS - REL_MAX_EXACT)
        assert d in (REL_MAX_EXACT, REL_MAX_DIST) or abs(t - round(t)) > 1e-6
        return min(REL_MAX_EXACT + int(t + 1e-9), REL_BUCKETS - 1)
    lower = []
    for b in range(REL_BUCKETS):
        d = 0
        while bucket(d) < b:
            d += 1
        lower.append(d)
    return lower


BUCKET_LOWER = _bucket_lower_bounds()


def _sigmoid(x):
    return 0.5 * jnp.tanh(0.5 * x) + 0.5


def _interleave(*stage_generators):
    live = list(stage_generators)
    while live:
        for gen in list(live):
            try:
                next(gen)
            except StopIteration:
                live.remove(gen)


def _rms(x, w):
    return x * lax.rsqrt(jnp.mean(x * x, axis=-1, keepdims=True) + EPS) * w


def _split_bf16(x, parts):
    out = []
    for _ in range(parts):
        h = x.astype(BF16)
        out.append(h)
        x = x - h.astype(F32)
    return out


def _dot_split_lhs(x, c, parts):
    acc = None
    for h in _split_bf16(x, parts):
        d = jnp.dot(h, c, preferred_element_type=F32)
        acc = d if acc is None else acc + d
    return acc


def _dot_split_rhs(c, x, parts):
    acc = None
    for h in _split_bf16(x, parts):
        d = jnp.dot(c, h, preferred_element_type=F32)
        acc = d if acc is None else acc + d
    return acc


CONV_COLS = 512


def _inproj_kernel(x_ref, xp_ref, nw_ref, wqT_ref, wk_ref, wvT_ref, wg_ref, wz_ref, wab_ref,
                   cw_ref, alog_ref, dtb_ref,
                   qT_ref, k_ref, vT_ref, g_ref, z_ref, gb_ref, *, tiles_per_seq):
    h = _rms(x_ref[...], nw_ref[...]).astype(BF16)

    hp = _rms(xp_ref[...], nw_ref[...]).astype(BF16)
    seq_start = (pl.program_id(0) % tiles_per_seq) == 0
    trow8 = lax.broadcasted_iota(jnp.int32, (SUBLANES, CONV_COLS), 0)
    for c in range(3 * GDN_WIDTH // CONV_COLS):
        cols = slice(c * CONV_COLS, (c + 1) * CONV_COLS)
        cur = jnp.dot(h, wg_ref[:, cols], preferred_element_type=F32)
        prev8 = jnp.dot(hp, wg_ref[:, cols], preferred_element_type=F32)
        prev8 = jnp.where(seq_start, 0.0, prev8)
        acc = cur * cw_ref[CONV_WIDTH - 1:CONV_WIDTH, cols]
        for s in range(1, CONV_WIDTH):
            rolled = pltpu.roll(cur, s, 0)
            top = jnp.where(trow8 < s, pltpu.roll(prev8, s, 0), rolled[0:SUBLANES])
            tap = jnp.concatenate([top, rolled[SUBLANES:]], axis=0)
            acc = acc + tap * cw_ref[CONV_WIDTH - 1 - s:CONV_WIDTH - s, cols]
        g_ref[:, cols] = acc * _sigmoid(acc)

    z = jnp.dot(h, wz_ref[...], preferred_element_type=F32)
    z_ref[...] = z * _sigmoid(z)
    ab = jnp.dot(h, wab_ref[...], preferred_element_type=F32)
    xs = ab + dtb_ref[...]
    log_decay = -jnp.exp(alog_ref[...]) * (jnp.maximum(xs, 0.0) + jnp.log1p(jnp.exp(-jnp.abs(xs))))
    lane = lax.broadcasted_iota(jnp.int32, ab.shape, 1)
    gb_ref[...] = jnp.where(lane < GDN_HEADS, log_decay, _sigmoid(ab))

    qT = lax.dot_general(wqT_ref[...], h, NT, preferred_element_type=F32)
    vT = lax.dot_general(wvT_ref[...], h, NT, preferred_element_type=F32)
    for t in range(ROW_TILE // MOBA_BLOCK):
        qT_ref[0, t] = qT[:, t * MOBA_BLOCK:(t + 1) * MOBA_BLOCK]
        vT_ref[0, t] = vT[:, t * MOBA_BLOCK:(t + 1) * MOBA_BLOCK].astype(BF16)
    k_ref[...] = jnp.dot(h, wk_ref[...], preferred_element_type=F32)


def _inproj(xf, nw, wqT, wk, wvT, wg, wz, wab, conv_w, alog_pad, dtb_pad, B, S):
    T = B * S
    nblk = S // MOBA_BLOCK
    tiles_per_seq = S // ROW_TILE
    blk_per_tile = ROW_TILE // MOBA_BLOCK
    const = lambda i: (0, 0)
    row = lambda i: (i, 0)
    tr = lambda i: (i // tiles_per_seq, i % tiles_per_seq, 0, 0)
    prev_rows = lambda i: (jnp.maximum(i * (ROW_TILE // SUBLANES) - 1, 0), 0)
    return pl.pallas_call(
        functools.partial(_inproj_kernel, tiles_per_seq=tiles_per_seq),
        grid=(T // ROW_TILE,),
        in_specs=[
            pl.BlockSpec((ROW_TILE, D_MODEL), row),
            pl.BlockSpec((SUBLANES, D_MODEL), prev_rows),
            pl.BlockSpec((1, D_MODEL), const),
            pl.BlockSpec(wqT.shape, const),
            pl.BlockSpec(wk.shape, const),
            pl.BlockSpec(wvT.shape, const),
            pl.BlockSpec(wg.shape, const),
            pl.BlockSpec(wz.shape, const),
            pl.BlockSpec(wab.shape, const),
            pl.BlockSpec(conv_w.shape, const),
            pl.BlockSpec((1, LANES), const),
            pl.BlockSpec((1, LANES), const),
        ],
        out_specs=[
            pl.BlockSpec((1, blk_per_tile, ATT_WIDTH, MOBA_BLOCK), tr),
            pl.BlockSpec((ROW_TILE, ATT_WIDTH), row),
            pl.BlockSpec((1, blk_per_tile, ATT_WIDTH, MOBA_BLOCK), tr),
            pl.BlockSpec((ROW_TILE, 3 * GDN_WIDTH), row),
            pl.BlockSpec((ROW_TILE, GDN_WIDTH), row),
            pl.BlockSpec((ROW_TILE, LANES), row),
        ],
        out_shape=[
            jax.ShapeDtypeStruct((B, nblk, ATT_WIDTH, MOBA_BLOCK), F32),
            jax.ShapeDtypeStruct((T, ATT_WIDTH), F32),
            jax.ShapeDtypeStruct((B, nblk, ATT_WIDTH, MOBA_BLOCK), BF16),
            jax.ShapeDtypeStruct((T, 3 * GDN_WIDTH), F32),
            jax.ShapeDtypeStruct((T, GDN_WIDTH), F32),
            jax.ShapeDtypeStruct((T, LANES), F32),
        ],
        compiler_params=pltpu.CompilerParams(
            dimension_semantics=("arbitrary",), vmem_limit_bytes=VMEM_LIMIT),
        name="inproj",
    )(xf, xf, nw, wqT, wk, wvT, wg, wz, wab, conv_w, alog_pad, dtb_pad)


V_ROWS = HEAD_DIM + 16


def _attn_item(i, nbatch, nblk):
    per_hp = nbatch * (nblk // 2)
    return i // per_hp, (i // (nblk // 2)) % nbatch, i % (nblk // 2)


def _attn_tiles(t, nblk):
    i_hi = nblk - 1 - t
    has_lo_prev = t >= 1
    n_lo_far = jnp.maximum(t - 1, 0)
    tiles = [(0, t, 0, False), (1, i_hi, 0, False), (1, i_hi - 1, 1, False),
             (jnp.where(has_lo_prev, 0, 1), jnp.where(has_lo_prev, t - 1, i_hi - 2),
              jnp.where(has_lo_prev, 1, 2), False)]
    for f in range(nblk - 3):
        f_lo = f < n_lo_far
        tiles.append((jnp.where(f_lo, 0, 1), jnp.where(f_lo, f, f - n_lo_far), None, True))
    return tiles


def _attn_kernel(relb_ref, qlo_ref, qhi_ref, k_ref, vT_ref, oT_ref,
                 kb_ref, km_ref, va_ref, bias_ref, addm_ref, qh_ref, lg_ref, moff_ref,
                 *, nblk, nbatch, nitems):
    step = pl.program_id(0)
    hp, b, t = _attn_item(jnp.minimum(step, nitems - 1), nbatch, nblk)
    vcur = (step // (nblk // 2)) % 2
    BLK = MOBA_BLOCK
    is_item = step < nitems

    @pl.when((b == 0) & (t == 0) & is_item)
    def _():
        kk = lax.broadcasted_iota(jnp.int32, (BLK, BLK), 0)
        qq = lax.broadcasted_iota(jnp.int32, (BLK, BLK), 1)
        for hh in range(2):
            h = 2 * hp + hh
            for kind in range(2):
                d = qq - kk + kind * BLK
                val = jnp.full((BLK, BLK), relb_ref[h, REL_BUCKETS - 1], F32)
                for bkt in range(REL_BUCKETS - 2, -1, -1):
                    val = jnp.where(d < BUCKET_LOWER[bkt + 1], relb_ref[h, bkt], val)
                val = val * LOG2E
                if kind == 0:
                    val = jnp.where(d >= 0, val, NEG)
                bias_ref[hh, kind] = val
            bias_ref[hh, 2] = jnp.full((BLK, BLK), relb_ref[h, REL_BUCKETS - 1] * LOG2E, F32)

    @pl.when((t == 0) & is_item)
    def _():
        lane = lax.broadcasted_iota(jnp.int32, (1, LANES), 1)
        ones_row = jnp.where(lax.broadcasted_iota(jnp.int32, (V_ROWS - HEAD_DIM, BLK), 0) == 0,
                             1.0, 0.0).astype(BF16)
        for j in range(nblk):
            kj = k_ref[0, j * BLK:(j + 1) * BLK, :]
            kb_ref[j * BLK:(j + 1) * BLK, :] = kj.astype(BF16)
            kmj = jnp.sum(kj, axis=0, keepdims=True) * (1.0 / BLK)
            km_ref[j:j + 1, :] = jnp.where(lane < HEAD_DIM, kmj, 0.0)
            km_ref[nblk + j:nblk + j + 1, :] = jnp.where(lane >= HEAD_DIM, kmj, 0.0)
            for hh in range(2):
                va_ref[vcur, j, hh, 0:HEAD_DIM, :] = vT_ref[0, j, HEAD_DIM * hh:HEAD_DIM * (hh + 1), :]
                va_ref[vcur, j, hh, HEAD_DIM:V_ROWS, :] = ones_row

    def to_lo(slot, x, other):
        if isinstance(slot, int):
            return x if slot == 0 else other
        return jnp.where(slot == 0, x, other)

    def score_stages(cur):
        ridx = lax.broadcasted_iota(jnp.int32, (nblk, BLK), 0)
        sub = lax.broadcasted_iota(jnp.int32, (LANES, BLK), 0)
        scale = HEAD_DIM ** -0.5 * LOG2E
        q_blocks = (t, nblk - 1 - t)
        for s, (q_ref, qi) in enumerate(zip((qlo_ref, qhi_ref), q_blocks)):
            qT = q_ref[0, 0]
            gT = jnp.dot(km_ref[...], qT, precision=HI, preferred_element_type=F32)
            past = ridx < qi
            for hh in range(2):
                gm = jnp.where(past, gT[nblk * hh:nblk * (hh + 1)], -jnp.inf)
                cnt = jnp.zeros((nblk, BLK), F32)
                for jp in range(nblk):
                    row = gm[jp:jp + 1, :]
                    beats = (row > gm) | ((row == gm) & (ridx > jp))
                    cnt = cnt + jnp.where(beats, 1.0, 0.0)
                visible = past & (cnt < MOBA_TOPK)
                addm_ref[s, nblk * hh:nblk * (hh + 1), :] = jnp.where(visible, 0.0, NEG)
                in_head = (sub >= HEAD_DIM * hh) & (sub < HEAD_DIM * (hh + 1))
                qh_ref[s, hh] = jnp.where(in_head, qT * scale, 0.0).astype(BF16)
        yield
        tiles = _attn_tiles(t, nblk)
        for hh in range(2):
            cmax = []
            rowoff = []
            for n, (slot, kblk, bias_kind, far) in enumerate(tiles):
                kj = kb_ref[pl.ds(pl.multiple_of(kblk * BLK, BLK), BLK), :]
                lg = jnp.dot(kj, qh_ref[slot, hh], preferred_element_type=F32)
                if bias_kind is not None:
                    lg = lg + bias_ref[hh, bias_kind]
                lg_ref[cur, hh, n] = lg
                cm = jnp.max(lg, axis=0, keepdims=True)
                off = None
                if n >= 2:
                    off = addm_ref[slot, pl.ds(nblk * hh + kblk, 1), :]
                    if far:
                        off = off + relb_ref[2 * hp + hh, REL_BUCKETS - 1] * LOG2E
                    cm = cm + off
                cmax.append(cm)
                rowoff.append(off)
                yield
            m_lo = cmax[0]
            m_hi = cmax[1]
            for n in range(2, len(tiles)):
                m_lo = jnp.maximum(m_lo, to_lo(tiles[n][0], cmax[n], -jnp.inf))
                m_hi = jnp.maximum(m_hi, to_lo(tiles[n][0], -jnp.inf, cmax[n]))
            for n, (slot, _, _, _) in enumerate(tiles):
                m_n = to_lo(slot, m_lo, m_hi)
                moff_ref[cur, hh, n:n + 1, :] = m_n if rowoff[n] is None else m_n - rowoff[n]

    def softmax_pv_stages(prv):
        _, _, t_prev = _attn_item(step - 1, nbatch, nblk)
        vprv = ((step - 1) // (nblk // 2)) % 2
        tiles = _attn_tiles(t_prev, nblk)
        for hh in range(2):
            acc_lo = None
            acc_hi = None
            for n, (slot, kblk, _, _) in enumerate(tiles):
                p = jnp.exp2(lg_ref[prv, hh, n] - moff_ref[prv, hh, n:n + 1, :])
                pvn = jnp.dot(va_ref[vprv, kblk, hh], p.astype(BF16),
                              preferred_element_type=F32)
                if isinstance(slot, int):
                    if slot == 0:
                        acc_lo = pvn if acc_lo is None else acc_lo + pvn
                    else:
                        acc_hi = pvn if acc_hi is None else acc_hi + pvn
                else:
                    acc_lo = acc_lo + jnp.where(slot == 0, pvn, 0.0)
                    acc_hi = acc_hi + jnp.where(slot == 0, 0.0, pvn)
                yield
            oT_ref[0, 0, HEAD_DIM * hh:HEAD_DIM * (hh + 1), :] = (
                acc_lo[0:HEAD_DIM] / acc_lo[HEAD_DIM:HEAD_DIM + 1])
            oT_ref[0, 1, HEAD_DIM * hh:HEAD_DIM * (hh + 1), :] = (
                acc_hi[0:HEAD_DIM] / acc_hi[HEAD_DIM:HEAD_DIM + 1])

    @pl.when(step == 0)
    def _():
        _interleave(score_stages(0))

    for parity in range(2):
        @pl.when((step > 0) & is_item & (step % 2 == parity))
        def _(parity=parity):
            _interleave(softmax_pv_stages(1 - parity), score_stages(parity))

    @pl.when(step == nitems)
    def _():
        _interleave(softmax_pv_stages((nitems - 1) % 2))


def _paired_pos(i, nblk):
    return jnp.where(i < nblk // 2, 2 * i, 2 * (nblk - 1 - i) + 1)


def _attention(rel_bias, qT, k3, vT, B, S):
    nblk = S // MOBA_BLOCK
    assert nblk % 2 == 0 and nblk >= 4
    assert BUCKET_LOWER[REL_BUCKETS - 1] <= MOBA_BLOCK + 1
    nitems = (ATT_HEADS // 2) * B * (nblk // 2)
    item = lambda s: _attn_item(jnp.minimum(s, nitems - 1), B, nblk)
    done = lambda s: _attn_item(jnp.maximum(s - 1, 0), B, nblk)

    def q_lo(s):
        hp, b, t = item(s)
        return (b, t, hp, 0)

    def q_hi(s):
        hp, b, t = item(s)
        return (b, nblk - 1 - t, hp, 0)

    def k_blk(s):
        hp, b, _ = item(s)
        return (b, 0, hp)

    def v_blk(s):
        hp, b, _ = item(s)
        return (b, 0, hp, 0)

    def o_blk(s):
        hp, b, t = done(s)
        return (b, t, hp, 0)

    return pl.pallas_call(
        functools.partial(_attn_kernel, nblk=nblk, nbatch=B, nitems=nitems),
        grid=(nitems + 1,),
        in_specs=[
            pl.BlockSpec(memory_space=pltpu.SMEM),
            pl.BlockSpec((1, 1, LANES, MOBA_BLOCK), q_lo),
            pl.BlockSpec((1, 1, LANES, MOBA_BLOCK), q_hi),
            pl.BlockSpec((1, S, LANES), k_blk),
            pl.BlockSpec((1, nblk, LANES, MOBA_BLOCK), v_blk),
        ],
        out_specs=pl.BlockSpec((1, 2, LANES, MOBA_BLOCK), o_blk),
        out_shape=jax.ShapeDtypeStruct((B, nblk, ATT_WIDTH, MOBA_BLOCK), F32),
        scratch_shapes=[
            pltpu.VMEM((S, LANES), BF16),
            pltpu.VMEM((2 * nblk, LANES), F32),
            pltpu.VMEM((2, nblk, 2, V_ROWS, MOBA_BLOCK), BF16),
            pltpu.VMEM((2, 3, MOBA_BLOCK, MOBA_BLOCK), F32),
            pltpu.VMEM((2, 2 * nblk, MOBA_BLOCK), F32),
            pltpu.VMEM((2, 2, LANES, MOBA_BLOCK), BF16),
            pltpu.VMEM((2, 2, nblk + 1, MOBA_BLOCK, MOBA_BLOCK), F32),
            pltpu.VMEM((2, 2, 2 * SUBLANES, MOBA_BLOCK), F32),
        ],
        compiler_params=pltpu.CompilerParams(
            dimension_semantics=("arbitrary",), vmem_limit_bytes=VMEM_LIMIT),
        name="moba_attn",
    )(rel_bias, qT, qT, k3, vT)


GDN_TILE = 256
GDN_HALF = 2 * LANES


def _gdn_kernel(yq_ref, yk_ref, yv_ref, gb_ref, sz_ref, nw_ref, out_ref,
                u_ref, wq_ref, a_ref, kd_ref, gl_ref, st_ref, *, tiles_per_seq, nsteps):
    step = pl.program_id(0)
    C = GDN_CHUNK
    W = GDN_WIDTH
    TILE = GDN_TILE
    npair = W // LANES

    r_w = lax.broadcasted_iota(jnp.int32, (GDN_HALF, GDN_HALF), 0)
    c_w = lax.broadcasted_iota(jnp.int32, (GDN_HALF, GDN_HALF), 1)
    head_ones = jnp.where((r_w // HEAD_DIM) == (c_w // HEAD_DIM), 1.0, 0.0).astype(BF16)
    ltri_bd = jnp.where(((r_w // C) == (c_w // C)) & (c_w <= r_w), 1.0, 0.0).astype(BF16)
    r_e = lax.broadcasted_iota(jnp.int32, (LANES, W), 0)
    c_e = lax.broadcasted_iota(jnp.int32, (LANES, W), 1)
    head_of_col = c_e // HEAD_DIM
    sel_g = jnp.where(r_e == head_of_col, 1.0, 0.0).astype(BF16)
    sel_beta = jnp.where(r_e == GDN_HEADS + head_of_col, 1.0, 0.0).astype(BF16)
    tok = lax.broadcasted_iota(jnp.int32, (TILE, W), 0) % C
    col = lax.broadcasted_iota(jnp.int32, (TILE, W), 1) % HEAD_DIM
    causal_t = tok >= col
    strict_t = tok > col

    lane = lax.broadcasted_iota(jnp.int32, (C, LANES), 1)
    rowi = lax.broadcasted_iota(jnp.int32, (C, LANES), 0)
    first_head = lane < HEAD_DIM
    strict = rowi > (lane % HEAD_DIM)
    eye2 = jnp.where(rowi == (lane % HEAD_DIM), 1.0, 0.0)
    lane2 = lax.broadcasted_iota(jnp.int32, (C, 2 * LANES), 1)
    first_head2 = (lane2 % LANES) < HEAD_DIM
    r_l = lax.broadcasted_iota(jnp.int32, (LANES, LANES), 0)
    c_l = lax.broadcasted_iota(jnp.int32, (LANES, LANES), 1)
    same_head = (r_l // HEAD_DIM) == (c_l // HEAD_DIM)
    pair_ones = jnp.where(same_head, 1.0, 0.0).astype(BF16)

    def stack(x, mask):
        return jnp.concatenate([jnp.where(mask, x, 0.0), jnp.where(mask, 0.0, x)], axis=0)

    dot = functools.partial(jnp.dot, preferred_element_type=F32)

    def head_sumsq(y):
        y2 = (y * y).astype(BF16)
        return jnp.concatenate([dot(y2[:, h:h + GDN_HALF], head_ones) for h in range(0, W, GDN_HALF)],
                               axis=1)

    def solve_stages(slot):
        yield
        yq = yq_ref[...]
        yk = yk_ref[...]
        yv = yv_ref[...]
        qn = yq * lax.rsqrt(head_sumsq(yq) + EPS) * (HEAD_DIM ** -0.5)
        kn = yk * lax.rsqrt(head_sumsq(yk) + EPS)
        gb2 = _split_bf16(gb_ref[...], 2)
        g = sum(dot(h, sel_g) for h in gb2)
        beta = sum(dot(h, sel_beta) for h in gb2)
        gcd = _dot_split_rhs(ltri_bd, jnp.concatenate([g, jnp.where(strict_t, g, 0.0)], axis=1), 2)
        gc = gcd[:, :W]
        decay = jnp.where(causal_t, jnp.exp(jnp.where(causal_t, gcd[:, W:], 0.0)), 0.0)
        egc = jnp.exp(gc)
        kb = kn * beta
        rv = yv * beta
        rk = kb * egc
        qd = qn * egc
        for cc in range(TILE // C):
            rs = slice(cc * C, (cc + 1) * C)
            g_last = gc[(cc + 1) * C - 1:(cc + 1) * C, :]
            kd_ref[slot, rs, :] = (kn[rs] * jnp.exp(g_last - gc[rs])).astype(BF16)
            gl_ref[slot, cc * SUBLANES:(cc + 1) * SUBLANES, :] = (
                jnp.broadcast_to(jnp.exp(g_last), (SUBLANES, W)))
        units = [(slice(cc * C, (cc + 1) * C), slice(LANES * p, LANES * (p + 1)), cc)
                 for cc in range(TILE // C) for p in range(npair)]
        kqs = [lax.dot_general(jnp.concatenate([kn[rs, ls], qn[rs, ls]], axis=0).astype(BF16),
                               stack(kn[rs, ls], first_head).astype(BF16), NT,
                               preferred_element_type=F32) for rs, ls, _ in units]
        yield
        ps = [-jnp.where(strict, kq[0:C] * beta[rs, ls] * decay[rs, ls], 0.0)
              for kq, (rs, ls, _) in zip(kqs, units)]
        ss = [eye2 + p for p in ps]
        ps = [dot(p.astype(BF16), stack(p, first_head).astype(BF16)) for p in ps]
        yield
        nround = int(math.log2(C))
        for k in range(1, nround):
            rhs = [stack(s_, first_head).astype(BF16) for s_ in ss]
            if k + 1 < nround:
                rhs = [jnp.concatenate([stack(p, first_head).astype(BF16), sx], axis=1)
                       for p, sx in zip(ps, rhs)]
            outs = [dot(p.astype(BF16), sx) for p, sx in zip(ps, rhs)]
            if k + 1 < nround:
                ps = [o[:, :LANES] for o in outs]
                ss = [s_ + o[:, LANES:] for s_, o in zip(ss, outs)]
            else:
                ss = [s_ + o for s_, o in zip(ss, outs)]
            yield
        xs = [dot(s_.astype(BF16),
                  stack(jnp.concatenate([rv[rs, ls], rk[rs, ls]], axis=1), first_head2).astype(BF16))
              for s_, (rs, ls, _) in zip(ss, units)]
        yield
        for x, kq, (rs, ls, cc) in zip(xs, kqs, units):
            u_ref[slot, rs, ls] = x[:, :LANES]
            wq_ref[slot, cc, 0:C, ls] = x[:, LANES:].astype(BF16)
            wq_ref[slot, cc, C:2 * C, ls] = qd[rs, ls].astype(BF16)
            a_ref[slot, rs, ls] = (kq[C:2 * C] * decay[rs, ls]).astype(BF16)

    lss = [slice(LANES * p, LANES * (p + 1)) for p in range(npair)]

    def recurrence_stages(slot, seq_start):
        states = [jnp.where(seq_start, 0.0, st_ref[p]) for p in range(npair)]
        pending = None

        def finish(rs, os_):
            for ls, o in zip(lss, os_):
                ms = dot((o * o).astype(BF16), pair_ones) * (1.0 / HEAD_DIM)
                out_ref[rs, ls] = o * lax.rsqrt(ms + EPS) * nw_ref[:, ls] * sz_ref[rs, ls]

        for cc in range(TILE // C):
            rs = slice(cc * C, (cc + 1) * C)
            wqs = [dot(wq_ref[slot, cc, :, ls], st.astype(BF16)) for ls, st in zip(lss, states)]
            if pending is not None:
                finish(*pending)
            yield
            v_news = [u_ref[slot, rs, ls] - wq[0:C] for ls, wq in zip(lss, wqs)]
            kvs = [lax.dot_general(kd_ref[slot, rs, ls], v.astype(BF16), TN, preferred_element_type=F32)
                   for ls, v in zip(lss, v_news)]
            os_ = [wq[C:2 * C] + dot(a_ref[slot, rs, ls], stack(v, first_head).astype(BF16))
                   for ls, wq, v in zip(lss, wqs, v_news)]
            states = [st * gl_ref[slot, cc * SUBLANES:cc * SUBLANES + 1, ls] + jnp.where(same_head, kv, 0.0)
                      for ls, st, kv in zip(lss, states, kvs)]
            pending = (rs, os_)
            yield
        finish(*pending)
        for p in range(npair):
            st_ref[p] = states[p]

    seq_start = ((step - 1) % tiles_per_seq) == 0

    @pl.when(step == 0)
    def _():
        st_ref[...] = jnp.zeros(st_ref.shape, F32)
        _interleave(solve_stages(0))

    for parity in range(2):
        @pl.when((step > 0) & (step < nsteps - 1) & (step % 2 == parity))
        def _(parity=parity):
            _interleave(recurrence_stages(1 - parity, seq_start), solve_stages(parity))

    @pl.when(step == nsteps - 1)
    def _():
        _interleave(recurrence_stages((nsteps - 2) % 2, seq_start))


def _gdn(gqkv, sz, gb, nw_row, B, S):
    T = B * S
    W = GDN_WIDTH
    TILE = GDN_TILE
    assert TILE == GDN_HALF and S % TILE == 0
    nchunk = TILE // GDN_CHUNK
    ntiles = T // TILE
    nsteps = ntiles + 1
    cur_tile = lambda off: (lambda s: (jnp.minimum(s, ntiles - 1), off))
    prev_tile = lambda s: (jnp.maximum(s - 1, 0), 0)
    return pl.pallas_call(
        functools.partial(_gdn_kernel, tiles_per_seq=S // TILE, nsteps=nsteps),
        grid=(nsteps,),
        in_specs=[
            pl.BlockSpec((TILE, W), cur_tile(0)),
            pl.BlockSpec((TILE, W), cur_tile(1)),
            pl.BlockSpec((TILE, W), cur_tile(2)),
            pl.BlockSpec((TILE, LANES), cur_tile(0)),
            pl.BlockSpec((TILE, W), prev_tile),
            pl.BlockSpec((1, W), lambda s: (0, 0)),
        ],
        out_specs=pl.BlockSpec((TILE, W), prev_tile),
        out_shape=jax.ShapeDtypeStruct((T, GDN_WIDTH), F32),
        scratch_shapes=[
            pltpu.VMEM((2, TILE, W), F32),
            pltpu.VMEM((2, nchunk, 2 * GDN_CHUNK, W), BF16),
            pltpu.VMEM((2, TILE, W), BF16),
            pltpu.VMEM((2, TILE, W), BF16),
            pltpu.VMEM((2, nchunk * SUBLANES, W), F32),
            pltpu.VMEM((W // LANES, LANES, LANES), F32),
        ],
        compiler_params=pltpu.CompilerParams(
            dimension_semantics=("arbitrary",), vmem_limit_bytes=VMEM_LIMIT),
        name="gdn",
    )(gqkv, gqkv, gqkv, gb, sz, nw_row)


def _out_mlp_kernel(x_ref, oTa_ref, oTb_ref, og_ref, woa_ref, wog_ref, pmn_ref, pre_ref, post_ref,
                    wup_ref, wdn_ref, out_ref):
    oT = jnp.concatenate([oTa_ref[0, 0], oTb_ref[0, 0]], axis=1)
    o_att = oT.T.astype(BF16)
    mix = jnp.dot(o_att, woa_ref[...], preferred_element_type=F32)
    mix = mix + jnp.dot(og_ref[...].astype(BF16), wog_ref[...], preferred_element_type=F32)
    x1 = x_ref[...] + _rms(mix, pmn_ref[...])
    h = _rms(x1, pre_ref[...]).astype(BF16)
    acc = jnp.zeros((ROW_TILE, D_MODEL), F32)
    for c in range(D_FF // FF_TILE):
        up = jnp.dot(h, wup_ref[:, c * FF_TILE:(c + 1) * FF_TILE], preferred_element_type=F32)
        act = jnp.square(jnp.maximum(up, 0.0)).astype(BF16)
        acc = acc + jnp.dot(act, wdn_ref[c * FF_TILE:(c + 1) * FF_TILE, :], preferred_element_type=F32)
    out_ref[...] = x1 + _rms(acc, post_ref[...])


def _out_mlp(xf, oT, og, woa, wog, pmn, pre, post, wup, wdn, B, S):
    T = B * S
    nblk = S // MOBA_BLOCK
    tiles_per_seq = S // ROW_TILE
    assert ROW_TILE == 2 * MOBA_BLOCK
    const = lambda i: (0, 0)
    row = lambda i: (i, 0)

    def att_block(which):
        def index(i):
            blk = 2 * (i % tiles_per_seq) + which
            return (i // tiles_per_seq, _paired_pos(blk, nblk), 0, 0)
        return index

    single = dict(pipeline_mode=pl.Buffered(1))
    return pl.pallas_call(
        _out_mlp_kernel,
        grid=(T // ROW_TILE,),
        in_specs=[
            pl.BlockSpec((ROW_TILE, D_MODEL), row),
            pl.BlockSpec((1, 1, ATT_WIDTH, MOBA_BLOCK), att_block(0)),
            pl.BlockSpec((1, 1, ATT_WIDTH, MOBA_BLOCK), att_block(1)),
            pl.BlockSpec((ROW_TILE, GDN_WIDTH), row),
            pl.BlockSpec(woa.shape, const, **single),
            pl.BlockSpec(wog.shape, const, **single),
            pl.BlockSpec((1, D_MODEL), const),
            pl.BlockSpec((1, D_MODEL), const),
            pl.BlockSpec((1, D_MODEL), const),
            pl.BlockSpec(wup.shape, const, **single),
            pl.BlockSpec(wdn.shape, const, **single),
        ],
        out_specs=pl.BlockSpec((ROW_TILE, D_MODEL), row),
        out_shape=jax.ShapeDtypeStruct((T, D_MODEL), F32),
        compiler_params=pltpu.CompilerParams(
            dimension_semantics=("arbitrary",), vmem_limit_bytes=VMEM_LIMIT),
        name="out_mlp",
    )(xf, oT, oT, og, woa, wog, pmn, pre, post, wup, wdn)


def kernel(x, w_in, w_out, conv_w, A_log, dt_bias, gdn_norm_w, rel_bias, pre_mix_norm,
           post_mix_norm, pre_mlp_norm, post_mlp_norm, w_up, w_down):
    B, S, D = x.shape
    assert D == D_MODEL and S % ROW_TILE == 0 and S % MOBA_BLOCK == 0
    T = B * S
    depth = w_in.shape[0]
    xf = x.reshape(T, D)
    o0, o1, o2, o3, o4 = 0, ATT_WIDTH, 2 * ATT_WIDTH, 3 * ATT_WIDTH, 3 * ATT_WIDTH + 3 * GDN_WIDTH
    o5 = o4 + GDN_WIDTH
    for l in range(depth):
        wi = w_in[l]
        wqT = wi[:, o0:o1].T.astype(BF16)
        wk = wi[:, o1:o2].astype(BF16)
        wvT = wi[:, o2:o3].T.astype(BF16)
        wg = wi[:, o3:o4].astype(BF16)
        wz = wi[:, o4:o5].astype(BF16)
        wab = jnp.pad(wi[:, o5:], ((0, 0), (0, LANES - 2 * GDN_HEADS))).astype(BF16)
        pad8 = lambda v: jnp.pad(v.astype(F32), (0, LANES - GDN_HEADS))[None, :]
        qT, k, vT, gqkv, sz, gb = _inproj(xf, pre_mix_norm[l][None, :], wqT, wk, wvT, wg, wz, wab,
                                          conv_w[l], pad8(A_log[l]), pad8(dt_bias[l]), B, S)
        oT = _attention(rel_bias.astype(F32), qT, k.reshape(B, S, ATT_WIDTH), vT, B, S)
        og = _gdn(gqkv, sz, gb, jnp.tile(gdn_norm_w[l], GDN_HEADS)[None, :], B, S)
        wo = w_out[l].astype(BF16)
        xf = _out_mlp(xf, oT, og, wo[:ATT_WIDTH], wo[ATT_WIDTH:], post_mix_norm[l][None, :],
                      pre_mlp_norm[l][None, :], post_mlp_norm[l][None, :],
                      w_up[l].astype(BF16), w_down[l].astype(BF16), B, S)
    return xf.reshape(B, S, D)
```

```python
import functools
import math

import jax
import jax.numpy as jnp
from jax import lax
from jax.experimental import pallas as pl
from jax.experimental.pallas import tpu as pltpu

F32 = jnp.float32
BF16 = jnp.bfloat16
HI = lax.Precision.HIGHEST

D_MODEL = 1024
HEAD_DIM = 64
ATT_HEADS = 8
GDN_HEADS = 8
ATT_WIDTH = ATT_HEADS * HEAD_DIM
GDN_WIDTH = GDN_HEADS * HEAD_DIM
MOBA_BLOCK = 256
MOBA_TOPK = 3
GDN_CHUNK = 64
CONV_WIDTH = 4
D_FF = 4 * D_MODEL
REL_BUCKETS = 32
REL_MAX_EXACT = 16
REL_MAX_DIST = 128
EPS = 1e-6
NEG = -1e30
LOG2E = math.log2(math.e)

LANES = 128
SUBLANES = 8
VMEM_LIMIT = 56 * 1024 * 1024
ROW_TILE = 512
INPROJ_TILE = 512
FF_TILE = 1024

NT = (((1,), (1,)), ((), ()))
TN = (((0,), (0,)), ((), ()))


def _bucket_lower_bounds():
    def bucket(d):
        if d < REL_MAX_EXACT:
            return d
        t = math.log(d / REL_MAX_EXACT) / math.log(REL_MAX_DIST / REL_MAX_EXACT)
        t = t * (REL_BUCKETS - REL_MAX_EXACT)
        assert d in (REL_MAX_EXACT, REL_MAX_DIST) or abs(t - round(t)) > 1e-6
        return min(REL_MAX_EXACT + int(t + 1e-9), REL_BUCKETS - 1)
    lower = []
    for b in range(REL_BUCKETS):
        d = 0
        while bucket(d) < b:
            d += 1
        lower.append(d)
    return lower


BUCKET_LOWER = _bucket_lower_bounds()


def _sigmoid(x):
    return 0.5 * jnp.tanh(0.5 * x) + 0.5


def _silu_of_half(h):
    return h + h * jnp.tanh(h)


def _interleave(*stage_generators):
    live = list(stage_generators)
    while live:
        for gen in list(live):
            try:
                next(gen)
            except StopIteration:
                live.remove(gen)


def _rms(x, w):
    return x * lax.rsqrt(jnp.mean(x * x, axis=-1, keepdims=True) + EPS) * w


def _split_bf16(x, parts):
    out = []
    for _ in range(parts):
        h = x.astype(BF16)
        out.append(h)
        x = x - h.astype(F32)
    return out


def _dot_split_lhs(x, c, parts):
    acc = None
    for h in _split_bf16(x, parts):
        d = jnp.dot(h, c, preferred_element_type=F32)
        acc = d if acc is None else acc + d
    return acc


def _dot_split_rhs(c, x, parts):
    acc = None
    for h in _split_bf16(x, parts):
        d = jnp.dot(c, h, preferred_element_type=F32)
        acc = d if acc is None else acc + d
    return acc


CONV_COLS = 512


def _inproj_kernel(x_ref, xp_ref, nw_ref, wqT_ref, wk_ref, wvT_ref, wg_ref, wz_ref, wab_ref,
                   cw_ref, alog_ref, dtb_ref,
                   qT_ref, k_ref, vT_ref, g_ref, z_ref, gb_ref, *, tiles_per_seq):
    h = _rms(x_ref[...], nw_ref[...]).astype(BF16)

    hp = _rms(xp_ref[...], nw_ref[...]).astype(BF16)
    seq_start = (pl.program_id(0) % tiles_per_seq) == 0
    trow8 = lax.broadcasted_iota(jnp.int32, (SUBLANES, CONV_COLS), 0)
    for c in range(3 * GDN_WIDTH // CONV_COLS):
        cols = slice(c * CONV_COLS, (c + 1) * CONV_COLS)
        cur = jnp.dot(h, wg_ref[:, cols], preferred_element_type=F32)
        prev8 = jnp.dot(hp, wg_ref[:, cols], preferred_element_type=F32)
        prev8 = jnp.where(seq_start, 0.0, prev8)
        cw_half = 0.5 * cw_ref[:, cols]
        acc = cur * cw_half[CONV_WIDTH - 1:CONV_WIDTH]
        for s in range(1, CONV_WIDTH):
            rolled = pltpu.roll(cur, s, 0)
            top = jnp.where(trow8 < s, pltpu.roll(prev8, s, 0), rolled[0:SUBLANES])
            tap = jnp.concatenate([top, rolled[SUBLANES:]], axis=0)
            acc = acc + tap * cw_half[CONV_WIDTH - 1 - s:CONV_WIDTH - s]
        g_ref[:, cols] = _silu_of_half(acc)

    z = jnp.dot(h, wz_ref[...], preferred_element_type=F32)
    z_ref[...] = _silu_of_half(0.5 * z)
    ab = jnp.dot(h, wab_ref[...], preferred_element_type=F32)
    xs = ab + dtb_ref[...]
    log_decay = -jnp.exp(alog_ref[...]) * (jnp.maximum(xs, 0.0) + jnp.log1p(jnp.exp(-jnp.abs(xs))))
    lane = lax.broadcasted_iota(jnp.int32, ab.shape, 1)
    gb_ref[...] = jnp.where(lane < GDN_HEADS, log_decay, _sigmoid(ab))

    qT = lax.dot_general(wqT_ref[...], h, NT, preferred_element_type=F32)
    vT = lax.dot_general(wvT_ref[...], h, NT, preferred_element_type=F32)
    for t in range(INPROJ_TILE // MOBA_BLOCK):
        qT_ref[0, t] = qT[:, t * MOBA_BLOCK:(t + 1) * MOBA_BLOCK]
        vT_ref[0, t] = vT[:, t * MOBA_BLOCK:(t + 1) * MOBA_BLOCK].astype(BF16)
    k_ref[...] = jnp.dot(h, wk_ref[...], preferred_element_type=F32)


def _inproj(xf, nw, wqT, wk, wvT, wg, wz, wab, conv_w, alog_pad, dtb_pad, B, S):
    T = B * S
    TM = INPROJ_TILE
    assert S % TM == 0
    nblk = S // MOBA_BLOCK
    tiles_per_seq = S // TM
    blk_per_tile = TM // MOBA_BLOCK
    const = lambda i: (0, 0)
    row = lambda i: (i, 0)
    tr = lambda i: (i // tiles_per_seq, i % tiles_per_seq, 0, 0)
    prev_rows = lambda i: (jnp.maximum(i * (TM // SUBLANES) - 1, 0), 0)
    single = dict(pipeline_mode=pl.Buffered(1))
    return pl.pallas_call(
        functools.partial(_inproj_kernel, tiles_per_seq=tiles_per_seq),
        grid=(T // TM,),
        in_specs=[
            pl.BlockSpec((TM, D_MODEL), row),
            pl.BlockSpec((SUBLANES, D_MODEL), prev_rows),
            pl.BlockSpec((1, D_MODEL), const),
            pl.BlockSpec(wqT.shape, const, **single),
            pl.BlockSpec(wk.shape, const, **single),
            pl.BlockSpec(wvT.shape, const, **single),
            pl.BlockSpec(wg.shape, const, **single),
            pl.BlockSpec(wz.shape, const, **single),
            pl.BlockSpec(wab.shape, const, **single),
            pl.BlockSpec(conv_w.shape, const),
            pl.BlockSpec((1, LANES), const),
            pl.BlockSpec((1, LANES), const),
        ],
        out_specs=[
            pl.BlockSpec((1, blk_per_tile, ATT_WIDTH, MOBA_BLOCK), tr),
            pl.BlockSpec((TM, ATT_WIDTH), row),
            pl.BlockSpec((1, blk_per_tile, ATT_WIDTH, MOBA_BLOCK), tr),
            pl.BlockSpec((TM, 3 * GDN_WIDTH), row),
            pl.BlockSpec((TM, GDN_WIDTH), row),
            pl.BlockSpec((TM, LANES), row),
        ],
        out_shape=[
            jax.ShapeDtypeStruct((B, nblk, ATT_WIDTH, MOBA_BLOCK), F32),
            jax.ShapeDtypeStruct((T, ATT_WIDTH), F32),
            jax.ShapeDtypeStruct((B, nblk, ATT_WIDTH, MOBA_BLOCK), BF16),
            jax.ShapeDtypeStruct((T, 3 * GDN_WIDTH), F32),
            jax.ShapeDtypeStruct((T, GDN_WIDTH), F32),
            jax.ShapeDtypeStruct((T, LANES), F32),
        ],
        compiler_params=pltpu.CompilerParams(
            dimension_semantics=("arbitrary",), vmem_limit_bytes=VMEM_LIMIT),
        name="inproj",
    )(xf, xf, nw, wqT, wk, wvT, wg, wz, wab, conv_w, alog_pad, dtb_pad)


V_ROWS = HEAD_DIM + 16


def _attn_item(i, nbatch, nblk):
    per_hp = nbatch * (nblk // 2)
    return i // per_hp, (i // (nblk // 2)) % nbatch, i % (nblk // 2)


def _attn_tiles(t, nblk):
    i_hi = nblk - 1 - t
    has_lo_prev = t >= 1
    n_lo_far = jnp.maximum(t - 1, 0)
    tiles = [(0, t, 0, False), (1, i_hi, 0, False), (1, i_hi - 1, 1, False),
             (jnp.where(has_lo_prev, 0, 1), jnp.where(has_lo_prev, t - 1, i_hi - 2),
              jnp.where(has_lo_prev, 1, 2), False)]
    for f in range(nblk - 3):
        f_lo = f < n_lo_far
        tiles.append((jnp.where(f_lo, 0, 1), jnp.where(f_lo, f, f - n_lo_far), None, True))
    return tiles


def _attn_kernel(relb_ref, qlo_ref, qhi_ref, k_ref, vT_ref, oT_ref,
                 kb_ref, km_ref, va_ref, bias_ref, addm_ref, qh_ref, lg_ref, moff_ref,
                 *, nblk, nbatch, nitems):
    step = pl.program_id(0)
    hp, b, t = _attn_item(jnp.minimum(step, nitems - 1), nbatch, nblk)
    vcur = (step // (nblk // 2)) % 2
    BLK = MOBA_BLOCK
    is_item = step < nitems

    @pl.when((b == 0) & (t == 0) & is_item)
    def _():
        kk = lax.broadcasted_iota(jnp.int32, (BLK, BLK), 0)
        qq = lax.broadcasted_iota(jnp.int32, (BLK, BLK), 1)
        for hh in range(2):
            h = 2 * hp + hh
            for kind in range(2):
                d = qq - kk + kind * BLK
                val = jnp.full((BLK, BLK), relb_ref[h, REL_BUCKETS - 1], F32)
                for bkt in range(REL_BUCKETS - 2, -1, -1):
                    val = jnp.where(d < BUCKET_LOWER[bkt + 1], relb_ref[h, bkt], val)
                val = val * LOG2E
                if kind == 0:
                    val = jnp.where(d >= 0, val, NEG)
                bias_ref[hh, kind] = val
            bias_ref[hh, 2] = jnp.full((BLK, BLK), relb_ref[h, REL_BUCKETS - 1] * LOG2E, F32)

    @pl.when((t == 0) & is_item)
    def _():
        lane = lax.broadcasted_iota(jnp.int32, (1, LANES), 1)
        ones_row = jnp.where(lax.broadcasted_iota(jnp.int32, (V_ROWS - HEAD_DIM, BLK), 0) == 0,
                             1.0, 0.0).astype(BF16)
        for j in range(nblk):
            kj = k_ref[0, j * BLK:(j + 1) * BLK, :]
            kb_ref[j * BLK:(j + 1) * BLK, :] = kj.astype(BF16)
            kmj = jnp.sum(kj, axis=0, keepdims=True) * (1.0 / BLK)
            km_ref[j:j + 1, :] = jnp.where(lane < HEAD_DIM, kmj, 0.0)
            km_ref[nblk + j:nblk + j + 1, :] = jnp.where(lane >= HEAD_DIM, kmj, 0.0)
            for hh in range(2):
                va_ref[vcur, j, hh, 0:HEAD_DIM, :] = vT_ref[0, j, HEAD_DIM * hh:HEAD_DIM * (hh + 1), :]
                va_ref[vcur, j, hh, HEAD_DIM:V_ROWS, :] = ones_row

    def to_lo(slot, x, other):
        if isinstance(slot, int):
            return x if slot == 0 else other
        return jnp.where(slot == 0, x, other)

    def score_stages(cur):
        ridx = lax.broadcasted_iota(jnp.int32, (nblk, BLK), 0)
        sub = lax.broadcasted_iota(jnp.int32, (LANES, BLK), 0)
        scale = HEAD_DIM ** -0.5 * LOG2E
        q_blocks = (t, nblk - 1 - t)
        for s, (q_ref, qi) in enumerate(zip((qlo_ref, qhi_ref), q_blocks)):
            qT = q_ref[0, 0]
            gT = jnp.dot(km_ref[...], qT, precision=HI, preferred_element_type=F32)
            past = ridx < qi
            for hh in range(2):
                gm = jnp.where(past, gT[nblk * hh:nblk * (hh + 1)], -jnp.inf)
                cnt = jnp.zeros((nblk, BLK), F32)
                for jp in range(nblk):
                    row = gm[jp:jp + 1, :]
                    beats = (row > gm) | ((row == gm) & (ridx > jp))
                    cnt = cnt + jnp.where(beats, 1.0, 0.0)
                visible = past & (cnt < MOBA_TOPK)
                addm_ref[s, nblk * hh:nblk * (hh + 1), :] = jnp.where(visible, 0.0, NEG)
                in_head = (sub >= HEAD_DIM * hh) & (sub < HEAD_DIM * (hh + 1))
                qh_ref[s, hh] = jnp.where(in_head, qT * scale, 0.0).astype(BF16)
        yield
        tiles = _attn_tiles(t, nblk)
        for hh in range(2):
            cmax = []
            rowoff = []
            for n, (slot, kblk, bias_kind, far) in enumerate(tiles):
                kj = kb_ref[pl.ds(pl.multiple_of(kblk * BLK, BLK), BLK), :]
                lg = jnp.dot(kj, qh_ref[slot, hh], preferred_element_type=F32)
                if bias_kind is not None:
                    lg = lg + bias_ref[hh, bias_kind]
                lg_ref[cur, hh, n] = lg
                cm = jnp.max(lg, axis=0, keepdims=True)
                off = None
                if n >= 2:
                    off = addm_ref[slot, pl.ds(nblk * hh + kblk, 1), :]
                    if far:
                        off = off + relb_ref[2 * hp + hh, REL_BUCKETS - 1] * LOG2E
                    cm = cm + off
                cmax.append(cm)
                rowoff.append(off)
                yield
            m_lo = cmax[0]
            m_hi = cmax[1]
            for n in range(2, len(tiles)):
                m_lo = jnp.maximum(m_lo, to_lo(tiles[n][0], cmax[n], -jnp.inf))
                m_hi = jnp.maximum(m_hi, to_lo(tiles[n][0], -jnp.inf, cmax[n]))
            for n, (slot, _, _, _) in enumerate(tiles):
                m_n = to_lo(slot, m_lo, m_hi)
                moff_ref[cur, hh, n:n + 1, :] = m_n if rowoff[n] is None else m_n - rowoff[n]

    def softmax_pv_stages(prv):
        _, _, t_prev = _attn_item(step - 1, nbatch, nblk)
        vprv = ((step - 1) // (nblk // 2)) % 2
        tiles = _attn_tiles(t_prev, nblk)
        for hh in range(2):
            acc_lo = None
            acc_hi = None
            for n, (slot, kblk, _, _) in enumerate(tiles):
                p = jnp.exp2(lg_ref[prv, hh, n] - moff_ref[prv, hh, n:n + 1, :])
                pvn = jnp.dot(va_ref[vprv, kblk, hh], p.astype(BF16),
                              preferred_element_type=F32)
                if isinstance(slot, int):
                    if slot == 0:
                        acc_lo = pvn if acc_lo is None else acc_lo + pvn
                    else:
                        acc_hi = pvn if acc_hi is None else acc_hi + pvn
                else:
                    acc_lo = acc_lo + jnp.where(slot == 0, pvn, 0.0)
                    acc_hi = acc_hi + jnp.where(slot == 0, 0.0, pvn)
                yield
            oT_ref[0, 0, HEAD_DIM * hh:HEAD_DIM * (hh + 1), :] = (
                acc_lo[0:HEAD_DIM] / acc_lo[HEAD_DIM:HEAD_DIM + 1])
            oT_ref[0, 1, HEAD_DIM * hh:HEAD_DIM * (hh + 1), :] = (
                acc_hi[0:HEAD_DIM] / acc_hi[HEAD_DIM:HEAD_DIM + 1])

    @pl.when(step == 0)
    def _():
        _interleave(score_stages(0))

    for parity in range(2):
        @pl.when((step > 0) & is_item & (step % 2 == parity))
        def _(parity=parity):
            _interleave(softmax_pv_stages(1 - parity), score_stages(parity))

    @pl.when(step == nitems)
    def _():
        _interleave(softmax_pv_stages((nitems - 1) % 2))


def _paired_pos(i, nblk):
    return jnp.where(i < nblk // 2, 2 * i, 2 * (nblk - 1 - i) + 1)


def _attention(rel_bias, qT, k3, vT, B, S):
    nblk = S // MOBA_BLOCK
    assert nblk % 2 == 0 and nblk >= 4
    assert BUCKET_LOWER[REL_BUCKETS - 1] <= MOBA_BLOCK + 1
    nitems = (ATT_HEADS // 2) * B * (nblk // 2)
    item = lambda s: _attn_item(jnp.minimum(s, nitems - 1), B, nblk)
    done = lambda s: _attn_item(jnp.maximum(s - 1, 0), B, nblk)

    def q_lo(s):
        hp, b, t = item(s)
        return (b, t, hp, 0)

    def q_hi(s):
        hp, b, t = item(s)
        return (b, nblk - 1 - t, hp, 0)

    def k_blk(s):
        hp, b, _ = item(s)
        return (b, 0, hp)

    def v_blk(s):
        hp, b, _ = item(s)
        return (b, 0, hp, 0)

    def o_blk(s):
        hp, b, t = done(s)
        return (b, t, hp, 0)

    return pl.pallas_call(
        functools.partial(_attn_kernel, nblk=nblk, nbatch=B, nitems=nitems),
        grid=(nitems + 1,),
        in_specs=[
            pl.BlockSpec(memory_space=pltpu.SMEM),
            pl.BlockSpec((1, 1, LANES, MOBA_BLOCK), q_lo),
            pl.BlockSpec((1, 1, LANES, MOBA_BLOCK), q_hi),
            pl.BlockSpec((1, S, LANES), k_blk),
            pl.BlockSpec((1, nblk, LANES, MOBA_BLOCK), v_blk),
        ],
        out_specs=pl.BlockSpec((1, 2, LANES, MOBA_BLOCK), o_blk),
        out_shape=jax.ShapeDtypeStruct((B, nblk, ATT_WIDTH, MOBA_BLOCK), F32),
        scratch_shapes=[
            pltpu.VMEM((S, LANES), BF16),
            pltpu.VMEM((2 * nblk, LANES), F32),
            pltpu.VMEM((2, nblk, 2, V_ROWS, MOBA_BLOCK), BF16),
            pltpu.VMEM((2, 3, MOBA_BLOCK, MOBA_BLOCK), F32),
            pltpu.VMEM((2, 2 * nblk, MOBA_BLOCK), F32),
            pltpu.VMEM((2, 2, LANES, MOBA_BLOCK), BF16),
            pltpu.VMEM((2, 2, nblk + 1, MOBA_BLOCK, MOBA_BLOCK), F32),
            pltpu.VMEM((2, 2, 2 * SUBLANES, MOBA_BLOCK), F32),
        ],
        compiler_params=pltpu.CompilerParams(
            dimension_semantics=("arbitrary",), vmem_limit_bytes=VMEM_LIMIT),
        name="moba_attn",
    )(rel_bias, qT, qT, k3, vT)


GDN_TILE = 256
GDN_HALF = 2 * LANES


def _gdn_kernel(yq_ref, yk_ref, yv_ref, gb_ref, sz_ref, nw_ref, out_ref,
                u_ref, wq_ref, a_ref, kd_ref, gl_ref, st_ref, *, tiles_per_seq, nsteps):
    step = pl.program_id(0)
    C = GDN_CHUNK
    W = GDN_WIDTH
    TILE = GDN_TILE
    npair = W // LANES

    r_w = lax.broadcasted_iota(jnp.int32, (GDN_HALF, GDN_HALF), 0)
    c_w = lax.broadcasted_iota(jnp.int32, (GDN_HALF, GDN_HALF), 1)
    head_ones = jnp.where((r_w // HEAD_DIM) == (c_w // HEAD_DIM), 1.0, 0.0).astype(BF16)
    ltri_bd = jnp.where(((r_w // C) == (c_w // C)) & (c_w <= r_w), 1.0, 0.0).astype(BF16)
    r_e = lax.broadcasted_iota(jnp.int32, (LANES, W), 0)
    c_e = lax.broadcasted_iota(jnp.int32, (LANES, W), 1)
    head_of_col = c_e // HEAD_DIM
    sel_g = jnp.where(r_e == head_of_col, 1.0, 0.0).astype(BF16)
    sel_beta = jnp.where(r_e == GDN_HEADS + head_of_col, 1.0, 0.0).astype(BF16)
    tok = lax.broadcasted_iota(jnp.int32, (TILE, W), 0) % C
    col = lax.broadcasted_iota(jnp.int32, (TILE, W), 1) % HEAD_DIM
    causal_t = tok >= col
    strict_t = tok > col

    lane = lax.broadcasted_iota(jnp.int32, (C, LANES), 1)
    rowi = lax.broadcasted_iota(jnp.int32, (C, LANES), 0)
    first_head = lane < HEAD_DIM
    strict = rowi > (lane % HEAD_DIM)
    eye2 = jnp.where(rowi == (lane % HEAD_DIM), 1.0, 0.0)
    lane2 = lax.broadcasted_iota(jnp.int32, (C, 2 * LANES), 1)
    first_head2 = (lane2 % LANES) < HEAD_DIM
    r_l = lax.broadcasted_iota(jnp.int32, (LANES, LANES), 0)
    c_l = lax.broadcasted_iota(jnp.int32, (LANES, LANES), 1)
    same_head = (r_l // HEAD_DIM) == (c_l // HEAD_DIM)
    pair_ones = jnp.where(same_head, 1.0, 0.0).astype(BF16)

    def stack(x, mask):
        return jnp.concatenate([jnp.where(mask, x, 0.0), jnp.where(mask, 0.0, x)], axis=0)

    dot = functools.partial(jnp.dot, preferred_element_type=F32)

    def head_sumsq(y):
        y2 = (y * y).astype(BF16)
        return jnp.concatenate([dot(y2[:, h:h + GDN_HALF], head_ones) for h in range(0, W, GDN_HALF)],
                               axis=1)

    def solve_stages(slot):
        yield
        yq = yq_ref[...]
        yk = yk_ref[...]
        yv = yv_ref[...]
        qn = yq * lax.rsqrt(head_sumsq(yq) + EPS) * (HEAD_DIM ** -0.5)
        kn = yk * lax.rsqrt(head_sumsq(yk) + EPS)
        gb2 = _split_bf16(gb_ref[...], 2)
        g = sum(dot(h, sel_g) for h in gb2)
        beta = sum(dot(h, sel_beta) for h in gb2)
        gcd = _dot_split_rhs(ltri_bd, jnp.concatenate([g, jnp.where(strict_t, g, 0.0)], axis=1), 2)
        gc = gcd[:, :W]
        decay = jnp.where(causal_t, jnp.exp(jnp.where(causal_t, gcd[:, W:], 0.0)), 0.0)
        egc = jnp.exp(gc)
        kb = kn * beta
        rv = yv * beta
        rk = kb * egc
        qd = qn * egc
        for cc in range(TILE // C):
            rs = slice(cc * C, (cc + 1) * C)
            g_last = gc[(cc + 1) * C - 1:(cc + 1) * C, :]
            kd_ref[slot, rs, :] = (kn[rs] * jnp.exp(g_last - gc[rs])).astype(BF16)
            gl_ref[slot, cc * SUBLANES:(cc + 1) * SUBLANES, :] = (
                jnp.broadcast_to(jnp.exp(g_last), (SUBLANES, W)))
        units = [(slice(cc * C, (cc + 1) * C), slice(LANES * p, LANES * (p + 1)), cc)
                 for cc in range(TILE // C) for p in range(npair)]
        kqs = [lax.dot_general(jnp.concatenate([kn[rs, ls], qn[rs, ls]], axis=0).astype(BF16),
                               stack(kn[rs, ls], first_head).astype(BF16), NT,
                               preferred_element_type=F32) for rs, ls, _ in units]
        yield
        ps = [-jnp.where(strict, kq[0:C] * beta[rs, ls] * decay[rs, ls], 0.0)
              for kq, (rs, ls, _) in zip(kqs, units)]
        ss = [eye2 + p for p in ps]
        ps = [dot(p.astype(BF16), stack(p, first_head).astype(BF16)) for p in ps]
        yield
        nround = int(math.log2(C))
        for k in range(1, nround):
            rhs = [stack(s_, first_head).astype(BF16) for s_ in ss]
            if k + 1 < nround:
                rhs = [jnp.concatenate([stack(p, first_head).astype(BF16), sx], axis=1)
                       for p, sx in zip(ps, rhs)]
            outs = [dot(p.astype(BF16), sx) for p, sx in zip(ps, rhs)]
            if k + 1 < nround:
                ps = [o[:, :LANES] for o in outs]
                ss = [s_ + o[:, LANES:] for s_, o in zip(ss, outs)]
            else:
                ss = [s_ + o for s_, o in zip(ss, outs)]
            yield
        xs = [dot(s_.astype(BF16),
                  stack(jnp.concatenate([rv[rs, ls], rk[rs, ls]], axis=1), first_head2).astype(BF16))
              for s_, (rs, ls, _) in zip(ss, units)]
        yield
        for x, kq, (rs, ls, cc) in zip(xs, kqs, units):
            u_ref[slot, rs, ls] = x[:, :LANES]
            wq_ref[slot, cc, 0:C, ls] = x[:, LANES:].astype(BF16)
            wq_ref[slot, cc, C:2 * C, ls] = qd[rs, ls].astype(BF16)
            a_ref[slot, rs, ls] = (kq[C:2 * C] * decay[rs, ls]).astype(BF16)

    lss = [slice(LANES * p, LANES * (p + 1)) for p in range(npair)]

    def recurrence_stages(slot, seq_start):
        states = [jnp.where(seq_start, 0.0, st_ref[p]) for p in range(npair)]
        pending = None

        def finish(rs, os_):
            for ls, o in zip(lss, os_):
                ms = dot((o * o).astype(BF16), pair_ones) * (1.0 / HEAD_DIM)
                out_ref[rs, ls] = o * lax.rsqrt(ms + EPS) * nw_ref[:, ls] * sz_ref[rs, ls]

        for cc in range(TILE // C):
            rs = slice(cc * C, (cc + 1) * C)
            wqs = [dot(wq_ref[slot, cc, :, ls], st.astype(BF16)) for ls, st in zip(lss, states)]
            if pending is not None:
                finish(*pending)
            yield
            v_news = [u_ref[slot, rs, ls] - wq[0:C] for ls, wq in zip(lss, wqs)]
            kvs = [lax.dot_general(kd_ref[slot, rs, ls], v.astype(BF16), TN, preferred_element_type=F32)
                   for ls, v in zip(lss, v_news)]
            os_ = [wq[C:2 * C] + dot(a_ref[slot, rs, ls], stack(v, first_head).astype(BF16))
                   for ls, wq, v in zip(lss, wqs, v_news)]
            states = [st * gl_ref[slot, cc * SUBLANES:cc * SUBLANES + 1, ls] + jnp.where(same_head, kv, 0.0)
                      for ls, st, kv in zip(lss, states, kvs)]
            pending = (rs, os_)
            yield
        finish(*pending)
        for p in range(npair):
            st_ref[p] = states[p]

    seq_start = ((step - 1) % tiles_per_seq) == 0

    @pl.when(step == 0)
    def _():
        st_ref[...] = jnp.zeros(st_ref.shape, F32)
        _interleave(solve_stages(0))

    for parity in range(2):
        @pl.when((step > 0) & (step < nsteps - 1) & (step % 2 == parity))
        def _(parity=parity):
            _interleave(recurrence_stages(1 - parity, seq_start), solve_stages(parity))

    @pl.when(step == nsteps - 1)
    def _():
        _interleave(recurrence_stages((nsteps - 2) % 2, seq_start))


def _gdn(gqkv, sz, gb, nw_row, B, S):
    T = B * S
    W = GDN_WIDTH
    TILE = GDN_TILE
    assert TILE == GDN_HALF and S % TILE == 0
    nchunk = TILE // GDN_CHUNK
    ntiles = T // TILE
    nsteps = ntiles + 1
    cur_tile = lambda off: (lambda s: (jnp.minimum(s, ntiles - 1), off))
    prev_tile = lambda s: (jnp.maximum(s - 1, 0), 0)
    return pl.pallas_call(
        functools.partial(_gdn_kernel, tiles_per_seq=S // TILE, nsteps=nsteps),
        grid=(nsteps,),
        in_specs=[
            pl.BlockSpec((TILE, W), cur_tile(0)),
            pl.BlockSpec((TILE, W), cur_tile(1)),
            pl.BlockSpec((TILE, W), cur_tile(2)),
            pl.BlockSpec((TILE, LANES), cur_tile(0)),
            pl.BlockSpec((TILE, W), prev_tile),
            pl.BlockSpec((1, W), lambda s: (0, 0)),
        ],
        out_specs=pl.BlockSpec((TILE, W), prev_tile),
        out_shape=jax.ShapeDtypeStruct((T, GDN_WIDTH), F32),
        scratch_shapes=[
            pltpu.VMEM((2, TILE, W), F32),
            pltpu.VMEM((2, nchunk, 2 * GDN_CHUNK, W), BF16),
            pltpu.VMEM((2, TILE, W), BF16),
            pltpu.VMEM((2, TILE, W), BF16),
            pltpu.VMEM((2, nchunk * SUBLANES, W), F32),
            pltpu.VMEM((W // LANES, LANES, LANES), F32),
        ],
        compiler_params=pltpu.CompilerParams(
            dimension_semantics=("arbitrary",), vmem_limit_bytes=VMEM_LIMIT),
        name="gdn",
    )(gqkv, gqkv, gqkv, gb, sz, nw_row)


def _out_mlp_kernel(x_ref, oTa_ref, oTb_ref, og_ref, woa_ref, wog_ref, pmn_ref, pre_ref, post_ref,
                    wup_ref, wdn_ref, out_ref):
    oT = jnp.concatenate([oTa_ref[0, 0], oTb_ref[0, 0]], axis=1)
    o_att = oT.T.astype(BF16)
    mix = jnp.dot(o_att, woa_ref[...], preferred_element_type=F32)
    mix = mix + jnp.dot(og_ref[...].astype(BF16), wog_ref[...], preferred_element_type=F32)
    x1 = x_ref[...] + _rms(mix, pmn_ref[...])
    h = _rms(x1, pre_ref[...]).astype(BF16)
    acc = jnp.zeros((ROW_TILE, D_MODEL), F32)
    for c in range(D_FF // FF_TILE):
        up = jnp.dot(h, wup_ref[:, c * FF_TILE:(c + 1) * FF_TILE], preferred_element_type=F32)
        act = jnp.square(jnp.maximum(up, 0.0)).astype(BF16)
        acc = acc + jnp.dot(act, wdn_ref[c * FF_TILE:(c + 1) * FF_TILE, :], preferred_element_type=F32)
    out_ref[...] = x1 + _rms(acc, post_ref[...])


def _out_mlp(xf, oT, og, woa, wog, pmn, pre, post, wup, wdn, B, S):
    T = B * S
    nblk = S // MOBA_BLOCK
    tiles_per_seq = S // ROW_TILE
    assert ROW_TILE == 2 * MOBA_BLOCK
    const = lambda i: (0, 0)
    row = lambda i: (i, 0)

    def att_block(which):
        def index(i):
            blk = 2 * (i % tiles_per_seq) + which
            return (i // tiles_per_seq, _paired_pos(blk, nblk), 0, 0)
        return index

    single = dict(pipeline_mode=pl.Buffered(1))
    return pl.pallas_call(
        _out_mlp_kernel,
        grid=(T // ROW_TILE,),
        in_specs=[
            pl.BlockSpec((ROW_TILE, D_MODEL), row),
            pl.BlockSpec((1, 1, ATT_WIDTH, MOBA_BLOCK), att_block(0)),
            pl.BlockSpec((1, 1, ATT_WIDTH, MOBA_BLOCK), att_block(1)),
            pl.BlockSpec((ROW_TILE, GDN_WIDTH), row),
            pl.BlockSpec(woa.shape, const, **single),
            pl.BlockSpec(wog.shape, const, **single),
            pl.BlockSpec((1, D_MODEL), const),
            pl.BlockSpec((1, D_MODEL), const),
            pl.BlockSpec((1, D_MODEL), const),
            pl.BlockSpec(wup.shape, const, **single),
            pl.BlockSpec(wdn.shape, const, **single),
        ],
        out_specs=pl.BlockSpec((ROW_TILE, D_MODEL), row),
        out_shape=jax.ShapeDtypeStruct((T, D_MODEL), F32),
        compiler_params=pltpu.CompilerParams(
            dimension_semantics=("arbitrary",), vmem_limit_bytes=VMEM_LIMIT),
        name="out_mlp",
    )(xf, oT, oT, og, woa, wog, pmn, pre, post, wup, wdn)


def kernel(x, w_in, w_out, conv_w, A_log, dt_bias, gdn_norm_w, rel_bias, pre_mix_norm,
           post_mix_norm, pre_mlp_norm, post_mlp_norm, w_up, w_down):
    B, S, D = x.shape
    assert D == D_MODEL and S % ROW_TILE == 0 and S % MOBA_BLOCK == 0
    T = B * S
    depth = w_in.shape[0]
    xf = x.reshape(T, D)
    o0, o1, o2, o3, o4 = 0, ATT_WIDTH, 2 * ATT_WIDTH, 3 * ATT_WIDTH, 3 * ATT_WIDTH + 3 * GDN_WIDTH
    o5 = o4 + GDN_WIDTH
    for l in range(depth):
        wi = w_in[l]
        wqT = wi[:, o0:o1].T.astype(BF16)
        wk = wi[:, o1:o2].astype(BF16)
        wvT = wi[:, o2:o3].T.astype(BF16)
        wg = wi[:, o3:o4].astype(BF16)
        wz = wi[:, o4:o5].astype(BF16)
        wab = jnp.pad(wi[:, o5:], ((0, 0), (0, LANES - 2 * GDN_HEADS))).astype(BF16)
        pad8 = lambda v: jnp.pad(v.astype(F32), (0, LANES - GDN_HEADS))[None, :]
        qT, k, vT, gqkv, sz, gb = _inproj(xf, pre_mix_norm[l][None, :], wqT, wk, wvT, wg, wz, wab,
                                          conv_w[l], pad8(A_log[l]), pad8(dt_bias[l]), B, S)
        oT = _attention(rel_bias.astype(F32), qT, k.reshape(B, S, ATT_WIDTH), vT, B, S)
        og = _gdn(gqkv, sz, gb, jnp.tile(gdn_norm_w[l], GDN_HEADS)[None, :], B, S)
        wo = w_out[l].astype(BF16)
        xf = _out_mlp(xf, oT, og, wo[:ATT_WIDTH], wo[ATT_WIDTH:], post_mix_norm[l][None, :],
                      pre_mlp_norm[l][None, :], post_mlp_norm[l][None, :],
                      w_up[l].astype(BF16), w_down[l].astype(BF16), B, S)
    return xf.reshape(B, S, D)
```

```python
import functools
import math

import jax
import jax.numpy as jnp
from jax import lax
from jax.experimental import pallas as pl
from jax.experimental.pallas import tpu as pltpu

F32 = jnp.float32
BF16 = jnp.bfloat16
HI = lax.Precision.HIGHEST

D_MODEL = 1024
HEAD_DIM = 64
ATT_HEADS = 8
GDN_HEADS = 8
ATT_WIDTH = ATT_HEADS * HEAD_DIM
GDN_WIDTH = GDN_HEADS * HEAD_DIM
MOBA_BLOCK = 256
MOBA_TOPK = 3
GDN_CHUNK = 64
CONV_WIDTH = 4
D_FF = 4 * D_MODEL
REL_BUCKETS = 32
REL_MAX_EXACT = 16
REL_MAX_DIST = 128
EPS = 1e-6
NEG = -1e30
LOG2E = math.log2(math.e)

LANES = 128
SUBLANES = 8
VMEM_LIMIT = 56 * 1024 * 1024
ROW_TILE = 512
INPROJ_TILE = 512
FF_TILE = 1024

NT = (((1,), (1,)), ((), ()))
TN = (((0,), (0,)), ((), ()))


def _bucket_lower_bounds():
    def bucket(d):
        if d < REL_MAX_EXACT:
            return d
        t = math.log(d / REL_MAX_EXACT) / math.log(REL_MAX_DIST / REL_MAX_EXACT)
        t = t * (REL_BUCKETS - REL_MAX_EXACT)
        assert d in (REL_MAX_EXACT, REL_MAX_DIST) or abs(t - round(t)) > 1e-6
        return min(REL_MAX_EXACT + int(t + 1e-9), REL_BUCKETS - 1)
    lower = []
    for b in range(REL_BUCKETS):
        d = 0
        while bucket(d) < b:
            d += 1
        lower.append(d)
    return lower


BUCKET_LOWER = _bucket_lower_bounds()


def _sigmoid(x):
    return 0.5 * jnp.tanh(0.5 * x) + 0.5


def _silu_of_half(h):
    return h + h * jnp.tanh(h)


def _interleave(*stage_generators):
    live = list(stage_generators)
    while live:
        for gen in list(live):
            try:
                next(gen)
            except StopIteration:
                live.remove(gen)


def _rms(x, w):
    return x * lax.rsqrt(jnp.mean(x * x, axis=-1, keepdims=True) + EPS) * w


def _split_bf16(x, parts):
    out = []
    for _ in range(parts):
        h = x.astype(BF16)
        out.append(h)
        x = x - h.astype(F32)
    return out


def _dot_split_lhs(x, c, parts):
    acc = None
    for h in _split_bf16(x, parts):
        d = jnp.dot(h, c, preferred_element_type=F32)
        acc = d if acc is None else acc + d
    return acc


def _dot_split_rhs(c, x, parts):
    acc = None
    for h in _split_bf16(x, parts):
        d = jnp.dot(c, h, preferred_element_type=F32)
        acc = d if acc is None else acc + d
    return acc


CONV_COLS = 512


def _inproj_kernel(x_ref, xp_ref, nw_ref, wqT_ref, wk_ref, wvT_ref, wg_ref, wz_ref, wab_ref,
                   cw_ref, alog_ref, dtb_ref,
                   qT_ref, k_ref, vT_ref, g_ref, z_ref, gb_ref, *, tiles_per_seq):
    h = _rms(x_ref[...], nw_ref[...]).astype(BF16)

    hp = _rms(xp_ref[...], nw_ref[...]).astype(BF16)
    seq_start = (pl.program_id(0) % tiles_per_seq) == 0
    trow8 = lax.broadcasted_iota(jnp.int32, (SUBLANES, CONV_COLS), 0)
    for c in range(3 * GDN_WIDTH // CONV_COLS):
        cols = slice(c * CONV_COLS, (c + 1) * CONV_COLS)
        cur = jnp.dot(h, wg_ref[:, cols], preferred_element_type=F32)
        prev8 = jnp.dot(hp, wg_ref[:, cols], preferred_element_type=F32)
        prev8 = jnp.where(seq_start, 0.0, prev8)
        cw_half = 0.5 * cw_ref[:, cols]
        acc = cur * cw_half[CONV_WIDTH - 1:CONV_WIDTH]
        for s in range(1, CONV_WIDTH):
            rolled = pltpu.roll(cur, s, 0)
            top = jnp.where(trow8 < s, pltpu.roll(prev8, s, 0), rolled[0:SUBLANES])
            tap = jnp.concatenate([top, rolled[SUBLANES:]], axis=0)
            acc = acc + tap * cw_half[CONV_WIDTH - 1 - s:CONV_WIDTH - s]
        g_ref[:, cols] = _silu_of_half(acc)

    z = jnp.dot(h, wz_ref[...], preferred_element_type=F32)
    z_ref[...] = _silu_of_half(0.5 * z)
    ab = jnp.dot(h, wab_ref[...], preferred_element_type=F32)
    xs = ab + dtb_ref[...]
    log_decay = -jnp.exp(alog_ref[...]) * (jnp.maximum(xs, 0.0) + jnp.log1p(jnp.exp(-jnp.abs(xs))))
    lane = lax.broadcasted_iota(jnp.int32, ab.shape, 1)
    gb_ref[...] = jnp.where(lane < GDN_HEADS, log_decay, _sigmoid(ab))

    qT = lax.dot_general(wqT_ref[...], h, NT, preferred_element_type=F32)
    vT = lax.dot_general(wvT_ref[...], h, NT, preferred_element_type=F32)
    for t in range(INPROJ_TILE // MOBA_BLOCK):
        qT_ref[0, t] = qT[:, t * MOBA_BLOCK:(t + 1) * MOBA_BLOCK]
        vT_ref[0, t] = vT[:, t * MOBA_BLOCK:(t + 1) * MOBA_BLOCK].astype(BF16)
    k_ref[...] = jnp.dot(h, wk_ref[...], preferred_element_type=F32)


def _inproj(xf, nw, wqT, wk, wvT, wg, wz, wab, conv_w, alog_pad, dtb_pad, B, S):
    T = B * S
    TM = INPROJ_TILE
    assert S % TM == 0
    nblk = S // MOBA_BLOCK
    tiles_per_seq = S // TM
    blk_per_tile = TM // MOBA_BLOCK
    const = lambda i: (0, 0)
    row = lambda i: (i, 0)
    tr = lambda i: (i // tiles_per_seq, i % tiles_per_seq, 0, 0)
    prev_rows = lambda i: (jnp.maximum(i * (TM // SUBLANES) - 1, 0), 0)
    single = dict(pipeline_mode=pl.Buffered(1))
    return pl.pallas_call(
        functools.partial(_inproj_kernel, tiles_per_seq=tiles_per_seq),
        grid=(T // TM,),
        in_specs=[
            pl.BlockSpec((TM, D_MODEL), row),
            pl.BlockSpec((SUBLANES, D_MODEL), prev_rows),
            pl.BlockSpec((1, D_MODEL), const),
            pl.BlockSpec(wqT.shape, const, **single),
            pl.BlockSpec(wk.shape, const, **single),
            pl.BlockSpec(wvT.shape, const, **single),
            pl.BlockSpec(wg.shape, const, **single),
            pl.BlockSpec(wz.shape, const, **single),
            pl.BlockSpec(wab.shape, const, **single),
            pl.BlockSpec(conv_w.shape, const),
            pl.BlockSpec((1, LANES), const),
            pl.BlockSpec((1, LANES), const),
        ],
        out_specs=[
            pl.BlockSpec((1, blk_per_tile, ATT_WIDTH, MOBA_BLOCK), tr),
            pl.BlockSpec((TM, ATT_WIDTH), row),
            pl.BlockSpec((1, blk_per_tile, ATT_WIDTH, MOBA_BLOCK), tr),
            pl.BlockSpec((TM, 3 * GDN_WIDTH), row),
            pl.BlockSpec((TM, GDN_WIDTH), row),
            pl.BlockSpec((TM, LANES), row),
        ],
        out_shape=[
            jax.ShapeDtypeStruct((B, nblk, ATT_WIDTH, MOBA_BLOCK), F32),
            jax.ShapeDtypeStruct((T, ATT_WIDTH), F32),
            jax.ShapeDtypeStruct((B, nblk, ATT_WIDTH, MOBA_BLOCK), BF16),
            jax.ShapeDtypeStruct((T, 3 * GDN_WIDTH), F32),
            jax.ShapeDtypeStruct((T, GDN_WIDTH), F32),
            jax.ShapeDtypeStruct((T, LANES), F32),
        ],
        compiler_params=pltpu.CompilerParams(
            dimension_semantics=("arbitrary",), vmem_limit_bytes=VMEM_LIMIT),
        name="inproj",
    )(xf, xf, nw, wqT, wk, wvT, wg, wz, wab, conv_w, alog_pad, dtb_pad)


V_ROWS = HEAD_DIM + 16


def _attn_item(i, nbatch, nblk):
    per_hp = nbatch * (nblk // 2)
    return i // per_hp, (i // (nblk // 2)) % nbatch, i % (nblk // 2)


def _attn_tiles(t, nblk):
    i_hi = nblk - 1 - t
    has_lo_prev = t >= 1
    n_lo_far = jnp.maximum(t - 1, 0)
    tiles = [(0, t, 0, False), (1, i_hi, 0, False), (1, i_hi - 1, 1, False),
             (jnp.where(has_lo_prev, 0, 1), jnp.where(has_lo_prev, t - 1, i_hi - 2),
              jnp.where(has_lo_prev, 1, 2), False)]
    for f in range(nblk - 3):
        f_lo = f < n_lo_far
        tiles.append((jnp.where(f_lo, 0, 1), jnp.where(f_lo, f, f - n_lo_far), None, True))
    return tiles


def _attn_kernel(relb_ref, qlo_ref, qhi_ref, k_ref, vT_ref, oT_ref,
                 kb_ref, km_ref, va_ref, bias_ref, addm_ref, qh_ref, lg_ref, moff_ref,
                 *, nblk, nbatch, nitems):
    step = pl.program_id(0)
    hp, b, t = _attn_item(jnp.minimum(step, nitems - 1), nbatch, nblk)
    vcur = (step // (nblk // 2)) % 2
    BLK = MOBA_BLOCK
    is_item = step < nitems

    @pl.when((b == 0) & (t == 0) & is_item)
    def _():
        kk = lax.broadcasted_iota(jnp.int32, (BLK, BLK), 0)
        qq = lax.broadcasted_iota(jnp.int32, (BLK, BLK), 1)
        for hh in range(2):
            h = 2 * hp + hh
            for kind in range(2):
                d = qq - kk + kind * BLK
                val = jnp.full((BLK, BLK), relb_ref[h, REL_BUCKETS - 1], F32)
                for bkt in range(REL_BUCKETS - 2, -1, -1):
                    val = jnp.where(d < BUCKET_LOWER[bkt + 1], relb_ref[h, bkt], val)
                val = val * LOG2E
                if kind == 0:
                    val = jnp.where(d >= 0, val, NEG)
                bias_ref[hh, kind] = val
            bias_ref[hh, 2] = jnp.full((BLK, BLK), relb_ref[h, REL_BUCKETS - 1] * LOG2E, F32)

    @pl.when((t == 0) & is_item)
    def _():
        lane = lax.broadcasted_iota(jnp.int32, (1, LANES), 1)
        ones_row = jnp.where(lax.broadcasted_iota(jnp.int32, (V_ROWS - HEAD_DIM, BLK), 0) == 0,
                             1.0, 0.0).astype(BF16)
        for j in range(nblk):
            kj = k_ref[0, j * BLK:(j + 1) * BLK, :]
            kb_ref[j * BLK:(j + 1) * BLK, :] = kj.astype(BF16)
            kmj = jnp.sum(kj, axis=0, keepdims=True) * (1.0 / BLK)
            km_ref[j:j + 1, :] = jnp.where(lane < HEAD_DIM, kmj, 0.0)
            km_ref[nblk + j:nblk + j + 1, :] = jnp.where(lane >= HEAD_DIM, kmj, 0.0)
            for hh in range(2):
                va_ref[vcur, j, hh, 0:HEAD_DIM, :] = vT_ref[0, j, HEAD_DIM * hh:HEAD_DIM * (hh + 1), :]
                va_ref[vcur, j, hh, HEAD_DIM:V_ROWS, :] = ones_row

    def to_lo(slot, x, other):
        if isinstance(slot, int):
            return x if slot == 0 else other
        return jnp.where(slot == 0, x, other)

    def score_stages(cur):
        ridx = lax.broadcasted_iota(jnp.int32, (nblk, BLK), 0)
        sub = lax.broadcasted_iota(jnp.int32, (LANES, BLK), 0)
        scale = HEAD_DIM ** -0.5 * LOG2E
        q_blocks = (t, nblk - 1 - t)
        for s, (q_ref, qi) in enumerate(zip((qlo_ref, qhi_ref), q_blocks)):
            qT = q_ref[0, 0]
            gT = jnp.dot(km_ref[...], qT, precision=HI, preferred_element_type=F32)
            past = ridx < qi
            for hh in range(2):
                gm = jnp.where(past, gT[nblk * hh:nblk * (hh + 1)], -jnp.inf)
                cnt = jnp.zeros((nblk, BLK), F32)
                for jp in range(nblk):
                    row = gm[jp:jp + 1, :]
                    beats = (row > gm) | ((row == gm) & (ridx > jp))
                    cnt = cnt + jnp.where(beats, 1.0, 0.0)
                visible = past & (cnt < MOBA_TOPK)
                addm_ref[s, nblk * hh:nblk * (hh + 1), :] = jnp.where(visible, 0.0, NEG)
                in_head = (sub >= HEAD_DIM * hh) & (sub < HEAD_DIM * (hh + 1))
                qh_ref[s, hh] = jnp.where(in_head, qT * scale, 0.0).astype(BF16)
        yield
        tiles = _attn_tiles(t, nblk)
        for hh in range(2):
            cmax = []
            rowoff = []
            for n, (slot, kblk, bias_kind, far) in enumerate(tiles):
                kj = kb_ref[pl.ds(pl.multiple_of(kblk * BLK, BLK), BLK), :]
                lg = jnp.dot(kj, qh_ref[slot, hh], preferred_element_type=F32)
                if bias_kind is not None:
                    lg = lg + bias_ref[hh, bias_kind]
                lg_ref[cur, hh, n] = lg
                cm = jnp.max(lg, axis=0, keepdims=True)
                off = None
                if n >= 2:
                    off = addm_ref[slot, pl.ds(nblk * hh + kblk, 1), :]
                    if far:
                        off = off + relb_ref[2 * hp + hh, REL_BUCKETS - 1] * LOG2E
                    cm = cm + off
                cmax.append(cm)
                rowoff.append(off)
                yield
            m_lo = cmax[0]
            m_hi = cmax[1]
            for n in range(2, len(tiles)):
                m_lo = jnp.maximum(m_lo, to_lo(tiles[n][0], cmax[n], -jnp.inf))
                m_hi = jnp.maximum(m_hi, to_lo(tiles[n][0], -jnp.inf, cmax[n]))
            for n, (slot, _, _, _) in enumerate(tiles):
                m_n = to_lo(slot, m_lo, m_hi)
                moff_ref[cur, hh, n:n + 1, :] = m_n if rowoff[n] is None else m_n - rowoff[n]

    def softmax_pv_stages(prv):
        _, _, t_prev = _attn_item(step - 1, nbatch, nblk)
        vprv = ((step - 1) // (nblk // 2)) % 2
        tiles = _attn_tiles(t_prev, nblk)
        for hh in range(2):
            acc_lo = None
            acc_hi = None
            for n, (slot, kblk, _, _) in enumerate(tiles):
                p = jnp.exp2(lg_ref[prv, hh, n] - moff_ref[prv, hh, n:n + 1, :])
                pvn = jnp.dot(va_ref[vprv, kblk, hh], p.astype(BF16),
                              preferred_element_type=F32)
                if isinstance(slot, int):
                    if slot == 0:
                        acc_lo = pvn if acc_lo is None else acc_lo + pvn
                    else:
                        acc_hi = pvn if acc_hi is None else acc_hi + pvn
                else:
                    acc_lo = acc_lo + jnp.where(slot == 0, pvn, 0.0)
                    acc_hi = acc_hi + jnp.where(slot == 0, 0.0, pvn)
                yield
            oT_ref[0, 0, HEAD_DIM * hh:HEAD_DIM * (hh + 1), :] = (
                acc_lo[0:HEAD_DIM] / acc_lo[HEAD_DIM:HEAD_DIM + 1])
            oT_ref[0, 1, HEAD_DIM * hh:HEAD_DIM * (hh + 1), :] = (
                acc_hi[0:HEAD_DIM] / acc_hi[HEAD_DIM:HEAD_DIM + 1])

    @pl.when(step == 0)
    def _():
        _interleave(score_stages(0))

    for parity in range(2):
        @pl.when((step > 0) & is_item & (step % 2 == parity))
        def _(parity=parity):
            _interleave(softmax_pv_stages(1 - parity), score_stages(parity))

    @pl.when(step == nitems)
    def _():
        _interleave(softmax_pv_stages((nitems - 1) % 2))


def _paired_pos(i, nblk):
    return jnp.where(i < nblk // 2, 2 * i, 2 * (nblk - 1 - i) + 1)


def _attention(rel_bias, qT, k3, vT, B, S):
    nblk = S // MOBA_BLOCK
    assert nblk % 2 == 0 and nblk >= 4
    assert BUCKET_LOWER[REL_BUCKETS - 1] <= MOBA_BLOCK + 1
    nitems = (ATT_HEADS // 2) * B * (nblk // 2)
    item = lambda s: _attn_item(jnp.minimum(s, nitems - 1), B, nblk)
    done = lambda s: _attn_item(jnp.maximum(s - 1, 0), B, nblk)

    def q_lo(s):
        hp, b, t = item(s)
        return (b, t, hp, 0)

    def q_hi(s):
        hp, b, t = item(s)
        return (b, nblk - 1 - t, hp, 0)

    def k_blk(s):
        hp, b, _ = item(s)
        return (b, 0, hp)

    def v_blk(s):
        hp, b, _ = item(s)
        return (b, 0, hp, 0)

    def o_blk(s):
        hp, b, t = done(s)
        return (b, t, hp, 0)

    return pl.pallas_call(
        functools.partial(_attn_kernel, nblk=nblk, nbatch=B, nitems=nitems),
        grid=(nitems + 1,),
        in_specs=[
            pl.BlockSpec(memory_space=pltpu.SMEM),
            pl.BlockSpec((1, 1, LANES, MOBA_BLOCK), q_lo),
            pl.BlockSpec((1, 1, LANES, MOBA_BLOCK), q_hi),
            pl.BlockSpec((1, S, LANES), k_blk),
            pl.BlockSpec((1, nblk, LANES, MOBA_BLOCK), v_blk),
        ],
        out_specs=pl.BlockSpec((1, 2, LANES, MOBA_BLOCK), o_blk),
        out_shape=jax.ShapeDtypeStruct((B, nblk, ATT_WIDTH, MOBA_BLOCK), F32),
        scratch_shapes=[
            pltpu.VMEM((S, LANES), BF16),
            pltpu.VMEM((2 * nblk, LANES), F32),
            pltpu.VMEM((2, nblk, 2, V_ROWS, MOBA_BLOCK), BF16),
            pltpu.VMEM((2, 3, MOBA_BLOCK, MOBA_BLOCK), F32),
            pltpu.VMEM((2, 2 * nblk, MOBA_BLOCK), F32),
            pltpu.VMEM((2, 2, LANES, MOBA_BLOCK), BF16),
            pltpu.VMEM((2, 2, nblk + 1, MOBA_BLOCK, MOBA_BLOCK), F32),
            pltpu.VMEM((2, 2, 2 * SUBLANES, MOBA_BLOCK), F32),
        ],
        compiler_params=pltpu.CompilerParams(
            dimension_semantics=("arbitrary",), vmem_limit_bytes=VMEM_LIMIT),
        name="moba_attn",
    )(rel_bias, qT, qT, k3, vT)


GDN_TILE = 256
GDN_HALF = 2 * LANES


def _gdn_kernel(yq_ref, yk_ref, yv_ref, gb_ref, sz_ref, nw_ref, out_ref,
                u_ref, wq_ref, a_ref, kd_ref, gl_ref, st_ref, *, tiles_per_seq, nsteps):
    step = pl.program_id(0)
    C = GDN_CHUNK
    W = GDN_WIDTH
    TILE = GDN_TILE
    npair = W // LANES

    r_w = lax.broadcasted_iota(jnp.int32, (GDN_HALF, GDN_HALF), 0)
    c_w = lax.broadcasted_iota(jnp.int32, (GDN_HALF, GDN_HALF), 1)
    head_ones = jnp.where((r_w // HEAD_DIM) == (c_w // HEAD_DIM), 1.0, 0.0).astype(BF16)
    ltri_bd = jnp.where(((r_w // C) == (c_w // C)) & (c_w <= r_w), 1.0, 0.0).astype(BF16)
    r_e = lax.broadcasted_iota(jnp.int32, (LANES, W), 0)
    c_e = lax.broadcasted_iota(jnp.int32, (LANES, W), 1)
    head_of_col = c_e // HEAD_DIM
    sel_g = jnp.where(r_e == head_of_col, 1.0, 0.0).astype(BF16)
    sel_beta = jnp.where(r_e == GDN_HEADS + head_of_col, 1.0, 0.0).astype(BF16)
    tok = lax.broadcasted_iota(jnp.int32, (TILE, W), 0) % C
    col = lax.broadcasted_iota(jnp.int32, (TILE, W), 1) % HEAD_DIM
    causal_t = tok >= col
    strict_t = tok > col
    lane_t = lax.broadcasted_iota(jnp.int32, (TILE, LANES), 1)

    lane = lax.broadcasted_iota(jnp.int32, (C, LANES), 1)
    rowi = lax.broadcasted_iota(jnp.int32, (C, LANES), 0)
    first_head = lane < HEAD_DIM
    strict = rowi > (lane % HEAD_DIM)
    eye2 = jnp.where(rowi == (lane % HEAD_DIM), 1.0, 0.0)
    lane2 = lax.broadcasted_iota(jnp.int32, (C, 2 * LANES), 1)
    first_head2 = (lane2 % LANES) < HEAD_DIM
    r_l = lax.broadcasted_iota(jnp.int32, (LANES, LANES), 0)
    c_l = lax.broadcasted_iota(jnp.int32, (LANES, LANES), 1)
    same_head = (r_l // HEAD_DIM) == (c_l // HEAD_DIM)
    pair_ones = jnp.where(same_head, 1.0, 0.0).astype(BF16)

    def stack(x, mask):
        return jnp.concatenate([jnp.where(mask, x, 0.0), jnp.where(mask, 0.0, x)], axis=0)

    dot = functools.partial(jnp.dot, preferred_element_type=F32)

    def head_sumsq(y):
        y2 = (y * y).astype(BF16)
        return jnp.concatenate([dot(y2[:, h:h + GDN_HALF], head_ones) for h in range(0, W, GDN_HALF)],
                               axis=1)

    def solve_stages(slot):
        yq = yq_ref[...]
        yk = yk_ref[...]
        yv = yv_ref[...]
        qn = yq * lax.rsqrt(head_sumsq(yq) + EPS) * (HEAD_DIM ** -0.5)
        kn = yk * lax.rsqrt(head_sumsq(yk) + EPS)
        yield
        gbt = gb_ref[...]

        def spread(col0):
            pairs = []
            for p in range(npair):
                a = jnp.broadcast_to(gbt[:, col0 + 2 * p:col0 + 2 * p + 1], (TILE, LANES))
                b = jnp.broadcast_to(gbt[:, col0 + 2 * p + 1:col0 + 2 * p + 2], (TILE, LANES))
                pairs.append(jnp.where(lane_t < HEAD_DIM, a, b))
            return jnp.concatenate(pairs, axis=1)

        g = spread(0)
        beta = spread(GDN_HEADS)
        gcd = _dot_split_rhs(ltri_bd, jnp.concatenate([g, jnp.where(strict_t, g, 0.0)], axis=1), 2)
        yield
        gc = gcd[:, :W]
        decay = jnp.where(causal_t, jnp.exp(jnp.where(causal_t, gcd[:, W:], 0.0)), 0.0)
        egc = jnp.exp(gc)
        kb = kn * beta
        rv = yv * beta
        rk = kb * egc
        qd = qn * egc
        for cc in range(TILE // C):
            rs = slice(cc * C, (cc + 1) * C)
            g_last = gc[(cc + 1) * C - 1:(cc + 1) * C, :]
            kd_ref[slot, rs, :] = (kn[rs] * jnp.exp(g_last - gc[rs])).astype(BF16)
            gl_ref[slot, cc * SUBLANES:(cc + 1) * SUBLANES, :] = (
                jnp.broadcast_to(jnp.exp(g_last), (SUBLANES, W)))
        units = [(slice(cc * C, (cc + 1) * C), slice(LANES * p, LANES * (p + 1)), cc)
                 for cc in range(TILE // C) for p in range(npair)]
        kqs = [lax.dot_general(jnp.concatenate([kn[rs, ls], qn[rs, ls]], axis=0).astype(BF16),
                               stack(kn[rs, ls], first_head).astype(BF16), NT,
                               preferred_element_type=F32) for rs, ls, _ in units]
        yield
        ps = [-jnp.where(strict, kq[0:C] * beta[rs, ls] * decay[rs, ls], 0.0)
              for kq, (rs, ls, _) in zip(kqs, units)]
        ss = [eye2 + p for p in ps]
        ps = [dot(p.astype(BF16), stack(p, first_head).astype(BF16)) for p in ps]
        yield
        nround = int(math.log2(C))
        for k in range(1, nround):
            rhs = [stack(s_, first_head).astype(BF16) for s_ in ss]
            if k + 1 < nround:
                rhs = [jnp.concatenate([stack(p, first_head).astype(BF16), sx], axis=1)
                       for p, sx in zip(ps, rhs)]
            outs = [dot(p.astype(BF16), sx) for p, sx in zip(ps, rhs)]
            if k + 1 < nround:
                ps = [o[:, :LANES] for o in outs]
                ss = [s_ + o[:, LANES:] for s_, o in zip(ss, outs)]
            else:
                ss = [s_ + o for s_, o in zip(ss, outs)]
            yield
        xs = [dot(s_.astype(BF16),
                  stack(jnp.concatenate([rv[rs, ls], rk[rs, ls]], axis=1), first_head2).astype(BF16))
              for s_, (rs, ls, _) in zip(ss, units)]
        yield
        for x, kq, (rs, ls, cc) in zip(xs, kqs, units):
            u_ref[slot, rs, ls] = x[:, :LANES]
            wq_ref[slot, cc, 0:C, ls] = x[:, LANES:].astype(BF16)
            wq_ref[slot, cc, C:2 * C, ls] = qd[rs, ls].astype(BF16)
            a_ref[slot, rs, ls] = (kq[C:2 * C] * decay[rs, ls]).astype(BF16)

    lss = [slice(LANES * p, LANES * (p + 1)) for p in range(npair)]

    def recurrence_stages(slot, seq_start):
        states = [jnp.where(seq_start, 0.0, st_ref[p]) for p in range(npair)]
        pending = None

        def finish(rs, os_):
            for ls, o in zip(lss, os_):
                ms = dot((o * o).astype(BF16), pair_ones) * (1.0 / HEAD_DIM)
                out_ref[rs, ls] = o * lax.rsqrt(ms + EPS) * nw_ref[:, ls] * sz_ref[rs, ls]

        for cc in range(TILE // C):
            rs = slice(cc * C, (cc + 1) * C)
            wqs = [dot(wq_ref[slot, cc, :, ls], st.astype(BF16)) for ls, st in zip(lss, states)]
            if pending is not None:
                finish(*pending)
            yield
            v_news = [u_ref[slot, rs, ls] - wq[0:C] for ls, wq in zip(lss, wqs)]
            kvs = [lax.dot_general(kd_ref[slot, rs, ls], v.astype(BF16), TN, preferred_element_type=F32)
                   for ls, v in zip(lss, v_news)]
            os_ = [wq[C:2 * C] + dot(a_ref[slot, rs, ls], stack(v, first_head).astype(BF16))
                   for ls, wq, v in zip(lss, wqs, v_news)]
            states = [st * gl_ref[slot, cc * SUBLANES:cc * SUBLANES + 1, ls] + jnp.where(same_head, kv, 0.0)
                      for ls, st, kv in zip(lss, states, kvs)]
            pending = (rs, os_)
            yield
        finish(*pending)
        for p in range(npair):
            st_ref[p] = states[p]

    seq_start = ((step - 1) % tiles_per_seq) == 0

    @pl.when(step == 0)
    def _():
        st_ref[...] = jnp.zeros(st_ref.shape, F32)
        _interleave(solve_stages(0))

    for parity in range(2):
        @pl.when((step > 0) & (step < nsteps - 1) & (step % 2 == parity))
        def _(parity=parity):
            _interleave(recurrence_stages(1 - parity, seq_start), solve_stages(parity))

    @pl.when(step == nsteps - 1)
    def _():
        _interleave(recurrence_stages((nsteps - 2) % 2, seq_start))


def _gdn(gqkv, sz, gb, nw_row, B, S):
    T = B * S
    W = GDN_WIDTH
    TILE = GDN_TILE
    assert TILE == GDN_HALF and S % TILE == 0
    nchunk = TILE // GDN_CHUNK
    ntiles = T // TILE
    nsteps = ntiles + 1
    cur_tile = lambda off: (lambda s: (jnp.minimum(s, ntiles - 1), off))
    prev_tile = lambda s: (jnp.maximum(s - 1, 0), 0)
    return pl.pallas_call(
        functools.partial(_gdn_kernel, tiles_per_seq=S // TILE, nsteps=nsteps),
        grid=(nsteps,),
        in_specs=[
            pl.BlockSpec((TILE, W), cur_tile(0)),
            pl.BlockSpec((TILE, W), cur_tile(1)),
            pl.BlockSpec((TILE, W), cur_tile(2)),
            pl.BlockSpec((TILE, LANES), cur_tile(0)),
            pl.BlockSpec((TILE, W), prev_tile),
            pl.BlockSpec((1, W), lambda s: (0, 0)),
        ],
        out_specs=pl.BlockSpec((TILE, W), prev_tile),
        out_shape=jax.ShapeDtypeStruct((T, GDN_WIDTH), F32),
        scratch_shapes=[
            pltpu.VMEM((2, TILE, W), F32),
            pltpu.VMEM((2, nchunk, 2 * GDN_CHUNK, W), BF16),
            pltpu.VMEM((2, TILE, W), BF16),
            pltpu.VMEM((2, TILE, W), BF16),
            pltpu.VMEM((2, nchunk * SUBLANES, W), F32),
            pltpu.VMEM((W // LANES, LANES, LANES), F32),
        ],
        compiler_params=pltpu.CompilerParams(
            dimension_semantics=("arbitrary",), vmem_limit_bytes=VMEM_LIMIT),
        name="gdn",
    )(gqkv, gqkv, gqkv, gb, sz, nw_row)


def _out_mlp_kernel(x_ref, oTa_ref, oTb_ref, og_ref, woa_ref, wog_ref, pmn_ref, pre_ref, post_ref,
                    wup_ref, wdn_ref, out_ref):
    oT = jnp.concatenate([oTa_ref[0, 0], oTb_ref[0, 0]], axis=1)
    o_att = oT.T.astype(BF16)
    mix = jnp.dot(o_att, woa_ref[...], preferred_element_type=F32)
    mix = mix + jnp.dot(og_ref[...].astype(BF16), wog_ref[...], preferred_element_type=F32)
    x1 = x_ref[...] + _rms(mix, pmn_ref[...])
    h = _rms(x1, pre_ref[...]).astype(BF16)
    acc = jnp.zeros((ROW_TILE, D_MODEL), F32)
    for c in range(D_FF // FF_TILE):
        up = jnp.dot(h, wup_ref[:, c * FF_TILE:(c + 1) * FF_TILE], preferred_element_type=F32)
        act = jnp.square(jnp.maximum(up, 0.0)).astype(BF16)
        acc = acc + jnp.dot(act, wdn_ref[c * FF_TILE:(c + 1) * FF_TILE, :], preferred_element_type=F32)
    out_ref[...] = x1 + _rms(acc, post_ref[...])


def _out_mlp(xf, oT, og, woa, wog, pmn, pre, post, wup, wdn, B, S):
    T = B * S
    nblk = S // MOBA_BLOCK
    tiles_per_seq = S // ROW_TILE
    assert ROW_TILE == 2 * MOBA_BLOCK
    const = lambda i: (0, 0)
    row = lambda i: (i, 0)

    def att_block(which):
        def index(i):
            blk = 2 * (i % tiles_per_seq) + which
            return (i // tiles_per_seq, _paired_pos(blk, nblk), 0, 0)
        return index

    single = dict(pipeline_mode=pl.Buffered(1))
    return pl.pallas_call(
        _out_mlp_kernel,
        grid=(T // ROW_TILE,),
        in_specs=[
            pl.BlockSpec((ROW_TILE, D_MODEL), row),
            pl.BlockSpec((1, 1, ATT_WIDTH, MOBA_BLOCK), att_block(0)),
            pl.BlockSpec((1, 1, ATT_WIDTH, MOBA_BLOCK), att_block(1)),
            pl.BlockSpec((ROW_TILE, GDN_WIDTH), row),
            pl.BlockSpec(woa.shape, const, **single),
            pl.BlockSpec(wog.shape, const, **single),
            pl.BlockSpec((1, D_MODEL), const),
            pl.BlockSpec((1, D_MODEL), const),
            pl.BlockSpec((1, D_MODEL), const),
            pl.BlockSpec(wup.shape, const, **single),
            pl.BlockSpec(wdn.shape, const, **single),
        ],
        out_specs=pl.BlockSpec((ROW_TILE, D_MODEL), row),
        out_shape=jax.ShapeDtypeStruct((T, D_MODEL), F32),
        compiler_params=pltpu.CompilerParams(
            dimension_semantics=("arbitrary",), vmem_limit_bytes=VMEM_LIMIT),
        name="out_mlp",
    )(xf, oT, oT, og, woa, wog, pmn, pre, post, wup, wdn)


def kernel(x, w_in, w_out, conv_w, A_log, dt_bias, gdn_norm_w, rel_bias, pre_mix_norm,
           post_mix_norm, pre_mlp_norm, post_mlp_norm, w_up, w_down):
    B, S, D = x.shape
    assert D == D_MODEL and S % ROW_TILE == 0 and S % MOBA_BLOCK == 0
    T = B * S
    depth = w_in.shape[0]
    xf = x.reshape(T, D)
    o0, o1, o2, o3, o4 = 0, ATT_WIDTH, 2 * ATT_WIDTH, 3 * ATT_WIDTH, 3 * ATT_WIDTH + 3 * GDN_WIDTH
    o5 = o4 + GDN_WIDTH
    for l in range(depth):
        wi = w_in[l]
        wqT = wi[:, o0:o1].T.astype(BF16)
        wk = wi[:, o1:o2].astype(BF16)
        wvT = wi[:, o2:o3].T.astype(BF16)
        wg = wi[:, o3:o4].astype(BF16)
        wz = wi[:, o4:o5].astype(BF16)
        wab = jnp.pad(wi[:, o5:], ((0, 0), (0, LANES - 2 * GDN_HEADS))).astype(BF16)
        pad8 = lambda v: jnp.pad(v.astype(F32), (0, LANES - GDN_HEADS))[None, :]
        qT, k, vT, gqkv, sz, gb = _inproj(xf, pre_mix_norm[l][None, :], wqT, wk, wvT, wg, wz, wab,
                                          conv_w[l], pad8(A_log[l]), pad8(dt_bias[l]), B, S)
        oT = _attention(rel_bias.astype(F32), qT, k.reshape(B, S, ATT_WIDTH), vT, B, S)
        og = _gdn(gqkv, sz, gb, jnp.tile(gdn_norm_w[l], GDN_HEADS)[None, :], B, S)
        wo = w_out[l].astype(BF16)
        xf = _out_mlp(xf, oT, og, wo[:ATT_WIDTH], wo[ATT_WIDTH:], post_mix_norm[l][None, :],
                      pre_mlp_norm[l][None, :], post_mlp_norm[l][None, :],
                      w_up[l].astype(BF16), w_down[l].astype(BF16), B, S)
    return xf.reshape(B, S, D)
```

```python
import functools
import math

import jax
import jax.numpy as jnp
from jax import lax
from jax.experimental import pallas as pl
from jax.experimental.pallas import tpu as pltpu

F32 = jnp.float32
BF16 = jnp.bfloat16
HI = lax.Precision.HIGHEST

D_MODEL = 1024
HEAD_DIM = 64
ATT_HEADS = 8
GDN_HEADS = 8
ATT_WIDTH = ATT_HEADS * HEAD_DIM
GDN_WIDTH = GDN_HEADS * HEAD_DIM
MOBA_BLOCK = 256
MOBA_TOPK = 3
GDN_CHUNK = 64
CONV_WIDTH = 4
D_FF = 4 * D_MODEL
REL_BUCKETS = 32
REL_MAX_EXACT = 16
REL_MAX_DIST = 128
EPS = 1e-6
NEG = -1e30
LOG2E = math.log2(math.e)

LANES = 128
SUBLANES = 8
VMEM_LIMIT = 56 * 1024 * 1024
ROW_TILE = 512
INPROJ_TILE = 512
FF_TILE = 1024

NT = (((1,), (1,)), ((), ()))
TN = (((0,), (0,)), ((), ()))


def _bucket_lower_bounds():
    def bucket(d):
        if d < REL_MAX_EXACT:
            return d
        t = math.log(d / REL_MAX_EXACT) / math.log(REL_MAX_DIST / REL_MAX_EXACT)
        t = t * (REL_BUCKETS - REL_MAX_EXACT)
        assert d in (REL_MAX_EXACT, REL_MAX_DIST) or abs(t - round(t)) > 1e-6
        return min(REL_MAX_EXACT + int(t + 1e-9), REL_BUCKETS - 1)
    lower = []
    for b in range(REL_BUCKETS):
        d = 0
        while bucket(d) < b:
            d += 1
        lower.append(d)
    return lower


BUCKET_LOWER = _bucket_lower_bounds()


def _sigmoid(x):
    return 0.5 * jnp.tanh(0.5 * x) + 0.5


def _silu_of_half(h):
    return h + h * jnp.tanh(h)


def _rms(x, w):
    return x * lax.rsqrt(jnp.mean(x * x, axis=-1, keepdims=True) + EPS) * w


def _split_bf16(x, parts):
    out = []
    for _ in range(parts):
        h = x.astype(BF16)
        out.append(h)
        x = x - h.astype(F32)
    return out


def _dot_split_rhs(c, x, parts):
    acc = None
    for h in _split_bf16(x, parts):
        d = jnp.dot(c, h, preferred_element_type=F32)
        acc = d if acc is None else acc + d
    return acc


CONV_COLS = 512


def _inproj_kernel(x_ref, xp_ref, nw_ref, wqT_ref, wk_ref, wvT_ref, wg_ref, wz_ref, wab_ref,
                   cw_ref, alog_ref, dtb_ref,
                   qT_ref, k_ref, vT_ref, g_ref, z_ref, gb_ref, *, tiles_per_seq):
    h = _rms(x_ref[...], nw_ref[...]).astype(BF16)

    hp = _rms(xp_ref[...], nw_ref[...]).astype(BF16)
    seq_start = (pl.program_id(0) % tiles_per_seq) == 0
    trow8 = lax.broadcasted_iota(jnp.int32, (SUBLANES, CONV_COLS), 0)
    for c in range(3 * GDN_WIDTH // CONV_COLS):
        cols = slice(c * CONV_COLS, (c + 1) * CONV_COLS)
        cur = jnp.dot(h, wg_ref[:, cols], preferred_element_type=F32)
        prev8 = jnp.dot(hp, wg_ref[:, cols], preferred_element_type=F32)
        prev8 = jnp.where(seq_start, 0.0, prev8)
        cw_half = 0.5 * cw_ref[:, cols]
        acc = cur * cw_half[CONV_WIDTH - 1:CONV_WIDTH]
        for s in range(1, CONV_WIDTH):
            rolled = pltpu.roll(cur, s, 0)
            top = jnp.where(trow8 < s, pltpu.roll(prev8, s, 0), rolled[0:SUBLANES])
            tap = jnp.concatenate([top, rolled[SUBLANES:]], axis=0)
            acc = acc + tap * cw_half[CONV_WIDTH - 1 - s:CONV_WIDTH - s]
        g_ref[:, cols] = _silu_of_half(acc)

    z = jnp.dot(h, wz_ref[...], preferred_element_type=F32)
    z_ref[...] = _silu_of_half(0.5 * z)
    ab = jnp.dot(h, wab_ref[...], preferred_element_type=F32)
    xs = ab + dtb_ref[...]
    log_decay = -jnp.exp(alog_ref[...]) * (jnp.maximum(xs, 0.0) + jnp.log1p(jnp.exp(-jnp.abs(xs))))
    lane = lax.broadcasted_iota(jnp.int32, ab.shape, 1)
    gb_ref[...] = jnp.where(lane < GDN_HEADS, log_decay, _sigmoid(ab))

    qT = lax.dot_general(wqT_ref[...], h, NT, preferred_element_type=F32)
    vT = lax.dot_general(wvT_ref[...], h, NT, preferred_element_type=F32)
    for t in range(INPROJ_TILE // MOBA_BLOCK):
        qT_ref[0, t] = qT[:, t * MOBA_BLOCK:(t + 1) * MOBA_BLOCK]
        vT_ref[0, t] = vT[:, t * MOBA_BLOCK:(t + 1) * MOBA_BLOCK].astype(BF16)
    k_ref[...] = jnp.dot(h, wk_ref[...], preferred_element_type=F32)


def _inproj(xf, nw, wqT, wk, wvT, wg, wz, wab, conv_w, alog_pad, dtb_pad, B, S):
    T = B * S
    TM = INPROJ_TILE
    assert S % TM == 0
    nblk = S // MOBA_BLOCK
    tiles_per_seq = S // TM
    blk_per_tile = TM // MOBA_BLOCK
    const = lambda i: (0, 0)
    row = lambda i: (i, 0)
    tr = lambda i: (i // tiles_per_seq, i % tiles_per_seq, 0, 0)
    prev_rows = lambda i: (jnp.maximum(i * (TM // SUBLANES) - 1, 0), 0)
    single = dict(pipeline_mode=pl.Buffered(1))
    return pl.pallas_call(
        functools.partial(_inproj_kernel, tiles_per_seq=tiles_per_seq),
        grid=(T // TM,),
        in_specs=[
            pl.BlockSpec((TM, D_MODEL), row),
            pl.BlockSpec((SUBLANES, D_MODEL), prev_rows),
            pl.BlockSpec((1, D_MODEL), const),
            pl.BlockSpec(wqT.shape, const, **single),
            pl.BlockSpec(wk.shape, const, **single),
            pl.BlockSpec(wvT.shape, const, **single),
            pl.BlockSpec(wg.shape, const, **single),
            pl.BlockSpec(wz.shape, const, **single),
            pl.BlockSpec(wab.shape, const, **single),
            pl.BlockSpec(conv_w.shape, const),
            pl.BlockSpec((1, LANES), const),
            pl.BlockSpec((1, LANES), const),
        ],
        out_specs=[
            pl.BlockSpec((1, blk_per_tile, ATT_WIDTH, MOBA_BLOCK), tr),
            pl.BlockSpec((TM, ATT_WIDTH), row),
            pl.BlockSpec((1, blk_per_tile, ATT_WIDTH, MOBA_BLOCK), tr),
            pl.BlockSpec((TM, 3 * GDN_WIDTH), row),
            pl.BlockSpec((TM, GDN_WIDTH), row),
            pl.BlockSpec((TM, LANES), row),
        ],
        out_shape=[
            jax.ShapeDtypeStruct((B, nblk, ATT_WIDTH, MOBA_BLOCK), F32),
            jax.ShapeDtypeStruct((T, ATT_WIDTH), F32),
            jax.ShapeDtypeStruct((B, nblk, ATT_WIDTH, MOBA_BLOCK), BF16),
            jax.ShapeDtypeStruct((T, 3 * GDN_WIDTH), F32),
            jax.ShapeDtypeStruct((T, GDN_WIDTH), F32),
            jax.ShapeDtypeStruct((T, LANES), F32),
        ],
        compiler_params=pltpu.CompilerParams(
            dimension_semantics=("arbitrary",), vmem_limit_bytes=VMEM_LIMIT),
        name="inproj",
    )(xf, xf, nw, wqT, wk, wvT, wg, wz, wab, conv_w, alog_pad, dtb_pad)


V_ROWS = HEAD_DIM + 16


def _paired_pos(i, nblk):
    return jnp.where(i < nblk // 2, 2 * i, 2 * (nblk - 1 - i) + 1)


GDN_TILE = 256
GDN_HALF = 2 * LANES


def _gdn_stages(yq_ref, yk_ref, yv_ref, gb_ref, sz_ref, nw_ref, out_ref,
                u_ref, wq_ref, a_ref, kd_ref, gl_ref, st_ref):
    C = GDN_CHUNK
    W = GDN_WIDTH
    TILE = GDN_TILE
    npair = W // LANES

    r_w = lax.broadcasted_iota(jnp.int32, (GDN_HALF, GDN_HALF), 0)
    c_w = lax.broadcasted_iota(jnp.int32, (GDN_HALF, GDN_HALF), 1)
    head_ones = jnp.where((r_w // HEAD_DIM) == (c_w // HEAD_DIM), 1.0, 0.0).astype(BF16)
    ltri_bd = jnp.where(((r_w // C) == (c_w // C)) & (c_w <= r_w), 1.0, 0.0).astype(BF16)
    tok = lax.broadcasted_iota(jnp.int32, (TILE, W), 0) % C
    col = lax.broadcasted_iota(jnp.int32, (TILE, W), 1) % HEAD_DIM
    causal_t = tok >= col
    strict_t = tok > col
    lane_t = lax.broadcasted_iota(jnp.int32, (TILE, LANES), 1)

    lane = lax.broadcasted_iota(jnp.int32, (C, LANES), 1)
    rowi = lax.broadcasted_iota(jnp.int32, (C, LANES), 0)
    first_head = lane < HEAD_DIM
    strict = rowi > (lane % HEAD_DIM)
    eye2 = jnp.where(rowi == (lane % HEAD_DIM), 1.0, 0.0)
    lane2 = lax.broadcasted_iota(jnp.int32, (C, 2 * LANES), 1)
    first_head2 = (lane2 % LANES) < HEAD_DIM
    r_l = lax.broadcasted_iota(jnp.int32, (LANES, LANES), 0)
    c_l = lax.broadcasted_iota(jnp.int32, (LANES, LANES), 1)
    same_head = (r_l // HEAD_DIM) == (c_l // HEAD_DIM)
    pair_ones = jnp.where(same_head, 1.0, 0.0).astype(BF16)

    def stack(x, mask):
        return jnp.concatenate([jnp.where(mask, x, 0.0), jnp.where(mask, 0.0, x)], axis=0)

    dot = functools.partial(jnp.dot, preferred_element_type=F32)

    def head_sumsq(y):
        y2 = (y * y).astype(BF16)
        return jnp.concatenate([dot(y2[:, h:h + GDN_HALF], head_ones) for h in range(0, W, GDN_HALF)],
                               axis=1)

    def solve_stages(slot):
        yq = yq_ref[...]
        yk = yk_ref[...]
        yv = yv_ref[...]
        qn = yq * lax.rsqrt(head_sumsq(yq) + EPS) * (HEAD_DIM ** -0.5)
        kn = yk * lax.rsqrt(head_sumsq(yk) + EPS)
        yield
        gbt = gb_ref[...]

        def spread(col0):
            pairs = []
            for p in range(npair):
                a = jnp.broadcast_to(gbt[:, col0 + 2 * p:col0 + 2 * p + 1], (TILE, LANES))
                b = jnp.broadcast_to(gbt[:, col0 + 2 * p + 1:col0 + 2 * p + 2], (TILE, LANES))
                pairs.append(jnp.where(lane_t < HEAD_DIM, a, b))
            return jnp.concatenate(pairs, axis=1)

        g = spread(0)
        beta = spread(GDN_HEADS)
        gcd = _dot_split_rhs(ltri_bd, jnp.concatenate([g, jnp.where(strict_t, g, 0.0)], axis=1), 2)
        yield
        gc = gcd[:, :W]
        decay = jnp.where(causal_t, jnp.exp(jnp.where(causal_t, gcd[:, W:], 0.0)), 0.0)
        egc = jnp.exp(gc)
        kb = kn * beta
        rv = yv * beta
        rk = kb * egc
        qd = qn * egc
        for cc in range(TILE // C):
            rs = slice(cc * C, (cc + 1) * C)
            g_last = gc[(cc + 1) * C - 1:(cc + 1) * C, :]
            kd_ref[slot, rs, :] = (kn[rs] * jnp.exp(g_last - gc[rs])).astype(BF16)
            gl_ref[slot, cc * SUBLANES:(cc + 1) * SUBLANES, :] = (
                jnp.broadcast_to(jnp.exp(g_last), (SUBLANES, W)))
        units = [(slice(cc * C, (cc + 1) * C), slice(LANES * p, LANES * (p + 1)), cc)
                 for cc in range(TILE // C) for p in range(npair)]
        kqs = [lax.dot_general(jnp.concatenate([kn[rs, ls], qn[rs, ls]], axis=0).astype(BF16),
                               stack(kn[rs, ls], first_head).astype(BF16), NT,
                               preferred_element_type=F32) for rs, ls, _ in units]
        yield
        ps = [-jnp.where(strict, kq[0:C] * beta[rs, ls] * decay[rs, ls], 0.0)
              for kq, (rs, ls, _) in zip(kqs, units)]
        ss = [eye2 + p for p in ps]
        ps = [dot(p.astype(BF16), stack(p, first_head).astype(BF16)) for p in ps]
        yield
        nround = int(math.log2(C))
        for k in range(1, nround):
            rhs = [stack(s_, first_head).astype(BF16) for s_ in ss]
            if k + 1 < nround:
                rhs = [jnp.concatenate([stack(p, first_head).astype(BF16), sx], axis=1)
                       for p, sx in zip(ps, rhs)]
            outs = [dot(p.astype(BF16), sx) for p, sx in zip(ps, rhs)]
            if k + 1 < nround:
                ps = [o[:, :LANES] for o in outs]
                ss = [s_ + o[:, LANES:] for s_, o in zip(ss, outs)]
            else:
                ss = [s_ + o for s_, o in zip(ss, outs)]
            yield
        xs = [dot(s_.astype(BF16),
                  stack(jnp.concatenate([rv[rs, ls], rk[rs, ls]], axis=1), first_head2).astype(BF16))
              for s_, (rs, ls, _) in zip(ss, units)]
        yield
        for x, kq, (rs, ls, cc) in zip(xs, kqs, units):
            u_ref[slot, rs, ls] = x[:, :LANES]
            wq_ref[slot, cc, 0:C, ls] = x[:, LANES:].astype(BF16)
            wq_ref[slot, cc, C:2 * C, ls] = qd[rs, ls].astype(BF16)
            a_ref[slot, rs, ls] = (kq[C:2 * C] * decay[rs, ls]).astype(BF16)

    lss = [slice(LANES * p, LANES * (p + 1)) for p in range(npair)]

    def recurrence_stages(slot, seq_start):
        states = [jnp.where(seq_start, 0.0, st_ref[p]) for p in range(npair)]
        pending = None

        def finish(rs, os_):
            for ls, o in zip(lss, os_):
                ms = dot((o * o).astype(BF16), pair_ones) * (1.0 / HEAD_DIM)
                out_ref[rs, ls] = o * lax.rsqrt(ms + EPS) * nw_ref[:, ls] * sz_ref[rs, ls]

        for cc in range(TILE // C):
            rs = slice(cc * C, (cc + 1) * C)
            wqs = [dot(wq_ref[slot, cc, :, ls], st.astype(BF16)) for ls, st in zip(lss, states)]
            if pending is not None:
                finish(*pending)
            yield
            v_news = [u_ref[slot, rs, ls] - wq[0:C] for ls, wq in zip(lss, wqs)]
            kvs = [lax.dot_general(kd_ref[slot, rs, ls], v.astype(BF16), TN, preferred_element_type=F32)
                   for ls, v in zip(lss, v_news)]
            os_ = [wq[C:2 * C] + dot(a_ref[slot, rs, ls], stack(v, first_head).astype(BF16))
                   for ls, wq, v in zip(lss, wqs, v_news)]
            states = [st * gl_ref[slot, cc * SUBLANES:cc * SUBLANES + 1, ls] + jnp.where(same_head, kv, 0.0)
                      for ls, st, kv in zip(lss, states, kvs)]
            pending = (rs, os_)
            yield
        finish(*pending)
        for p in range(npair):
            st_ref[p] = states[p]

    return solve_stages, recurrence_stages


def _interleave_weighted(*gens_and_weights):
    live = [[gen, weight] for gen, weight in gens_and_weights]
    while live:
        for entry in list(live):
            for _ in range(entry[1]):
                try:
                    next(entry[0])
                except StopIteration:
                    live.remove(entry)
                    break


def _mixers_kernel(relb_ref, qlo_ref, qhi_ref, k_ref, vT_ref,
                   yq_ref, yk_ref, yv_ref, gb_ref, sz_ref, nw_ref,
                   oT_ref, og_ref,
                   kb_ref, km_ref, va_ref, bias_ref, addm_ref, qh_ref, lg_ref, moff_ref,
                   u_ref, wq_ref, a_ref, kd_ref, gl_ref, st_ref,
                   *, nblk, nbatch, tiles_per_seq, nsteps):
    step = pl.program_id(0)
    BLK = MOBA_BLOCK
    half = nblk // 2
    pair = jnp.minimum(step, nsteps - 2)
    hp = pair // (nbatch * (half // 2))
    b = (pair // (half // 2)) % nbatch
    vcur = (pair // (half // 2)) % 2
    vprev = ((step - 1) // (half // 2)) % 2
    scoring = step < nsteps - 1
    new_kv = (pair % (half // 2)) == 0

    solve_stages, recurrence_stages = _gdn_stages(
        yq_ref, yk_ref, yv_ref, gb_ref, sz_ref, nw_ref, og_ref,
        u_ref, wq_ref, a_ref, kd_ref, gl_ref, st_ref)
    seq_start = ((step - 1) % tiles_per_seq) == 0

    @pl.when((b == 0) & new_kv & scoring)
    def _():
        kk = lax.broadcasted_iota(jnp.int32, (BLK, BLK), 0)
        qq = lax.broadcasted_iota(jnp.int32, (BLK, BLK), 1)
        for hh in range(2):
            h = 2 * hp + hh
            for kind in range(2):
                d = qq - kk + kind * BLK
                val = jnp.full((BLK, BLK), relb_ref[h, REL_BUCKETS - 1], F32)
                for bkt in range(REL_BUCKETS - 2, -1, -1):
                    val = jnp.where(d < BUCKET_LOWER[bkt + 1], relb_ref[h, bkt], val)
                val = val * LOG2E
                if kind == 0:
                    val = jnp.where(d >= 0, val, NEG)
                bias_ref[hh, kind] = val

    @pl.when(new_kv & scoring)
    def _():
        lane = lax.broadcasted_iota(jnp.int32, (1, LANES), 1)
        ones_row = jnp.where(lax.broadcasted_iota(jnp.int32, (V_ROWS - HEAD_DIM, BLK), 0) == 0,
                             1.0, 0.0).astype(BF16)
        for j in range(nblk):
            kj = k_ref[0, j * BLK:(j + 1) * BLK, :]
            kb_ref[j * BLK:(j + 1) * BLK, :] = kj.astype(BF16)
            kmj = jnp.sum(kj, axis=0, keepdims=True) * (1.0 / BLK)
            km_ref[j:j + 1, :] = jnp.where(lane < HEAD_DIM, kmj, 0.0)
            km_ref[nblk + j:nblk + j + 1, :] = jnp.where(lane >= HEAD_DIM, kmj, 0.0)
            for hh in range(2):
                va_ref[vcur, j, hh, 0:HEAD_DIM, :] = vT_ref[0, j, HEAD_DIM * hh:HEAD_DIM * (hh + 1), :]
                va_ref[vcur, j, hh, HEAD_DIM:V_ROWS, :] = ones_row

    def item_tiles(t):
        i_hi = nblk - 1 - t
        tiles = [(0, t, "own"), (1, i_hi, "own"), (1, i_hi - 1, "prev")]
        if t >= 1:
            tiles.append((0, t - 1, "prev"))
        tiles += [(0, j, "far") for j in range(t - 1)]
        tiles += [(1, j, "far") for j in range(i_hi - 1)]
        assert len(tiles) == nblk + 1
        return tiles

    def score_stages(parity):
        ridx = lax.broadcasted_iota(jnp.int32, (nblk, BLK), 0)
        sub = lax.broadcasted_iota(jnp.int32, (LANES, BLK), 0)
        scale = HEAD_DIM ** -0.5 * LOG2E
        for e in range(2):
            t = 2 * parity + e
            slot = 2 * parity + e
            q_of = ((qlo_ref, e, t), (qhi_ref, 1 - e, nblk - 1 - t))
            for s, (q_ref, w, qi) in enumerate(q_of):
                qT = q_ref[0, w]
                gT = jnp.dot(km_ref[...], qT, precision=HI, preferred_element_type=F32)
                past = ridx < qi
                for hh in range(2):
                    gm = jnp.where(past, gT[nblk * hh:nblk * (hh + 1)], -jnp.inf)
                    cnt = jnp.zeros((nblk, BLK), F32)
                    for jp in range(nblk):
                        row = gm[jp:jp + 1, :]
                        beats = (row > gm) | ((row == gm) & (ridx > jp))
                        cnt = cnt + jnp.where(beats, 1.0, 0.0)
                    visible = past & (cnt < MOBA_TOPK)
                    addm_ref[e, s, nblk * hh:nblk * (hh + 1), :] = jnp.where(visible, 0.0, NEG)
                    in_head = (sub >= HEAD_DIM * hh) & (sub < HEAD_DIM * (hh + 1))
                    qh_ref[e, s, hh] = jnp.where(in_head, qT * scale, 0.0).astype(BF16)
            yield
            tiles = item_tiles(t)
            for hh in range(2):
                cmax = {0: [], 1: []}
                offs = []
                for n, (s, kblk, cls) in enumerate(tiles):
                    lg = jnp.dot(kb_ref[kblk * BLK:(kblk + 1) * BLK, :], qh_ref[e, s, hh],
                                 preferred_element_type=F32)
                    if cls != "far":
                        lg = lg + bias_ref[hh, 0 if cls == "own" else 1]
                    lg_ref[slot, hh, n] = lg
                    cm = jnp.max(lg, axis=0, keepdims=True)
                    off = None
                    if cls != "own":
                        off = addm_ref[e, s, nblk * hh + kblk:nblk * hh + kblk + 1, :]
                        if cls == "far":
                            off = off + relb_ref[2 * hp + hh, REL_BUCKETS - 1] * LOG2E
                        cm = cm + off
                    cmax[s].append(cm)
                    offs.append(off)
                    yield
                m = {s: functools.reduce(jnp.maximum, cmax[s]) for s in (0, 1)}
                for n, (s, _, _) in enumerate(tiles):
                    moff_ref[slot, hh, n:n + 1, :] = m[s] if offs[n] is None else m[s] - offs[n]

    def softmax_pv_stages(parity):
        for e in range(2):
            t = 2 * parity + e
            slot = 2 * parity + e
            tiles = item_tiles(t)
            for hh in range(2):
                acc = {0: None, 1: None}
                for n, (s, kblk, _) in enumerate(tiles):
                    p = jnp.exp2(lg_ref[slot, hh, n] - moff_ref[slot, hh, n:n + 1, :])
                    pvn = jnp.dot(va_ref[vprev, kblk, hh], p.astype(BF16),
                                  preferred_element_type=F32)
                    acc[s] = pvn if acc[s] is None else acc[s] + pvn
                    yield
                for s in (0, 1):
                    oT_ref[0, 2 * e + s, HEAD_DIM * hh:HEAD_DIM * (hh + 1), :] = (
                        acc[s][0:HEAD_DIM] / acc[s][HEAD_DIM:HEAD_DIM + 1])

    ATT, GDN = 3, 1

    @pl.when(step == 0)
    def _():
        st_ref[...] = jnp.zeros(st_ref.shape, F32)
        _interleave_weighted((score_stages(0), ATT), (solve_stages(0), GDN))

    for parity in range(2):
        @pl.when((step > 0) & (step < nsteps - 1) & (step % 2 == parity))
        def _(parity=parity):
            _interleave_weighted((recurrence_stages(1 - parity, seq_start), GDN),
                                 (softmax_pv_stages(1 - parity), ATT),
                                 (solve_stages(parity), GDN),
                                 (score_stages(parity), ATT))

    @pl.when(step == nsteps - 1)
    def _():
        last = (nsteps - 2) % 2
        _interleave_weighted((recurrence_stages(last, seq_start), GDN), (softmax_pv_stages(last), ATT))


def _token_mixers(rel_bias, qT, k3, vT, gqkv, sz, gb, nw_row, B, S):
    T = B * S
    W = GDN_WIDTH
    TILE = GDN_TILE
    nblk = S // MOBA_BLOCK
    half = nblk // 2
    assert TILE == GDN_HALF and S % TILE == 0
    assert BUCKET_LOWER[REL_BUCKETS - 1] <= MOBA_BLOCK + 1
    assert nblk == 8 and nblk + 1 <= 2 * SUBLANES
    nchunk = TILE // GDN_CHUNK
    ntiles = T // TILE
    npairs = (ATT_HEADS // 2) * B * (half // 2)
    assert npairs == ntiles
    nsteps = ntiles + 1

    def scored(s):
        p = jnp.minimum(s, npairs - 1)
        return p // (B * (half // 2)), (p // (half // 2)) % B, p % (half // 2)

    def done(s):
        p = jnp.maximum(s - 1, 0)
        return p // (B * (half // 2)), (p // (half // 2)) % B, p % (half // 2)

    def q_lo(s):
        hp, b, m = scored(s)
        return (b, m, hp, 0)

    def q_hi(s):
        hp, b, m = scored(s)
        return (b, half - 1 - m, hp, 0)

    def k_blk(s):
        hp, b, _ = scored(s)
        return (b, 0, hp)

    def v_blk(s):
        hp, b, _ = scored(s)
        return (b, 0, hp, 0)

    def o_blk(s):
        hp, b, m = done(s)
        return (b, m, hp, 0)

    cur_tile = lambda off: (lambda s: (jnp.minimum(s, ntiles - 1), off))
    prev_tile = lambda s: (jnp.maximum(s - 1, 0), 0)
    return pl.pallas_call(
        functools.partial(_mixers_kernel, nblk=nblk, nbatch=B, tiles_per_seq=S // TILE, nsteps=nsteps),
        grid=(nsteps,),
        in_specs=[
            pl.BlockSpec(memory_space=pltpu.SMEM),
            pl.BlockSpec((1, 2, LANES, MOBA_BLOCK), q_lo),
            pl.BlockSpec((1, 2, LANES, MOBA_BLOCK), q_hi),
            pl.BlockSpec((1, S, LANES), k_blk),
            pl.BlockSpec((1, nblk, LANES, MOBA_BLOCK), v_blk),
            pl.BlockSpec((TILE, W), cur_tile(0)),
            pl.BlockSpec((TILE, W), cur_tile(1)),
            pl.BlockSpec((TILE, W), cur_tile(2)),
            pl.BlockSpec((TILE, LANES), cur_tile(0)),
            pl.BlockSpec((TILE, W), prev_tile),
            pl.BlockSpec((1, W), lambda s: (0, 0)),
        ],
        out_specs=[
            pl.BlockSpec((1, 4, LANES, MOBA_BLOCK), o_blk),
            pl.BlockSpec((TILE, W), prev_tile),
        ],
        out_shape=[
            jax.ShapeDtypeStruct((B, nblk, ATT_WIDTH, MOBA_BLOCK), F32),
            jax.ShapeDtypeStruct((T, GDN_WIDTH), F32),
        ],
        scratch_shapes=[
            pltpu.VMEM((S, LANES), BF16),
            pltpu.VMEM((2 * nblk, LANES), F32),
            pltpu.VMEM((2, nblk, 2, V_ROWS, MOBA_BLOCK), BF16),
            pltpu.VMEM((2, 2, MOBA_BLOCK, MOBA_BLOCK), F32),
            pltpu.VMEM((2, 2, 2 * nblk, MOBA_BLOCK), F32),
            pltpu.VMEM((2, 2, 2, LANES, MOBA_BLOCK), BF16),
            pltpu.VMEM((4, 2, nblk + 1, MOBA_BLOCK, MOBA_BLOCK), F32),
            pltpu.VMEM((4, 2, 2 * SUBLANES, MOBA_BLOCK), F32),
            pltpu.VMEM((2, TILE, W), F32),
            pltpu.VMEM((2, nchunk, 2 * GDN_CHUNK, W), BF16),
            pltpu.VMEM((2, TILE, W), BF16),
            pltpu.VMEM((2, TILE, W), BF16),
            pltpu.VMEM((2, nchunk * SUBLANES, W), F32),
            pltpu.VMEM((W // LANES, LANES, LANES), F32),
        ],
        compiler_params=pltpu.CompilerParams(
            dimension_semantics=("arbitrary",), vmem_limit_bytes=VMEM_LIMIT),
        name="token_mixers",
    )(rel_bias, qT, qT, k3, vT, gqkv, gqkv, gqkv, gb, sz, nw_row)


def _out_mlp_kernel(x_ref, oTa_ref, oTb_ref, og_ref, woa_ref, wog_ref, pmn_ref, pre_ref, post_ref,
                    wup_ref, wdn_ref, out_ref):
    oT = jnp.concatenate([oTa_ref[0, 0], oTb_ref[0, 0]], axis=1)
    o_att = oT.T.astype(BF16)
    mix = jnp.dot(o_att, woa_ref[...], preferred_element_type=F32)
    mix = mix + jnp.dot(og_ref[...].astype(BF16), wog_ref[...], preferred_element_type=F32)
    x1 = x_ref[...] + _rms(mix, pmn_ref[...])
    h = _rms(x1, pre_ref[...]).astype(BF16)
    acc = jnp.zeros((ROW_TILE, D_MODEL), F32)
    for c in range(D_FF // FF_TILE):
        up = jnp.dot(h, wup_ref[:, c * FF_TILE:(c + 1) * FF_TILE], preferred_element_type=F32)
        act = jnp.square(jnp.maximum(up, 0.0)).astype(BF16)
        acc = acc + jnp.dot(act, wdn_ref[c * FF_TILE:(c + 1) * FF_TILE, :], preferred_element_type=F32)
    out_ref[...] = x1 + _rms(acc, post_ref[...])


def _out_mlp(xf, oT, og, woa, wog, pmn, pre, post, wup, wdn, B, S):
    T = B * S
    nblk = S // MOBA_BLOCK
    tiles_per_seq = S // ROW_TILE
    assert ROW_TILE == 2 * MOBA_BLOCK
    const = lambda i: (0, 0)
    row = lambda i: (i, 0)

    def att_block(which):
        def index(i):
            blk = 2 * (i % tiles_per_seq) + which
            return (i // tiles_per_seq, _paired_pos(blk, nblk), 0, 0)
        return index

    single = dict(pipeline_mode=pl.Buffered(1))
    return pl.pallas_call(
        _out_mlp_kernel,
        grid=(T // ROW_TILE,),
        in_specs=[
            pl.BlockSpec((ROW_TILE, D_MODEL), row),
            pl.BlockSpec((1, 1, ATT_WIDTH, MOBA_BLOCK), att_block(0)),
            pl.BlockSpec((1, 1, ATT_WIDTH, MOBA_BLOCK), att_block(1)),
            pl.BlockSpec((ROW_TILE, GDN_WIDTH), row),
            pl.BlockSpec(woa.shape, const, **single),
            pl.BlockSpec(wog.shape, const, **single),
            pl.BlockSpec((1, D_MODEL), const),
            pl.BlockSpec((1, D_MODEL), const),
            pl.BlockSpec((1, D_MODEL), const),
            pl.BlockSpec(wup.shape, const, **single),
            pl.BlockSpec(wdn.shape, const, **single),
        ],
        out_specs=pl.BlockSpec((ROW_TILE, D_MODEL), row),
        out_shape=jax.ShapeDtypeStruct((T, D_MODEL), F32),
        compiler_params=pltpu.CompilerParams(
            dimension_semantics=("arbitrary",), vmem_limit_bytes=VMEM_LIMIT),
        name="out_mlp",
    )(xf, oT, oT, og, woa, wog, pmn, pre, post, wup, wdn)


def kernel(x, w_in, w_out, conv_w, A_log, dt_bias, gdn_norm_w, rel_bias, pre_mix_norm,
           post_mix_norm, pre_mlp_norm, post_mlp_norm, w_up, w_down):
    B, S, D = x.shape
    assert D == D_MODEL and S % ROW_TILE == 0 and S % MOBA_BLOCK == 0
    T = B * S
    depth = w_in.shape[0]
    xf = x.reshape(T, D)
    o0, o1, o2, o3, o4 = 0, ATT_WIDTH, 2 * ATT_WIDTH, 3 * ATT_WIDTH, 3 * ATT_WIDTH + 3 * GDN_WIDTH
    o5 = o4 + GDN_WIDTH
    for l in range(depth):
        wi = w_in[l]
        wqT = wi[:, o0:o1].T.astype(BF16)
        wk = wi[:, o1:o2].astype(BF16)
        wvT = wi[:, o2:o3].T.astype(BF16)
        wg = wi[:, o3:o4].astype(BF16)
        wz = wi[:, o4:o5].astype(BF16)
        wab = jnp.pad(wi[:, o5:], ((0, 0), (0, LANES - 2 * GDN_HEADS))).astype(BF16)
        pad8 = lambda v: jnp.pad(v.astype(F32), (0, LANES - GDN_HEADS))[None, :]
        qT, k, vT, gqkv, sz, gb = _inproj(xf, pre_mix_norm[l][None, :], wqT, wk, wvT, wg, wz, wab,
                                          conv_w[l], pad8(A_log[l]), pad8(dt_bias[l]), B, S)
        oT, og = _token_mixers(rel_bias.astype(F32), qT, k.reshape(B, S, ATT_WIDTH), vT, gqkv, sz, gb,
                               jnp.tile(gdn_norm_w[l], GDN_HEADS)[None, :], B, S)
        wo = w_out[l].astype(BF16)
        xf = _out_mlp(xf, oT, og, wo[:ATT_WIDTH], wo[ATT_WIDTH:], post_mix_norm[l][None, :],
                      pre_mlp_norm[l][None, :], post_mlp_norm[l][None, :],
                      w_up[l].astype(BF16), w_down[l].astype(BF16), B, S)
    return xf.reshape(B, S, D)
```

```python
import functools
import math

import jax
import jax.numpy as jnp
from jax import lax
from jax.experimental import pallas as pl
from jax.experimental.pallas import tpu as pltpu

F32 = jnp.float32
BF16 = jnp.bfloat16
HI = lax.Precision.HIGHEST

D_MODEL = 1024
HEAD_DIM = 64
ATT_HEADS = 8
GDN_HEADS = 8
ATT_WIDTH = ATT_HEADS * HEAD_DIM
GDN_WIDTH = GDN_HEADS * HEAD_DIM
MOBA_BLOCK = 256
MOBA_TOPK = 3
GDN_CHUNK = 64
CONV_WIDTH = 4
D_FF = 4 * D_MODEL
REL_BUCKETS = 32
REL_MAX_EXACT = 16
REL_MAX_DIST = 128
EPS = 1e-6
NEG = -1e30
LOG2E = math.log2(math.e)

LANES = 128
SUBLANES = 8
VMEM_LIMIT = 56 * 1024 * 1024
ROW_TILE = 512
INPROJ_TILE = 512
FF_TILE = 1024

NT = (((1,), (1,)), ((), ()))
TN = (((0,), (0,)), ((), ()))


def _bucket_lower_bounds():
    def bucket(d):
        if d < REL_MAX_EXACT:
            return d
        t = math.log(d / REL_MAX_EXACT) / math.log(REL_MAX_DIST / REL_MAX_EXACT)
        t = t * (REL_BUCKETS - REL_MAX_EXACT)
        assert d in (REL_MAX_EXACT, REL_MAX_DIST) or abs(t - round(t)) > 1e-6
        return min(REL_MAX_EXACT + int(t + 1e-9), REL_BUCKETS - 1)
    lower = []
    for b in range(REL_BUCKETS):
        d = 0
        while bucket(d) < b:
            d += 1
        lower.append(d)
    return lower


BUCKET_LOWER = _bucket_lower_bounds()


def _sigmoid(x):
    return 0.5 * jnp.tanh(0.5 * x) + 0.5


def _silu_of_half(h):
    return h + h * jnp.tanh(h)


def _rms(x, w):
    return x * lax.rsqrt(jnp.mean(x * x, axis=-1, keepdims=True) + EPS) * w


def _split_bf16(x, parts):
    out = []
    for _ in range(parts):
        h = x.astype(BF16)
        out.append(h)
        x = x - h.astype(F32)
    return out


def _dot_split_rhs(c, x, parts):
    acc = None
    for h in _split_bf16(x, parts):
        d = jnp.dot(c, h, preferred_element_type=F32)
        acc = d if acc is None else acc + d
    return acc


CONV_COLS = 512


def _inproj_kernel(x_ref, xp_ref, nw_ref, wqT_ref, wk_ref, wvT_ref, wg_ref, wz_ref, wab_ref,
                   cw_ref, alog_ref, dtb_ref,
                   qT_ref, k_ref, ksum_ref, va_ref, g_ref, z_ref, gb_ref, *, tiles_per_seq):
    h = _rms(x_ref[...], nw_ref[...]).astype(BF16)

    hp = _rms(xp_ref[...], nw_ref[...]).astype(BF16)
    seq_start = (pl.program_id(0) % tiles_per_seq) == 0
    trow8 = lax.broadcasted_iota(jnp.int32, (SUBLANES, CONV_COLS), 0)
    for c in range(3 * GDN_WIDTH // CONV_COLS):
        cols = slice(c * CONV_COLS, (c + 1) * CONV_COLS)
        cur = jnp.dot(h, wg_ref[:, cols], preferred_element_type=F32)
        prev8 = jnp.dot(hp, wg_ref[:, cols], preferred_element_type=F32)
        prev8 = jnp.where(seq_start, 0.0, prev8)
        cw_half = 0.5 * cw_ref[:, cols]
        acc = cur * cw_half[CONV_WIDTH - 1:CONV_WIDTH]
        for s in range(1, CONV_WIDTH):
            rolled = pltpu.roll(cur, s, 0)
            top = jnp.where(trow8 < s, pltpu.roll(prev8, s, 0), rolled[0:SUBLANES])
            tap = jnp.concatenate([top, rolled[SUBLANES:]], axis=0)
            acc = acc + tap * cw_half[CONV_WIDTH - 1 - s:CONV_WIDTH - s]
        g_ref[:, cols] = _silu_of_half(acc)

    z = jnp.dot(h, wz_ref[...], preferred_element_type=F32)
    z_ref[...] = _silu_of_half(0.5 * z)
    ab = jnp.dot(h, wab_ref[...], preferred_element_type=F32)
    xs = ab + dtb_ref[...]
    log_decay = -jnp.exp(alog_ref[...]) * (jnp.maximum(xs, 0.0) + jnp.log1p(jnp.exp(-jnp.abs(xs))))
    lane = lax.broadcasted_iota(jnp.int32, ab.shape, 1)
    gb_ref[...] = jnp.where(lane < GDN_HEADS, log_decay, _sigmoid(ab))

    qT = lax.dot_general(wqT_ref[...], h, NT, preferred_element_type=F32)
    vT = lax.dot_general(wvT_ref[...], h, NT, preferred_element_type=F32)
    k = jnp.dot(h, wk_ref[...], preferred_element_type=F32)
    k_ref[...] = k.astype(BF16)
    ones_row = jnp.where(lax.broadcasted_iota(jnp.int32, (V_ROWS - HEAD_DIM, MOBA_BLOCK), 0) == 0,
                         1.0, 0.0).astype(BF16)
    ksum_ref[...] = jnp.zeros(ksum_ref.shape, F32)
    for t in range(INPROJ_TILE // MOBA_BLOCK):
        blk = slice(t * MOBA_BLOCK, (t + 1) * MOBA_BLOCK)
        qT_ref[0, t] = qT[:, blk]
        ksum_ref[0, t:t + 1, :] = jnp.sum(k[blk], axis=0, keepdims=True)
        for hh in range(ATT_HEADS):
            va_ref[0, t, hh, 0:HEAD_DIM, :] = vT[HEAD_DIM * hh:HEAD_DIM * (hh + 1), blk].astype(BF16)
            va_ref[0, t, hh, HEAD_DIM:V_ROWS, :] = ones_row


def _inproj(xf, nw, wqT, wk, wvT, wg, wz, wab, conv_w, alog_pad, dtb_pad, B, S):
    T = B * S
    TM = INPROJ_TILE
    assert S % TM == 0
    nblk = S // MOBA_BLOCK
    tiles_per_seq = S // TM
    blk_per_tile = TM // MOBA_BLOCK
    const = lambda i: (0, 0)
    row = lambda i: (i, 0)
    tr = lambda i: (i // tiles_per_seq, i % tiles_per_seq, 0, 0)
    prev_rows = lambda i: (jnp.maximum(i * (TM // SUBLANES) - 1, 0), 0)
    single = dict(pipeline_mode=pl.Buffered(1))
    return pl.pallas_call(
        functools.partial(_inproj_kernel, tiles_per_seq=tiles_per_seq),
        grid=(T // TM,),
        in_specs=[
            pl.BlockSpec((TM, D_MODEL), row),
            pl.BlockSpec((SUBLANES, D_MODEL), prev_rows),
            pl.BlockSpec((1, D_MODEL), const),
            pl.BlockSpec(wqT.shape, const, **single),
            pl.BlockSpec(wk.shape, const, **single),
            pl.BlockSpec(wvT.shape, const, **single),
            pl.BlockSpec(wg.shape, const, **single),
            pl.BlockSpec(wz.shape, const, **single),
            pl.BlockSpec(wab.shape, const, **single),
            pl.BlockSpec(conv_w.shape, const),
            pl.BlockSpec((1, LANES), const),
            pl.BlockSpec((1, LANES), const),
        ],
        out_specs=[
            pl.BlockSpec((1, blk_per_tile, ATT_WIDTH, MOBA_BLOCK), tr),
            pl.BlockSpec((TM, ATT_WIDTH), row),
            pl.BlockSpec((1, SUBLANES, ATT_WIDTH), lambda i: (i, 0, 0)),
            pl.BlockSpec((1, blk_per_tile, ATT_HEADS, V_ROWS, MOBA_BLOCK),
                         lambda i: (i // tiles_per_seq, i % tiles_per_seq, 0, 0, 0)),
            pl.BlockSpec((TM, 3 * GDN_WIDTH), row),
            pl.BlockSpec((TM, GDN_WIDTH), row),
            pl.BlockSpec((TM, LANES), row),
        ],
        out_shape=[
            jax.ShapeDtypeStruct((B, nblk, ATT_WIDTH, MOBA_BLOCK), F32),
            jax.ShapeDtypeStruct((T, ATT_WIDTH), BF16),
            jax.ShapeDtypeStruct((T // TM, SUBLANES, ATT_WIDTH), F32),
            jax.ShapeDtypeStruct((B, nblk, ATT_HEADS, V_ROWS, MOBA_BLOCK), BF16),
            jax.ShapeDtypeStruct((T, 3 * GDN_WIDTH), F32),
            jax.ShapeDtypeStruct((T, GDN_WIDTH), F32),
            jax.ShapeDtypeStruct((T, LANES), F32),
        ],
        compiler_params=pltpu.CompilerParams(
            dimension_semantics=("arbitrary",), vmem_limit_bytes=VMEM_LIMIT),
        name="inproj",
    )(xf, xf, nw, wqT, wk, wvT, wg, wz, wab, conv_w, alog_pad, dtb_pad)


V_ROWS = HEAD_DIM + 16


def _paired_pos(i, nblk):
    return jnp.where(i < nblk // 2, 2 * i, 2 * (nblk - 1 - i) + 1)


GDN_TILE = 256
GDN_HALF = 2 * LANES


def _gdn_stages(yq_ref, yk_ref, yv_ref, gb_ref, sz_ref, nw_ref, out_ref,
                u_ref, wq_ref, a_ref, kd_ref, gl_ref, st_ref):
    C = GDN_CHUNK
    W = GDN_WIDTH
    TILE = GDN_TILE
    npair = W // LANES

    r_w = lax.broadcasted_iota(jnp.int32, (GDN_HALF, GDN_HALF), 0)
    c_w = lax.broadcasted_iota(jnp.int32, (GDN_HALF, GDN_HALF), 1)
    head_ones = jnp.where((r_w // HEAD_DIM) == (c_w // HEAD_DIM), 1.0, 0.0).astype(BF16)
    ltri_bd = jnp.where(((r_w // C) == (c_w // C)) & (c_w <= r_w), 1.0, 0.0).astype(BF16)
    tok = lax.broadcasted_iota(jnp.int32, (TILE, W), 0) % C
    col = lax.broadcasted_iota(jnp.int32, (TILE, W), 1) % HEAD_DIM
    causal_t = tok >= col
    strict_t = tok > col
    lane_t = lax.broadcasted_iota(jnp.int32, (TILE, LANES), 1)

    lane = lax.broadcasted_iota(jnp.int32, (C, LANES), 1)
    rowi = lax.broadcasted_iota(jnp.int32, (C, LANES), 0)
    first_head = lane < HEAD_DIM
    strict = rowi > (lane % HEAD_DIM)
    eye2 = jnp.where(rowi == (lane % HEAD_DIM), 1.0, 0.0)
    lane2 = lax.broadcasted_iota(jnp.int32, (C, 2 * LANES), 1)
    first_head2 = (lane2 % LANES) < HEAD_DIM
    r_l = lax.broadcasted_iota(jnp.int32, (LANES, LANES), 0)
    c_l = lax.broadcasted_iota(jnp.int32, (LANES, LANES), 1)
    same_head = (r_l // HEAD_DIM) == (c_l // HEAD_DIM)
    pair_ones = jnp.where(same_head, 1.0, 0.0).astype(BF16)

    def stack(x, mask):
        return jnp.concatenate([jnp.where(mask, x, 0.0), jnp.where(mask, 0.0, x)], axis=0)

    dot = functools.partial(jnp.dot, preferred_element_type=F32)

    def head_sumsq(y):
        y2 = (y * y).astype(BF16)
        return jnp.concatenate([dot(y2[:, h:h + GDN_HALF], head_ones) for h in range(0, W, GDN_HALF)],
                               axis=1)

    def solve_stages(slot):
        yq = yq_ref[...]
        yk = yk_ref[...]
        yv = yv_ref[...]
        qn = yq * lax.rsqrt(head_sumsq(yq) + EPS) * (HEAD_DIM ** -0.5)
        kn = yk * lax.rsqrt(head_sumsq(yk) + EPS)
        yield
        gbt = gb_ref[...]

        def spread(col0):
            pairs = []
            for p in range(npair):
                a = jnp.broadcast_to(gbt[:, col0 + 2 * p:col0 + 2 * p + 1], (TILE, LANES))
                b = jnp.broadcast_to(gbt[:, col0 + 2 * p + 1:col0 + 2 * p + 2], (TILE, LANES))
                pairs.append(jnp.where(lane_t < HEAD_DIM, a, b))
            return jnp.concatenate(pairs, axis=1)

        g = spread(0)
        beta = spread(GDN_HEADS)
        gcd = _dot_split_rhs(ltri_bd, jnp.concatenate([g, jnp.where(strict_t, g, 0.0)], axis=1), 2)
        yield
        gc = gcd[:, :W]
        decay = jnp.where(causal_t, jnp.exp(jnp.where(causal_t, gcd[:, W:], 0.0)), 0.0)
        egc = jnp.exp(gc)
        kb = kn * beta
        rv = yv * beta
        rk = kb * egc
        qd = qn * egc
        for cc in range(TILE // C):
            rs = slice(cc * C, (cc + 1) * C)
            g_last = gc[(cc + 1) * C - 1:(cc + 1) * C, :]
            kd_ref[slot, rs, :] = (kn[rs] * jnp.exp(g_last - gc[rs])).astype(BF16)
            gl_ref[slot, cc * SUBLANES:(cc + 1) * SUBLANES, :] = (
                jnp.broadcast_to(jnp.exp(g_last), (SUBLANES, W)))
        units = [(slice(cc * C, (cc + 1) * C), slice(LANES * p, LANES * (p + 1)), cc)
                 for cc in range(TILE // C) for p in range(npair)]
        kqs = [lax.dot_general(jnp.concatenate([kn[rs, ls], qn[rs, ls]], axis=0).astype(BF16),
                               stack(kn[rs, ls], first_head).astype(BF16), NT,
                               preferred_element_type=F32) for rs, ls, _ in units]
        yield
        ps = [-jnp.where(strict, kq[0:C] * beta[rs, ls] * decay[rs, ls], 0.0)
              for kq, (rs, ls, _) in zip(kqs, units)]
        ss = [eye2 + p for p in ps]
        ps = [dot(p.astype(BF16), stack(p, first_head).astype(BF16)) for p in ps]
        yield
        nround = int(math.log2(C))
        for k in range(1, nround):
            rhs = [stack(s_, first_head).astype(BF16) for s_ in ss]
            if k + 1 < nround:
                rhs = [jnp.concatenate([stack(p, first_head).astype(BF16), sx], axis=1)
                       for p, sx in zip(ps, rhs)]
            outs = [dot(p.astype(BF16), sx) for p, sx in zip(ps, rhs)]
            if k + 1 < nround:
                ps = [o[:, :LANES] for o in outs]
                ss = [s_ + o[:, LANES:] for s_, o in zip(ss, outs)]
            else:
                ss = [s_ + o for s_, o in zip(ss, outs)]
            yield
        xs = [dot(s_.astype(BF16),
                  stack(jnp.concatenate([rv[rs, ls], rk[rs, ls]], axis=1), first_head2).astype(BF16))
              for s_, (rs, ls, _) in zip(ss, units)]
        yield
        for x, kq, (rs, ls, cc) in zip(xs, kqs, units):
            u_ref[slot, rs, ls] = x[:, :LANES]
            wq_ref[slot, cc, 0:C, ls] = x[:, LANES:].astype(BF16)
            wq_ref[slot, cc, C:2 * C, ls] = qd[rs, ls].astype(BF16)
            a_ref[slot, rs, ls] = (kq[C:2 * C] * decay[rs, ls]).astype(BF16)

    lss = [slice(LANES * p, LANES * (p + 1)) for p in range(npair)]

    def recurrence_stages(slot, seq_start):
        states = [jnp.where(seq_start, 0.0, st_ref[p]) for p in range(npair)]
        pending = None

        def finish(rs, os_):
            for ls, o in zip(lss, os_):
                ms = dot((o * o).astype(BF16), pair_ones) * (1.0 / HEAD_DIM)
                out_ref[rs, ls] = o * lax.rsqrt(ms + EPS) * nw_ref[:, ls] * sz_ref[rs, ls]

        for cc in range(TILE // C):
            rs = slice(cc * C, (cc + 1) * C)
            wqs = [dot(wq_ref[slot, cc, :, ls], st.astype(BF16)) for ls, st in zip(lss, states)]
            if pending is not None:
                finish(*pending)
            yield
            v_news = [u_ref[slot, rs, ls] - wq[0:C] for ls, wq in zip(lss, wqs)]
            kvs = [lax.dot_general(kd_ref[slot, rs, ls], v.astype(BF16), TN, preferred_element_type=F32)
                   for ls, v in zip(lss, v_news)]
            os_ = [wq[C:2 * C] + dot(a_ref[slot, rs, ls], stack(v, first_head).astype(BF16))
                   for ls, wq, v in zip(lss, wqs, v_news)]
            states = [st * gl_ref[slot, cc * SUBLANES:cc * SUBLANES + 1, ls] + jnp.where(same_head, kv, 0.0)
                      for ls, st, kv in zip(lss, states, kvs)]
            pending = (rs, os_)
            yield
        finish(*pending)
        for p in range(npair):
            st_ref[p] = states[p]

    return solve_stages, recurrence_stages


def _interleave_weighted(*gens_and_weights):
    live = [[gen, weight] for gen, weight in gens_and_weights]
    while live:
        for entry in list(live):
            for _ in range(entry[1]):
                try:
                    next(entry[0])
                except StopIteration:
                    live.remove(entry)
                    break


def _mixers_kernel(relb_ref, qlo_ref, qhi_ref, kb_ref, ksum_ref, va_ref,
                   yq_ref, yk_ref, yv_ref, gb_ref, sz_ref, nw_ref,
                   oT_ref, og_ref,
                   bias_ref, addm_ref, qh_ref, lg_ref, moff_ref,
                   u_ref, wq_ref, a_ref, kd_ref, gl_ref, st_ref,
                   *, nblk, nbatch, tiles_per_seq, nsteps):
    step = pl.program_id(0)
    BLK = MOBA_BLOCK
    half = nblk // 2
    pair = jnp.minimum(step, nsteps - 2)
    hp = pair // (nbatch * (half // 2))
    b = (pair // (half // 2)) % nbatch
    scoring = step < nsteps - 1
    new_kv = (pair % (half // 2)) == 0

    solve_stages, recurrence_stages = _gdn_stages(
        yq_ref, yk_ref, yv_ref, gb_ref, sz_ref, nw_ref, og_ref,
        u_ref, wq_ref, a_ref, kd_ref, gl_ref, st_ref)
    seq_start = ((step - 1) % tiles_per_seq) == 0

    @pl.when((b == 0) & new_kv & scoring)
    def _():
        kk = lax.broadcasted_iota(jnp.int32, (BLK, BLK), 0)
        qq = lax.broadcasted_iota(jnp.int32, (BLK, BLK), 1)
        for hh in range(2):
            h = 2 * hp + hh
            for kind in range(2):
                d = qq - kk + kind * BLK
                val = jnp.full((BLK, BLK), relb_ref[h, REL_BUCKETS - 1], F32)
                for bkt in range(REL_BUCKETS - 2, -1, -1):
                    val = jnp.where(d < BUCKET_LOWER[bkt + 1], relb_ref[h, bkt], val)
                val = val * LOG2E
                if kind == 0:
                    val = jnp.where(d >= 0, val, NEG)
                bias_ref[hh, kind] = val

    def key_means():
        per_tile = INPROJ_TILE // BLK
        ks = ksum_ref[...]
        km = jnp.concatenate([ks[j // per_tile, j % per_tile:j % per_tile + 1, :] for j in range(nblk)],
                             axis=0) * (1.0 / BLK)
        lane = lax.broadcasted_iota(jnp.int32, (nblk, LANES), 1)
        return jnp.concatenate([jnp.where(lane < HEAD_DIM, km, 0.0),
                                jnp.where(lane >= HEAD_DIM, km, 0.0)], axis=0)

    def item_tiles(t):
        i_hi = nblk - 1 - t
        tiles = [(0, t, "own"), (1, i_hi, "own"), (1, i_hi - 1, "prev")]
        if t >= 1:
            tiles.append((0, t - 1, "prev"))
        tiles += [(0, j, "far") for j in range(t - 1)]
        tiles += [(1, j, "far") for j in range(i_hi - 1)]
        assert len(tiles) == nblk + 1
        return tiles

    def score_stages(parity):
        ridx = lax.broadcasted_iota(jnp.int32, (nblk, BLK), 0)
        sub = lax.broadcasted_iota(jnp.int32, (LANES, BLK), 0)
        scale = HEAD_DIM ** -0.5 * LOG2E
        km = key_means()
        for e in range(2):
            t = 2 * parity + e
            slot = 2 * parity + e
            q_of = ((qlo_ref, e, t), (qhi_ref, 1 - e, nblk - 1 - t))
            for s, (q_ref, w, qi) in enumerate(q_of):
                qT = q_ref[0, w]
                gT = jnp.dot(km, qT, precision=HI, preferred_element_type=F32)
                past = ridx < qi
                for hh in range(2):
                    gm = jnp.where(past, gT[nblk * hh:nblk * (hh + 1)], -jnp.inf)
                    cnt = jnp.zeros((nblk, BLK), F32)
                    for jp in range(nblk):
                        row = gm[jp:jp + 1, :]
                        beats = (row > gm) | ((row == gm) & (ridx > jp))
                        cnt = cnt + jnp.where(beats, 1.0, 0.0)
                    visible = past & (cnt < MOBA_TOPK)
                    addm_ref[e, s, nblk * hh:nblk * (hh + 1), :] = jnp.where(visible, 0.0, NEG)
                    in_head = (sub >= HEAD_DIM * hh) & (sub < HEAD_DIM * (hh + 1))
                    qh_ref[e, s, hh] = jnp.where(in_head, qT * scale, 0.0).astype(BF16)
            yield
            tiles = item_tiles(t)
            for hh in range(2):
                cmax = {0: [], 1: []}
                offs = []
                for n, (s, kblk, cls) in enumerate(tiles):
                    lg = jnp.dot(kb_ref[0, kblk * BLK:(kblk + 1) * BLK, :], qh_ref[e, s, hh],
                                 preferred_element_type=F32)
                    if cls != "far":
                        lg = lg + bias_ref[hh, 0 if cls == "own" else 1]
                    lg_ref[slot, hh, n] = lg
                    cm = jnp.max(lg, axis=0, keepdims=True)
                    off = None
                    if cls != "own":
                        off = addm_ref[e, s, nblk * hh + kblk:nblk * hh + kblk + 1, :]
                        if cls == "far":
                            off = off + relb_ref[2 * hp + hh, REL_BUCKETS - 1] * LOG2E
                        cm = cm + off
                    cmax[s].append(cm)
                    offs.append(off)
                    yield
                m = {s: functools.reduce(jnp.maximum, cmax[s]) for s in (0, 1)}
                for n, (s, _, _) in enumerate(tiles):
                    moff_ref[slot, hh, n:n + 1, :] = m[s] if offs[n] is None else m[s] - offs[n]

    def softmax_pv_stages(parity):
        for e in range(2):
            t = 2 * parity + e
            slot = 2 * parity + e
            tiles = item_tiles(t)
            for hh in range(2):
                acc = {0: None, 1: None}
                for n, (s, kblk, _) in enumerate(tiles):
                    p = jnp.exp2(lg_ref[slot, hh, n] - moff_ref[slot, hh, n:n + 1, :])
                    pvn = jnp.dot(va_ref[0, kblk, hh], p.astype(BF16),
                                  preferred_element_type=F32)
                    acc[s] = pvn if acc[s] is None else acc[s] + pvn
                    yield
                for s in (0, 1):
                    oT_ref[0, 2 * e + s, HEAD_DIM * hh:HEAD_DIM * (hh + 1), :] = (
                        acc[s][0:HEAD_DIM] / acc[s][HEAD_DIM:HEAD_DIM + 1])

    ATT, GDN = 3, 1

    @pl.when(step == 0)
    def _():
        st_ref[...] = jnp.zeros(st_ref.shape, F32)
        _interleave_weighted((score_stages(0), ATT), (solve_stages(0), GDN))

    for parity in range(2):
        @pl.when((step > 0) & (step < nsteps - 1) & (step % 2 == parity))
        def _(parity=parity):
            _interleave_weighted((recurrence_stages(1 - parity, seq_start), GDN),
                                 (softmax_pv_stages(1 - parity), ATT),
                                 (solve_stages(parity), GDN),
                                 (score_stages(parity), ATT))

    @pl.when(step == nsteps - 1)
    def _():
        last = (nsteps - 2) % 2
        _interleave_weighted((recurrence_stages(last, seq_start), GDN), (softmax_pv_stages(last), ATT))


def _token_mixers(rel_bias, qT, k3, ksum, va, gqkv, sz, gb, nw_row, B, S):
    T = B * S
    W = GDN_WIDTH
    TILE = GDN_TILE
    nblk = S // MOBA_BLOCK
    half = nblk // 2
    assert TILE == GDN_HALF and S % TILE == 0
    assert BUCKET_LOWER[REL_BUCKETS - 1] <= MOBA_BLOCK + 1
    assert nblk == 8 and nblk + 1 <= 2 * SUBLANES
    nchunk = TILE // GDN_CHUNK
    ntiles = T // TILE
    npairs = (ATT_HEADS // 2) * B * (half // 2)
    assert npairs == ntiles
    nsteps = ntiles + 1

    def scored(s):
        p = jnp.minimum(s, npairs - 1)
        return p // (B * (half // 2)), (p // (half // 2)) % B, p % (half // 2)

    def done(s):
        p = jnp.maximum(s - 1, 0)
        return p // (B * (half // 2)), (p // (half // 2)) % B, p % (half // 2)

    def q_lo(s):
        hp, b, m = scored(s)
        return (b, m, hp, 0)

    def q_hi(s):
        hp, b, m = scored(s)
        return (b, half - 1 - m, hp, 0)

    def k_blk(s):
        hp, b, _ = scored(s)
        return (b, 0, hp)

    def v_blk(s):
        hp, b, _ = done(s)
        return (b, 0, hp, 0, 0)

    def o_blk(s):
        hp, b, m = done(s)
        return (b, m, hp, 0)

    cur_tile = lambda off: (lambda s: (jnp.minimum(s, ntiles - 1), off))
    prev_tile = lambda s: (jnp.maximum(s - 1, 0), 0)
    return pl.pallas_call(
        functools.partial(_mixers_kernel, nblk=nblk, nbatch=B, tiles_per_seq=S // TILE, nsteps=nsteps),
        grid=(nsteps,),
        in_specs=[
            pl.BlockSpec(memory_space=pltpu.SMEM),
            pl.BlockSpec((1, 2, LANES, MOBA_BLOCK), q_lo),
            pl.BlockSpec((1, 2, LANES, MOBA_BLOCK), q_hi),
            pl.BlockSpec((1, S, LANES), k_blk),
            pl.BlockSpec((S // INPROJ_TILE, SUBLANES, LANES), k_blk),
            pl.BlockSpec((1, nblk, 2, V_ROWS, MOBA_BLOCK), v_blk),
            pl.BlockSpec((TILE, W), cur_tile(0)),
            pl.BlockSpec((TILE, W), cur_tile(1)),
            pl.BlockSpec((TILE, W), cur_tile(2)),
            pl.BlockSpec((TILE, LANES), cur_tile(0)),
            pl.BlockSpec((TILE, W), prev_tile),
            pl.BlockSpec((1, W), lambda s: (0, 0)),
        ],
        out_specs=[
            pl.BlockSpec((1, 4, LANES, MOBA_BLOCK), o_blk),
            pl.BlockSpec((TILE, W), prev_tile),
        ],
        out_shape=[
            jax.ShapeDtypeStruct((B, nblk, ATT_WIDTH, MOBA_BLOCK), F32),
            jax.ShapeDtypeStruct((T, GDN_WIDTH), F32),
        ],
        scratch_shapes=[
            pltpu.VMEM((2, 2, MOBA_BLOCK, MOBA_BLOCK), F32),
            pltpu.VMEM((2, 2, 2 * nblk, MOBA_BLOCK), F32),
            pltpu.VMEM((2, 2, 2, LANES, MOBA_BLOCK), BF16),
            pltpu.VMEM((4, 2, nblk + 1, MOBA_BLOCK, MOBA_BLOCK), F32),
            pltpu.VMEM((4, 2, 2 * SUBLANES, MOBA_BLOCK), F32),
            pltpu.VMEM((2, TILE, W), F32),
            pltpu.VMEM((2, nchunk, 2 * GDN_CHUNK, W), BF16),
            pltpu.VMEM((2, TILE, W), BF16),
            pltpu.VMEM((2, TILE, W), BF16),
            pltpu.VMEM((2, nchunk * SUBLANES, W), F32),
            pltpu.VMEM((W // LANES, LANES, LANES), F32),
        ],
        compiler_params=pltpu.CompilerParams(
            dimension_semantics=("arbitrary",), vmem_limit_bytes=VMEM_LIMIT),
        name="token_mixers",
    )(rel_bias, qT, qT, k3, ksum, va, gqkv, gqkv, gqkv, gb, sz, nw_row)


def _out_mlp_kernel(x_ref, oTa_ref, oTb_ref, og_ref, woa_ref, wog_ref, pmn_ref, pre_ref, post_ref,
                    wup_ref, wdn_ref, out_ref):
    oT = jnp.concatenate([oTa_ref[0, 0], oTb_ref[0, 0]], axis=1)
    o_att = oT.T.astype(BF16)
    mix = jnp.dot(o_att, woa_ref[...], preferred_element_type=F32)
    mix = mix + jnp.dot(og_ref[...].astype(BF16), wog_ref[...], preferred_element_type=F32)
    x1 = x_ref[...] + _rms(mix, pmn_ref[...])
    h = _rms(x1, pre_ref[...]).astype(BF16)
    acc = jnp.zeros((ROW_TILE, D_MODEL), F32)
    for c in range(D_FF // FF_TILE):
        up = jnp.dot(h, wup_ref[:, c * FF_TILE:(c + 1) * FF_TILE], preferred_element_type=F32)
        act = jnp.square(jnp.maximum(up, 0.0)).astype(BF16)
        acc = acc + jnp.dot(act, wdn_ref[c * FF_TILE:(c + 1) * FF_TILE, :], preferred_element_type=F32)
    out_ref[...] = x1 + _rms(acc, post_ref[...])


def _out_mlp(xf, oT, og, woa, wog, pmn, pre, post, wup, wdn, B, S):
    T = B * S
    nblk = S // MOBA_BLOCK
    tiles_per_seq = S // ROW_TILE
    assert ROW_TILE == 2 * MOBA_BLOCK
    const = lambda i: (0, 0)
    row = lambda i: (i, 0)

    def att_block(which):
        def index(i):
            blk = 2 * (i % tiles_per_seq) + which
            return (i // tiles_per_seq, _paired_pos(blk, nblk), 0, 0)
        return index

    single = dict(pipeline_mode=pl.Buffered(1))
    return pl.pallas_call(
        _out_mlp_kernel,
        grid=(T // ROW_TILE,),
        in_specs=[
            pl.BlockSpec((ROW_TILE, D_MODEL), row),
            pl.BlockSpec((1, 1, ATT_WIDTH, MOBA_BLOCK), att_block(0)),
            pl.BlockSpec((1, 1, ATT_WIDTH, MOBA_BLOCK), att_block(1)),
            pl.BlockSpec((ROW_TILE, GDN_WIDTH), row),
            pl.BlockSpec(woa.shape, const, **single),
            pl.BlockSpec(wog.shape, const, **single),
            pl.BlockSpec((1, D_MODEL), const),
            pl.BlockSpec((1, D_MODEL), const),
            pl.BlockSpec((1, D_MODEL), const),
            pl.BlockSpec(wup.shape, const, **single),
            pl.BlockSpec(wdn.shape, const, **single),
        ],
        out_specs=pl.BlockSpec((ROW_TILE, D_MODEL), row),
        out_shape=jax.ShapeDtypeStruct((T, D_MODEL), F32),
        compiler_params=pltpu.CompilerParams(
            dimension_semantics=("arbitrary",), vmem_limit_bytes=VMEM_LIMIT),
        name="out_mlp",
    )(xf, oT, oT, og, woa, wog, pmn, pre, post, wup, wdn)


def kernel(x, w_in, w_out, conv_w, A_log, dt_bias, gdn_norm_w, rel_bias, pre_mix_norm,
           post_mix_norm, pre_mlp_norm, post_mlp_norm, w_up, w_down):
    B, S, D = x.shape
    assert D == D_MODEL and S % ROW_TILE == 0 and S % MOBA_BLOCK == 0
    T = B * S
    depth = w_in.shape[0]
    xf = x.reshape(T, D)
    o0, o1, o2, o3, o4 = 0, ATT_WIDTH, 2 * ATT_WIDTH, 3 * ATT_WIDTH, 3 * ATT_WIDTH + 3 * GDN_WIDTH
    o5 = o4 + GDN_WIDTH
    for l in range(depth):
        wi = w_in[l]
        wqT = wi[:, o0:o1].T.astype(BF16)
        wk = wi[:, o1:o2].astype(BF16)
        wvT = wi[:, o2:o3].T.astype(BF16)
        wg = wi[:, o3:o4].astype(BF16)
        wz = wi[:, o4:o5].astype(BF16)
        wab = jnp.pad(wi[:, o5:], ((0, 0), (0, LANES - 2 * GDN_HEADS))).astype(BF16)
        pad8 = lambda v: jnp.pad(v.astype(F32), (0, LANES - GDN_HEADS))[None, :]
        qT, k, ksum, va, gqkv, sz, gb = _inproj(xf, pre_mix_norm[l][None, :], wqT, wk, wvT, wg, wz, wab,
                                                conv_w[l], pad8(A_log[l]), pad8(dt_bias[l]), B, S)
        oT, og = _token_mixers(rel_bias.astype(F32), qT, k.reshape(B, S, ATT_WIDTH), ksum, va, gqkv, sz,
                               gb, jnp.tile(gdn_norm_w[l], GDN_HEADS)[None, :], B, S)
        wo = w_out[l].astype(BF16)
        xf = _out_mlp(xf, oT, og, wo[:ATT_WIDTH], wo[ATT_WIDTH:], post_mix_norm[l][None, :],
                      pre_mlp_norm[l][None, :], post_mlp_norm[l][None, :],
                      w_up[l].astype(BF16), w_down[l].astype(BF16), B, S)
    return xf.reshape(B, S, D)
```

```python
import functools
import math

import jax
import jax.numpy as jnp
from jax import lax
from jax.experimental import pallas as pl
from jax.experimental.pallas import tpu as pltpu

F32 = jnp.float32
BF16 = jnp.bfloat16
HI = lax.Precision.HIGHEST

D_MODEL = 1024
HEAD_DIM = 64
ATT_HEADS = 8
GDN_HEADS = 8
ATT_WIDTH = ATT_HEADS * HEAD_DIM
GDN_WIDTH = GDN_HEADS * HEAD_DIM
MOBA_BLOCK = 256
MOBA_TOPK = 3
GDN_CHUNK = 64
CONV_WIDTH = 4
D_FF = 4 * D_MODEL
REL_BUCKETS = 32
REL_MAX_EXACT = 16
REL_MAX_DIST = 128
EPS = 1e-6
NEG = -1e30
LOG2E = math.log2(math.e)

LANES = 128
SUBLANES = 8
VMEM_LIMIT = 56 * 1024 * 1024
ROW_TILE = 512
INPROJ_TILE = 512
FF_TILE = 1024

NT = (((1,), (1,)), ((), ()))
TN = (((0,), (0,)), ((), ()))


def _bucket_lower_bounds():
    def bucket(d):
        if d < REL_MAX_EXACT:
            return d
        t = math.log(d / REL_MAX_EXACT) / math.log(REL_MAX_DIST / REL_MAX_EXACT)
        t = t * (REL_BUCKETS - REL_MAX_EXACT)
        assert d in (REL_MAX_EXACT, REL_MAX_DIST) or abs(t - round(t)) > 1e-6
        return min(REL_MAX_EXACT + int(t + 1e-9), REL_BUCKETS - 1)
    lower = []
    for b in range(REL_BUCKETS):
        d = 0
        while bucket(d) < b:
            d += 1
        lower.append(d)
    return lower


BUCKET_LOWER = _bucket_lower_bounds()


def _sigmoid(x):
    return 0.5 * jnp.tanh(0.5 * x) + 0.5


def _silu_of_half(h):
    return h + h * jnp.tanh(h)


def _rms(x, w):
    return x * lax.rsqrt(jnp.mean(x * x, axis=-1, keepdims=True) + EPS) * w


def _split_bf16(x, parts):
    out = []
    for _ in range(parts):
        h = x.astype(BF16)
        out.append(h)
        x = x - h.astype(F32)
    return out


def _dot_split_rhs(c, x, parts):
    acc = None
    for h in _split_bf16(x, parts):
        d = jnp.dot(c, h, preferred_element_type=F32)
        acc = d if acc is None else acc + d
    return acc


CONV_COLS = 512


def _inproj_kernel(x_ref, xp_ref, nw_ref, wqT_ref, wk_ref, wvT_ref, wg_ref, wz_ref, wab_ref,
                   cw_ref, alog_ref, dtb_ref,
                   qT_ref, k_ref, ksum_ref, va_ref, g_ref, z_ref, gb_ref, *, tiles_per_seq):
    h = _rms(x_ref[...], nw_ref[...]).astype(BF16)

    hp = _rms(xp_ref[...], nw_ref[...]).astype(BF16)
    seq_start = (pl.program_id(0) % tiles_per_seq) == 0
    trow8 = lax.broadcasted_iota(jnp.int32, (SUBLANES, CONV_COLS), 0)
    for c in range(3 * GDN_WIDTH // CONV_COLS):
        cols = slice(c * CONV_COLS, (c + 1) * CONV_COLS)
        cur = jnp.dot(h, wg_ref[:, cols], preferred_element_type=F32)
        prev8 = jnp.dot(hp, wg_ref[:, cols], preferred_element_type=F32)
        prev8 = jnp.where(seq_start, 0.0, prev8)
        cw_half = 0.5 * cw_ref[:, cols]
        acc = cur * cw_half[CONV_WIDTH - 1:CONV_WIDTH]
        for s in range(1, CONV_WIDTH):
            rolled = pltpu.roll(cur, s, 0)
            top = jnp.where(trow8 < s, pltpu.roll(prev8, s, 0), rolled[0:SUBLANES])
            tap = jnp.concatenate([top, rolled[SUBLANES:]], axis=0)
            acc = acc + tap * cw_half[CONV_WIDTH - 1 - s:CONV_WIDTH - s]
        g_ref[:, cols] = _silu_of_half(acc)

    z = jnp.dot(h, wz_ref[...], preferred_element_type=F32)
    z_ref[...] = _silu_of_half(0.5 * z)
    ab = jnp.dot(h, wab_ref[...], preferred_element_type=F32)
    xs = ab + dtb_ref[...]
    log_decay = -jnp.exp(alog_ref[...]) * (jnp.maximum(xs, 0.0) + jnp.log1p(jnp.exp(-jnp.abs(xs))))
    lane = lax.broadcasted_iota(jnp.int32, ab.shape, 1)
    gb_ref[...] = jnp.where(lane < GDN_HEADS, log_decay, _sigmoid(ab))

    qT = lax.dot_general(wqT_ref[...], h, NT, preferred_element_type=F32)
    vT = lax.dot_general(wvT_ref[...], h, NT, preferred_element_type=F32)
    k = jnp.dot(h, wk_ref[...], preferred_element_type=F32)
    k_ref[...] = k.astype(BF16)
    ones_row = jnp.where(lax.broadcasted_iota(jnp.int32, (V_ROWS - HEAD_DIM, MOBA_BLOCK), 0) == 0,
                         1.0, 0.0).astype(BF16)
    ksum_ref[...] = jnp.zeros(ksum_ref.shape, F32)
    for t in range(INPROJ_TILE // MOBA_BLOCK):
        blk = slice(t * MOBA_BLOCK, (t + 1) * MOBA_BLOCK)
        qT_ref[0, t] = qT[:, blk]
        ksum_ref[0, t:t + 1, :] = jnp.sum(k[blk], axis=0, keepdims=True)
        for hh in range(ATT_HEADS):
            va_ref[0, t, hh, 0:HEAD_DIM, :] = vT[HEAD_DIM * hh:HEAD_DIM * (hh + 1), blk].astype(BF16)
            va_ref[0, t, hh, HEAD_DIM:V_ROWS, :] = ones_row


def _inproj(xf, nw, wqT, wk, wvT, wg, wz, wab, conv_w, alog_pad, dtb_pad, B, S):
    T = B * S
    TM = INPROJ_TILE
    assert S % TM == 0
    nblk = S // MOBA_BLOCK
    tiles_per_seq = S // TM
    blk_per_tile = TM // MOBA_BLOCK
    const = lambda i: (0, 0)
    row = lambda i: (i, 0)
    tr = lambda i: (i // tiles_per_seq, i % tiles_per_seq, 0, 0)
    prev_rows = lambda i: (jnp.maximum(i * (TM // SUBLANES) - 1, 0), 0)
    single = dict(pipeline_mode=pl.Buffered(1))
    return pl.pallas_call(
        functools.partial(_inproj_kernel, tiles_per_seq=tiles_per_seq),
        grid=(T // TM,),
        in_specs=[
            pl.BlockSpec((TM, D_MODEL), row),
            pl.BlockSpec((SUBLANES, D_MODEL), prev_rows),
            pl.BlockSpec((1, D_MODEL), const),
            pl.BlockSpec(wqT.shape, const, **single),
            pl.BlockSpec(wk.shape, const, **single),
            pl.BlockSpec(wvT.shape, const, **single),
            pl.BlockSpec(wg.shape, const, **single),
            pl.BlockSpec(wz.shape, const, **single),
            pl.BlockSpec(wab.shape, const, **single),
            pl.BlockSpec(conv_w.shape, const),
            pl.BlockSpec((1, LANES), const),
            pl.BlockSpec((1, LANES), const),
        ],
        out_specs=[
            pl.BlockSpec((1, blk_per_tile, ATT_WIDTH, MOBA_BLOCK), tr),
            pl.BlockSpec((TM, ATT_WIDTH), row),
            pl.BlockSpec((1, SUBLANES, ATT_WIDTH), lambda i: (i, 0, 0)),
            pl.BlockSpec((1, blk_per_tile, ATT_HEADS, V_ROWS, MOBA_BLOCK),
                         lambda i: (i // tiles_per_seq, i % tiles_per_seq, 0, 0, 0)),
            pl.BlockSpec((TM, 3 * GDN_WIDTH), row),
            pl.BlockSpec((TM, GDN_WIDTH), row),
            pl.BlockSpec((TM, LANES), row),
        ],
        out_shape=[
            jax.ShapeDtypeStruct((B, nblk, ATT_WIDTH, MOBA_BLOCK), F32),
            jax.ShapeDtypeStruct((T, ATT_WIDTH), BF16),
            jax.ShapeDtypeStruct((T // TM, SUBLANES, ATT_WIDTH), F32),
            jax.ShapeDtypeStruct((B, nblk, ATT_HEADS, V_ROWS, MOBA_BLOCK), BF16),
            jax.ShapeDtypeStruct((T, 3 * GDN_WIDTH), F32),
            jax.ShapeDtypeStruct((T, GDN_WIDTH), F32),
            jax.ShapeDtypeStruct((T, LANES), F32),
        ],
        compiler_params=pltpu.CompilerParams(
            dimension_semantics=("arbitrary",), vmem_limit_bytes=VMEM_LIMIT),
        name="inproj",
    )(xf, xf, nw, wqT, wk, wvT, wg, wz, wab, conv_w, alog_pad, dtb_pad)


V_ROWS = HEAD_DIM + 16


def _paired_pos(i, nblk):
    return jnp.where(i < nblk // 2, 2 * i, 2 * (nblk - 1 - i) + 1)


GDN_TILE = 256
GDN_HALF = 2 * LANES


def _gdn_stages(yq_ref, yk_ref, yv_ref, gb_ref, sz_ref, nw_ref, out_ref,
                u_ref, wq_ref, a_ref, kd_ref, gl_ref, st_ref):
    C = GDN_CHUNK
    W = GDN_WIDTH
    TILE = GDN_TILE
    npair = W // LANES

    r_w = lax.broadcasted_iota(jnp.int32, (GDN_HALF, GDN_HALF), 0)
    c_w = lax.broadcasted_iota(jnp.int32, (GDN_HALF, GDN_HALF), 1)
    head_ones = jnp.where((r_w // HEAD_DIM) == (c_w // HEAD_DIM), 1.0, 0.0).astype(BF16)
    ltri_bd = jnp.where(((r_w // C) == (c_w // C)) & (c_w <= r_w), 1.0, 0.0).astype(BF16)
    tok = lax.broadcasted_iota(jnp.int32, (TILE, W), 0) % C
    col = lax.broadcasted_iota(jnp.int32, (TILE, W), 1) % HEAD_DIM
    causal_t = tok >= col
    strict_t = tok > col
    lane_t = lax.broadcasted_iota(jnp.int32, (TILE, LANES), 1)

    lane = lax.broadcasted_iota(jnp.int32, (C, LANES), 1)
    rowi = lax.broadcasted_iota(jnp.int32, (C, LANES), 0)
    first_head = lane < HEAD_DIM
    strict = rowi > (lane % HEAD_DIM)
    eye2 = jnp.where(rowi == (lane % HEAD_DIM), 1.0, 0.0)
    lane2 = lax.broadcasted_iota(jnp.int32, (C, 2 * LANES), 1)
    first_head2 = (lane2 % LANES) < HEAD_DIM
    r_l = lax.broadcasted_iota(jnp.int32, (LANES, LANES), 0)
    c_l = lax.broadcasted_iota(jnp.int32, (LANES, LANES), 1)
    same_head = (r_l // HEAD_DIM) == (c_l // HEAD_DIM)
    pair_ones = jnp.where(same_head, 1.0, 0.0).astype(BF16)

    def stack(x, mask):
        return jnp.concatenate([jnp.where(mask, x, 0.0), jnp.where(mask, 0.0, x)], axis=0)

    dot = functools.partial(jnp.dot, preferred_element_type=F32)

    def head_sumsq(y):
        y2 = (y * y).astype(BF16)
        return jnp.concatenate([dot(y2[:, h:h + GDN_HALF], head_ones) for h in range(0, W, GDN_HALF)],
                               axis=1)

    def solve_stages(slot):
        yq = yq_ref[...]
        yk = yk_ref[...]
        yv = yv_ref[...]
        qn = yq * lax.rsqrt(head_sumsq(yq) + EPS) * (HEAD_DIM ** -0.5)
        kn = yk * lax.rsqrt(head_sumsq(yk) + EPS)
        yield
        gbt = gb_ref[...]

        def spread(col0):
            pairs = []
            for p in range(npair):
                a = jnp.broadcast_to(gbt[:, col0 + 2 * p:col0 + 2 * p + 1], (TILE, LANES))
                b = jnp.broadcast_to(gbt[:, col0 + 2 * p + 1:col0 + 2 * p + 2], (TILE, LANES))
                pairs.append(jnp.where(lane_t < HEAD_DIM, a, b))
            return jnp.concatenate(pairs, axis=1)

        g = spread(0)
        beta = spread(GDN_HEADS)
        gcd = _dot_split_rhs(ltri_bd, jnp.concatenate([g, jnp.where(strict_t, g, 0.0)], axis=1), 2)
        yield
        gc = gcd[:, :W]
        decay = jnp.where(causal_t, jnp.exp(jnp.where(causal_t, gcd[:, W:], 0.0)), 0.0)
        egc = jnp.exp(gc)
        kb = kn * beta
        rv = yv * beta
        rk = kb * egc
        qd = qn * egc
        for cc in range(TILE // C):
            rs = slice(cc * C, (cc + 1) * C)
            g_last = gc[(cc + 1) * C - 1:(cc + 1) * C, :]
            kd_ref[slot, rs, :] = (kn[rs] * jnp.exp(g_last - gc[rs])).astype(BF16)
            gl_ref[slot, cc * SUBLANES:(cc + 1) * SUBLANES, :] = (
                jnp.broadcast_to(jnp.exp(g_last), (SUBLANES, W)))
        units = [(slice(cc * C, (cc + 1) * C), slice(LANES * p, LANES * (p + 1)), cc)
                 for cc in range(TILE // C) for p in range(npair)]
        kqs = [lax.dot_general(jnp.concatenate([kn[rs, ls], qn[rs, ls]], axis=0).astype(BF16),
                               stack(kn[rs, ls], first_head).astype(BF16), NT,
                               preferred_element_type=F32) for rs, ls, _ in units]
        yield
        ps = [-jnp.where(strict, kq[0:C] * beta[rs, ls] * decay[rs, ls], 0.0)
              for kq, (rs, ls, _) in zip(kqs, units)]
        ss = [eye2 + p for p in ps]
        ps = [dot(p.astype(BF16), stack(p, first_head).astype(BF16)) for p in ps]
        yield
        nround = int(math.log2(C))
        for k in range(1, nround):
            rhs = [stack(s_, first_head).astype(BF16) for s_ in ss]
            if k + 1 < nround:
                rhs = [jnp.concatenate([stack(p, first_head).astype(BF16), sx], axis=1)
                       for p, sx in zip(ps, rhs)]
            outs = [dot(p.astype(BF16), sx) for p, sx in zip(ps, rhs)]
            if k + 1 < nround:
                ps = [o[:, :LANES] for o in outs]
                ss = [s_ + o[:, LANES:] for s_, o in zip(ss, outs)]
            else:
                ss = [s_ + o for s_, o in zip(ss, outs)]
            yield
        xs = [dot(s_.astype(BF16),
                  stack(jnp.concatenate([rv[rs, ls], rk[rs, ls]], axis=1), first_head2).astype(BF16))
              for s_, (rs, ls, _) in zip(ss, units)]
        yield
        for x, kq, (rs, ls, cc) in zip(xs, kqs, units):
            u_ref[slot, rs, ls] = x[:, :LANES]
            wq_ref[slot, cc, 0:C, ls] = x[:, LANES:].astype(BF16)
            wq_ref[slot, cc, C:2 * C, ls] = qd[rs, ls].astype(BF16)
            a_ref[slot, rs, ls] = (kq[C:2 * C] * decay[rs, ls]).astype(BF16)

    lss = [slice(LANES * p, LANES * (p + 1)) for p in range(npair)]

    def recurrence_stages(slot, seq_start):
        states = [jnp.where(seq_start, 0.0, st_ref[p]) for p in range(npair)]
        pending = None

        def finish(rs, os_):
            for ls, o in zip(lss, os_):
                ms = dot((o * o).astype(BF16), pair_ones) * (1.0 / HEAD_DIM)
                out_ref[rs, ls] = o * lax.rsqrt(ms + EPS) * nw_ref[:, ls] * sz_ref[rs, ls]

        for cc in range(TILE // C):
            rs = slice(cc * C, (cc + 1) * C)
            wqs = [dot(wq_ref[slot, cc, :, ls], st.astype(BF16)) for ls, st in zip(lss, states)]
            if pending is not None:
                finish(*pending)
            yield
            v_news = [u_ref[slot, rs, ls] - wq[0:C] for ls, wq in zip(lss, wqs)]
            kvs = [lax.dot_general(kd_ref[slot, rs, ls], v.astype(BF16), TN, preferred_element_type=F32)
                   for ls, v in zip(lss, v_news)]
            os_ = [wq[C:2 * C] + dot(a_ref[slot, rs, ls], stack(v, first_head).astype(BF16))
                   for ls, wq, v in zip(lss, wqs, v_news)]
            states = [st * gl_ref[slot, cc * SUBLANES:cc * SUBLANES + 1, ls] + jnp.where(same_head, kv, 0.0)
                      for ls, st, kv in zip(lss, states, kvs)]
            pending = (rs, os_)
            yield
        finish(*pending)
        for p in range(npair):
            st_ref[p] = states[p]

    return solve_stages, recurrence_stages


def _interleave_weighted(*gens_and_weights):
    live = [[gen, weight] for gen, weight in gens_and_weights]
    while live:
        for entry in list(live):
            for _ in range(entry[1]):
                try:
                    next(entry[0])
                except StopIteration:
                    live.remove(entry)
                    break


def _mixers_kernel(relb_ref, qlo_ref, qhi_ref, kb_ref, ksum_ref, va_ref,
                   yq_ref, yk_ref, yv_ref, gb_ref, sz_ref, nw_ref,
                   oT_ref, og_ref,
                   bias_ref, addm_ref, qh_ref, lg_ref, moff_ref,
                   u_ref, wq_ref, a_ref, kd_ref, gl_ref, st_ref,
                   *, nblk, nbatch, tiles_per_seq, nsteps):
    step = pl.program_id(0)
    BLK = MOBA_BLOCK
    half = nblk // 2
    pair = jnp.minimum(step, nsteps - 2)
    hp = pair // (nbatch * (half // 2))
    b = (pair // (half // 2)) % nbatch
    scoring = step < nsteps - 1
    new_kv = (pair % (half // 2)) == 0

    solve_stages, recurrence_stages = _gdn_stages(
        yq_ref, yk_ref, yv_ref, gb_ref, sz_ref, nw_ref, og_ref,
        u_ref, wq_ref, a_ref, kd_ref, gl_ref, st_ref)
    seq_start = ((step - 1) % tiles_per_seq) == 0

    @pl.when((b == 0) & new_kv & scoring)
    def _():
        kk = lax.broadcasted_iota(jnp.int32, (BLK, BLK), 0)
        qq = lax.broadcasted_iota(jnp.int32, (BLK, BLK), 1)
        for hh in range(2):
            h = 2 * hp + hh
            for kind in range(2):
                d = qq - kk + kind * BLK
                val = jnp.full((BLK, BLK), relb_ref[h, REL_BUCKETS - 1], F32)
                for bkt in range(REL_BUCKETS - 2, -1, -1):
                    val = jnp.where(d < BUCKET_LOWER[bkt + 1], relb_ref[h, bkt], val)
                val = val * LOG2E
                if kind == 0:
                    val = jnp.where(d >= 0, val, NEG)
                bias_ref[hh, kind] = val

    def key_means():
        per_tile = INPROJ_TILE // BLK
        ks = ksum_ref[...]
        km = jnp.concatenate([ks[j // per_tile, j % per_tile:j % per_tile + 1, :] for j in range(nblk)],
                             axis=0) * (1.0 / BLK)
        lane = lax.broadcasted_iota(jnp.int32, (nblk, LANES), 1)
        return jnp.concatenate([jnp.where(lane < HEAD_DIM, km, 0.0),
                                jnp.where(lane >= HEAD_DIM, km, 0.0)], axis=0)

    def item_tiles(t):
        i_hi = nblk - 1 - t
        tiles = [(0, t, "own"), (1, i_hi, "own"), (1, i_hi - 1, "prev")]
        if t >= 1:
            tiles.append((0, t - 1, "prev"))
        tiles += [(0, j, "far") for j in range(t - 1)]
        tiles += [(1, j, "far") for j in range(i_hi - 1)]
        assert len(tiles) == nblk + 1
        return tiles

    def score_stages(parity):
        ridx = lax.broadcasted_iota(jnp.int32, (nblk, BLK), 0)
        sub = lax.broadcasted_iota(jnp.int32, (LANES, BLK), 0)
        scale = HEAD_DIM ** -0.5 * LOG2E
        km = key_means()
        for e in range(2):
            t = 2 * parity + e
            slot = 2 * parity + e
            q_of = ((qlo_ref, e, t), (qhi_ref, 1 - e, nblk - 1 - t))
            for s, (q_ref, w, qi) in enumerate(q_of):
                qT = q_ref[0, w]
                gT = jnp.dot(km, qT, precision=HI, preferred_element_type=F32)
                past = ridx < qi
                for hh in range(2):
                    gm = jnp.where(past, gT[nblk * hh:nblk * (hh + 1)], -jnp.inf)
                    cnt = jnp.zeros((nblk, BLK), F32)
                    for jp in range(nblk):
                        row = gm[jp:jp + 1, :]
                        beats = (row > gm) | ((row == gm) & (ridx > jp))
                        cnt = cnt + jnp.where(beats, 1.0, 0.0)
                    visible = past & (cnt < MOBA_TOPK)
                    addm_ref[e, s, nblk * hh:nblk * (hh + 1), :] = jnp.where(visible, 0.0, NEG)
                    in_head = (sub >= HEAD_DIM * hh) & (sub < HEAD_DIM * (hh + 1))
                    qh_ref[e, s, hh] = jnp.where(in_head, qT * scale, 0.0).astype(BF16)
            yield
            tiles = item_tiles(t)
            for hh in range(2):
                cmax = {0: [], 1: []}
                offs = []
                for n, (s, kblk, cls) in enumerate(tiles):
                    lg = jnp.dot(kb_ref[0, kblk * BLK:(kblk + 1) * BLK, :], qh_ref[e, s, hh],
                                 preferred_element_type=F32)
                    if cls != "far":
                        lg = lg + bias_ref[hh, 0 if cls == "own" else 1]
                    lg_ref[slot, hh, n] = lg
                    cm = jnp.max(lg, axis=0, keepdims=True)
                    off = None
                    if cls != "own":
                        off = addm_ref[e, s, nblk * hh + kblk:nblk * hh + kblk + 1, :]
                        if cls == "far":
                            off = off + relb_ref[2 * hp + hh, REL_BUCKETS - 1] * LOG2E
                        cm = cm + off
                    cmax[s].append(cm)
                    offs.append(off)
                    yield
                m = {s: functools.reduce(jnp.maximum, cmax[s]) for s in (0, 1)}
                for n, (s, _, _) in enumerate(tiles):
                    moff_ref[slot, hh, n:n + 1, :] = m[s] if offs[n] is None else m[s] - offs[n]

    def softmax_pv_stages(parity):
        for e in range(2):
            t = 2 * parity + e
            slot = 2 * parity + e
            tiles = item_tiles(t)
            for hh in range(2):
                acc = {0: None, 1: None}
                for n, (s, kblk, _) in enumerate(tiles):
                    p = jnp.exp2(lg_ref[slot, hh, n] - moff_ref[slot, hh, n:n + 1, :])
                    pvn = jnp.dot(va_ref[0, kblk, hh], p.astype(BF16),
                                  preferred_element_type=F32)
                    acc[s] = pvn if acc[s] is None else acc[s] + pvn
                    yield
                for s in (0, 1):
                    oT_ref[0, 2 * e + s, HEAD_DIM * hh:HEAD_DIM * (hh + 1), :] = (
                        acc[s][0:HEAD_DIM] / acc[s][HEAD_DIM:HEAD_DIM + 1])

    ATT, GDN = 3, 1

    @pl.when(step == 0)
    def _():
        st_ref[...] = jnp.zeros(st_ref.shape, F32)
        _interleave_weighted((score_stages(0), ATT), (solve_stages(0), GDN))

    for parity in range(2):
        @pl.when((step > 0) & (step < nsteps - 1) & (step % 2 == parity))
        def _(parity=parity):
            _interleave_weighted((recurrence_stages(1 - parity, seq_start), GDN),
                                 (score_stages(parity), ATT),
                                 (solve_stages(parity), GDN),
                                 (softmax_pv_stages(1 - parity), ATT))

    @pl.when(step == nsteps - 1)
    def _():
        last = (nsteps - 2) % 2
        _interleave_weighted((recurrence_stages(last, seq_start), GDN), (softmax_pv_stages(last), ATT))


def _token_mixers(rel_bias, qT, k3, ksum, va, gqkv, sz, gb, nw_row, B, S):
    T = B * S
    W = GDN_WIDTH
    TILE = GDN_TILE
    nblk = S // MOBA_BLOCK
    half = nblk // 2
    assert TILE == GDN_HALF and S % TILE == 0
    assert BUCKET_LOWER[REL_BUCKETS - 1] <= MOBA_BLOCK + 1
    assert nblk == 8 and nblk + 1 <= 2 * SUBLANES
    nchunk = TILE // GDN_CHUNK
    ntiles = T // TILE
    npairs = (ATT_HEADS // 2) * B * (half // 2)
    assert npairs == ntiles
    nsteps = ntiles + 1

    def scored(s):
        p = jnp.minimum(s, npairs - 1)
        return p // (B * (half // 2)), (p // (half // 2)) % B, p % (half // 2)

    def done(s):
        p = jnp.maximum(s - 1, 0)
        return p // (B * (half // 2)), (p // (half // 2)) % B, p % (half // 2)

    def q_lo(s):
        hp, b, m = scored(s)
        return (b, m, hp, 0)

    def q_hi(s):
        hp, b, m = scored(s)
        return (b, half - 1 - m, hp, 0)

    def k_blk(s):
        hp, b, _ = scored(s)
        return (b, 0, hp)

    def v_blk(s):
        hp, b, _ = done(s)
        return (b, 0, hp, 0, 0)

    def o_blk(s):
        hp, b, m = done(s)
        return (b, m, hp, 0)

    cur_tile = lambda off: (lambda s: (jnp.minimum(s, ntiles - 1), off))
    prev_tile = lambda s: (jnp.maximum(s - 1, 0), 0)
    return pl.pallas_call(
        functools.partial(_mixers_kernel, nblk=nblk, nbatch=B, tiles_per_seq=S // TILE, nsteps=nsteps),
        grid=(nsteps,),
        in_specs=[
            pl.BlockSpec(memory_space=pltpu.SMEM),
            pl.BlockSpec((1, 2, LANES, MOBA_BLOCK), q_lo),
            pl.BlockSpec((1, 2, LANES, MOBA_BLOCK), q_hi),
            pl.BlockSpec((1, S, LANES), k_blk),
            pl.BlockSpec((S // INPROJ_TILE, SUBLANES, LANES), k_blk),
            pl.BlockSpec((1, nblk, 2, V_ROWS, MOBA_BLOCK), v_blk),
            pl.BlockSpec((TILE, W), cur_tile(0)),
            pl.BlockSpec((TILE, W), cur_tile(1)),
            pl.BlockSpec((TILE, W), cur_tile(2)),
            pl.BlockSpec((TILE, LANES), cur_tile(0)),
            pl.BlockSpec((TILE, W), prev_tile),
            pl.BlockSpec((1, W), lambda s: (0, 0)),
        ],
        out_specs=[
            pl.BlockSpec((1, 4, LANES, MOBA_BLOCK), o_blk),
            pl.BlockSpec((TILE, W), prev_tile),
        ],
        out_shape=[
            jax.ShapeDtypeStruct((B, nblk, ATT_WIDTH, MOBA_BLOCK), F32),
            jax.ShapeDtypeStruct((T, GDN_WIDTH), F32),
        ],
        scratch_shapes=[
            pltpu.VMEM((2, 2, MOBA_BLOCK, MOBA_BLOCK), F32),
            pltpu.VMEM((2, 2, 2 * nblk, MOBA_BLOCK), F32),
            pltpu.VMEM((2, 2, 2, LANES, MOBA_BLOCK), BF16),
            pltpu.VMEM((4, 2, nblk + 1, MOBA_BLOCK, MOBA_BLOCK), F32),
            pltpu.VMEM((4, 2, 2 * SUBLANES, MOBA_BLOCK), F32),
            pltpu.VMEM((2, TILE, W), F32),
            pltpu.VMEM((2, nchunk, 2 * GDN_CHUNK, W), BF16),
            pltpu.VMEM((2, TILE, W), BF16),
            pltpu.VMEM((2, TILE, W), BF16),
            pltpu.VMEM((2, nchunk * SUBLANES, W), F32),
            pltpu.VMEM((W // LANES, LANES, LANES), F32),
        ],
        compiler_params=pltpu.CompilerParams(
            dimension_semantics=("arbitrary",), vmem_limit_bytes=VMEM_LIMIT),
        name="token_mixers",
    )(rel_bias, qT, qT, k3, ksum, va, gqkv, gqkv, gqkv, gb, sz, nw_row)


def _out_mlp_kernel(x_ref, oTa_ref, oTb_ref, og_ref, woa_ref, wog_ref, pmn_ref, pre_ref, post_ref,
                    wup_ref, wdn_ref, out_ref):
    oT = jnp.concatenate([oTa_ref[0, 0], oTb_ref[0, 0]], axis=1)
    o_att = oT.T.astype(BF16)
    mix = jnp.dot(o_att, woa_ref[...], preferred_element_type=F32)
    mix = mix + jnp.dot(og_ref[...].astype(BF16), wog_ref[...], preferred_element_type=F32)
    x1 = x_ref[...] + _rms(mix, pmn_ref[...])
    h = _rms(x1, pre_ref[...]).astype(BF16)
    acc = jnp.zeros((ROW_TILE, D_MODEL), F32)
    for c in range(D_FF // FF_TILE):
        up = jnp.dot(h, wup_ref[:, c * FF_TILE:(c + 1) * FF_TILE], preferred_element_type=F32)
        act = jnp.square(jnp.maximum(up, 0.0)).astype(BF16)
        acc = acc + jnp.dot(act, wdn_ref[c * FF_TILE:(c + 1) * FF_TILE, :], preferred_element_type=F32)
    out_ref[...] = x1 + _rms(acc, post_ref[...])


def _out_mlp(xf, oT, og, woa, wog, pmn, pre, post, wup, wdn, B, S):
    T = B * S
    nblk = S // MOBA_BLOCK
    tiles_per_seq = S // ROW_TILE
    assert ROW_TILE == 2 * MOBA_BLOCK
    const = lambda i: (0, 0)
    row = lambda i: (i, 0)

    def att_block(which):
        def index(i):
            blk = 2 * (i % tiles_per_seq) + which
            return (i // tiles_per_seq, _paired_pos(blk, nblk), 0, 0)
        return index

    single = dict(pipeline_mode=pl.Buffered(1))
    return pl.pallas_call(
        _out_mlp_kernel,
        grid=(T // ROW_TILE,),
        in_specs=[
            pl.BlockSpec((ROW_TILE, D_MODEL), row),
            pl.BlockSpec((1, 1, ATT_WIDTH, MOBA_BLOCK), att_block(0)),
            pl.BlockSpec((1, 1, ATT_WIDTH, MOBA_BLOCK), att_block(1)),
            pl.BlockSpec((ROW_TILE, GDN_WIDTH), row),
            pl.BlockSpec(woa.shape, const, **single),
            pl.BlockSpec(wog.shape, const, **single),
            pl.BlockSpec((1, D_MODEL), const),
            pl.BlockSpec((1, D_MODEL), const),
            pl.BlockSpec((1, D_MODEL), const),
            pl.BlockSpec(wup.shape, const, **single),
            pl.BlockSpec(wdn.shape, const, **single),
        ],
        out_specs=pl.BlockSpec((ROW_TILE, D_MODEL), row),
        out_shape=jax.ShapeDtypeStruct((T, D_MODEL), F32),
        compiler_params=pltpu.CompilerParams(
            dimension_semantics=("arbitrary",), vmem_limit_bytes=VMEM_LIMIT),
        name="out_mlp",
    )(xf, oT, oT, og, woa, wog, pmn, pre, post, wup, wdn)


def kernel(x, w_in, w_out, conv_w, A_log, dt_bias, gdn_norm_w, rel_bias, pre_mix_norm,
           post_mix_norm, pre_mlp_norm, post_mlp_norm, w_up, w_down):
    B, S, D = x.shape
    assert D == D_MODEL and S % ROW_TILE == 0 and S % MOBA_BLOCK == 0
    T = B * S
    depth = w_in.shape[0]
    xf = x.reshape(T, D)
    o0, o1, o2, o3, o4 = 0, ATT_WIDTH, 2 * ATT_WIDTH, 3 * ATT_WIDTH, 3 * ATT_WIDTH + 3 * GDN_WIDTH
    o5 = o4 + GDN_WIDTH
    for l in range(depth):
        wi = w_in[l]
        wqT = wi[:, o0:o1].T.astype(BF16)
        wk = wi[:, o1:o2].astype(BF16)
        wvT = wi[:, o2:o3].T.astype(BF16)
        wg = wi[:, o3:o4].astype(BF16)
        wz = wi[:, o4:o5].astype(BF16)
        wab = jnp.pad(wi[:, o5:], ((0, 0), (0, LANES - 2 * GDN_HEADS))).astype(BF16)
        pad8 = lambda v: jnp.pad(v.astype(F32), (0, LANES - GDN_HEADS))[None, :]
        qT, k, ksum, va, gqkv, sz, gb = _inproj(xf, pre_mix_norm[l][None, :], wqT, wk, wvT, wg, wz, wab,
                                                conv_w[l], pad8(A_log[l]), pad8(dt_bias[l]), B, S)
        oT, og = _token_mixers(rel_bias.astype(F32), qT, k.reshape(B, S, ATT_WIDTH), ksum, va, gqkv, sz,
                               gb, jnp.tile(gdn_norm_w[l], GDN_HEADS)[None, :], B, S)
        wo = w_out[l].astype(BF16)
        xf = _out_mlp(xf, oT, og, wo[:ATT_WIDTH], wo[ATT_WIDTH:], post_mix_norm[l][None, :],
                      pre_mlp_norm[l][None, :], post_mlp_norm[l][None, :],
                      w_up[l].astype(BF16), w_down[l].astype(BF16), B, S)
    return xf.reshape(B, S, D)
```

```python
import functools
import math

import jax
import jax.numpy as jnp
from jax import lax
from jax.experimental import pallas as pl
from jax.experimental.pallas import tpu as pltpu

F32 = jnp.float32
BF16 = jnp.bfloat16
HI = lax.Precision.HIGHEST

D_MODEL = 1024
HEAD_DIM = 64
ATT_HEADS = 8
GDN_HEADS = 8
ATT_WIDTH = ATT_HEADS * HEAD_DIM
GDN_WIDTH = GDN_HEADS * HEAD_DIM
MOBA_BLOCK = 256
MOBA_TOPK = 3
GDN_CHUNK = 64
CONV_WIDTH = 4
D_FF = 4 * D_MODEL
REL_BUCKETS = 32
REL_MAX_EXACT = 16
REL_MAX_DIST = 128
EPS = 1e-6
NEG = -1e30
LOG2E = math.log2(math.e)

LANES = 128
SUBLANES = 8
VMEM_LIMIT = 56 * 1024 * 1024
ROW_TILE = 512
INPROJ_TILE = 512
FF_TILE = 1024

NT = (((1,), (1,)), ((), ()))
TN = (((0,), (0,)), ((), ()))


def _bucket_lower_bounds():
    def bucket(d):
        if d < REL_MAX_EXACT:
            return d
        t = math.log(d / REL_MAX_EXACT) / math.log(REL_MAX_DIST / REL_MAX_EXACT)
        t = t * (REL_BUCKETS - REL_MAX_EXACT)
        assert d in (REL_MAX_EXACT, REL_MAX_DIST) or abs(t - round(t)) > 1e-6
        return min(REL_MAX_EXACT + int(t + 1e-9), REL_BUCKETS - 1)
    lower = []
    for b in range(REL_BUCKETS):
        d = 0
        while bucket(d) < b:
            d += 1
        lower.append(d)
    return lower


BUCKET_LOWER = _bucket_lower_bounds()


def _sigmoid(x):
    return 0.5 * jnp.tanh(0.5 * x) + 0.5


def _silu_of_half(h):
    return h + h * jnp.tanh(h)


def _rms(x, w):
    return x * lax.rsqrt(jnp.mean(x * x, axis=-1, keepdims=True) + EPS) * w


def _split_bf16(x, parts):
    out = []
    for _ in range(parts):
        h = x.astype(BF16)
        out.append(h)
        x = x - h.astype(F32)
    return out


def _dot_split_rhs(c, x, parts):
    acc = None
    for h in _split_bf16(x, parts):
        d = jnp.dot(c, h, preferred_element_type=F32)
        acc = d if acc is None else acc + d
    return acc


CONV_COLS = 512


def _inproj_kernel(x_ref, xp_ref, nw_ref, wqT_ref, wk_ref, wvT_ref, wg_ref, wz_ref, wab_ref,
                   cw_ref, alog_ref, dtb_ref,
                   qT_ref, k_ref, ksum_ref, va_ref, g_ref, z_ref, gb_ref, *, tiles_per_seq):
    h = _rms(x_ref[...], nw_ref[...]).astype(BF16)

    hp = _rms(xp_ref[...], nw_ref[...]).astype(BF16)
    seq_start = (pl.program_id(0) % tiles_per_seq) == 0
    trow8 = lax.broadcasted_iota(jnp.int32, (SUBLANES, CONV_COLS), 0)
    for c in range(3 * GDN_WIDTH // CONV_COLS):
        cols = slice(c * CONV_COLS, (c + 1) * CONV_COLS)
        cur = jnp.dot(h, wg_ref[:, cols], preferred_element_type=F32)
        prev8 = jnp.dot(hp, wg_ref[:, cols], preferred_element_type=F32)
        prev8 = jnp.where(seq_start, 0.0, prev8)
        cw_half = 0.5 * cw_ref[:, cols]
        acc = cur * cw_half[CONV_WIDTH - 1:CONV_WIDTH]
        for s in range(1, CONV_WIDTH):
            rolled = pltpu.roll(cur, s, 0)
            top = jnp.where(trow8 < s, pltpu.roll(prev8, s, 0), rolled[0:SUBLANES])
            tap = jnp.concatenate([top, rolled[SUBLANES:]], axis=0)
            acc = acc + tap * cw_half[CONV_WIDTH - 1 - s:CONV_WIDTH - s]
        g_ref[:, cols] = _silu_of_half(acc)

    z = jnp.dot(h, wz_ref[...], preferred_element_type=F32)
    z_ref[...] = _silu_of_half(0.5 * z)
    ab = jnp.dot(h, wab_ref[...], preferred_element_type=F32)
    xs = ab + dtb_ref[...]
    log_decay = -jnp.exp(alog_ref[...]) * (jnp.maximum(xs, 0.0) + jnp.log1p(jnp.exp(-jnp.abs(xs))))
    lane = lax.broadcasted_iota(jnp.int32, ab.shape, 1)
    gb_ref[...] = jnp.where(lane < GDN_HEADS, log_decay, _sigmoid(ab))

    qT = lax.dot_general(wqT_ref[...], h, NT, preferred_element_type=F32)
    vT = lax.dot_general(wvT_ref[...], h, NT, preferred_element_type=F32)
    k = jnp.dot(h, wk_ref[...], preferred_element_type=F32)
    k_ref[...] = k.astype(BF16)
    ones_row = jnp.where(lax.broadcasted_iota(jnp.int32, (V_ROWS - HEAD_DIM, MOBA_BLOCK), 0) == 0,
                         1.0, 0.0).astype(BF16)
    ksum_ref[...] = jnp.zeros(ksum_ref.shape, F32)
    for t in range(INPROJ_TILE // MOBA_BLOCK):
        blk = slice(t * MOBA_BLOCK, (t + 1) * MOBA_BLOCK)
        qT_ref[0, t] = qT[:, blk]
        ksum_ref[0, t:t + 1, :] = jnp.sum(k[blk], axis=0, keepdims=True)
        for hh in range(ATT_HEADS):
            va_ref[0, t, hh, 0:HEAD_DIM, :] = vT[HEAD_DIM * hh:HEAD_DIM * (hh + 1), blk].astype(BF16)
            va_ref[0, t, hh, HEAD_DIM:V_ROWS, :] = ones_row


def _inproj(xf, nw, wqT, wk, wvT, wg, wz, wab, conv_w, alog_pad, dtb_pad, B, S):
    T = B * S
    TM = INPROJ_TILE
    assert S % TM == 0
    nblk = S // MOBA_BLOCK
    tiles_per_seq = S // TM
    blk_per_tile = TM // MOBA_BLOCK
    const = lambda i: (0, 0)
    row = lambda i: (i, 0)
    tr = lambda i: (i // tiles_per_seq, i % tiles_per_seq, 0, 0)
    prev_rows = lambda i: (jnp.maximum(i * (TM // SUBLANES) - 1, 0), 0)
    single = dict(pipeline_mode=pl.Buffered(1))
    return pl.pallas_call(
        functools.partial(_inproj_kernel, tiles_per_seq=tiles_per_seq),
        grid=(T // TM,),
        in_specs=[
            pl.BlockSpec((TM, D_MODEL), row),
            pl.BlockSpec((SUBLANES, D_MODEL), prev_rows),
            pl.BlockSpec((1, D_MODEL), const),
            pl.BlockSpec(wqT.shape, const, **single),
            pl.BlockSpec(wk.shape, const, **single),
            pl.BlockSpec(wvT.shape, const, **single),
            pl.BlockSpec(wg.shape, const, **single),
            pl.BlockSpec(wz.shape, const, **single),
            pl.BlockSpec(wab.shape, const, **single),
            pl.BlockSpec(conv_w.shape, const),
            pl.BlockSpec((1, LANES), const),
            pl.BlockSpec((1, LANES), const),
        ],
        out_specs=[
            pl.BlockSpec((1, blk_per_tile, ATT_WIDTH, MOBA_BLOCK), tr),
            pl.BlockSpec((TM, ATT_WIDTH), row),
            pl.BlockSpec((1, SUBLANES, ATT_WIDTH), lambda i: (i, 0, 0)),
            pl.BlockSpec((1, blk_per_tile, ATT_HEADS, V_ROWS, MOBA_BLOCK),
                         lambda i: (i // tiles_per_seq, i % tiles_per_seq, 0, 0, 0)),
            pl.BlockSpec((TM, 3 * GDN_WIDTH), row),
            pl.BlockSpec((TM, GDN_WIDTH), row),
            pl.BlockSpec((TM, LANES), row),
        ],
        out_shape=[
            jax.ShapeDtypeStruct((B, nblk, ATT_WIDTH, MOBA_BLOCK), F32),
            jax.ShapeDtypeStruct((T, ATT_WIDTH), BF16),
            jax.ShapeDtypeStruct((T // TM, SUBLANES, ATT_WIDTH), F32),
            jax.ShapeDtypeStruct((B, nblk, ATT_HEADS, V_ROWS, MOBA_BLOCK), BF16),
            jax.ShapeDtypeStruct((T, 3 * GDN_WIDTH), F32),
            jax.ShapeDtypeStruct((T, GDN_WIDTH), F32),
            jax.ShapeDtypeStruct((T, LANES), F32),
        ],
        compiler_params=pltpu.CompilerParams(
            dimension_semantics=("arbitrary",), vmem_limit_bytes=VMEM_LIMIT),
        name="inproj",
    )(xf, xf, nw, wqT, wk, wvT, wg, wz, wab, conv_w, alog_pad, dtb_pad)


V_ROWS = HEAD_DIM + 16


def _paired_pos(i, nblk):
    return jnp.where(i < nblk // 2, 2 * i, 2 * (nblk - 1 - i) + 1)


GDN_TILE = 256
GDN_HALF = 2 * LANES


def _gdn_stages(yq_ref, yk_ref, yv_ref, gb_ref, sz_ref, nw_ref, out_ref,
                u_ref, wq_ref, a_ref, kd_ref, gl_ref, st_ref):
    C = GDN_CHUNK
    W = GDN_WIDTH
    TILE = GDN_TILE
    npair = W // LANES

    r_w = lax.broadcasted_iota(jnp.int32, (GDN_HALF, GDN_HALF), 0)
    c_w = lax.broadcasted_iota(jnp.int32, (GDN_HALF, GDN_HALF), 1)
    head_ones = jnp.where((r_w // HEAD_DIM) == (c_w // HEAD_DIM), 1.0, 0.0).astype(BF16)
    ltri_bd = jnp.where(((r_w // C) == (c_w // C)) & (c_w <= r_w), 1.0, 0.0).astype(BF16)
    tok = lax.broadcasted_iota(jnp.int32, (TILE, W), 0) % C
    col = lax.broadcasted_iota(jnp.int32, (TILE, W), 1) % HEAD_DIM
    causal_t = tok >= col
    strict_t = tok > col
    lane_t = lax.broadcasted_iota(jnp.int32, (TILE, LANES), 1)

    lane = lax.broadcasted_iota(jnp.int32, (C, LANES), 1)
    rowi = lax.broadcasted_iota(jnp.int32, (C, LANES), 0)
    first_head = lane < HEAD_DIM
    strict = rowi > (lane % HEAD_DIM)
    eye2 = jnp.where(rowi == (lane % HEAD_DIM), 1.0, 0.0)
    lane2 = lax.broadcasted_iota(jnp.int32, (C, 2 * LANES), 1)
    first_head2 = (lane2 % LANES) < HEAD_DIM
    r_l = lax.broadcasted_iota(jnp.int32, (LANES, LANES), 0)
    c_l = lax.broadcasted_iota(jnp.int32, (LANES, LANES), 1)
    same_head = (r_l // HEAD_DIM) == (c_l // HEAD_DIM)
    pair_ones = jnp.where(same_head, 1.0, 0.0).astype(BF16)

    def stack(x, mask):
        return jnp.concatenate([jnp.where(mask, x, 0.0), jnp.where(mask, 0.0, x)], axis=0)

    dot = functools.partial(jnp.dot, preferred_element_type=F32)

    def head_sumsq(y):
        y2 = (y * y).astype(BF16)
        return jnp.concatenate([dot(y2[:, h:h + GDN_HALF], head_ones) for h in range(0, W, GDN_HALF)],
                               axis=1)

    def solve_stages(slot):
        yq = yq_ref[...]
        yk = yk_ref[...]
        yv = yv_ref[...]
        qn = yq * lax.rsqrt(head_sumsq(yq) + EPS) * (HEAD_DIM ** -0.5)
        kn = yk * lax.rsqrt(head_sumsq(yk) + EPS)
        yield
        gbt = gb_ref[...]

        def spread(col0):
            pairs = []
            for p in range(npair):
                a = jnp.broadcast_to(gbt[:, col0 + 2 * p:col0 + 2 * p + 1], (TILE, LANES))
                b = jnp.broadcast_to(gbt[:, col0 + 2 * p + 1:col0 + 2 * p + 2], (TILE, LANES))
                pairs.append(jnp.where(lane_t < HEAD_DIM, a, b))
            return jnp.concatenate(pairs, axis=1)

        g = spread(0)
        beta = spread(GDN_HEADS)
        gcd = _dot_split_rhs(ltri_bd, jnp.concatenate([g, jnp.where(strict_t, g, 0.0)], axis=1), 2)
        yield
        gc = gcd[:, :W]
        decay = jnp.where(causal_t, jnp.exp(jnp.where(causal_t, gcd[:, W:], 0.0)), 0.0)
        egc = jnp.exp(gc)
        kb = kn * beta
        rv = yv * beta
        rk = kb * egc
        qd = qn * egc
        for cc in range(TILE // C):
            rs = slice(cc * C, (cc + 1) * C)
            g_last = gc[(cc + 1) * C - 1:(cc + 1) * C, :]
            kd_ref[slot, rs, :] = (kn[rs] * jnp.exp(g_last - gc[rs])).astype(BF16)
            gl_ref[slot, cc * SUBLANES:(cc + 1) * SUBLANES, :] = (
                jnp.broadcast_to(jnp.exp(g_last), (SUBLANES, W)))
        units = [(slice(cc * C, (cc + 1) * C), slice(LANES * p, LANES * (p + 1)), cc)
                 for cc in range(TILE // C) for p in range(npair)]
        kqs = [lax.dot_general(jnp.concatenate([kn[rs, ls], qn[rs, ls]], axis=0).astype(BF16),
                               stack(kn[rs, ls], first_head).astype(BF16), NT,
                               preferred_element_type=F32) for rs, ls, _ in units]
        yield
        ps = [-jnp.where(strict, kq[0:C] * beta[rs, ls] * decay[rs, ls], 0.0)
              for kq, (rs, ls, _) in zip(kqs, units)]
        ss = [eye2 + p for p in ps]
        ps = [dot(p.astype(BF16), stack(p, first_head).astype(BF16)) for p in ps]
        yield
        nround = int(math.log2(C))
        for k in range(1, nround):
            rhs = [stack(s_, first_head).astype(BF16) for s_ in ss]
            if k + 1 < nround:
                rhs = [jnp.concatenate([stack(p, first_head).astype(BF16), sx], axis=1)
                       for p, sx in zip(ps, rhs)]
            outs = [dot(p.astype(BF16), sx) for p, sx in zip(ps, rhs)]
            if k + 1 < nround:
                ps = [o[:, :LANES] for o in outs]
                ss = [s_ + o[:, LANES:] for s_, o in zip(ss, outs)]
            else:
                ss = [s_ + o for s_, o in zip(ss, outs)]
            yield
        xs = [dot(s_.astype(BF16),
                  stack(jnp.concatenate([rv[rs, ls], rk[rs, ls]], axis=1), first_head2).astype(BF16))
              for s_, (rs, ls, _) in zip(ss, units)]
        yield
        for x, kq, (rs, ls, cc) in zip(xs, kqs, units):
            u_ref[slot, rs, ls] = x[:, :LANES]
            wq_ref[slot, cc, 0:C, ls] = x[:, LANES:].astype(BF16)
            wq_ref[slot, cc, C:2 * C, ls] = qd[rs, ls].astype(BF16)
            a_ref[slot, rs, ls] = (kq[C:2 * C] * decay[rs, ls]).astype(BF16)

    lss = [slice(LANES * p, LANES * (p + 1)) for p in range(npair)]

    def recurrence_stages(slot, seq_start):
        states = [jnp.where(seq_start, 0.0, st_ref[p]) for p in range(npair)]
        pending = None

        def finish(rs, os_):
            for ls, o in zip(lss, os_):
                ms = dot((o * o).astype(BF16), pair_ones) * (1.0 / HEAD_DIM)
                out_ref[rs, ls] = o * lax.rsqrt(ms + EPS) * nw_ref[:, ls] * sz_ref[rs, ls]

        for cc in range(TILE // C):
            rs = slice(cc * C, (cc + 1) * C)
            wqs = [dot(wq_ref[slot, cc, :, ls], st.astype(BF16)) for ls, st in zip(lss, states)]
            if pending is not None:
                finish(*pending)
            yield
            v_news = [u_ref[slot, rs, ls] - wq[0:C] for ls, wq in zip(lss, wqs)]
            kvs = [lax.dot_general(kd_ref[slot, rs, ls], v.astype(BF16), TN, preferred_element_type=F32)
                   for ls, v in zip(lss, v_news)]
            os_ = [wq[C:2 * C] + dot(a_ref[slot, rs, ls], stack(v, first_head).astype(BF16))
                   for ls, wq, v in zip(lss, wqs, v_news)]
            states = [st * gl_ref[slot, cc * SUBLANES:cc * SUBLANES + 1, ls] + jnp.where(same_head, kv, 0.0)
                      for ls, st, kv in zip(lss, states, kvs)]
            pending = (rs, os_)
            yield
        finish(*pending)
        for p in range(npair):
            st_ref[p] = states[p]

    return solve_stages, recurrence_stages


def _interleave_weighted(*gens_and_weights):
    live = [[gen, weight] for gen, weight in gens_and_weights]
    while live:
        for entry in list(live):
            for _ in range(entry[1]):
                try:
                    next(entry[0])
                except StopIteration:
                    live.remove(entry)
                    break


def _mixers_kernel(relb_ref, qlo_ref, qhi_ref, kb_ref, ksum_ref, va_ref,
                   yq_ref, yk_ref, yv_ref, gb_ref, sz_ref, nw_ref,
                   oT_ref, og_ref,
                   bias_ref, addm_ref, qh_ref, lg_ref, moff_ref,
                   u_ref, wq_ref, a_ref, kd_ref, gl_ref, st_ref,
                   *, nblk, nbatch, tiles_per_seq, nsteps):
    step = pl.program_id(0)
    BLK = MOBA_BLOCK
    half = nblk // 2
    pair = jnp.minimum(step, nsteps - 2)
    hp = pair // (nbatch * (half // 2))
    b = (pair // (half // 2)) % nbatch
    scoring = step < nsteps - 1
    new_kv = (pair % (half // 2)) == 0

    solve_stages, recurrence_stages = _gdn_stages(
        yq_ref, yk_ref, yv_ref, gb_ref, sz_ref, nw_ref, og_ref,
        u_ref, wq_ref, a_ref, kd_ref, gl_ref, st_ref)
    seq_start = ((step - 1) % tiles_per_seq) == 0

    @pl.when((b == 0) & new_kv & scoring)
    def _():
        kk = lax.broadcasted_iota(jnp.int32, (BLK, BLK), 0)
        qq = lax.broadcasted_iota(jnp.int32, (BLK, BLK), 1)
        for hh in range(2):
            h = 2 * hp + hh
            for kind in range(2):
                d = qq - kk + kind * BLK
                val = jnp.full((BLK, BLK), relb_ref[h, REL_BUCKETS - 1], F32)
                for bkt in range(REL_BUCKETS - 2, -1, -1):
                    val = jnp.where(d < BUCKET_LOWER[bkt + 1], relb_ref[h, bkt], val)
                val = val * LOG2E
                if kind == 0:
                    val = jnp.where(d >= 0, val, NEG)
                bias_ref[hh, kind] = val

    def key_means():
        per_tile = INPROJ_TILE // BLK
        ks = ksum_ref[...]
        km = jnp.concatenate([ks[j // per_tile, j % per_tile:j % per_tile + 1, :] for j in range(nblk)],
                             axis=0) * (1.0 / BLK)
        lane = lax.broadcasted_iota(jnp.int32, (nblk, LANES), 1)
        return jnp.concatenate([jnp.where(lane < HEAD_DIM, km, 0.0),
                                jnp.where(lane >= HEAD_DIM, km, 0.0)], axis=0)

    def item_tiles(t):
        i_hi = nblk - 1 - t
        tiles = [(0, t, "own"), (1, i_hi, "own"), (1, i_hi - 1, "prev")]
        if t >= 1:
            tiles.append((0, t - 1, "prev"))
        tiles += [(0, j, "far") for j in range(t - 1)]
        tiles += [(1, j, "far") for j in range(i_hi - 1)]
        assert len(tiles) == nblk + 1
        return tiles

    def score_stages(parity):
        ridx = lax.broadcasted_iota(jnp.int32, (nblk, BLK), 0)
        sub = lax.broadcasted_iota(jnp.int32, (LANES, BLK), 0)
        scale = HEAD_DIM ** -0.5 * LOG2E
        km = key_means()
        for e in range(2):
            t = 2 * parity + e
            slot = 2 * parity + e
            q_of = ((qlo_ref, e, t), (qhi_ref, 1 - e, nblk - 1 - t))
            for s, (q_ref, w, qi) in enumerate(q_of):
                qT = q_ref[0, w]
                gT = jnp.dot(km, qT, precision=HI, preferred_element_type=F32)
                past = ridx < qi
                for hh in range(2):
                    gm = jnp.where(past, gT[nblk * hh:nblk * (hh + 1)], -jnp.inf)
                    cnt = jnp.zeros((nblk, BLK), F32)
                    for jp in range(nblk):
                        row = gm[jp:jp + 1, :]
                        beats = (row > gm) | ((row == gm) & (ridx > jp))
                        cnt = cnt + jnp.where(beats, 1.0, 0.0)
                    visible = past & (cnt < MOBA_TOPK)
                    addm_ref[e, s, nblk * hh:nblk * (hh + 1), :] = jnp.where(visible, 0.0, NEG)
                    in_head = (sub >= HEAD_DIM * hh) & (sub < HEAD_DIM * (hh + 1))
                    qh_ref[e, s, hh] = jnp.where(in_head, qT * scale, 0.0).astype(BF16)
            yield
            tiles = item_tiles(t)
            for hh in range(2):
                cmax = {0: [], 1: []}
                offs = []
                for n, (s, kblk, cls) in enumerate(tiles):
                    lg = jnp.dot(kb_ref[0, kblk * BLK:(kblk + 1) * BLK, :], qh_ref[e, s, hh],
                                 preferred_element_type=F32)
                    if cls != "far":
                        lg = lg + bias_ref[hh, 0 if cls == "own" else 1]
                    lg_ref[slot, hh, n] = lg
                    cm = jnp.max(lg, axis=0, keepdims=True)
                    off = None
                    if cls != "own":
                        off = addm_ref[e, s, nblk * hh + kblk:nblk * hh + kblk + 1, :]
                        if cls == "far":
                            off = off + relb_ref[2 * hp + hh, REL_BUCKETS - 1] * LOG2E
                        cm = cm + off
                    cmax[s].append(cm)
                    offs.append(off)
                    yield
                m = {s: functools.reduce(jnp.maximum, cmax[s]) for s in (0, 1)}
                for n, (s, _, _) in enumerate(tiles):
                    moff_ref[slot, hh, n:n + 1, :] = m[s] if offs[n] is None else m[s] - offs[n]

    def softmax_pv_stages(parity):
        for e in range(2):
            t = 2 * parity + e
            slot = 2 * parity + e
            tiles = item_tiles(t)
            for hh in range(2):
                acc = {0: None, 1: None}
                for n, (s, kblk, _) in enumerate(tiles):
                    p = jnp.exp2(lg_ref[slot, hh, n] - moff_ref[slot, hh, n:n + 1, :])
                    pvn = jnp.dot(va_ref[0, kblk, hh], p.astype(BF16),
                                  preferred_element_type=F32)
                    acc[s] = pvn if acc[s] is None else acc[s] + pvn
                    yield
                for s in (0, 1):
                    oT_ref[0, 2 * e + s, HEAD_DIM * hh:HEAD_DIM * (hh + 1), :] = (
                        acc[s][0:HEAD_DIM] / acc[s][HEAD_DIM:HEAD_DIM + 1])

    ATT, GDN = 3, 1

    @pl.when(step == 0)
    def _():
        st_ref[...] = jnp.zeros(st_ref.shape, F32)
        _interleave_weighted((score_stages(0), ATT), (solve_stages(0), GDN))

    for parity in range(2):
        @pl.when((step > 0) & (step < nsteps - 1) & (step % 2 == parity))
        def _(parity=parity):
            _interleave_weighted((recurrence_stages(1 - parity, seq_start), GDN),
                                 (solve_stages(parity), GDN),
                                 (score_stages(parity), ATT),
                                 (softmax_pv_stages(1 - parity), ATT))

    @pl.when(step == nsteps - 1)
    def _():
        last = (nsteps - 2) % 2
        _interleave_weighted((recurrence_stages(last, seq_start), GDN), (softmax_pv_stages(last), ATT))


def _token_mixers(rel_bias, qT, k3, ksum, va, gqkv, sz, gb, nw_row, B, S):
    T = B * S
    W = GDN_WIDTH
    TILE = GDN_TILE
    nblk = S // MOBA_BLOCK
    half = nblk // 2
    assert TILE == GDN_HALF and S % TILE == 0
    assert BUCKET_LOWER[REL_BUCKETS - 1] <= MOBA_BLOCK + 1
    assert nblk == 8 and nblk + 1 <= 2 * SUBLANES
    nchunk = TILE // GDN_CHUNK
    ntiles = T // TILE
    npairs = (ATT_HEADS // 2) * B * (half // 2)
    assert npairs == ntiles
    nsteps = ntiles + 1

    def scored(s):
        p = jnp.minimum(s, npairs - 1)
        return p // (B * (half // 2)), (p // (half // 2)) % B, p % (half // 2)

    def done(s):
        p = jnp.maximum(s - 1, 0)
        return p // (B * (half // 2)), (p // (half // 2)) % B, p % (half // 2)

    def q_lo(s):
        hp, b, m = scored(s)
        return (b, m, hp, 0)

    def q_hi(s):
        hp, b, m = scored(s)
        return (b, half - 1 - m, hp, 0)

    def k_blk(s):
        hp, b, _ = scored(s)
        return (b, 0, hp)

    def v_blk(s):
        hp, b, _ = done(s)
        return (b, 0, hp, 0, 0)

    def o_blk(s):
        hp, b, m = done(s)
        return (b, m, hp, 0)

    cur_tile = lambda off: (lambda s: (jnp.minimum(s, ntiles - 1), off))
    prev_tile = lambda s: (jnp.maximum(s - 1, 0), 0)
    return pl.pallas_call(
        functools.partial(_mixers_kernel, nblk=nblk, nbatch=B, tiles_per_seq=S // TILE, nsteps=nsteps),
        grid=(nsteps,),
        in_specs=[
            pl.BlockSpec(memory_space=pltpu.SMEM),
            pl.BlockSpec((1, 2, LANES, MOBA_BLOCK), q_lo),
            pl.BlockSpec((1, 2, LANES, MOBA_BLOCK), q_hi),
            pl.BlockSpec((1, S, LANES), k_blk),
            pl.BlockSpec((S // INPROJ_TILE, SUBLANES, LANES), k_blk),
            pl.BlockSpec((1, nblk, 2, V_ROWS, MOBA_BLOCK), v_blk),
            pl.BlockSpec((TILE, W), cur_tile(0)),
            pl.BlockSpec((TILE, W), cur_tile(1)),
            pl.BlockSpec((TILE, W), cur_tile(2)),
            pl.BlockSpec((TILE, LANES), cur_tile(0)),
            pl.BlockSpec((TILE, W), prev_tile),
            pl.BlockSpec((1, W), lambda s: (0, 0)),
        ],
        out_specs=[
            pl.BlockSpec((1, 4, LANES, MOBA_BLOCK), o_blk),
            pl.BlockSpec((TILE, W), prev_tile),
        ],
        out_shape=[
            jax.ShapeDtypeStruct((B, nblk, ATT_WIDTH, MOBA_BLOCK), F32),
            jax.ShapeDtypeStruct((T, GDN_WIDTH), F32),
        ],
        scratch_shapes=[
            pltpu.VMEM((2, 2, MOBA_BLOCK, MOBA_BLOCK), F32),
            pltpu.VMEM((2, 2, 2 * nblk, MOBA_BLOCK), F32),
            pltpu.VMEM((2, 2, 2, LANES, MOBA_BLOCK), BF16),
            pltpu.VMEM((4, 2, nblk + 1, MOBA_BLOCK, MOBA_BLOCK), F32),
            pltpu.VMEM((4, 2, 2 * SUBLANES, MOBA_BLOCK), F32),
            pltpu.VMEM((2, TILE, W), F32),
            pltpu.VMEM((2, nchunk, 2 * GDN_CHUNK, W), BF16),
            pltpu.VMEM((2, TILE, W), BF16),
            pltpu.VMEM((2, TILE, W), BF16),
            pltpu.VMEM((2, nchunk * SUBLANES, W), F32),
            pltpu.VMEM((W // LANES, LANES, LANES), F32),
        ],
        compiler_params=pltpu.CompilerParams(
            dimension_semantics=("arbitrary",), vmem_limit_bytes=VMEM_LIMIT),
        name="token_mixers",
    )(rel_bias, qT, qT, k3, ksum, va, gqkv, gqkv, gqkv, gb, sz, nw_row)


def _out_mlp_kernel(x_ref, oTa_ref, oTb_ref, og_ref, woa_ref, wog_ref, pmn_ref, pre_ref, post_ref,
                    wup_ref, wdn_ref, out_ref):
    oT = jnp.concatenate([oTa_ref[0, 0], oTb_ref[0, 0]], axis=1)
    o_att = oT.T.astype(BF16)
    mix = jnp.dot(o_att, woa_ref[...], preferred_element_type=F32)
    mix = mix + jnp.dot(og_ref[...].astype(BF16), wog_ref[...], preferred_element_type=F32)
    x1 = x_ref[...] + _rms(mix, pmn_ref[...])
    h = _rms(x1, pre_ref[...]).astype(BF16)
    acc = jnp.zeros((ROW_TILE, D_MODEL), F32)
    for c in range(D_FF // FF_TILE):
        up = jnp.dot(h, wup_ref[:, c * FF_TILE:(c + 1) * FF_TILE], preferred_element_type=F32)
        act = jnp.square(jnp.maximum(up, 0.0)).astype(BF16)
        acc = acc + jnp.dot(act, wdn_ref[c * FF_TILE:(c + 1) * FF_TILE, :], preferred_element_type=F32)
    out_ref[...] = x1 + _rms(acc, post_ref[...])


def _out_mlp(xf, oT, og, woa, wog, pmn, pre, post, wup, wdn, B, S):
    T = B * S
    nblk = S // MOBA_BLOCK
    tiles_per_seq = S // ROW_TILE
    assert ROW_TILE == 2 * MOBA_BLOCK
    const = lambda i: (0, 0)
    row = lambda i: (i, 0)

    def att_block(which):
        def index(i):
            blk = 2 * (i % tiles_per_seq) + which
            return (i // tiles_per_seq, _paired_pos(blk, nblk), 0, 0)
        return index

    single = dict(pipeline_mode=pl.Buffered(1))
    return pl.pallas_call(
        _out_mlp_kernel,
        grid=(T // ROW_TILE,),
        in_specs=[
            pl.BlockSpec((ROW_TILE, D_MODEL), row),
            pl.BlockSpec((1, 1, ATT_WIDTH, MOBA_BLOCK), att_block(0)),
            pl.BlockSpec((1, 1, ATT_WIDTH, MOBA_BLOCK), att_block(1)),
            pl.BlockSpec((ROW_TILE, GDN_WIDTH), row),
            pl.BlockSpec(woa.shape, const, **single),
            pl.BlockSpec(wog.shape, const, **single),
            pl.BlockSpec((1, D_MODEL), const),
            pl.BlockSpec((1, D_MODEL), const),
            pl.BlockSpec((1, D_MODEL), const),
            pl.BlockSpec(wup.shape, const, **single),
            pl.BlockSpec(wdn.shape, const, **single),
        ],
        out_specs=pl.BlockSpec((ROW_TILE, D_MODEL), row),
        out_shape=jax.ShapeDtypeStruct((T, D_MODEL), F32),
        compiler_params=pltpu.CompilerParams(
            dimension_semantics=("arbitrary",), vmem_limit_bytes=VMEM_LIMIT),
        name="out_mlp",
    )(xf, oT, oT, og, woa, wog, pmn, pre, post, wup, wdn)


def kernel(x, w_in, w_out, conv_w, A_log, dt_bias, gdn_norm_w, rel_bias, pre_mix_norm,
           post_mix_norm, pre_mlp_norm, post_mlp_norm, w_up, w_down):
    B, S, D = x.shape
    assert D == D_MODEL and S % ROW_TILE == 0 and S % MOBA_BLOCK == 0
    T = B * S
    depth = w_in.shape[0]
    xf = x.reshape(T, D)
    o0, o1, o2, o3, o4 = 0, ATT_WIDTH, 2 * ATT_WIDTH, 3 * ATT_WIDTH, 3 * ATT_WIDTH + 3 * GDN_WIDTH
    o5 = o4 + GDN_WIDTH
    for l in range(depth):
        wi = w_in[l]
        wqT = wi[:, o0:o1].T.astype(BF16)
        wk = wi[:, o1:o2].astype(BF16)
        wvT = wi[:, o2:o3].T.astype(BF16)
        wg = wi[:, o3:o4].astype(BF16)
        wz = wi[:, o4:o5].astype(BF16)
        wab = jnp.pad(wi[:, o5:], ((0, 0), (0, LANES - 2 * GDN_HEADS))).astype(BF16)
        pad8 = lambda v: jnp.pad(v.astype(F32), (0, LANES - GDN_HEADS))[None, :]
        qT, k, ksum, va, gqkv, sz, gb = _inproj(xf, pre_mix_norm[l][None, :], wqT, wk, wvT, wg, wz, wab,
                                                conv_w[l], pad8(A_log[l]), pad8(dt_bias[l]), B, S)
        oT, og = _token_mixers(rel_bias.astype(F32), qT, k.reshape(B, S, ATT_WIDTH), ksum, va, gqkv, sz,
                               gb, jnp.tile(gdn_norm_w[l], GDN_HEADS)[None, :], B, S)
        wo = w_out[l].astype(BF16)
        xf = _out_mlp(xf, oT, og, wo[:ATT_WIDTH], wo[ATT_WIDTH:], post_mix_norm[l][None, :],
                      pre_mlp_norm[l][None, :], post_mlp_norm[l][None, :],
                      w_up[l].astype(BF16), w_down[l].astype(BF16), B, S)
    return xf.reshape(B, S, D)
```

```python
import functools
import math

import jax
import jax.numpy as jnp
from jax import lax
from jax.experimental import pallas as pl
from jax.experimental.pallas import tpu as pltpu

F32 = jnp.float32
BF16 = jnp.bfloat16
HI = lax.Precision.HIGHEST

D_MODEL = 1024
HEAD_DIM = 64
ATT_HEADS = 8
GDN_HEADS = 8
ATT_WIDTH = ATT_HEADS * HEAD_DIM
GDN_WIDTH = GDN_HEADS * HEAD_DIM
MOBA_BLOCK = 256
MOBA_TOPK = 3
GDN_CHUNK = 64
CONV_WIDTH = 4
D_FF = 4 * D_MODEL
REL_BUCKETS = 32
REL_MAX_EXACT = 16
REL_MAX_DIST = 128
EPS = 1e-6
NEG = -1e30
LOG2E = math.log2(math.e)

LANES = 128
SUBLANES = 8
VMEM_LIMIT = 56 * 1024 * 1024
ROW_TILE = 512
INPROJ_TILE = 512
FF_TILE = 1024

NT = (((1,), (1,)), ((), ()))
TN = (((0,), (0,)), ((), ()))


def _bucket_lower_bounds():
    def bucket(d):
        if d < REL_MAX_EXACT:
            return d
        t = math.log(d / REL_MAX_EXACT) / math.log(REL_MAX_DIST / REL_MAX_EXACT)
        t = t * (REL_BUCKETS - REL_MAX_EXACT)
        assert d in (REL_MAX_EXACT, REL_MAX_DIST) or abs(t - round(t)) > 1e-6
        return min(REL_MAX_EXACT + int(t + 1e-9), REL_BUCKETS - 1)
    lower = []
    for b in range(REL_BUCKETS):
        d = 0
        while bucket(d) < b:
            d += 1
        lower.append(d)
    return lower


BUCKET_LOWER = _bucket_lower_bounds()


def _sigmoid(x):
    return 0.5 * jnp.tanh(0.5 * x) + 0.5


def _silu_of_half(h):
    return h + h * jnp.tanh(h)


def _rms(x, w):
    return x * lax.rsqrt(jnp.mean(x * x, axis=-1, keepdims=True) + EPS) * w


def _split_bf16(x, parts):
    out = []
    for _ in range(parts):
        h = x.astype(BF16)
        out.append(h)
        x = x - h.astype(F32)
    return out


def _dot_split_rhs(c, x, parts):
    acc = None
    for h in _split_bf16(x, parts):
        d = jnp.dot(c, h, preferred_element_type=F32)
        acc = d if acc is None else acc + d
    return acc


CONV_COLS = 512


def _inproj_kernel(x_ref, xp_ref, nw_ref, wqT_ref, wk_ref, wvT_ref, wg_ref, wz_ref, wab_ref,
                   cw_ref, alog_ref, dtb_ref,
                   qT_ref, k_ref, ksum_ref, va_ref, g_ref, z_ref, gb_ref, *, tiles_per_seq):
    h = _rms(x_ref[...], nw_ref[...]).astype(BF16)

    hp = _rms(xp_ref[...], nw_ref[...]).astype(BF16)
    seq_start = (pl.program_id(0) % tiles_per_seq) == 0
    trow8 = lax.broadcasted_iota(jnp.int32, (SUBLANES, CONV_COLS), 0)
    for c in range(3 * GDN_WIDTH // CONV_COLS):
        cols = slice(c * CONV_COLS, (c + 1) * CONV_COLS)
        cur = jnp.dot(h, wg_ref[:, cols], preferred_element_type=F32)
        prev8 = jnp.dot(hp, wg_ref[:, cols], preferred_element_type=F32)
        prev8 = jnp.where(seq_start, 0.0, prev8)
        cw_half = 0.5 * cw_ref[:, cols]
        acc = cur * cw_half[CONV_WIDTH - 1:CONV_WIDTH]
        for s in range(1, CONV_WIDTH):
            rolled = pltpu.roll(cur, s, 0)
            top = jnp.where(trow8 < s, pltpu.roll(prev8, s, 0), rolled[0:SUBLANES])
            tap = jnp.concatenate([top, rolled[SUBLANES:]], axis=0)
            acc = acc + tap * cw_half[CONV_WIDTH - 1 - s:CONV_WIDTH - s]
        g_ref[:, cols] = _silu_of_half(acc)

    z = jnp.dot(h, wz_ref[...], preferred_element_type=F32)
    z_ref[...] = _silu_of_half(0.5 * z)
    ab = jnp.dot(h, wab_ref[...], preferred_element_type=F32)
    xs = ab + dtb_ref[...]
    log_decay = -jnp.exp(alog_ref[...]) * (jnp.maximum(xs, 0.0) + jnp.log1p(jnp.exp(-jnp.abs(xs))))
    lane = lax.broadcasted_iota(jnp.int32, ab.shape, 1)
    gb_ref[...] = jnp.where(lane < GDN_HEADS, log_decay, _sigmoid(ab))

    qT = lax.dot_general(wqT_ref[...], h, NT, preferred_element_type=F32)
    vT = lax.dot_general(wvT_ref[...], h, NT, preferred_element_type=F32)
    k = jnp.dot(h, wk_ref[...], preferred_element_type=F32)
    k_ref[...] = k.astype(BF16)
    ones_row = jnp.where(lax.broadcasted_iota(jnp.int32, (V_ROWS - HEAD_DIM, MOBA_BLOCK), 0) == 0,
                         1.0, 0.0).astype(BF16)
    ksum_ref[...] = jnp.zeros(ksum_ref.shape, F32)
    for t in range(INPROJ_TILE // MOBA_BLOCK):
        blk = slice(t * MOBA_BLOCK, (t + 1) * MOBA_BLOCK)
        qT_ref[0, t] = qT[:, blk]
        ksum_ref[0, t:t + 1, :] = jnp.sum(k[blk], axis=0, keepdims=True)
        for hh in range(ATT_HEADS):
            va_ref[0, t, hh, 0:HEAD_DIM, :] = vT[HEAD_DIM * hh:HEAD_DIM * (hh + 1), blk].astype(BF16)
            va_ref[0, t, hh, HEAD_DIM:V_ROWS, :] = ones_row


def _inproj(xf, nw, wqT, wk, wvT, wg, wz, wab, conv_w, alog_pad, dtb_pad, B, S):
    T = B * S
    TM = INPROJ_TILE
    assert S % TM == 0
    nblk = S // MOBA_BLOCK
    tiles_per_seq = S // TM
    blk_per_tile = TM // MOBA_BLOCK
    const = lambda i: (0, 0)
    row = lambda i: (i, 0)
    tr = lambda i: (i // tiles_per_seq, i % tiles_per_seq, 0, 0)
    prev_rows = lambda i: (jnp.maximum(i * (TM // SUBLANES) - 1, 0), 0)
    single = dict(pipeline_mode=pl.Buffered(1))
    return pl.pallas_call(
        functools.partial(_inproj_kernel, tiles_per_seq=tiles_per_seq),
        grid=(T // TM,),
        in_specs=[
            pl.BlockSpec((TM, D_MODEL), row),
            pl.BlockSpec((SUBLANES, D_MODEL), prev_rows),
            pl.BlockSpec((1, D_MODEL), const),
            pl.BlockSpec(wqT.shape, const, **single),
            pl.BlockSpec(wk.shape, const, **single),
            pl.BlockSpec(wvT.shape, const, **single),
            pl.BlockSpec(wg.shape, const, **single),
            pl.BlockSpec(wz.shape, const, **single),
            pl.BlockSpec(wab.shape, const, **single),
            pl.BlockSpec(conv_w.shape, const),
            pl.BlockSpec((1, LANES), const),
            pl.BlockSpec((1, LANES), const),
        ],
        out_specs=[
            pl.BlockSpec((1, blk_per_tile, ATT_WIDTH, MOBA_BLOCK), tr),
            pl.BlockSpec((TM, ATT_WIDTH), row),
            pl.BlockSpec((1, SUBLANES, ATT_WIDTH), lambda i: (i, 0, 0)),
            pl.BlockSpec((1, blk_per_tile, ATT_HEADS, V_ROWS, MOBA_BLOCK),
                         lambda i: (i // tiles_per_seq, i % tiles_per_seq, 0, 0, 0)),
            pl.BlockSpec((TM, 3 * GDN_WIDTH), row),
            pl.BlockSpec((TM, GDN_WIDTH), row),
            pl.BlockSpec((TM, LANES), row),
        ],
        out_shape=[
            jax.ShapeDtypeStruct((B, nblk, ATT_WIDTH, MOBA_BLOCK), F32),
            jax.ShapeDtypeStruct((T, ATT_WIDTH), BF16),
            jax.ShapeDtypeStruct((T // TM, SUBLANES, ATT_WIDTH), F32),
            jax.ShapeDtypeStruct((B, nblk, ATT_HEADS, V_ROWS, MOBA_BLOCK), BF16),
            jax.ShapeDtypeStruct((T, 3 * GDN_WIDTH), F32),
            jax.ShapeDtypeStruct((T, GDN_WIDTH), F32),
            jax.ShapeDtypeStruct((T, LANES), F32),
        ],
        compiler_params=pltpu.CompilerParams(
            dimension_semantics=("arbitrary",), vmem_limit_bytes=VMEM_LIMIT),
        name="inproj",
    )(xf, xf, nw, wqT, wk, wvT, wg, wz, wab, conv_w, alog_pad, dtb_pad)


V_ROWS = HEAD_DIM + 16


def _paired_pos(i, nblk):
    return jnp.where(i < nblk // 2, 2 * i, 2 * (nblk - 1 - i) + 1)


GDN_TILE = 256
GDN_HALF = 2 * LANES


def _gdn_stages(yq_ref, yk_ref, yv_ref, gb_ref, sz_ref, nw_ref, out_ref,
                u_ref, wq_ref, a_ref, kd_ref, gl_ref, st_ref):
    C = GDN_CHUNK
    W = GDN_WIDTH
    TILE = GDN_TILE
    npair = W // LANES

    r_w = lax.broadcasted_iota(jnp.int32, (GDN_HALF, GDN_HALF), 0)
    c_w = lax.broadcasted_iota(jnp.int32, (GDN_HALF, GDN_HALF), 1)
    head_ones = jnp.where((r_w // HEAD_DIM) == (c_w // HEAD_DIM), 1.0, 0.0).astype(BF16)
    ltri_bd = jnp.where(((r_w // C) == (c_w // C)) & (c_w <= r_w), 1.0, 0.0).astype(BF16)
    tok = lax.broadcasted_iota(jnp.int32, (TILE, W), 0) % C
    col = lax.broadcasted_iota(jnp.int32, (TILE, W), 1) % HEAD_DIM
    causal_t = tok >= col
    strict_t = tok > col
    lane_t = lax.broadcasted_iota(jnp.int32, (TILE, LANES), 1)

    lane = lax.broadcasted_iota(jnp.int32, (C, LANES), 1)
    rowi = lax.broadcasted_iota(jnp.int32, (C, LANES), 0)
    first_head = lane < HEAD_DIM
    strict = rowi > (lane % HEAD_DIM)
    eye2 = jnp.where(rowi == (lane % HEAD_DIM), 1.0, 0.0)
    lane2 = lax.broadcasted_iota(jnp.int32, (C, 2 * LANES), 1)
    first_head2 = (lane2 % LANES) < HEAD_DIM
    r_l = lax.broadcasted_iota(jnp.int32, (LANES, LANES), 0)
    c_l = lax.broadcasted_iota(jnp.int32, (LANES, LANES), 1)
    same_head = (r_l // HEAD_DIM) == (c_l // HEAD_DIM)
    pair_ones = jnp.where(same_head, 1.0, 0.0).astype(BF16)

    def stack(x, mask):
        return jnp.concatenate([jnp.where(mask, x, 0.0), jnp.where(mask, 0.0, x)], axis=0)

    dot = functools.partial(jnp.dot, preferred_element_type=F32)

    def head_sumsq(y):
        y2 = (y * y).astype(BF16)
        return jnp.concatenate([dot(y2[:, h:h + GDN_HALF], head_ones) for h in range(0, W, GDN_HALF)],
                               axis=1)

    def solve_stages(slot):
        yq = yq_ref[...]
        yk = yk_ref[...]
        yv = yv_ref[...]
        qn = yq * lax.rsqrt(head_sumsq(yq) + EPS) * (HEAD_DIM ** -0.5)
        kn = yk * lax.rsqrt(head_sumsq(yk) + EPS)
        yield
        gbt = gb_ref[...]

        def spread(col0):
            pairs = []
            for p in range(npair):
                a = jnp.broadcast_to(gbt[:, col0 + 2 * p:col0 + 2 * p + 1], (TILE, LANES))
                b = jnp.broadcast_to(gbt[:, col0 + 2 * p + 1:col0 + 2 * p + 2], (TILE, LANES))
                pairs.append(jnp.where(lane_t < HEAD_DIM, a, b))
            return jnp.concatenate(pairs, axis=1)

        g = spread(0)
        beta = spread(GDN_HEADS)
        gcd = _dot_split_rhs(ltri_bd, jnp.concatenate([g, jnp.where(strict_t, g, 0.0)], axis=1), 2)
        yield
        gc = gcd[:, :W]
        decay = jnp.where(causal_t, jnp.exp(jnp.where(causal_t, gcd[:, W:], 0.0)), 0.0)
        egc = jnp.exp(gc)
        kb = kn * beta
        rv = yv * beta
        rk = kb * egc
        qd = qn * egc
        for cc in range(TILE // C):
            rs = slice(cc * C, (cc + 1) * C)
            g_last = gc[(cc + 1) * C - 1:(cc + 1) * C, :]
            kd_ref[slot, rs, :] = (kn[rs] * jnp.exp(g_last - gc[rs])).astype(BF16)
            gl_ref[slot, cc * SUBLANES:(cc + 1) * SUBLANES, :] = (
                jnp.broadcast_to(jnp.exp(g_last), (SUBLANES, W)))
        units = [(slice(cc * C, (cc + 1) * C), slice(LANES * p, LANES * (p + 1)), cc)
                 for cc in range(TILE // C) for p in range(npair)]
        kqs = [lax.dot_general(jnp.concatenate([kn[rs, ls], qn[rs, ls]], axis=0).astype(BF16),
                               stack(kn[rs, ls], first_head).astype(BF16), NT,
                               preferred_element_type=F32) for rs, ls, _ in units]
        yield
        ps = [-jnp.where(strict, kq[0:C] * beta[rs, ls] * decay[rs, ls], 0.0)
              for kq, (rs, ls, _) in zip(kqs, units)]
        ss = [eye2 + p for p in ps]
        ps = [dot(p.astype(BF16), stack(p, first_head).astype(BF16)) for p in ps]
        yield
        nround = int(math.log2(C))
        for k in range(1, nround):
            rhs = [stack(s_, first_head).astype(BF16) for s_ in ss]
            if k + 1 < nround:
                rhs = [jnp.concatenate([stack(p, first_head).astype(BF16), sx], axis=1)
                       for p, sx in zip(ps, rhs)]
            outs = [dot(p.astype(BF16), sx) for p, sx in zip(ps, rhs)]
            if k + 1 < nround:
                ps = [o[:, :LANES] for o in outs]
                ss = [s_ + o[:, LANES:] for s_, o in zip(ss, outs)]
            else:
                ss = [s_ + o for s_, o in zip(ss, outs)]
            yield
        xs = [dot(s_.astype(BF16),
                  stack(jnp.concatenate([rv[rs, ls], rk[rs, ls]], axis=1), first_head2).astype(BF16))
              for s_, (rs, ls, _) in zip(ss, units)]
        yield
        for x, kq, (rs, ls, cc) in zip(xs, kqs, units):
            u_ref[slot, rs, ls] = x[:, :LANES]
            wq_ref[slot, cc, 0:C, ls] = x[:, LANES:].astype(BF16)
            wq_ref[slot, cc, C:2 * C, ls] = qd[rs, ls].astype(BF16)
            a_ref[slot, rs, ls] = (kq[C:2 * C] * decay[rs, ls]).astype(BF16)

    lss = [slice(LANES * p, LANES * (p + 1)) for p in range(npair)]

    def recurrence_stages(slot, seq_start):
        states = [jnp.where(seq_start, 0.0, st_ref[p]) for p in range(npair)]
        pending = None

        def finish(rs, os_):
            for ls, o in zip(lss, os_):
                ms = dot((o * o).astype(BF16), pair_ones) * (1.0 / HEAD_DIM)
                out_ref[rs, ls] = o * lax.rsqrt(ms + EPS) * nw_ref[:, ls] * sz_ref[rs, ls]

        for cc in range(TILE // C):
            rs = slice(cc * C, (cc + 1) * C)
            wqs = [dot(wq_ref[slot, cc, :, ls], st.astype(BF16)) for ls, st in zip(lss, states)]
            if pending is not None:
                finish(*pending)
            yield
            v_news = [u_ref[slot, rs, ls] - wq[0:C] for ls, wq in zip(lss, wqs)]
            kvs = [lax.dot_general(kd_ref[slot, rs, ls], v.astype(BF16), TN, preferred_element_type=F32)
                   for ls, v in zip(lss, v_news)]
            os_ = [wq[C:2 * C] + dot(a_ref[slot, rs, ls], stack(v, first_head).astype(BF16))
                   for ls, wq, v in zip(lss, wqs, v_news)]
            states = [st * gl_ref[slot, cc * SUBLANES:cc * SUBLANES + 1, ls] + jnp.where(same_head, kv, 0.0)
                      for ls, st, kv in zip(lss, states, kvs)]
            pending = (rs, os_)
            yield
        finish(*pending)
        for p in range(npair):
            st_ref[p] = states[p]

    return solve_stages, recurrence_stages


def _interleave_weighted(*gens_and_weights):
    live = [[gen, weight] for gen, weight in gens_and_weights]
    while live:
        for entry in list(live):
            for _ in range(entry[1]):
                try:
                    next(entry[0])
                except StopIteration:
                    live.remove(entry)
                    break


def _mixers_kernel(relb_ref, qlo_ref, qhi_ref, kb_ref, ksum_ref, va_ref,
                   yq_ref, yk_ref, yv_ref, gb_ref, sz_ref, nw_ref,
                   oT_ref, og_ref,
                   bias_ref, addm_ref, qh_ref, lg_ref, moff_ref,
                   u_ref, wq_ref, a_ref, kd_ref, gl_ref, st_ref,
                   *, nblk, nbatch, tiles_per_seq, nsteps):
    step = pl.program_id(0)
    BLK = MOBA_BLOCK
    half = nblk // 2
    pair = jnp.minimum(step, nsteps - 2)
    hp = pair // (nbatch * (half // 2))
    b = (pair // (half // 2)) % nbatch
    scoring = step < nsteps - 1
    new_kv = (pair % (half // 2)) == 0

    solve_stages, recurrence_stages = _gdn_stages(
        yq_ref, yk_ref, yv_ref, gb_ref, sz_ref, nw_ref, og_ref,
        u_ref, wq_ref, a_ref, kd_ref, gl_ref, st_ref)
    seq_start = ((step - 1) % tiles_per_seq) == 0

    @pl.when((b == 0) & new_kv & scoring)
    def _():
        kk = lax.broadcasted_iota(jnp.int32, (BLK, BLK), 0)
        qq = lax.broadcasted_iota(jnp.int32, (BLK, BLK), 1)
        for hh in range(2):
            h = 2 * hp + hh
            for kind in range(2):
                d = qq - kk + kind * BLK
                val = jnp.full((BLK, BLK), relb_ref[h, REL_BUCKETS - 1], F32)
                for bkt in range(REL_BUCKETS - 2, -1, -1):
                    val = jnp.where(d < BUCKET_LOWER[bkt + 1], relb_ref[h, bkt], val)
                val = val * LOG2E
                if kind == 0:
                    val = jnp.where(d >= 0, val, NEG)
                bias_ref[hh, kind] = val

    def key_means():
        per_tile = INPROJ_TILE // BLK
        ks = ksum_ref[...]
        km = jnp.concatenate([ks[j // per_tile, j % per_tile:j % per_tile + 1, :] for j in range(nblk)],
                             axis=0) * (1.0 / BLK)
        lane = lax.broadcasted_iota(jnp.int32, (nblk, LANES), 1)
        return jnp.concatenate([jnp.where(lane < HEAD_DIM, km, 0.0),
                                jnp.where(lane >= HEAD_DIM, km, 0.0)], axis=0)

    def item_tiles(t):
        i_hi = nblk - 1 - t
        tiles = [(0, t, "own"), (1, i_hi, "own"), (1, i_hi - 1, "prev")]
        if t >= 1:
            tiles.append((0, t - 1, "prev"))
        tiles += [(0, j, "far") for j in range(t - 1)]
        tiles += [(1, j, "far") for j in range(i_hi - 1)]
        assert len(tiles) == nblk + 1
        return tiles

    def score_stages(parity):
        ridx = lax.broadcasted_iota(jnp.int32, (nblk, BLK), 0)
        sub = lax.broadcasted_iota(jnp.int32, (LANES, BLK), 0)
        scale = HEAD_DIM ** -0.5 * LOG2E
        km = key_means()
        for e in range(2):
            t = 2 * parity + e
            slot = 2 * parity + e
            q_of = ((qlo_ref, e, t), (qhi_ref, 1 - e, nblk - 1 - t))
            for s, (q_ref, w, qi) in enumerate(q_of):
                qT = q_ref[0, w]
                gT = jnp.dot(km, qT, precision=HI, preferred_element_type=F32)
                past = ridx < qi
                for hh in range(2):
                    gm = jnp.where(past, gT[nblk * hh:nblk * (hh + 1)], -jnp.inf)
                    cnt = jnp.zeros((nblk, BLK), F32)
                    for jp in range(nblk):
                        row = gm[jp:jp + 1, :]
                        beats = (row > gm) | ((row == gm) & (ridx > jp))
                        cnt = cnt + jnp.where(beats, 1.0, 0.0)
                    visible = past & (cnt < MOBA_TOPK)
                    addm_ref[e, s, nblk * hh:nblk * (hh + 1), :] = jnp.where(visible, 0.0, NEG)
                    in_head = (sub >= HEAD_DIM * hh) & (sub < HEAD_DIM * (hh + 1))
                    qh_ref[e, s, hh] = jnp.where(in_head, qT * scale, 0.0).astype(BF16)
            yield
            tiles = item_tiles(t)
            for hh in range(2):
                cmax = {0: [], 1: []}
                offs = []
                for n, (s, kblk, cls) in enumerate(tiles):
                    lg = jnp.dot(kb_ref[0, kblk * BLK:(kblk + 1) * BLK, :], qh_ref[e, s, hh],
                                 preferred_element_type=F32)
                    if cls != "far":
                        lg = lg + bias_ref[hh, 0 if cls == "own" else 1]
                    lg_ref[slot, hh, n] = lg
                    cm = jnp.max(lg, axis=0, keepdims=True)
                    off = None
                    if cls != "own":
                        off = addm_ref[e, s, nblk * hh + kblk:nblk * hh + kblk + 1, :]
                        if cls == "far":
                            off = off + relb_ref[2 * hp + hh, REL_BUCKETS - 1] * LOG2E
                        cm = cm + off
                    cmax[s].append(cm)
                    offs.append(off)
                    yield
                m = {s: functools.reduce(jnp.maximum, cmax[s]) for s in (0, 1)}
                for n, (s, _, _) in enumerate(tiles):
                    moff_ref[slot, hh, n:n + 1, :] = m[s] if offs[n] is None else m[s] - offs[n]

    def softmax_pv_stages(parity):
        for e in range(2):
            t = 2 * parity + e
            slot = 2 * parity + e
            tiles = item_tiles(t)
            for hh in range(2):
                acc = {0: None, 1: None}
                for n, (s, kblk, _) in enumerate(tiles):
                    p = jnp.exp2(lg_ref[slot, hh, n] - moff_ref[slot, hh, n:n + 1, :])
                    pvn = jnp.dot(va_ref[0, kblk, hh], p.astype(BF16),
                                  preferred_element_type=F32)
                    acc[s] = pvn if acc[s] is None else acc[s] + pvn
                    yield
                for s in (0, 1):
                    oT_ref[0, 2 * e + s, HEAD_DIM * hh:HEAD_DIM * (hh + 1), :] = (
                        acc[s][0:HEAD_DIM] / acc[s][HEAD_DIM:HEAD_DIM + 1])

    ATT, GDN = 2, 1

    @pl.when(step == 0)
    def _():
        st_ref[...] = jnp.zeros(st_ref.shape, F32)
        _interleave_weighted((score_stages(0), ATT), (solve_stages(0), GDN))

    for parity in range(2):
        @pl.when((step > 0) & (step < nsteps - 1) & (step % 2 == parity))
        def _(parity=parity):
            _interleave_weighted((recurrence_stages(1 - parity, seq_start), GDN),
                                 (score_stages(parity), ATT),
                                 (solve_stages(parity), GDN),
                                 (softmax_pv_stages(1 - parity), ATT))

    @pl.when(step == nsteps - 1)
    def _():
        last = (nsteps - 2) % 2
        _interleave_weighted((recurrence_stages(last, seq_start), GDN), (softmax_pv_stages(last), ATT))


def _token_mixers(rel_bias, qT, k3, ksum, va, gqkv, sz, gb, nw_row, B, S):
    T = B * S
    W = GDN_WIDTH
    TILE = GDN_TILE
    nblk = S // MOBA_BLOCK
    half = nblk // 2
    assert TILE == GDN_HALF and S % TILE == 0
    assert BUCKET_LOWER[REL_BUCKETS - 1] <= MOBA_BLOCK + 1
    assert nblk == 8 and nblk + 1 <= 2 * SUBLANES
    nchunk = TILE // GDN_CHUNK
    ntiles = T // TILE
    npairs = (ATT_HEADS // 2) * B * (half // 2)
    assert npairs == ntiles
    nsteps = ntiles + 1

    def scored(s):
        p = jnp.minimum(s, npairs - 1)
        return p // (B * (half // 2)), (p // (half // 2)) % B, p % (half // 2)

    def done(s):
        p = jnp.maximum(s - 1, 0)
        return p // (B * (half // 2)), (p // (half // 2)) % B, p % (half // 2)

    def q_lo(s):
        hp, b, m = scored(s)
        return (b, m, hp, 0)

    def q_hi(s):
        hp, b, m = scored(s)
        return (b, half - 1 - m, hp, 0)

    def k_blk(s):
        hp, b, _ = scored(s)
        return (b, 0, hp)

    def v_blk(s):
        hp, b, _ = done(s)
        return (b, 0, hp, 0, 0)

    def o_blk(s):
        hp, b, m = done(s)
        return (b, m, hp, 0)

    cur_tile = lambda off: (lambda s: (jnp.minimum(s, ntiles - 1), off))
    prev_tile = lambda s: (jnp.maximum(s - 1, 0), 0)
    return pl.pallas_call(
        functools.partial(_mixers_kernel, nblk=nblk, nbatch=B, tiles_per_seq=S // TILE, nsteps=nsteps),
        grid=(nsteps,),
        in_specs=[
            pl.BlockSpec(memory_space=pltpu.SMEM),
            pl.BlockSpec((1, 2, LANES, MOBA_BLOCK), q_lo),
            pl.BlockSpec((1, 2, LANES, MOBA_BLOCK), q_hi),
            pl.BlockSpec((1, S, LANES), k_blk),
            pl.BlockSpec((S // INPROJ_TILE, SUBLANES, LANES), k_blk),
            pl.BlockSpec((1, nblk, 2, V_ROWS, MOBA_BLOCK), v_blk),
            pl.BlockSpec((TILE, W), cur_tile(0)),
            pl.BlockSpec((TILE, W), cur_tile(1)),
            pl.BlockSpec((TILE, W), cur_tile(2)),
            pl.BlockSpec((TILE, LANES), cur_tile(0)),
            pl.BlockSpec((TILE, W), prev_tile),
            pl.BlockSpec((1, W), lambda s: (0, 0)),
        ],
        out_specs=[
            pl.BlockSpec((1, 4, LANES, MOBA_BLOCK), o_blk),
            pl.BlockSpec((TILE, W), prev_tile),
        ],
        out_shape=[
            jax.ShapeDtypeStruct((B, nblk, ATT_WIDTH, MOBA_BLOCK), F32),
            jax.ShapeDtypeStruct((T, GDN_WIDTH), F32),
        ],
        scratch_shapes=[
            pltpu.VMEM((2, 2, MOBA_BLOCK, MOBA_BLOCK), F32),
            pltpu.VMEM((2, 2, 2 * nblk, MOBA_BLOCK), F32),
            pltpu.VMEM((2, 2, 2, LANES, MOBA_BLOCK), BF16),
            pltpu.VMEM((4, 2, nblk + 1, MOBA_BLOCK, MOBA_BLOCK), F32),
            pltpu.VMEM((4, 2, 2 * SUBLANES, MOBA_BLOCK), F32),
            pltpu.VMEM((2, TILE, W), F32),
            pltpu.VMEM((2, nchunk, 2 * GDN_CHUNK, W), BF16),
            pltpu.VMEM((2, TILE, W), BF16),
            pltpu.VMEM((2, TILE, W), BF16),
            pltpu.VMEM((2, nchunk * SUBLANES, W), F32),
            pltpu.VMEM((W // LANES, LANES, LANES), F32),
        ],
        compiler_params=pltpu.CompilerParams(
            dimension_semantics=("arbitrary",), vmem_limit_bytes=VMEM_LIMIT),
        name="token_mixers",
    )(rel_bias, qT, qT, k3, ksum, va, gqkv, gqkv, gqkv, gb, sz, nw_row)


def _out_mlp_kernel(x_ref, oTa_ref, oTb_ref, og_ref, woa_ref, wog_ref, pmn_ref, pre_ref, post_ref,
                    wup_ref, wdn_ref, out_ref):
    oT = jnp.concatenate([oTa_ref[0, 0], oTb_ref[0, 0]], axis=1)
    o_att = oT.T.astype(BF16)
    mix = jnp.dot(o_att, woa_ref[...], preferred_element_type=F32)
    mix = mix + jnp.dot(og_ref[...].astype(BF16), wog_ref[...], preferred_element_type=F32)
    x1 = x_ref[...] + _rms(mix, pmn_ref[...])
    h = _rms(x1, pre_ref[...]).astype(BF16)
    acc = jnp.zeros((ROW_TILE, D_MODEL), F32)
    for c in range(D_FF // FF_TILE):
        up = jnp.dot(h, wup_ref[:, c * FF_TILE:(c + 1) * FF_TILE], preferred_element_type=F32)
        act = jnp.square(jnp.maximum(up, 0.0)).astype(BF16)
        acc = acc + jnp.dot(act, wdn_ref[c * FF_TILE:(c + 1) * FF_TILE, :], preferred_element_type=F32)
    out_ref[...] = x1 + _rms(acc, post_ref[...])


def _out_mlp(xf, oT, og, woa, wog, pmn, pre, post, wup, wdn, B, S):
    T = B * S
    nblk = S // MOBA_BLOCK
    tiles_per_seq = S // ROW_TILE
    assert ROW_TILE == 2 * MOBA_BLOCK
    const = lambda i: (0, 0)
    row = lambda i: (i, 0)

    def att_block(which):
        def index(i):
            blk = 2 * (i % tiles_per_seq) + which
            return (i // tiles_per_seq, _paired_pos(blk, nblk), 0, 0)
        return index

    single = dict(pipeline_mode=pl.Buffered(1))
    return pl.pallas_call(
        _out_mlp_kernel,
        grid=(T // ROW_TILE,),
        in_specs=[
            pl.BlockSpec((ROW_TILE, D_MODEL), row),
            pl.BlockSpec((1, 1, ATT_WIDTH, MOBA_BLOCK), att_block(0)),
            pl.BlockSpec((1, 1, ATT_WIDTH, MOBA_BLOCK), att_block(1)),
            pl.BlockSpec((ROW_TILE, GDN_WIDTH), row),
            pl.BlockSpec(woa.shape, const, **single),
            pl.BlockSpec(wog.shape, const, **single),
            pl.BlockSpec((1, D_MODEL), const),
            pl.BlockSpec((1, D_MODEL), const),
            pl.BlockSpec((1, D_MODEL), const),
            pl.BlockSpec(wup.shape, const, **single),
            pl.BlockSpec(wdn.shape, const, **single),
        ],
        out_specs=pl.BlockSpec((ROW_TILE, D_MODEL), row),
        out_shape=jax.ShapeDtypeStruct((T, D_MODEL), F32),
        compiler_params=pltpu.CompilerParams(
            dimension_semantics=("arbitrary",), vmem_limit_bytes=VMEM_LIMIT),
        name="out_mlp",
    )(xf, oT, oT, og, woa, wog, pmn, pre, post, wup, wdn)


def kernel(x, w_in, w_out, conv_w, A_log, dt_bias, gdn_norm_w, rel_bias, pre_mix_norm,
           post_mix_norm, pre_mlp_norm, post_mlp_norm, w_up, w_down):
    B, S, D = x.shape
    assert D == D_MODEL and S % ROW_TILE == 0 and S % MOBA_BLOCK == 0
    T = B * S
    depth = w_in.shape[0]
    xf = x.reshape(T, D)
    o0, o1, o2, o3, o4 = 0, ATT_WIDTH, 2 * ATT_WIDTH, 3 * ATT_WIDTH, 3 * ATT_WIDTH + 3 * GDN_WIDTH
    o5 = o4 + GDN_WIDTH
    for l in range(depth):
        wi = w_in[l]
        wqT = wi[:, o0:o1].T.astype(BF16)
        wk = wi[:, o1:o2].astype(BF16)
        wvT = wi[:, o2:o3].T.astype(BF16)
        wg = wi[:, o3:o4].astype(BF16)
        wz = wi[:, o4:o5].astype(BF16)
        wab = jnp.pad(wi[:, o5:], ((0, 0), (0, LANES - 2 * GDN_HEADS))).astype(BF16)
        pad8 = lambda v: jnp.pad(v.astype(F32), (0, LANES - GDN_HEADS))[None, :]
        qT, k, ksum, va, gqkv, sz, gb = _inproj(xf, pre_mix_norm[l][None, :], wqT, wk, wvT, wg, wz, wab,
                                                conv_w[l], pad8(A_log[l]), pad8(dt_bias[l]), B, S)
        oT, og = _token_mixers(rel_bias.astype(F32), qT, k.reshape(B, S, ATT_WIDTH), ksum, va, gqkv, sz,
                               gb, jnp.tile(gdn_norm_w[l], GDN_HEADS)[None, :], B, S)
        wo = w_out[l].astype(BF16)
        xf = _out_mlp(xf, oT, og, wo[:ATT_WIDTH], wo[ATT_WIDTH:], post_mix_norm[l][None, :],
                      pre_mlp_norm[l][None, :], post_mlp_norm[l][None, :],
                      w_up[l].astype(BF16), w_down[l].astype(BF16), B, S)
    return xf.reshape(B, S, D)
```

```python
import functools
import math

import jax
import jax.numpy as jnp
from jax import lax
from jax.experimental import pallas as pl
from jax.experimental.pallas import tpu as pltpu

F32 = jnp.float32
BF16 = jnp.bfloat16
HI = lax.Precision.HIGHEST

D_MODEL = 1024
HEAD_DIM = 64
ATT_HEADS = 8
GDN_HEADS = 8
ATT_WIDTH = ATT_HEADS * HEAD_DIM
GDN_WIDTH = GDN_HEADS * HEAD_DIM
MOBA_BLOCK = 256
MOBA_TOPK = 3
GDN_CHUNK = 64
CONV_WIDTH = 4
D_FF = 4 * D_MODEL
REL_BUCKETS = 32
REL_MAX_EXACT = 16
REL_MAX_DIST = 128
EPS = 1e-6
NEG = -1e30
LOG2E = math.log2(math.e)

LANES = 128
SUBLANES = 8
VMEM_LIMIT = 56 * 1024 * 1024
ROW_TILE = 512
INPROJ_TILE = 512
FF_TILE = 1024

NT = (((1,), (1,)), ((), ()))
TN = (((0,), (0,)), ((), ()))


def _bucket_lower_bounds():
    def bucket(d):
        if d < REL_MAX_EXACT:
            return d
        t = math.log(d / REL_MAX_EXACT) / math.log(REL_MAX_DIST / REL_MAX_EXACT)
        t = t * (REL_BUCKETS - REL_MAX_EXACT)
        assert d in (REL_MAX_EXACT, REL_MAX_DIST) or abs(t - round(t)) > 1e-6
        return min(REL_MAX_EXACT + int(t + 1e-9), REL_BUCKETS - 1)
    lower = []
    for b in range(REL_BUCKETS):
        d = 0
        while bucket(d) < b:
            d += 1
        lower.append(d)
    return lower


BUCKET_LOWER = _bucket_lower_bounds()


def _sigmoid(x):
    return 0.5 * jnp.tanh(0.5 * x) + 0.5


def _silu_of_half(h):
    return h + h * jnp.tanh(h)


def _rms(x, w):
    return x * lax.rsqrt(jnp.mean(x * x, axis=-1, keepdims=True) + EPS) * w


def _split_bf16(x, parts):
    out = []
    for _ in range(parts):
        h = x.astype(BF16)
        out.append(h)
        x = x - h.astype(F32)
    return out


def _dot_split_rhs(c, x, parts):
    acc = None
    for h in _split_bf16(x, parts):
        d = jnp.dot(c, h, preferred_element_type=F32)
        acc = d if acc is None else acc + d
    return acc


CONV_COLS = 512


def _inproj_kernel(x_ref, xp_ref, nw_ref, wqT_ref, wk_ref, wvT_ref, wg_ref, wz_ref, wab_ref,
                   cw_ref, alog_ref, dtb_ref,
                   qT_ref, k_ref, ksum_ref, va_ref, g_ref, z_ref, gb_ref, *, tiles_per_seq):
    h = _rms(x_ref[...], nw_ref[...]).astype(BF16)

    hp = _rms(xp_ref[...], nw_ref[...]).astype(BF16)
    seq_start = (pl.program_id(0) % tiles_per_seq) == 0
    trow8 = lax.broadcasted_iota(jnp.int32, (SUBLANES, CONV_COLS), 0)
    for c in range(3 * GDN_WIDTH // CONV_COLS):
        cols = slice(c * CONV_COLS, (c + 1) * CONV_COLS)
        cur = jnp.dot(h, wg_ref[:, cols], preferred_element_type=F32)
        prev8 = jnp.dot(hp, wg_ref[:, cols], preferred_element_type=F32)
        prev8 = jnp.where(seq_start, 0.0, prev8)
        cw_half = 0.5 * cw_ref[:, cols]
        acc = cur * cw_half[CONV_WIDTH - 1:CONV_WIDTH]
        for s in range(1, CONV_WIDTH):
            rolled = pltpu.roll(cur, s, 0)
            top = jnp.where(trow8 < s, pltpu.roll(prev8, s, 0), rolled[0:SUBLANES])
            tap = jnp.concatenate([top, rolled[SUBLANES:]], axis=0)
            acc = acc + tap * cw_half[CONV_WIDTH - 1 - s:CONV_WIDTH - s]
        g_ref[:, cols] = _silu_of_half(acc)

    z = jnp.dot(h, wz_ref[...], preferred_element_type=F32)
    z_ref[...] = _silu_of_half(0.5 * z)
    ab = jnp.dot(h, wab_ref[...], preferred_element_type=F32)
    xs = ab + dtb_ref[...]
    log_decay = -jnp.exp(alog_ref[...]) * (jnp.maximum(xs, 0.0) + jnp.log1p(jnp.exp(-jnp.abs(xs))))
    lane = lax.broadcasted_iota(jnp.int32, ab.shape, 1)
    gb_ref[...] = jnp.where(lane < GDN_HEADS, log_decay, _sigmoid(ab))

    qT = lax.dot_general(wqT_ref[...], h, NT, preferred_element_type=F32)
    vT = lax.dot_general(wvT_ref[...], h, NT, preferred_element_type=F32)
    k = jnp.dot(h, wk_ref[...], preferred_element_type=F32)
    k_ref[...] = k.astype(BF16)
    ones_row = jnp.where(lax.broadcasted_iota(jnp.int32, (V_ROWS - HEAD_DIM, MOBA_BLOCK), 0) == 0,
                         1.0, 0.0).astype(BF16)
    ksum_ref[...] = jnp.zeros(ksum_ref.shape, F32)
    for t in range(INPROJ_TILE // MOBA_BLOCK):
        blk = slice(t * MOBA_BLOCK, (t + 1) * MOBA_BLOCK)
        qT_ref[0, t] = qT[:, blk]
        ksum_ref[0, t:t + 1, :] = jnp.sum(k[blk], axis=0, keepdims=True)
        for hh in range(ATT_HEADS):
            va_ref[0, t, hh, 0:HEAD_DIM, :] = vT[HEAD_DIM * hh:HEAD_DIM * (hh + 1), blk].astype(BF16)
            va_ref[0, t, hh, HEAD_DIM:V_ROWS, :] = ones_row


def _inproj(xf, nw, wqT, wk, wvT, wg, wz, wab, conv_w, alog_pad, dtb_pad, B, S):
    T = B * S
    TM = INPROJ_TILE
    assert S % TM == 0
    nblk = S // MOBA_BLOCK
    tiles_per_seq = S // TM
    blk_per_tile = TM // MOBA_BLOCK
    const = lambda i: (0, 0)
    row = lambda i: (i, 0)
    tr = lambda i: (i // tiles_per_seq, i % tiles_per_seq, 0, 0)
    prev_rows = lambda i: (jnp.maximum(i * (TM // SUBLANES) - 1, 0), 0)
    single = dict(pipeline_mode=pl.Buffered(1))
    return pl.pallas_call(
        functools.partial(_inproj_kernel, tiles_per_seq=tiles_per_seq),
        grid=(T // TM,),
        in_specs=[
            pl.BlockSpec((TM, D_MODEL), row),
            pl.BlockSpec((SUBLANES, D_MODEL), prev_rows),
            pl.BlockSpec((1, D_MODEL), const),
            pl.BlockSpec(wqT.shape, const, **single),
            pl.BlockSpec(wk.shape, const, **single),
            pl.BlockSpec(wvT.shape, const, **single),
            pl.BlockSpec(wg.shape, const, **single),
            pl.BlockSpec(wz.shape, const, **single),
            pl.BlockSpec(wab.shape, const, **single),
            pl.BlockSpec(conv_w.shape, const),
            pl.BlockSpec((1, LANES), const),
            pl.BlockSpec((1, LANES), const),
        ],
        out_specs=[
            pl.BlockSpec((1, blk_per_tile, ATT_WIDTH, MOBA_BLOCK), tr),
            pl.BlockSpec((TM, ATT_WIDTH), row),
            pl.BlockSpec((1, SUBLANES, ATT_WIDTH), lambda i: (i, 0, 0)),
            pl.BlockSpec((1, blk_per_tile, ATT_HEADS, V_ROWS, MOBA_BLOCK),
                         lambda i: (i // tiles_per_seq, i % tiles_per_seq, 0, 0, 0)),
            pl.BlockSpec((TM, 3 * GDN_WIDTH), row),
            pl.BlockSpec((TM, GDN_WIDTH), row),
            pl.BlockSpec((TM, LANES), row),
        ],
        out_shape=[
            jax.ShapeDtypeStruct((B, nblk, ATT_WIDTH, MOBA_BLOCK), F32),
            jax.ShapeDtypeStruct((T, ATT_WIDTH), BF16),
            jax.ShapeDtypeStruct((T // TM, SUBLANES, ATT_WIDTH), F32),
            jax.ShapeDtypeStruct((B, nblk, ATT_HEADS, V_ROWS, MOBA_BLOCK), BF16),
            jax.ShapeDtypeStruct((T, 3 * GDN_WIDTH), F32),
            jax.ShapeDtypeStruct((T, GDN_WIDTH), F32),
            jax.ShapeDtypeStruct((T, LANES), F32),
        ],
        compiler_params=pltpu.CompilerParams(
            dimension_semantics=("arbitrary",), vmem_limit_bytes=VMEM_LIMIT),
        name="inproj",
    )(xf, xf, nw, wqT, wk, wvT, wg, wz, wab, conv_w, alog_pad, dtb_pad)


V_ROWS = HEAD_DIM + 16


def _paired_pos(i, nblk):
    return jnp.where(i < nblk // 2, 2 * i, 2 * (nblk - 1 - i) + 1)


GDN_TILE = 256
GDN_HALF = 2 * LANES


def _gdn_stages(yq_ref, yk_ref, yv_ref, gb_ref, sz_ref, nw_ref, out_ref,
                u_ref, wq_ref, a_ref, kd_ref, gl_ref, st_ref):
    C = GDN_CHUNK
    W = GDN_WIDTH
    TILE = GDN_TILE
    npair = W // LANES

    r_w = lax.broadcasted_iota(jnp.int32, (GDN_HALF, GDN_HALF), 0)
    c_w = lax.broadcasted_iota(jnp.int32, (GDN_HALF, GDN_HALF), 1)
    head_ones = jnp.where((r_w // HEAD_DIM) == (c_w // HEAD_DIM), 1.0, 0.0).astype(BF16)
    ltri_bd = jnp.where(((r_w // C) == (c_w // C)) & (c_w <= r_w), 1.0, 0.0).astype(BF16)
    tok = lax.broadcasted_iota(jnp.int32, (TILE, W), 0) % C
    col = lax.broadcasted_iota(jnp.int32, (TILE, W), 1) % HEAD_DIM
    causal_t = tok >= col
    strict_t = tok > col
    lane_t = lax.broadcasted_iota(jnp.int32, (TILE, LANES), 1)

    lane = lax.broadcasted_iota(jnp.int32, (C, LANES), 1)
    rowi = lax.broadcasted_iota(jnp.int32, (C, LANES), 0)
    first_head = lane < HEAD_DIM
    strict = rowi > (lane % HEAD_DIM)
    eye2 = jnp.where(rowi == (lane % HEAD_DIM), 1.0, 0.0)
    lane2 = lax.broadcasted_iota(jnp.int32, (C, 2 * LANES), 1)
    first_head2 = (lane2 % LANES) < HEAD_DIM
    r_l = lax.broadcasted_iota(jnp.int32, (LANES, LANES), 0)
    c_l = lax.broadcasted_iota(jnp.int32, (LANES, LANES), 1)
    same_head = (r_l // HEAD_DIM) == (c_l // HEAD_DIM)
    pair_ones = jnp.where(same_head, 1.0, 0.0).astype(BF16)

    def stack(x, mask):
        return jnp.concatenate([jnp.where(mask, x, 0.0), jnp.where(mask, 0.0, x)], axis=0)

    dot = functools.partial(jnp.dot, preferred_element_type=F32)

    def head_sumsq(y):
        y2 = (y * y).astype(BF16)
        return jnp.concatenate([dot(y2[:, h:h + GDN_HALF], head_ones) for h in range(0, W, GDN_HALF)],
                               axis=1)

    def solve_stages(slot):
        yq = yq_ref[...]
        yk = yk_ref[...]
        yv = yv_ref[...]
        qn = yq * lax.rsqrt(head_sumsq(yq) + EPS) * (HEAD_DIM ** -0.5)
        kn = yk * lax.rsqrt(head_sumsq(yk) + EPS)
        yield
        gbt = gb_ref[...]

        def spread(col0):
            pairs = []
            for p in range(npair):
                a = jnp.broadcast_to(gbt[:, col0 + 2 * p:col0 + 2 * p + 1], (TILE, LANES))
                b = jnp.broadcast_to(gbt[:, col0 + 2 * p + 1:col0 + 2 * p + 2], (TILE, LANES))
                pairs.append(jnp.where(lane_t < HEAD_DIM, a, b))
            return jnp.concatenate(pairs, axis=1)

        g = spread(0)
        beta = spread(GDN_HEADS)
        gcd = _dot_split_rhs(ltri_bd, jnp.concatenate([g, jnp.where(strict_t, g, 0.0)], axis=1), 2)
        yield
        gc = gcd[:, :W]
        decay = jnp.where(causal_t, jnp.exp(jnp.where(causal_t, gcd[:, W:], 0.0)), 0.0)
        egc = jnp.exp(gc)
        kb = kn * beta
        rv = yv * beta
        rk = kb * egc
        qd = qn * egc
        for cc in range(TILE // C):
            rs = slice(cc * C, (cc + 1) * C)
            g_last = gc[(cc + 1) * C - 1:(cc + 1) * C, :]
            kd_ref[slot, rs, :] = (kn[rs] * jnp.exp(g_last - gc[rs])).astype(BF16)
            gl_ref[slot, cc * SUBLANES:(cc + 1) * SUBLANES, :] = (
                jnp.broadcast_to(jnp.exp(g_last), (SUBLANES, W)))
        units = [(slice(cc * C, (cc + 1) * C), slice(LANES * p, LANES * (p + 1)), cc)
                 for cc in range(TILE // C) for p in range(npair)]
        def in_two_stages(fn, args):
            out = []
            half_n = len(args) // 2
            for i, a in enumerate(args):
                if i == half_n:
                    yield
                out.append(fn(*a))
            yield
            return out

        kqs = yield from in_two_stages(
            lambda rs, ls, _: lax.dot_general(
                jnp.concatenate([kn[rs, ls], qn[rs, ls]], axis=0).astype(BF16),
                stack(kn[rs, ls], first_head).astype(BF16), NT, preferred_element_type=F32),
            units)
        ps = [-jnp.where(strict, kq[0:C] * beta[rs, ls] * decay[rs, ls], 0.0)
              for kq, (rs, ls, _) in zip(kqs, units)]
        ss = [eye2 + p for p in ps]
        ps = yield from in_two_stages(
            lambda p: dot(p.astype(BF16), stack(p, first_head).astype(BF16)), [(p,) for p in ps])
        nround = int(math.log2(C))
        for k in range(1, nround):
            last = k + 1 == nround

            def advance(p, s_, last=last):
                sx = stack(s_, first_head).astype(BF16)
                if not last:
                    sx = jnp.concatenate([stack(p, first_head).astype(BF16), sx], axis=1)
                return dot(p.astype(BF16), sx)

            outs = yield from in_two_stages(advance, list(zip(ps, ss)))
            if not last:
                ps = [o[:, :LANES] for o in outs]
                ss = [s_ + o[:, LANES:] for s_, o in zip(ss, outs)]
            else:
                ss = [s_ + o for s_, o in zip(ss, outs)]
        xs = yield from in_two_stages(
            lambda s_, rs, ls: dot(s_.astype(BF16), stack(jnp.concatenate([rv[rs, ls], rk[rs, ls]], axis=1),
                                                          first_head2).astype(BF16)),
            [(s_, rs, ls) for s_, (rs, ls, _) in zip(ss, units)])
        for x, kq, (rs, ls, cc) in zip(xs, kqs, units):
            u_ref[slot, rs, ls] = x[:, :LANES]
            wq_ref[slot, cc, 0:C, ls] = x[:, LANES:].astype(BF16)
            wq_ref[slot, cc, C:2 * C, ls] = qd[rs, ls].astype(BF16)
            a_ref[slot, rs, ls] = (kq[C:2 * C] * decay[rs, ls]).astype(BF16)

    lss = [slice(LANES * p, LANES * (p + 1)) for p in range(npair)]

    def recurrence_stages(slot, seq_start):
        states = [jnp.where(seq_start, 0.0, st_ref[p]) for p in range(npair)]
        pending = None

        def finish(rs, os_):
            for ls, o in zip(lss, os_):
                ms = dot((o * o).astype(BF16), pair_ones) * (1.0 / HEAD_DIM)
                out_ref[rs, ls] = o * lax.rsqrt(ms + EPS) * nw_ref[:, ls] * sz_ref[rs, ls]

        for cc in range(TILE // C):
            rs = slice(cc * C, (cc + 1) * C)
            wqs = [dot(wq_ref[slot, cc, :, ls], st.astype(BF16)) for ls, st in zip(lss, states)]
            if pending is not None:
                finish(*pending)
            yield
            v_news = [u_ref[slot, rs, ls] - wq[0:C] for ls, wq in zip(lss, wqs)]
            kvs = [lax.dot_general(kd_ref[slot, rs, ls], v.astype(BF16), TN, preferred_element_type=F32)
                   for ls, v in zip(lss, v_news)]
            os_ = [wq[C:2 * C] + dot(a_ref[slot, rs, ls], stack(v, first_head).astype(BF16))
                   for ls, wq, v in zip(lss, wqs, v_news)]
            states = [st * gl_ref[slot, cc * SUBLANES:cc * SUBLANES + 1, ls] + jnp.where(same_head, kv, 0.0)
                      for ls, st, kv in zip(lss, states, kvs)]
            pending = (rs, os_)
            yield
        finish(*pending)
        for p in range(npair):
            st_ref[p] = states[p]

    return solve_stages, recurrence_stages


def _interleave_weighted(*gens_and_weights):
    live = [[gen, weight] for gen, weight in gens_and_weights]
    while live:
        for entry in list(live):
            for _ in range(entry[1]):
                try:
                    next(entry[0])
                except StopIteration:
                    live.remove(entry)
                    break


def _mixers_kernel(relb_ref, qlo_ref, qhi_ref, kb_ref, ksum_ref, va_ref,
                   yq_ref, yk_ref, yv_ref, gb_ref, sz_ref, nw_ref,
                   oT_ref, og_ref,
                   bias_ref, addm_ref, qh_ref, lg_ref, moff_ref,
                   u_ref, wq_ref, a_ref, kd_ref, gl_ref, st_ref,
                   *, nblk, nbatch, tiles_per_seq, nsteps):
    step = pl.program_id(0)
    BLK = MOBA_BLOCK
    half = nblk // 2
    pair = jnp.minimum(step, nsteps - 2)
    hp = pair // (nbatch * (half // 2))
    b = (pair // (half // 2)) % nbatch
    scoring = step < nsteps - 1
    new_kv = (pair % (half // 2)) == 0

    solve_stages, recurrence_stages = _gdn_stages(
        yq_ref, yk_ref, yv_ref, gb_ref, sz_ref, nw_ref, og_ref,
        u_ref, wq_ref, a_ref, kd_ref, gl_ref, st_ref)
    seq_start = ((step - 1) % tiles_per_seq) == 0

    @pl.when((b == 0) & new_kv & scoring)
    def _():
        kk = lax.broadcasted_iota(jnp.int32, (BLK, BLK), 0)
        qq = lax.broadcasted_iota(jnp.int32, (BLK, BLK), 1)
        for hh in range(2):
            h = 2 * hp + hh
            for kind in range(2):
                d = qq - kk + kind * BLK
                val = jnp.full((BLK, BLK), relb_ref[h, REL_BUCKETS - 1], F32)
                for bkt in range(REL_BUCKETS - 2, -1, -1):
                    val = jnp.where(d < BUCKET_LOWER[bkt + 1], relb_ref[h, bkt], val)
                val = val * LOG2E
                if kind == 0:
                    val = jnp.where(d >= 0, val, NEG)
                bias_ref[hh, kind] = val

    def key_means():
        per_tile = INPROJ_TILE // BLK
        ks = ksum_ref[...]
        km = jnp.concatenate([ks[j // per_tile, j % per_tile:j % per_tile + 1, :] for j in range(nblk)],
                             axis=0) * (1.0 / BLK)
        lane = lax.broadcasted_iota(jnp.int32, (nblk, LANES), 1)
        return jnp.concatenate([jnp.where(lane < HEAD_DIM, km, 0.0),
                                jnp.where(lane >= HEAD_DIM, km, 0.0)], axis=0)

    def item_tiles(t):
        i_hi = nblk - 1 - t
        tiles = [(0, t, "own"), (1, i_hi, "own"), (1, i_hi - 1, "prev")]
        if t >= 1:
            tiles.append((0, t - 1, "prev"))
        tiles += [(0, j, "far") for j in range(t - 1)]
        tiles += [(1, j, "far") for j in range(i_hi - 1)]
        assert len(tiles) == nblk + 1
        return tiles

    def score_stages(parity):
        ridx = lax.broadcasted_iota(jnp.int32, (nblk, BLK), 0)
        sub = lax.broadcasted_iota(jnp.int32, (LANES, BLK), 0)
        scale = HEAD_DIM ** -0.5 * LOG2E
        km = key_means()
        for e in range(2):
            t = 2 * parity + e
            slot = 2 * parity + e
            q_of = ((qlo_ref, e, t), (qhi_ref, 1 - e, nblk - 1 - t))
            for s, (q_ref, w, qi) in enumerate(q_of):
                qT = q_ref[0, w]
                gT = jnp.dot(km, qT, precision=HI, preferred_element_type=F32)
                past = ridx < qi
                for hh in range(2):
                    gm = jnp.where(past, gT[nblk * hh:nblk * (hh + 1)], -jnp.inf)
                    cnt = jnp.zeros((nblk, BLK), F32)
                    for jp in range(nblk):
                        row = gm[jp:jp + 1, :]
                        beats = (row > gm) | ((row == gm) & (ridx > jp))
                        cnt = cnt + jnp.where(beats, 1.0, 0.0)
                    visible = past & (cnt < MOBA_TOPK)
                    addm_ref[e, s, nblk * hh:nblk * (hh + 1), :] = jnp.where(visible, 0.0, NEG)
                    in_head = (sub >= HEAD_DIM * hh) & (sub < HEAD_DIM * (hh + 1))
                    qh_ref[e, s, hh] = jnp.where(in_head, qT * scale, 0.0).astype(BF16)
            yield
            tiles = item_tiles(t)
            for hh in range(2):
                cmax = {0: [], 1: []}
                offs = []
                for n, (s, kblk, cls) in enumerate(tiles):
                    lg = jnp.dot(kb_ref[0, kblk * BLK:(kblk + 1) * BLK, :], qh_ref[e, s, hh],
                                 preferred_element_type=F32)
                    if cls != "far":
                        lg = lg + bias_ref[hh, 0 if cls == "own" else 1]
                    lg_ref[slot, hh, n] = lg
                    cm = jnp.max(lg, axis=0, keepdims=True)
                    off = None
                    if cls != "own":
                        off = addm_ref[e, s, nblk * hh + kblk:nblk * hh + kblk + 1, :]
                        if cls == "far":
                            off = off + relb_ref[2 * hp + hh, REL_BUCKETS - 1] * LOG2E
                        cm = cm + off
                    cmax[s].append(cm)
                    offs.append(off)
                    yield
                m = {s: functools.reduce(jnp.maximum, cmax[s]) for s in (0, 1)}
                for n, (s, _, _) in enumerate(tiles):
                    moff_ref[slot, hh, n:n + 1, :] = m[s] if offs[n] is None else m[s] - offs[n]

    def softmax_pv_stages(parity):
        for e in range(2):
            t = 2 * parity + e
            slot = 2 * parity + e
            tiles = item_tiles(t)
            for hh in range(2):
                acc = {0: None, 1: None}
                for n, (s, kblk, _) in enumerate(tiles):
                    p = jnp.exp2(lg_ref[slot, hh, n] - moff_ref[slot, hh, n:n + 1, :])
                    pvn = jnp.dot(va_ref[0, kblk, hh], p.astype(BF16),
                                  preferred_element_type=F32)
                    acc[s] = pvn if acc[s] is None else acc[s] + pvn
                    yield
                for s in (0, 1):
                    oT_ref[0, 2 * e + s, HEAD_DIM * hh:HEAD_DIM * (hh + 1), :] = (
                        acc[s][0:HEAD_DIM] / acc[s][HEAD_DIM:HEAD_DIM + 1])

    ATT, GDN = 2, 1

    @pl.when(step == 0)
    def _():
        st_ref[...] = jnp.zeros(st_ref.shape, F32)
        _interleave_weighted((score_stages(0), ATT), (solve_stages(0), GDN))

    for parity in range(2):
        @pl.when((step > 0) & (step < nsteps - 1) & (step % 2 == parity))
        def _(parity=parity):
            _interleave_weighted((recurrence_stages(1 - parity, seq_start), GDN),
                                 (score_stages(parity), ATT),
                                 (solve_stages(parity), GDN),
                                 (softmax_pv_stages(1 - parity), ATT))

    @pl.when(step == nsteps - 1)
    def _():
        last = (nsteps - 2) % 2
        _interleave_weighted((recurrence_stages(last, seq_start), GDN), (softmax_pv_stages(last), ATT))


def _token_mixers(rel_bias, qT, k3, ksum, va, gqkv, sz, gb, nw_row, B, S):
    T = B * S
    W = GDN_WIDTH
    TILE = GDN_TILE
    nblk = S // MOBA_BLOCK
    half = nblk // 2
    assert TILE == GDN_HALF and S % TILE == 0
    assert BUCKET_LOWER[REL_BUCKETS - 1] <= MOBA_BLOCK + 1
    assert nblk == 8 and nblk + 1 <= 2 * SUBLANES
    nchunk = TILE // GDN_CHUNK
    ntiles = T // TILE
    npairs = (ATT_HEADS // 2) * B * (half // 2)
    assert npairs == ntiles
    nsteps = ntiles + 1

    def scored(s):
        p = jnp.minimum(s, npairs - 1)
        return p // (B * (half // 2)), (p // (half // 2)) % B, p % (half // 2)

    def done(s):
        p = jnp.maximum(s - 1, 0)
        return p // (B * (half // 2)), (p // (half // 2)) % B, p % (half // 2)

    def q_lo(s):
        hp, b, m = scored(s)
        return (b, m, hp, 0)

    def q_hi(s):
        hp, b, m = scored(s)
        return (b, half - 1 - m, hp, 0)

    def k_blk(s):
        hp, b, _ = scored(s)
        return (b, 0, hp)

    def v_blk(s):
        hp, b, _ = done(s)
        return (b, 0, hp, 0, 0)

    def o_blk(s):
        hp, b, m = done(s)
        return (b, m, hp, 0)

    cur_tile = lambda off: (lambda s: (jnp.minimum(s, ntiles - 1), off))
    prev_tile = lambda s: (jnp.maximum(s - 1, 0), 0)
    return pl.pallas_call(
        functools.partial(_mixers_kernel, nblk=nblk, nbatch=B, tiles_per_seq=S // TILE, nsteps=nsteps),
        grid=(nsteps,),
        in_specs=[
            pl.BlockSpec(memory_space=pltpu.SMEM),
            pl.BlockSpec((1, 2, LANES, MOBA_BLOCK), q_lo),
            pl.BlockSpec((1, 2, LANES, MOBA_BLOCK), q_hi),
            pl.BlockSpec((1, S, LANES), k_blk),
            pl.BlockSpec((S // INPROJ_TILE, SUBLANES, LANES), k_blk),
            pl.BlockSpec((1, nblk, 2, V_ROWS, MOBA_BLOCK), v_blk),
            pl.BlockSpec((TILE, W), cur_tile(0)),
            pl.BlockSpec((TILE, W), cur_tile(1)),
            pl.BlockSpec((TILE, W), cur_tile(2)),
            pl.BlockSpec((TILE, LANES), cur_tile(0)),
            pl.BlockSpec((TILE, W), prev_tile),
            pl.BlockSpec((1, W), lambda s: (0, 0)),
        ],
        out_specs=[
            pl.BlockSpec((1, 4, LANES, MOBA_BLOCK), o_blk),
            pl.BlockSpec((TILE, W), prev_tile),
        ],
        out_shape=[
            jax.ShapeDtypeStruct((B, nblk, ATT_WIDTH, MOBA_BLOCK), F32),
            jax.ShapeDtypeStruct((T, GDN_WIDTH), F32),
        ],
        scratch_shapes=[
            pltpu.VMEM((2, 2, MOBA_BLOCK, MOBA_BLOCK), F32),
            pltpu.VMEM((2, 2, 2 * nblk, MOBA_BLOCK), F32),
            pltpu.VMEM((2, 2, 2, LANES, MOBA_BLOCK), BF16),
            pltpu.VMEM((4, 2, nblk + 1, MOBA_BLOCK, MOBA_BLOCK), F32),
            pltpu.VMEM((4, 2, 2 * SUBLANES, MOBA_BLOCK), F32),
            pltpu.VMEM((2, TILE, W), F32),
            pltpu.VMEM((2, nchunk, 2 * GDN_CHUNK, W), BF16),
            pltpu.VMEM((2, TILE, W), BF16),
            pltpu.VMEM((2, TILE, W), BF16),
            pltpu.VMEM((2, nchunk * SUBLANES, W), F32),
            pltpu.VMEM((W // LANES, LANES, LANES), F32),
        ],
        compiler_params=pltpu.CompilerParams(
            dimension_semantics=("arbitrary",), vmem_limit_bytes=VMEM_LIMIT),
        name="token_mixers",
    )(rel_bias, qT, qT, k3, ksum, va, gqkv, gqkv, gqkv, gb, sz, nw_row)


def _out_mlp_kernel(x_ref, oTa_ref, oTb_ref, og_ref, woa_ref, wog_ref, pmn_ref, pre_ref, post_ref,
                    wup_ref, wdn_ref, out_ref):
    oT = jnp.concatenate([oTa_ref[0, 0], oTb_ref[0, 0]], axis=1)
    o_att = oT.T.astype(BF16)
    mix = jnp.dot(o_att, woa_ref[...], preferred_element_type=F32)
    mix = mix + jnp.dot(og_ref[...].astype(BF16), wog_ref[...], preferred_element_type=F32)
    x1 = x_ref[...] + _rms(mix, pmn_ref[...])
    h = _rms(x1, pre_ref[...]).astype(BF16)
    acc = jnp.zeros((ROW_TILE, D_MODEL), F32)
    for c in range(D_FF // FF_TILE):
        up = jnp.dot(h, wup_ref[:, c * FF_TILE:(c + 1) * FF_TILE], preferred_element_type=F32)
        act = jnp.square(jnp.maximum(up, 0.0)).astype(BF16)
        acc = acc + jnp.dot(act, wdn_ref[c * FF_TILE:(c + 1) * FF_TILE, :], preferred_element_type=F32)
    out_ref[...] = x1 + _rms(acc, post_ref[...])


def _out_mlp(xf, oT, og, woa, wog, pmn, pre, post, wup, wdn, B, S):
    T = B * S
    nblk = S // MOBA_BLOCK
    tiles_per_seq = S // ROW_TILE
    assert ROW_TILE == 2 * MOBA_BLOCK
    const = lambda i: (0, 0)
    row = lambda i: (i, 0)

    def att_block(which):
        def index(i):
            blk = 2 * (i % tiles_per_seq) + which
            return (i // tiles_per_seq, _paired_pos(blk, nblk), 0, 0)
        return index

    single = dict(pipeline_mode=pl.Buffered(1))
    return pl.pallas_call(
        _out_mlp_kernel,
        grid=(T // ROW_TILE,),
        in_specs=[
            pl.BlockSpec((ROW_TILE, D_MODEL), row),
            pl.BlockSpec((1, 1, ATT_WIDTH, MOBA_BLOCK), att_block(0)),
            pl.BlockSpec((1, 1, ATT_WIDTH, MOBA_BLOCK), att_block(1)),
            pl.BlockSpec((ROW_TILE, GDN_WIDTH), row),
            pl.BlockSpec(woa.shape, const, **single),
            pl.BlockSpec(wog.shape, const, **single),
            pl.BlockSpec((1, D_MODEL), const),
            pl.BlockSpec((1, D_MODEL), const),
            pl.BlockSpec((1, D_MODEL), const),
            pl.BlockSpec(wup.shape, const, **single),
            pl.BlockSpec(wdn.shape, const, **single),
        ],
        out_specs=pl.BlockSpec((ROW_TILE, D_MODEL), row),
        out_shape=jax.ShapeDtypeStruct((T, D_MODEL), F32),
        compiler_params=pltpu.CompilerParams(
            dimension_semantics=("arbitrary",), vmem_limit_bytes=VMEM_LIMIT),
        name="out_mlp",
    )(xf, oT, oT, og, woa, wog, pmn, pre, post, wup, wdn)


def kernel(x, w_in, w_out, conv_w, A_log, dt_bias, gdn_norm_w, rel_bias, pre_mix_norm,
           post_mix_norm, pre_mlp_norm, post_mlp_norm, w_up, w_down):
    B, S, D = x.shape
    assert D == D_MODEL and S % ROW_TILE == 0 and S % MOBA_BLOCK == 0
    T = B * S
    depth = w_in.shape[0]
    xf = x.reshape(T, D)
    o0, o1, o2, o3, o4 = 0, ATT_WIDTH, 2 * ATT_WIDTH, 3 * ATT_WIDTH, 3 * ATT_WIDTH + 3 * GDN_WIDTH
    o5 = o4 + GDN_WIDTH
    for l in range(depth):
        wi = w_in[l]
        wqT = wi[:, o0:o1].T.astype(BF16)
        wk = wi[:, o1:o2].astype(BF16)
        wvT = wi[:, o2:o3].T.astype(BF16)
        wg = wi[:, o3:o4].astype(BF16)
        wz = wi[:, o4:o5].astype(BF16)
        wab = jnp.pad(wi[:, o5:], ((0, 0), (0, LANES - 2 * GDN_HEADS))).astype(BF16)
        pad8 = lambda v: jnp.pad(v.astype(F32), (0, LANES - GDN_HEADS))[None, :]
        qT, k, ksum, va, gqkv, sz, gb = _inproj(xf, pre_mix_norm[l][None, :], wqT, wk, wvT, wg, wz, wab,
                                                conv_w[l], pad8(A_log[l]), pad8(dt_bias[l]), B, S)
        oT, og = _token_mixers(rel_bias.astype(F32), qT, k.reshape(B, S, ATT_WIDTH), ksum, va, gqkv, sz,
                               gb, jnp.tile(gdn_norm_w[l], GDN_HEADS)[None, :], B, S)
        wo = w_out[l].astype(BF16)
        xf = _out_mlp(xf, oT, og, wo[:ATT_WIDTH], wo[ATT_WIDTH:], post_mix_norm[l][None, :],
                      pre_mlp_norm[l][None, :], post_mlp_norm[l][None, :],
                      w_up[l].astype(BF16), w_down[l].astype(BF16), B, S)
    return xf.reshape(B, S, D)
```

```python
import functools
import math

import jax
import jax.numpy as jnp
from jax import lax
from jax.experimental import pallas as pl
from jax.experimental.pallas import tpu as pltpu

F32 = jnp.float32
BF16 = jnp.bfloat16
HI = lax.Precision.HIGHEST

D_MODEL = 1024
HEAD_DIM = 64
ATT_HEADS = 8
GDN_HEADS = 8
ATT_WIDTH = ATT_HEADS * HEAD_DIM
GDN_WIDTH = GDN_HEADS * HEAD_DIM
MOBA_BLOCK = 256
MOBA_TOPK = 3
GDN_CHUNK = 64
CONV_WIDTH = 4
D_FF = 4 * D_MODEL
REL_BUCKETS = 32
REL_MAX_EXACT = 16
REL_MAX_DIST = 128
EPS = 1e-6
NEG = -1e30
LOG2E = math.log2(math.e)

LANES = 128
SUBLANES = 8
VMEM_LIMIT = 56 * 1024 * 1024
ROW_TILE = 512
INPROJ_TILE = 512
FF_TILE = 1024

NT = (((1,), (1,)), ((), ()))
TN = (((0,), (0,)), ((), ()))


def _bucket_lower_bounds():
    def bucket(d):
        if d < REL_MAX_EXACT:
            return d
        t = math.log(d / REL_MAX_EXACT) / math.log(REL_MAX_DIST / REL_MAX_EXACT)
        t = t * (REL_BUCKETS - REL_MAX_EXACT)
        assert d in (REL_MAX_EXACT, REL_MAX_DIST) or abs(t - round(t)) > 1e-6
        return min(REL_MAX_EXACT + int(t + 1e-9), REL_BUCKETS - 1)
    lower = []
    for b in range(REL_BUCKETS):
        d = 0
        while bucket(d) < b:
            d += 1
        lower.append(d)
    return lower


BUCKET_LOWER = _bucket_lower_bounds()


def _sigmoid(x):
    return 0.5 * jnp.tanh(0.5 * x) + 0.5


def _silu_of_half(h):
    return h + h * jnp.tanh(h)


def _rms(x, w):
    return x * lax.rsqrt(jnp.mean(x * x, axis=-1, keepdims=True) + EPS) * w


def _split_bf16(x, parts):
    out = []
    for _ in range(parts):
        h = x.astype(BF16)
        out.append(h)
        x = x - h.astype(F32)
    return out


def _dot_split_rhs(c, x, parts):
    acc = None
    for h in _split_bf16(x, parts):
        d = jnp.dot(c, h, preferred_element_type=F32)
        acc = d if acc is None else acc + d
    return acc


CONV_COLS = 512


def _inproj_kernel(x_ref, xp_ref, nw_ref, wqT_ref, wk_ref, wvT_ref, wg_ref, wz_ref, wab_ref,
                   cw_ref, alog_ref, dtb_ref,
                   qT_ref, k_ref, ksum_ref, va_ref, g_ref, z_ref, gb_ref, *, tiles_per_seq):
    h = _rms(x_ref[...], nw_ref[...]).astype(BF16)

    hp = _rms(xp_ref[...], nw_ref[...]).astype(BF16)
    seq_start = (pl.program_id(0) % tiles_per_seq) == 0
    trow8 = lax.broadcasted_iota(jnp.int32, (SUBLANES, CONV_COLS), 0)
    for c in range(3 * GDN_WIDTH // CONV_COLS):
        cols = slice(c * CONV_COLS, (c + 1) * CONV_COLS)
        cur = jnp.dot(h, wg_ref[:, cols], preferred_element_type=F32)
        prev8 = jnp.dot(hp, wg_ref[:, cols], preferred_element_type=F32)
        prev8 = jnp.where(seq_start, 0.0, prev8)
        cw_half = 0.5 * cw_ref[:, cols]
        acc = cur * cw_half[CONV_WIDTH - 1:CONV_WIDTH]
        for s in range(1, CONV_WIDTH):
            rolled = pltpu.roll(cur, s, 0)
            top = jnp.where(trow8 < s, pltpu.roll(prev8, s, 0), rolled[0:SUBLANES])
            tap = jnp.concatenate([top, rolled[SUBLANES:]], axis=0)
            acc = acc + tap * cw_half[CONV_WIDTH - 1 - s:CONV_WIDTH - s]
        g_ref[:, cols] = _silu_of_half(acc)

    z = jnp.dot(h, wz_ref[...], preferred_element_type=F32)
    z_ref[...] = _silu_of_half(0.5 * z)
    ab = jnp.dot(h, wab_ref[...], preferred_element_type=F32)
    xs = ab + dtb_ref[...]
    log_decay = -jnp.exp(alog_ref[...]) * (jnp.maximum(xs, 0.0) + jnp.log1p(jnp.exp(-jnp.abs(xs))))
    lane = lax.broadcasted_iota(jnp.int32, ab.shape, 1)
    gb_ref[...] = jnp.where(lane < GDN_HEADS, log_decay, _sigmoid(ab))

    qT = lax.dot_general(wqT_ref[...], h, NT, preferred_element_type=F32)
    vT = lax.dot_general(wvT_ref[...], h, NT, preferred_element_type=F32)
    k = jnp.dot(h, wk_ref[...], preferred_element_type=F32)
    k_ref[...] = k.astype(BF16)
    ones_row = jnp.where(lax.broadcasted_iota(jnp.int32, (V_ROWS - HEAD_DIM, MOBA_BLOCK), 0) == 0,
                         1.0, 0.0).astype(BF16)
    ksum_ref[...] = jnp.zeros(ksum_ref.shape, F32)
    for t in range(INPROJ_TILE // MOBA_BLOCK):
        blk = slice(t * MOBA_BLOCK, (t + 1) * MOBA_BLOCK)
        qT_ref[0, t] = qT[:, blk]
        ksum_ref[0, t:t + 1, :] = jnp.sum(k[blk], axis=0, keepdims=True)
        for hh in range(ATT_HEADS):
            va_ref[0, t, hh, 0:HEAD_DIM, :] = vT[HEAD_DIM * hh:HEAD_DIM * (hh + 1), blk].astype(BF16)
            va_ref[0, t, hh, HEAD_DIM:V_ROWS, :] = ones_row


def _inproj(xf, nw, wqT, wk, wvT, wg, wz, wab, conv_w, alog_pad, dtb_pad, B, S):
    T = B * S
    TM = INPROJ_TILE
    assert S % TM == 0
    nblk = S // MOBA_BLOCK
    tiles_per_seq = S // TM
    blk_per_tile = TM // MOBA_BLOCK
    const = lambda i: (0, 0)
    row = lambda i: (i, 0)
    tr = lambda i: (i // tiles_per_seq, i % tiles_per_seq, 0, 0)
    prev_rows = lambda i: (jnp.maximum(i * (TM // SUBLANES) - 1, 0), 0)
    single = dict(pipeline_mode=pl.Buffered(1))
    return pl.pallas_call(
        functools.partial(_inproj_kernel, tiles_per_seq=tiles_per_seq),
        grid=(T // TM,),
        in_specs=[
            pl.BlockSpec((TM, D_MODEL), row),
            pl.BlockSpec((SUBLANES, D_MODEL), prev_rows),
            pl.BlockSpec((1, D_MODEL), const),
            pl.BlockSpec(wqT.shape, const, **single),
            pl.BlockSpec(wk.shape, const, **single),
            pl.BlockSpec(wvT.shape, const, **single),
            pl.BlockSpec(wg.shape, const, **single),
            pl.BlockSpec(wz.shape, const, **single),
            pl.BlockSpec(wab.shape, const, **single),
            pl.BlockSpec(conv_w.shape, const),
            pl.BlockSpec((1, LANES), const),
            pl.BlockSpec((1, LANES), const),
        ],
        out_specs=[
            pl.BlockSpec((1, blk_per_tile, ATT_WIDTH, MOBA_BLOCK), tr),
            pl.BlockSpec((TM, ATT_WIDTH), row),
            pl.BlockSpec((1, SUBLANES, ATT_WIDTH), lambda i: (i, 0, 0)),
            pl.BlockSpec((1, blk_per_tile, ATT_HEADS, V_ROWS, MOBA_BLOCK),
                         lambda i: (i // tiles_per_seq, i % tiles_per_seq, 0, 0, 0)),
            pl.BlockSpec((TM, 3 * GDN_WIDTH), row),
            pl.BlockSpec((TM, GDN_WIDTH), row),
            pl.BlockSpec((TM, LANES), row),
        ],
        out_shape=[
            jax.ShapeDtypeStruct((B, nblk, ATT_WIDTH, MOBA_BLOCK), F32),
            jax.ShapeDtypeStruct((T, ATT_WIDTH), BF16),
            jax.ShapeDtypeStruct((T // TM, SUBLANES, ATT_WIDTH), F32),
            jax.ShapeDtypeStruct((B, nblk, ATT_HEADS, V_ROWS, MOBA_BLOCK), BF16),
            jax.ShapeDtypeStruct((T, 3 * GDN_WIDTH), F32),
            jax.ShapeDtypeStruct((T, GDN_WIDTH), F32),
            jax.ShapeDtypeStruct((T, LANES), F32),
        ],
        compiler_params=pltpu.CompilerParams(
            dimension_semantics=("arbitrary",), vmem_limit_bytes=VMEM_LIMIT),
        name="inproj",
    )(xf, xf, nw, wqT, wk, wvT, wg, wz, wab, conv_w, alog_pad, dtb_pad)


V_ROWS = HEAD_DIM + 16


def _paired_pos(i, nblk):
    return jnp.where(i < nblk // 2, 2 * i, 2 * (nblk - 1 - i) + 1)


GDN_TILE = 256
GDN_HALF = 2 * LANES


def _gdn_stages(yq_ref, yk_ref, yv_ref, gb_ref, sz_ref, nw_ref, out_ref,
                u_ref, wq_ref, a_ref, kd_ref, gl_ref, st_ref):
    C = GDN_CHUNK
    W = GDN_WIDTH
    TILE = GDN_TILE
    npair = W // LANES

    r_w = lax.broadcasted_iota(jnp.int32, (GDN_HALF, GDN_HALF), 0)
    c_w = lax.broadcasted_iota(jnp.int32, (GDN_HALF, GDN_HALF), 1)
    head_ones = jnp.where((r_w // HEAD_DIM) == (c_w // HEAD_DIM), 1.0, 0.0).astype(BF16)
    ltri_bd = jnp.where(((r_w // C) == (c_w // C)) & (c_w <= r_w), 1.0, 0.0).astype(BF16)
    tok = lax.broadcasted_iota(jnp.int32, (TILE, W), 0) % C
    col = lax.broadcasted_iota(jnp.int32, (TILE, W), 1) % HEAD_DIM
    causal_t = tok >= col
    strict_t = tok > col
    lane_t = lax.broadcasted_iota(jnp.int32, (TILE, LANES), 1)

    lane = lax.broadcasted_iota(jnp.int32, (C, LANES), 1)
    rowi = lax.broadcasted_iota(jnp.int32, (C, LANES), 0)
    first_head = lane < HEAD_DIM
    strict = rowi > (lane % HEAD_DIM)
    eye2 = jnp.where(rowi == (lane % HEAD_DIM), 1.0, 0.0)
    lane2 = lax.broadcasted_iota(jnp.int32, (C, 2 * LANES), 1)
    first_head2 = (lane2 % LANES) < HEAD_DIM
    r_l = lax.broadcasted_iota(jnp.int32, (LANES, LANES), 0)
    c_l = lax.broadcasted_iota(jnp.int32, (LANES, LANES), 1)
    same_head = (r_l // HEAD_DIM) == (c_l // HEAD_DIM)
    pair_ones = jnp.where(same_head, 1.0, 0.0).astype(BF16)

    def stack(x, mask):
        return jnp.concatenate([jnp.where(mask, x, 0.0), jnp.where(mask, 0.0, x)], axis=0)

    dot = functools.partial(jnp.dot, preferred_element_type=F32)

    def head_sumsq(ys):
        halves = [(y * y).astype(BF16)[:, h:h + GDN_HALF] for y in ys for h in range(0, W, GDN_HALF)]
        sums = dot(jnp.concatenate(halves, axis=0), head_ones)
        per = W // GDN_HALF
        return [jnp.concatenate([sums[(i * per + j) * TILE:(i * per + j + 1) * TILE] for j in range(per)],
                                axis=1) for i in range(len(ys))]

    def solve_stages(slot):
        yq = yq_ref[...]
        yk = yk_ref[...]
        yv = yv_ref[...]
        ssq, ssk = head_sumsq([yq, yk])
        qn = yq * lax.rsqrt(ssq + EPS) * (HEAD_DIM ** -0.5)
        kn = yk * lax.rsqrt(ssk + EPS)
        yield
        gbt = gb_ref[...]

        def spread(col0):
            pairs = []
            for p in range(npair):
                a = jnp.broadcast_to(gbt[:, col0 + 2 * p:col0 + 2 * p + 1], (TILE, LANES))
                b = jnp.broadcast_to(gbt[:, col0 + 2 * p + 1:col0 + 2 * p + 2], (TILE, LANES))
                pairs.append(jnp.where(lane_t < HEAD_DIM, a, b))
            return jnp.concatenate(pairs, axis=1)

        g = spread(0)
        beta = spread(GDN_HEADS)
        gcd = _dot_split_rhs(ltri_bd, jnp.concatenate([g, jnp.where(strict_t, g, 0.0)], axis=1), 2)
        yield
        gc = gcd[:, :W]
        decay = jnp.where(causal_t, jnp.exp(jnp.where(causal_t, gcd[:, W:], 0.0)), 0.0)
        egc = jnp.exp(gc)
        kb = kn * beta
        rv = yv * beta
        rk = kb * egc
        qd = qn * egc
        for cc in range(TILE // C):
            rs = slice(cc * C, (cc + 1) * C)
            g_last = gc[(cc + 1) * C - 1:(cc + 1) * C, :]
            kd_ref[slot, rs, :] = (kn[rs] * jnp.exp(g_last - gc[rs])).astype(BF16)
            gl_ref[slot, cc * SUBLANES:(cc + 1) * SUBLANES, :] = (
                jnp.broadcast_to(jnp.exp(g_last), (SUBLANES, W)))
        units = [(slice(cc * C, (cc + 1) * C), slice(LANES * p, LANES * (p + 1)), cc)
                 for cc in range(TILE // C) for p in range(npair)]
        kqs = [lax.dot_general(jnp.concatenate([kn[rs, ls], qn[rs, ls]], axis=0).astype(BF16),
                               stack(kn[rs, ls], first_head).astype(BF16), NT,
                               preferred_element_type=F32) for rs, ls, _ in units]
        yield
        ps = [-jnp.where(strict, kq[0:C] * beta[rs, ls] * decay[rs, ls], 0.0)
              for kq, (rs, ls, _) in zip(kqs, units)]
        ss = [eye2 + p for p in ps]
        ps = [dot(p.astype(BF16), stack(p, first_head).astype(BF16)) for p in ps]
        yield
        nround = int(math.log2(C))
        for k in range(1, nround):
            rhs = [stack(s_, first_head).astype(BF16) for s_ in ss]
            if k + 1 < nround:
                rhs = [jnp.concatenate([stack(p, first_head).astype(BF16), sx], axis=1)
                       for p, sx in zip(ps, rhs)]
            outs = [dot(p.astype(BF16), sx) for p, sx in zip(ps, rhs)]
            if k + 1 < nround:
                ps = [o[:, :LANES] for o in outs]
                ss = [s_ + o[:, LANES:] for s_, o in zip(ss, outs)]
            else:
                ss = [s_ + o for s_, o in zip(ss, outs)]
            yield
        xs = [dot(s_.astype(BF16),
                  stack(jnp.concatenate([rv[rs, ls], rk[rs, ls]], axis=1), first_head2).astype(BF16))
              for s_, (rs, ls, _) in zip(ss, units)]
        yield
        for x, kq, (rs, ls, cc) in zip(xs, kqs, units):
            u_ref[slot, rs, ls] = x[:, :LANES]
            wq_ref[slot, cc, 0:C, ls] = x[:, LANES:].astype(BF16)
            wq_ref[slot, cc, C:2 * C, ls] = qd[rs, ls].astype(BF16)
            a_ref[slot, rs, ls] = (kq[C:2 * C] * decay[rs, ls]).astype(BF16)

    lss = [slice(LANES * p, LANES * (p + 1)) for p in range(npair)]

    def recurrence_stages(slot, seq_start):
        states = [jnp.where(seq_start, 0.0, st_ref[p]) for p in range(npair)]
        pending = None

        def finish(rs, os_):
            sq = jnp.concatenate([(o * o).astype(BF16) for o in os_], axis=0)
            ms_all = dot(sq, pair_ones) * (1.0 / HEAD_DIM)
            for p, (ls, o) in enumerate(zip(lss, os_)):
                ms = ms_all[p * C:(p + 1) * C]
                out_ref[rs, ls] = o * lax.rsqrt(ms + EPS) * nw_ref[:, ls] * sz_ref[rs, ls]

        for cc in range(TILE // C):
            rs = slice(cc * C, (cc + 1) * C)
            wqs = [dot(wq_ref[slot, cc, :, ls], st.astype(BF16)) for ls, st in zip(lss, states)]
            if pending is not None:
                finish(*pending)
            yield
            v_news = [u_ref[slot, rs, ls] - wq[0:C] for ls, wq in zip(lss, wqs)]
            kvs = [lax.dot_general(kd_ref[slot, rs, ls], v.astype(BF16), TN, preferred_element_type=F32)
                   for ls, v in zip(lss, v_news)]
            os_ = [wq[C:2 * C] + dot(a_ref[slot, rs, ls], stack(v, first_head).astype(BF16))
                   for ls, wq, v in zip(lss, wqs, v_news)]
            states = [st * gl_ref[slot, cc * SUBLANES:cc * SUBLANES + 1, ls] + jnp.where(same_head, kv, 0.0)
                      for ls, st, kv in zip(lss, states, kvs)]
            pending = (rs, os_)
            yield
        finish(*pending)
        for p in range(npair):
            st_ref[p] = states[p]

    return solve_stages, recurrence_stages


def _interleave_weighted(*gens_and_weights):
    live = [[gen, weight] for gen, weight in gens_and_weights]
    while live:
        for entry in list(live):
            for _ in range(entry[1]):
                try:
                    next(entry[0])
                except StopIteration:
                    live.remove(entry)
                    break


def _mixers_kernel(relb_ref, qlo_ref, qhi_ref, kb_ref, ksum_ref, va_ref,
                   yq_ref, yk_ref, yv_ref, gb_ref, sz_ref, nw_ref,
                   oT_ref, og_ref,
                   bias_ref, addm_ref, qh_ref, lg_ref, moff_ref,
                   u_ref, wq_ref, a_ref, kd_ref, gl_ref, st_ref,
                   *, nblk, nbatch, tiles_per_seq, nsteps):
    step = pl.program_id(0)
    BLK = MOBA_BLOCK
    half = nblk // 2
    pair = jnp.minimum(step, nsteps - 2)
    hp = pair // (nbatch * (half // 2))
    b = (pair // (half // 2)) % nbatch
    scoring = step < nsteps - 1
    new_kv = (pair % (half // 2)) == 0

    solve_stages, recurrence_stages = _gdn_stages(
        yq_ref, yk_ref, yv_ref, gb_ref, sz_ref, nw_ref, og_ref,
        u_ref, wq_ref, a_ref, kd_ref, gl_ref, st_ref)
    seq_start = ((step - 1) % tiles_per_seq) == 0

    @pl.when((b == 0) & new_kv & scoring)
    def _():
        kk = lax.broadcasted_iota(jnp.int32, (BLK, BLK), 0)
        qq = lax.broadcasted_iota(jnp.int32, (BLK, BLK), 1)
        for hh in range(2):
            h = 2 * hp + hh
            for kind in range(2):
                d = qq - kk + kind * BLK
                val = jnp.full((BLK, BLK), relb_ref[h, REL_BUCKETS - 1], F32)
                for bkt in range(REL_BUCKETS - 2, -1, -1):
                    val = jnp.where(d < BUCKET_LOWER[bkt + 1], relb_ref[h, bkt], val)
                val = val * LOG2E
                if kind == 0:
                    val = jnp.where(d >= 0, val, NEG)
                bias_ref[hh, kind] = val

    def key_means():
        per_tile = INPROJ_TILE // BLK
        ks = ksum_ref[...]
        km = jnp.concatenate([ks[j // per_tile, j % per_tile:j % per_tile + 1, :] for j in range(nblk)],
                             axis=0) * (1.0 / BLK)
        lane = lax.broadcasted_iota(jnp.int32, (nblk, LANES), 1)
        return jnp.concatenate([jnp.where(lane < HEAD_DIM, km, 0.0),
                                jnp.where(lane >= HEAD_DIM, km, 0.0)], axis=0)

    def item_tiles(t):
        i_hi = nblk - 1 - t
        tiles = [(0, t, "own"), (1, i_hi, "own"), (1, i_hi - 1, "prev")]
        if t >= 1:
            tiles.append((0, t - 1, "prev"))
        tiles += [(0, j, "far") for j in range(t - 1)]
        tiles += [(1, j, "far") for j in range(i_hi - 1)]
        assert len(tiles) == nblk + 1
        return tiles

    def score_stages(parity):
        ridx = lax.broadcasted_iota(jnp.int32, (nblk, BLK), 0)
        sub = lax.broadcasted_iota(jnp.int32, (LANES, BLK), 0)
        scale = HEAD_DIM ** -0.5 * LOG2E
        km = key_means()
        for e in range(2):
            t = 2 * parity + e
            slot = 2 * parity + e
            q_of = ((qlo_ref, e, t), (qhi_ref, 1 - e, nblk - 1 - t))
            for s, (q_ref, w, qi) in enumerate(q_of):
                qT = q_ref[0, w]
                gT = jnp.dot(km, qT, precision=HI, preferred_element_type=F32)
                past = ridx < qi
                for hh in range(2):
                    gm = jnp.where(past, gT[nblk * hh:nblk * (hh + 1)], -jnp.inf)
                    cnt = jnp.zeros((nblk, BLK), F32)
                    for jp in range(nblk):
                        row = gm[jp:jp + 1, :]
                        beats = (row > gm) | ((row == gm) & (ridx > jp))
                        cnt = cnt + jnp.where(beats, 1.0, 0.0)
                    visible = past & (cnt < MOBA_TOPK)
                    addm_ref[e, s, nblk * hh:nblk * (hh + 1), :] = jnp.where(visible, 0.0, NEG)
                    in_head = (sub >= HEAD_DIM * hh) & (sub < HEAD_DIM * (hh + 1))
                    qh_ref[e, s, hh] = jnp.where(in_head, qT * scale, 0.0).astype(BF16)
            yield
            tiles = item_tiles(t)
            for hh in range(2):
                cmax = {0: [], 1: []}
                offs = []
                for n, (s, kblk, cls) in enumerate(tiles):
                    lg = jnp.dot(kb_ref[0, kblk * BLK:(kblk + 1) * BLK, :], qh_ref[e, s, hh],
                                 preferred_element_type=F32)
                    if cls != "far":
                        lg = lg + bias_ref[hh, 0 if cls == "own" else 1]
                    lg_ref[slot, hh, n] = lg
                    cm = jnp.max(lg, axis=0, keepdims=True)
                    off = None
                    if cls != "own":
                        off = addm_ref[e, s, nblk * hh + kblk:nblk * hh + kblk + 1, :]
                        if cls == "far":
                            off = off + relb_ref[2 * hp + hh, REL_BUCKETS - 1] * LOG2E
                        cm = cm + off
                    cmax[s].append(cm)
                    offs.append(off)
                    yield
                m = {s: functools.reduce(jnp.maximum, cmax[s]) for s in (0, 1)}
                for n, (s, _, _) in enumerate(tiles):
                    moff_ref[slot, hh, n:n + 1, :] = m[s] if offs[n] is None else m[s] - offs[n]

    def softmax_pv_stages(parity):
        for e in range(2):
            t = 2 * parity + e
            slot = 2 * parity + e
            tiles = item_tiles(t)
            for hh in range(2):
                acc = {0: None, 1: None}
                for n, (s, kblk, _) in enumerate(tiles):
                    p = jnp.exp2(lg_ref[slot, hh, n] - moff_ref[slot, hh, n:n + 1, :])
                    pvn = jnp.dot(va_ref[0, kblk, hh], p.astype(BF16),
                                  preferred_element_type=F32)
                    acc[s] = pvn if acc[s] is None else acc[s] + pvn
                    yield
                for s in (0, 1):
                    oT_ref[0, 2 * e + s, HEAD_DIM * hh:HEAD_DIM * (hh + 1), :] = (
                        acc[s][0:HEAD_DIM] / acc[s][HEAD_DIM:HEAD_DIM + 1])

    ATT, GDN = 2, 1

    @pl.when(step == 0)
    def _():
        st_ref[...] = jnp.zeros(st_ref.shape, F32)
        _interleave_weighted((score_stages(0), ATT), (solve_stages(0), GDN))

    for parity in range(2):
        @pl.when((step > 0) & (step < nsteps - 1) & (step % 2 == parity))
        def _(parity=parity):
            _interleave_weighted((recurrence_stages(1 - parity, seq_start), GDN),
                                 (score_stages(parity), ATT),
                                 (solve_stages(parity), GDN),
                                 (softmax_pv_stages(1 - parity), ATT))

    @pl.when(step == nsteps - 1)
    def _():
        last = (nsteps - 2) % 2
        _interleave_weighted((recurrence_stages(last, seq_start), GDN), (softmax_pv_stages(last), ATT))


def _token_mixers(rel_bias, qT, k3, ksum, va, gqkv, sz, gb, nw_row, B, S):
    T = B * S
    W = GDN_WIDTH
    TILE = GDN_TILE
    nblk = S // MOBA_BLOCK
    half = nblk // 2
    assert TILE == GDN_HALF and S % TILE == 0
    assert BUCKET_LOWER[REL_BUCKETS - 1] <= MOBA_BLOCK + 1
    assert nblk == 8 and nblk + 1 <= 2 * SUBLANES
    nchunk = TILE // GDN_CHUNK
    ntiles = T // TILE
    npairs = (ATT_HEADS // 2) * B * (half // 2)
    assert npairs == ntiles
    nsteps = ntiles + 1

    def scored(s):
        p = jnp.minimum(s, npairs - 1)
        return p // (B * (half // 2)), (p // (half // 2)) % B, p % (half // 2)

    def done(s):
        p = jnp.maximum(s - 1, 0)
        return p // (B * (half // 2)), (p // (half // 2)) % B, p % (half // 2)

    def q_lo(s):
        hp, b, m = scored(s)
        return (b, m, hp, 0)

    def q_hi(s):
        hp, b, m = scored(s)
        return (b, half - 1 - m, hp, 0)

    def k_blk(s):
        hp, b, _ = scored(s)
        return (b, 0, hp)

    def v_blk(s):
        hp, b, _ = done(s)
        return (b, 0, hp, 0, 0)

    def o_blk(s):
        hp, b, m = done(s)
        return (b, m, hp, 0)

    cur_tile = lambda off: (lambda s: (jnp.minimum(s, ntiles - 1), off))
    prev_tile = lambda s: (jnp.maximum(s - 1, 0), 0)
    return pl.pallas_call(
        functools.partial(_mixers_kernel, nblk=nblk, nbatch=B, tiles_per_seq=S // TILE, nsteps=nsteps),
        grid=(nsteps,),
        in_specs=[
            pl.BlockSpec(memory_space=pltpu.SMEM),
            pl.BlockSpec((1, 2, LANES, MOBA_BLOCK), q_lo),
            pl.BlockSpec((1, 2, LANES, MOBA_BLOCK), q_hi),
            pl.BlockSpec((1, S, LANES), k_blk),
            pl.BlockSpec((S // INPROJ_TILE, SUBLANES, LANES), k_blk),
            pl.BlockSpec((1, nblk, 2, V_ROWS, MOBA_BLOCK), v_blk),
            pl.BlockSpec((TILE, W), cur_tile(0)),
            pl.BlockSpec((TILE, W), cur_tile(1)),
            pl.BlockSpec((TILE, W), cur_tile(2)),
            pl.BlockSpec((TILE, LANES), cur_tile(0)),
            pl.BlockSpec((TILE, W), prev_tile),
            pl.BlockSpec((1, W), lambda s: (0, 0)),
        ],
        out_specs=[
            pl.BlockSpec((1, 4, LANES, MOBA_BLOCK), o_blk),
            pl.BlockSpec((TILE, W), prev_tile),
        ],
        out_shape=[
            jax.ShapeDtypeStruct((B, nblk, ATT_WIDTH, MOBA_BLOCK), F32),
            jax.ShapeDtypeStruct((T, GDN_WIDTH), F32),
        ],
        scratch_shapes=[
            pltpu.VMEM((2, 2, MOBA_BLOCK, MOBA_BLOCK), F32),
            pltpu.VMEM((2, 2, 2 * nblk, MOBA_BLOCK), F32),
            pltpu.VMEM((2, 2, 2, LANES, MOBA_BLOCK), BF16),
            pltpu.VMEM((4, 2, nblk + 1, MOBA_BLOCK, MOBA_BLOCK), F32),
            pltpu.VMEM((4, 2, 2 * SUBLANES, MOBA_BLOCK), F32),
            pltpu.VMEM((2, TILE, W), F32),
            pltpu.VMEM((2, nchunk, 2 * GDN_CHUNK, W), BF16),
            pltpu.VMEM((2, TILE, W), BF16),
            pltpu.VMEM((2, TILE, W), BF16),
            pltpu.VMEM((2, nchunk * SUBLANES, W), F32),
            pltpu.VMEM((W // LANES, LANES, LANES), F32),
        ],
        compiler_params=pltpu.CompilerParams(
            dimension_semantics=("arbitrary",), vmem_limit_bytes=VMEM_LIMIT),
        name="token_mixers",
    )(rel_bias, qT, qT, k3, ksum, va, gqkv, gqkv, gqkv, gb, sz, nw_row)


def _out_mlp_kernel(x_ref, oTa_ref, oTb_ref, og_ref, woa_ref, wog_ref, pmn_ref, pre_ref, post_ref,
                    wup_ref, wdn_ref, out_ref):
    oT = jnp.concatenate([oTa_ref[0, 0], oTb_ref[0, 0]], axis=1)
    o_att = oT.T.astype(BF16)
    mix = jnp.dot(o_att, woa_ref[...], preferred_element_type=F32)
    mix = mix + jnp.dot(og_ref[...].astype(BF16), wog_ref[...], preferred_element_type=F32)
    x1 = x_ref[...] + _rms(mix, pmn_ref[...])
    h = _rms(x1, pre_ref[...]).astype(BF16)
    acc = jnp.zeros((ROW_TILE, D_MODEL), F32)
    for c in range(D_FF // FF_TILE):
        up = jnp.dot(h, wup_ref[:, c * FF_TILE:(c + 1) * FF_TILE], preferred_element_type=F32)
        act = jnp.square(jnp.maximum(up, 0.0)).astype(BF16)
        acc = acc + jnp.dot(act, wdn_ref[c * FF_TILE:(c + 1) * FF_TILE, :], preferred_element_type=F32)
    out_ref[...] = x1 + _rms(acc, post_ref[...])


def _out_mlp(xf, oT, og, woa, wog, pmn, pre, post, wup, wdn, B, S):
    T = B * S
    nblk = S // MOBA_BLOCK
    tiles_per_seq = S // ROW_TILE
    assert ROW_TILE == 2 * MOBA_BLOCK
    const = lambda i: (0, 0)
    row = lambda i: (i, 0)

    def att_block(which):
        def index(i):
            blk = 2 * (i % tiles_per_seq) + which
            return (i // tiles_per_seq, _paired_pos(blk, nblk), 0, 0)
        return index

    single = dict(pipeline_mode=pl.Buffered(1))
    return pl.pallas_call(
        _out_mlp_kernel,
        grid=(T // ROW_TILE,),
        in_specs=[
            pl.BlockSpec((ROW_TILE, D_MODEL), row),
            pl.BlockSpec((1, 1, ATT_WIDTH, MOBA_BLOCK), att_block(0)),
            pl.BlockSpec((1, 1, ATT_WIDTH, MOBA_BLOCK), att_block(1)),
            pl.BlockSpec((ROW_TILE, GDN_WIDTH), row),
            pl.BlockSpec(woa.shape, const, **single),
            pl.BlockSpec(wog.shape, const, **single),
            pl.BlockSpec((1, D_MODEL), const),
            pl.BlockSpec((1, D_MODEL), const),
            pl.BlockSpec((1, D_MODEL), const),
            pl.BlockSpec(wup.shape, const, **single),
            pl.BlockSpec(wdn.shape, const, **single),
        ],
        out_specs=pl.BlockSpec((ROW_TILE, D_MODEL), row),
        out_shape=jax.ShapeDtypeStruct((T, D_MODEL), F32),
        compiler_params=pltpu.CompilerParams(
            dimension_semantics=("arbitrary",), vmem_limit_bytes=VMEM_LIMIT),
        name="out_mlp",
    )(xf, oT, oT, og, woa, wog, pmn, pre, post, wup, wdn)


def kernel(x, w_in, w_out, conv_w, A_log, dt_bias, gdn_norm_w, rel_bias, pre_mix_norm,
           post_mix_norm, pre_mlp_norm, post_mlp_norm, w_up, w_down):
    B, S, D = x.shape
    assert D == D_MODEL and S % ROW_TILE == 0 and S % MOBA_BLOCK == 0
    T = B * S
    depth = w_in.shape[0]
    xf = x.reshape(T, D)
    o0, o1, o2, o3, o4 = 0, ATT_WIDTH, 2 * ATT_WIDTH, 3 * ATT_WIDTH, 3 * ATT_WIDTH + 3 * GDN_WIDTH
    o5 = o4 + GDN_WIDTH
    for l in range(depth):
        wi = w_in[l]
        wqT = wi[:, o0:o1].T.astype(BF16)
        wk = wi[:, o1:o2].astype(BF16)
        wvT = wi[:, o2:o3].T.astype(BF16)
        wg = wi[:, o3:o4].astype(BF16)
        wz = wi[:, o4:o5].astype(BF16)
        wab = jnp.pad(wi[:, o5:], ((0, 0), (0, LANES - 2 * GDN_HEADS))).astype(BF16)
        pad8 = lambda v: jnp.pad(v.astype(F32), (0, LANES - GDN_HEADS))[None, :]
        qT, k, ksum, va, gqkv, sz, gb = _inproj(xf, pre_mix_norm[l][None, :], wqT, wk, wvT, wg, wz, wab,
                                                conv_w[l], pad8(A_log[l]), pad8(dt_bias[l]), B, S)
        oT, og = _token_mixers(rel_bias.astype(F32), qT, k.reshape(B, S, ATT_WIDTH), ksum, va, gqkv, sz,
                               gb, jnp.tile(gdn_norm_w[l], GDN_HEADS)[None, :], B, S)
        wo = w_out[l].astype(BF16)
        xf = _out_mlp(xf, oT, og, wo[:ATT_WIDTH], wo[ATT_WIDTH:], post_mix_norm[l][None, :],
                      pre_mlp_norm[l][None, :], post_mlp_norm[l][None, :],
                      w_up[l].astype(BF16), w_down[l].astype(BF16), B, S)
    return xf.reshape(B, S, D)
```

```python
import functools
import math

import jax
import jax.numpy as jnp
from jax import lax
from jax.experimental import pallas as pl
from jax.experimental.pallas import tpu as pltpu

F32 = jnp.float32
BF16 = jnp.bfloat16
HI = lax.Precision.HIGHEST

D_MODEL = 1024
HEAD_DIM = 64
ATT_HEADS = 8
GDN_HEADS = 8
ATT_WIDTH = ATT_HEADS * HEAD_DIM
GDN_WIDTH = GDN_HEADS * HEAD_DIM
MOBA_BLOCK = 256
MOBA_TOPK = 3
GDN_CHUNK = 64
CONV_WIDTH = 4
D_FF = 4 * D_MODEL
REL_BUCKETS = 32
REL_MAX_EXACT = 16
REL_MAX_DIST = 128
EPS = 1e-6
NEG = -1e30
LOG2E = math.log2(math.e)

LANES = 128
SUBLANES = 8
VMEM_LIMIT = 56 * 1024 * 1024
ROW_TILE = 512
INPROJ_TILE = 512
FF_TILE = 1024

NT = (((1,), (1,)), ((), ()))
TN = (((0,), (0,)), ((), ()))


def _bucket_lower_bounds():
    def bucket(d):
        if d < REL_MAX_EXACT:
            return d
        t = math.log(d / REL_MAX_EXACT) / math.log(REL_MAX_DIST / REL_MAX_EXACT)
        t = t * (REL_BUCKETS - REL_MAX_EXACT)
        assert d in (REL_MAX_EXACT, REL_MAX_DIST) or abs(t - round(t)) > 1e-6
        return min(REL_MAX_EXACT + int(t + 1e-9), REL_BUCKETS - 1)
    lower = []
    for b in range(REL_BUCKETS):
        d = 0
        while bucket(d) < b:
            d += 1
        lower.append(d)
    return lower


BUCKET_LOWER = _bucket_lower_bounds()


def _sigmoid(x):
    return 0.5 * jnp.tanh(0.5 * x) + 0.5


def _silu_of_half(h):
    return h + h * jnp.tanh(h)


def _rms(x, w):
    return x * lax.rsqrt(jnp.mean(x * x, axis=-1, keepdims=True) + EPS) * w


def _split_bf16(x, parts):
    out = []
    for _ in range(parts):
        h = x.astype(BF16)
        out.append(h)
        x = x - h.astype(F32)
    return out


def _dot_split_rhs(c, x, parts):
    acc = None
    for h in _split_bf16(x, parts):
        d = jnp.dot(c, h, preferred_element_type=F32)
        acc = d if acc is None else acc + d
    return acc


CONV_COLS = 512


def _inproj_kernel(x_ref, xp_ref, nw_ref, wqT_ref, wk_ref, wvT_ref, wg_ref, wz_ref, wab_ref,
                   cw_ref, alog_ref, dtb_ref,
                   qT_ref, k_ref, ksum_ref, va_ref, g_ref, z_ref, gb_ref, *, tiles_per_seq):
    h = _rms(x_ref[...], nw_ref[...]).astype(BF16)

    hp = _rms(xp_ref[...], nw_ref[...]).astype(BF16)
    seq_start = (pl.program_id(0) % tiles_per_seq) == 0
    trow8 = lax.broadcasted_iota(jnp.int32, (SUBLANES, CONV_COLS), 0)
    for c in range(3 * GDN_WIDTH // CONV_COLS):
        cols = slice(c * CONV_COLS, (c + 1) * CONV_COLS)
        cur = jnp.dot(h, wg_ref[:, cols], preferred_element_type=F32)
        prev8 = jnp.dot(hp, wg_ref[:, cols], preferred_element_type=F32)
        prev8 = jnp.where(seq_start, 0.0, prev8)
        cw_half = 0.5 * cw_ref[:, cols]
        acc = cur * cw_half[CONV_WIDTH - 1:CONV_WIDTH]
        for s in range(1, CONV_WIDTH):
            rolled = pltpu.roll(cur, s, 0)
            top = jnp.where(trow8 < s, pltpu.roll(prev8, s, 0), rolled[0:SUBLANES])
            tap = jnp.concatenate([top, rolled[SUBLANES:]], axis=0)
            acc = acc + tap * cw_half[CONV_WIDTH - 1 - s:CONV_WIDTH - s]
        g_ref[:, cols] = _silu_of_half(acc)

    z = jnp.dot(h, wz_ref[...], preferred_element_type=F32)
    z_ref[...] = _silu_of_half(0.5 * z)
    ab = jnp.dot(h, wab_ref[...], preferred_element_type=F32)
    xs = ab + dtb_ref[...]
    log_decay = -jnp.exp(alog_ref[...]) * (jnp.maximum(xs, 0.0) + jnp.log1p(jnp.exp(-jnp.abs(xs))))
    lane = lax.broadcasted_iota(jnp.int32, ab.shape, 1)
    gb_ref[...] = jnp.where(lane < GDN_HEADS, log_decay, _sigmoid(ab))

    qT = lax.dot_general(wqT_ref[...], h, NT, preferred_element_type=F32)
    vT = lax.dot_general(wvT_ref[...], h, NT, preferred_element_type=F32)
    k = jnp.dot(h, wk_ref[...], preferred_element_type=F32)
    k_ref[...] = k.astype(BF16)
    ones_row = jnp.where(lax.broadcasted_iota(jnp.int32, (V_ROWS - HEAD_DIM, MOBA_BLOCK), 0) == 0,
                         1.0, 0.0).astype(BF16)
    ksum_ref[...] = jnp.zeros(ksum_ref.shape, F32)
    for t in range(INPROJ_TILE // MOBA_BLOCK):
        blk = slice(t * MOBA_BLOCK, (t + 1) * MOBA_BLOCK)
        qT_ref[0, t] = qT[:, blk]
        ksum_ref[0, t:t + 1, :] = jnp.sum(k[blk], axis=0, keepdims=True)
        for hh in range(ATT_HEADS):
            va_ref[0, t, hh, 0:HEAD_DIM, :] = vT[HEAD_DIM * hh:HEAD_DIM * (hh + 1), blk].astype(BF16)
            va_ref[0, t, hh, HEAD_DIM:V_ROWS, :] = ones_row


def _inproj(xf, nw, wqT, wk, wvT, wg, wz, wab, conv_w, alog_pad, dtb_pad, B, S):
    T = B * S
    TM = INPROJ_TILE
    assert S % TM == 0
    nblk = S // MOBA_BLOCK
    tiles_per_seq = S // TM
    blk_per_tile = TM // MOBA_BLOCK
    const = lambda i: (0, 0)
    row = lambda i: (i, 0)
    tr = lambda i: (i // tiles_per_seq, i % tiles_per_seq, 0, 0)
    prev_rows = lambda i: (jnp.maximum(i * (TM // SUBLANES) - 1, 0), 0)
    single = dict(pipeline_mode=pl.Buffered(1))
    return pl.pallas_call(
        functools.partial(_inproj_kernel, tiles_per_seq=tiles_per_seq),
        grid=(T // TM,),
        in_specs=[
            pl.BlockSpec((TM, D_MODEL), row),
            pl.BlockSpec((SUBLANES, D_MODEL), prev_rows),
            pl.BlockSpec((1, D_MODEL), const),
            pl.BlockSpec(wqT.shape, const, **single),
            pl.BlockSpec(wk.shape, const, **single),
            pl.BlockSpec(wvT.shape, const, **single),
            pl.BlockSpec(wg.shape, const, **single),
            pl.BlockSpec(wz.shape, const, **single),
            pl.BlockSpec(wab.shape, const, **single),
            pl.BlockSpec(conv_w.shape, const),
            pl.BlockSpec((1, LANES), const),
            pl.BlockSpec((1, LANES), const),
        ],
        out_specs=[
            pl.BlockSpec((1, blk_per_tile, ATT_WIDTH, MOBA_BLOCK), tr),
            pl.BlockSpec((TM, ATT_WIDTH), row),
            pl.BlockSpec((1, SUBLANES, ATT_WIDTH), lambda i: (i, 0, 0)),
            pl.BlockSpec((1, blk_per_tile, ATT_HEADS, V_ROWS, MOBA_BLOCK),
                         lambda i: (i // tiles_per_seq, i % tiles_per_seq, 0, 0, 0)),
            pl.BlockSpec((TM, 3 * GDN_WIDTH), row),
            pl.BlockSpec((TM, GDN_WIDTH), row),
            pl.BlockSpec((TM, LANES), row),
        ],
        out_shape=[
            jax.ShapeDtypeStruct((B, nblk, ATT_WIDTH, MOBA_BLOCK), F32),
            jax.ShapeDtypeStruct((T, ATT_WIDTH), BF16),
            jax.ShapeDtypeStruct((T // TM, SUBLANES, ATT_WIDTH), F32),
            jax.ShapeDtypeStruct((B, nblk, ATT_HEADS, V_ROWS, MOBA_BLOCK), BF16),
            jax.ShapeDtypeStruct((T, 3 * GDN_WIDTH), F32),
            jax.ShapeDtypeStruct((T, GDN_WIDTH), F32),
            jax.ShapeDtypeStruct((T, LANES), F32),
        ],
        compiler_params=pltpu.CompilerParams(
            dimension_semantics=("arbitrary",), vmem_limit_bytes=VMEM_LIMIT),
        name="inproj",
    )(xf, xf, nw, wqT, wk, wvT, wg, wz, wab, conv_w, alog_pad, dtb_pad)


V_ROWS = HEAD_DIM + 16


def _paired_pos(i, nblk):
    return jnp.where(i < nblk // 2, 2 * i, 2 * (nblk - 1 - i) + 1)


GDN_TILE = 256
GDN_HALF = 2 * LANES


def _gdn_stages(yq_ref, yk_ref, yv_ref, gb_ref, sz_ref, nw_ref, out_ref,
                u_ref, wq_ref, a_ref, kd_ref, gl_ref, st_ref):
    C = GDN_CHUNK
    W = GDN_WIDTH
    TILE = GDN_TILE
    npair = W // LANES

    r_w = lax.broadcasted_iota(jnp.int32, (GDN_HALF, GDN_HALF), 0)
    c_w = lax.broadcasted_iota(jnp.int32, (GDN_HALF, GDN_HALF), 1)
    head_ones = jnp.where((r_w // HEAD_DIM) == (c_w // HEAD_DIM), 1.0, 0.0).astype(BF16)
    ltri_bd = jnp.where(((r_w // C) == (c_w // C)) & (c_w <= r_w), 1.0, 0.0).astype(BF16)
    tok = lax.broadcasted_iota(jnp.int32, (TILE, W), 0) % C
    col = lax.broadcasted_iota(jnp.int32, (TILE, W), 1) % HEAD_DIM
    causal_t = tok >= col
    strict_t = tok > col
    lane_t = lax.broadcasted_iota(jnp.int32, (TILE, LANES), 1)

    lane = lax.broadcasted_iota(jnp.int32, (C, LANES), 1)
    rowi = lax.broadcasted_iota(jnp.int32, (C, LANES), 0)
    first_head = lane < HEAD_DIM
    strict = rowi > (lane % HEAD_DIM)
    eye2 = jnp.where(rowi == (lane % HEAD_DIM), 1.0, 0.0)
    lane2 = lax.broadcasted_iota(jnp.int32, (C, 2 * LANES), 1)
    first_head2 = (lane2 % LANES) < HEAD_DIM
    r_l = lax.broadcasted_iota(jnp.int32, (LANES, LANES), 0)
    c_l = lax.broadcasted_iota(jnp.int32, (LANES, LANES), 1)
    same_head = (r_l // HEAD_DIM) == (c_l // HEAD_DIM)
    pair_ones = jnp.where(same_head, 1.0, 0.0).astype(BF16)

    def stack(x, mask):
        return jnp.concatenate([jnp.where(mask, x, 0.0), jnp.where(mask, 0.0, x)], axis=0)

    dot = functools.partial(jnp.dot, preferred_element_type=F32)

    def head_sumsq(ys):
        halves = [(y * y).astype(BF16)[:, h:h + GDN_HALF] for y in ys for h in range(0, W, GDN_HALF)]
        sums = dot(jnp.concatenate(halves, axis=0), head_ones)
        per = W // GDN_HALF
        return [jnp.concatenate([sums[(i * per + j) * TILE:(i * per + j + 1) * TILE] for j in range(per)],
                                axis=1) for i in range(len(ys))]

    def solve_stages(slot):
        yq = yq_ref[...]
        yk = yk_ref[...]
        yv = yv_ref[...]
        ssq, ssk = head_sumsq([yq, yk])
        qn = yq * lax.rsqrt(ssq + EPS) * (HEAD_DIM ** -0.5)
        kn = yk * lax.rsqrt(ssk + EPS)
        yield
        gbt = gb_ref[...]

        def spread(col0):
            pairs = []
            for p in range(npair):
                a = jnp.broadcast_to(gbt[:, col0 + 2 * p:col0 + 2 * p + 1], (TILE, LANES))
                b = jnp.broadcast_to(gbt[:, col0 + 2 * p + 1:col0 + 2 * p + 2], (TILE, LANES))
                pairs.append(jnp.where(lane_t < HEAD_DIM, a, b))
            return jnp.concatenate(pairs, axis=1)

        g = spread(0)
        beta = spread(GDN_HEADS)
        gcd = _dot_split_rhs(ltri_bd, jnp.concatenate([g, jnp.where(strict_t, g, 0.0)], axis=1), 2)
        yield
        gc = gcd[:, :W]
        decay = jnp.where(causal_t, jnp.exp(jnp.where(causal_t, gcd[:, W:], 0.0)), 0.0)
        egc = jnp.exp(gc)
        kb = kn * beta
        rv = yv * beta
        rk = kb * egc
        qd = qn * egc
        for cc in range(TILE // C):
            rs = slice(cc * C, (cc + 1) * C)
            g_last = gc[(cc + 1) * C - 1:(cc + 1) * C, :]
            kd_ref[slot, rs, :] = (kn[rs] * jnp.exp(g_last - gc[rs])).astype(BF16)
            gl_ref[slot, cc * SUBLANES:(cc + 1) * SUBLANES, :] = (
                jnp.broadcast_to(jnp.exp(g_last), (SUBLANES, W)))
        units = [(slice(cc * C, (cc + 1) * C), slice(LANES * p, LANES * (p + 1)), cc)
                 for cc in range(TILE // C) for p in range(npair)]
        kqs = [lax.dot_general(jnp.concatenate([kn[rs, ls], qn[rs, ls]], axis=0).astype(BF16),
                               stack(kn[rs, ls], first_head).astype(BF16), NT,
                               preferred_element_type=F32) for rs, ls, _ in units]
        yield
        ps = [-jnp.where(strict, kq[0:C] * beta[rs, ls] * decay[rs, ls], 0.0)
              for kq, (rs, ls, _) in zip(kqs, units)]
        ss = [eye2 + p for p in ps]
        ps = [dot(p.astype(BF16), stack(p, first_head).astype(BF16)) for p in ps]
        yield
        nround = int(math.log2(C))
        for k in range(1, nround):
            rhs = [stack(s_, first_head).astype(BF16) for s_ in ss]
            if k + 1 < nround:
                rhs = [jnp.concatenate([stack(p, first_head).astype(BF16), sx], axis=1)
                       for p, sx in zip(ps, rhs)]
            outs = [dot(p.astype(BF16), sx) for p, sx in zip(ps, rhs)]
            if k + 1 < nround:
                ps = [o[:, :LANES] for o in outs]
                ss = [s_ + o[:, LANES:] for s_, o in zip(ss, outs)]
            else:
                ss = [s_ + o for s_, o in zip(ss, outs)]
            yield
        xs = [dot(s_.astype(BF16),
                  stack(jnp.concatenate([rv[rs, ls], rk[rs, ls]], axis=1), first_head2).astype(BF16))
              for s_, (rs, ls, _) in zip(ss, units)]
        yield
        for x, kq, (rs, ls, cc) in zip(xs, kqs, units):
            u_ref[slot, rs, ls] = x[:, :LANES]
            wq_ref[slot, cc, 0:C, ls] = x[:, LANES:].astype(BF16)
            wq_ref[slot, cc, C:2 * C, ls] = qd[rs, ls].astype(BF16)
            a_ref[slot, rs, ls] = (kq[C:2 * C] * decay[rs, ls]).astype(BF16)

    lss = [slice(LANES * p, LANES * (p + 1)) for p in range(npair)]

    def recurrence_stages(slot, seq_start):
        states = [jnp.where(seq_start, 0.0, st_ref[p]) for p in range(npair)]
        pending = None

        def finish(rs, os_):
            sq = jnp.concatenate([(o * o).astype(BF16) for o in os_], axis=0)
            ms_all = dot(sq, pair_ones) * (1.0 / HEAD_DIM)
            for p, (ls, o) in enumerate(zip(lss, os_)):
                ms = ms_all[p * C:(p + 1) * C]
                out_ref[rs, ls] = o * lax.rsqrt(ms + EPS) * nw_ref[:, ls] * sz_ref[rs, ls]

        for cc in range(TILE // C):
            rs = slice(cc * C, (cc + 1) * C)
            wqs = [dot(wq_ref[slot, cc, :, ls], st.astype(BF16)) for ls, st in zip(lss, states)]
            if pending is not None:
                finish(*pending)
            yield
            v_news = [u_ref[slot, rs, ls] - wq[0:C] for ls, wq in zip(lss, wqs)]
            kvs = [lax.dot_general(kd_ref[slot, rs, ls], v.astype(BF16), TN, preferred_element_type=F32)
                   for ls, v in zip(lss, v_news)]
            os_ = [wq[C:2 * C] + dot(a_ref[slot, rs, ls], stack(v, first_head).astype(BF16))
                   for ls, wq, v in zip(lss, wqs, v_news)]
            states = [st * gl_ref[slot, cc * SUBLANES:cc * SUBLANES + 1, ls] + jnp.where(same_head, kv, 0.0)
                      for ls, st, kv in zip(lss, states, kvs)]
            pending = (rs, os_)
            yield
        finish(*pending)
        for p in range(npair):
            st_ref[p] = states[p]

    return solve_stages, recurrence_stages


def _interleave_weighted(*gens_and_weights):
    live = [[gen, weight] for gen, weight in gens_and_weights]
    while live:
        for entry in list(live):
            for _ in range(entry[1]):
                try:
                    next(entry[0])
                except StopIteration:
                    live.remove(entry)
                    break


def _mixers_kernel(relb_ref, qlo_ref, qhi_ref, kb_ref, ksum_ref, va_ref,
                   yq_ref, yk_ref, yv_ref, gb_ref, sz_ref, nw_ref,
                   oT_ref, og_ref,
                   bias_ref, addm_ref, qh_ref, lg_ref, moff_ref,
                   u_ref, wq_ref, a_ref, kd_ref, gl_ref, st_ref,
                   *, nblk, nbatch, tiles_per_seq, nsteps):
    step = pl.program_id(0)
    BLK = MOBA_BLOCK
    half = nblk // 2
    pair = jnp.minimum(step, nsteps - 2)
    hp = pair // (nbatch * (half // 2))
    b = (pair // (half // 2)) % nbatch
    scoring = step < nsteps - 1
    new_kv = (pair % (half // 2)) == 0

    solve_stages, recurrence_stages = _gdn_stages(
        yq_ref, yk_ref, yv_ref, gb_ref, sz_ref, nw_ref, og_ref,
        u_ref, wq_ref, a_ref, kd_ref, gl_ref, st_ref)
    seq_start = ((step - 1) % tiles_per_seq) == 0

    @pl.when((b == 0) & new_kv & scoring)
    def _():
        kk = lax.broadcasted_iota(jnp.int32, (BLK, BLK), 0)
        qq = lax.broadcasted_iota(jnp.int32, (BLK, BLK), 1)
        for hh in range(2):
            h = 2 * hp + hh
            for kind in range(2):
                d = qq - kk + kind * BLK
                val = jnp.full((BLK, BLK), relb_ref[h, REL_BUCKETS - 1], F32)
                for bkt in range(REL_BUCKETS - 2, -1, -1):
                    val = jnp.where(d < BUCKET_LOWER[bkt + 1], relb_ref[h, bkt], val)
                val = val * LOG2E
                if kind == 0:
                    val = jnp.where(d >= 0, val, NEG)
                bias_ref[hh, kind] = val

    def key_means():
        per_tile = INPROJ_TILE // BLK
        ks = ksum_ref[...]
        km = jnp.concatenate([ks[j // per_tile, j % per_tile:j % per_tile + 1, :] for j in range(nblk)],
                             axis=0) * (1.0 / BLK)
        lane = lax.broadcasted_iota(jnp.int32, (nblk, LANES), 1)
        return jnp.concatenate([jnp.where(lane < HEAD_DIM, km, 0.0),
                                jnp.where(lane >= HEAD_DIM, km, 0.0)], axis=0)

    def item_tiles(t):
        i_hi = nblk - 1 - t
        tiles = [(0, t, "own"), (1, i_hi, "own"), (1, i_hi - 1, "prev")]
        if t >= 1:
            tiles.append((0, t - 1, "prev"))
        tiles += [(0, j, "far") for j in range(t - 1)]
        tiles += [(1, j, "far") for j in range(i_hi - 1)]
        assert len(tiles) == nblk + 1
        return tiles

    def score_stages(parity):
        ridx = lax.broadcasted_iota(jnp.int32, (nblk, BLK), 0)
        sub = lax.broadcasted_iota(jnp.int32, (LANES, BLK), 0)
        scale = HEAD_DIM ** -0.5 * LOG2E
        km = key_means()
        for e in range(2):
            t = 2 * parity + e
            slot = 2 * parity + e
            q_of = ((qlo_ref, e, t), (qhi_ref, 1 - e, nblk - 1 - t))
            for s, (q_ref, w, qi) in enumerate(q_of):
                qT = q_ref[0, w]
                gT = jnp.dot(km, qT, precision=HI, preferred_element_type=F32)
                past = ridx < qi
                for hh in range(2):
                    gm = jnp.where(past, gT[nblk * hh:nblk * (hh + 1)], -jnp.inf)
                    cnt = jnp.zeros((nblk, BLK), F32)
                    for jp in range(nblk):
                        row = gm[jp:jp + 1, :]
                        beats = (row > gm) | ((row == gm) & (ridx > jp))
                        cnt = cnt + jnp.where(beats, 1.0, 0.0)
                    visible = past & (cnt < MOBA_TOPK)
                    addm_ref[e, s, nblk * hh:nblk * (hh + 1), :] = jnp.where(visible, 0.0, NEG)
                    in_head = (sub >= HEAD_DIM * hh) & (sub < HEAD_DIM * (hh + 1))
                    qh_ref[e, s, hh] = jnp.where(in_head, qT * scale, 0.0).astype(BF16)
            yield
            tiles = item_tiles(t)
            for hh in range(2):
                cmax = {0: [], 1: []}
                offs = []
                seen = {0: t + 1, 1: nblk - t}
                scores = {s: jnp.dot(kb_ref[0, 0:seen[s] * BLK, :], qh_ref[e, s, hh],
                                     preferred_element_type=F32) for s in (0, 1)}
                yield
                for n, (s, kblk, cls) in enumerate(tiles):
                    lg = scores[s][kblk * BLK:(kblk + 1) * BLK]
                    if cls != "far":
                        lg = lg + bias_ref[hh, 0 if cls == "own" else 1]
                    lg_ref[slot, hh, n] = lg
                    cm = jnp.max(lg, axis=0, keepdims=True)
                    off = None
                    if cls != "own":
                        off = addm_ref[e, s, nblk * hh + kblk:nblk * hh + kblk + 1, :]
                        if cls == "far":
                            off = off + relb_ref[2 * hp + hh, REL_BUCKETS - 1] * LOG2E
                        cm = cm + off
                    cmax[s].append(cm)
                    offs.append(off)
                    yield
                m = {s: functools.reduce(jnp.maximum, cmax[s]) for s in (0, 1)}
                for n, (s, _, _) in enumerate(tiles):
                    moff_ref[slot, hh, n:n + 1, :] = m[s] if offs[n] is None else m[s] - offs[n]

    def softmax_pv_stages(parity):
        for e in range(2):
            t = 2 * parity + e
            slot = 2 * parity + e
            tiles = item_tiles(t)
            for hh in range(2):
                acc = {0: None, 1: None}
                for n, (s, kblk, _) in enumerate(tiles):
                    p = jnp.exp2(lg_ref[slot, hh, n] - moff_ref[slot, hh, n:n + 1, :])
                    pvn = jnp.dot(va_ref[0, kblk, hh], p.astype(BF16),
                                  preferred_element_type=F32)
                    acc[s] = pvn if acc[s] is None else acc[s] + pvn
                    yield
                for s in (0, 1):
                    oT_ref[0, 2 * e + s, HEAD_DIM * hh:HEAD_DIM * (hh + 1), :] = (
                        acc[s][0:HEAD_DIM] / acc[s][HEAD_DIM:HEAD_DIM + 1])

    ATT, GDN = 2, 1

    @pl.when(step == 0)
    def _():
        st_ref[...] = jnp.zeros(st_ref.shape, F32)
        _interleave_weighted((score_stages(0), ATT), (solve_stages(0), GDN))

    for parity in range(2):
        @pl.when((step > 0) & (step < nsteps - 1) & (step % 2 == parity))
        def _(parity=parity):
            _interleave_weighted((recurrence_stages(1 - parity, seq_start), GDN),
                                 (score_stages(parity), ATT),
                                 (solve_stages(parity), GDN),
                                 (softmax_pv_stages(1 - parity), ATT))

    @pl.when(step == nsteps - 1)
    def _():
        last = (nsteps - 2) % 2
        _interleave_weighted((recurrence_stages(last, seq_start), GDN), (softmax_pv_stages(last), ATT))


def _token_mixers(rel_bias, qT, k3, ksum, va, gqkv, sz, gb, nw_row, B, S):
    T = B * S
    W = GDN_WIDTH
    TILE = GDN_TILE
    nblk = S // MOBA_BLOCK
    half = nblk // 2
    assert TILE == GDN_HALF and S % TILE == 0
    assert BUCKET_LOWER[REL_BUCKETS - 1] <= MOBA_BLOCK + 1
    assert nblk == 8 and nblk + 1 <= 2 * SUBLANES
    nchunk = TILE // GDN_CHUNK
    ntiles = T // TILE
    npairs = (ATT_HEADS // 2) * B * (half // 2)
    assert npairs == ntiles
    nsteps = ntiles + 1

    def scored(s):
        p = jnp.minimum(s, npairs - 1)
        return p // (B * (half // 2)), (p // (half // 2)) % B, p % (half // 2)

    def done(s):
        p = jnp.maximum(s - 1, 0)
        return p // (B * (half // 2)), (p // (half // 2)) % B, p % (half // 2)

    def q_lo(s):
        hp, b, m = scored(s)
        return (b, m, hp, 0)

    def q_hi(s):
        hp, b, m = scored(s)
        return (b, half - 1 - m, hp, 0)

    def k_blk(s):
        hp, b, _ = scored(s)
        return (b, 0, hp)

    def v_blk(s):
        hp, b, _ = done(s)
        return (b, 0, hp, 0, 0)

    def o_blk(s):
        hp, b, m = done(s)
        return (b, m, hp, 0)

    cur_tile = lambda off: (lambda s: (jnp.minimum(s, ntiles - 1), off))
    prev_tile = lambda s: (jnp.maximum(s - 1, 0), 0)
    return pl.pallas_call(
        functools.partial(_mixers_kernel, nblk=nblk, nbatch=B, tiles_per_seq=S // TILE, nsteps=nsteps),
        grid=(nsteps,),
        in_specs=[
            pl.BlockSpec(memory_space=pltpu.SMEM),
            pl.BlockSpec((1, 2, LANES, MOBA_BLOCK), q_lo),
            pl.BlockSpec((1, 2, LANES, MOBA_BLOCK), q_hi),
            pl.BlockSpec((1, S, LANES), k_blk),
            pl.BlockSpec((S // INPROJ_TILE, SUBLANES, LANES), k_blk),
            pl.BlockSpec((1, nblk, 2, V_ROWS, MOBA_BLOCK), v_blk),
            pl.BlockSpec((TILE, W), cur_tile(0)),
            pl.BlockSpec((TILE, W), cur_tile(1)),
            pl.BlockSpec((TILE, W), cur_tile(2)),
            pl.BlockSpec((TILE, LANES), cur_tile(0)),
            pl.BlockSpec((TILE, W), prev_tile),
            pl.BlockSpec((1, W), lambda s: (0, 0)),
        ],
        out_specs=[
            pl.BlockSpec((1, 4, LANES, MOBA_BLOCK), o_blk),
            pl.BlockSpec((TILE, W), prev_tile),
        ],
        out_shape=[
            jax.ShapeDtypeStruct((B, nblk, ATT_WIDTH, MOBA_BLOCK), F32),
            jax.ShapeDtypeStruct((T, GDN_WIDTH), F32),
        ],
        scratch_shapes=[
            pltpu.VMEM((2, 2, MOBA_BLOCK, MOBA_BLOCK), F32),
            pltpu.VMEM((2, 2, 2 * nblk, MOBA_BLOCK), F32),
            pltpu.VMEM((2, 2, 2, LANES, MOBA_BLOCK), BF16),
            pltpu.VMEM((4, 2, nblk + 1, MOBA_BLOCK, MOBA_BLOCK), F32),
            pltpu.VMEM((4, 2, 2 * SUBLANES, MOBA_BLOCK), F32),
            pltpu.VMEM((2, TILE, W), F32),
            pltpu.VMEM((2, nchunk, 2 * GDN_CHUNK, W), BF16),
            pltpu.VMEM((2, TILE, W), BF16),
            pltpu.VMEM((2, TILE, W), BF16),
            pltpu.VMEM((2, nchunk * SUBLANES, W), F32),
            pltpu.VMEM((W // LANES, LANES, LANES), F32),
        ],
        compiler_params=pltpu.CompilerParams(
            dimension_semantics=("arbitrary",), vmem_limit_bytes=VMEM_LIMIT),
        name="token_mixers",
    )(rel_bias, qT, qT, k3, ksum, va, gqkv, gqkv, gqkv, gb, sz, nw_row)


def _out_mlp_kernel(x_ref, oTa_ref, oTb_ref, og_ref, woa_ref, wog_ref, pmn_ref, pre_ref, post_ref,
                    wup_ref, wdn_ref, out_ref):
    oT = jnp.concatenate([oTa_ref[0, 0], oTb_ref[0, 0]], axis=1)
    o_att = oT.T.astype(BF16)
    mix = jnp.dot(o_att, woa_ref[...], preferred_element_type=F32)
    mix = mix + jnp.dot(og_ref[...].astype(BF16), wog_ref[...], preferred_element_type=F32)
    x1 = x_ref[...] + _rms(mix, pmn_ref[...])
    h = _rms(x1, pre_ref[...]).astype(BF16)
    acc = jnp.zeros((ROW_TILE, D_MODEL), F32)
    for c in range(D_FF // FF_TILE):
        up = jnp.dot(h, wup_ref[:, c * FF_TILE:(c + 1) * FF_TILE], preferred_element_type=F32)
        act = jnp.square(jnp.maximum(up, 0.0)).astype(BF16)
        acc = acc + jnp.dot(act, wdn_ref[c * FF_TILE:(c + 1) * FF_TILE, :], preferred_element_type=F32)
    out_ref[...] = x1 + _rms(acc, post_ref[...])


def _out_mlp(xf, oT, og, woa, wog, pmn, pre, post, wup, wdn, B, S):
    T = B * S
    nblk = S // MOBA_BLOCK
    tiles_per_seq = S // ROW_TILE
    assert ROW_TILE == 2 * MOBA_BLOCK
    const = lambda i: (0, 0)
    row = lambda i: (i, 0)

    def att_block(which):
        def index(i):
            blk = 2 * (i % tiles_per_seq) + which
            return (i // tiles_per_seq, _paired_pos(blk, nblk), 0, 0)
        return index

    single = dict(pipeline_mode=pl.Buffered(1))
    return pl.pallas_call(
        _out_mlp_kernel,
        grid=(T // ROW_TILE,),
        in_specs=[
            pl.BlockSpec((ROW_TILE, D_MODEL), row),
            pl.BlockSpec((1, 1, ATT_WIDTH, MOBA_BLOCK), att_block(0)),
            pl.BlockSpec((1, 1, ATT_WIDTH, MOBA_BLOCK), att_block(1)),
            pl.BlockSpec((ROW_TILE, GDN_WIDTH), row),
            pl.BlockSpec(woa.shape, const, **single),
            pl.BlockSpec(wog.shape, const, **single),
            pl.BlockSpec((1, D_MODEL), const),
            pl.BlockSpec((1, D_MODEL), const),
            pl.BlockSpec((1, D_MODEL), const),
            pl.BlockSpec(wup.shape, const, **single),
            pl.BlockSpec(wdn.shape, const, **single),
        ],
        out_specs=pl.BlockSpec((ROW_TILE, D_MODEL), row),
        out_shape=jax.ShapeDtypeStruct((T, D_MODEL), F32),
        compiler_params=pltpu.CompilerParams(
            dimension_semantics=("arbitrary",), vmem_limit_bytes=VMEM_LIMIT),
        name="out_mlp",
    )(xf, oT, oT, og, woa, wog, pmn, pre, post, wup, wdn)


def kernel(x, w_in, w_out, conv_w, A_log, dt_bias, gdn_norm_w, rel_bias, pre_mix_norm,
           post_mix_norm, pre_mlp_norm, post_mlp_norm, w_up, w_down):
    B, S, D = x.shape
    assert D == D_MODEL and S % ROW_TILE == 0 and S % MOBA_BLOCK == 0
    T = B * S
    depth = w_in.shape[0]
    xf = x.reshape(T, D)
    o0, o1, o2, o3, o4 = 0, ATT_WIDTH, 2 * ATT_WIDTH, 3 * ATT_WIDTH, 3 * ATT_WIDTH + 3 * GDN_WIDTH
    o5 = o4 + GDN_WIDTH
    for l in range(depth):
        wi = w_in[l]
        wqT = wi[:, o0:o1].T.astype(BF16)
        wk = wi[:, o1:o2].astype(BF16)
        wvT = wi[:, o2:o3].T.astype(BF16)
        wg = wi[:, o3:o4].astype(BF16)
        wz = wi[:, o4:o5].astype(BF16)
        wab = jnp.pad(wi[:, o5:], ((0, 0), (0, LANES - 2 * GDN_HEADS))).astype(BF16)
        pad8 = lambda v: jnp.pad(v.astype(F32), (0, LANES - GDN_HEADS))[None, :]
        qT, k, ksum, va, gqkv, sz, gb = _inproj(xf, pre_mix_norm[l][None, :], wqT, wk, wvT, wg, wz, wab,
                                                conv_w[l], pad8(A_log[l]), pad8(dt_bias[l]), B, S)
        oT, og = _token_mixers(rel_bias.astype(F32), qT, k.reshape(B, S, ATT_WIDTH), ksum, va, gqkv, sz,
                               gb, jnp.tile(gdn_norm_w[l], GDN_HEADS)[None, :], B, S)
        wo = w_out[l].astype(BF16)
        xf = _out_mlp(xf, oT, og, wo[:ATT_WIDTH], wo[ATT_WIDTH:], post_mix_norm[l][None, :],
                      pre_mlp_norm[l][None, :], post_mlp_norm[l][None, :],
                      w_up[l].astype(BF16), w_down[l].astype(BF16), B, S)
    return xf.reshape(B, S, D)
```

```python
import functools
import math

import jax
import jax.numpy as jnp
from jax import lax
from jax.experimental import pallas as pl
from jax.experimental.pallas import tpu as pltpu

F32 = jnp.float32
BF16 = jnp.bfloat16
HI = lax.Precision.HIGHEST

D_MODEL = 1024
HEAD_DIM = 64
ATT_HEADS = 8
GDN_HEADS = 8
ATT_WIDTH = ATT_HEADS * HEAD_DIM
GDN_WIDTH = GDN_HEADS * HEAD_DIM
MOBA_BLOCK = 256
MOBA_TOPK = 3
GDN_CHUNK = 64
CONV_WIDTH = 4
D_FF = 4 * D_MODEL
REL_BUCKETS = 32
REL_MAX_EXACT = 16
REL_MAX_DIST = 128
EPS = 1e-6
NEG = -1e30
LOG2E = math.log2(math.e)

LANES = 128
SUBLANES = 8
VMEM_LIMIT = 56 * 1024 * 1024
ROW_TILE = 512
INPROJ_TILE = 1024
FF_TILE = 1024

NT = (((1,), (1,)), ((), ()))
TN = (((0,), (0,)), ((), ()))


def _bucket_lower_bounds():
    def bucket(d):
        if d < REL_MAX_EXACT:
            return d
        t = math.log(d / REL_MAX_EXACT) / math.log(REL_MAX_DIST / REL_MAX_EXACT)
        t = t * (REL_BUCKETS - REL_MAX_EXACT)
        assert d in (REL_MAX_EXACT, REL_MAX_DIST) or abs(t - round(t)) > 1e-6
        return min(REL_MAX_EXACT + int(t + 1e-9), REL_BUCKETS - 1)
    lower = []
    for b in range(REL_BUCKETS):
        d = 0
        while bucket(d) < b:
            d += 1
        lower.append(d)
    return lower


BUCKET_LOWER = _bucket_lower_bounds()


def _sigmoid(x):
    return 0.5 * jnp.tanh(0.5 * x) + 0.5


def _silu_of_half(h):
    return h + h * jnp.tanh(h)


def _rms(x, w):
    return x * lax.rsqrt(jnp.mean(x * x, axis=-1, keepdims=True) + EPS) * w


def _split_bf16(x, parts):
    out = []
    for _ in range(parts):
        h = x.astype(BF16)
        out.append(h)
        x = x - h.astype(F32)
    return out


def _dot_split_rhs(c, x, parts):
    acc = None
    for h in _split_bf16(x, parts):
        d = jnp.dot(c, h, preferred_element_type=F32)
        acc = d if acc is None else acc + d
    return acc


CONV_COLS = 512


def _inproj_kernel(x_ref, xp_ref, nw_ref, wqT_ref, wk_ref, wvT_ref, wg_ref, wz_ref, wab_ref,
                   cw_ref, alog_ref, dtb_ref,
                   qT_ref, k_ref, ksum_ref, va_ref, g_ref, z_ref, gb_ref, *, tiles_per_seq):
    h = _rms(x_ref[...], nw_ref[...]).astype(BF16)

    hp = _rms(xp_ref[...], nw_ref[...]).astype(BF16)
    seq_start = (pl.program_id(0) % tiles_per_seq) == 0
    trow8 = lax.broadcasted_iota(jnp.int32, (SUBLANES, CONV_COLS), 0)
    for c in range(3 * GDN_WIDTH // CONV_COLS):
        cols = slice(c * CONV_COLS, (c + 1) * CONV_COLS)
        cur = jnp.dot(h, wg_ref[:, cols], preferred_element_type=F32)
        prev8 = jnp.dot(hp, wg_ref[:, cols], preferred_element_type=F32)
        prev8 = jnp.where(seq_start, 0.0, prev8)
        cw_half = 0.5 * cw_ref[:, cols]
        acc = cur * cw_half[CONV_WIDTH - 1:CONV_WIDTH]
        for s in range(1, CONV_WIDTH):
            rolled = pltpu.roll(cur, s, 0)
            top = jnp.where(trow8 < s, pltpu.roll(prev8, s, 0), rolled[0:SUBLANES])
            tap = jnp.concatenate([top, rolled[SUBLANES:]], axis=0)
            acc = acc + tap * cw_half[CONV_WIDTH - 1 - s:CONV_WIDTH - s]
        g_ref[:, cols] = _silu_of_half(acc)

    z = jnp.dot(h, wz_ref[...], preferred_element_type=F32)
    z_ref[...] = _silu_of_half(0.5 * z)
    ab = jnp.dot(h, wab_ref[...], preferred_element_type=F32)
    xs = ab + dtb_ref[...]
    log_decay = -jnp.exp(alog_ref[...]) * (jnp.maximum(xs, 0.0) + jnp.log1p(jnp.exp(-jnp.abs(xs))))
    lane = lax.broadcasted_iota(jnp.int32, ab.shape, 1)
    gb_ref[...] = jnp.where(lane < GDN_HEADS, log_decay, _sigmoid(ab))

    qT = lax.dot_general(wqT_ref[...], h, NT, preferred_element_type=F32)
    vT = lax.dot_general(wvT_ref[...], h, NT, preferred_element_type=F32)
    k = jnp.dot(h, wk_ref[...], preferred_element_type=F32)
    k_ref[...] = k.astype(BF16)
    ones_row = jnp.where(lax.broadcasted_iota(jnp.int32, (V_ROWS - HEAD_DIM, MOBA_BLOCK), 0) == 0,
                         1.0, 0.0).astype(BF16)
    ksum_ref[...] = jnp.zeros(ksum_ref.shape, F32)
    for t in range(INPROJ_TILE // MOBA_BLOCK):
        blk = slice(t * MOBA_BLOCK, (t + 1) * MOBA_BLOCK)
        qT_ref[0, t] = qT[:, blk]
        ksum_ref[0, t:t + 1, :] = jnp.sum(k[blk], axis=0, keepdims=True)
        for hh in range(ATT_HEADS):
            va_ref[0, t, hh, 0:HEAD_DIM, :] = vT[HEAD_DIM * hh:HEAD_DIM * (hh + 1), blk].astype(BF16)
            va_ref[0, t, hh, HEAD_DIM:V_ROWS, :] = ones_row


def _inproj(xf, nw, wqT, wk, wvT, wg, wz, wab, conv_w, alog_pad, dtb_pad, B, S):
    T = B * S
    TM = INPROJ_TILE
    assert S % TM == 0
    nblk = S // MOBA_BLOCK
    tiles_per_seq = S // TM
    blk_per_tile = TM // MOBA_BLOCK
    const = lambda i: (0, 0)
    row = lambda i: (i, 0)
    tr = lambda i: (i // tiles_per_seq, i % tiles_per_seq, 0, 0)
    prev_rows = lambda i: (jnp.maximum(i * (TM // SUBLANES) - 1, 0), 0)
    single = dict(pipeline_mode=pl.Buffered(1))
    return pl.pallas_call(
        functools.partial(_inproj_kernel, tiles_per_seq=tiles_per_seq),
        grid=(T // TM,),
        in_specs=[
            pl.BlockSpec((TM, D_MODEL), row),
            pl.BlockSpec((SUBLANES, D_MODEL), prev_rows),
            pl.BlockSpec((1, D_MODEL), const),
            pl.BlockSpec(wqT.shape, const, **single),
            pl.BlockSpec(wk.shape, const, **single),
            pl.BlockSpec(wvT.shape, const, **single),
            pl.BlockSpec(wg.shape, const, **single),
            pl.BlockSpec(wz.shape, const, **single),
            pl.BlockSpec(wab.shape, const, **single),
            pl.BlockSpec(conv_w.shape, const),
            pl.BlockSpec((1, LANES), const),
            pl.BlockSpec((1, LANES), const),
        ],
        out_specs=[
            pl.BlockSpec((1, blk_per_tile, ATT_WIDTH, MOBA_BLOCK), tr),
            pl.BlockSpec((TM, ATT_WIDTH), row),
            pl.BlockSpec((1, SUBLANES, ATT_WIDTH), lambda i: (i, 0, 0)),
            pl.BlockSpec((1, blk_per_tile, ATT_HEADS, V_ROWS, MOBA_BLOCK),
                         lambda i: (i // tiles_per_seq, i % tiles_per_seq, 0, 0, 0)),
            pl.BlockSpec((TM, 3 * GDN_WIDTH), row),
            pl.BlockSpec((TM, GDN_WIDTH), row),
            pl.BlockSpec((TM, LANES), row),
        ],
        out_shape=[
            jax.ShapeDtypeStruct((B, nblk, ATT_WIDTH, MOBA_BLOCK), F32),
            jax.ShapeDtypeStruct((T, ATT_WIDTH), BF16),
            jax.ShapeDtypeStruct((T // TM, SUBLANES, ATT_WIDTH), F32),
            jax.ShapeDtypeStruct((B, nblk, ATT_HEADS, V_ROWS, MOBA_BLOCK), BF16),
            jax.ShapeDtypeStruct((T, 3 * GDN_WIDTH), F32),
            jax.ShapeDtypeStruct((T, GDN_WIDTH), F32),
            jax.ShapeDtypeStruct((T, LANES), F32),
        ],
        compiler_params=pltpu.CompilerParams(
            dimension_semantics=("arbitrary",), vmem_limit_bytes=VMEM_LIMIT),
        name="inproj",
    )(xf, xf, nw, wqT, wk, wvT, wg, wz, wab, conv_w, alog_pad, dtb_pad)


V_ROWS = HEAD_DIM + 16


def _paired_pos(i, nblk):
    return jnp.where(i < nblk // 2, 2 * i, 2 * (nblk - 1 - i) + 1)


GDN_TILE = 256
GDN_HALF = 2 * LANES


def _gdn_stages(yq_ref, yk_ref, yv_ref, gb_ref, sz_ref, nw_ref, out_ref,
                u_ref, wq_ref, a_ref, kd_ref, gl_ref, st_ref):
    C = GDN_CHUNK
    W = GDN_WIDTH
    TILE = GDN_TILE
    npair = W // LANES

    r_w = lax.broadcasted_iota(jnp.int32, (GDN_HALF, GDN_HALF), 0)
    c_w = lax.broadcasted_iota(jnp.int32, (GDN_HALF, GDN_HALF), 1)
    head_ones = jnp.where((r_w // HEAD_DIM) == (c_w // HEAD_DIM), 1.0, 0.0).astype(BF16)
    ltri_bd = jnp.where(((r_w // C) == (c_w // C)) & (c_w <= r_w), 1.0, 0.0).astype(BF16)
    tok = lax.broadcasted_iota(jnp.int32, (TILE, W), 0) % C
    col = lax.broadcasted_iota(jnp.int32, (TILE, W), 1) % HEAD_DIM
    causal_t = tok >= col
    strict_t = tok > col
    lane_t = lax.broadcasted_iota(jnp.int32, (TILE, LANES), 1)

    lane = lax.broadcasted_iota(jnp.int32, (C, LANES), 1)
    rowi = lax.broadcasted_iota(jnp.int32, (C, LANES), 0)
    first_head = lane < HEAD_DIM
    strict = rowi > (lane % HEAD_DIM)
    eye2 = jnp.where(rowi == (lane % HEAD_DIM), 1.0, 0.0)
    lane2 = lax.broadcasted_iota(jnp.int32, (C, 2 * LANES), 1)
    first_head2 = (lane2 % LANES) < HEAD_DIM
    r_l = lax.broadcasted_iota(jnp.int32, (LANES, LANES), 0)
    c_l = lax.broadcasted_iota(jnp.int32, (LANES, LANES), 1)
    same_head = (r_l // HEAD_DIM) == (c_l // HEAD_DIM)
    pair_ones = jnp.where(same_head, 1.0, 0.0).astype(BF16)

    def stack(x, mask):
        return jnp.concatenate([jnp.where(mask, x, 0.0), jnp.where(mask, 0.0, x)], axis=0)

    dot = functools.partial(jnp.dot, preferred_element_type=F32)

    def head_sumsq(ys):
        halves = [(y * y).astype(BF16)[:, h:h + GDN_HALF] for y in ys for h in range(0, W, GDN_HALF)]
        sums = dot(jnp.concatenate(halves, axis=0), head_ones)
        per = W // GDN_HALF
        return [jnp.concatenate([sums[(i * per + j) * TILE:(i * per + j + 1) * TILE] for j in range(per)],
                                axis=1) for i in range(len(ys))]

    def solve_stages(slot):
        yq = yq_ref[...]
        yk = yk_ref[...]
        yv = yv_ref[...]
        ssq, ssk = head_sumsq([yq, yk])
        qn = yq * lax.rsqrt(ssq + EPS) * (HEAD_DIM ** -0.5)
        kn = yk * lax.rsqrt(ssk + EPS)
        yield
        gbt = gb_ref[...]

        def spread(col0):
            pairs = []
            for p in range(npair):
                a = jnp.broadcast_to(gbt[:, col0 + 2 * p:col0 + 2 * p + 1], (TILE, LANES))
                b = jnp.broadcast_to(gbt[:, col0 + 2 * p + 1:col0 + 2 * p + 2], (TILE, LANES))
                pairs.append(jnp.where(lane_t < HEAD_DIM, a, b))
            return jnp.concatenate(pairs, axis=1)

        g = spread(0)
        beta = spread(GDN_HEADS)
        gcd = _dot_split_rhs(ltri_bd, jnp.concatenate([g, jnp.where(strict_t, g, 0.0)], axis=1), 2)
        yield
        gc = gcd[:, :W]
        decay = jnp.where(causal_t, jnp.exp(jnp.where(causal_t, gcd[:, W:], 0.0)), 0.0)
        egc = jnp.exp(gc)
        kb = kn * beta
        rv = yv * beta
        rk = kb * egc
        qd = qn * egc
        for cc in range(TILE // C):
            rs = slice(cc * C, (cc + 1) * C)
            g_last = gc[(cc + 1) * C - 1:(cc + 1) * C, :]
            kd_ref[slot, rs, :] = (kn[rs] * jnp.exp(g_last - gc[rs])).astype(BF16)
            gl_ref[slot, cc * SUBLANES:(cc + 1) * SUBLANES, :] = (
                jnp.broadcast_to(jnp.exp(g_last), (SUBLANES, W)))
        units = [(slice(cc * C, (cc + 1) * C), slice(LANES * p, LANES * (p + 1)), cc)
                 for cc in range(TILE // C) for p in range(npair)]
        kqs = [lax.dot_general(jnp.concatenate([kn[rs, ls], qn[rs, ls]], axis=0).astype(BF16),
                               stack(kn[rs, ls], first_head).astype(BF16), NT,
                               preferred_element_type=F32) for rs, ls, _ in units]
        yield
        ps = [-jnp.where(strict, kq[0:C] * beta[rs, ls] * decay[rs, ls], 0.0)
              for kq, (rs, ls, _) in zip(kqs, units)]
        ss = [eye2 + p for p in ps]
        ps = [dot(p.astype(BF16), stack(p, first_head).astype(BF16)) for p in ps]
        yield
        nround = int(math.log2(C))
        for k in range(1, nround):
            rhs = [stack(s_, first_head).astype(BF16) for s_ in ss]
            if k + 1 < nround:
                rhs = [jnp.concatenate([stack(p, first_head).astype(BF16), sx], axis=1)
                       for p, sx in zip(ps, rhs)]
            outs = [dot(p.astype(BF16), sx) for p, sx in zip(ps, rhs)]
            if k + 1 < nround:
                ps = [o[:, :LANES] for o in outs]
                ss = [s_ + o[:, LANES:] for s_, o in zip(ss, outs)]
            else:
                ss = [s_ + o for s_, o in zip(ss, outs)]
            yield
        xs = [dot(s_.astype(BF16),
                  stack(jnp.concatenate([rv[rs, ls], rk[rs, ls]], axis=1), first_head2).astype(BF16))
              for s_, (rs, ls, _) in zip(ss, units)]
        yield
        for x, kq, (rs, ls, cc) in zip(xs, kqs, units):
            u_ref[slot, rs, ls] = x[:, :LANES]
            wq_ref[slot, cc, 0:C, ls] = x[:, LANES:].astype(BF16)
            wq_ref[slot, cc, C:2 * C, ls] = qd[rs, ls].astype(BF16)
            a_ref[slot, rs, ls] = (kq[C:2 * C] * decay[rs, ls]).astype(BF16)

    lss = [slice(LANES * p, LANES * (p + 1)) for p in range(npair)]

    def recurrence_stages(slot, seq_start):
        states = [jnp.where(seq_start, 0.0, st_ref[p]) for p in range(npair)]
        pending = None

        def finish(rs, os_):
            sq = jnp.concatenate([(o * o).astype(BF16) for o in os_], axis=0)
            ms_all = dot(sq, pair_ones) * (1.0 / HEAD_DIM)
            for p, (ls, o) in enumerate(zip(lss, os_)):
                ms = ms_all[p * C:(p + 1) * C]
                out_ref[rs, ls] = o * lax.rsqrt(ms + EPS) * nw_ref[:, ls] * sz_ref[rs, ls]

        for cc in range(TILE // C):
            rs = slice(cc * C, (cc + 1) * C)
            wqs = [dot(wq_ref[slot, cc, :, ls], st.astype(BF16)) for ls, st in zip(lss, states)]
            if pending is not None:
                finish(*pending)
            yield
            v_news = [u_ref[slot, rs, ls] - wq[0:C] for ls, wq in zip(lss, wqs)]
            kvs = [lax.dot_general(kd_ref[slot, rs, ls], v.astype(BF16), TN, preferred_element_type=F32)
                   for ls, v in zip(lss, v_news)]
            os_ = [wq[C:2 * C] + dot(a_ref[slot, rs, ls], stack(v, first_head).astype(BF16))
                   for ls, wq, v in zip(lss, wqs, v_news)]
            states = [st * gl_ref[slot, cc * SUBLANES:cc * SUBLANES + 1, ls] + jnp.where(same_head, kv, 0.0)
                      for ls, st, kv in zip(lss, states, kvs)]
            pending = (rs, os_)
            yield
        finish(*pending)
        for p in range(npair):
            st_ref[p] = states[p]

    return solve_stages, recurrence_stages


def _interleave_weighted(*gens_and_weights):
    live = [[gen, weight] for gen, weight in gens_and_weights]
    while live:
        for entry in list(live):
            for _ in range(entry[1]):
                try:
                    next(entry[0])
                except StopIteration:
                    live.remove(entry)
                    break


def _mixers_kernel(relb_ref, qlo_ref, qhi_ref, kb_ref, ksum_ref, va_ref,
                   yq_ref, yk_ref, yv_ref, gb_ref, sz_ref, nw_ref,
                   oT_ref, og_ref,
                   bias_ref, addm_ref, qh_ref, lg_ref, moff_ref,
                   u_ref, wq_ref, a_ref, kd_ref, gl_ref, st_ref,
                   *, nblk, nbatch, tiles_per_seq, nsteps):
    step = pl.program_id(0)
    BLK = MOBA_BLOCK
    half = nblk // 2
    pair = jnp.minimum(step, nsteps - 2)
    hp = pair // (nbatch * (half // 2))
    b = (pair // (half // 2)) % nbatch
    scoring = step < nsteps - 1
    new_kv = (pair % (half // 2)) == 0

    solve_stages, recurrence_stages = _gdn_stages(
        yq_ref, yk_ref, yv_ref, gb_ref, sz_ref, nw_ref, og_ref,
        u_ref, wq_ref, a_ref, kd_ref, gl_ref, st_ref)
    seq_start = ((step - 1) % tiles_per_seq) == 0

    @pl.when((b == 0) & new_kv & scoring)
    def _():
        kk = lax.broadcasted_iota(jnp.int32, (BLK, BLK), 0)
        qq = lax.broadcasted_iota(jnp.int32, (BLK, BLK), 1)
        for hh in range(2):
            h = 2 * hp + hh
            for kind in range(2):
                d = qq - kk + kind * BLK
                val = jnp.full((BLK, BLK), relb_ref[h, REL_BUCKETS - 1], F32)
                for bkt in range(REL_BUCKETS - 2, -1, -1):
                    val = jnp.where(d < BUCKET_LOWER[bkt + 1], relb_ref[h, bkt], val)
                val = val * LOG2E
                if kind == 0:
                    val = jnp.where(d >= 0, val, NEG)
                bias_ref[hh, kind] = val

    def key_means():
        per_tile = INPROJ_TILE // BLK
        ks = ksum_ref[...]
        km = jnp.concatenate([ks[j // per_tile, j % per_tile:j % per_tile + 1, :] for j in range(nblk)],
                             axis=0) * (1.0 / BLK)
        lane = lax.broadcasted_iota(jnp.int32, (nblk, LANES), 1)
        return jnp.concatenate([jnp.where(lane < HEAD_DIM, km, 0.0),
                                jnp.where(lane >= HEAD_DIM, km, 0.0)], axis=0)

    def item_tiles(t):
        i_hi = nblk - 1 - t
        tiles = [(0, t, "own"), (1, i_hi, "own"), (1, i_hi - 1, "prev")]
        if t >= 1:
            tiles.append((0, t - 1, "prev"))
        tiles += [(0, j, "far") for j in range(t - 1)]
        tiles += [(1, j, "far") for j in range(i_hi - 1)]
        assert len(tiles) == nblk + 1
        return tiles

    def score_stages(parity):
        ridx = lax.broadcasted_iota(jnp.int32, (nblk, BLK), 0)
        sub = lax.broadcasted_iota(jnp.int32, (LANES, BLK), 0)
        scale = HEAD_DIM ** -0.5 * LOG2E
        km = key_means()
        for e in range(2):
            t = 2 * parity + e
            slot = 2 * parity + e
            q_of = ((qlo_ref, e, t), (qhi_ref, 1 - e, nblk - 1 - t))
            for s, (q_ref, w, qi) in enumerate(q_of):
                qT = q_ref[0, w]
                gT = jnp.dot(km, qT, precision=HI, preferred_element_type=F32)
                past = ridx < qi
                for hh in range(2):
                    gm = jnp.where(past, gT[nblk * hh:nblk * (hh + 1)], -jnp.inf)
                    cnt = jnp.zeros((nblk, BLK), F32)
                    for jp in range(nblk):
                        row = gm[jp:jp + 1, :]
                        beats = (row > gm) | ((row == gm) & (ridx > jp))
                        cnt = cnt + jnp.where(beats, 1.0, 0.0)
                    visible = past & (cnt < MOBA_TOPK)
                    addm_ref[e, s, nblk * hh:nblk * (hh + 1), :] = jnp.where(visible, 0.0, NEG)
                    in_head = (sub >= HEAD_DIM * hh) & (sub < HEAD_DIM * (hh + 1))
                    qh_ref[e, s, hh] = jnp.where(in_head, qT * scale, 0.0).astype(BF16)
            yield
            tiles = item_tiles(t)
            for hh in range(2):
                cmax = {0: [], 1: []}
                offs = []
                for n, (s, kblk, cls) in enumerate(tiles):
                    lg = jnp.dot(kb_ref[0, kblk * BLK:(kblk + 1) * BLK, :], qh_ref[e, s, hh],
                                 preferred_element_type=F32)
                    if cls != "far":
                        lg = lg + bias_ref[hh, 0 if cls == "own" else 1]
                    lg_ref[slot, hh, n] = lg
                    cm = jnp.max(lg, axis=0, keepdims=True)
                    off = None
                    if cls != "own":
                        off = addm_ref[e, s, nblk * hh + kblk:nblk * hh + kblk + 1, :]
                        if cls == "far":
                            off = off + relb_ref[2 * hp + hh, REL_BUCKETS - 1] * LOG2E
                        cm = cm + off
                    cmax[s].append(cm)
                    offs.append(off)
                    yield
                m = {s: functools.reduce(jnp.maximum, cmax[s]) for s in (0, 1)}
                for n, (s, _, _) in enumerate(tiles):
                    moff_ref[slot, hh, n:n + 1, :] = m[s] if offs[n] is None else m[s] - offs[n]

    def softmax_pv_stages(parity):
        for e in range(2):
            t = 2 * parity + e
            slot = 2 * parity + e
            tiles = item_tiles(t)
            for hh in range(2):
                acc = {0: None, 1: None}
                for n, (s, kblk, _) in enumerate(tiles):
                    p = jnp.exp2(lg_ref[slot, hh, n] - moff_ref[slot, hh, n:n + 1, :])
                    pvn = jnp.dot(va_ref[0, kblk, hh], p.astype(BF16),
                                  preferred_element_type=F32)
                    acc[s] = pvn if acc[s] is None else acc[s] + pvn
                    yield
                for s in (0, 1):
                    oT_ref[0, 2 * e + s, HEAD_DIM * hh:HEAD_DIM * (hh + 1), :] = (
                        acc[s][0:HEAD_DIM] / acc[s][HEAD_DIM:HEAD_DIM + 1])

    ATT, GDN = 2, 1

    @pl.when(step == 0)
    def _():
        st_ref[...] = jnp.zeros(st_ref.shape, F32)
        _interleave_weighted((score_stages(0), ATT), (solve_stages(0), GDN))

    for parity in range(2):
        @pl.when((step > 0) & (step < nsteps - 1) & (step % 2 == parity))
        def _(parity=parity):
            _interleave_weighted((recurrence_stages(1 - parity, seq_start), GDN),
                                 (score_stages(parity), ATT),
                                 (solve_stages(parity), GDN),
                                 (softmax_pv_stages(1 - parity), ATT))

    @pl.when(step == nsteps - 1)
    def _():
        last = (nsteps - 2) % 2
        _interleave_weighted((recurrence_stages(last, seq_start), GDN), (softmax_pv_stages(last), ATT))


def _token_mixers(rel_bias, qT, k3, ksum, va, gqkv, sz, gb, nw_row, B, S):
    T = B * S
    W = GDN_WIDTH
    TILE = GDN_TILE
    nblk = S // MOBA_BLOCK
    half = nblk // 2
    assert TILE == GDN_HALF and S % TILE == 0
    assert BUCKET_LOWER[REL_BUCKETS - 1] <= MOBA_BLOCK + 1
    assert nblk == 8 and nblk + 1 <= 2 * SUBLANES
    nchunk = TILE // GDN_CHUNK
    ntiles = T // TILE
    npairs = (ATT_HEADS // 2) * B * (half // 2)
    assert npairs == ntiles
    nsteps = ntiles + 1

    def scored(s):
        p = jnp.minimum(s, npairs - 1)
        return p // (B * (half // 2)), (p // (half // 2)) % B, p % (half // 2)

    def done(s):
        p = jnp.maximum(s - 1, 0)
        return p // (B * (half // 2)), (p // (half // 2)) % B, p % (half // 2)

    def q_lo(s):
        hp, b, m = scored(s)
        return (b, m, hp, 0)

    def q_hi(s):
        hp, b, m = scored(s)
        return (b, half - 1 - m, hp, 0)

    def k_blk(s):
        hp, b, _ = scored(s)
        return (b, 0, hp)

    def v_blk(s):
        hp, b, _ = done(s)
        return (b, 0, hp, 0, 0)

    def o_blk(s):
        hp, b, m = done(s)
        return (b, m, hp, 0)

    cur_tile = lambda off: (lambda s: (jnp.minimum(s, ntiles - 1), off))
    prev_tile = lambda s: (jnp.maximum(s - 1, 0), 0)
    return pl.pallas_call(
        functools.partial(_mixers_kernel, nblk=nblk, nbatch=B, tiles_per_seq=S // TILE, nsteps=nsteps),
        grid=(nsteps,),
        in_specs=[
            pl.BlockSpec(memory_space=pltpu.SMEM),
            pl.BlockSpec((1, 2, LANES, MOBA_BLOCK), q_lo),
            pl.BlockSpec((1, 2, LANES, MOBA_BLOCK), q_hi),
            pl.BlockSpec((1, S, LANES), k_blk),
            pl.BlockSpec((S // INPROJ_TILE, SUBLANES, LANES), k_blk),
            pl.BlockSpec((1, nblk, 2, V_ROWS, MOBA_BLOCK), v_blk),
            pl.BlockSpec((TILE, W), cur_tile(0)),
            pl.BlockSpec((TILE, W), cur_tile(1)),
            pl.BlockSpec((TILE, W), cur_tile(2)),
            pl.BlockSpec((TILE, LANES), cur_tile(0)),
            pl.BlockSpec((TILE, W), prev_tile),
            pl.BlockSpec((1, W), lambda s: (0, 0)),
        ],
        out_specs=[
            pl.BlockSpec((1, 4, LANES, MOBA_BLOCK), o_blk),
            pl.BlockSpec((TILE, W), prev_tile),
        ],
        out_shape=[
            jax.ShapeDtypeStruct((B, nblk, ATT_WIDTH, MOBA_BLOCK), F32),
            jax.ShapeDtypeStruct((T, GDN_WIDTH), F32),
        ],
        scratch_shapes=[
            pltpu.VMEM((2, 2, MOBA_BLOCK, MOBA_BLOCK), F32),
            pltpu.VMEM((2, 2, 2 * nblk, MOBA_BLOCK), F32),
            pltpu.VMEM((2, 2, 2, LANES, MOBA_BLOCK), BF16),
            pltpu.VMEM((4, 2, nblk + 1, MOBA_BLOCK, MOBA_BLOCK), F32),
            pltpu.VMEM((4, 2, 2 * SUBLANES, MOBA_BLOCK), F32),
            pltpu.VMEM((2, TILE, W), F32),
            pltpu.VMEM((2, nchunk, 2 * GDN_CHUNK, W), BF16),
            pltpu.VMEM((2, TILE, W), BF16),
            pltpu.VMEM((2, TILE, W), BF16),
            pltpu.VMEM((2, nchunk * SUBLANES, W), F32),
            pltpu.VMEM((W // LANES, LANES, LANES), F32),
        ],
        compiler_params=pltpu.CompilerParams(
            dimension_semantics=("arbitrary",), vmem_limit_bytes=VMEM_LIMIT),
        name="token_mixers",
    )(rel_bias, qT, qT, k3, ksum, va, gqkv, gqkv, gqkv, gb, sz, nw_row)


def _out_mlp_kernel(x_ref, oTa_ref, oTb_ref, og_ref, woa_ref, wog_ref, pmn_ref, pre_ref, post_ref,
                    wup_ref, wdn_ref, out_ref):
    oT = jnp.concatenate([oTa_ref[0, 0], oTb_ref[0, 0]], axis=1)
    o_att = oT.T.astype(BF16)
    mix = jnp.dot(o_att, woa_ref[...], preferred_element_type=F32)
    mix = mix + jnp.dot(og_ref[...].astype(BF16), wog_ref[...], preferred_element_type=F32)
    x1 = x_ref[...] + _rms(mix, pmn_ref[...])
    h = _rms(x1, pre_ref[...]).astype(BF16)
    acc = jnp.zeros((ROW_TILE, D_MODEL), F32)
    for c in range(D_FF // FF_TILE):
        up = jnp.dot(h, wup_ref[:, c * FF_TILE:(c + 1) * FF_TILE], preferred_element_type=F32)
        act = jnp.square(jnp.maximum(up, 0.0)).astype(BF16)
        acc = acc + jnp.dot(act, wdn_ref[c * FF_TILE:(c + 1) * FF_TILE, :], preferred_element_type=F32)
    out_ref[...] = x1 + _rms(acc, post_ref[...])


def _out_mlp(xf, oT, og, woa, wog, pmn, pre, post, wup, wdn, B, S):
    T = B * S
    nblk = S // MOBA_BLOCK
    tiles_per_seq = S // ROW_TILE
    assert ROW_TILE == 2 * MOBA_BLOCK
    const = lambda i: (0, 0)
    row = lambda i: (i, 0)

    def att_block(which):
        def index(i):
            blk = 2 * (i % tiles_per_seq) + which
            return (i // tiles_per_seq, _paired_pos(blk, nblk), 0, 0)
        return index

    single = dict(pipeline_mode=pl.Buffered(1))
    return pl.pallas_call(
        _out_mlp_kernel,
        grid=(T // ROW_TILE,),
        in_specs=[
            pl.BlockSpec((ROW_TILE, D_MODEL), row),
            pl.BlockSpec((1, 1, ATT_WIDTH, MOBA_BLOCK), att_block(0)),
            pl.BlockSpec((1, 1, ATT_WIDTH, MOBA_BLOCK), att_block(1)),
            pl.BlockSpec((ROW_TILE, GDN_WIDTH), row),
            pl.BlockSpec(woa.shape, const, **single),
            pl.BlockSpec(wog.shape, const, **single),
            pl.BlockSpec((1, D_MODEL), const),
            pl.BlockSpec((1, D_MODEL), const),
            pl.BlockSpec((1, D_MODEL), const),
            pl.BlockSpec(wup.shape, const, **single),
            pl.BlockSpec(wdn.shape, const, **single),
        ],
        out_specs=pl.BlockSpec((ROW_TILE, D_MODEL), row),
        out_shape=jax.ShapeDtypeStruct((T, D_MODEL), F32),
        compiler_params=pltpu.CompilerParams(
            dimension_semantics=("arbitrary",), vmem_limit_bytes=VMEM_LIMIT),
        name="out_mlp",
    )(xf, oT, oT, og, woa, wog, pmn, pre, post, wup, wdn)


def kernel(x, w_in, w_out, conv_w, A_log, dt_bias, gdn_norm_w, rel_bias, pre_mix_norm,
           post_mix_norm, pre_mlp_norm, post_mlp_norm, w_up, w_down):
    B, S, D = x.shape
    assert D == D_MODEL and S % ROW_TILE == 0 and S % MOBA_BLOCK == 0
    T = B * S
    depth = w_in.shape[0]
    xf = x.reshape(T, D)
    o0, o1, o2, o3, o4 = 0, ATT_WIDTH, 2 * ATT_WIDTH, 3 * ATT_WIDTH, 3 * ATT_WIDTH + 3 * GDN_WIDTH
    o5 = o4 + GDN_WIDTH
    for l in range(depth):
        wi = w_in[l]
        wqT = wi[:, o0:o1].T.astype(BF16)
        wk = wi[:, o1:o2].astype(BF16)
        wvT = wi[:, o2:o3].T.astype(BF16)
        wg = wi[:, o3:o4].astype(BF16)
        wz = wi[:, o4:o5].astype(BF16)
        wab = jnp.pad(wi[:, o5:], ((0, 0), (0, LANES - 2 * GDN_HEADS))).astype(BF16)
        pad8 = lambda v: jnp.pad(v.astype(F32), (0, LANES - GDN_HEADS))[None, :]
        qT, k, ksum, va, gqkv, sz, gb = _inproj(xf, pre_mix_norm[l][None, :], wqT, wk, wvT, wg, wz, wab,
                                                conv_w[l], pad8(A_log[l]), pad8(dt_bias[l]), B, S)
        oT, og = _token_mixers(rel_bias.astype(F32), qT, k.reshape(B, S, ATT_WIDTH), ksum, va, gqkv, sz,
                               gb, jnp.tile(gdn_norm_w[l], GDN_HEADS)[None, :], B, S)
        wo = w_out[l].astype(BF16)
        xf = _out_mlp(xf, oT, og, wo[:ATT_WIDTH], wo[ATT_WIDTH:], post_mix_norm[l][None, :],
                      pre_mlp_norm[l][None, :], post_mlp_norm[l][None, :],
                      w_up[l].astype(BF16), w_down[l].astype(BF16), B, S)
    return xf.reshape(B, S, D)
```

```python
import functools
import math

import jax
import jax.numpy as jnp
from jax import lax
from jax.experimental import pallas as pl
from jax.experimental.pallas import tpu as pltpu

F32 = jnp.float32
BF16 = jnp.bfloat16
HI = lax.Precision.HIGHEST

D_MODEL = 1024
HEAD_DIM = 64
ATT_HEADS = 8
GDN_HEADS = 8
ATT_WIDTH = ATT_HEADS * HEAD_DIM
GDN_WIDTH = GDN_HEADS * HEAD_DIM
MOBA_BLOCK = 256
MOBA_TOPK = 3
GDN_CHUNK = 64
CONV_WIDTH = 4
D_FF = 4 * D_MODEL
REL_BUCKETS = 32
REL_MAX_EXACT = 16
REL_MAX_DIST = 128
EPS = 1e-6
NEG = -1e30
LOG2E = math.log2(math.e)

LANES = 128
SUBLANES = 8
VMEM_LIMIT = 56 * 1024 * 1024
ROW_TILE = 512
INPROJ_TILE = 512
FF_TILE = 1024

NT = (((1,), (1,)), ((), ()))
TN = (((0,), (0,)), ((), ()))


def _bucket_lower_bounds():
    def bucket(d):
        if d < REL_MAX_EXACT:
            return d
        t = math.log(d / REL_MAX_EXACT) / math.log(REL_MAX_DIST / REL_MAX_EXACT)
        t = t * (REL_BUCKETS - REL_MAX_EXACT)
        assert d in (REL_MAX_EXACT, REL_MAX_DIST) or abs(t - round(t)) > 1e-6
        return min(REL_MAX_EXACT + int(t + 1e-9), REL_BUCKETS - 1)
    lower = []
    for b in range(REL_BUCKETS):
        d = 0
        while bucket(d) < b:
            d += 1
        lower.append(d)
    return lower


BUCKET_LOWER = _bucket_lower_bounds()


def _sigmoid(x):
    return 0.5 * jnp.tanh(0.5 * x) + 0.5


def _silu_of_half(h):
    return h + h * jnp.tanh(h)


def _rms(x, w):
    return x * lax.rsqrt(jnp.mean(x * x, axis=-1, keepdims=True) + EPS) * w


def _split_bf16(x, parts):
    out = []
    for _ in range(parts):
        h = x.astype(BF16)
        out.append(h)
        x = x - h.astype(F32)
    return out


def _dot_split_rhs(c, x, parts):
    acc = None
    for h in _split_bf16(x, parts):
        d = jnp.dot(c, h, preferred_element_type=F32)
        acc = d if acc is None else acc + d
    return acc


CONV_COLS = 512


def _inproj_kernel(x_ref, xp_ref, nw_ref, wqT_ref, wk_ref, wvT_ref, wg_ref, wz_ref, wab_ref,
                   cw_ref, alog_ref, dtb_ref,
                   qT_ref, k_ref, ksum_ref, va_ref, g_ref, z_ref, gb_ref, *, tiles_per_seq):
    h = _rms(x_ref[...], nw_ref[...]).astype(BF16)

    hp = _rms(xp_ref[...], nw_ref[...]).astype(BF16)
    seq_start = (pl.program_id(0) % tiles_per_seq) == 0
    trow8 = lax.broadcasted_iota(jnp.int32, (SUBLANES, CONV_COLS), 0)
    for c in range(3 * GDN_WIDTH // CONV_COLS):
        cols = slice(c * CONV_COLS, (c + 1) * CONV_COLS)
        cur = jnp.dot(h, wg_ref[:, cols], preferred_element_type=F32)
        prev8 = jnp.dot(hp, wg_ref[:, cols], preferred_element_type=F32)
        prev8 = jnp.where(seq_start, 0.0, prev8)
        cw_half = 0.5 * cw_ref[:, cols]
        acc = cur * cw_half[CONV_WIDTH - 1:CONV_WIDTH]
        for s in range(1, CONV_WIDTH):
            rolled = pltpu.roll(cur, s, 0)
            top = jnp.where(trow8 < s, pltpu.roll(prev8, s, 0), rolled[0:SUBLANES])
            tap = jnp.concatenate([top, rolled[SUBLANES:]], axis=0)
            acc = acc + tap * cw_half[CONV_WIDTH - 1 - s:CONV_WIDTH - s]
        g_ref[:, cols] = _silu_of_half(acc)

    z = jnp.dot(h, wz_ref[...], preferred_element_type=F32)
    z_ref[...] = _silu_of_half(0.5 * z)
    ab = jnp.dot(h, wab_ref[...], preferred_element_type=F32)
    xs = ab + dtb_ref[...]
    log_decay = -jnp.exp(alog_ref[...]) * (jnp.maximum(xs, 0.0) + jnp.log1p(jnp.exp(-jnp.abs(xs))))
    lane = lax.broadcasted_iota(jnp.int32, ab.shape, 1)
    gb_ref[...] = jnp.where(lane < GDN_HEADS, log_decay, _sigmoid(ab))

    qT = lax.dot_general(wqT_ref[...], h, NT, preferred_element_type=F32)
    vT = lax.dot_general(wvT_ref[...], h, NT, preferred_element_type=F32)
    k = jnp.dot(h, wk_ref[...], preferred_element_type=F32)
    k_ref[...] = k.astype(BF16)
    ones_row = jnp.where(lax.broadcasted_iota(jnp.int32, (V_ROWS - HEAD_DIM, MOBA_BLOCK), 0) == 0,
                         1.0, 0.0).astype(BF16)
    ksum_ref[...] = jnp.zeros(ksum_ref.shape, F32)
    for t in range(INPROJ_TILE // MOBA_BLOCK):
        blk = slice(t * MOBA_BLOCK, (t + 1) * MOBA_BLOCK)
        qT_ref[0, t] = qT[:, blk]
        ksum_ref[0, t:t + 1, :] = jnp.sum(k[blk], axis=0, keepdims=True)
        for hh in range(ATT_HEADS):
            va_ref[0, t, hh, 0:HEAD_DIM, :] = vT[HEAD_DIM * hh:HEAD_DIM * (hh + 1), blk].astype(BF16)
            va_ref[0, t, hh, HEAD_DIM:V_ROWS, :] = ones_row


def _inproj(xf, nw, wqT, wk, wvT, wg, wz, wab, conv_w, alog_pad, dtb_pad, B, S):
    T = B * S
    TM = INPROJ_TILE
    assert S % TM == 0
    nblk = S // MOBA_BLOCK
    tiles_per_seq = S // TM
    blk_per_tile = TM // MOBA_BLOCK
    const = lambda i: (0, 0)
    row = lambda i: (i, 0)
    tr = lambda i: (i // tiles_per_seq, i % tiles_per_seq, 0, 0)
    prev_rows = lambda i: (jnp.maximum(i * (TM // SUBLANES) - 1, 0), 0)
    single = dict(pipeline_mode=pl.Buffered(1))
    return pl.pallas_call(
        functools.partial(_inproj_kernel, tiles_per_seq=tiles_per_seq),
        grid=(T // TM,),
        in_specs=[
            pl.BlockSpec((TM, D_MODEL), row),
            pl.BlockSpec((SUBLANES, D_MODEL), prev_rows),
            pl.BlockSpec((1, D_MODEL), const),
            pl.BlockSpec(wqT.shape, const, **single),
            pl.BlockSpec(wk.shape, const, **single),
            pl.BlockSpec(wvT.shape, const, **single),
            pl.BlockSpec(wg.shape, const, **single),
            pl.BlockSpec(wz.shape, const, **single),
            pl.BlockSpec(wab.shape, const, **single),
            pl.BlockSpec(conv_w.shape, const),
            pl.BlockSpec((1, LANES), const),
            pl.BlockSpec((1, LANES), const),
        ],
        out_specs=[
            pl.BlockSpec((1, blk_per_tile, ATT_WIDTH, MOBA_BLOCK), tr),
            pl.BlockSpec((TM, ATT_WIDTH), row),
            pl.BlockSpec((1, SUBLANES, ATT_WIDTH), lambda i: (i, 0, 0)),
            pl.BlockSpec((1, blk_per_tile, ATT_HEADS, V_ROWS, MOBA_BLOCK),
                         lambda i: (i // tiles_per_seq, i % tiles_per_seq, 0, 0, 0)),
            pl.BlockSpec((TM, 3 * GDN_WIDTH), row),
            pl.BlockSpec((TM, GDN_WIDTH), row),
            pl.BlockSpec((TM, LANES), row),
        ],
        out_shape=[
            jax.ShapeDtypeStruct((B, nblk, ATT_WIDTH, MOBA_BLOCK), F32),
            jax.ShapeDtypeStruct((T, ATT_WIDTH), BF16),
            jax.ShapeDtypeStruct((T // TM, SUBLANES, ATT_WIDTH), F32),
            jax.ShapeDtypeStruct((B, nblk, ATT_HEADS, V_ROWS, MOBA_BLOCK), BF16),
            jax.ShapeDtypeStruct((T, 3 * GDN_WIDTH), F32),
            jax.ShapeDtypeStruct((T, GDN_WIDTH), F32),
            jax.ShapeDtypeStruct((T, LANES), F32),
        ],
        compiler_params=pltpu.CompilerParams(
            dimension_semantics=("arbitrary",), vmem_limit_bytes=VMEM_LIMIT),
        name="inproj",
    )(xf, xf, nw, wqT, wk, wvT, wg, wz, wab, conv_w, alog_pad, dtb_pad)


V_ROWS = HEAD_DIM + 16


def _paired_pos(i, nblk):
    return jnp.where(i < nblk // 2, 2 * i, 2 * (nblk - 1 - i) + 1)


GDN_TILE = 256
GDN_HALF = 2 * LANES


def _gdn_stages(yq_ref, yk_ref, yv_ref, gb_ref, sz_ref, nw_ref, out_ref,
                u_ref, wq_ref, a_ref, kd_ref, gl_ref, st_ref):
    C = GDN_CHUNK
    W = GDN_WIDTH
    TILE = GDN_TILE
    npair = W // LANES

    r_w = lax.broadcasted_iota(jnp.int32, (GDN_HALF, GDN_HALF), 0)
    c_w = lax.broadcasted_iota(jnp.int32, (GDN_HALF, GDN_HALF), 1)
    head_ones = jnp.where((r_w // HEAD_DIM) == (c_w // HEAD_DIM), 1.0, 0.0).astype(BF16)
    ltri_bd = jnp.where(((r_w // C) == (c_w // C)) & (c_w <= r_w), 1.0, 0.0).astype(BF16)
    tok = lax.broadcasted_iota(jnp.int32, (TILE, W), 0) % C
    col = lax.broadcasted_iota(jnp.int32, (TILE, W), 1) % HEAD_DIM
    causal_t = tok >= col
    strict_t = tok > col
    lane_t = lax.broadcasted_iota(jnp.int32, (TILE, LANES), 1)

    lane = lax.broadcasted_iota(jnp.int32, (C, LANES), 1)
    rowi = lax.broadcasted_iota(jnp.int32, (C, LANES), 0)
    first_head = lane < HEAD_DIM
    strict = rowi > (lane % HEAD_DIM)
    eye2 = jnp.where(rowi == (lane % HEAD_DIM), 1.0, 0.0)
    lane2 = lax.broadcasted_iota(jnp.int32, (C, 2 * LANES), 1)
    first_head2 = (lane2 % LANES) < HEAD_DIM
    r_l = lax.broadcasted_iota(jnp.int32, (LANES, LANES), 0)
    c_l = lax.broadcasted_iota(jnp.int32, (LANES, LANES), 1)
    same_head = (r_l // HEAD_DIM) == (c_l // HEAD_DIM)
    pair_ones = jnp.where(same_head, 1.0, 0.0).astype(BF16)

    def stack(x, mask):
        return jnp.concatenate([jnp.where(mask, x, 0.0), jnp.where(mask, 0.0, x)], axis=0)

    dot = functools.partial(jnp.dot, preferred_element_type=F32)

    def head_sumsq(ys):
        halves = [(y * y).astype(BF16)[:, h:h + GDN_HALF] for y in ys for h in range(0, W, GDN_HALF)]
        sums = dot(jnp.concatenate(halves, axis=0), head_ones)
        per = W // GDN_HALF
        return [jnp.concatenate([sums[(i * per + j) * TILE:(i * per + j + 1) * TILE] for j in range(per)],
                                axis=1) for i in range(len(ys))]

    def solve_stages(slot):
        yq = yq_ref[...]
        yk = yk_ref[...]
        yv = yv_ref[...]
        ssq, ssk = head_sumsq([yq, yk])
        qn = yq * lax.rsqrt(ssq + EPS) * (HEAD_DIM ** -0.5)
        kn = yk * lax.rsqrt(ssk + EPS)
        yield
        gbt = gb_ref[...]

        def spread(col0):
            pairs = []
            for p in range(npair):
                a = jnp.broadcast_to(gbt[:, col0 + 2 * p:col0 + 2 * p + 1], (TILE, LANES))
                b = jnp.broadcast_to(gbt[:, col0 + 2 * p + 1:col0 + 2 * p + 2], (TILE, LANES))
                pairs.append(jnp.where(lane_t < HEAD_DIM, a, b))
            return jnp.concatenate(pairs, axis=1)

        g = spread(0)
        beta = spread(GDN_HEADS)
        gcd = _dot_split_rhs(ltri_bd, jnp.concatenate([g, jnp.where(strict_t, g, 0.0)], axis=1), 2)
        yield
        gc = gcd[:, :W]
        decay = jnp.where(causal_t, jnp.exp(jnp.where(causal_t, gcd[:, W:], 0.0)), 0.0)
        egc = jnp.exp(gc)
        kb = kn * beta
        rv = yv * beta
        rk = kb * egc
        qd = qn * egc
        for cc in range(TILE // C):
            rs = slice(cc * C, (cc + 1) * C)
            g_last = gc[(cc + 1) * C - 1:(cc + 1) * C, :]
            kd_ref[slot, rs, :] = (kn[rs] * jnp.exp(g_last - gc[rs])).astype(BF16)
            gl_ref[slot, cc * SUBLANES:(cc + 1) * SUBLANES, :] = (
                jnp.broadcast_to(jnp.exp(g_last), (SUBLANES, W)))
        units = [(slice(cc * C, (cc + 1) * C), slice(LANES * p, LANES * (p + 1)), cc)
                 for cc in range(TILE // C) for p in range(npair)]
        kqs = [lax.dot_general(jnp.concatenate([kn[rs, ls], qn[rs, ls]], axis=0).astype(BF16),
                               stack(kn[rs, ls], first_head).astype(BF16), NT,
                               preferred_element_type=F32) for rs, ls, _ in units]
        yield
        ps = [-jnp.where(strict, kq[0:C] * beta[rs, ls] * decay[rs, ls], 0.0)
              for kq, (rs, ls, _) in zip(kqs, units)]
        ss = [eye2 + p for p in ps]
        ps = [dot(p.astype(BF16), stack(p, first_head).astype(BF16)) for p in ps]
        yield
        nround = int(math.log2(C))
        for k in range(1, nround):
            rhs = [stack(s_, first_head).astype(BF16) for s_ in ss]
            if k + 1 < nround:
                rhs = [jnp.concatenate([stack(p, first_head).astype(BF16), sx], axis=1)
                       for p, sx in zip(ps, rhs)]
            outs = [dot(p.astype(BF16), sx) for p, sx in zip(ps, rhs)]
            if k + 1 < nround:
                ps = [o[:, :LANES] for o in outs]
                ss = [s_ + o[:, LANES:] for s_, o in zip(ss, outs)]
            else:
                ss = [s_ + o for s_, o in zip(ss, outs)]
            yield
        xs = [dot(s_.astype(BF16),
                  stack(jnp.concatenate([rv[rs, ls], rk[rs, ls]], axis=1), first_head2).astype(BF16))
              for s_, (rs, ls, _) in zip(ss, units)]
        yield
        for x, kq, (rs, ls, cc) in zip(xs, kqs, units):
            u_ref[slot, rs, ls] = x[:, :LANES]
            wq_ref[slot, cc, 0:C, ls] = x[:, LANES:].astype(BF16)
            wq_ref[slot, cc, C:2 * C, ls] = qd[rs, ls].astype(BF16)
            a_ref[slot, rs, ls] = (kq[C:2 * C] * decay[rs, ls]).astype(BF16)

    lss = [slice(LANES * p, LANES * (p + 1)) for p in range(npair)]

    def recurrence_stages(slot, seq_start):
        states = [jnp.where(seq_start, 0.0, st_ref[p]) for p in range(npair)]
        pending = None

        def finish(rs, os_):
            sq = jnp.concatenate([(o * o).astype(BF16) for o in os_], axis=0)
            ms_all = dot(sq, pair_ones) * (1.0 / HEAD_DIM)
            for p, (ls, o) in enumerate(zip(lss, os_)):
                ms = ms_all[p * C:(p + 1) * C]
                out_ref[rs, ls] = o * lax.rsqrt(ms + EPS) * nw_ref[:, ls] * sz_ref[rs, ls]

        for cc in range(TILE // C):
            rs = slice(cc * C, (cc + 1) * C)
            wqs = [dot(wq_ref[slot, cc, :, ls], st.astype(BF16)) for ls, st in zip(lss, states)]
            if pending is not None:
                finish(*pending)
            yield
            v_news = [u_ref[slot, rs, ls] - wq[0:C] for ls, wq in zip(lss, wqs)]
            kvs = [lax.dot_general(kd_ref[slot, rs, ls], v.astype(BF16), TN, preferred_element_type=F32)
                   for ls, v in zip(lss, v_news)]
            os_ = [wq[C:2 * C] + dot(a_ref[slot, rs, ls], stack(v, first_head).astype(BF16))
                   for ls, wq, v in zip(lss, wqs, v_news)]
            states = [st * gl_ref[slot, cc * SUBLANES:cc * SUBLANES + 1, ls] + jnp.where(same_head, kv, 0.0)
                      for ls, st, kv in zip(lss, states, kvs)]
            pending = (rs, os_)
            yield
        finish(*pending)
        for p in range(npair):
            st_ref[p] = states[p]

    return solve_stages, recurrence_stages


def _interleave_weighted(*gens_and_weights):
    live = [[gen, weight] for gen, weight in gens_and_weights]
    while live:
        for entry in list(live):
            for _ in range(entry[1]):
                try:
                    next(entry[0])
                except StopIteration:
                    live.remove(entry)
                    break


def _mixers_kernel(relb_ref, qlo_ref, qhi_ref, kb_ref, ksum_ref, va_ref,
                   yq_ref, yk_ref, yv_ref, gb_ref, sz_ref, nw_ref,
                   oT_ref, og_ref,
                   bias_ref, addm_ref, qh_ref, lg_ref, moff_ref,
                   u_ref, wq_ref, a_ref, kd_ref, gl_ref, st_ref,
                   *, nblk, nbatch, tiles_per_seq, nsteps):
    step = pl.program_id(0)
    BLK = MOBA_BLOCK
    half = nblk // 2
    pair = jnp.minimum(step, nsteps - 2)
    hp = pair // (nbatch * (half // 2))
    b = (pair // (half // 2)) % nbatch
    scoring = step < nsteps - 1
    new_kv = (pair % (half // 2)) == 0

    solve_stages, recurrence_stages = _gdn_stages(
        yq_ref, yk_ref, yv_ref, gb_ref, sz_ref, nw_ref, og_ref,
        u_ref, wq_ref, a_ref, kd_ref, gl_ref, st_ref)
    seq_start = ((step - 1) % tiles_per_seq) == 0

    @pl.when((b == 0) & new_kv & scoring)
    def _():
        kk = lax.broadcasted_iota(jnp.int32, (BLK, BLK), 0)
        qq = lax.broadcasted_iota(jnp.int32, (BLK, BLK), 1)
        for hh in range(2):
            h = 2 * hp + hh
            for kind in range(2):
                d = qq - kk + kind * BLK
                val = jnp.full((BLK, BLK), relb_ref[h, REL_BUCKETS - 1], F32)
                for bkt in range(REL_BUCKETS - 2, -1, -1):
                    val = jnp.where(d < BUCKET_LOWER[bkt + 1], relb_ref[h, bkt], val)
                val = val * LOG2E
                if kind == 0:
                    val = jnp.where(d >= 0, val, NEG)
                bias_ref[hh, kind] = val

    def key_means():
        per_tile = INPROJ_TILE // BLK
        ks = ksum_ref[...]
        km = jnp.concatenate([ks[j // per_tile, j % per_tile:j % per_tile + 1, :] for j in range(nblk)],
                             axis=0) * (1.0 / BLK)
        lane = lax.broadcasted_iota(jnp.int32, (nblk, LANES), 1)
        return jnp.concatenate([jnp.where(lane < HEAD_DIM, km, 0.0),
                                jnp.where(lane >= HEAD_DIM, km, 0.0)], axis=0)

    def item_tiles(t):
        i_hi = nblk - 1 - t
        tiles = [(0, t, "own"), (1, i_hi, "own"), (1, i_hi - 1, "prev")]
        if t >= 1:
            tiles.append((0, t - 1, "prev"))
        tiles += [(0, j, "far") for j in range(t - 1)]
        tiles += [(1, j, "far") for j in range(i_hi - 1)]
        assert len(tiles) == nblk + 1
        return tiles

    def score_stages(parity):
        ridx = lax.broadcasted_iota(jnp.int32, (nblk, BLK), 0)
        sub = lax.broadcasted_iota(jnp.int32, (LANES, BLK), 0)
        scale = HEAD_DIM ** -0.5 * LOG2E
        km = key_means()
        for e in range(2):
            t = 2 * parity + e
            slot = 2 * parity + e
            q_of = ((qlo_ref, e, t), (qhi_ref, 1 - e, nblk - 1 - t))
            for s, (q_ref, w, qi) in enumerate(q_of):
                qT = q_ref[0, w]
                gT = jnp.dot(km, qT, precision=HI, preferred_element_type=F32)
                past = ridx < qi
                for hh in range(2):
                    gm = jnp.where(past, gT[nblk * hh:nblk * (hh + 1)], -jnp.inf)
                    cnt = jnp.zeros((nblk, BLK), F32)
                    for jp in range(nblk):
                        row = gm[jp:jp + 1, :]
                        beats = (row > gm) | ((row == gm) & (ridx > jp))
                        cnt = cnt + jnp.where(beats, 1.0, 0.0)
                    visible = past & (cnt < MOBA_TOPK)
                    addm_ref[e, s, nblk * hh:nblk * (hh + 1), :] = jnp.where(visible, 0.0, NEG)
                    in_head = (sub >= HEAD_DIM * hh) & (sub < HEAD_DIM * (hh + 1))
                    qh_ref[e, s, hh] = jnp.where(in_head, qT * scale, 0.0).astype(BF16)
            yield
            tiles = item_tiles(t)
            for hh in range(2):
                cmax = {0: [], 1: []}
                offs = []
                for n, (s, kblk, cls) in enumerate(tiles):
                    lg = jnp.dot(kb_ref[0, kblk * BLK:(kblk + 1) * BLK, :], qh_ref[e, s, hh],
                                 preferred_element_type=F32)
                    if cls != "far":
                        lg = lg + bias_ref[hh, 0 if cls == "own" else 1]
                    lg_ref[slot, hh, n] = lg
                    cm = jnp.max(lg, axis=0, keepdims=True)
                    off = None
                    if cls != "own":
                        off = addm_ref[e, s, nblk * hh + kblk:nblk * hh + kblk + 1, :]
                        if cls == "far":
                            off = off + relb_ref[2 * hp + hh, REL_BUCKETS - 1] * LOG2E
                        cm = cm + off
                    cmax[s].append(cm)
                    offs.append(off)
                    yield
                m = {s: functools.reduce(jnp.maximum, cmax[s]) for s in (0, 1)}
                for n, (s, _, _) in enumerate(tiles):
                    moff_ref[slot, hh, n:n + 1, :] = m[s] if offs[n] is None else m[s] - offs[n]

    def softmax_pv_stages(parity):
        for e in range(2):
            t = 2 * parity + e
            slot = 2 * parity + e
            tiles = item_tiles(t)
            for hh in range(2):
                acc = {0: None, 1: None}
                for n, (s, kblk, _) in enumerate(tiles):
                    p = jnp.exp2(lg_ref[slot, hh, n] - moff_ref[slot, hh, n:n + 1, :])
                    pvn = jnp.dot(va_ref[0, kblk, hh], p.astype(BF16),
                                  preferred_element_type=F32)
                    acc[s] = pvn if acc[s] is None else acc[s] + pvn
                    yield
                for s in (0, 1):
                    oT_ref[0, 2 * e + s, HEAD_DIM * hh:HEAD_DIM * (hh + 1), :] = (
                        acc[s][0:HEAD_DIM] / acc[s][HEAD_DIM:HEAD_DIM + 1])

    ATT, GDN = 2, 1

    @pl.when(step == 0)
    def _():
        st_ref[...] = jnp.zeros(st_ref.shape, F32)
        _interleave_weighted((score_stages(0), ATT), (solve_stages(0), GDN))

    for parity in range(2):
        @pl.when((step > 0) & (step < nsteps - 1) & (step % 2 == parity))
        def _(parity=parity):
            _interleave_weighted((recurrence_stages(1 - parity, seq_start), GDN),
                                 (score_stages(parity), ATT),
                                 (solve_stages(parity), GDN),
                                 (softmax_pv_stages(1 - parity), ATT))

    @pl.when(step == nsteps - 1)
    def _():
        last = (nsteps - 2) % 2
        _interleave_weighted((recurrence_stages(last, seq_start), GDN), (softmax_pv_stages(last), ATT))


def _token_mixers(rel_bias, qT, k3, ksum, va, gqkv, sz, gb, nw_row, B, S):
    T = B * S
    W = GDN_WIDTH
    TILE = GDN_TILE
    nblk = S // MOBA_BLOCK
    half = nblk // 2
    assert TILE == GDN_HALF and S % TILE == 0
    assert BUCKET_LOWER[REL_BUCKETS - 1] <= MOBA_BLOCK + 1
    assert nblk == 8 and nblk + 1 <= 2 * SUBLANES
    nchunk = TILE // GDN_CHUNK
    ntiles = T // TILE
    npairs = (ATT_HEADS // 2) * B * (half // 2)
    assert npairs == ntiles
    nsteps = ntiles + 1

    def scored(s):
        p = jnp.minimum(s, npairs - 1)
        return p // (B * (half // 2)), (p // (half // 2)) % B, p % (half // 2)

    def done(s):
        p = jnp.maximum(s - 1, 0)
        return p // (B * (half // 2)), (p // (half // 2)) % B, p % (half // 2)

    def q_lo(s):
        hp, b, m = scored(s)
        return (b, m, hp, 0)

    def q_hi(s):
        hp, b, m = scored(s)
        return (b, half - 1 - m, hp, 0)

    def k_blk(s):
        hp, b, _ = scored(s)
        return (b, 0, hp)

    def v_blk(s):
        hp, b, _ = done(s)
        return (b, 0, hp, 0, 0)

    def o_blk(s):
        hp, b, m = done(s)
        return (b, m, hp, 0)

    cur_tile = lambda off: (lambda s: (jnp.minimum(s, ntiles - 1), off))
    prev_tile = lambda s: (jnp.maximum(s - 1, 0), 0)
    return pl.pallas_call(
        functools.partial(_mixers_kernel, nblk=nblk, nbatch=B, tiles_per_seq=S // TILE, nsteps=nsteps),
        grid=(nsteps,),
        in_specs=[
            pl.BlockSpec(memory_space=pltpu.SMEM),
            pl.BlockSpec((1, 2, LANES, MOBA_BLOCK), q_lo),
            pl.BlockSpec((1, 2, LANES, MOBA_BLOCK), q_hi),
            pl.BlockSpec((1, S, LANES), k_blk),
            pl.BlockSpec((S // INPROJ_TILE, SUBLANES, LANES), k_blk),
            pl.BlockSpec((1, nblk, 2, V_ROWS, MOBA_BLOCK), v_blk),
            pl.BlockSpec((TILE, W), cur_tile(0)),
            pl.BlockSpec((TILE, W), cur_tile(1)),
            pl.BlockSpec((TILE, W), cur_tile(2)),
            pl.BlockSpec((TILE, LANES), cur_tile(0)),
            pl.BlockSpec((TILE, W), prev_tile),
            pl.BlockSpec((1, W), lambda s: (0, 0)),
        ],
        out_specs=[
            pl.BlockSpec((1, 4, LANES, MOBA_BLOCK), o_blk),
            pl.BlockSpec((TILE, W), prev_tile),
        ],
        out_shape=[
            jax.ShapeDtypeStruct((B, nblk, ATT_WIDTH, MOBA_BLOCK), F32),
            jax.ShapeDtypeStruct((T, GDN_WIDTH), F32),
        ],
        scratch_shapes=[
            pltpu.VMEM((2, 2, MOBA_BLOCK, MOBA_BLOCK), F32),
            pltpu.VMEM((2, 2, 2 * nblk, MOBA_BLOCK), F32),
            pltpu.VMEM((2, 2, 2, LANES, MOBA_BLOCK), BF16),
            pltpu.VMEM((4, 2, nblk + 1, MOBA_BLOCK, MOBA_BLOCK), F32),
            pltpu.VMEM((4, 2, 2 * SUBLANES, MOBA_BLOCK), F32),
            pltpu.VMEM((2, TILE, W), F32),
            pltpu.VMEM((2, nchunk, 2 * GDN_CHUNK, W), BF16),
            pltpu.VMEM((2, TILE, W), BF16),
            pltpu.VMEM((2, TILE, W), BF16),
            pltpu.VMEM((2, nchunk * SUBLANES, W), F32),
            pltpu.VMEM((W // LANES, LANES, LANES), F32),
        ],
        compiler_params=pltpu.CompilerParams(
            dimension_semantics=("arbitrary",), vmem_limit_bytes=VMEM_LIMIT),
        name="token_mixers",
    )(rel_bias, qT, qT, k3, ksum, va, gqkv, gqkv, gqkv, gb, sz, nw_row)


def _out_mlp_kernel(x_ref, oTa_ref, oTb_ref, og_ref, woa_ref, wog_ref, pmn_ref, pre_ref, post_ref,
                    wup_ref, wdn_ref, out_ref):
    oT = jnp.concatenate([oTa_ref[0, 0], oTb_ref[0, 0]], axis=1)
    o_att = oT.T.astype(BF16)
    mix = jnp.dot(o_att, woa_ref[...], preferred_element_type=F32)
    mix = mix + jnp.dot(og_ref[...].astype(BF16), wog_ref[...], preferred_element_type=F32)
    x1 = x_ref[...] + _rms(mix, pmn_ref[...])
    h = _rms(x1, pre_ref[...]).astype(BF16)
    acc = jnp.zeros((ROW_TILE, D_MODEL), F32)
    for c in range(D_FF // FF_TILE):
        up = jnp.dot(h, wup_ref[:, c * FF_TILE:(c + 1) * FF_TILE], preferred_element_type=F32)
        act = jnp.square(jnp.maximum(up, 0.0)).astype(BF16)
        acc = acc + jnp.dot(act, wdn_ref[c * FF_TILE:(c + 1) * FF_TILE, :], preferred_element_type=F32)
    out_ref[...] = x1 + _rms(acc, post_ref[...])


def _out_mlp(xf, oT, og, woa, wog, pmn, pre, post, wup, wdn, B, S):
    T = B * S
    nblk = S // MOBA_BLOCK
    tiles_per_seq = S // ROW_TILE
    assert ROW_TILE == 2 * MOBA_BLOCK
    const = lambda i: (0, 0)
    row = lambda i: (i, 0)

    def att_block(which):
        def index(i):
            blk = 2 * (i % tiles_per_seq) + which
            return (i // tiles_per_seq, _paired_pos(blk, nblk), 0, 0)
        return index

    single = dict(pipeline_mode=pl.Buffered(1))
    return pl.pallas_call(
        _out_mlp_kernel,
        grid=(T // ROW_TILE,),
        in_specs=[
            pl.BlockSpec((ROW_TILE, D_MODEL), row),
            pl.BlockSpec((1, 1, ATT_WIDTH, MOBA_BLOCK), att_block(0)),
            pl.BlockSpec((1, 1, ATT_WIDTH, MOBA_BLOCK), att_block(1)),
            pl.BlockSpec((ROW_TILE, GDN_WIDTH), row),
            pl.BlockSpec(woa.shape, const, **single),
            pl.BlockSpec(wog.shape, const, **single),
            pl.BlockSpec((1, D_MODEL), const),
            pl.BlockSpec((1, D_MODEL), const),
            pl.BlockSpec((1, D_MODEL), const),
            pl.BlockSpec(wup.shape, const, **single),
            pl.BlockSpec(wdn.shape, const, **single),
        ],
        out_specs=pl.BlockSpec((ROW_TILE, D_MODEL), row),
        out_shape=jax.ShapeDtypeStruct((T, D_MODEL), F32),
        compiler_params=pltpu.CompilerParams(
            dimension_semantics=("arbitrary",), vmem_limit_bytes=VMEM_LIMIT),
        name="out_mlp",
    )(xf, oT, oT, og, woa, wog, pmn, pre, post, wup, wdn)


def kernel(x, w_in, w_out, conv_w, A_log, dt_bias, gdn_norm_w, rel_bias, pre_mix_norm,
           post_mix_norm, pre_mlp_norm, post_mlp_norm, w_up, w_down):
    B, S, D = x.shape
    assert D == D_MODEL and S % ROW_TILE == 0 and S % MOBA_BLOCK == 0
    T = B * S
    depth = w_in.shape[0]
    xf = x.reshape(T, D)
    o0, o1, o2, o3, o4 = 0, ATT_WIDTH, 2 * ATT_WIDTH, 3 * ATT_WIDTH, 3 * ATT_WIDTH + 3 * GDN_WIDTH
    o5 = o4 + GDN_WIDTH
    for l in range(depth):
        wi = w_in[l]
        wqT = wi[:, o0:o1].T.astype(BF16)
        wk = wi[:, o1:o2].astype(BF16)
        wvT = wi[:, o2:o3].T.astype(BF16)
        wg = wi[:, o3:o4].astype(BF16)
        wz = wi[:, o4:o5].astype(BF16)
        wab = jnp.pad(wi[:, o5:], ((0, 0), (0, LANES - 2 * GDN_HEADS))).astype(BF16)
        pad8 = lambda v: jnp.pad(v.astype(F32), (0, LANES - GDN_HEADS))[None, :]
        qT, k, ksum, va, gqkv, sz, gb = _inproj(xf, pre_mix_norm[l][None, :], wqT, wk, wvT, wg, wz, wab,
                                                conv_w[l], pad8(A_log[l]), pad8(dt_bias[l]), B, S)
        oT, og = _token_mixers(rel_bias.astype(F32), qT, k.reshape(B, S, ATT_WIDTH), ksum, va, gqkv, sz,
                               gb, jnp.tile(gdn_norm_w[l], GDN_HEADS)[None, :], B, S)
        wo = w_out[l].astype(BF16)
        xf = _out_mlp(xf, oT, og, wo[:ATT_WIDTH], wo[ATT_WIDTH:], post_mix_norm[l][None, :],
                      pre_mlp_norm[l][None, :], post_mlp_norm[l][None, :],
                      w_up[l].astype(BF16), w_down[l].astype(BF16), B, S)
    return xf.reshape(B, S, D)
```

```python
import functools
import math

import jax
import jax.numpy as jnp
from jax import lax
from jax.experimental import pallas as pl
from jax.experimental.pallas import tpu as pltpu

F32 = jnp.float32
BF16 = jnp.bfloat16
HI = lax.Precision.HIGHEST

D_MODEL = 1024
HEAD_DIM = 64
ATT_HEADS = 8
GDN_HEADS = 8
ATT_WIDTH = ATT_HEADS * HEAD_DIM
GDN_WIDTH = GDN_HEADS * HEAD_DIM
MOBA_BLOCK = 256
MOBA_TOPK = 3
GDN_CHUNK = 64
CONV_WIDTH = 4
D_FF = 4 * D_MODEL
REL_BUCKETS = 32
REL_MAX_EXACT = 16
REL_MAX_DIST = 128
EPS = 1e-6
NEG = -1e30
LOG2E = math.log2(math.e)

LANES = 128
SUBLANES = 8
VMEM_LIMIT = 56 * 1024 * 1024
ROW_TILE = 512
INPROJ_TILE = 512
FF_TILE = 1024

NT = (((1,), (1,)), ((), ()))
TN = (((0,), (0,)), ((), ()))


def _bucket_lower_bounds():
    def bucket(d):
        if d < REL_MAX_EXACT:
            return d
        t = math.log(d / REL_MAX_EXACT) / math.log(REL_MAX_DIST / REL_MAX_EXACT)
        t = t * (REL_BUCKETS - REL_MAX_EXACT)
        assert d in (REL_MAX_EXACT, REL_MAX_DIST) or abs(t - round(t)) > 1e-6
        return min(REL_MAX_EXACT + int(t + 1e-9), REL_BUCKETS - 1)
    lower = []
    for b in range(REL_BUCKETS):
        d = 0
        while bucket(d) < b:
            d += 1
        lower.append(d)
    return lower


BUCKET_LOWER = _bucket_lower_bounds()


def _sigmoid(x):
    return 0.5 * jnp.tanh(0.5 * x) + 0.5


def _silu_of_half(h):
    return h + h * jnp.tanh(h)


def _rms(x, w):
    return x * lax.rsqrt(jnp.mean(x * x, axis=-1, keepdims=True) + EPS) * w


def _split_bf16(x, parts):
    out = []
    for _ in range(parts):
        h = x.astype(BF16)
        out.append(h)
        x = x - h.astype(F32)
    return out


def _dot_split_rhs(c, x, parts):
    acc = None
    for h in _split_bf16(x, parts):
        d = jnp.dot(c, h, preferred_element_type=F32)
        acc = d if acc is None else acc + d
    return acc


CONV_COLS = 512


def _inproj_kernel(x_ref, xp_ref, nw_ref, wqT_ref, wk_ref, wvT_ref, wg_ref, wz_ref, wab_ref,
                   cw_ref, alog_ref, dtb_ref,
                   qT_ref, k_ref, ksum_ref, va_ref, g_ref, z_ref, gb_ref, *, tiles_per_seq):
    h = _rms(x_ref[...], nw_ref[...]).astype(BF16)

    hp = _rms(xp_ref[...], nw_ref[...]).astype(BF16)
    seq_start = (pl.program_id(0) % tiles_per_seq) == 0
    trow8 = lax.broadcasted_iota(jnp.int32, (SUBLANES, CONV_COLS), 0)
    for c in range(3 * GDN_WIDTH // CONV_COLS):
        cols = slice(c * CONV_COLS, (c + 1) * CONV_COLS)
        cur = jnp.dot(h, wg_ref[:, cols], preferred_element_type=F32)
        prev8 = jnp.dot(hp, wg_ref[:, cols], preferred_element_type=F32)
        prev8 = jnp.where(seq_start, 0.0, prev8)
        cw_half = 0.5 * cw_ref[:, cols]
        acc = cur * cw_half[CONV_WIDTH - 1:CONV_WIDTH]
        for s in range(1, CONV_WIDTH):
            rolled = pltpu.roll(cur, s, 0)
            top = jnp.where(trow8 < s, pltpu.roll(prev8, s, 0), rolled[0:SUBLANES])
            tap = jnp.concatenate([top, rolled[SUBLANES:]], axis=0)
            acc = acc + tap * cw_half[CONV_WIDTH - 1 - s:CONV_WIDTH - s]
        g_ref[:, cols] = _silu_of_half(acc)

    z = jnp.dot(h, wz_ref[...], preferred_element_type=F32)
    z_ref[...] = _silu_of_half(0.5 * z)
    ab = jnp.dot(h, wab_ref[...], preferred_element_type=F32)
    xs = ab + dtb_ref[...]
    log_decay = -jnp.exp(alog_ref[...]) * (jnp.maximum(xs, 0.0) + jnp.log1p(jnp.exp(-jnp.abs(xs))))
    lane = lax.broadcasted_iota(jnp.int32, ab.shape, 1)
    gb_ref[...] = jnp.where(lane < GDN_HEADS, log_decay, _sigmoid(ab))

    qT = lax.dot_general(wqT_ref[...], h, NT, preferred_element_type=F32)
    vT = lax.dot_general(wvT_ref[...], h, NT, preferred_element_type=F32)
    k = jnp.dot(h, wk_ref[...], preferred_element_type=F32)
    k_ref[...] = k.astype(BF16)
    ones_row = jnp.where(lax.broadcasted_iota(jnp.int32, (V_ROWS - HEAD_DIM, MOBA_BLOCK), 0) == 0,
                         1.0, 0.0).astype(BF16)
    ksum_ref[...] = jnp.zeros(ksum_ref.shape, F32)
    for t in range(INPROJ_TILE // MOBA_BLOCK):
        blk = slice(t * MOBA_BLOCK, (t + 1) * MOBA_BLOCK)
        qT_ref[0, t] = qT[:, blk]
        ksum_ref[0, t:t + 1, :] = jnp.sum(k[blk], axis=0, keepdims=True)
        for hh in range(ATT_HEADS):
            va_ref[0, t, hh, 0:HEAD_DIM, :] = vT[HEAD_DIM * hh:HEAD_DIM * (hh + 1), blk].astype(BF16)
            va_ref[0, t, hh, HEAD_DIM:V_ROWS, :] = ones_row


def _inproj(xf, nw, wqT, wk, wvT, wg, wz, wab, conv_w, alog_pad, dtb_pad, B, S):
    T = B * S
    TM = INPROJ_TILE
    assert S % TM == 0
    nblk = S // MOBA_BLOCK
    tiles_per_seq = S // TM
    blk_per_tile = TM // MOBA_BLOCK
    const = lambda i: (0, 0)
    row = lambda i: (i, 0)
    tr = lambda i: (i // tiles_per_seq, i % tiles_per_seq, 0, 0)
    prev_rows = lambda i: (jnp.maximum(i * (TM // SUBLANES) - 1, 0), 0)
    single = dict(pipeline_mode=pl.Buffered(1))
    return pl.pallas_call(
        functools.partial(_inproj_kernel, tiles_per_seq=tiles_per_seq),
        grid=(T // TM,),
        in_specs=[
            pl.BlockSpec((TM, D_MODEL), row),
            pl.BlockSpec((SUBLANES, D_MODEL), prev_rows),
            pl.BlockSpec((1, D_MODEL), const),
            pl.BlockSpec(wqT.shape, const, **single),
            pl.BlockSpec(wk.shape, const, **single),
            pl.BlockSpec(wvT.shape, const, **single),
            pl.BlockSpec(wg.shape, const, **single),
            pl.BlockSpec(wz.shape, const, **single),
            pl.BlockSpec(wab.shape, const, **single),
            pl.BlockSpec(conv_w.shape, const),
            pl.BlockSpec((1, LANES), const),
            pl.BlockSpec((1, LANES), const),
        ],
        out_specs=[
            pl.BlockSpec((1, blk_per_tile, ATT_WIDTH, MOBA_BLOCK), tr),
            pl.BlockSpec((TM, ATT_WIDTH), row),
            pl.BlockSpec((1, SUBLANES, ATT_WIDTH), lambda i: (i, 0, 0)),
            pl.BlockSpec((1, blk_per_tile, ATT_HEADS, V_ROWS, MOBA_BLOCK),
                         lambda i: (i // tiles_per_seq, i % tiles_per_seq, 0, 0, 0)),
            pl.BlockSpec((TM, 3 * GDN_WIDTH), row),
            pl.BlockSpec((TM, GDN_WIDTH), row),
            pl.BlockSpec((TM, LANES), row),
        ],
        out_shape=[
            jax.ShapeDtypeStruct((B, nblk, ATT_WIDTH, MOBA_BLOCK), F32),
            jax.ShapeDtypeStruct((T, ATT_WIDTH), BF16),
            jax.ShapeDtypeStruct((T // TM, SUBLANES, ATT_WIDTH), F32),
            jax.ShapeDtypeStruct((B, nblk, ATT_HEADS, V_ROWS, MOBA_BLOCK), BF16),
            jax.ShapeDtypeStruct((T, 3 * GDN_WIDTH), F32),
            jax.ShapeDtypeStruct((T, GDN_WIDTH), F32),
            jax.ShapeDtypeStruct((T, LANES), F32),
        ],
        compiler_params=pltpu.CompilerParams(
            dimension_semantics=("arbitrary",), vmem_limit_bytes=VMEM_LIMIT),
        name="inproj",
    )(xf, xf, nw, wqT, wk, wvT, wg, wz, wab, conv_w, alog_pad, dtb_pad)


V_ROWS = HEAD_DIM + 16


def _paired_pos(i, nblk):
    return jnp.where(i < nblk // 2, 2 * i, 2 * (nblk - 1 - i) + 1)


GDN_TILE = 256
GDN_HALF = 2 * LANES


def _gdn_stages(y_ref, gb_ref, sz_ref, nw_ref, out_ref,
                u_ref, wq_ref, a_ref, kd_ref, gl_ref, st_ref):
    C = GDN_CHUNK
    W = GDN_WIDTH
    TILE = GDN_TILE
    npair = W // LANES

    r_w = lax.broadcasted_iota(jnp.int32, (GDN_HALF, GDN_HALF), 0)
    c_w = lax.broadcasted_iota(jnp.int32, (GDN_HALF, GDN_HALF), 1)
    head_ones = jnp.where((r_w // HEAD_DIM) == (c_w // HEAD_DIM), 1.0, 0.0).astype(BF16)
    ltri_bd = jnp.where(((r_w // C) == (c_w // C)) & (c_w <= r_w), 1.0, 0.0).astype(BF16)
    tok = lax.broadcasted_iota(jnp.int32, (TILE, W), 0) % C
    col = lax.broadcasted_iota(jnp.int32, (TILE, W), 1) % HEAD_DIM
    causal_t = tok >= col
    strict_t = tok > col
    lane_t = lax.broadcasted_iota(jnp.int32, (TILE, LANES), 1)

    lane = lax.broadcasted_iota(jnp.int32, (C, LANES), 1)
    rowi = lax.broadcasted_iota(jnp.int32, (C, LANES), 0)
    first_head = lane < HEAD_DIM
    strict = rowi > (lane % HEAD_DIM)
    eye2 = jnp.where(rowi == (lane % HEAD_DIM), 1.0, 0.0)
    lane2 = lax.broadcasted_iota(jnp.int32, (C, 2 * LANES), 1)
    first_head2 = (lane2 % LANES) < HEAD_DIM
    r_l = lax.broadcasted_iota(jnp.int32, (LANES, LANES), 0)
    c_l = lax.broadcasted_iota(jnp.int32, (LANES, LANES), 1)
    same_head = (r_l // HEAD_DIM) == (c_l // HEAD_DIM)
    pair_ones = jnp.where(same_head, 1.0, 0.0).astype(BF16)

    def stack(x, mask):
        return jnp.concatenate([jnp.where(mask, x, 0.0), jnp.where(mask, 0.0, x)], axis=0)

    dot = functools.partial(jnp.dot, preferred_element_type=F32)

    def head_sumsq(ys):
        halves = [(y * y).astype(BF16)[:, h:h + GDN_HALF] for y in ys for h in range(0, W, GDN_HALF)]
        sums = dot(jnp.concatenate(halves, axis=0), head_ones)
        per = W // GDN_HALF
        return [jnp.concatenate([sums[(i * per + j) * TILE:(i * per + j + 1) * TILE] for j in range(per)],
                                axis=1) for i in range(len(ys))]

    def solve_stages(slot):
        yq = y_ref[:, 0:W]
        yk = y_ref[:, W:2 * W]
        yv = y_ref[:, 2 * W:3 * W]
        ssq, ssk = head_sumsq([yq, yk])
        qn = yq * lax.rsqrt(ssq + EPS) * (HEAD_DIM ** -0.5)
        kn = yk * lax.rsqrt(ssk + EPS)
        yield
        gbt = gb_ref[...]

        def spread(col0):
            pairs = []
            for p in range(npair):
                a = jnp.broadcast_to(gbt[:, col0 + 2 * p:col0 + 2 * p + 1], (TILE, LANES))
                b = jnp.broadcast_to(gbt[:, col0 + 2 * p + 1:col0 + 2 * p + 2], (TILE, LANES))
                pairs.append(jnp.where(lane_t < HEAD_DIM, a, b))
            return jnp.concatenate(pairs, axis=1)

        g = spread(0)
        beta = spread(GDN_HEADS)
        gcd = _dot_split_rhs(ltri_bd, jnp.concatenate([g, jnp.where(strict_t, g, 0.0)], axis=1), 2)
        yield
        gc = gcd[:, :W]
        decay = jnp.where(causal_t, jnp.exp(jnp.where(causal_t, gcd[:, W:], 0.0)), 0.0)
        egc = jnp.exp(gc)
        kb = kn * beta
        rv = yv * beta
        rk = kb * egc
        qd = qn * egc
        for cc in range(TILE // C):
            rs = slice(cc * C, (cc + 1) * C)
            g_last = gc[(cc + 1) * C - 1:(cc + 1) * C, :]
            kd_ref[slot, rs, :] = (kn[rs] * jnp.exp(g_last - gc[rs])).astype(BF16)
            gl_ref[slot, cc * SUBLANES:(cc + 1) * SUBLANES, :] = (
                jnp.broadcast_to(jnp.exp(g_last), (SUBLANES, W)))
        units = [(slice(cc * C, (cc + 1) * C), slice(LANES * p, LANES * (p + 1)), cc)
                 for cc in range(TILE // C) for p in range(npair)]
        kqs = [lax.dot_general(jnp.concatenate([kn[rs, ls], qn[rs, ls]], axis=0).astype(BF16),
                               stack(kn[rs, ls], first_head).astype(BF16), NT,
                               preferred_element_type=F32) for rs, ls, _ in units]
        yield
        ps = [-jnp.where(strict, kq[0:C] * beta[rs, ls] * decay[rs, ls], 0.0)
              for kq, (rs, ls, _) in zip(kqs, units)]
        ss = [eye2 + p for p in ps]
        ps = [dot(p.astype(BF16), stack(p, first_head).astype(BF16)) for p in ps]
        yield
        nround = int(math.log2(C))
        for k in range(1, nround):
            rhs = [stack(s_, first_head).astype(BF16) for s_ in ss]
            if k + 1 < nround:
                rhs = [jnp.concatenate([stack(p, first_head).astype(BF16), sx], axis=1)
                       for p, sx in zip(ps, rhs)]
            outs = [dot(p.astype(BF16), sx) for p, sx in zip(ps, rhs)]
            if k + 1 < nround:
                ps = [o[:, :LANES] for o in outs]
                ss = [s_ + o[:, LANES:] for s_, o in zip(ss, outs)]
            else:
                ss = [s_ + o for s_, o in zip(ss, outs)]
            yield
        xs = [dot(s_.astype(BF16),
                  stack(jnp.concatenate([rv[rs, ls], rk[rs, ls]], axis=1), first_head2).astype(BF16))
              for s_, (rs, ls, _) in zip(ss, units)]
        yield
        for x, kq, (rs, ls, cc) in zip(xs, kqs, units):
            u_ref[slot, rs, ls] = x[:, :LANES]
            wq_ref[slot, cc, 0:C, ls] = x[:, LANES:].astype(BF16)
            wq_ref[slot, cc, C:2 * C, ls] = qd[rs, ls].astype(BF16)
            a_ref[slot, rs, ls] = (kq[C:2 * C] * decay[rs, ls]).astype(BF16)

    lss = [slice(LANES * p, LANES * (p + 1)) for p in range(npair)]

    def recurrence_stages(slot, seq_start):
        states = [jnp.where(seq_start, 0.0, st_ref[p]) for p in range(npair)]
        pending = None

        def finish(rs, os_):
            sq = jnp.concatenate([(o * o).astype(BF16) for o in os_], axis=0)
            ms_all = dot(sq, pair_ones) * (1.0 / HEAD_DIM)
            for p, (ls, o) in enumerate(zip(lss, os_)):
                ms = ms_all[p * C:(p + 1) * C]
                out_ref[rs, ls] = o * lax.rsqrt(ms + EPS) * nw_ref[:, ls] * sz_ref[rs, ls]

        for cc in range(TILE // C):
            rs = slice(cc * C, (cc + 1) * C)
            wqs = [dot(wq_ref[slot, cc, :, ls], st.astype(BF16)) for ls, st in zip(lss, states)]
            if pending is not None:
                finish(*pending)
            yield
            v_news = [u_ref[slot, rs, ls] - wq[0:C] for ls, wq in zip(lss, wqs)]
            kvs = [lax.dot_general(kd_ref[slot, rs, ls], v.astype(BF16), TN, preferred_element_type=F32)
                   for ls, v in zip(lss, v_news)]
            os_ = [wq[C:2 * C] + dot(a_ref[slot, rs, ls], stack(v, first_head).astype(BF16))
                   for ls, wq, v in zip(lss, wqs, v_news)]
            states = [st * gl_ref[slot, cc * SUBLANES:cc * SUBLANES + 1, ls] + jnp.where(same_head, kv, 0.0)
                      for ls, st, kv in zip(lss, states, kvs)]
            pending = (rs, os_)
            yield
        finish(*pending)
        for p in range(npair):
            st_ref[p] = states[p]

    return solve_stages, recurrence_stages


def _interleave_weighted(*gens_and_weights):
    live = [[gen, weight] for gen, weight in gens_and_weights]
    while live:
        for entry in list(live):
            for _ in range(entry[1]):
                try:
                    next(entry[0])
                except StopIteration:
                    live.remove(entry)
                    break


def _mixers_kernel(relb_ref, qlo_ref, qhi_ref, kb_ref, ksum_ref, va_ref,
                   y_ref, gb_ref, sz_ref, nw_ref,
                   oT_ref, og_ref,
                   bias_ref, addm_ref, qh_ref, lg_ref, moff_ref,
                   u_ref, wq_ref, a_ref, kd_ref, gl_ref, st_ref,
                   *, nblk, nbatch, tiles_per_seq, nsteps):
    step = pl.program_id(0)
    BLK = MOBA_BLOCK
    half = nblk // 2
    pair = jnp.minimum(step, nsteps - 2)
    hp = pair // (nbatch * (half // 2))
    b = (pair // (half // 2)) % nbatch
    scoring = step < nsteps - 1
    new_kv = (pair % (half // 2)) == 0

    solve_stages, recurrence_stages = _gdn_stages(
        y_ref, gb_ref, sz_ref, nw_ref, og_ref,
        u_ref, wq_ref, a_ref, kd_ref, gl_ref, st_ref)
    seq_start = ((step - 1) % tiles_per_seq) == 0

    @pl.when((b == 0) & new_kv & scoring)
    def _():
        kk = lax.broadcasted_iota(jnp.int32, (BLK, BLK), 0)
        qq = lax.broadcasted_iota(jnp.int32, (BLK, BLK), 1)
        for hh in range(2):
            h = 2 * hp + hh
            for kind in range(2):
                d = qq - kk + kind * BLK
                val = jnp.full((BLK, BLK), relb_ref[h, REL_BUCKETS - 1], F32)
                for bkt in range(REL_BUCKETS - 2, -1, -1):
                    val = jnp.where(d < BUCKET_LOWER[bkt + 1], relb_ref[h, bkt], val)
                val = val * LOG2E
                if kind == 0:
                    val = jnp.where(d >= 0, val, NEG)
                bias_ref[hh, kind] = val

    def key_means():
        per_tile = INPROJ_TILE // BLK
        ks = ksum_ref[...]
        km = jnp.concatenate([ks[j // per_tile, j % per_tile:j % per_tile + 1, :] for j in range(nblk)],
                             axis=0) * (1.0 / BLK)
        lane = lax.broadcasted_iota(jnp.int32, (nblk, LANES), 1)
        return jnp.concatenate([jnp.where(lane < HEAD_DIM, km, 0.0),
                                jnp.where(lane >= HEAD_DIM, km, 0.0)], axis=0)

    def item_tiles(t):
        i_hi = nblk - 1 - t
        tiles = [(0, t, "own"), (1, i_hi, "own"), (1, i_hi - 1, "prev")]
        if t >= 1:
            tiles.append((0, t - 1, "prev"))
        tiles += [(0, j, "far") for j in range(t - 1)]
        tiles += [(1, j, "far") for j in range(i_hi - 1)]
        assert len(tiles) == nblk + 1
        return tiles

    def score_stages(parity):
        ridx = lax.broadcasted_iota(jnp.int32, (nblk, BLK), 0)
        sub = lax.broadcasted_iota(jnp.int32, (LANES, BLK), 0)
        scale = HEAD_DIM ** -0.5 * LOG2E
        km = key_means()
        for e in range(2):
            t = 2 * parity + e
            slot = 2 * parity + e
            q_of = ((qlo_ref, e, t), (qhi_ref, 1 - e, nblk - 1 - t))
            for s, (q_ref, w, qi) in enumerate(q_of):
                qT = q_ref[0, w]
                gT = jnp.dot(km, qT, precision=HI, preferred_element_type=F32)
                past = ridx < qi
                for hh in range(2):
                    gm = jnp.where(past, gT[nblk * hh:nblk * (hh + 1)], -jnp.inf)
                    cnt = jnp.zeros((nblk, BLK), F32)
                    for jp in range(nblk):
                        row = gm[jp:jp + 1, :]
                        beats = (row > gm) | ((row == gm) & (ridx > jp))
                        cnt = cnt + jnp.where(beats, 1.0, 0.0)
                    visible = past & (cnt < MOBA_TOPK)
                    addm_ref[e, s, nblk * hh:nblk * (hh + 1), :] = jnp.where(visible, 0.0, NEG)
                    in_head = (sub >= HEAD_DIM * hh) & (sub < HEAD_DIM * (hh + 1))
                    qh_ref[e, s, hh] = jnp.where(in_head, qT * scale, 0.0).astype(BF16)
            yield
            tiles = item_tiles(t)
            for hh in range(2):
                cmax = {0: [], 1: []}
                offs = []
                for n, (s, kblk, cls) in enumerate(tiles):
                    lg = jnp.dot(kb_ref[0, kblk * BLK:(kblk + 1) * BLK, :], qh_ref[e, s, hh],
                                 preferred_element_type=F32)
                    if cls != "far":
                        lg = lg + bias_ref[hh, 0 if cls == "own" else 1]
                    lg_ref[slot, hh, n] = lg
                    cm = jnp.max(lg, axis=0, keepdims=True)
                    off = None
                    if cls != "own":
                        off = addm_ref[e, s, nblk * hh + kblk:nblk * hh + kblk + 1, :]
                        if cls == "far":
                            off = off + relb_ref[2 * hp + hh, REL_BUCKETS - 1] * LOG2E
                        cm = cm + off
                    cmax[s].append(cm)
                    offs.append(off)
                    yield
                m = {s: functools.reduce(jnp.maximum, cmax[s]) for s in (0, 1)}
                for n, (s, _, _) in enumerate(tiles):
                    moff_ref[slot, hh, n:n + 1, :] = m[s] if offs[n] is None else m[s] - offs[n]

    def softmax_pv_stages(parity):
        for e in range(2):
            t = 2 * parity + e
            slot = 2 * parity + e
            tiles = item_tiles(t)
            for hh in range(2):
                acc = {0: None, 1: None}
                for n, (s, kblk, _) in enumerate(tiles):
                    p = jnp.exp2(lg_ref[slot, hh, n] - moff_ref[slot, hh, n:n + 1, :])
                    pvn = jnp.dot(va_ref[0, kblk, hh], p.astype(BF16),
                                  preferred_element_type=F32)
                    acc[s] = pvn if acc[s] is None else acc[s] + pvn
                    yield
                for s in (0, 1):
                    oT_ref[0, 2 * e + s, HEAD_DIM * hh:HEAD_DIM * (hh + 1), :] = (
                        acc[s][0:HEAD_DIM] / acc[s][HEAD_DIM:HEAD_DIM + 1])

    ATT, GDN = 2, 1

    @pl.when(step == 0)
    def _():
        st_ref[...] = jnp.zeros(st_ref.shape, F32)
        _interleave_weighted((score_stages(0), ATT), (solve_stages(0), GDN))

    for parity in range(2):
        @pl.when((step > 0) & (step < nsteps - 1) & (step % 2 == parity))
        def _(parity=parity):
            _interleave_weighted((recurrence_stages(1 - parity, seq_start), GDN),
                                 (score_stages(parity), ATT),
                                 (solve_stages(parity), GDN),
                                 (softmax_pv_stages(1 - parity), ATT))

    @pl.when(step == nsteps - 1)
    def _():
        last = (nsteps - 2) % 2
        _interleave_weighted((recurrence_stages(last, seq_start), GDN), (softmax_pv_stages(last), ATT))


def _token_mixers(rel_bias, qT, k3, ksum, va, gqkv, sz, gb, nw_row, B, S):
    T = B * S
    W = GDN_WIDTH
    TILE = GDN_TILE
    nblk = S // MOBA_BLOCK
    half = nblk // 2
    assert TILE == GDN_HALF and S % TILE == 0
    assert BUCKET_LOWER[REL_BUCKETS - 1] <= MOBA_BLOCK + 1
    assert nblk == 8 and nblk + 1 <= 2 * SUBLANES
    nchunk = TILE // GDN_CHUNK
    ntiles = T // TILE
    npairs = (ATT_HEADS // 2) * B * (half // 2)
    assert npairs == ntiles
    nsteps = ntiles + 1

    def scored(s):
        p = jnp.minimum(s, npairs - 1)
        return p // (B * (half // 2)), (p // (half // 2)) % B, p % (half // 2)

    def done(s):
        p = jnp.maximum(s - 1, 0)
        return p // (B * (half // 2)), (p // (half // 2)) % B, p % (half // 2)

    def q_lo(s):
        hp, b, m = scored(s)
        return (b, m, hp, 0)

    def q_hi(s):
        hp, b, m = scored(s)
        return (b, half - 1 - m, hp, 0)

    def k_blk(s):
        hp, b, _ = scored(s)
        return (b, 0, hp)

    def v_blk(s):
        hp, b, _ = done(s)
        return (b, 0, hp, 0, 0)

    def o_blk(s):
        hp, b, m = done(s)
        return (b, m, hp, 0)

    cur_tile = lambda off: (lambda s: (jnp.minimum(s, ntiles - 1), off))
    prev_tile = lambda s: (jnp.maximum(s - 1, 0), 0)
    return pl.pallas_call(
        functools.partial(_mixers_kernel, nblk=nblk, nbatch=B, tiles_per_seq=S // TILE, nsteps=nsteps),
        grid=(nsteps,),
        in_specs=[
            pl.BlockSpec(memory_space=pltpu.SMEM),
            pl.BlockSpec((1, 2, LANES, MOBA_BLOCK), q_lo),
            pl.BlockSpec((1, 2, LANES, MOBA_BLOCK), q_hi),
            pl.BlockSpec((1, S, LANES), k_blk),
            pl.BlockSpec((S // INPROJ_TILE, SUBLANES, LANES), k_blk),
            pl.BlockSpec((1, nblk, 2, V_ROWS, MOBA_BLOCK), v_blk),
            pl.BlockSpec((TILE, 3 * W), cur_tile(0)),
            pl.BlockSpec((TILE, LANES), cur_tile(0)),
            pl.BlockSpec((TILE, W), prev_tile),
            pl.BlockSpec((1, W), lambda s: (0, 0)),
        ],
        out_specs=[
            pl.BlockSpec((1, 4, LANES, MOBA_BLOCK), o_blk),
            pl.BlockSpec((TILE, W), prev_tile),
        ],
        out_shape=[
            jax.ShapeDtypeStruct((B, nblk, ATT_WIDTH, MOBA_BLOCK), F32),
            jax.ShapeDtypeStruct((T, GDN_WIDTH), F32),
        ],
        scratch_shapes=[
            pltpu.VMEM((2, 2, MOBA_BLOCK, MOBA_BLOCK), F32),
            pltpu.VMEM((2, 2, 2 * nblk, MOBA_BLOCK), F32),
            pltpu.VMEM((2, 2, 2, LANES, MOBA_BLOCK), BF16),
            pltpu.VMEM((4, 2, nblk + 1, MOBA_BLOCK, MOBA_BLOCK), F32),
            pltpu.VMEM((4, 2, 2 * SUBLANES, MOBA_BLOCK), F32),
            pltpu.VMEM((2, TILE, W), F32),
            pltpu.VMEM((2, nchunk, 2 * GDN_CHUNK, W), BF16),
            pltpu.VMEM((2, TILE, W), BF16),
            pltpu.VMEM((2, TILE, W), BF16),
            pltpu.VMEM((2, nchunk * SUBLANES, W), F32),
            pltpu.VMEM((W // LANES, LANES, LANES), F32),
        ],
        compiler_params=pltpu.CompilerParams(
            dimension_semantics=("arbitrary",), vmem_limit_bytes=VMEM_LIMIT),
        name="token_mixers",
    )(rel_bias, qT, qT, k3, ksum, va, gqkv, gb, sz, nw_row)


def _out_mlp_kernel(x_ref, oTa_ref, oTb_ref, og_ref, woa_ref, wog_ref, pmn_ref, pre_ref, post_ref,
                    wup_ref, wdn_ref, out_ref):
    oT = jnp.concatenate([oTa_ref[0, 0], oTb_ref[0, 0]], axis=1)
    o_att = oT.T.astype(BF16)
    mix = jnp.dot(o_att, woa_ref[...], preferred_element_type=F32)
    mix = mix + jnp.dot(og_ref[...].astype(BF16), wog_ref[...], preferred_element_type=F32)
    x1 = x_ref[...] + _rms(mix, pmn_ref[...])
    h = _rms(x1, pre_ref[...]).astype(BF16)
    acc = jnp.zeros((ROW_TILE, D_MODEL), F32)
    for c in range(D_FF // FF_TILE):
        up = jnp.dot(h, wup_ref[:, c * FF_TILE:(c + 1) * FF_TILE], preferred_element_type=F32)
        act = jnp.square(jnp.maximum(up, 0.0)).astype(BF16)
        acc = acc + jnp.dot(act, wdn_ref[c * FF_TILE:(c + 1) * FF_TILE, :], preferred_element_type=F32)
    out_ref[...] = x1 + _rms(acc, post_ref[...])


def _out_mlp(xf, oT, og, woa, wog, pmn, pre, post, wup, wdn, B, S):
    T = B * S
    nblk = S // MOBA_BLOCK
    tiles_per_seq = S // ROW_TILE
    assert ROW_TILE == 2 * MOBA_BLOCK
    const = lambda i: (0, 0)
    row = lambda i: (i, 0)

    def att_block(which):
        def index(i):
            blk = 2 * (i % tiles_per_seq) + which
            return (i // tiles_per_seq, _paired_pos(blk, nblk), 0, 0)
        return index

    single = dict(pipeline_mode=pl.Buffered(1))
    return pl.pallas_call(
        _out_mlp_kernel,
        grid=(T // ROW_TILE,),
        in_specs=[
            pl.BlockSpec((ROW_TILE, D_MODEL), row),
            pl.BlockSpec((1, 1, ATT_WIDTH, MOBA_BLOCK), att_block(0)),
            pl.BlockSpec((1, 1, ATT_WIDTH, MOBA_BLOCK), att_block(1)),
            pl.BlockSpec((ROW_TILE, GDN_WIDTH), row),
            pl.BlockSpec(woa.shape, const, **single),
            pl.BlockSpec(wog.shape, const, **single),
            pl.BlockSpec((1, D_MODEL), const),
            pl.BlockSpec((1, D_MODEL), const),
            pl.BlockSpec((1, D_MODEL), const),
            pl.BlockSpec(wup.shape, const, **single),
            pl.BlockSpec(wdn.shape, const, **single),
        ],
        out_specs=pl.BlockSpec((ROW_TILE, D_MODEL), row),
        out_shape=jax.ShapeDtypeStruct((T, D_MODEL), F32),
        compiler_params=pltpu.CompilerParams(
            dimension_semantics=("arbitrary",), vmem_limit_bytes=VMEM_LIMIT),
        name="out_mlp",
    )(xf, oT, oT, og, woa, wog, pmn, pre, post, wup, wdn)


def kernel(x, w_in, w_out, conv_w, A_log, dt_bias, gdn_norm_w, rel_bias, pre_mix_norm,
           post_mix_norm, pre_mlp_norm, post_mlp_norm, w_up, w_down):
    B, S, D = x.shape
    assert D == D_MODEL and S % ROW_TILE == 0 and S % MOBA_BLOCK == 0
    T = B * S
    depth = w_in.shape[0]
    xf = x.reshape(T, D)
    o0, o1, o2, o3, o4 = 0, ATT_WIDTH, 2 * ATT_WIDTH, 3 * ATT_WIDTH, 3 * ATT_WIDTH + 3 * GDN_WIDTH
    o5 = o4 + GDN_WIDTH
    for l in range(depth):
        wi = w_in[l]
        wqT = wi[:, o0:o1].T.astype(BF16)
        wk = wi[:, o1:o2].astype(BF16)
        wvT = wi[:, o2:o3].T.astype(BF16)
        wg = wi[:, o3:o4].astype(BF16)
        wz = wi[:, o4:o5].astype(BF16)
        wab = jnp.pad(wi[:, o5:], ((0, 0), (0, LANES - 2 * GDN_HEADS))).astype(BF16)
        pad8 = lambda v: jnp.pad(v.astype(F32), (0, LANES - GDN_HEADS))[None, :]
        qT, k, ksum, va, gqkv, sz, gb = _inproj(xf, pre_mix_norm[l][None, :], wqT, wk, wvT, wg, wz, wab,
                                                conv_w[l], pad8(A_log[l]), pad8(dt_bias[l]), B, S)
        oT, og = _token_mixers(rel_bias.astype(F32), qT, k.reshape(B, S, ATT_WIDTH), ksum, va, gqkv, sz,
                               gb, jnp.tile(gdn_norm_w[l], GDN_HEADS)[None, :], B, S)
        wo = w_out[l].astype(BF16)
        xf = _out_mlp(xf, oT, og, wo[:ATT_WIDTH], wo[ATT_WIDTH:], post_mix_norm[l][None, :],
                      pre_mlp_norm[l][None, :], post_mlp_norm[l][None, :],
                      w_up[l].astype(BF16), w_down[l].astype(BF16), B, S)
    return xf.reshape(B, S, D)
```

```python
import functools
import math

import jax
import jax.numpy as jnp
from jax import lax
from jax.experimental import pallas as pl
from jax.experimental.pallas import tpu as pltpu

F32 = jnp.float32
BF16 = jnp.bfloat16
HI = lax.Precision.HIGHEST

D_MODEL = 1024
HEAD_DIM = 64
ATT_HEADS = 8
GDN_HEADS = 8
ATT_WIDTH = ATT_HEADS * HEAD_DIM
GDN_WIDTH = GDN_HEADS * HEAD_DIM
MOBA_BLOCK = 256
MOBA_TOPK = 3
GDN_CHUNK = 64
CONV_WIDTH = 4
D_FF = 4 * D_MODEL
REL_BUCKETS = 32
REL_MAX_EXACT = 16
REL_MAX_DIST = 128
EPS = 1e-6
NEG = -1e30
LOG2E = math.log2(math.e)

LANES = 128
SUBLANES = 8
VMEM_LIMIT = 56 * 1024 * 1024
ROW_TILE = 512
INPROJ_TILE = 512
FF_TILE = 1024

NT = (((1,), (1,)), ((), ()))
TN = (((0,), (0,)), ((), ()))


def _bucket_lower_bounds():
    def bucket(d):
        if d < REL_MAX_EXACT:
            return d
        t = math.log(d / REL_MAX_EXACT) / math.log(REL_MAX_DIST / REL_MAX_EXACT)
        t = t * (REL_BUCKETS - REL_MAX_EXACT)
        assert d in (REL_MAX_EXACT, REL_MAX_DIST) or abs(t - round(t)) > 1e-6
        return min(REL_MAX_EXACT + int(t + 1e-9), REL_BUCKETS - 1)
    lower = []
    for b in range(REL_BUCKETS):
        d = 0
        while bucket(d) < b:
            d += 1
        lower.append(d)
    return lower


BUCKET_LOWER = _bucket_lower_bounds()


def _sigmoid(x):
    return 0.5 * jnp.tanh(0.5 * x) + 0.5


def _silu_of_half(h):
    return h + h * jnp.tanh(h)


def _rms(x, w):
    return x * lax.rsqrt(jnp.mean(x * x, axis=-1, keepdims=True) + EPS) * w


def _split_bf16(x, parts):
    out = []
    for _ in range(parts):
        h = x.astype(BF16)
        out.append(h)
        x = x - h.astype(F32)
    return out


def _dot_split_rhs(c, x, parts):
    acc = None
    for h in _split_bf16(x, parts):
        d = jnp.dot(c, h, preferred_element_type=F32)
        acc = d if acc is None else acc + d
    return acc


CONV_COLS = 512


def _inproj_kernel(x_ref, xp_ref, nw_ref, wqT_ref, wk_ref, wvT_ref, wg_ref, wz_ref, wab_ref,
                   cw_ref, alog_ref, dtb_ref,
                   qT_ref, k_ref, ksum_ref, va_ref, g_ref, z_ref, gb_ref, *, tiles_per_seq):
    h = _rms(x_ref[...], nw_ref[...]).astype(BF16)

    hp = _rms(xp_ref[...], nw_ref[...]).astype(BF16)
    seq_start = (pl.program_id(0) % tiles_per_seq) == 0
    trow8 = lax.broadcasted_iota(jnp.int32, (SUBLANES, CONV_COLS), 0)
    for c in range(3 * GDN_WIDTH // CONV_COLS):
        cols = slice(c * CONV_COLS, (c + 1) * CONV_COLS)
        cur = jnp.dot(h, wg_ref[:, cols], preferred_element_type=F32)
        prev8 = jnp.dot(hp, wg_ref[:, cols], preferred_element_type=F32)
        prev8 = jnp.where(seq_start, 0.0, prev8)
        cw_half = 0.5 * cw_ref[:, cols]
        acc = cur * cw_half[CONV_WIDTH - 1:CONV_WIDTH]
        for s in range(1, CONV_WIDTH):
            rolled = pltpu.roll(cur, s, 0)
            top = jnp.where(trow8 < s, pltpu.roll(prev8, s, 0), rolled[0:SUBLANES])
            tap = jnp.concatenate([top, rolled[SUBLANES:]], axis=0)
            acc = acc + tap * cw_half[CONV_WIDTH - 1 - s:CONV_WIDTH - s]
        g_ref[:, cols] = _silu_of_half(acc)

    z = jnp.dot(h, wz_ref[...], preferred_element_type=F32)
    z_ref[...] = _silu_of_half(0.5 * z)
    ab = jnp.dot(h, wab_ref[...], preferred_element_type=F32)
    xs = ab + dtb_ref[...]
    log_decay = -jnp.exp(alog_ref[...]) * (jnp.maximum(xs, 0.0) + jnp.log1p(jnp.exp(-jnp.abs(xs))))
    lane = lax.broadcasted_iota(jnp.int32, ab.shape, 1)
    gb_ref[...] = jnp.where(lane < GDN_HEADS, log_decay, _sigmoid(ab))

    qT = lax.dot_general(wqT_ref[...], h, NT, preferred_element_type=F32)
    vT = lax.dot_general(wvT_ref[...], h, NT, preferred_element_type=F32)
    k = jnp.dot(h, wk_ref[...], preferred_element_type=F32)
    k_ref[...] = k.astype(BF16)
    ones_row = jnp.where(lax.broadcasted_iota(jnp.int32, (V_ROWS - HEAD_DIM, MOBA_BLOCK), 0) == 0,
                         1.0, 0.0).astype(BF16)
    ksum_ref[...] = jnp.zeros(ksum_ref.shape, F32)
    for t in range(INPROJ_TILE // MOBA_BLOCK):
        blk = slice(t * MOBA_BLOCK, (t + 1) * MOBA_BLOCK)
        qT_ref[0, t] = qT[:, blk]
        ksum_ref[0, t:t + 1, :] = jnp.sum(k[blk], axis=0, keepdims=True)
        for hh in range(ATT_HEADS):
            va_ref[0, t, hh, 0:HEAD_DIM, :] = vT[HEAD_DIM * hh:HEAD_DIM * (hh + 1), blk].astype(BF16)
            va_ref[0, t, hh, HEAD_DIM:V_ROWS, :] = ones_row


def _inproj(xf, nw, wqT, wk, wvT, wg, wz, wab, conv_w, alog_pad, dtb_pad, B, S):
    T = B * S
    TM = INPROJ_TILE
    assert S % TM == 0
    nblk = S // MOBA_BLOCK
    tiles_per_seq = S // TM
    blk_per_tile = TM // MOBA_BLOCK
    const = lambda i: (0, 0)
    row = lambda i: (i, 0)
    tr = lambda i: (i // tiles_per_seq, i % tiles_per_seq, 0, 0)
    prev_rows = lambda i: (jnp.maximum(i * (TM // SUBLANES) - 1, 0), 0)
    single = dict(pipeline_mode=pl.Buffered(1))
    return pl.pallas_call(
        functools.partial(_inproj_kernel, tiles_per_seq=tiles_per_seq),
        grid=(T // TM,),
        in_specs=[
            pl.BlockSpec((TM, D_MODEL), row),
            pl.BlockSpec((SUBLANES, D_MODEL), prev_rows),
            pl.BlockSpec((1, D_MODEL), const),
            pl.BlockSpec(wqT.shape, const, **single),
            pl.BlockSpec(wk.shape, const, **single),
            pl.BlockSpec(wvT.shape, const, **single),
            pl.BlockSpec(wg.shape, const, **single),
            pl.BlockSpec(wz.shape, const, **single),
            pl.BlockSpec(wab.shape, const, **single),
            pl.BlockSpec(conv_w.shape, const),
            pl.BlockSpec((1, LANES), const),
            pl.BlockSpec((1, LANES), const),
        ],
        out_specs=[
            pl.BlockSpec((1, blk_per_tile, ATT_WIDTH, MOBA_BLOCK), tr),
            pl.BlockSpec((TM, ATT_WIDTH), row),
            pl.BlockSpec((1, SUBLANES, ATT_WIDTH), lambda i: (i, 0, 0)),
            pl.BlockSpec((1, blk_per_tile, ATT_HEADS, V_ROWS, MOBA_BLOCK),
                         lambda i: (i // tiles_per_seq, i % tiles_per_seq, 0, 0, 0)),
            pl.BlockSpec((TM, 3 * GDN_WIDTH), row),
            pl.BlockSpec((TM, GDN_WIDTH), row),
            pl.BlockSpec((TM, LANES), row),
        ],
        out_shape=[
            jax.ShapeDtypeStruct((B, nblk, ATT_WIDTH, MOBA_BLOCK), F32),
            jax.ShapeDtypeStruct((T, ATT_WIDTH), BF16),
            jax.ShapeDtypeStruct((T // TM, SUBLANES, ATT_WIDTH), F32),
            jax.ShapeDtypeStruct((B, nblk, ATT_HEADS, V_ROWS, MOBA_BLOCK), BF16),
            jax.ShapeDtypeStruct((T, 3 * GDN_WIDTH), F32),
            jax.ShapeDtypeStruct((T, GDN_WIDTH), F32),
            jax.ShapeDtypeStruct((T, LANES), F32),
        ],
        compiler_params=pltpu.CompilerParams(
            dimension_semantics=("arbitrary",), vmem_limit_bytes=VMEM_LIMIT),
        name="inproj",
    )(xf, xf, nw, wqT, wk, wvT, wg, wz, wab, conv_w, alog_pad, dtb_pad)


V_ROWS = HEAD_DIM + 16


def _paired_pos(i, nblk):
    return jnp.where(i < nblk // 2, 2 * i, 2 * (nblk - 1 - i) + 1)


GDN_TILE = 256
GDN_HALF = 2 * LANES


def _gdn_stages(y_ref, gb_ref, sz_ref, nw_ref, out_ref,
                u_ref, wq_ref, a_ref, kd_ref, gl_ref, st_ref):
    C = GDN_CHUNK
    W = GDN_WIDTH
    TILE = GDN_TILE
    npair = W // LANES

    r_w = lax.broadcasted_iota(jnp.int32, (GDN_HALF, GDN_HALF), 0)
    c_w = lax.broadcasted_iota(jnp.int32, (GDN_HALF, GDN_HALF), 1)
    head_ones = jnp.where((r_w // HEAD_DIM) == (c_w // HEAD_DIM), 1.0, 0.0).astype(BF16)
    ltri_bd = jnp.where(((r_w // C) == (c_w // C)) & (c_w <= r_w), 1.0, 0.0).astype(BF16)
    tok = lax.broadcasted_iota(jnp.int32, (TILE, W), 0) % C
    col = lax.broadcasted_iota(jnp.int32, (TILE, W), 1) % HEAD_DIM
    causal_t = tok >= col
    strict_t = tok > col
    lane_t = lax.broadcasted_iota(jnp.int32, (TILE, LANES), 1)

    lane = lax.broadcasted_iota(jnp.int32, (C, LANES), 1)
    rowi = lax.broadcasted_iota(jnp.int32, (C, LANES), 0)
    first_head = lane < HEAD_DIM
    strict = rowi > (lane % HEAD_DIM)
    eye2 = jnp.where(rowi == (lane % HEAD_DIM), 1.0, 0.0)
    lane2 = lax.broadcasted_iota(jnp.int32, (C, 2 * LANES), 1)
    first_head2 = (lane2 % LANES) < HEAD_DIM
    r_l = lax.broadcasted_iota(jnp.int32, (LANES, LANES), 0)
    c_l = lax.broadcasted_iota(jnp.int32, (LANES, LANES), 1)
    same_head = (r_l // HEAD_DIM) == (c_l // HEAD_DIM)
    pair_ones = jnp.where(same_head, 1.0, 0.0).astype(BF16)

    def stack(x, mask):
        return jnp.concatenate([jnp.where(mask, x, 0.0), jnp.where(mask, 0.0, x)], axis=0)

    dot = functools.partial(jnp.dot, preferred_element_type=F32)

    def head_sumsq(ys):
        halves = [(y * y).astype(BF16)[:, h:h + GDN_HALF] for y in ys for h in range(0, W, GDN_HALF)]
        sums = dot(jnp.concatenate(halves, axis=0), head_ones)
        per = W // GDN_HALF
        return [jnp.concatenate([sums[(i * per + j) * TILE:(i * per + j + 1) * TILE] for j in range(per)],
                                axis=1) for i in range(len(ys))]

    def solve_stages(slot):
        yq = y_ref[:, 0:W]
        yk = y_ref[:, W:2 * W]
        yv = y_ref[:, 2 * W:3 * W]
        ssq, ssk = head_sumsq([yq, yk])
        qn = yq * lax.rsqrt(ssq + EPS) * (HEAD_DIM ** -0.5)
        kn = yk * lax.rsqrt(ssk + EPS)
        yield
        gbt = gb_ref[...]

        def spread(col0):
            pairs = []
            for p in range(npair):
                a = jnp.broadcast_to(gbt[:, col0 + 2 * p:col0 + 2 * p + 1], (TILE, LANES))
                b = jnp.broadcast_to(gbt[:, col0 + 2 * p + 1:col0 + 2 * p + 2], (TILE, LANES))
                pairs.append(jnp.where(lane_t < HEAD_DIM, a, b))
            return jnp.concatenate(pairs, axis=1)

        g = spread(0)
        beta = spread(GDN_HEADS)
        gcd = _dot_split_rhs(ltri_bd, jnp.concatenate([g, jnp.where(strict_t, g, 0.0)], axis=1), 2)
        yield
        gc = gcd[:, :W]
        decay = jnp.where(causal_t, jnp.exp(jnp.where(causal_t, gcd[:, W:], 0.0)), 0.0)
        egc = jnp.exp(gc)
        kb = kn * beta
        rv = yv * beta
        rk = kb * egc
        qd = qn * egc
        for cc in range(TILE // C):
            rs = slice(cc * C, (cc + 1) * C)
            g_last = gc[(cc + 1) * C - 1:(cc + 1) * C, :]
            kd_ref[slot, rs, :] = (kn[rs] * jnp.exp(g_last - gc[rs])).astype(BF16)
            gl_ref[slot, cc * SUBLANES:(cc + 1) * SUBLANES, :] = (
                jnp.broadcast_to(jnp.exp(g_last), (SUBLANES, W)))
        units = [(slice(cc * C, (cc + 1) * C), slice(LANES * p, LANES * (p + 1)), cc)
                 for cc in range(TILE // C) for p in range(npair)]
        kqs = [lax.dot_general(jnp.concatenate([kn[rs, ls], qn[rs, ls]], axis=0).astype(BF16),
                               stack(kn[rs, ls], first_head).astype(BF16), NT,
                               preferred_element_type=F32) for rs, ls, _ in units]
        yield
        ps = [-jnp.where(strict, kq[0:C] * beta[rs, ls] * decay[rs, ls], 0.0)
              for kq, (rs, ls, _) in zip(kqs, units)]
        ss = [eye2 + p for p in ps]
        ps = [dot(p.astype(BF16), stack(p, first_head).astype(BF16)) for p in ps]
        yield
        nround = int(math.log2(C))
        for k in range(1, nround):
            rhs = [stack(s_, first_head).astype(BF16) for s_ in ss]
            if k + 1 < nround:
                rhs = [jnp.concatenate([stack(p, first_head).astype(BF16), sx], axis=1)
                       for p, sx in zip(ps, rhs)]
            outs = [dot(p.astype(BF16), sx) for p, sx in zip(ps, rhs)]
            if k + 1 < nround:
                ps = [o[:, :LANES] for o in outs]
                ss = [s_ + o[:, LANES:] for s_, o in zip(ss, outs)]
            else:
                ss = [s_ + o for s_, o in zip(ss, outs)]
            yield
        xs = [dot(s_.astype(BF16),
                  stack(jnp.concatenate([rv[rs, ls], rk[rs, ls]], axis=1), first_head2).astype(BF16))
              for s_, (rs, ls, _) in zip(ss, units)]
        yield
        for x, kq, (rs, ls, cc) in zip(xs, kqs, units):
            u_ref[slot, rs, ls] = x[:, :LANES]
            wq_ref[slot, cc, 0:C, ls] = x[:, LANES:].astype(BF16)
            wq_ref[slot, cc, C:2 * C, ls] = qd[rs, ls].astype(BF16)
            a_ref[slot, rs, ls] = (kq[C:2 * C] * decay[rs, ls]).astype(BF16)

    lss = [slice(LANES * p, LANES * (p + 1)) for p in range(npair)]

    def recurrence_stages(slot, seq_start):
        states = [jnp.where(seq_start, 0.0, st_ref[p]) for p in range(npair)]
        pending = None

        def finish(rs, os_):
            sq = jnp.concatenate([(o * o).astype(BF16) for o in os_], axis=0)
            ms_all = dot(sq, pair_ones) * (1.0 / HEAD_DIM)
            for p, (ls, o) in enumerate(zip(lss, os_)):
                ms = ms_all[p * C:(p + 1) * C]
                out_ref[rs, ls] = o * lax.rsqrt(ms + EPS) * nw_ref[:, ls] * sz_ref[rs, ls]

        for cc in range(TILE // C):
            rs = slice(cc * C, (cc + 1) * C)
            wqs = [dot(wq_ref[slot, cc, :, ls], st.astype(BF16)) for ls, st in zip(lss, states)]
            if pending is not None:
                finish(*pending)
            yield
            v_news = [u_ref[slot, rs, ls] - wq[0:C] for ls, wq in zip(lss, wqs)]
            kvs = [lax.dot_general(kd_ref[slot, rs, ls], v.astype(BF16), TN, preferred_element_type=F32)
                   for ls, v in zip(lss, v_news)]
            os_ = [wq[C:2 * C] + dot(a_ref[slot, rs, ls], stack(v, first_head).astype(BF16))
                   for ls, wq, v in zip(lss, wqs, v_news)]
            states = [st * gl_ref[slot, cc * SUBLANES:cc * SUBLANES + 1, ls] + jnp.where(same_head, kv, 0.0)
                      for ls, st, kv in zip(lss, states, kvs)]
            pending = (rs, os_)
            yield
        finish(*pending)
        for p in range(npair):
            st_ref[p] = states[p]

    return solve_stages, recurrence_stages


def _interleave_weighted(*gens_and_weights):
    live = [[gen, weight] for gen, weight in gens_and_weights]
    while live:
        for entry in list(live):
            for _ in range(entry[1]):
                try:
                    next(entry[0])
                except StopIteration:
                    live.remove(entry)
                    break


def _mixers_kernel(relb_ref, qlo_ref, qhi_ref, kb_ref, ksum_ref, va_ref,
                   y_ref, gb_ref, sz_ref, nw_ref,
                   oT_ref, og_ref,
                   bias_ref, addm_ref, qh_ref, lg_ref, moff_ref,
                   u_ref, wq_ref, a_ref, kd_ref, gl_ref, st_ref,
                   *, nblk, nbatch, tiles_per_seq, nsteps):
    step = pl.program_id(0)
    BLK = MOBA_BLOCK
    half = nblk // 2
    pair = jnp.minimum(step, nsteps - 2)
    hp = pair // (nbatch * (half // 2))
    b = (pair // (half // 2)) % nbatch
    scoring = step < nsteps - 1
    new_kv = (pair % (half // 2)) == 0

    solve_stages, recurrence_stages = _gdn_stages(
        y_ref, gb_ref, sz_ref, nw_ref, og_ref,
        u_ref, wq_ref, a_ref, kd_ref, gl_ref, st_ref)
    seq_start = ((step - 1) % tiles_per_seq) == 0

    @pl.when((b == 0) & new_kv & scoring)
    def _():
        kk = lax.broadcasted_iota(jnp.int32, (BLK, BLK), 0)
        qq = lax.broadcasted_iota(jnp.int32, (BLK, BLK), 1)
        for hh in range(2):
            h = 2 * hp + hh
            for kind in range(2):
                d = qq - kk + kind * BLK
                val = jnp.full((BLK, BLK), relb_ref[h, REL_BUCKETS - 1], F32)
                for bkt in range(REL_BUCKETS - 2, -1, -1):
                    val = jnp.where(d < BUCKET_LOWER[bkt + 1], relb_ref[h, bkt], val)
                val = val * LOG2E
                if kind == 0:
                    val = jnp.where(d >= 0, val, NEG)
                bias_ref[hh, kind] = val

    def key_means():
        per_tile = INPROJ_TILE // BLK
        ks = ksum_ref[...]
        km = jnp.concatenate([ks[j // per_tile, j % per_tile:j % per_tile + 1, :] for j in range(nblk)],
                             axis=0) * (1.0 / BLK)
        lane = lax.broadcasted_iota(jnp.int32, (nblk, LANES), 1)
        return jnp.concatenate([jnp.where(lane < HEAD_DIM, km, 0.0),
                                jnp.where(lane >= HEAD_DIM, km, 0.0)], axis=0)

    def item_tiles(t):
        i_hi = nblk - 1 - t
        tiles = [(0, t, "own"), (1, i_hi, "own"), (1, i_hi - 1, "prev")]
        if t >= 1:
            tiles.append((0, t - 1, "prev"))
        tiles += [(0, j, "far") for j in range(t - 1)]
        tiles += [(1, j, "far") for j in range(i_hi - 1)]
        assert len(tiles) == nblk + 1
        return tiles

    def score_stages(parity):
        ridx = lax.broadcasted_iota(jnp.int32, (nblk, BLK), 0)
        sub = lax.broadcasted_iota(jnp.int32, (LANES, BLK), 0)
        scale = HEAD_DIM ** -0.5 * LOG2E
        km = key_means()
        for e in range(2):
            t = 2 * parity + e
            slot = 2 * parity + e
            q_of = ((qlo_ref, e, t), (qhi_ref, 1 - e, nblk - 1 - t))
            for s, (q_ref, w, qi) in enumerate(q_of):
                qT = q_ref[0, w]
                gT = jnp.dot(km, qT, precision=HI, preferred_element_type=F32)
                past = ridx < qi
                for hh in range(2):
                    gm = jnp.where(past, gT[nblk * hh:nblk * (hh + 1)], -jnp.inf)
                    cnt = jnp.zeros((nblk, BLK), F32)
                    for jp in range(nblk):
                        row = gm[jp:jp + 1, :]
                        beats = (row > gm) | ((row == gm) & (ridx > jp))
                        cnt = cnt + jnp.where(beats, 1.0, 0.0)
                    visible = past & (cnt < MOBA_TOPK)
                    addm_ref[e, s, nblk * hh:nblk * (hh + 1), :] = jnp.where(visible, 0.0, NEG)
                    in_head = (sub >= HEAD_DIM * hh) & (sub < HEAD_DIM * (hh + 1))
                    qh_ref[e, s, hh] = jnp.where(in_head, qT * scale, 0.0).astype(BF16)
            yield
            tiles = item_tiles(t)
            for hh in range(2):
                cmax = {0: [], 1: []}
                offs = []
                for n, (s, kblk, cls) in enumerate(tiles):
                    lg = jnp.dot(kb_ref[0, kblk * BLK:(kblk + 1) * BLK, :], qh_ref[e, s, hh],
                                 preferred_element_type=F32)
                    if cls != "far":
                        lg = lg + bias_ref[hh, 0 if cls == "own" else 1]
                    lg_ref[slot, hh, n] = lg
                    cm = jnp.max(lg, axis=0, keepdims=True)
                    off = None
                    if cls != "own":
                        off = addm_ref[e, s, nblk * hh + kblk:nblk * hh + kblk + 1, :]
                        if cls == "far":
                            off = off + relb_ref[2 * hp + hh, REL_BUCKETS - 1] * LOG2E
                        cm = cm + off
                    cmax[s].append(cm)
                    offs.append(off)
                    yield
                m = {s: functools.reduce(jnp.maximum, cmax[s]) for s in (0, 1)}
                for n, (s, _, _) in enumerate(tiles):
                    moff_ref[slot, hh, n:n + 1, :] = m[s] if offs[n] is None else m[s] - offs[n]

    def softmax_pv_stages(parity):
        for e in range(2):
            t = 2 * parity + e
            slot = 2 * parity + e
            tiles = item_tiles(t)
            for hh in range(2):
                acc = {0: None, 1: None}
                for n, (s, kblk, _) in enumerate(tiles):
                    p = jnp.exp2(lg_ref[slot, hh, n] - moff_ref[slot, hh, n:n + 1, :])
                    pvn = jnp.dot(va_ref[0, kblk, hh], p.astype(BF16),
                                  preferred_element_type=F32)
                    acc[s] = pvn if acc[s] is None else acc[s] + pvn
                    yield
                for s in (0, 1):
                    oT_ref[0, 2 * e + s, HEAD_DIM * hh:HEAD_DIM * (hh + 1), :] = (
                        acc[s][0:HEAD_DIM] / acc[s][HEAD_DIM:HEAD_DIM + 1])

    ATT, GDN = 2, 1

    @pl.when(step == 0)
    def _():
        st_ref[...] = jnp.zeros(st_ref.shape, F32)
        _interleave_weighted((score_stages(0), ATT), (solve_stages(0), GDN))

    for parity in range(2):
        @pl.when((step > 0) & (step < nsteps - 1) & (step % 2 == parity))
        def _(parity=parity):
            _interleave_weighted((recurrence_stages(1 - parity, seq_start), GDN),
                                 (score_stages(parity), ATT),
                                 (solve_stages(parity), GDN),
                                 (softmax_pv_stages(1 - parity), ATT))

    @pl.when(step == nsteps - 1)
    def _():
        last = (nsteps - 2) % 2
        _interleave_weighted((recurrence_stages(last, seq_start), GDN), (softmax_pv_stages(last), ATT))


def _token_mixers(rel_bias, qT, k3, ksum, va, gqkv, sz, gb, nw_row, B, S):
    T = B * S
    W = GDN_WIDTH
    TILE = GDN_TILE
    nblk = S // MOBA_BLOCK
    half = nblk // 2
    assert TILE == GDN_HALF and S % TILE == 0
    assert BUCKET_LOWER[REL_BUCKETS - 1] <= MOBA_BLOCK + 1
    assert nblk == 8 and nblk + 1 <= 2 * SUBLANES
    nchunk = TILE // GDN_CHUNK
    ntiles = T // TILE
    npairs = (ATT_HEADS // 2) * B * (half // 2)
    assert npairs == ntiles
    nsteps = ntiles + 1

    def scored(s):
        p = jnp.minimum(s, npairs - 1)
        return p // (B * (half // 2)), (p // (half // 2)) % B, p % (half // 2)

    def done(s):
        p = jnp.maximum(s - 1, 0)
        return p // (B * (half // 2)), (p // (half // 2)) % B, p % (half // 2)

    def q_lo(s):
        hp, b, m = scored(s)
        return (b, m, hp, 0)

    def q_hi(s):
        hp, b, m = scored(s)
        return (b, half - 1 - m, hp, 0)

    def k_blk(s):
        hp, b, _ = scored(s)
        return (b, 0, hp)

    def v_blk(s):
        hp, b, _ = done(s)
        return (b, 0, hp, 0, 0)

    def o_blk(s):
        hp, b, m = done(s)
        return (b, m, hp, 0)

    cur_tile = lambda off: (lambda s: (jnp.minimum(s, ntiles - 1), off))
    prev_tile = lambda s: (jnp.maximum(s - 1, 0), 0)
    return pl.pallas_call(
        functools.partial(_mixers_kernel, nblk=nblk, nbatch=B, tiles_per_seq=S // TILE, nsteps=nsteps),
        grid=(nsteps,),
        in_specs=[
            pl.BlockSpec(memory_space=pltpu.SMEM),
            pl.BlockSpec((1, 2, LANES, MOBA_BLOCK), q_lo),
            pl.BlockSpec((1, 2, LANES, MOBA_BLOCK), q_hi),
            pl.BlockSpec((1, S, LANES), k_blk),
            pl.BlockSpec((S // INPROJ_TILE, SUBLANES, LANES), k_blk),
            pl.BlockSpec((1, nblk, 2, V_ROWS, MOBA_BLOCK), v_blk),
            pl.BlockSpec((TILE, 3 * W), cur_tile(0)),
            pl.BlockSpec((TILE, LANES), cur_tile(0)),
            pl.BlockSpec((TILE, W), prev_tile),
            pl.BlockSpec((1, W), lambda s: (0, 0)),
        ],
        out_specs=[
            pl.BlockSpec((1, 4, LANES, MOBA_BLOCK), o_blk),
            pl.BlockSpec((TILE, W), prev_tile),
        ],
        out_shape=[
            jax.ShapeDtypeStruct((B, nblk, ATT_WIDTH, MOBA_BLOCK), F32),
            jax.ShapeDtypeStruct((T, GDN_WIDTH), F32),
        ],
        scratch_shapes=[
            pltpu.VMEM((2, 2, MOBA_BLOCK, MOBA_BLOCK), F32),
            pltpu.VMEM((2, 2, 2 * nblk, MOBA_BLOCK), F32),
            pltpu.VMEM((2, 2, 2, LANES, MOBA_BLOCK), BF16),
            pltpu.VMEM((4, 2, nblk + 1, MOBA_BLOCK, MOBA_BLOCK), F32),
            pltpu.VMEM((4, 2, 2 * SUBLANES, MOBA_BLOCK), F32),
            pltpu.VMEM((2, TILE, W), F32),
            pltpu.VMEM((2, nchunk, 2 * GDN_CHUNK, W), BF16),
            pltpu.VMEM((2, TILE, W), BF16),
            pltpu.VMEM((2, TILE, W), BF16),
            pltpu.VMEM((2, nchunk * SUBLANES, W), F32),
            pltpu.VMEM((W // LANES, LANES, LANES), F32),
        ],
        compiler_params=pltpu.CompilerParams(
            dimension_semantics=("arbitrary",), vmem_limit_bytes=VMEM_LIMIT),
        name="token_mixers",
    )(rel_bias, qT, qT, k3, ksum, va, gqkv, gb, sz, nw_row)


def _out_mlp_kernel(x_ref, oTa_ref, oTb_ref, og_ref, woa_ref, wog_ref, pmn_ref, pre_ref, post_ref,
                    wup_ref, wdn_ref, out_ref, x1_ref, h_ref, *, nsteps):
    step = pl.program_id(0)

    def mix_stages(slot):
        oT = jnp.concatenate([oTa_ref[0, 0], oTb_ref[0, 0]], axis=1)
        mix = jnp.dot(oT.T.astype(BF16), woa_ref[...], preferred_element_type=F32)
        yield
        mix = mix + jnp.dot(og_ref[...].astype(BF16), wog_ref[...], preferred_element_type=F32)
        yield
        x1 = x_ref[...] + _rms(mix, pmn_ref[...])
        x1_ref[slot] = x1
        h_ref[slot] = _rms(x1, pre_ref[...]).astype(BF16)
        yield

    def mlp_stages(slot):
        h = h_ref[slot]
        acc = jnp.zeros((ROW_TILE, D_MODEL), F32)
        for c in range(D_FF // FF_TILE):
            up = jnp.dot(h, wup_ref[:, c * FF_TILE:(c + 1) * FF_TILE], preferred_element_type=F32)
            act = jnp.square(jnp.maximum(up, 0.0)).astype(BF16)
            yield
            acc = acc + jnp.dot(act, wdn_ref[c * FF_TILE:(c + 1) * FF_TILE, :], preferred_element_type=F32)
            yield
        out_ref[...] = x1_ref[slot] + _rms(acc, post_ref[...])
        yield

    @pl.when(step == 0)
    def _():
        _interleave_weighted((mix_stages(0), 1))

    for parity in range(2):
        @pl.when((step > 0) & (step < nsteps - 1) & (step % 2 == parity))
        def _(parity=parity):
            _interleave_weighted((mlp_stages(1 - parity), 1), (mix_stages(parity), 1))

    @pl.when(step == nsteps - 1)
    def _():
        _interleave_weighted((mlp_stages((nsteps - 2) % 2), 1))


def _out_mlp(xf, oT, og, woa, wog, pmn, pre, post, wup, wdn, B, S):
    T = B * S
    nblk = S // MOBA_BLOCK
    tiles_per_seq = S // ROW_TILE
    assert ROW_TILE == 2 * MOBA_BLOCK
    ntiles = T // ROW_TILE
    nsteps = ntiles + 1
    const = lambda i: (0, 0)
    row = lambda i: (jnp.minimum(i, ntiles - 1), 0)
    out_row = lambda i: (jnp.maximum(i - 1, 0), 0)

    def att_block(which):
        def index(i):
            i = jnp.minimum(i, ntiles - 1)
            blk = 2 * (i % tiles_per_seq) + which
            return (i // tiles_per_seq, _paired_pos(blk, nblk), 0, 0)
        return index

    single = dict(pipeline_mode=pl.Buffered(1))
    return pl.pallas_call(
        functools.partial(_out_mlp_kernel, nsteps=nsteps),
        grid=(nsteps,),
        in_specs=[
            pl.BlockSpec((ROW_TILE, D_MODEL), row),
            pl.BlockSpec((1, 1, ATT_WIDTH, MOBA_BLOCK), att_block(0)),
            pl.BlockSpec((1, 1, ATT_WIDTH, MOBA_BLOCK), att_block(1)),
            pl.BlockSpec((ROW_TILE, GDN_WIDTH), row),
            pl.BlockSpec(woa.shape, const, **single),
            pl.BlockSpec(wog.shape, const, **single),
            pl.BlockSpec((1, D_MODEL), const),
            pl.BlockSpec((1, D_MODEL), const),
            pl.BlockSpec((1, D_MODEL), const),
            pl.BlockSpec(wup.shape, const, **single),
            pl.BlockSpec(wdn.shape, const, **single),
        ],
        out_specs=pl.BlockSpec((ROW_TILE, D_MODEL), out_row),
        out_shape=jax.ShapeDtypeStruct((T, D_MODEL), F32),
        scratch_shapes=[pltpu.VMEM((2, ROW_TILE, D_MODEL), F32),
                        pltpu.VMEM((2, ROW_TILE, D_MODEL), BF16)],
        compiler_params=pltpu.CompilerParams(
            dimension_semantics=("arbitrary",), vmem_limit_bytes=VMEM_LIMIT),
        name="out_mlp",
    )(xf, oT, oT, og, woa, wog, pmn, pre, post, wup, wdn)


def kernel(x, w_in, w_out, conv_w, A_log, dt_bias, gdn_norm_w, rel_bias, pre_mix_norm,
           post_mix_norm, pre_mlp_norm, post_mlp_norm, w_up, w_down):
    B, S, D = x.shape
    assert D == D_MODEL and S % ROW_TILE == 0 and S % MOBA_BLOCK == 0
    T = B * S
    depth = w_in.shape[0]
    xf = x.reshape(T, D)
    o0, o1, o2, o3, o4 = 0, ATT_WIDTH, 2 * ATT_WIDTH, 3 * ATT_WIDTH, 3 * ATT_WIDTH + 3 * GDN_WIDTH
    o5 = o4 + GDN_WIDTH
    for l in range(depth):
        wi = w_in[l]
        wqT = wi[:, o0:o1].T.astype(BF16)
        wk = wi[:, o1:o2].astype(BF16)
        wvT = wi[:, o2:o3].T.astype(BF16)
        wg = wi[:, o3:o4].astype(BF16)
        wz = wi[:, o4:o5].astype(BF16)
        wab = jnp.pad(wi[:, o5:], ((0, 0), (0, LANES - 2 * GDN_HEADS))).astype(BF16)
        pad8 = lambda v: jnp.pad(v.astype(F32), (0, LANES - GDN_HEADS))[None, :]
        qT, k, ksum, va, gqkv, sz, gb = _inproj(xf, pre_mix_norm[l][None, :], wqT, wk, wvT, wg, wz, wab,
                                                conv_w[l], pad8(A_log[l]), pad8(dt_bias[l]), B, S)
        oT, og = _token_mixers(rel_bias.astype(F32), qT, k.reshape(B, S, ATT_WIDTH), ksum, va, gqkv, sz,
                               gb, jnp.tile(gdn_norm_w[l], GDN_HEADS)[None, :], B, S)
        wo = w_out[l].astype(BF16)
        xf = _out_mlp(xf, oT, og, wo[:ATT_WIDTH], wo[ATT_WIDTH:], post_mix_norm[l][None, :],
                      pre_mlp_norm[l][None, :], post_mlp_norm[l][None, :],
                      w_up[l].astype(BF16), w_down[l].astype(BF16), B, S)
    return xf.reshape(B, S, D)
```

```python
import functools
import math

import jax
import jax.numpy as jnp
from jax import lax
from jax.experimental import pallas as pl
from jax.experimental.pallas import tpu as pltpu

F32 = jnp.float32
BF16 = jnp.bfloat16
HI = lax.Precision.HIGHEST

D_MODEL = 1024
HEAD_DIM = 64
ATT_HEADS = 8
GDN_HEADS = 8
ATT_WIDTH = ATT_HEADS * HEAD_DIM
GDN_WIDTH = GDN_HEADS * HEAD_DIM
MOBA_BLOCK = 256
MOBA_TOPK = 3
GDN_CHUNK = 64
CONV_WIDTH = 4
D_FF = 4 * D_MODEL
REL_BUCKETS = 32
REL_MAX_EXACT = 16
REL_MAX_DIST = 128
EPS = 1e-6
NEG = -1e30
LOG2E = math.log2(math.e)

LANES = 128
SUBLANES = 8
VMEM_LIMIT = 56 * 1024 * 1024
ROW_TILE = 512
INPROJ_TILE = 512
FF_TILE = 1024

NT = (((1,), (1,)), ((), ()))
TN = (((0,), (0,)), ((), ()))


def _bucket_lower_bounds():
    def bucket(d):
        if d < REL_MAX_EXACT:
            return d
        t = math.log(d / REL_MAX_EXACT) / math.log(REL_MAX_DIST / REL_MAX_EXACT)
        t = t * (REL_BUCKETS - REL_MAX_EXACT)
        assert d in (REL_MAX_EXACT, REL_MAX_DIST) or abs(t - round(t)) > 1e-6
        return min(REL_MAX_EXACT + int(t + 1e-9), REL_BUCKETS - 1)
    lower = []
    for b in range(REL_BUCKETS):
        d = 0
        while bucket(d) < b:
            d += 1
        lower.append(d)
    return lower


BUCKET_LOWER = _bucket_lower_bounds()


def _sigmoid(x):
    return 0.5 * jnp.tanh(0.5 * x) + 0.5


def _silu_of_half(h):
    return h + h * jnp.tanh(h)


def _rms(x, w):
    return x * lax.rsqrt(jnp.mean(x * x, axis=-1, keepdims=True) + EPS) * w


def _split_bf16(x, parts):
    out = []
    for _ in range(parts):
        h = x.astype(BF16)
        out.append(h)
        x = x - h.astype(F32)
    return out


def _dot_split_rhs(c, x, parts):
    acc = None
    for h in _split_bf16(x, parts):
        d = jnp.dot(c, h, preferred_element_type=F32)
        acc = d if acc is None else acc + d
    return acc


CONV_COLS = 512


def _inproj_kernel(x_ref, xp_ref, nw_ref, wqT_ref, wk_ref, wvT_ref, wg_ref, wz_ref, wab_ref,
                   cw_ref, alog_ref, dtb_ref,
                   qT_ref, k_ref, ksum_ref, va_ref, g_ref, z_ref, gb_ref, *, tiles_per_seq):
    h = _rms(x_ref[...], nw_ref[...]).astype(BF16)

    hp = _rms(xp_ref[...], nw_ref[...]).astype(BF16)
    seq_start = (pl.program_id(0) % tiles_per_seq) == 0
    trow8 = lax.broadcasted_iota(jnp.int32, (SUBLANES, CONV_COLS), 0)
    for c in range(3 * GDN_WIDTH // CONV_COLS):
        cols = slice(c * CONV_COLS, (c + 1) * CONV_COLS)
        cur = jnp.dot(h, wg_ref[:, cols], preferred_element_type=F32)
        prev8 = jnp.dot(hp, wg_ref[:, cols], preferred_element_type=F32)
        prev8 = jnp.where(seq_start, 0.0, prev8)
        cw_half = 0.5 * cw_ref[:, cols]
        acc = cur * cw_half[CONV_WIDTH - 1:CONV_WIDTH]
        for s in range(1, CONV_WIDTH):
            rolled = pltpu.roll(cur, s, 0)
            top = jnp.where(trow8 < s, pltpu.roll(prev8, s, 0), rolled[0:SUBLANES])
            tap = jnp.concatenate([top, rolled[SUBLANES:]], axis=0)
            acc = acc + tap * cw_half[CONV_WIDTH - 1 - s:CONV_WIDTH - s]
        g_ref[:, cols] = _silu_of_half(acc)

    z = jnp.dot(h, wz_ref[...], preferred_element_type=F32)
    z_ref[...] = _silu_of_half(0.5 * z)
    ab = jnp.dot(h, wab_ref[...], preferred_element_type=F32)
    xs = ab + dtb_ref[...]
    log_decay = -jnp.exp(alog_ref[...]) * (jnp.maximum(xs, 0.0) + jnp.log1p(jnp.exp(-jnp.abs(xs))))
    lane = lax.broadcasted_iota(jnp.int32, ab.shape, 1)
    gb_ref[...] = jnp.where(lane < GDN_HEADS, log_decay, _sigmoid(ab))

    qT = lax.dot_general(wqT_ref[...], h, NT, preferred_element_type=F32)
    vT = lax.dot_general(wvT_ref[...], h, NT, preferred_element_type=F32)
    k = jnp.dot(h, wk_ref[...], preferred_element_type=F32)
    k_ref[...] = k.astype(BF16)
    ones_row = jnp.where(lax.broadcasted_iota(jnp.int32, (V_ROWS - HEAD_DIM, MOBA_BLOCK), 0) == 0,
                         1.0, 0.0).astype(BF16)
    ksum_ref[...] = jnp.zeros(ksum_ref.shape, F32)
    for t in range(INPROJ_TILE // MOBA_BLOCK):
        blk = slice(t * MOBA_BLOCK, (t + 1) * MOBA_BLOCK)
        qT_ref[0, t] = qT[:, blk]
        ksum_ref[0, t:t + 1, :] = jnp.sum(k[blk], axis=0, keepdims=True)
        for hh in range(ATT_HEADS):
            va_ref[0, t, hh, 0:HEAD_DIM, :] = vT[HEAD_DIM * hh:HEAD_DIM * (hh + 1), blk].astype(BF16)
            va_ref[0, t, hh, HEAD_DIM:V_ROWS, :] = ones_row


def _inproj(xf, nw, wqT, wk, wvT, wg, wz, wab, conv_w, alog_pad, dtb_pad, B, S):
    T = B * S
    TM = INPROJ_TILE
    assert S % TM == 0
    nblk = S // MOBA_BLOCK
    tiles_per_seq = S // TM
    blk_per_tile = TM // MOBA_BLOCK
    const = lambda i: (0, 0)
    row = lambda i: (i, 0)
    tr = lambda i: (i // tiles_per_seq, i % tiles_per_seq, 0, 0)
    prev_rows = lambda i: (jnp.maximum(i * (TM // SUBLANES) - 1, 0), 0)
    single = dict(pipeline_mode=pl.Buffered(1))
    return pl.pallas_call(
        functools.partial(_inproj_kernel, tiles_per_seq=tiles_per_seq),
        grid=(T // TM,),
        in_specs=[
            pl.BlockSpec((TM, D_MODEL), row),
            pl.BlockSpec((SUBLANES, D_MODEL), prev_rows),
            pl.BlockSpec((1, D_MODEL), const),
            pl.BlockSpec(wqT.shape, const, **single),
            pl.BlockSpec(wk.shape, const, **single),
            pl.BlockSpec(wvT.shape, const, **single),
            pl.BlockSpec(wg.shape, const, **single),
            pl.BlockSpec(wz.shape, const, **single),
            pl.BlockSpec(wab.shape, const, **single),
            pl.BlockSpec(conv_w.shape, const),
            pl.BlockSpec((1, LANES), const),
            pl.BlockSpec((1, LANES), const),
        ],
        out_specs=[
            pl.BlockSpec((1, blk_per_tile, ATT_WIDTH, MOBA_BLOCK), tr),
            pl.BlockSpec((TM, ATT_WIDTH), row),
            pl.BlockSpec((1, SUBLANES, ATT_WIDTH), lambda i: (i, 0, 0)),
            pl.BlockSpec((1, blk_per_tile, ATT_HEADS, V_ROWS, MOBA_BLOCK),
                         lambda i: (i // tiles_per_seq, i % tiles_per_seq, 0, 0, 0)),
            pl.BlockSpec((TM, 3 * GDN_WIDTH), row),
            pl.BlockSpec((TM, GDN_WIDTH), row),
            pl.BlockSpec((TM, LANES), row),
        ],
        out_shape=[
            jax.ShapeDtypeStruct((B, nblk, ATT_WIDTH, MOBA_BLOCK), F32),
            jax.ShapeDtypeStruct((T, ATT_WIDTH), BF16),
            jax.ShapeDtypeStruct((T // TM, SUBLANES, ATT_WIDTH), F32),
            jax.ShapeDtypeStruct((B, nblk, ATT_HEADS, V_ROWS, MOBA_BLOCK), BF16),
            jax.ShapeDtypeStruct((T, 3 * GDN_WIDTH), F32),
            jax.ShapeDtypeStruct((T, GDN_WIDTH), F32),
            jax.ShapeDtypeStruct((T, LANES), F32),
        ],
        compiler_params=pltpu.CompilerParams(
            dimension_semantics=("arbitrary",), vmem_limit_bytes=VMEM_LIMIT),
        name="inproj",
    )(xf, xf, nw, wqT, wk, wvT, wg, wz, wab, conv_w, alog_pad, dtb_pad)


V_ROWS = HEAD_DIM + 16


def _paired_pos(i, nblk):
    return jnp.where(i < nblk // 2, 2 * i, 2 * (nblk - 1 - i) + 1)


GDN_TILE = 256
GDN_HALF = 2 * LANES


def _gdn_stages(y_ref, gb_ref, sz_ref, nw_ref, out_ref,
                u_ref, wq_ref, a_ref, kd_ref, gl_ref, st_ref):
    C = GDN_CHUNK
    W = GDN_WIDTH
    TILE = GDN_TILE
    npair = W // LANES

    r_w = lax.broadcasted_iota(jnp.int32, (GDN_HALF, GDN_HALF), 0)
    c_w = lax.broadcasted_iota(jnp.int32, (GDN_HALF, GDN_HALF), 1)
    head_ones = jnp.where((r_w // HEAD_DIM) == (c_w // HEAD_DIM), 1.0, 0.0).astype(BF16)
    ltri_bd = jnp.where(((r_w // C) == (c_w // C)) & (c_w <= r_w), 1.0, 0.0).astype(BF16)
    tok = lax.broadcasted_iota(jnp.int32, (TILE, W), 0) % C
    col = lax.broadcasted_iota(jnp.int32, (TILE, W), 1) % HEAD_DIM
    causal_t = tok >= col
    strict_t = tok > col
    lane_t = lax.broadcasted_iota(jnp.int32, (TILE, LANES), 1)

    lane = lax.broadcasted_iota(jnp.int32, (C, LANES), 1)
    rowi = lax.broadcasted_iota(jnp.int32, (C, LANES), 0)
    first_head = lane < HEAD_DIM
    strict = rowi > (lane % HEAD_DIM)
    eye2 = jnp.where(rowi == (lane % HEAD_DIM), 1.0, 0.0)
    lane2 = lax.broadcasted_iota(jnp.int32, (C, 2 * LANES), 1)
    first_head2 = (lane2 % LANES) < HEAD_DIM
    r_l = lax.broadcasted_iota(jnp.int32, (LANES, LANES), 0)
    c_l = lax.broadcasted_iota(jnp.int32, (LANES, LANES), 1)
    same_head = (r_l // HEAD_DIM) == (c_l // HEAD_DIM)
    pair_ones = jnp.where(same_head, 1.0, 0.0).astype(BF16)

    def stack(x, mask):
        return jnp.concatenate([jnp.where(mask, x, 0.0), jnp.where(mask, 0.0, x)], axis=0)

    dot = functools.partial(jnp.dot, preferred_element_type=F32)

    def head_sumsq(ys):
        halves = [(y * y).astype(BF16)[:, h:h + GDN_HALF] for y in ys for h in range(0, W, GDN_HALF)]
        sums = dot(jnp.concatenate(halves, axis=0), head_ones)
        per = W // GDN_HALF
        return [jnp.concatenate([sums[(i * per + j) * TILE:(i * per + j + 1) * TILE] for j in range(per)],
                                axis=1) for i in range(len(ys))]

    def solve_stages(slot):
        yq = y_ref[:, 0:W]
        yk = y_ref[:, W:2 * W]
        yv = y_ref[:, 2 * W:3 * W]
        ssq, ssk = head_sumsq([yq, yk])
        qn = yq * lax.rsqrt(ssq + EPS) * (HEAD_DIM ** -0.5)
        kn = yk * lax.rsqrt(ssk + EPS)
        yield
        gbt = gb_ref[...]

        def spread(col0):
            pairs = []
            for p in range(npair):
                a = jnp.broadcast_to(gbt[:, col0 + 2 * p:col0 + 2 * p + 1], (TILE, LANES))
                b = jnp.broadcast_to(gbt[:, col0 + 2 * p + 1:col0 + 2 * p + 2], (TILE, LANES))
                pairs.append(jnp.where(lane_t < HEAD_DIM, a, b))
            return jnp.concatenate(pairs, axis=1)

        g = spread(0)
        beta = spread(GDN_HEADS)
        gcd = _dot_split_rhs(ltri_bd, jnp.concatenate([g, jnp.where(strict_t, g, 0.0)], axis=1), 2)
        yield
        gc = gcd[:, :W]
        decay = jnp.where(causal_t, jnp.exp(jnp.where(causal_t, gcd[:, W:], 0.0)), 0.0)
        egc = jnp.exp(gc)
        kb = kn * beta
        rv = yv * beta
        rk = kb * egc
        qd = qn * egc
        for cc in range(TILE // C):
            rs = slice(cc * C, (cc + 1) * C)
            g_last = gc[(cc + 1) * C - 1:(cc + 1) * C, :]
            kd_ref[slot, rs, :] = (kn[rs] * jnp.exp(g_last - gc[rs])).astype(BF16)
            gl_ref[slot, cc * SUBLANES:(cc + 1) * SUBLANES, :] = (
                jnp.broadcast_to(jnp.exp(g_last), (SUBLANES, W)))
        units = [(slice(cc * C, (cc + 1) * C), slice(LANES * p, LANES * (p + 1)), cc)
                 for cc in range(TILE // C) for p in range(npair)]
        kqs = [lax.dot_general(jnp.concatenate([kn[rs, ls], qn[rs, ls]], axis=0).astype(BF16),
                               stack(kn[rs, ls], first_head).astype(BF16), NT,
                               preferred_element_type=F32) for rs, ls, _ in units]
        yield
        ps = [-jnp.where(strict, kq[0:C] * beta[rs, ls] * decay[rs, ls], 0.0)
              for kq, (rs, ls, _) in zip(kqs, units)]
        ss = [eye2 + p for p in ps]
        ps = [dot(p.astype(BF16), stack(p, first_head).astype(BF16)) for p in ps]
        yield
        nround = int(math.log2(C))
        for k in range(1, nround):
            rhs = [stack(s_, first_head).astype(BF16) for s_ in ss]
            if k + 1 < nround:
                rhs = [jnp.concatenate([stack(p, first_head).astype(BF16), sx], axis=1)
                       for p, sx in zip(ps, rhs)]
            outs = [dot(p.astype(BF16), sx) for p, sx in zip(ps, rhs)]
            if k + 1 < nround:
                ps = [o[:, :LANES] for o in outs]
                ss = [s_ + o[:, LANES:] for s_, o in zip(ss, outs)]
            else:
                ss = [s_ + o for s_, o in zip(ss, outs)]
            yield
        xs = [dot(s_.astype(BF16),
                  stack(jnp.concatenate([rv[rs, ls], rk[rs, ls]], axis=1), first_head2).astype(BF16))
              for s_, (rs, ls, _) in zip(ss, units)]
        yield
        for x, kq, (rs, ls, cc) in zip(xs, kqs, units):
            u_ref[slot, rs, ls] = x[:, :LANES]
            wq_ref[slot, cc, 0:C, ls] = x[:, LANES:].astype(BF16)
            wq_ref[slot, cc, C:2 * C, ls] = qd[rs, ls].astype(BF16)
            a_ref[slot, rs, ls] = (kq[C:2 * C] * decay[rs, ls]).astype(BF16)

    lss = [slice(LANES * p, LANES * (p + 1)) for p in range(npair)]

    def recurrence_stages(slot, seq_start):
        states = [jnp.where(seq_start, 0.0, st_ref[p]) for p in range(npair)]
        pending = None

        def finish(rs, os_):
            sq = jnp.concatenate([(o * o).astype(BF16) for o in os_], axis=0)
            ms_all = dot(sq, pair_ones) * (1.0 / HEAD_DIM)
            for p, (ls, o) in enumerate(zip(lss, os_)):
                ms = ms_all[p * C:(p + 1) * C]
                out_ref[rs, ls] = o * lax.rsqrt(ms + EPS) * nw_ref[:, ls] * sz_ref[rs, ls]

        for cc in range(TILE // C):
            rs = slice(cc * C, (cc + 1) * C)
            wqs = [dot(wq_ref[slot, cc, :, ls], st.astype(BF16)) for ls, st in zip(lss, states)]
            if pending is not None:
                finish(*pending)
            yield
            v_news = [u_ref[slot, rs, ls] - wq[0:C] for ls, wq in zip(lss, wqs)]
            kvs = [lax.dot_general(kd_ref[slot, rs, ls], v.astype(BF16), TN, preferred_element_type=F32)
                   for ls, v in zip(lss, v_news)]
            os_ = [wq[C:2 * C] + dot(a_ref[slot, rs, ls], stack(v, first_head).astype(BF16))
                   for ls, wq, v in zip(lss, wqs, v_news)]
            states = [st * gl_ref[slot, cc * SUBLANES:cc * SUBLANES + 1, ls] + jnp.where(same_head, kv, 0.0)
                      for ls, st, kv in zip(lss, states, kvs)]
            pending = (rs, os_)
            yield
        finish(*pending)
        for p in range(npair):
            st_ref[p] = states[p]

    return solve_stages, recurrence_stages


def _interleave_weighted(*gens_and_weights):
    live = [[gen, weight] for gen, weight in gens_and_weights]
    while live:
        for entry in list(live):
            for _ in range(entry[1]):
                try:
                    next(entry[0])
                except StopIteration:
                    live.remove(entry)
                    break


def _mixers_kernel(relb_ref, qlo_ref, qhi_ref, kb_ref, ksum_ref, va_ref,
                   y_ref, gb_ref, sz_ref, nw_ref,
                   oT_ref, og_ref,
                   bias_ref, addm_ref, qh_ref, lg_ref, moff_ref,
                   u_ref, wq_ref, a_ref, kd_ref, gl_ref, st_ref,
                   *, nblk, nbatch, tiles_per_seq, nsteps):
    step = pl.program_id(0)
    BLK = MOBA_BLOCK
    half = nblk // 2
    pair = jnp.minimum(step, nsteps - 2)
    hp = pair // (nbatch * (half // 2))
    b = (pair // (half // 2)) % nbatch
    scoring = step < nsteps - 1
    new_kv = (pair % (half // 2)) == 0

    solve_stages, recurrence_stages = _gdn_stages(
        y_ref, gb_ref, sz_ref, nw_ref, og_ref,
        u_ref, wq_ref, a_ref, kd_ref, gl_ref, st_ref)
    seq_start = ((step - 1) % tiles_per_seq) == 0

    @pl.when((b == 0) & new_kv & scoring)
    def _():
        kk = lax.broadcasted_iota(jnp.int32, (BLK, BLK), 0)
        qq = lax.broadcasted_iota(jnp.int32, (BLK, BLK), 1)
        for hh in range(2):
            h = 2 * hp + hh
            for kind in range(2):
                d = qq - kk + kind * BLK
                val = jnp.full((BLK, BLK), relb_ref[h, REL_BUCKETS - 1], F32)
                for bkt in range(REL_BUCKETS - 2, -1, -1):
                    val = jnp.where(d < BUCKET_LOWER[bkt + 1], relb_ref[h, bkt], val)
                val = val * LOG2E
                if kind == 0:
                    val = jnp.where(d >= 0, val, NEG)
                bias_ref[hh, kind] = val

    def key_means():
        per_tile = INPROJ_TILE // BLK
        ks = ksum_ref[...]
        km = jnp.concatenate([ks[j // per_tile, j % per_tile:j % per_tile + 1, :] for j in range(nblk)],
                             axis=0) * (1.0 / BLK)
        lane = lax.broadcasted_iota(jnp.int32, (nblk, LANES), 1)
        return jnp.concatenate([jnp.where(lane < HEAD_DIM, km, 0.0),
                                jnp.where(lane >= HEAD_DIM, km, 0.0)], axis=0)

    def item_tiles(t):
        i_hi = nblk - 1 - t
        tiles = [(0, t, "own"), (1, i_hi, "own"), (1, i_hi - 1, "prev")]
        if t >= 1:
            tiles.append((0, t - 1, "prev"))
        tiles += [(0, j, "far") for j in range(t - 1)]
        tiles += [(1, j, "far") for j in range(i_hi - 1)]
        assert len(tiles) == nblk + 1
        return tiles

    def score_stages(parity):
        ridx = lax.broadcasted_iota(jnp.int32, (nblk, BLK), 0)
        sub = lax.broadcasted_iota(jnp.int32, (LANES, BLK), 0)
        scale = HEAD_DIM ** -0.5 * LOG2E
        km = key_means()
        for e in range(2):
            t = 2 * parity + e
            slot = 2 * parity + e
            q_of = ((qlo_ref, e, t), (qhi_ref, 1 - e, nblk - 1 - t))
            for s, (q_ref, w, qi) in enumerate(q_of):
                qT = q_ref[0, w]
                gT = jnp.dot(km, qT, precision=HI, preferred_element_type=F32)
                past = ridx < qi
                for hh in range(2):
                    gm = jnp.where(past, gT[nblk * hh:nblk * (hh + 1)], -jnp.inf)
                    cnt = jnp.zeros((nblk, BLK), F32)
                    for jp in range(nblk):
                        row = gm[jp:jp + 1, :]
                        beats = (row > gm) | ((row == gm) & (ridx > jp))
                        cnt = cnt + jnp.where(beats, 1.0, 0.0)
                    visible = past & (cnt < MOBA_TOPK)
                    addm_ref[e, s, nblk * hh:nblk * (hh + 1), :] = jnp.where(visible, 0.0, NEG)
                    in_head = (sub >= HEAD_DIM * hh) & (sub < HEAD_DIM * (hh + 1))
                    qh_ref[e, s, hh] = jnp.where(in_head, qT * scale, 0.0).astype(BF16)
            yield
            tiles = item_tiles(t)
            for hh in range(2):
                cmax = {0: [], 1: []}
                offs = []
                for n, (s, kblk, cls) in enumerate(tiles):
                    lg = jnp.dot(kb_ref[0, kblk * BLK:(kblk + 1) * BLK, :], qh_ref[e, s, hh],
                                 preferred_element_type=F32)
                    if cls != "far":
                        lg = lg + bias_ref[hh, 0 if cls == "own" else 1]
                    lg_ref[slot, hh, n] = lg
                    cm = jnp.max(lg, axis=0, keepdims=True)
                    off = None
                    if cls != "own":
                        off = addm_ref[e, s, nblk * hh + kblk:nblk * hh + kblk + 1, :]
                        if cls == "far":
                            off = off + relb_ref[2 * hp + hh, REL_BUCKETS - 1] * LOG2E
                        cm = cm + off
                    cmax[s].append(cm)
                    offs.append(off)
                    yield
                m = {s: functools.reduce(jnp.maximum, cmax[s]) for s in (0, 1)}
                for n, (s, _, _) in enumerate(tiles):
                    moff_ref[slot, hh, n:n + 1, :] = m[s] if offs[n] is None else m[s] - offs[n]

    def softmax_pv_stages(parity):
        for e in range(2):
            t = 2 * parity + e
            slot = 2 * parity + e
            tiles = item_tiles(t)
            for hh in range(2):
                acc = {0: None, 1: None}
                for n, (s, kblk, _) in enumerate(tiles):
                    p = jnp.exp2(lg_ref[slot, hh, n] - moff_ref[slot, hh, n:n + 1, :])
                    pvn = jnp.dot(va_ref[0, kblk, hh], p.astype(BF16),
                                  preferred_element_type=F32)
                    acc[s] = pvn if acc[s] is None else acc[s] + pvn
                    yield
                for s in (0, 1):
                    oT_ref[0, 2 * e + s, HEAD_DIM * hh:HEAD_DIM * (hh + 1), :] = (
                        acc[s][0:HEAD_DIM] / acc[s][HEAD_DIM:HEAD_DIM + 1])

    ATT, GDN = 1, 1

    @pl.when(step == 0)
    def _():
        st_ref[...] = jnp.zeros(st_ref.shape, F32)
        _interleave_weighted((score_stages(0), ATT), (solve_stages(0), GDN))

    for parity in range(2):
        @pl.when((step > 0) & (step < nsteps - 1) & (step % 2 == parity))
        def _(parity=parity):
            _interleave_weighted((recurrence_stages(1 - parity, seq_start), GDN),
                                 (score_stages(parity), ATT),
                                 (solve_stages(parity), GDN),
                                 (softmax_pv_stages(1 - parity), ATT))

    @pl.when(step == nsteps - 1)
    def _():
        last = (nsteps - 2) % 2
        _interleave_weighted((recurrence_stages(last, seq_start), GDN), (softmax_pv_stages(last), ATT))


def _token_mixers(rel_bias, qT, k3, ksum, va, gqkv, sz, gb, nw_row, B, S):
    T = B * S
    W = GDN_WIDTH
    TILE = GDN_TILE
    nblk = S // MOBA_BLOCK
    half = nblk // 2
    assert TILE == GDN_HALF and S % TILE == 0
    assert BUCKET_LOWER[REL_BUCKETS - 1] <= MOBA_BLOCK + 1
    assert nblk == 8 and nblk + 1 <= 2 * SUBLANES
    nchunk = TILE // GDN_CHUNK
    ntiles = T // TILE
    npairs = (ATT_HEADS // 2) * B * (half // 2)
    assert npairs == ntiles
    nsteps = ntiles + 1

    def scored(s):
        p = jnp.minimum(s, npairs - 1)
        return p // (B * (half // 2)), (p // (half // 2)) % B, p % (half // 2)

    def done(s):
        p = jnp.maximum(s - 1, 0)
        return p // (B * (half // 2)), (p // (half // 2)) % B, p % (half // 2)

    def q_lo(s):
        hp, b, m = scored(s)
        return (b, m, hp, 0)

    def q_hi(s):
        hp, b, m = scored(s)
        return (b, half - 1 - m, hp, 0)

    def k_blk(s):
        hp, b, _ = scored(s)
        return (b, 0, hp)

    def v_blk(s):
        hp, b, _ = done(s)
        return (b, 0, hp, 0, 0)

    def o_blk(s):
        hp, b, m = done(s)
        return (b, m, hp, 0)

    cur_tile = lambda off: (lambda s: (jnp.minimum(s, ntiles - 1), off))
    prev_tile = lambda s: (jnp.maximum(s - 1, 0), 0)
    return pl.pallas_call(
        functools.partial(_mixers_kernel, nblk=nblk, nbatch=B, tiles_per_seq=S // TILE, nsteps=nsteps),
        grid=(nsteps,),
        in_specs=[
            pl.BlockSpec(memory_space=pltpu.SMEM),
            pl.BlockSpec((1, 2, LANES, MOBA_BLOCK), q_lo),
            pl.BlockSpec((1, 2, LANES, MOBA_BLOCK), q_hi),
            pl.BlockSpec((1, S, LANES), k_blk),
            pl.BlockSpec((S // INPROJ_TILE, SUBLANES, LANES), k_blk),
            pl.BlockSpec((1, nblk, 2, V_ROWS, MOBA_BLOCK), v_blk),
            pl.BlockSpec((TILE, 3 * W), cur_tile(0)),
            pl.BlockSpec((TILE, LANES), cur_tile(0)),
            pl.BlockSpec((TILE, W), prev_tile),
            pl.BlockSpec((1, W), lambda s: (0, 0)),
        ],
        out_specs=[
            pl.BlockSpec((1, 4, LANES, MOBA_BLOCK), o_blk),
            pl.BlockSpec((TILE, W), prev_tile),
        ],
        out_shape=[
            jax.ShapeDtypeStruct((B, nblk, ATT_WIDTH, MOBA_BLOCK), F32),
            jax.ShapeDtypeStruct((T, GDN_WIDTH), F32),
        ],
        scratch_shapes=[
            pltpu.VMEM((2, 2, MOBA_BLOCK, MOBA_BLOCK), F32),
            pltpu.VMEM((2, 2, 2 * nblk, MOBA_BLOCK), F32),
            pltpu.VMEM((2, 2, 2, LANES, MOBA_BLOCK), BF16),
            pltpu.VMEM((4, 2, nblk + 1, MOBA_BLOCK, MOBA_BLOCK), F32),
            pltpu.VMEM((4, 2, 2 * SUBLANES, MOBA_BLOCK), F32),
            pltpu.VMEM((2, TILE, W), F32),
            pltpu.VMEM((2, nchunk, 2 * GDN_CHUNK, W), BF16),
            pltpu.VMEM((2, TILE, W), BF16),
            pltpu.VMEM((2, TILE, W), BF16),
            pltpu.VMEM((2, nchunk * SUBLANES, W), F32),
            pltpu.VMEM((W // LANES, LANES, LANES), F32),
        ],
        compiler_params=pltpu.CompilerParams(
            dimension_semantics=("arbitrary",), vmem_limit_bytes=VMEM_LIMIT),
        name="token_mixers",
    )(rel_bias, qT, qT, k3, ksum, va, gqkv, gb, sz, nw_row)


def _out_mlp_kernel(x_ref, oTa_ref, oTb_ref, og_ref, woa_ref, wog_ref, pmn_ref, pre_ref, post_ref,
                    wup_ref, wdn_ref, out_ref):
    oT = jnp.concatenate([oTa_ref[0, 0], oTb_ref[0, 0]], axis=1)
    o_att = oT.T.astype(BF16)
    mix = jnp.dot(o_att, woa_ref[...], preferred_element_type=F32)
    mix = mix + jnp.dot(og_ref[...].astype(BF16), wog_ref[...], preferred_element_type=F32)
    x1 = x_ref[...] + _rms(mix, pmn_ref[...])
    h = _rms(x1, pre_ref[...]).astype(BF16)
    acc = jnp.zeros((ROW_TILE, D_MODEL), F32)
    for c in range(D_FF // FF_TILE):
        up = jnp.dot(h, wup_ref[:, c * FF_TILE:(c + 1) * FF_TILE], preferred_element_type=F32)
        act = jnp.square(jnp.maximum(up, 0.0)).astype(BF16)
        acc = acc + jnp.dot(act, wdn_ref[c * FF_TILE:(c + 1) * FF_TILE, :], preferred_element_type=F32)
    out_ref[...] = x1 + _rms(acc, post_ref[...])


def _out_mlp(xf, oT, og, woa, wog, pmn, pre, post, wup, wdn, B, S):
    T = B * S
    nblk = S // MOBA_BLOCK
    tiles_per_seq = S // ROW_TILE
    assert ROW_TILE == 2 * MOBA_BLOCK
    const = lambda i: (0, 0)
    row = lambda i: (i, 0)

    def att_block(which):
        def index(i):
            blk = 2 * (i % tiles_per_seq) + which
            return (i // tiles_per_seq, _paired_pos(blk, nblk), 0, 0)
        return index

    single = dict(pipeline_mode=pl.Buffered(1))
    return pl.pallas_call(
        _out_mlp_kernel,
        grid=(T // ROW_TILE,),
        in_specs=[
            pl.BlockSpec((ROW_TILE, D_MODEL), row),
            pl.BlockSpec((1, 1, ATT_WIDTH, MOBA_BLOCK), att_block(0)),
            pl.BlockSpec((1, 1, ATT_WIDTH, MOBA_BLOCK), att_block(1)),
            pl.BlockSpec((ROW_TILE, GDN_WIDTH), row),
            pl.BlockSpec(woa.shape, const, **single),
            pl.BlockSpec(wog.shape, const, **single),
            pl.BlockSpec((1, D_MODEL), const),
            pl.BlockSpec((1, D_MODEL), const),
            pl.BlockSpec((1, D_MODEL), const),
            pl.BlockSpec(wup.shape, const, **single),
            pl.BlockSpec(wdn.shape, const, **single),
        ],
        out_specs=pl.BlockSpec((ROW_TILE, D_MODEL), row),
        out_shape=jax.ShapeDtypeStruct((T, D_MODEL), F32),
        compiler_params=pltpu.CompilerParams(
            dimension_semantics=("arbitrary",), vmem_limit_bytes=VMEM_LIMIT),
        name="out_mlp",
    )(xf, oT, oT, og, woa, wog, pmn, pre, post, wup, wdn)


def kernel(x, w_in, w_out, conv_w, A_log, dt_bias, gdn_norm_w, rel_bias, pre_mix_norm,
           post_mix_norm, pre_mlp_norm, post_mlp_norm, w_up, w_down):
    B, S, D = x.shape
    assert D == D_MODEL and S % ROW_TILE == 0 and S % MOBA_BLOCK == 0
    T = B * S
    depth = w_in.shape[0]
    xf = x.reshape(T, D)
    o0, o1, o2, o3, o4 = 0, ATT_WIDTH, 2 * ATT_WIDTH, 3 * ATT_WIDTH, 3 * ATT_WIDTH + 3 * GDN_WIDTH
    o5 = o4 + GDN_WIDTH
    for l in range(depth):
        wi = w_in[l]
        wqT = wi[:, o0:o1].T.astype(BF16)
        wk = wi[:, o1:o2].astype(BF16)
        wvT = wi[:, o2:o3].T.astype(BF16)
        wg = wi[:, o3:o4].astype(BF16)
        wz = wi[:, o4:o5].astype(BF16)
        wab = jnp.pad(wi[:, o5:], ((0, 0), (0, LANES - 2 * GDN_HEADS))).astype(BF16)
        pad8 = lambda v: jnp.pad(v.astype(F32), (0, LANES - GDN_HEADS))[None, :]
        qT, k, ksum, va, gqkv, sz, gb = _inproj(xf, pre_mix_norm[l][None, :], wqT, wk, wvT, wg, wz, wab,
                                                conv_w[l], pad8(A_log[l]), pad8(dt_bias[l]), B, S)
        oT, og = _token_mixers(rel_bias.astype(F32), qT, k.reshape(B, S, ATT_WIDTH), ksum, va, gqkv, sz,
                               gb, jnp.tile(gdn_norm_w[l], GDN_HEADS)[None, :], B, S)
        wo = w_out[l].astype(BF16)
        xf = _out_mlp(xf, oT, og, wo[:ATT_WIDTH], wo[ATT_WIDTH:], post_mix_norm[l][None, :],
                      pre_mlp_norm[l][None, :], post_mlp_norm[l][None, :],
                      w_up[l].astype(BF16), w_down[l].astype(BF16), B, S)
    return xf.reshape(B, S, D)
```

```python
import functools
import math

import jax
import jax.numpy as jnp
from jax import lax
from jax.experimental import pallas as pl
from jax.experimental.pallas import tpu as pltpu

F32 = jnp.float32
BF16 = jnp.bfloat16
HI = lax.Precision.HIGHEST

D_MODEL = 1024
HEAD_DIM = 64
ATT_HEADS = 8
GDN_HEADS = 8
ATT_WIDTH = ATT_HEADS * HEAD_DIM
GDN_WIDTH = GDN_HEADS * HEAD_DIM
MOBA_BLOCK = 256
MOBA_TOPK = 3
GDN_CHUNK = 64
CONV_WIDTH = 4
D_FF = 4 * D_MODEL
REL_BUCKETS = 32
REL_MAX_EXACT = 16
REL_MAX_DIST = 128
EPS = 1e-6
NEG = -1e30
LOG2E = math.log2(math.e)

LANES = 128
SUBLANES = 8
VMEM_LIMIT = 56 * 1024 * 1024
ROW_TILE = 512
INPROJ_TILE = 512
FF_TILE = 1024

NT = (((1,), (1,)), ((), ()))
TN = (((0,), (0,)), ((), ()))


def _bucket_lower_bounds():
    def bucket(d):
        if d < REL_MAX_EXACT:
            return d
        t = math.log(d / REL_MAX_EXACT) / math.log(REL_MAX_DIST / REL_MAX_EXACT)
        t = t * (REL_BUCKETS - REL_MAX_EXACT)
        assert d in (REL_MAX_EXACT, REL_MAX_DIST) or abs(t - round(t)) > 1e-6
        return min(REL_MAX_EXACT + int(t + 1e-9), REL_BUCKETS - 1)
    lower = []
    for b in range(REL_BUCKETS):
        d = 0
        while bucket(d) < b:
            d += 1
        lower.append(d)
    return lower


BUCKET_LOWER = _bucket_lower_bounds()


def _sigmoid(x):
    return 0.5 * jnp.tanh(0.5 * x) + 0.5


def _silu_of_half(h):
    return h + h * jnp.tanh(h)


def _rms(x, w):
    return x * lax.rsqrt(jnp.mean(x * x, axis=-1, keepdims=True) + EPS) * w


def _split_bf16(x, parts):
    out = []
    for _ in range(parts):
        h = x.astype(BF16)
        out.append(h)
        x = x - h.astype(F32)
    return out


def _dot_split_rhs(c, x, parts):
    acc = None
    for h in _split_bf16(x, parts):
        d = jnp.dot(c, h, preferred_element_type=F32)
        acc = d if acc is None else acc + d
    return acc


CONV_COLS = 512
ROW_PITCH = 72


def _inproj_kernel(x_ref, xp_ref, nw_ref, wqT_ref, wk_ref, wvT_ref, wg_ref, wz_ref, wab_ref,
                   cw_ref, alog_ref, dtb_ref,
                   qT_ref, k_ref, ksum_ref, va_ref, g_ref, z_ref, gb_ref, hn_ref, y_ref, *, tiles_per_seq):
    TM = INPROJ_TILE
    NV = TM // SUBLANES
    hn = _rms(x_ref[...], nw_ref[...])
    h = hn.astype(BF16)

    for j in range(D_MODEL // LANES):
        for b in range(SUBLANES):
            hn_ref[j, ROW_PITCH * b:ROW_PITCH * b + NV, :] = hn[NV * b:NV * (b + 1), LANES * j:LANES * (j + 1)]
    h_perm = jnp.concatenate(
        [jnp.concatenate([hn_ref[j, pl.ds(u, SUBLANES, stride=ROW_PITCH), :] for j in range(D_MODEL // LANES)],
                         axis=1)
         for u in range(NV)], axis=0).astype(BF16)
    hp = _rms(xp_ref[...], nw_ref[...]).astype(BF16)
    seq_start = (pl.program_id(0) % tiles_per_seq) == 0
    first_sublane = lax.broadcasted_iota(jnp.int32, (SUBLANES, CONV_COLS), 0) == 0
    for c in range(3 * GDN_WIDTH // CONV_COLS):
        cols = slice(c * CONV_COLS, (c + 1) * CONV_COLS)
        cur = jnp.dot(h_perm, wg_ref[:, cols], preferred_element_type=F32)
        prev8 = jnp.dot(hp, wg_ref[:, cols], preferred_element_type=F32)
        prev8 = jnp.where(seq_start, 0.0, prev8)
        wrap = []
        for i in range(CONV_WIDTH - 1):
            u = NV - (CONV_WIDTH - 1) + i
            ctx = prev8[SUBLANES - (CONV_WIDTH - 1) + i:SUBLANES - (CONV_WIDTH - 1) + i + 1]
            wrap.append(jnp.where(first_sublane, ctx, pltpu.roll(cur[SUBLANES * u:SUBLANES * (u + 1)], 1, 0)))
        cw_half = 0.5 * cw_ref[:, cols]
        acc = cur * cw_half[CONV_WIDTH - 1:CONV_WIDTH]
        for s in range(1, CONV_WIDTH):
            tap = jnp.concatenate(wrap[CONV_WIDTH - 1 - s:] + [cur[0:TM - SUBLANES * s]], axis=0)
            acc = acc + tap * cw_half[CONV_WIDTH - 1 - s:CONV_WIDTH - s]
        y = _silu_of_half(acc)
        for j in range(CONV_COLS // LANES):
            for u in range(NV):
                y_ref[c, j, pl.ds(u, SUBLANES, stride=ROW_PITCH), :] = y[SUBLANES * u:SUBLANES * (u + 1),
                                                                         LANES * j:LANES * (j + 1)]
        g_ref[:, cols] = jnp.concatenate(
            [jnp.concatenate([y_ref[c, j, ROW_PITCH * b:ROW_PITCH * b + NV, :] for b in range(SUBLANES)], axis=0)
             for j in range(CONV_COLS // LANES)], axis=1)

    z = jnp.dot(h, wz_ref[...], preferred_element_type=F32)
    z_ref[...] = _silu_of_half(0.5 * z)
    ab = jnp.dot(h, wab_ref[...], preferred_element_type=F32)
    xs = ab + dtb_ref[...]
    log_decay = -jnp.exp(alog_ref[...]) * (jnp.maximum(xs, 0.0) + jnp.log1p(jnp.exp(-jnp.abs(xs))))
    lane = lax.broadcasted_iota(jnp.int32, ab.shape, 1)
    gb_ref[...] = jnp.where(lane < GDN_HEADS, log_decay, _sigmoid(ab))

    qT = lax.dot_general(wqT_ref[...], h, NT, preferred_element_type=F32)
    vT = lax.dot_general(wvT_ref[...], h, NT, preferred_element_type=F32)
    k = jnp.dot(h, wk_ref[...], preferred_element_type=F32)
    k_ref[...] = k.astype(BF16)
    ones_row = jnp.where(lax.broadcasted_iota(jnp.int32, (V_ROWS - HEAD_DIM, MOBA_BLOCK), 0) == 0,
                         1.0, 0.0).astype(BF16)
    ksum_ref[...] = jnp.zeros(ksum_ref.shape, F32)
    for t in range(INPROJ_TILE // MOBA_BLOCK):
        blk = slice(t * MOBA_BLOCK, (t + 1) * MOBA_BLOCK)
        qT_ref[0, t] = qT[:, blk]
        ksum_ref[0, t:t + 1, :] = jnp.sum(k[blk], axis=0, keepdims=True)
        for hh in range(ATT_HEADS):
            va_ref[0, t, hh, 0:HEAD_DIM, :] = vT[HEAD_DIM * hh:HEAD_DIM * (hh + 1), blk].astype(BF16)
            va_ref[0, t, hh, HEAD_DIM:V_ROWS, :] = ones_row


def _inproj(xf, nw, wqT, wk, wvT, wg, wz, wab, conv_w, alog_pad, dtb_pad, B, S):
    T = B * S
    TM = INPROJ_TILE
    assert S % TM == 0
    nblk = S // MOBA_BLOCK
    tiles_per_seq = S // TM
    blk_per_tile = TM // MOBA_BLOCK
    const = lambda i: (0, 0)
    row = lambda i: (i, 0)
    tr = lambda i: (i // tiles_per_seq, i % tiles_per_seq, 0, 0)
    prev_rows = lambda i: (jnp.maximum(i * (TM // SUBLANES) - 1, 0), 0)
    single = dict(pipeline_mode=pl.Buffered(1))
    return pl.pallas_call(
        functools.partial(_inproj_kernel, tiles_per_seq=tiles_per_seq),
        grid=(T // TM,),
        in_specs=[
            pl.BlockSpec((TM, D_MODEL), row),
            pl.BlockSpec((SUBLANES, D_MODEL), prev_rows),
            pl.BlockSpec((1, D_MODEL), const),
            pl.BlockSpec(wqT.shape, const, **single),
            pl.BlockSpec(wk.shape, const, **single),
            pl.BlockSpec(wvT.shape, const, **single),
            pl.BlockSpec(wg.shape, const, **single),
            pl.BlockSpec(wz.shape, const, **single),
            pl.BlockSpec(wab.shape, const, **single),
            pl.BlockSpec(conv_w.shape, const),
            pl.BlockSpec((1, LANES), const),
            pl.BlockSpec((1, LANES), const),
        ],
        out_specs=[
            pl.BlockSpec((1, blk_per_tile, ATT_WIDTH, MOBA_BLOCK), tr),
            pl.BlockSpec((TM, ATT_WIDTH), row),
            pl.BlockSpec((1, SUBLANES, ATT_WIDTH), lambda i: (i, 0, 0)),
            pl.BlockSpec((1, blk_per_tile, ATT_HEADS, V_ROWS, MOBA_BLOCK),
                         lambda i: (i // tiles_per_seq, i % tiles_per_seq, 0, 0, 0)),
            pl.BlockSpec((TM, 3 * GDN_WIDTH), row),
            pl.BlockSpec((TM, GDN_WIDTH), row),
            pl.BlockSpec((TM, LANES), row),
        ],
        out_shape=[
            jax.ShapeDtypeStruct((B, nblk, ATT_WIDTH, MOBA_BLOCK), F32),
            jax.ShapeDtypeStruct((T, ATT_WIDTH), BF16),
            jax.ShapeDtypeStruct((T // TM, SUBLANES, ATT_WIDTH), F32),
            jax.ShapeDtypeStruct((B, nblk, ATT_HEADS, V_ROWS, MOBA_BLOCK), BF16),
            jax.ShapeDtypeStruct((T, 3 * GDN_WIDTH), F32),
            jax.ShapeDtypeStruct((T, GDN_WIDTH), F32),
            jax.ShapeDtypeStruct((T, LANES), F32),
        ],
        scratch_shapes=[
            pltpu.VMEM((D_MODEL // LANES, SUBLANES * ROW_PITCH, LANES), F32),
            pltpu.VMEM((3 * GDN_WIDTH // CONV_COLS, CONV_COLS // LANES, SUBLANES * ROW_PITCH, LANES), F32),
        ],
        compiler_params=pltpu.CompilerParams(
            dimension_semantics=("arbitrary",), vmem_limit_bytes=VMEM_LIMIT),
        name="inproj",
    )(xf, xf, nw, wqT, wk, wvT, wg, wz, wab, conv_w, alog_pad, dtb_pad)


V_ROWS = HEAD_DIM + 16


def _paired_pos(i, nblk):
    return jnp.where(i < nblk // 2, 2 * i, 2 * (nblk - 1 - i) + 1)


GDN_TILE = 256
GDN_HALF = 2 * LANES


def _gdn_stages(y_ref, gb_ref, sz_ref, nw_ref, out_ref,
                u_ref, wq_ref, a_ref, kd_ref, gl_ref, st_ref):
    C = GDN_CHUNK
    W = GDN_WIDTH
    TILE = GDN_TILE
    npair = W // LANES

    r_w = lax.broadcasted_iota(jnp.int32, (GDN_HALF, GDN_HALF), 0)
    c_w = lax.broadcasted_iota(jnp.int32, (GDN_HALF, GDN_HALF), 1)
    head_ones = jnp.where((r_w // HEAD_DIM) == (c_w // HEAD_DIM), 1.0, 0.0).astype(BF16)
    ltri_bd = jnp.where(((r_w // C) == (c_w // C)) & (c_w <= r_w), 1.0, 0.0).astype(BF16)
    tok = lax.broadcasted_iota(jnp.int32, (TILE, W), 0) % C
    col = lax.broadcasted_iota(jnp.int32, (TILE, W), 1) % HEAD_DIM
    causal_t = tok >= col
    strict_t = tok > col
    lane_t = lax.broadcasted_iota(jnp.int32, (TILE, LANES), 1)

    lane = lax.broadcasted_iota(jnp.int32, (C, LANES), 1)
    rowi = lax.broadcasted_iota(jnp.int32, (C, LANES), 0)
    first_head = lane < HEAD_DIM
    strict = rowi > (lane % HEAD_DIM)
    eye2 = jnp.where(rowi == (lane % HEAD_DIM), 1.0, 0.0)
    lane2 = lax.broadcasted_iota(jnp.int32, (C, 2 * LANES), 1)
    first_head2 = (lane2 % LANES) < HEAD_DIM
    r_l = lax.broadcasted_iota(jnp.int32, (LANES, LANES), 0)
    c_l = lax.broadcasted_iota(jnp.int32, (LANES, LANES), 1)
    same_head = (r_l // HEAD_DIM) == (c_l // HEAD_DIM)
    pair_ones = jnp.where(same_head, 1.0, 0.0).astype(BF16)

    def stack(x, mask):
        return jnp.concatenate([jnp.where(mask, x, 0.0), jnp.where(mask, 0.0, x)], axis=0)

    dot = functools.partial(jnp.dot, preferred_element_type=F32)

    def head_sumsq(ys):
        halves = [(y * y).astype(BF16)[:, h:h + GDN_HALF] for y in ys for h in range(0, W, GDN_HALF)]
        sums = dot(jnp.concatenate(halves, axis=0), head_ones)
        per = W // GDN_HALF
        return [jnp.concatenate([sums[(i * per + j) * TILE:(i * per + j + 1) * TILE] for j in range(per)],
                                axis=1) for i in range(len(ys))]

    def solve_stages(slot):
        yq = y_ref[:, 0:W]
        yk = y_ref[:, W:2 * W]
        yv = y_ref[:, 2 * W:3 * W]
        ssq, ssk = head_sumsq([yq, yk])
        qn = yq * lax.rsqrt(ssq + EPS) * (HEAD_DIM ** -0.5)
        kn = yk * lax.rsqrt(ssk + EPS)
        yield
        gbt = gb_ref[...]

        def spread(col0):
            pairs = []
            for p in range(npair):
                a = jnp.broadcast_to(gbt[:, col0 + 2 * p:col0 + 2 * p + 1], (TILE, LANES))
                b = jnp.broadcast_to(gbt[:, col0 + 2 * p + 1:col0 + 2 * p + 2], (TILE, LANES))
                pairs.append(jnp.where(lane_t < HEAD_DIM, a, b))
            return jnp.concatenate(pairs, axis=1)

        g = spread(0)
        beta = spread(GDN_HEADS)
        gcd = _dot_split_rhs(ltri_bd, jnp.concatenate([g, jnp.where(strict_t, g, 0.0)], axis=1), 2)
        yield
        gc = gcd[:, :W]
        decay = jnp.where(causal_t, jnp.exp(jnp.where(causal_t, gcd[:, W:], 0.0)), 0.0)
        egc = jnp.exp(gc)
        kb = kn * beta
        rv = yv * beta
        rk = kb * egc
        qd = qn * egc
        for cc in range(TILE // C):
            rs = slice(cc * C, (cc + 1) * C)
            g_last = gc[(cc + 1) * C - 1:(cc + 1) * C, :]
            kd_ref[slot, rs, :] = (kn[rs] * jnp.exp(g_last - gc[rs])).astype(BF16)
            gl_ref[slot, cc * SUBLANES:(cc + 1) * SUBLANES, :] = (
                jnp.broadcast_to(jnp.exp(g_last), (SUBLANES, W)))
        units = [(slice(cc * C, (cc + 1) * C), slice(LANES * p, LANES * (p + 1)), cc)
                 for cc in range(TILE // C) for p in range(npair)]
        kqs = [lax.dot_general(jnp.concatenate([kn[rs, ls], qn[rs, ls]], axis=0).astype(BF16),
                               stack(kn[rs, ls], first_head).astype(BF16), NT,
                               preferred_element_type=F32) for rs, ls, _ in units]
        yield
        ps = [-jnp.where(strict, kq[0:C] * beta[rs, ls] * decay[rs, ls], 0.0)
              for kq, (rs, ls, _) in zip(kqs, units)]
        ss = [eye2 + p for p in ps]
        ps = [dot(p.astype(BF16), stack(p, first_head).astype(BF16)) for p in ps]
        yield
        nround = int(math.log2(C))
        for k in range(1, nround):
            rhs = [stack(s_, first_head).astype(BF16) for s_ in ss]
            if k + 1 < nround:
                rhs = [jnp.concatenate([stack(p, first_head).astype(BF16), sx], axis=1)
                       for p, sx in zip(ps, rhs)]
            outs = [dot(p.astype(BF16), sx) for p, sx in zip(ps, rhs)]
            if k + 1 < nround:
                ps = [o[:, :LANES] for o in outs]
                ss = [s_ + o[:, LANES:] for s_, o in zip(ss, outs)]
            else:
                ss = [s_ + o for s_, o in zip(ss, outs)]
            yield
        xs = [dot(s_.astype(BF16),
                  stack(jnp.concatenate([rv[rs, ls], rk[rs, ls]], axis=1), first_head2).astype(BF16))
              for s_, (rs, ls, _) in zip(ss, units)]
        yield
        for x, kq, (rs, ls, cc) in zip(xs, kqs, units):
            u_ref[slot, rs, ls] = x[:, :LANES]
            wq_ref[slot, cc, 0:C, ls] = x[:, LANES:].astype(BF16)
            wq_ref[slot, cc, C:2 * C, ls] = qd[rs, ls].astype(BF16)
            a_ref[slot, rs, ls] = (kq[C:2 * C] * decay[rs, ls]).astype(BF16)

    lss = [slice(LANES * p, LANES * (p + 1)) for p in range(npair)]

    def recurrence_stages(slot, seq_start):
        states = [jnp.where(seq_start, 0.0, st_ref[p]) for p in range(npair)]
        pending = None

        def finish(rs, os_):
            sq = jnp.concatenate([(o * o).astype(BF16) for o in os_], axis=0)
            ms_all = dot(sq, pair_ones) * (1.0 / HEAD_DIM)
            for p, (ls, o) in enumerate(zip(lss, os_)):
                ms = ms_all[p * C:(p + 1) * C]
                out_ref[rs, ls] = o * lax.rsqrt(ms + EPS) * nw_ref[:, ls] * sz_ref[rs, ls]

        for cc in range(TILE // C):
            rs = slice(cc * C, (cc + 1) * C)
            wqs = [dot(wq_ref[slot, cc, :, ls], st.astype(BF16)) for ls, st in zip(lss, states)]
            if pending is not None:
                finish(*pending)
            yield
            v_news = [u_ref[slot, rs, ls] - wq[0:C] for ls, wq in zip(lss, wqs)]
            kvs = [lax.dot_general(kd_ref[slot, rs, ls], v.astype(BF16), TN, preferred_element_type=F32)
                   for ls, v in zip(lss, v_news)]
            os_ = [wq[C:2 * C] + dot(a_ref[slot, rs, ls], stack(v, first_head).astype(BF16))
                   for ls, wq, v in zip(lss, wqs, v_news)]
            states = [st * gl_ref[slot, cc * SUBLANES:cc * SUBLANES + 1, ls] + jnp.where(same_head, kv, 0.0)
                      for ls, st, kv in zip(lss, states, kvs)]
            pending = (rs, os_)
            yield
        finish(*pending)
        for p in range(npair):
            st_ref[p] = states[p]

    return solve_stages, recurrence_stages


def _interleave_weighted(*gens_and_weights):
    live = [[gen, weight] for gen, weight in gens_and_weights]
    while live:
        for entry in list(live):
            for _ in range(entry[1]):
                try:
                    next(entry[0])
                except StopIteration:
                    live.remove(entry)
                    break


def _mixers_kernel(relb_ref, qlo_ref, qhi_ref, kb_ref, ksum_ref, va_ref,
                   y_ref, gb_ref, sz_ref, nw_ref,
                   oT_ref, og_ref,
                   bias_ref, addm_ref, qh_ref, lg_ref, moff_ref,
                   u_ref, wq_ref, a_ref, kd_ref, gl_ref, st_ref,
                   *, nblk, nbatch, tiles_per_seq, nsteps):
    step = pl.program_id(0)
    BLK = MOBA_BLOCK
    half = nblk // 2
    pair = jnp.minimum(step, nsteps - 2)
    hp = pair // (nbatch * (half // 2))
    b = (pair // (half // 2)) % nbatch
    scoring = step < nsteps - 1
    new_kv = (pair % (half // 2)) == 0

    solve_stages, recurrence_stages = _gdn_stages(
        y_ref, gb_ref, sz_ref, nw_ref, og_ref,
        u_ref, wq_ref, a_ref, kd_ref, gl_ref, st_ref)
    seq_start = ((step - 1) % tiles_per_seq) == 0

    @pl.when((b == 0) & new_kv & scoring)
    def _():
        kk = lax.broadcasted_iota(jnp.int32, (BLK, BLK), 0)
        qq = lax.broadcasted_iota(jnp.int32, (BLK, BLK), 1)
        for hh in range(2):
            h = 2 * hp + hh
            for kind in range(2):
                d = qq - kk + kind * BLK
                val = jnp.full((BLK, BLK), relb_ref[h, REL_BUCKETS - 1], F32)
                for bkt in range(REL_BUCKETS - 2, -1, -1):
                    val = jnp.where(d < BUCKET_LOWER[bkt + 1], relb_ref[h, bkt], val)
                val = val * LOG2E
                if kind == 0:
                    val = jnp.where(d >= 0, val, NEG)
                bias_ref[hh, kind] = val

    def key_means():
        per_tile = INPROJ_TILE // BLK
        ks = ksum_ref[...]
        km = jnp.concatenate([ks[j // per_tile, j % per_tile:j % per_tile + 1, :] for j in range(nblk)],
                             axis=0) * (1.0 / BLK)
        lane = lax.broadcasted_iota(jnp.int32, (nblk, LANES), 1)
        return jnp.concatenate([jnp.where(lane < HEAD_DIM, km, 0.0),
                                jnp.where(lane >= HEAD_DIM, km, 0.0)], axis=0)

    def item_tiles(t):
        i_hi = nblk - 1 - t
        tiles = [(0, t, "own"), (1, i_hi, "own"), (1, i_hi - 1, "prev")]
        if t >= 1:
            tiles.append((0, t - 1, "prev"))
        tiles += [(0, j, "far") for j in range(t - 1)]
        tiles += [(1, j, "far") for j in range(i_hi - 1)]
        assert len(tiles) == nblk + 1
        return tiles

    def score_stages(parity):
        ridx = lax.broadcasted_iota(jnp.int32, (nblk, BLK), 0)
        sub = lax.broadcasted_iota(jnp.int32, (LANES, BLK), 0)
        scale = HEAD_DIM ** -0.5 * LOG2E
        km = key_means()
        for e in range(2):
            t = 2 * parity + e
            slot = 2 * parity + e
            q_of = ((qlo_ref, e, t), (qhi_ref, 1 - e, nblk - 1 - t))
            for s, (q_ref, w, qi) in enumerate(q_of):
                qT = q_ref[0, w]
                gT = jnp.dot(km, qT, precision=HI, preferred_element_type=F32)
                past = ridx < qi
                for hh in range(2):
                    gm = jnp.where(past, gT[nblk * hh:nblk * (hh + 1)], -jnp.inf)
                    cnt = jnp.zeros((nblk, BLK), F32)
                    for jp in range(nblk):
                        row = gm[jp:jp + 1, :]
                        beats = (row > gm) | ((row == gm) & (ridx > jp))
                        cnt = cnt + jnp.where(beats, 1.0, 0.0)
                    visible = past & (cnt < MOBA_TOPK)
                    addm_ref[e, s, nblk * hh:nblk * (hh + 1), :] = jnp.where(visible, 0.0, NEG)
                    in_head = (sub >= HEAD_DIM * hh) & (sub < HEAD_DIM * (hh + 1))
                    qh_ref[e, s, hh] = jnp.where(in_head, qT * scale, 0.0).astype(BF16)
            yield
            tiles = item_tiles(t)
            for hh in range(2):
                cmax = {0: [], 1: []}
                offs = []
                for n, (s, kblk, cls) in enumerate(tiles):
                    lg = jnp.dot(kb_ref[0, kblk * BLK:(kblk + 1) * BLK, :], qh_ref[e, s, hh],
                                 preferred_element_type=F32)
                    if cls != "far":
                        lg = lg + bias_ref[hh, 0 if cls == "own" else 1]
                    lg_ref[slot, hh, n] = lg
                    cm = jnp.max(lg, axis=0, keepdims=True)
                    off = None
                    if cls != "own":
                        off = addm_ref[e, s, nblk * hh + kblk:nblk * hh + kblk + 1, :]
                        if cls == "far":
                            off = off + relb_ref[2 * hp + hh, REL_BUCKETS - 1] * LOG2E
                        cm = cm + off
                    cmax[s].append(cm)
                    offs.append(off)
                    yield
                m = {s: functools.reduce(jnp.maximum, cmax[s]) for s in (0, 1)}
                for n, (s, _, _) in enumerate(tiles):
                    moff_ref[slot, hh, n:n + 1, :] = m[s] if offs[n] is None else m[s] - offs[n]

    def softmax_pv_stages(parity):
        for e in range(2):
            t = 2 * parity + e
            slot = 2 * parity + e
            tiles = item_tiles(t)
            for hh in range(2):
                acc = {0: None, 1: None}
                for n, (s, kblk, _) in enumerate(tiles):
                    p = jnp.exp2(lg_ref[slot, hh, n] - moff_ref[slot, hh, n:n + 1, :])
                    pvn = jnp.dot(va_ref[0, kblk, hh], p.astype(BF16),
                                  preferred_element_type=F32)
                    acc[s] = pvn if acc[s] is None else acc[s] + pvn
                    yield
                for s in (0, 1):
                    oT_ref[0, 2 * e + s, HEAD_DIM * hh:HEAD_DIM * (hh + 1), :] = (
                        acc[s][0:HEAD_DIM] / acc[s][HEAD_DIM:HEAD_DIM + 1])

    ATT, GDN = 2, 1

    @pl.when(step == 0)
    def _():
        st_ref[...] = jnp.zeros(st_ref.shape, F32)
        _interleave_weighted((score_stages(0), ATT), (solve_stages(0), GDN))

    for parity in range(2):
        @pl.when((step > 0) & (step < nsteps - 1) & (step % 2 == parity))
        def _(parity=parity):
            _interleave_weighted((recurrence_stages(1 - parity, seq_start), GDN),
                                 (score_stages(parity), ATT),
                                 (solve_stages(parity), GDN),
                                 (softmax_pv_stages(1 - parity), ATT))

    @pl.when(step == nsteps - 1)
    def _():
        last = (nsteps - 2) % 2
        _interleave_weighted((recurrence_stages(last, seq_start), GDN), (softmax_pv_stages(last), ATT))


def _token_mixers(rel_bias, qT, k3, ksum, va, gqkv, sz, gb, nw_row, B, S):
    T = B * S
    W = GDN_WIDTH
    TILE = GDN_TILE
    nblk = S // MOBA_BLOCK
    half = nblk // 2
    assert TILE == GDN_HALF and S % TILE == 0
    assert BUCKET_LOWER[REL_BUCKETS - 1] <= MOBA_BLOCK + 1
    assert nblk == 8 and nblk + 1 <= 2 * SUBLANES
    nchunk = TILE // GDN_CHUNK
    ntiles = T // TILE
    npairs = (ATT_HEADS // 2) * B * (half // 2)
    assert npairs == ntiles
    nsteps = ntiles + 1

    def scored(s):
        p = jnp.minimum(s, npairs - 1)
        return p // (B * (half // 2)), (p // (half // 2)) % B, p % (half // 2)

    def done(s):
        p = jnp.maximum(s - 1, 0)
        return p // (B * (half // 2)), (p // (half // 2)) % B, p % (half // 2)

    def q_lo(s):
        hp, b, m = scored(s)
        return (b, m, hp, 0)

    def q_hi(s):
        hp, b, m = scored(s)
        return (b, half - 1 - m, hp, 0)

    def k_blk(s):
        hp, b, _ = scored(s)
        return (b, 0, hp)

    def v_blk(s):
        hp, b, _ = done(s)
        return (b, 0, hp, 0, 0)

    def o_blk(s):
        hp, b, m = done(s)
        return (b, m, hp, 0)

    cur_tile = lambda off: (lambda s: (jnp.minimum(s, ntiles - 1), off))
    prev_tile = lambda s: (jnp.maximum(s - 1, 0), 0)
    return pl.pallas_call(
        functools.partial(_mixers_kernel, nblk=nblk, nbatch=B, tiles_per_seq=S // TILE, nsteps=nsteps),
        grid=(nsteps,),
        in_specs=[
            pl.BlockSpec(memory_space=pltpu.SMEM),
            pl.BlockSpec((1, 2, LANES, MOBA_BLOCK), q_lo),
            pl.BlockSpec((1, 2, LANES, MOBA_BLOCK), q_hi),
            pl.BlockSpec((1, S, LANES), k_blk),
            pl.BlockSpec((S // INPROJ_TILE, SUBLANES, LANES), k_blk),
            pl.BlockSpec((1, nblk, 2, V_ROWS, MOBA_BLOCK), v_blk),
            pl.BlockSpec((TILE, 3 * W), cur_tile(0)),
            pl.BlockSpec((TILE, LANES), cur_tile(0)),
            pl.BlockSpec((TILE, W), prev_tile),
            pl.BlockSpec((1, W), lambda s: (0, 0)),
        ],
        out_specs=[
            pl.BlockSpec((1, 4, LANES, MOBA_BLOCK), o_blk),
            pl.BlockSpec((TILE, W), prev_tile),
        ],
        out_shape=[
            jax.ShapeDtypeStruct((B, nblk, ATT_WIDTH, MOBA_BLOCK), F32),
            jax.ShapeDtypeStruct((T, GDN_WIDTH), F32),
        ],
        scratch_shapes=[
            pltpu.VMEM((2, 2, MOBA_BLOCK, MOBA_BLOCK), F32),
            pltpu.VMEM((2, 2, 2 * nblk, MOBA_BLOCK), F32),
            pltpu.VMEM((2, 2, 2, LANES, MOBA_BLOCK), BF16),
            pltpu.VMEM((4, 2, nblk + 1, MOBA_BLOCK, MOBA_BLOCK), F32),
            pltpu.VMEM((4, 2, 2 * SUBLANES, MOBA_BLOCK), F32),
            pltpu.VMEM((2, TILE, W), F32),
            pltpu.VMEM((2, nchunk, 2 * GDN_CHUNK, W), BF16),
            pltpu.VMEM((2, TILE, W), BF16),
            pltpu.VMEM((2, TILE, W), BF16),
            pltpu.VMEM((2, nchunk * SUBLANES, W), F32),
            pltpu.VMEM((W // LANES, LANES, LANES), F32),
        ],
        compiler_params=pltpu.CompilerParams(
            dimension_semantics=("arbitrary",), vmem_limit_bytes=VMEM_LIMIT),
        name="token_mixers",
    )(rel_bias, qT, qT, k3, ksum, va, gqkv, gb, sz, nw_row)


def _out_mlp_kernel(x_ref, oTa_ref, oTb_ref, og_ref, woa_ref, wog_ref, pmn_ref, pre_ref, post_ref,
                    wup_ref, wdn_ref, out_ref):
    oT = jnp.concatenate([oTa_ref[0, 0], oTb_ref[0, 0]], axis=1)
    o_att = oT.T.astype(BF16)
    mix = jnp.dot(o_att, woa_ref[...], preferred_element_type=F32)
    mix = mix + jnp.dot(og_ref[...].astype(BF16), wog_ref[...], preferred_element_type=F32)
    x1 = x_ref[...] + _rms(mix, pmn_ref[...])
    h = _rms(x1, pre_ref[...]).astype(BF16)
    acc = jnp.zeros((ROW_TILE, D_MODEL), F32)
    for c in range(D_FF // FF_TILE):
        up = jnp.dot(h, wup_ref[:, c * FF_TILE:(c + 1) * FF_TILE], preferred_element_type=F32)
        act = jnp.square(jnp.maximum(up, 0.0)).astype(BF16)
        acc = acc + jnp.dot(act, wdn_ref[c * FF_TILE:(c + 1) * FF_TILE, :], preferred_element_type=F32)
    out_ref[...] = x1 + _rms(acc, post_ref[...])


def _out_mlp(xf, oT, og, woa, wog, pmn, pre, post, wup, wdn, B, S):
    T = B * S
    nblk = S // MOBA_BLOCK
    tiles_per_seq = S // ROW_TILE
    assert ROW_TILE == 2 * MOBA_BLOCK
    const = lambda i: (0, 0)
    row = lambda i: (i, 0)

    def att_block(which):
        def index(i):
            blk = 2 * (i % tiles_per_seq) + which
            return (i // tiles_per_seq, _paired_pos(blk, nblk), 0, 0)
        return index

    single = dict(pipeline_mode=pl.Buffered(1))
    return pl.pallas_call(
        _out_mlp_kernel,
        grid=(T // ROW_TILE,),
        in_specs=[
            pl.BlockSpec((ROW_TILE, D_MODEL), row),
            pl.BlockSpec((1, 1, ATT_WIDTH, MOBA_BLOCK), att_block(0)),
            pl.BlockSpec((1, 1, ATT_WIDTH, MOBA_BLOCK), att_block(1)),
            pl.BlockSpec((ROW_TILE, GDN_WIDTH), row),
            pl.BlockSpec(woa.shape, const, **single),
            pl.BlockSpec(wog.shape, const, **single),
            pl.BlockSpec((1, D_MODEL), const),
            pl.BlockSpec((1, D_MODEL), const),
            pl.BlockSpec((1, D_MODEL), const),
            pl.BlockSpec(wup.shape, const, **single),
            pl.BlockSpec(wdn.shape, const, **single),
        ],
        out_specs=pl.BlockSpec((ROW_TILE, D_MODEL), row),
        out_shape=jax.ShapeDtypeStruct((T, D_MODEL), F32),
        compiler_params=pltpu.CompilerParams(
            dimension_semantics=("arbitrary",), vmem_limit_bytes=VMEM_LIMIT),
        name="out_mlp",
    )(xf, oT, oT, og, woa, wog, pmn, pre, post, wup, wdn)


def kernel(x, w_in, w_out, conv_w, A_log, dt_bias, gdn_norm_w, rel_bias, pre_mix_norm,
           post_mix_norm, pre_mlp_norm, post_mlp_norm, w_up, w_down):
    B, S, D = x.shape
    assert D == D_MODEL and S % ROW_TILE == 0 and S % MOBA_BLOCK == 0
    T = B * S
    depth = w_in.shape[0]
    xf = x.reshape(T, D)
    o0, o1, o2, o3, o4 = 0, ATT_WIDTH, 2 * ATT_WIDTH, 3 * ATT_WIDTH, 3 * ATT_WIDTH + 3 * GDN_WIDTH
    o5 = o4 + GDN_WIDTH
    for l in range(depth):
        wi = w_in[l]
        wqT = wi[:, o0:o1].T.astype(BF16)
        wk = wi[:, o1:o2].astype(BF16)
        wvT = wi[:, o2:o3].T.astype(BF16)
        wg = wi[:, o3:o4].astype(BF16)
        wz = wi[:, o4:o5].astype(BF16)
        wab = jnp.pad(wi[:, o5:], ((0, 0), (0, LANES - 2 * GDN_HEADS))).astype(BF16)
        pad8 = lambda v: jnp.pad(v.astype(F32), (0, LANES - GDN_HEADS))[None, :]
        qT, k, ksum, va, gqkv, sz, gb = _inproj(xf, pre_mix_norm[l][None, :], wqT, wk, wvT, wg, wz, wab,
                                                conv_w[l], pad8(A_log[l]), pad8(dt_bias[l]), B, S)
        oT, og = _token_mixers(rel_bias.astype(F32), qT, k.reshape(B, S, ATT_WIDTH), ksum, va, gqkv, sz,
                               gb, jnp.tile(gdn_norm_w[l], GDN_HEADS)[None, :], B, S)
        wo = w_out[l].astype(BF16)
        xf = _out_mlp(xf, oT, og, wo[:ATT_WIDTH], wo[ATT_WIDTH:], post_mix_norm[l][None, :],
                      pre_mlp_norm[l][None, :], post_mlp_norm[l][None, :],
                      w_up[l].astype(BF16), w_down[l].astype(BF16), B, S)
    return xf.reshape(B, S, D)
```

```python
import functools
import math

import jax
import jax.numpy as jnp
from jax import lax
from jax.experimental import pallas as pl
from jax.experimental.pallas import tpu as pltpu

F32 = jnp.float32
BF16 = jnp.bfloat16
HI = lax.Precision.HIGHEST

D_MODEL = 1024
HEAD_DIM = 64
ATT_HEADS = 8
GDN_HEADS = 8
ATT_WIDTH = ATT_HEADS * HEAD_DIM
GDN_WIDTH = GDN_HEADS * HEAD_DIM
MOBA_BLOCK = 256
MOBA_TOPK = 3
GDN_CHUNK = 64
CONV_WIDTH = 4
D_FF = 4 * D_MODEL
REL_BUCKETS = 32
REL_MAX_EXACT = 16
REL_MAX_DIST = 128
EPS = 1e-6
NEG = -1e30
LOG2E = math.log2(math.e)

LANES = 128
SUBLANES = 8
VMEM_LIMIT = 56 * 1024 * 1024
ROW_TILE = 512
INPROJ_TILE = 512
FF_TILE = 1024

NT = (((1,), (1,)), ((), ()))
TN = (((0,), (0,)), ((), ()))


def _bucket_lower_bounds():
    def bucket(d):
        if d < REL_MAX_EXACT:
            return d
        t = math.log(d / REL_MAX_EXACT) / math.log(REL_MAX_DIST / REL_MAX_EXACT)
        t = t * (REL_BUCKETS - REL_MAX_EXACT)
        assert d in (REL_MAX_EXACT, REL_MAX_DIST) or abs(t - round(t)) > 1e-6
        return min(REL_MAX_EXACT + int(t + 1e-9), REL_BUCKETS - 1)
    lower = []
    for b in range(REL_BUCKETS):
        d = 0
        while bucket(d) < b:
            d += 1
        lower.append(d)
    return lower


BUCKET_LOWER = _bucket_lower_bounds()


def _sigmoid(x):
    return 0.5 * jnp.tanh(0.5 * x) + 0.5


def _silu_of_half(h):
    return h + h * jnp.tanh(h)


def _rms(x, w):
    return x * lax.rsqrt(jnp.mean(x * x, axis=-1, keepdims=True) + EPS) * w


def _split_bf16(x, parts):
    out = []
    for _ in range(parts):
        h = x.astype(BF16)
        out.append(h)
        x = x - h.astype(F32)
    return out


def _dot_split_rhs(c, x, parts):
    acc = None
    for h in _split_bf16(x, parts):
        d = jnp.dot(c, h, preferred_element_type=F32)
        acc = d if acc is None else acc + d
    return acc


CONV_COLS = 512
ROW_PITCH = 72


def _inproj_kernel(x_ref, xp_ref, nw_ref, wqT_ref, wk_ref, wvT_ref, wg_ref, wz_ref, wab_ref,
                   cw_ref, alog_ref, dtb_ref,
                   qT_ref, k_ref, ksum_ref, va_ref, g_ref, z_ref, gb_ref, hn_ref, y_ref, *, tiles_per_seq):
    TM = INPROJ_TILE
    NV = TM // SUBLANES
    hn = _rms(x_ref[...], nw_ref[...])
    h = hn.astype(BF16)

    z = jnp.dot(h, wz_ref[...], preferred_element_type=F32)
    z_ref[...] = _silu_of_half(0.5 * z)
    ab = jnp.dot(h, wab_ref[...], preferred_element_type=F32)
    xs = ab + dtb_ref[...]
    log_decay = -jnp.exp(alog_ref[...]) * (jnp.maximum(xs, 0.0) + jnp.log1p(jnp.exp(-jnp.abs(xs))))
    lane = lax.broadcasted_iota(jnp.int32, ab.shape, 1)
    gb_ref[...] = jnp.where(lane < GDN_HEADS, log_decay, _sigmoid(ab))

    for j in range(D_MODEL // LANES):
        for b in range(SUBLANES):
            hn_ref[j, ROW_PITCH * b:ROW_PITCH * b + NV, :] = hn[NV * b:NV * (b + 1), LANES * j:LANES * (j + 1)]
    h_perm = jnp.concatenate(
        [jnp.concatenate([hn_ref[j, pl.ds(u, SUBLANES, stride=ROW_PITCH), :] for j in range(D_MODEL // LANES)],
                         axis=1)
         for u in range(NV)], axis=0).astype(BF16)
    hp = _rms(xp_ref[...], nw_ref[...]).astype(BF16)
    seq_start = (pl.program_id(0) % tiles_per_seq) == 0
    first_sublane = lax.broadcasted_iota(jnp.int32, (SUBLANES, CONV_COLS), 0) == 0
    for c in range(3 * GDN_WIDTH // CONV_COLS):
        cols = slice(c * CONV_COLS, (c + 1) * CONV_COLS)
        cur = jnp.dot(h_perm, wg_ref[:, cols], preferred_element_type=F32)
        prev8 = jnp.dot(hp, wg_ref[:, cols], preferred_element_type=F32)
        prev8 = jnp.where(seq_start, 0.0, prev8)
        wrap = []
        for i in range(CONV_WIDTH - 1):
            u = NV - (CONV_WIDTH - 1) + i
            ctx = prev8[SUBLANES - (CONV_WIDTH - 1) + i:SUBLANES - (CONV_WIDTH - 1) + i + 1]
            wrap.append(jnp.where(first_sublane, ctx, pltpu.roll(cur[SUBLANES * u:SUBLANES * (u + 1)], 1, 0)))
        cw_half = 0.5 * cw_ref[:, cols]
        acc = cur * cw_half[CONV_WIDTH - 1:CONV_WIDTH]
        for s in range(1, CONV_WIDTH):
            tap = jnp.concatenate(wrap[CONV_WIDTH - 1 - s:] + [cur[0:TM - SUBLANES * s]], axis=0)
            acc = acc + tap * cw_half[CONV_WIDTH - 1 - s:CONV_WIDTH - s]
        y = _silu_of_half(acc)
        for j in range(CONV_COLS // LANES):
            for u in range(NV):
                y_ref[c, j, pl.ds(u, SUBLANES, stride=ROW_PITCH), :] = y[SUBLANES * u:SUBLANES * (u + 1),
                                                                         LANES * j:LANES * (j + 1)]
        g_ref[:, cols] = jnp.concatenate(
            [jnp.concatenate([y_ref[c, j, ROW_PITCH * b:ROW_PITCH * b + NV, :] for b in range(SUBLANES)], axis=0)
             for j in range(CONV_COLS // LANES)], axis=1)

    qT = lax.dot_general(wqT_ref[...], h, NT, preferred_element_type=F32)
    vT = lax.dot_general(wvT_ref[...], h, NT, preferred_element_type=F32)
    k = jnp.dot(h, wk_ref[...], preferred_element_type=F32)
    k_ref[...] = k.astype(BF16)
    ones_row = jnp.where(lax.broadcasted_iota(jnp.int32, (V_ROWS - HEAD_DIM, MOBA_BLOCK), 0) == 0,
                         1.0, 0.0).astype(BF16)
    ksum_ref[...] = jnp.zeros(ksum_ref.shape, F32)
    for t in range(INPROJ_TILE // MOBA_BLOCK):
        blk = slice(t * MOBA_BLOCK, (t + 1) * MOBA_BLOCK)
        qT_ref[0, t] = qT[:, blk]
        ksum_ref[0, t:t + 1, :] = jnp.sum(k[blk], axis=0, keepdims=True)
        for hh in range(ATT_HEADS):
            va_ref[0, t, hh, 0:HEAD_DIM, :] = vT[HEAD_DIM * hh:HEAD_DIM * (hh + 1), blk].astype(BF16)
            va_ref[0, t, hh, HEAD_DIM:V_ROWS, :] = ones_row


def _inproj(xf, nw, wqT, wk, wvT, wg, wz, wab, conv_w, alog_pad, dtb_pad, B, S):
    T = B * S
    TM = INPROJ_TILE
    assert S % TM == 0
    nblk = S // MOBA_BLOCK
    tiles_per_seq = S // TM
    blk_per_tile = TM // MOBA_BLOCK
    const = lambda i: (0, 0)
    row = lambda i: (i, 0)
    tr = lambda i: (i // tiles_per_seq, i % tiles_per_seq, 0, 0)
    prev_rows = lambda i: (jnp.maximum(i * (TM // SUBLANES) - 1, 0), 0)
    single = dict(pipeline_mode=pl.Buffered(1))
    return pl.pallas_call(
        functools.partial(_inproj_kernel, tiles_per_seq=tiles_per_seq),
        grid=(T // TM,),
        in_specs=[
            pl.BlockSpec((TM, D_MODEL), row),
            pl.BlockSpec((SUBLANES, D_MODEL), prev_rows),
            pl.BlockSpec((1, D_MODEL), const),
            pl.BlockSpec(wqT.shape, const, **single),
            pl.BlockSpec(wk.shape, const, **single),
            pl.BlockSpec(wvT.shape, const, **single),
            pl.BlockSpec(wg.shape, const, **single),
            pl.BlockSpec(wz.shape, const, **single),
            pl.BlockSpec(wab.shape, const, **single),
            pl.BlockSpec(conv_w.shape, const),
            pl.BlockSpec((1, LANES), const),
            pl.BlockSpec((1, LANES), const),
        ],
        out_specs=[
            pl.BlockSpec((1, blk_per_tile, ATT_WIDTH, MOBA_BLOCK), tr),
            pl.BlockSpec((TM, ATT_WIDTH), row),
            pl.BlockSpec((1, SUBLANES, ATT_WIDTH), lambda i: (i, 0, 0)),
            pl.BlockSpec((1, blk_per_tile, ATT_HEADS, V_ROWS, MOBA_BLOCK),
                         lambda i: (i // tiles_per_seq, i % tiles_per_seq, 0, 0, 0)),
            pl.BlockSpec((TM, 3 * GDN_WIDTH), row),
            pl.BlockSpec((TM, GDN_WIDTH), row),
            pl.BlockSpec((TM, LANES), row),
        ],
        out_shape=[
            jax.ShapeDtypeStruct((B, nblk, ATT_WIDTH, MOBA_BLOCK), F32),
            jax.ShapeDtypeStruct((T, ATT_WIDTH), BF16),
            jax.ShapeDtypeStruct((T // TM, SUBLANES, ATT_WIDTH), F32),
            jax.ShapeDtypeStruct((B, nblk, ATT_HEADS, V_ROWS, MOBA_BLOCK), BF16),
            jax.ShapeDtypeStruct((T, 3 * GDN_WIDTH), F32),
            jax.ShapeDtypeStruct((T, GDN_WIDTH), F32),
            jax.ShapeDtypeStruct((T, LANES), F32),
        ],
        scratch_shapes=[
            pltpu.VMEM((D_MODEL // LANES, SUBLANES * ROW_PITCH, LANES), F32),
            pltpu.VMEM((3 * GDN_WIDTH // CONV_COLS, CONV_COLS // LANES, SUBLANES * ROW_PITCH, LANES), F32),
        ],
        compiler_params=pltpu.CompilerParams(
            dimension_semantics=("arbitrary",), vmem_limit_bytes=VMEM_LIMIT),
        name="inproj",
    )(xf, xf, nw, wqT, wk, wvT, wg, wz, wab, conv_w, alog_pad, dtb_pad)


V_ROWS = HEAD_DIM + 16


def _paired_pos(i, nblk):
    return jnp.where(i < nblk // 2, 2 * i, 2 * (nblk - 1 - i) + 1)


GDN_TILE = 256
GDN_HALF = 2 * LANES


def _gdn_stages(y_ref, gb_ref, sz_ref, nw_ref, out_ref,
                u_ref, wq_ref, a_ref, kd_ref, gl_ref, st_ref):
    C = GDN_CHUNK
    W = GDN_WIDTH
    TILE = GDN_TILE
    npair = W // LANES

    r_w = lax.broadcasted_iota(jnp.int32, (GDN_HALF, GDN_HALF), 0)
    c_w = lax.broadcasted_iota(jnp.int32, (GDN_HALF, GDN_HALF), 1)
    head_ones = jnp.where((r_w // HEAD_DIM) == (c_w // HEAD_DIM), 1.0, 0.0).astype(BF16)
    ltri_bd = jnp.where(((r_w // C) == (c_w // C)) & (c_w <= r_w), 1.0, 0.0).astype(BF16)
    tok = lax.broadcasted_iota(jnp.int32, (TILE, W), 0) % C
    col = lax.broadcasted_iota(jnp.int32, (TILE, W), 1) % HEAD_DIM
    causal_t = tok >= col
    strict_t = tok > col
    lane_t = lax.broadcasted_iota(jnp.int32, (TILE, LANES), 1)

    lane = lax.broadcasted_iota(jnp.int32, (C, LANES), 1)
    rowi = lax.broadcasted_iota(jnp.int32, (C, LANES), 0)
    first_head = lane < HEAD_DIM
    strict = rowi > (lane % HEAD_DIM)
    eye2 = jnp.where(rowi == (lane % HEAD_DIM), 1.0, 0.0)
    lane2 = lax.broadcasted_iota(jnp.int32, (C, 2 * LANES), 1)
    first_head2 = (lane2 % LANES) < HEAD_DIM
    r_l = lax.broadcasted_iota(jnp.int32, (LANES, LANES), 0)
    c_l = lax.broadcasted_iota(jnp.int32, (LANES, LANES), 1)
    same_head = (r_l // HEAD_DIM) == (c_l // HEAD_DIM)
    pair_ones = jnp.where(same_head, 1.0, 0.0).astype(BF16)

    def stack(x, mask):
        return jnp.concatenate([jnp.where(mask, x, 0.0), jnp.where(mask, 0.0, x)], axis=0)

    dot = functools.partial(jnp.dot, preferred_element_type=F32)

    def head_sumsq(ys):
        halves = [(y * y).astype(BF16)[:, h:h + GDN_HALF] for y in ys for h in range(0, W, GDN_HALF)]
        sums = dot(jnp.concatenate(halves, axis=0), head_ones)
        per = W // GDN_HALF
        return [jnp.concatenate([sums[(i * per + j) * TILE:(i * per + j + 1) * TILE] for j in range(per)],
                                axis=1) for i in range(len(ys))]

    def solve_stages(slot):
        yq = y_ref[:, 0:W]
        yk = y_ref[:, W:2 * W]
        yv = y_ref[:, 2 * W:3 * W]
        ssq, ssk = head_sumsq([yq, yk])
        qn = yq * lax.rsqrt(ssq + EPS) * (HEAD_DIM ** -0.5)
        kn = yk * lax.rsqrt(ssk + EPS)
        yield
        gbt = gb_ref[...]

        def spread(col0):
            pairs = []
            for p in range(npair):
                a = jnp.broadcast_to(gbt[:, col0 + 2 * p:col0 + 2 * p + 1], (TILE, LANES))
                b = jnp.broadcast_to(gbt[:, col0 + 2 * p + 1:col0 + 2 * p + 2], (TILE, LANES))
                pairs.append(jnp.where(lane_t < HEAD_DIM, a, b))
            return jnp.concatenate(pairs, axis=1)

        g = spread(0)
        beta = spread(GDN_HEADS)
        gcd = _dot_split_rhs(ltri_bd, jnp.concatenate([g, jnp.where(strict_t, g, 0.0)], axis=1), 2)
        yield
        gc = gcd[:, :W]
        decay = jnp.where(causal_t, jnp.exp(jnp.where(causal_t, gcd[:, W:], 0.0)), 0.0)
        egc = jnp.exp(gc)
        kb = kn * beta
        rv = yv * beta
        rk = kb * egc
        qd = qn * egc
        for cc in range(TILE // C):
            rs = slice(cc * C, (cc + 1) * C)
            g_last = gc[(cc + 1) * C - 1:(cc + 1) * C, :]
            kd_ref[slot, rs, :] = (kn[rs] * jnp.exp(g_last - gc[rs])).astype(BF16)
            gl_ref[slot, cc * SUBLANES:(cc + 1) * SUBLANES, :] = (
                jnp.broadcast_to(jnp.exp(g_last), (SUBLANES, W)))
        units = [(slice(cc * C, (cc + 1) * C), slice(LANES * p, LANES * (p + 1)), cc)
                 for cc in range(TILE // C) for p in range(npair)]
        kqs = [lax.dot_general(jnp.concatenate([kn[rs, ls], qn[rs, ls]], axis=0).astype(BF16),
                               stack(kn[rs, ls], first_head).astype(BF16), NT,
                               preferred_element_type=F32) for rs, ls, _ in units]
        yield
        ps = [-jnp.where(strict, kq[0:C] * beta[rs, ls] * decay[rs, ls], 0.0)
              for kq, (rs, ls, _) in zip(kqs, units)]
        ss = [eye2 + p for p in ps]
        ps = [dot(p.astype(BF16), stack(p, first_head).astype(BF16)) for p in ps]
        yield
        nround = int(math.log2(C))
        for k in range(1, nround):
            rhs = [stack(s_, first_head).astype(BF16) for s_ in ss]
            if k + 1 < nround:
                rhs = [jnp.concatenate([stack(p, first_head).astype(BF16), sx], axis=1)
                       for p, sx in zip(ps, rhs)]
            outs = [dot(p.astype(BF16), sx) for p, sx in zip(ps, rhs)]
            if k + 1 < nround:
                ps = [o[:, :LANES] for o in outs]
                ss = [s_ + o[:, LANES:] for s_, o in zip(ss, outs)]
            else:
                ss = [s_ + o for s_, o in zip(ss, outs)]
            yield
        xs = [dot(s_.astype(BF16),
                  stack(jnp.concatenate([rv[rs, ls], rk[rs, ls]], axis=1), first_head2).astype(BF16))
              for s_, (rs, ls, _) in zip(ss, units)]
        yield
        for x, kq, (rs, ls, cc) in zip(xs, kqs, units):
            u_ref[slot, rs, ls] = x[:, :LANES]
            wq_ref[slot, cc, 0:C, ls] = x[:, LANES:].astype(BF16)
            wq_ref[slot, cc, C:2 * C, ls] = qd[rs, ls].astype(BF16)
            a_ref[slot, rs, ls] = (kq[C:2 * C] * decay[rs, ls]).astype(BF16)

    lss = [slice(LANES * p, LANES * (p + 1)) for p in range(npair)]

    def recurrence_stages(slot, seq_start):
        states = [jnp.where(seq_start, 0.0, st_ref[p]) for p in range(npair)]
        pending = None

        def finish(rs, os_):
            sq = jnp.concatenate([(o * o).astype(BF16) for o in os_], axis=0)
            ms_all = dot(sq, pair_ones) * (1.0 / HEAD_DIM)
            for p, (ls, o) in enumerate(zip(lss, os_)):
                ms = ms_all[p * C:(p + 1) * C]
                out_ref[rs, ls] = o * lax.rsqrt(ms + EPS) * nw_ref[:, ls] * sz_ref[rs, ls]

        for cc in range(TILE // C):
            rs = slice(cc * C, (cc + 1) * C)
            wqs = [dot(wq_ref[slot, cc, :, ls], st.astype(BF16)) for ls, st in zip(lss, states)]
            if pending is not None:
                finish(*pending)
            yield
            v_news = [u_ref[slot, rs, ls] - wq[0:C] for ls, wq in zip(lss, wqs)]
            kvs = [lax.dot_general(kd_ref[slot, rs, ls], v.astype(BF16), TN, preferred_element_type=F32)
                   for ls, v in zip(lss, v_news)]
            os_ = [wq[C:2 * C] + dot(a_ref[slot, rs, ls], stack(v, first_head).astype(BF16))
                   for ls, wq, v in zip(lss, wqs, v_news)]
            states = [st * gl_ref[slot, cc * SUBLANES:cc * SUBLANES + 1, ls] + jnp.where(same_head, kv, 0.0)
                      for ls, st, kv in zip(lss, states, kvs)]
            pending = (rs, os_)
            yield
        finish(*pending)
        for p in range(npair):
            st_ref[p] = states[p]

    return solve_stages, recurrence_stages


def _interleave_weighted(*gens_and_weights):
    live = [[gen, weight] for gen, weight in gens_and_weights]
    while live:
        for entry in list(live):
            for _ in range(entry[1]):
                try:
                    next(entry[0])
                except StopIteration:
                    live.remove(entry)
                    break


def _mixers_kernel(relb_ref, qlo_ref, qhi_ref, kb_ref, ksum_ref, va_ref,
                   y_ref, gb_ref, sz_ref, nw_ref,
                   oT_ref, og_ref,
                   bias_ref, addm_ref, qh_ref, lg_ref, moff_ref,
                   u_ref, wq_ref, a_ref, kd_ref, gl_ref, st_ref,
                   *, nblk, nbatch, tiles_per_seq, nsteps):
    step = pl.program_id(0)
    BLK = MOBA_BLOCK
    half = nblk // 2
    pair = jnp.minimum(step, nsteps - 2)
    hp = pair // (nbatch * (half // 2))
    b = (pair // (half // 2)) % nbatch
    scoring = step < nsteps - 1
    new_kv = (pair % (half // 2)) == 0

    solve_stages, recurrence_stages = _gdn_stages(
        y_ref, gb_ref, sz_ref, nw_ref, og_ref,
        u_ref, wq_ref, a_ref, kd_ref, gl_ref, st_ref)
    seq_start = ((step - 1) % tiles_per_seq) == 0

    @pl.when((b == 0) & new_kv & scoring)
    def _():
        kk = lax.broadcasted_iota(jnp.int32, (BLK, BLK), 0)
        qq = lax.broadcasted_iota(jnp.int32, (BLK, BLK), 1)
        for hh in range(2):
            h = 2 * hp + hh
            for kind in range(2):
                d = qq - kk + kind * BLK
                val = jnp.full((BLK, BLK), relb_ref[h, REL_BUCKETS - 1], F32)
                for bkt in range(REL_BUCKETS - 2, -1, -1):
                    val = jnp.where(d < BUCKET_LOWER[bkt + 1], relb_ref[h, bkt], val)
                val = val * LOG2E
                if kind == 0:
                    val = jnp.where(d >= 0, val, NEG)
                bias_ref[hh, kind] = val

    def key_means():
        per_tile = INPROJ_TILE // BLK
        ks = ksum_ref[...]
        km = jnp.concatenate([ks[j // per_tile, j % per_tile:j % per_tile + 1, :] for j in range(nblk)],
                             axis=0) * (1.0 / BLK)
        lane = lax.broadcasted_iota(jnp.int32, (nblk, LANES), 1)
        return jnp.concatenate([jnp.where(lane < HEAD_DIM, km, 0.0),
                                jnp.where(lane >= HEAD_DIM, km, 0.0)], axis=0)

    def item_tiles(t):
        i_hi = nblk - 1 - t
        tiles = [(0, t, "own"), (1, i_hi, "own"), (1, i_hi - 1, "prev")]
        if t >= 1:
            tiles.append((0, t - 1, "prev"))
        tiles += [(0, j, "far") for j in range(t - 1)]
        tiles += [(1, j, "far") for j in range(i_hi - 1)]
        assert len(tiles) == nblk + 1
        return tiles

    def score_stages(parity):
        ridx = lax.broadcasted_iota(jnp.int32, (nblk, BLK), 0)
        sub = lax.broadcasted_iota(jnp.int32, (LANES, BLK), 0)
        scale = HEAD_DIM ** -0.5 * LOG2E
        km = key_means()
        for e in range(2):
            t = 2 * parity + e
            slot = 2 * parity + e
            q_of = ((qlo_ref, e, t), (qhi_ref, 1 - e, nblk - 1 - t))
            for s, (q_ref, w, qi) in enumerate(q_of):
                qT = q_ref[0, w]
                gT = jnp.dot(km, qT, precision=HI, preferred_element_type=F32)
                past = ridx < qi
                for hh in range(2):
                    gm = jnp.where(past, gT[nblk * hh:nblk * (hh + 1)], -jnp.inf)
                    cnt = jnp.zeros((nblk, BLK), F32)
                    for jp in range(nblk):
                        row = gm[jp:jp + 1, :]
                        beats = (row > gm) | ((row == gm) & (ridx > jp))
                        cnt = cnt + jnp.where(beats, 1.0, 0.0)
                    visible = past & (cnt < MOBA_TOPK)
                    addm_ref[e, s, nblk * hh:nblk * (hh + 1), :] = jnp.where(visible, 0.0, NEG)
                    in_head = (sub >= HEAD_DIM * hh) & (sub < HEAD_DIM * (hh + 1))
                    qh_ref[e, s, hh] = jnp.where(in_head, qT * scale, 0.0).astype(BF16)
            yield
            tiles = item_tiles(t)
            for hh in range(2):
                cmax = {0: [], 1: []}
                offs = []
                for n, (s, kblk, cls) in enumerate(tiles):
                    lg = jnp.dot(kb_ref[0, kblk * BLK:(kblk + 1) * BLK, :], qh_ref[e, s, hh],
                                 preferred_element_type=F32)
                    if cls != "far":
                        lg = lg + bias_ref[hh, 0 if cls == "own" else 1]
                    lg_ref[slot, hh, n] = lg
                    cm = jnp.max(lg, axis=0, keepdims=True)
                    off = None
                    if cls != "own":
                        off = addm_ref[e, s, nblk * hh + kblk:nblk * hh + kblk + 1, :]
                        if cls == "far":
                            off = off + relb_ref[2 * hp + hh, REL_BUCKETS - 1] * LOG2E
                        cm = cm + off
                    cmax[s].append(cm)
                    offs.append(off)
                    yield
                m = {s: functools.reduce(jnp.maximum, cmax[s]) for s in (0, 1)}
                for n, (s, _, _) in enumerate(tiles):
                    moff_ref[slot, hh, n:n + 1, :] = m[s] if offs[n] is None else m[s] - offs[n]

    def softmax_pv_stages(parity):
        for e in range(2):
            t = 2 * parity + e
            slot = 2 * parity + e
            tiles = item_tiles(t)
            for hh in range(2):
                acc = {0: None, 1: None}
                for n, (s, kblk, _) in enumerate(tiles):
                    p = jnp.exp2(lg_ref[slot, hh, n] - moff_ref[slot, hh, n:n + 1, :])
                    pvn = jnp.dot(va_ref[0, kblk, hh], p.astype(BF16),
                                  preferred_element_type=F32)
                    acc[s] = pvn if acc[s] is None else acc[s] + pvn
                    yield
                for s in (0, 1):
                    oT_ref[0, 2 * e + s, HEAD_DIM * hh:HEAD_DIM * (hh + 1), :] = (
                        acc[s][0:HEAD_DIM] / acc[s][HEAD_DIM:HEAD_DIM + 1])

    ATT, GDN = 2, 1

    @pl.when(step == 0)
    def _():
        st_ref[...] = jnp.zeros(st_ref.shape, F32)
        _interleave_weighted((score_stages(0), ATT), (solve_stages(0), GDN))

    for parity in range(2):
        @pl.when((step > 0) & (step < nsteps - 1) & (step % 2 == parity))
        def _(parity=parity):
            _interleave_weighted((recurrence_stages(1 - parity, seq_start), GDN),
                                 (score_stages(parity), ATT),
                                 (solve_stages(parity), GDN),
                                 (softmax_pv_stages(1 - parity), ATT))

    @pl.when(step == nsteps - 1)
    def _():
        last = (nsteps - 2) % 2
        _interleave_weighted((recurrence_stages(last, seq_start), GDN), (softmax_pv_stages(last), ATT))


def _token_mixers(rel_bias, qT, k3, ksum, va, gqkv, sz, gb, nw_row, B, S):
    T = B * S
    W = GDN_WIDTH
    TILE = GDN_TILE
    nblk = S // MOBA_BLOCK
    half = nblk // 2
    assert TILE == GDN_HALF and S % TILE == 0
    assert BUCKET_LOWER[REL_BUCKETS - 1] <= MOBA_BLOCK + 1
    assert nblk == 8 and nblk + 1 <= 2 * SUBLANES
    nchunk = TILE // GDN_CHUNK
    ntiles = T // TILE
    npairs = (ATT_HEADS // 2) * B * (half // 2)
    assert npairs == ntiles
    nsteps = ntiles + 1

    def scored(s):
        p = jnp.minimum(s, npairs - 1)
        return p // (B * (half // 2)), (p // (half // 2)) % B, p % (half // 2)

    def done(s):
        p = jnp.maximum(s - 1, 0)
        return p // (B * (half // 2)), (p // (half // 2)) % B, p % (half // 2)

    def q_lo(s):
        hp, b, m = scored(s)
        return (b, m, hp, 0)

    def q_hi(s):
        hp, b, m = scored(s)
        return (b, half - 1 - m, hp, 0)

    def k_blk(s):
        hp, b, _ = scored(s)
        return (b, 0, hp)

    def v_blk(s):
        hp, b, _ = done(s)
        return (b, 0, hp, 0, 0)

    def o_blk(s):
        hp, b, m = done(s)
        return (b, m, hp, 0)

    cur_tile = lambda off: (lambda s: (jnp.minimum(s, ntiles - 1), off))
    prev_tile = lambda s: (jnp.maximum(s - 1, 0), 0)
    return pl.pallas_call(
        functools.partial(_mixers_kernel, nblk=nblk, nbatch=B, tiles_per_seq=S // TILE, nsteps=nsteps),
        grid=(nsteps,),
        in_specs=[
            pl.BlockSpec(memory_space=pltpu.SMEM),
            pl.BlockSpec((1, 2, LANES, MOBA_BLOCK), q_lo),
            pl.BlockSpec((1, 2, LANES, MOBA_BLOCK), q_hi),
            pl.BlockSpec((1, S, LANES), k_blk),
            pl.BlockSpec((S // INPROJ_TILE, SUBLANES, LANES), k_blk),
            pl.BlockSpec((1, nblk, 2, V_ROWS, MOBA_BLOCK), v_blk),
            pl.BlockSpec((TILE, 3 * W), cur_tile(0)),
            pl.BlockSpec((TILE, LANES), cur_tile(0)),
            pl.BlockSpec((TILE, W), prev_tile),
            pl.BlockSpec((1, W), lambda s: (0, 0)),
        ],
        out_specs=[
            pl.BlockSpec((1, 4, LANES, MOBA_BLOCK), o_blk),
            pl.BlockSpec((TILE, W), prev_tile),
        ],
        out_shape=[
            jax.ShapeDtypeStruct((B, nblk, ATT_WIDTH, MOBA_BLOCK), F32),
            jax.ShapeDtypeStruct((T, GDN_WIDTH), F32),
        ],
        scratch_shapes=[
            pltpu.VMEM((2, 2, MOBA_BLOCK, MOBA_BLOCK), F32),
            pltpu.VMEM((2, 2, 2 * nblk, MOBA_BLOCK), F32),
            pltpu.VMEM((2, 2, 2, LANES, MOBA_BLOCK), BF16),
            pltpu.VMEM((4, 2, nblk + 1, MOBA_BLOCK, MOBA_BLOCK), F32),
            pltpu.VMEM((4, 2, 2 * SUBLANES, MOBA_BLOCK), F32),
            pltpu.VMEM((2, TILE, W), F32),
            pltpu.VMEM((2, nchunk, 2 * GDN_CHUNK, W), BF16),
            pltpu.VMEM((2, TILE, W), BF16),
            pltpu.VMEM((2, TILE, W), BF16),
            pltpu.VMEM((2, nchunk * SUBLANES, W), F32),
            pltpu.VMEM((W // LANES, LANES, LANES), F32),
        ],
        compiler_params=pltpu.CompilerParams(
            dimension_semantics=("arbitrary",), vmem_limit_bytes=VMEM_LIMIT),
        name="token_mixers",
    )(rel_bias, qT, qT, k3, ksum, va, gqkv, gb, sz, nw_row)


def _out_mlp_kernel(x_ref, oTa_ref, oTb_ref, og_ref, woa_ref, wog_ref, pmn_ref, pre_ref, post_ref,
                    wup_ref, wdn_ref, out_ref):
    oT = jnp.concatenate([oTa_ref[0, 0], oTb_ref[0, 0]], axis=1)
    o_att = oT.T.astype(BF16)
    mix = jnp.dot(o_att, woa_ref[...], preferred_element_type=F32)
    mix = mix + jnp.dot(og_ref[...].astype(BF16), wog_ref[...], preferred_element_type=F32)
    x1 = x_ref[...] + _rms(mix, pmn_ref[...])
    h = _rms(x1, pre_ref[...]).astype(BF16)
    acc = jnp.zeros((ROW_TILE, D_MODEL), F32)
    for c in range(D_FF // FF_TILE):
        up = jnp.dot(h, wup_ref[:, c * FF_TILE:(c + 1) * FF_TILE], preferred_element_type=F32)
        act = jnp.square(jnp.maximum(up, 0.0)).astype(BF16)
        acc = acc + jnp.dot(act, wdn_ref[c * FF_TILE:(c + 1) * FF_TILE, :], preferred_element_type=F32)
    out_ref[...] = x1 + _rms(acc, post_ref[...])


def _out_mlp(xf, oT, og, woa, wog, pmn, pre, post, wup, wdn, B, S):
    T = B * S
    nblk = S // MOBA_BLOCK
    tiles_per_seq = S // ROW_TILE
    assert ROW_TILE == 2 * MOBA_BLOCK
    const = lambda i: (0, 0)
    row = lambda i: (i, 0)

    def att_block(which):
        def index(i):
            blk = 2 * (i % tiles_per_seq) + which
            return (i // tiles_per_seq, _paired_pos(blk, nblk), 0, 0)
        return index

    single = dict(pipeline_mode=pl.Buffered(1))
    return pl.pallas_call(
        _out_mlp_kernel,
        grid=(T // ROW_TILE,),
        in_specs=[
            pl.BlockSpec((ROW_TILE, D_MODEL), row),
            pl.BlockSpec((1, 1, ATT_WIDTH, MOBA_BLOCK), att_block(0)),
            pl.BlockSpec((1, 1, ATT_WIDTH, MOBA_BLOCK), att_block(1)),
            pl.BlockSpec((ROW_TILE, GDN_WIDTH), row),
            pl.BlockSpec(woa.shape, const, **single),
            pl.BlockSpec(wog.shape, const, **single),
            pl.BlockSpec((1, D_MODEL), const),
            pl.BlockSpec((1, D_MODEL), const),
            pl.BlockSpec((1, D_MODEL), const),
            pl.BlockSpec(wup.shape, const, **single),
            pl.BlockSpec(wdn.shape, const, **single),
        ],
        out_specs=pl.BlockSpec((ROW_TILE, D_MODEL), row),
        out_shape=jax.ShapeDtypeStruct((T, D_MODEL), F32),
        compiler_params=pltpu.CompilerParams(
            dimension_semantics=("arbitrary",), vmem_limit_bytes=VMEM_LIMIT),
        name="out_mlp",
    )(xf, oT, oT, og, woa, wog, pmn, pre, post, wup, wdn)


def kernel(x, w_in, w_out, conv_w, A_log, dt_bias, gdn_norm_w, rel_bias, pre_mix_norm,
           post_mix_norm, pre_mlp_norm, post_mlp_norm, w_up, w_down):
    B, S, D = x.shape
    assert D == D_MODEL and S % ROW_TILE == 0 and S % MOBA_BLOCK == 0
    T = B * S
    depth = w_in.shape[0]
    xf = x.reshape(T, D)
    o0, o1, o2, o3, o4 = 0, ATT_WIDTH, 2 * ATT_WIDTH, 3 * ATT_WIDTH, 3 * ATT_WIDTH + 3 * GDN_WIDTH
    o5 = o4 + GDN_WIDTH
    for l in range(depth):
        wi = w_in[l]
        wqT = wi[:, o0:o1].T.astype(BF16)
        wk = wi[:, o1:o2].astype(BF16)
        wvT = wi[:, o2:o3].T.astype(BF16)
        wg = wi[:, o3:o4].astype(BF16)
        wz = wi[:, o4:o5].astype(BF16)
        wab = jnp.pad(wi[:, o5:], ((0, 0), (0, LANES - 2 * GDN_HEADS))).astype(BF16)
        pad8 = lambda v: jnp.pad(v.astype(F32), (0, LANES - GDN_HEADS))[None, :]
        qT, k, ksum, va, gqkv, sz, gb = _inproj(xf, pre_mix_norm[l][None, :], wqT, wk, wvT, wg, wz, wab,
                                                conv_w[l], pad8(A_log[l]), pad8(dt_bias[l]), B, S)
        oT, og = _token_mixers(rel_bias.astype(F32), qT, k.reshape(B, S, ATT_WIDTH), ksum, va, gqkv, sz,
                               gb, jnp.tile(gdn_norm_w[l], GDN_HEADS)[None, :], B, S)
        wo = w_out[l].astype(BF16)
        xf = _out_mlp(xf, oT, og, wo[:ATT_WIDTH], wo[ATT_WIDTH:], post_mix_norm[l][None, :],
                      pre_mlp_norm[l][None, :], post_mlp_norm[l][None, :],
                      w_up[l].astype(BF16), w_down[l].astype(BF16), B, S)
    return xf.reshape(B, S, D)
```

```python
import functools
import math

import jax
import jax.numpy as jnp
from jax import lax
from jax.experimental import pallas as pl
from jax.experimental.pallas import tpu as pltpu

F32 = jnp.float32
BF16 = jnp.bfloat16
HI = lax.Precision.HIGHEST

D_MODEL = 1024
HEAD_DIM = 64
ATT_HEADS = 8
GDN_HEADS = 8
ATT_WIDTH = ATT_HEADS * HEAD_DIM
GDN_WIDTH = GDN_HEADS * HEAD_DIM
MOBA_BLOCK = 256
MOBA_TOPK = 3
GDN_CHUNK = 64
CONV_WIDTH = 4
D_FF = 4 * D_MODEL
REL_BUCKETS = 32
REL_MAX_EXACT = 16
REL_MAX_DIST = 128
EPS = 1e-6
NEG = -1e30
LOG2E = math.log2(math.e)

LANES = 128
SUBLANES = 8
VMEM_LIMIT = 56 * 1024 * 1024
ROW_TILE = 512
INPROJ_TILE = 512
FF_TILE = 1024

NT = (((1,), (1,)), ((), ()))
TN = (((0,), (0,)), ((), ()))


def _bucket_lower_bounds():
    def bucket(d):
        if d < REL_MAX_EXACT:
            return d
        t = math.log(d / REL_MAX_EXACT) / math.log(REL_MAX_DIST / REL_MAX_EXACT)
        t = t * (REL_BUCKETS - REL_MAX_EXACT)
        assert d in (REL_MAX_EXACT, REL_MAX_DIST) or abs(t - round(t)) > 1e-6
        return min(REL_MAX_EXACT + int(t + 1e-9), REL_BUCKETS - 1)
    lower = []
    for b in range(REL_BUCKETS):
        d = 0
        while bucket(d) < b:
            d += 1
        lower.append(d)
    return lower


BUCKET_LOWER = _bucket_lower_bounds()


def _sigmoid(x):
    return 0.5 * jnp.tanh(0.5 * x) + 0.5


def _silu_of_half(h):
    return h + h * jnp.tanh(h)


def _rms(x, w):
    return x * lax.rsqrt(jnp.mean(x * x, axis=-1, keepdims=True) + EPS) * w


def _split_bf16(x, parts):
    out = []
    for _ in range(parts):
        h = x.astype(BF16)
        out.append(h)
        x = x - h.astype(F32)
    return out


def _dot_split_rhs(c, x, parts):
    acc = None
    for h in _split_bf16(x, parts):
        d = jnp.dot(c, h, preferred_element_type=F32)
        acc = d if acc is None else acc + d
    return acc


CONV_COLS = 512
ROW_PITCH = 72


def _inproj_kernel(x_ref, xp_ref, nw_ref, wqT_ref, wk_ref, wvT_ref, wg_ref, wz_ref, wab_ref,
                   cw_ref, alog_ref, dtb_ref,
                   qT_ref, k_ref, ksum_ref, va_ref, g_ref, z_ref, gb_ref, hn_ref, y_ref, *, tiles_per_seq):
    TM = INPROJ_TILE
    NV = TM // SUBLANES
    hn = _rms(x_ref[...], nw_ref[...])
    h = hn.astype(BF16)

    z = jnp.dot(h, wz_ref[...], preferred_element_type=F32)
    z_ref[...] = _silu_of_half(0.5 * z)
    ab = jnp.dot(h, wab_ref[...], preferred_element_type=F32)
    xs = ab + dtb_ref[...]
    log_decay = -jnp.exp(alog_ref[...]) * (jnp.maximum(xs, 0.0) + jnp.log1p(jnp.exp(-jnp.abs(xs))))
    lane = lax.broadcasted_iota(jnp.int32, ab.shape, 1)
    gb_ref[...] = jnp.where(lane < GDN_HEADS, log_decay, _sigmoid(ab))
    k = jnp.dot(h, wk_ref[...], preferred_element_type=F32)
    k_ref[...] = k.astype(BF16)
    ksum_ref[...] = jnp.zeros(ksum_ref.shape, F32)
    for t in range(INPROJ_TILE // MOBA_BLOCK):
        ksum_ref[0, t:t + 1, :] = jnp.sum(k[t * MOBA_BLOCK:(t + 1) * MOBA_BLOCK], axis=0, keepdims=True)

    for j in range(D_MODEL // LANES):
        for b in range(SUBLANES):
            hn_ref[j, ROW_PITCH * b:ROW_PITCH * b + NV, :] = hn[NV * b:NV * (b + 1), LANES * j:LANES * (j + 1)]
    h_perm = jnp.concatenate(
        [jnp.concatenate([hn_ref[j, pl.ds(u, SUBLANES, stride=ROW_PITCH), :] for j in range(D_MODEL // LANES)],
                         axis=1)
         for u in range(NV)], axis=0).astype(BF16)
    hp = _rms(xp_ref[...], nw_ref[...]).astype(BF16)
    seq_start = (pl.program_id(0) % tiles_per_seq) == 0
    first_sublane = lax.broadcasted_iota(jnp.int32, (SUBLANES, CONV_COLS), 0) == 0
    for c in range(3 * GDN_WIDTH // CONV_COLS):
        cols = slice(c * CONV_COLS, (c + 1) * CONV_COLS)
        cur = jnp.dot(h_perm, wg_ref[:, cols], preferred_element_type=F32)
        prev8 = jnp.dot(hp, wg_ref[:, cols], preferred_element_type=F32)
        prev8 = jnp.where(seq_start, 0.0, prev8)
        wrap = []
        for i in range(CONV_WIDTH - 1):
            u = NV - (CONV_WIDTH - 1) + i
            ctx = prev8[SUBLANES - (CONV_WIDTH - 1) + i:SUBLANES - (CONV_WIDTH - 1) + i + 1]
            wrap.append(jnp.where(first_sublane, ctx, pltpu.roll(cur[SUBLANES * u:SUBLANES * (u + 1)], 1, 0)))
        cw_half = 0.5 * cw_ref[:, cols]
        acc = cur * cw_half[CONV_WIDTH - 1:CONV_WIDTH]
        for s in range(1, CONV_WIDTH):
            tap = jnp.concatenate(wrap[CONV_WIDTH - 1 - s:] + [cur[0:TM - SUBLANES * s]], axis=0)
            acc = acc + tap * cw_half[CONV_WIDTH - 1 - s:CONV_WIDTH - s]
        y = _silu_of_half(acc)
        for j in range(CONV_COLS // LANES):
            for u in range(NV):
                y_ref[c, j, pl.ds(u, SUBLANES, stride=ROW_PITCH), :] = y[SUBLANES * u:SUBLANES * (u + 1),
                                                                         LANES * j:LANES * (j + 1)]
        g_ref[:, cols] = jnp.concatenate(
            [jnp.concatenate([y_ref[c, j, ROW_PITCH * b:ROW_PITCH * b + NV, :] for b in range(SUBLANES)], axis=0)
             for j in range(CONV_COLS // LANES)], axis=1)

    qT = lax.dot_general(wqT_ref[...], h, NT, preferred_element_type=F32)
    vT = lax.dot_general(wvT_ref[...], h, NT, preferred_element_type=F32)
    ones_row = jnp.where(lax.broadcasted_iota(jnp.int32, (V_ROWS - HEAD_DIM, MOBA_BLOCK), 0) == 0,
                         1.0, 0.0).astype(BF16)
    for t in range(INPROJ_TILE // MOBA_BLOCK):
        blk = slice(t * MOBA_BLOCK, (t + 1) * MOBA_BLOCK)
        qT_ref[0, t] = qT[:, blk]
        for hh in range(ATT_HEADS):
            va_ref[0, t, hh, 0:HEAD_DIM, :] = vT[HEAD_DIM * hh:HEAD_DIM * (hh + 1), blk].astype(BF16)
            va_ref[0, t, hh, HEAD_DIM:V_ROWS, :] = ones_row


def _inproj(xf, nw, wqT, wk, wvT, wg, wz, wab, conv_w, alog_pad, dtb_pad, B, S):
    T = B * S
    TM = INPROJ_TILE
    assert S % TM == 0
    nblk = S // MOBA_BLOCK
    tiles_per_seq = S // TM
    blk_per_tile = TM // MOBA_BLOCK
    const = lambda i: (0, 0)
    row = lambda i: (i, 0)
    tr = lambda i: (i // tiles_per_seq, i % tiles_per_seq, 0, 0)
    prev_rows = lambda i: (jnp.maximum(i * (TM // SUBLANES) - 1, 0), 0)
    single = dict(pipeline_mode=pl.Buffered(1))
    return pl.pallas_call(
        functools.partial(_inproj_kernel, tiles_per_seq=tiles_per_seq),
        grid=(T // TM,),
        in_specs=[
            pl.BlockSpec((TM, D_MODEL), row),
            pl.BlockSpec((SUBLANES, D_MODEL), prev_rows),
            pl.BlockSpec((1, D_MODEL), const),
            pl.BlockSpec(wqT.shape, const, **single),
            pl.BlockSpec(wk.shape, const, **single),
            pl.BlockSpec(wvT.shape, const, **single),
            pl.BlockSpec(wg.shape, const, **single),
            pl.BlockSpec(wz.shape, const, **single),
            pl.BlockSpec(wab.shape, const, **single),
            pl.BlockSpec(conv_w.shape, const),
            pl.BlockSpec((1, LANES), const),
            pl.BlockSpec((1, LANES), const),
        ],
        out_specs=[
            pl.BlockSpec((1, blk_per_tile, ATT_WIDTH, MOBA_BLOCK), tr),
            pl.BlockSpec((TM, ATT_WIDTH), row),
            pl.BlockSpec((1, SUBLANES, ATT_WIDTH), lambda i: (i, 0, 0)),
            pl.BlockSpec((1, blk_per_tile, ATT_HEADS, V_ROWS, MOBA_BLOCK),
                         lambda i: (i // tiles_per_seq, i % tiles_per_seq, 0, 0, 0)),
            pl.BlockSpec((TM, 3 * GDN_WIDTH), row),
            pl.BlockSpec((TM, GDN_WIDTH), row),
            pl.BlockSpec((TM, LANES), row),
        ],
        out_shape=[
            jax.ShapeDtypeStruct((B, nblk, ATT_WIDTH, MOBA_BLOCK), F32),
            jax.ShapeDtypeStruct((T, ATT_WIDTH), BF16),
            jax.ShapeDtypeStruct((T // TM, SUBLANES, ATT_WIDTH), F32),
            jax.ShapeDtypeStruct((B, nblk, ATT_HEADS, V_ROWS, MOBA_BLOCK), BF16),
            jax.ShapeDtypeStruct((T, 3 * GDN_WIDTH), F32),
            jax.ShapeDtypeStruct((T, GDN_WIDTH), F32),
            jax.ShapeDtypeStruct((T, LANES), F32),
        ],
        scratch_shapes=[
            pltpu.VMEM((D_MODEL // LANES, SUBLANES * ROW_PITCH, LANES), F32),
            pltpu.VMEM((3 * GDN_WIDTH // CONV_COLS, CONV_COLS // LANES, SUBLANES * ROW_PITCH, LANES), F32),
        ],
        compiler_params=pltpu.CompilerParams(
            dimension_semantics=("arbitrary",), vmem_limit_bytes=VMEM_LIMIT),
        name="inproj",
    )(xf, xf, nw, wqT, wk, wvT, wg, wz, wab, conv_w, alog_pad, dtb_pad)


V_ROWS = HEAD_DIM + 16


def _paired_pos(i, nblk):
    return jnp.where(i < nblk // 2, 2 * i, 2 * (nblk - 1 - i) + 1)


GDN_TILE = 256
GDN_HALF = 2 * LANES


def _gdn_stages(y_ref, gb_ref, sz_ref, nw_ref, out_ref,
                u_ref, wq_ref, a_ref, kd_ref, gl_ref, st_ref):
    C = GDN_CHUNK
    W = GDN_WIDTH
    TILE = GDN_TILE
    npair = W // LANES

    r_w = lax.broadcasted_iota(jnp.int32, (GDN_HALF, GDN_HALF), 0)
    c_w = lax.broadcasted_iota(jnp.int32, (GDN_HALF, GDN_HALF), 1)
    head_ones = jnp.where((r_w // HEAD_DIM) == (c_w // HEAD_DIM), 1.0, 0.0).astype(BF16)
    ltri_bd = jnp.where(((r_w // C) == (c_w // C)) & (c_w <= r_w), 1.0, 0.0).astype(BF16)
    tok = lax.broadcasted_iota(jnp.int32, (TILE, W), 0) % C
    col = lax.broadcasted_iota(jnp.int32, (TILE, W), 1) % HEAD_DIM
    causal_t = tok >= col
    strict_t = tok > col
    lane_t = lax.broadcasted_iota(jnp.int32, (TILE, LANES), 1)

    lane = lax.broadcasted_iota(jnp.int32, (C, LANES), 1)
    rowi = lax.broadcasted_iota(jnp.int32, (C, LANES), 0)
    first_head = lane < HEAD_DIM
    strict = rowi > (lane % HEAD_DIM)
    eye2 = jnp.where(rowi == (lane % HEAD_DIM), 1.0, 0.0)
    lane2 = lax.broadcasted_iota(jnp.int32, (C, 2 * LANES), 1)
    first_head2 = (lane2 % LANES) < HEAD_DIM
    r_l = lax.broadcasted_iota(jnp.int32, (LANES, LANES), 0)
    c_l = lax.broadcasted_iota(jnp.int32, (LANES, LANES), 1)
    same_head = (r_l // HEAD_DIM) == (c_l // HEAD_DIM)
    pair_ones = jnp.where(same_head, 1.0, 0.0).astype(BF16)

    def stack(x, mask):
        return jnp.concatenate([jnp.where(mask, x, 0.0), jnp.where(mask, 0.0, x)], axis=0)

    dot = functools.partial(jnp.dot, preferred_element_type=F32)

    def head_sumsq(ys):
        halves = [(y * y).astype(BF16)[:, h:h + GDN_HALF] for y in ys for h in range(0, W, GDN_HALF)]
        sums = dot(jnp.concatenate(halves, axis=0), head_ones)
        per = W // GDN_HALF
        return [jnp.concatenate([sums[(i * per + j) * TILE:(i * per + j + 1) * TILE] for j in range(per)],
                                axis=1) for i in range(len(ys))]

    def solve_stages(slot):
        yq = y_ref[:, 0:W]
        yk = y_ref[:, W:2 * W]
        yv = y_ref[:, 2 * W:3 * W]
        ssq, ssk = head_sumsq([yq, yk])
        qn = yq * lax.rsqrt(ssq + EPS) * (HEAD_DIM ** -0.5)
        kn = yk * lax.rsqrt(ssk + EPS)
        yield
        gbt = gb_ref[...]

        def spread(col0):
            pairs = []
            for p in range(npair):
                a = jnp.broadcast_to(gbt[:, col0 + 2 * p:col0 + 2 * p + 1], (TILE, LANES))
                b = jnp.broadcast_to(gbt[:, col0 + 2 * p + 1:col0 + 2 * p + 2], (TILE, LANES))
                pairs.append(jnp.where(lane_t < HEAD_DIM, a, b))
            return jnp.concatenate(pairs, axis=1)

        g = spread(0)
        beta = spread(GDN_HEADS)
        gcd = _dot_split_rhs(ltri_bd, jnp.concatenate([g, jnp.where(strict_t, g, 0.0)], axis=1), 2)
        yield
        gc = gcd[:, :W]
        decay = jnp.where(causal_t, jnp.exp(jnp.where(causal_t, gcd[:, W:], 0.0)), 0.0)
        egc = jnp.exp(gc)
        kb = kn * beta
        rv = yv * beta
        rk = kb * egc
        qd = qn * egc
        for cc in range(TILE // C):
            rs = slice(cc * C, (cc + 1) * C)
            g_last = gc[(cc + 1) * C - 1:(cc + 1) * C, :]
            kd_ref[slot, rs, :] = (kn[rs] * jnp.exp(g_last - gc[rs])).astype(BF16)
            gl_ref[slot, cc * SUBLANES:(cc + 1) * SUBLANES, :] = (
                jnp.broadcast_to(jnp.exp(g_last), (SUBLANES, W)))
        units = [(slice(cc * C, (cc + 1) * C), slice(LANES * p, LANES * (p + 1)), cc)
                 for cc in range(TILE // C) for p in range(npair)]
        kqs = [lax.dot_general(jnp.concatenate([kn[rs, ls], qn[rs, ls]], axis=0).astype(BF16),
                               stack(kn[rs, ls], first_head).astype(BF16), NT,
                               preferred_element_type=F32) for rs, ls, _ in units]
        yield
        ps = [-jnp.where(strict, kq[0:C] * beta[rs, ls] * decay[rs, ls], 0.0)
              for kq, (rs, ls, _) in zip(kqs, units)]
        ss = [eye2 + p for p in ps]
        ps = [dot(p.astype(BF16), stack(p, first_head).astype(BF16)) for p in ps]
        yield
        nround = int(math.log2(C))
        for k in range(1, nround):
            rhs = [stack(s_, first_head).astype(BF16) for s_ in ss]
            if k + 1 < nround:
                rhs = [jnp.concatenate([stack(p, first_head).astype(BF16), sx], axis=1)
                       for p, sx in zip(ps, rhs)]
            outs = [dot(p.astype(BF16), sx) for p, sx in zip(ps, rhs)]
            if k + 1 < nround:
                ps = [o[:, :LANES] for o in outs]
                ss = [s_ + o[:, LANES:] for s_, o in zip(ss, outs)]
            else:
                ss = [s_ + o for s_, o in zip(ss, outs)]
            yield
        xs = [dot(s_.astype(BF16),
                  stack(jnp.concatenate([rv[rs, ls], rk[rs, ls]], axis=1), first_head2).astype(BF16))
              for s_, (rs, ls, _) in zip(ss, units)]
        yield
        for x, kq, (rs, ls, cc) in zip(xs, kqs, units):
            u_ref[slot, rs, ls] = x[:, :LANES]
            wq_ref[slot, cc, 0:C, ls] = x[:, LANES:].astype(BF16)
            wq_ref[slot, cc, C:2 * C, ls] = qd[rs, ls].astype(BF16)
            a_ref[slot, rs, ls] = (kq[C:2 * C] * decay[rs, ls]).astype(BF16)

    lss = [slice(LANES * p, LANES * (p + 1)) for p in range(npair)]

    def recurrence_stages(slot, seq_start):
        states = [jnp.where(seq_start, 0.0, st_ref[p]) for p in range(npair)]
        pending = None

        def finish(rs, os_):
            sq = jnp.concatenate([(o * o).astype(BF16) for o in os_], axis=0)
            ms_all = dot(sq, pair_ones) * (1.0 / HEAD_DIM)
            for p, (ls, o) in enumerate(zip(lss, os_)):
                ms = ms_all[p * C:(p + 1) * C]
                out_ref[rs, ls] = o * lax.rsqrt(ms + EPS) * nw_ref[:, ls] * sz_ref[rs, ls]

        for cc in range(TILE // C):
            rs = slice(cc * C, (cc + 1) * C)
            wqs = [dot(wq_ref[slot, cc, :, ls], st.astype(BF16)) for ls, st in zip(lss, states)]
            if pending is not None:
                finish(*pending)
            yield
            v_news = [u_ref[slot, rs, ls] - wq[0:C] for ls, wq in zip(lss, wqs)]
            kvs = [lax.dot_general(kd_ref[slot, rs, ls], v.astype(BF16), TN, preferred_element_type=F32)
                   for ls, v in zip(lss, v_news)]
            os_ = [wq[C:2 * C] + dot(a_ref[slot, rs, ls], stack(v, first_head).astype(BF16))
                   for ls, wq, v in zip(lss, wqs, v_news)]
            states = [st * gl_ref[slot, cc * SUBLANES:cc * SUBLANES + 1, ls] + jnp.where(same_head, kv, 0.0)
                      for ls, st, kv in zip(lss, states, kvs)]
            pending = (rs, os_)
            yield
        finish(*pending)
        for p in range(npair):
            st_ref[p] = states[p]

    return solve_stages, recurrence_stages


def _interleave_weighted(*gens_and_weights):
    live = [[gen, weight] for gen, weight in gens_and_weights]
    while live:
        for entry in list(live):
            for _ in range(entry[1]):
                try:
                    next(entry[0])
                except StopIteration:
                    live.remove(entry)
                    break


def _mixers_kernel(relb_ref, qlo_ref, qhi_ref, kb_ref, ksum_ref, va_ref,
                   y_ref, gb_ref, sz_ref, nw_ref,
                   oT_ref, og_ref,
                   bias_ref, addm_ref, qh_ref, lg_ref, moff_ref,
                   u_ref, wq_ref, a_ref, kd_ref, gl_ref, st_ref,
                   *, nblk, nbatch, tiles_per_seq, nsteps):
    step = pl.program_id(0)
    BLK = MOBA_BLOCK
    half = nblk // 2
    pair = jnp.minimum(step, nsteps - 2)
    hp = pair // (nbatch * (half // 2))
    b = (pair // (half // 2)) % nbatch
    scoring = step < nsteps - 1
    new_kv = (pair % (half // 2)) == 0

    solve_stages, recurrence_stages = _gdn_stages(
        y_ref, gb_ref, sz_ref, nw_ref, og_ref,
        u_ref, wq_ref, a_ref, kd_ref, gl_ref, st_ref)
    seq_start = ((step - 1) % tiles_per_seq) == 0

    @pl.when((b == 0) & new_kv & scoring)
    def _():
        kk = lax.broadcasted_iota(jnp.int32, (BLK, BLK), 0)
        qq = lax.broadcasted_iota(jnp.int32, (BLK, BLK), 1)
        for hh in range(2):
            h = 2 * hp + hh
            for kind in range(2):
                d = qq - kk + kind * BLK
                val = jnp.full((BLK, BLK), relb_ref[h, REL_BUCKETS - 1], F32)
                for bkt in range(REL_BUCKETS - 2, -1, -1):
                    val = jnp.where(d < BUCKET_LOWER[bkt + 1], relb_ref[h, bkt], val)
                val = val * LOG2E
                if kind == 0:
                    val = jnp.where(d >= 0, val, NEG)
                bias_ref[hh, kind] = val

    def key_means():
        per_tile = INPROJ_TILE // BLK
        ks = ksum_ref[...]
        km = jnp.concatenate([ks[j // per_tile, j % per_tile:j % per_tile + 1, :] for j in range(nblk)],
                             axis=0) * (1.0 / BLK)
        lane = lax.broadcasted_iota(jnp.int32, (nblk, LANES), 1)
        return jnp.concatenate([jnp.where(lane < HEAD_DIM, km, 0.0),
                                jnp.where(lane >= HEAD_DIM, km, 0.0)], axis=0)

    def item_tiles(t):
        i_hi = nblk - 1 - t
        tiles = [(0, t, "own"), (1, i_hi, "own"), (1, i_hi - 1, "prev")]
        if t >= 1:
            tiles.append((0, t - 1, "prev"))
        tiles += [(0, j, "far") for j in range(t - 1)]
        tiles += [(1, j, "far") for j in range(i_hi - 1)]
        assert len(tiles) == nblk + 1
        return tiles

    def score_stages(parity):
        ridx = lax.broadcasted_iota(jnp.int32, (nblk, BLK), 0)
        sub = lax.broadcasted_iota(jnp.int32, (LANES, BLK), 0)
        scale = HEAD_DIM ** -0.5 * LOG2E
        km = key_means()
        for e in range(2):
            t = 2 * parity + e
            slot = 2 * parity + e
            q_of = ((qlo_ref, e, t), (qhi_ref, 1 - e, nblk - 1 - t))
            for s, (q_ref, w, qi) in enumerate(q_of):
                qT = q_ref[0, w]
                gT = jnp.dot(km, qT, precision=HI, preferred_element_type=F32)
                past = ridx < qi
                for hh in range(2):
                    gm = jnp.where(past, gT[nblk * hh:nblk * (hh + 1)], -jnp.inf)
                    cnt = jnp.zeros((nblk, BLK), F32)
                    for jp in range(nblk):
                        row = gm[jp:jp + 1, :]
                        beats = (row > gm) | ((row == gm) & (ridx > jp))
                        cnt = cnt + jnp.where(beats, 1.0, 0.0)
                    visible = past & (cnt < MOBA_TOPK)
                    addm_ref[e, s, nblk * hh:nblk * (hh + 1), :] = jnp.where(visible, 0.0, NEG)
                    in_head = (sub >= HEAD_DIM * hh) & (sub < HEAD_DIM * (hh + 1))
                    qh_ref[e, s, hh] = jnp.where(in_head, qT * scale, 0.0).astype(BF16)
            yield
            tiles = item_tiles(t)
            for hh in range(2):
                cmax = {0: [], 1: []}
                offs = []
                for n, (s, kblk, cls) in enumerate(tiles):
                    lg = jnp.dot(kb_ref[0, kblk * BLK:(kblk + 1) * BLK, :], qh_ref[e, s, hh],
                                 preferred_element_type=F32)
                    if cls != "far":
                        lg = lg + bias_ref[hh, 0 if cls == "own" else 1]
                    lg_ref[slot, hh, n] = lg
                    cm = jnp.max(lg, axis=0, keepdims=True)
                    off = None
                    if cls != "own":
                        off = addm_ref[e, s, nblk * hh + kblk:nblk * hh + kblk + 1, :]
                        if cls == "far":
                            off = off + relb_ref[2 * hp + hh, REL_BUCKETS - 1] * LOG2E
                        cm = cm + off
                    cmax[s].append(cm)
                    offs.append(off)
                    yield
                m = {s: functools.reduce(jnp.maximum, cmax[s]) for s in (0, 1)}
                for n, (s, _, _) in enumerate(tiles):
                    moff_ref[slot, hh, n:n + 1, :] = m[s] if offs[n] is None else m[s] - offs[n]

    def softmax_pv_stages(parity):
        for e in range(2):
            t = 2 * parity + e
            slot = 2 * parity + e
            tiles = item_tiles(t)
            for hh in range(2):
                acc = {0: None, 1: None}
                for n, (s, kblk, _) in enumerate(tiles):
                    p = jnp.exp2(lg_ref[slot, hh, n] - moff_ref[slot, hh, n:n + 1, :])
                    pvn = jnp.dot(va_ref[0, kblk, hh], p.astype(BF16),
                                  preferred_element_type=F32)
                    acc[s] = pvn if acc[s] is None else acc[s] + pvn
                    yield
                for s in (0, 1):
                    oT_ref[0, 2 * e + s, HEAD_DIM * hh:HEAD_DIM * (hh + 1), :] = (
                        acc[s][0:HEAD_DIM] / acc[s][HEAD_DIM:HEAD_DIM + 1])

    ATT, GDN = 2, 1

    @pl.when(step == 0)
    def _():
        st_ref[...] = jnp.zeros(st_ref.shape, F32)
        _interleave_weighted((score_stages(0), ATT), (solve_stages(0), GDN))

    for parity in range(2):
        @pl.when((step > 0) & (step < nsteps - 1) & (step % 2 == parity))
        def _(parity=parity):
            _interleave_weighted((recurrence_stages(1 - parity, seq_start), GDN),
                                 (score_stages(parity), ATT),
                                 (solve_stages(parity), GDN),
                                 (softmax_pv_stages(1 - parity), ATT))

    @pl.when(step == nsteps - 1)
    def _():
        last = (nsteps - 2) % 2
        _interleave_weighted((recurrence_stages(last, seq_start), GDN), (softmax_pv_stages(last), ATT))


def _token_mixers(rel_bias, qT, k3, ksum, va, gqkv, sz, gb, nw_row, B, S):
    T = B * S
    W = GDN_WIDTH
    TILE = GDN_TILE
    nblk = S // MOBA_BLOCK
    half = nblk // 2
    assert TILE == GDN_HALF and S % TILE == 0
    assert BUCKET_LOWER[REL_BUCKETS - 1] <= MOBA_BLOCK + 1
    assert nblk == 8 and nblk + 1 <= 2 * SUBLANES
    nchunk = TILE // GDN_CHUNK
    ntiles = T // TILE
    npairs = (ATT_HEADS // 2) * B * (half // 2)
    assert npairs == ntiles
    nsteps = ntiles + 1

    def scored(s):
        p = jnp.minimum(s, npairs - 1)
        return p // (B * (half // 2)), (p // (half // 2)) % B, p % (half // 2)

    def done(s):
        p = jnp.maximum(s - 1, 0)
        return p // (B * (half // 2)), (p // (half // 2)) % B, p % (half // 2)

    def q_lo(s):
        hp, b, m = scored(s)
        return (b, m, hp, 0)

    def q_hi(s):
        hp, b, m = scored(s)
        return (b, half - 1 - m, hp, 0)

    def k_blk(s):
        hp, b, _ = scored(s)
        return (b, 0, hp)

    def v_blk(s):
        hp, b, _ = done(s)
        return (b, 0, hp, 0, 0)

    def o_blk(s):
        hp, b, m = done(s)
        return (b, m, hp, 0)

    cur_tile = lambda off: (lambda s: (jnp.minimum(s, ntiles - 1), off))
    prev_tile = lambda s: (jnp.maximum(s - 1, 0), 0)
    return pl.pallas_call(
        functools.partial(_mixers_kernel, nblk=nblk, nbatch=B, tiles_per_seq=S // TILE, nsteps=nsteps),
        grid=(nsteps,),
        in_specs=[
            pl.BlockSpec(memory_space=pltpu.SMEM),
            pl.BlockSpec((1, 2, LANES, MOBA_BLOCK), q_lo),
            pl.BlockSpec((1, 2, LANES, MOBA_BLOCK), q_hi),
            pl.BlockSpec((1, S, LANES), k_blk),
            pl.BlockSpec((S // INPROJ_TILE, SUBLANES, LANES), k_blk),
            pl.BlockSpec((1, nblk, 2, V_ROWS, MOBA_BLOCK), v_blk),
            pl.BlockSpec((TILE, 3 * W), cur_tile(0)),
            pl.BlockSpec((TILE, LANES), cur_tile(0)),
            pl.BlockSpec((TILE, W), prev_tile),
            pl.BlockSpec((1, W), lambda s: (0, 0)),
        ],
        out_specs=[
            pl.BlockSpec((1, 4, LANES, MOBA_BLOCK), o_blk),
            pl.BlockSpec((TILE, W), prev_tile),
        ],
        out_shape=[
            jax.ShapeDtypeStruct((B, nblk, ATT_WIDTH, MOBA_BLOCK), F32),
            jax.ShapeDtypeStruct((T, GDN_WIDTH), F32),
        ],
        scratch_shapes=[
            pltpu.VMEM((2, 2, MOBA_BLOCK, MOBA_BLOCK), F32),
            pltpu.VMEM((2, 2, 2 * nblk, MOBA_BLOCK), F32),
            pltpu.VMEM((2, 2, 2, LANES, MOBA_BLOCK), BF16),
            pltpu.VMEM((4, 2, nblk + 1, MOBA_BLOCK, MOBA_BLOCK), F32),
            pltpu.VMEM((4, 2, 2 * SUBLANES, MOBA_BLOCK), F32),
            pltpu.VMEM((2, TILE, W), F32),
            pltpu.VMEM((2, nchunk, 2 * GDN_CHUNK, W), BF16),
            pltpu.VMEM((2, TILE, W), BF16),
            pltpu.VMEM((2, TILE, W), BF16),
            pltpu.VMEM((2, nchunk * SUBLANES, W), F32),
            pltpu.VMEM((W // LANES, LANES, LANES), F32),
        ],
        compiler_params=pltpu.CompilerParams(
            dimension_semantics=("arbitrary",), vmem_limit_bytes=VMEM_LIMIT),
        name="token_mixers",
    )(rel_bias, qT, qT, k3, ksum, va, gqkv, gb, sz, nw_row)


def _out_mlp_kernel(x_ref, oTa_ref, oTb_ref, og_ref, woa_ref, wog_ref, pmn_ref, pre_ref, post_ref,
                    wup_ref, wdn_ref, out_ref):
    oT = jnp.concatenate([oTa_ref[0, 0], oTb_ref[0, 0]], axis=1)
    o_att = oT.T.astype(BF16)
    mix = jnp.dot(o_att, woa_ref[...], preferred_element_type=F32)
    mix = mix + jnp.dot(og_ref[...].astype(BF16), wog_ref[...], preferred_element_type=F32)
    x1 = x_ref[...] + _rms(mix, pmn_ref[...])
    h = _rms(x1, pre_ref[...]).astype(BF16)
    acc = jnp.zeros((ROW_TILE, D_MODEL), F32)
    for c in range(D_FF // FF_TILE):
        up = jnp.dot(h, wup_ref[:, c * FF_TILE:(c + 1) * FF_TILE], preferred_element_type=F32)
        act = jnp.square(jnp.maximum(up, 0.0)).astype(BF16)
        acc = acc + jnp.dot(act, wdn_ref[c * FF_TILE:(c + 1) * FF_TILE, :], preferred_element_type=F32)
    out_ref[...] = x1 + _rms(acc, post_ref[...])


def _out_mlp(xf, oT, og, woa, wog, pmn, pre, post, wup, wdn, B, S):
    T = B * S
    nblk = S // MOBA_BLOCK
    tiles_per_seq = S // ROW_TILE
    assert ROW_TILE == 2 * MOBA_BLOCK
    const = lambda i: (0, 0)
    row = lambda i: (i, 0)

    def att_block(which):
        def index(i):
            blk = 2 * (i % tiles_per_seq) + which
            return (i // tiles_per_seq, _paired_pos(blk, nblk), 0, 0)
        return index

    single = dict(pipeline_mode=pl.Buffered(1))
    return pl.pallas_call(
        _out_mlp_kernel,
        grid=(T // ROW_TILE,),
        in_specs=[
            pl.BlockSpec((ROW_TILE, D_MODEL), row),
            pl.BlockSpec((1, 1, ATT_WIDTH, MOBA_BLOCK), att_block(0)),
            pl.BlockSpec((1, 1, ATT_WIDTH, MOBA_BLOCK), att_block(1)),
            pl.BlockSpec((ROW_TILE, GDN_WIDTH), row),
            pl.BlockSpec(woa.shape, const, **single),
            pl.BlockSpec(wog.shape, const, **single),
            pl.BlockSpec((1, D_MODEL), const),
            pl.BlockSpec((1, D_MODEL), const),
            pl.BlockSpec((1, D_MODEL), const),
            pl.BlockSpec(wup.shape, const, **single),
            pl.BlockSpec(wdn.shape, const, **single),
        ],
        out_specs=pl.BlockSpec((ROW_TILE, D_MODEL), row),
        out_shape=jax.ShapeDtypeStruct((T, D_MODEL), F32),
        compiler_params=pltpu.CompilerParams(
            dimension_semantics=("arbitrary",), vmem_limit_bytes=VMEM_LIMIT),
        name="out_mlp",
    )(xf, oT, oT, og, woa, wog, pmn, pre, post, wup, wdn)


def kernel(x, w_in, w_out, conv_w, A_log, dt_bias, gdn_norm_w, rel_bias, pre_mix_norm,
           post_mix_norm, pre_mlp_norm, post_mlp_norm, w_up, w_down):
    B, S, D = x.shape
    assert D == D_MODEL and S % ROW_TILE == 0 and S % MOBA_BLOCK == 0
    T = B * S
    depth = w_in.shape[0]
    xf = x.reshape(T, D)
    o0, o1, o2, o3, o4 = 0, ATT_WIDTH, 2 * ATT_WIDTH, 3 * ATT_WIDTH, 3 * ATT_WIDTH + 3 * GDN_WIDTH
    o5 = o4 + GDN_WIDTH
    for l in range(depth):
        wi = w_in[l]
        wqT = wi[:, o0:o1].T.astype(BF16)
        wk = wi[:, o1:o2].astype(BF16)
        wvT = wi[:, o2:o3].T.astype(BF16)
        wg = wi[:, o3:o4].astype(BF16)
        wz = wi[:, o4:o5].astype(BF16)
        wab = jnp.pad(wi[:, o5:], ((0, 0), (0, LANES - 2 * GDN_HEADS))).astype(BF16)
        pad8 = lambda v: jnp.pad(v.astype(F32), (0, LANES - GDN_HEADS))[None, :]
        qT, k, ksum, va, gqkv, sz, gb = _inproj(xf, pre_mix_norm[l][None, :], wqT, wk, wvT, wg, wz, wab,
                                                conv_w[l], pad8(A_log[l]), pad8(dt_bias[l]), B, S)
        oT, og = _token_mixers(rel_bias.astype(F32), qT, k.reshape(B, S, ATT_WIDTH), ksum, va, gqkv, sz,
                               gb, jnp.tile(gdn_norm_w[l], GDN_HEADS)[None, :], B, S)
        wo = w_out[l].astype(BF16)
        xf = _out_mlp(xf, oT, og, wo[:ATT_WIDTH], wo[ATT_WIDTH:], post_mix_norm[l][None, :],
                      pre_mlp_norm[l][None, :], post_mlp_norm[l][None, :],
                      w_up[l].astype(BF16), w_down[l].astype(BF16), B, S)
    return xf.reshape(B, S, D)
```

```python
import functools
import math

import jax
import jax.numpy as jnp
from jax import lax
from jax.experimental import pallas as pl
from jax.experimental.pallas import tpu as pltpu

F32 = jnp.float32
BF16 = jnp.bfloat16
HI = lax.Precision.HIGHEST

D_MODEL = 1024
HEAD_DIM = 64
ATT_HEADS = 8
GDN_HEADS = 8
ATT_WIDTH = ATT_HEADS * HEAD_DIM
GDN_WIDTH = GDN_HEADS * HEAD_DIM
MOBA_BLOCK = 256
MOBA_TOPK = 3
GDN_CHUNK = 64
CONV_WIDTH = 4
D_FF = 4 * D_MODEL
REL_BUCKETS = 32
REL_MAX_EXACT = 16
REL_MAX_DIST = 128
EPS = 1e-6
NEG = -1e30
LOG2E = math.log2(math.e)

LANES = 128
SUBLANES = 8
VMEM_LIMIT = 56 * 1024 * 1024
ROW_TILE = 512
INPROJ_TILE = 512
FF_TILE = 1024

NT = (((1,), (1,)), ((), ()))
TN = (((0,), (0,)), ((), ()))


def _bucket_lower_bounds():
    def bucket(d):
        if d < REL_MAX_EXACT:
            return d
        t = math.log(d / REL_MAX_EXACT) / math.log(REL_MAX_DIST / REL_MAX_EXACT)
        t = t * (REL_BUCKETS - REL_MAX_EXACT)
        assert d in (REL_MAX_EXACT, REL_MAX_DIST) or abs(t - round(t)) > 1e-6
        return min(REL_MAX_EXACT + int(t + 1e-9), REL_BUCKETS - 1)
    lower = []
    for b in range(REL_BUCKETS):
        d = 0
        while bucket(d) < b:
            d += 1
        lower.append(d)
    return lower


BUCKET_LOWER = _bucket_lower_bounds()


def _sigmoid(x):
    return 0.5 * jnp.tanh(0.5 * x) + 0.5


def _silu_of_half(h):
    return h + h * jnp.tanh(h)


def _rms(x, w):
    return x * lax.rsqrt(jnp.mean(x * x, axis=-1, keepdims=True) + EPS) * w


def _split_bf16(x, parts):
    out = []
    for _ in range(parts):
        h = x.astype(BF16)
        out.append(h)
        x = x - h.astype(F32)
    return out


def _dot_split_rhs(c, x, parts):
    acc = None
    for h in _split_bf16(x, parts):
        d = jnp.dot(c, h, preferred_element_type=F32)
        acc = d if acc is None else acc + d
    return acc


CONV_COLS = 512
ROW_PITCH = 72


def _inproj_kernel(x_ref, xp_ref, nw_ref, wqT_ref, wk_ref, wvT_ref, wg_ref, wz_ref, wab_ref,
                   cw_ref, alog_ref, dtb_ref,
                   qT_ref, k_ref, ksum_ref, va_ref, g_ref, z_ref, gb_ref, hn_ref, y_ref, *, tiles_per_seq):
    TM = INPROJ_TILE
    NV = TM // SUBLANES
    hn = _rms(x_ref[...], nw_ref[...])
    h = hn.astype(BF16)

    z = jnp.dot(h, wz_ref[...], preferred_element_type=F32)
    z_ref[...] = _silu_of_half(0.5 * z)
    ab = jnp.dot(h, wab_ref[...], preferred_element_type=F32)
    xs = ab + dtb_ref[...]
    log_decay = -jnp.exp(alog_ref[...]) * (jnp.maximum(xs, 0.0) + jnp.log1p(jnp.exp(-jnp.abs(xs))))
    lane = lax.broadcasted_iota(jnp.int32, ab.shape, 1)
    gb_ref[...] = jnp.where(lane < GDN_HEADS, log_decay, _sigmoid(ab))
    k = jnp.dot(h, wk_ref[...], preferred_element_type=F32)
    k_ref[...] = k.astype(BF16)
    ksum_ref[...] = jnp.zeros(ksum_ref.shape, F32)
    for t in range(INPROJ_TILE // MOBA_BLOCK):
        ksum_ref[0, t:t + 1, :] = jnp.sum(k[t * MOBA_BLOCK:(t + 1) * MOBA_BLOCK], axis=0, keepdims=True)

    qT = lax.dot_general(wqT_ref[...], h, NT, preferred_element_type=F32)
    vT = lax.dot_general(wvT_ref[...], h, NT, preferred_element_type=F32)
    ones_row = jnp.where(lax.broadcasted_iota(jnp.int32, (V_ROWS - HEAD_DIM, MOBA_BLOCK), 0) == 0,
                         1.0, 0.0).astype(BF16)
    for t in range(INPROJ_TILE // MOBA_BLOCK):
        blk = slice(t * MOBA_BLOCK, (t + 1) * MOBA_BLOCK)
        qT_ref[0, t] = qT[:, blk]
        for hh in range(ATT_HEADS):
            va_ref[0, t, hh, 0:HEAD_DIM, :] = vT[HEAD_DIM * hh:HEAD_DIM * (hh + 1), blk].astype(BF16)
            va_ref[0, t, hh, HEAD_DIM:V_ROWS, :] = ones_row

    for j in range(D_MODEL // LANES):
        for b in range(SUBLANES):
            hn_ref[j, ROW_PITCH * b:ROW_PITCH * b + NV, :] = hn[NV * b:NV * (b + 1), LANES * j:LANES * (j + 1)]
    h_perm = jnp.concatenate(
        [jnp.concatenate([hn_ref[j, pl.ds(u, SUBLANES, stride=ROW_PITCH), :] for j in range(D_MODEL // LANES)],
                         axis=1)
         for u in range(NV)], axis=0).astype(BF16)
    hp = _rms(xp_ref[...], nw_ref[...]).astype(BF16)
    seq_start = (pl.program_id(0) % tiles_per_seq) == 0
    first_sublane = lax.broadcasted_iota(jnp.int32, (SUBLANES, CONV_COLS), 0) == 0
    for c in range(3 * GDN_WIDTH // CONV_COLS):
        cols = slice(c * CONV_COLS, (c + 1) * CONV_COLS)
        cur = jnp.dot(h_perm, wg_ref[:, cols], preferred_element_type=F32)
        prev8 = jnp.dot(hp, wg_ref[:, cols], preferred_element_type=F32)
        prev8 = jnp.where(seq_start, 0.0, prev8)
        wrap = []
        for i in range(CONV_WIDTH - 1):
            u = NV - (CONV_WIDTH - 1) + i
            ctx = prev8[SUBLANES - (CONV_WIDTH - 1) + i:SUBLANES - (CONV_WIDTH - 1) + i + 1]
            wrap.append(jnp.where(first_sublane, ctx, pltpu.roll(cur[SUBLANES * u:SUBLANES * (u + 1)], 1, 0)))
        cw_half = 0.5 * cw_ref[:, cols]
        acc = cur * cw_half[CONV_WIDTH - 1:CONV_WIDTH]
        for s in range(1, CONV_WIDTH):
            tap = jnp.concatenate(wrap[CONV_WIDTH - 1 - s:] + [cur[0:TM - SUBLANES * s]], axis=0)
            acc = acc + tap * cw_half[CONV_WIDTH - 1 - s:CONV_WIDTH - s]
        y = _silu_of_half(acc)
        for j in range(CONV_COLS // LANES):
            for u in range(NV):
                y_ref[c, j, pl.ds(u, SUBLANES, stride=ROW_PITCH), :] = y[SUBLANES * u:SUBLANES * (u + 1),
                                                                         LANES * j:LANES * (j + 1)]
        g_ref[:, cols] = jnp.concatenate(
            [jnp.concatenate([y_ref[c, j, ROW_PITCH * b:ROW_PITCH * b + NV, :] for b in range(SUBLANES)], axis=0)
             for j in range(CONV_COLS // LANES)], axis=1)


def _inproj(xf, nw, wqT, wk, wvT, wg, wz, wab, conv_w, alog_pad, dtb_pad, B, S):
    T = B * S
    TM = INPROJ_TILE
    assert S % TM == 0
    nblk = S // MOBA_BLOCK
    tiles_per_seq = S // TM
    blk_per_tile = TM // MOBA_BLOCK
    const = lambda i: (0, 0)
    row = lambda i: (i, 0)
    tr = lambda i: (i // tiles_per_seq, i % tiles_per_seq, 0, 0)
    prev_rows = lambda i: (jnp.maximum(i * (TM // SUBLANES) - 1, 0), 0)
    single = dict(pipeline_mode=pl.Buffered(1))
    return pl.pallas_call(
        functools.partial(_inproj_kernel, tiles_per_seq=tiles_per_seq),
        grid=(T // TM,),
        in_specs=[
            pl.BlockSpec((TM, D_MODEL), row),
            pl.BlockSpec((SUBLANES, D_MODEL), prev_rows),
            pl.BlockSpec((1, D_MODEL), const),
            pl.BlockSpec(wqT.shape, const, **single),
            pl.BlockSpec(wk.shape, const, **single),
            pl.BlockSpec(wvT.shape, const, **single),
            pl.BlockSpec(wg.shape, const, **single),
            pl.BlockSpec(wz.shape, const, **single),
            pl.BlockSpec(wab.shape, const, **single),
            pl.BlockSpec(conv_w.shape, const),
            pl.BlockSpec((1, LANES), const),
            pl.BlockSpec((1, LANES), const),
        ],
        out_specs=[
            pl.BlockSpec((1, blk_per_tile, ATT_WIDTH, MOBA_BLOCK), tr),
            pl.BlockSpec((TM, ATT_WIDTH), row),
            pl.BlockSpec((1, SUBLANES, ATT_WIDTH), lambda i: (i, 0, 0)),
            pl.BlockSpec((1, blk_per_tile, ATT_HEADS, V_ROWS, MOBA_BLOCK),
                         lambda i: (i // tiles_per_seq, i % tiles_per_seq, 0, 0, 0)),
            pl.BlockSpec((TM, 3 * GDN_WIDTH), row),
            pl.BlockSpec((TM, GDN_WIDTH), row),
            pl.BlockSpec((TM, LANES), row),
        ],
        out_shape=[
            jax.ShapeDtypeStruct((B, nblk, ATT_WIDTH, MOBA_BLOCK), F32),
            jax.ShapeDtypeStruct((T, ATT_WIDTH), BF16),
            jax.ShapeDtypeStruct((T // TM, SUBLANES, ATT_WIDTH), F32),
            jax.ShapeDtypeStruct((B, nblk, ATT_HEADS, V_ROWS, MOBA_BLOCK), BF16),
            jax.ShapeDtypeStruct((T, 3 * GDN_WIDTH), F32),
            jax.ShapeDtypeStruct((T, GDN_WIDTH), F32),
            jax.ShapeDtypeStruct((T, LANES), F32),
        ],
        scratch_shapes=[
            pltpu.VMEM((D_MODEL // LANES, SUBLANES * ROW_PITCH, LANES), F32),
            pltpu.VMEM((3 * GDN_WIDTH // CONV_COLS, CONV_COLS // LANES, SUBLANES * ROW_PITCH, LANES), F32),
        ],
        compiler_params=pltpu.CompilerParams(
            dimension_semantics=("arbitrary",), vmem_limit_bytes=VMEM_LIMIT),
        name="inproj",
    )(xf, xf, nw, wqT, wk, wvT, wg, wz, wab, conv_w, alog_pad, dtb_pad)


V_ROWS = HEAD_DIM + 16


def _paired_pos(i, nblk):
    return jnp.where(i < nblk // 2, 2 * i, 2 * (nblk - 1 - i) + 1)


GDN_TILE = 256
GDN_HALF = 2 * LANES


def _gdn_stages(y_ref, gb_ref, sz_ref, nw_ref, out_ref,
                u_ref, wq_ref, a_ref, kd_ref, gl_ref, st_ref):
    C = GDN_CHUNK
    W = GDN_WIDTH
    TILE = GDN_TILE
    npair = W // LANES

    r_w = lax.broadcasted_iota(jnp.int32, (GDN_HALF, GDN_HALF), 0)
    c_w = lax.broadcasted_iota(jnp.int32, (GDN_HALF, GDN_HALF), 1)
    head_ones = jnp.where((r_w // HEAD_DIM) == (c_w // HEAD_DIM), 1.0, 0.0).astype(BF16)
    ltri_bd = jnp.where(((r_w // C) == (c_w // C)) & (c_w <= r_w), 1.0, 0.0).astype(BF16)
    tok = lax.broadcasted_iota(jnp.int32, (TILE, W), 0) % C
    col = lax.broadcasted_iota(jnp.int32, (TILE, W), 1) % HEAD_DIM
    causal_t = tok >= col
    strict_t = tok > col
    lane_t = lax.broadcasted_iota(jnp.int32, (TILE, LANES), 1)

    lane = lax.broadcasted_iota(jnp.int32, (C, LANES), 1)
    rowi = lax.broadcasted_iota(jnp.int32, (C, LANES), 0)
    first_head = lane < HEAD_DIM
    strict = rowi > (lane % HEAD_DIM)
    eye2 = jnp.where(rowi == (lane % HEAD_DIM), 1.0, 0.0)
    lane2 = lax.broadcasted_iota(jnp.int32, (C, 2 * LANES), 1)
    first_head2 = (lane2 % LANES) < HEAD_DIM
    r_l = lax.broadcasted_iota(jnp.int32, (LANES, LANES), 0)
    c_l = lax.broadcasted_iota(jnp.int32, (LANES, LANES), 1)
    same_head = (r_l // HEAD_DIM) == (c_l // HEAD_DIM)
    pair_ones = jnp.where(same_head, 1.0, 0.0).astype(BF16)

    def stack(x, mask):
        return jnp.concatenate([jnp.where(mask, x, 0.0), jnp.where(mask, 0.0, x)], axis=0)

    dot = functools.partial(jnp.dot, preferred_element_type=F32)

    def head_sumsq(ys):
        halves = [(y * y).astype(BF16)[:, h:h + GDN_HALF] for y in ys for h in range(0, W, GDN_HALF)]
        sums = dot(jnp.concatenate(halves, axis=0), head_ones)
        per = W // GDN_HALF
        return [jnp.concatenate([sums[(i * per + j) * TILE:(i * per + j + 1) * TILE] for j in range(per)],
                                axis=1) for i in range(len(ys))]

    def solve_stages(slot):
        yq = y_ref[:, 0:W]
        yk = y_ref[:, W:2 * W]
        yv = y_ref[:, 2 * W:3 * W]
        ssq, ssk = head_sumsq([yq, yk])
        qn = yq * lax.rsqrt(ssq + EPS) * (HEAD_DIM ** -0.5)
        kn = yk * lax.rsqrt(ssk + EPS)
        yield
        gbt = gb_ref[...]

        def spread(col0):
            pairs = []
            for p in range(npair):
                a = jnp.broadcast_to(gbt[:, col0 + 2 * p:col0 + 2 * p + 1], (TILE, LANES))
                b = jnp.broadcast_to(gbt[:, col0 + 2 * p + 1:col0 + 2 * p + 2], (TILE, LANES))
                pairs.append(jnp.where(lane_t < HEAD_DIM, a, b))
            return jnp.concatenate(pairs, axis=1)

        g = spread(0)
        beta = spread(GDN_HEADS)
        gcd = _dot_split_rhs(ltri_bd, jnp.concatenate([g, jnp.where(strict_t, g, 0.0)], axis=1), 2)
        yield
        gc = gcd[:, :W]
        decay = jnp.where(causal_t, jnp.exp(jnp.where(causal_t, gcd[:, W:], 0.0)), 0.0)
        egc = jnp.exp(gc)
        kb = kn * beta
        rv = yv * beta
        rk = kb * egc
        qd = qn * egc
        for cc in range(TILE // C):
            rs = slice(cc * C, (cc + 1) * C)
            g_last = gc[(cc + 1) * C - 1:(cc + 1) * C, :]
            kd_ref[slot, rs, :] = (kn[rs] * jnp.exp(g_last - gc[rs])).astype(BF16)
            gl_ref[slot, cc * SUBLANES:(cc + 1) * SUBLANES, :] = (
                jnp.broadcast_to(jnp.exp(g_last), (SUBLANES, W)))
        units = [(slice(cc * C, (cc + 1) * C), slice(LANES * p, LANES * (p + 1)), cc)
                 for cc in range(TILE // C) for p in range(npair)]
        kqs = [lax.dot_general(jnp.concatenate([kn[rs, ls], qn[rs, ls]], axis=0).astype(BF16),
                               stack(kn[rs, ls], first_head).astype(BF16), NT,
                               preferred_element_type=F32) for rs, ls, _ in units]
        yield
        ps = [-jnp.where(strict, kq[0:C] * beta[rs, ls] * decay[rs, ls], 0.0)
              for kq, (rs, ls, _) in zip(kqs, units)]
        ss = [eye2 + p for p in ps]
        ps = [dot(p.astype(BF16), stack(p, first_head).astype(BF16)) for p in ps]
        yield
        nround = int(math.log2(C))
        for k in range(1, nround):
            rhs = [stack(s_, first_head).astype(BF16) for s_ in ss]
            if k + 1 < nround:
                rhs = [jnp.concatenate([stack(p, first_head).astype(BF16), sx], axis=1)
                       for p, sx in zip(ps, rhs)]
            outs = [dot(p.astype(BF16), sx) for p, sx in zip(ps, rhs)]
            if k + 1 < nround:
                ps = [o[:, :LANES] for o in outs]
                ss = [s_ + o[:, LANES:] for s_, o in zip(ss, outs)]
            else:
                ss = [s_ + o for s_, o in zip(ss, outs)]
            yield
        xs = [dot(s_.astype(BF16),
                  stack(jnp.concatenate([rv[rs, ls], rk[rs, ls]], axis=1), first_head2).astype(BF16))
              for s_, (rs, ls, _) in zip(ss, units)]
        yield
        for x, kq, (rs, ls, cc) in zip(xs, kqs, units):
            u_ref[slot, rs, ls] = x[:, :LANES]
            wq_ref[slot, cc, 0:C, ls] = x[:, LANES:].astype(BF16)
            wq_ref[slot, cc, C:2 * C, ls] = qd[rs, ls].astype(BF16)
            a_ref[slot, rs, ls] = (kq[C:2 * C] * decay[rs, ls]).astype(BF16)

    lss = [slice(LANES * p, LANES * (p + 1)) for p in range(npair)]

    def recurrence_stages(slot, seq_start):
        states = [jnp.where(seq_start, 0.0, st_ref[p]) for p in range(npair)]
        pending = None

        def finish(rs, os_):
            sq = jnp.concatenate([(o * o).astype(BF16) for o in os_], axis=0)
            ms_all = dot(sq, pair_ones) * (1.0 / HEAD_DIM)
            for p, (ls, o) in enumerate(zip(lss, os_)):
                ms = ms_all[p * C:(p + 1) * C]
                out_ref[rs, ls] = o * lax.rsqrt(ms + EPS) * nw_ref[:, ls] * sz_ref[rs, ls]

        for cc in range(TILE // C):
            rs = slice(cc * C, (cc + 1) * C)
            wqs = [dot(wq_ref[slot, cc, :, ls], st.astype(BF16)) for ls, st in zip(lss, states)]
            if pending is not None:
                finish(*pending)
            yield
            v_news = [u_ref[slot, rs, ls] - wq[0:C] for ls, wq in zip(lss, wqs)]
            kvs = [lax.dot_general(kd_ref[slot, rs, ls], v.astype(BF16), TN, preferred_element_type=F32)
                   for ls, v in zip(lss, v_news)]
            os_ = [wq[C:2 * C] + dot(a_ref[slot, rs, ls], stack(v, first_head).astype(BF16))
                   for ls, wq, v in zip(lss, wqs, v_news)]
            states = [st * gl_ref[slot, cc * SUBLANES:cc * SUBLANES + 1, ls] + jnp.where(same_head, kv, 0.0)
                      for ls, st, kv in zip(lss, states, kvs)]
            pending = (rs, os_)
            yield
        finish(*pending)
        for p in range(npair):
            st_ref[p] = states[p]

    return solve_stages, recurrence_stages


def _interleave_weighted(*gens_and_weights):
    live = [[gen, weight] for gen, weight in gens_and_weights]
    while live:
        for entry in list(live):
            for _ in range(entry[1]):
                try:
                    next(entry[0])
                except StopIteration:
                    live.remove(entry)
                    break


def _mixers_kernel(relb_ref, qlo_ref, qhi_ref, kb_ref, ksum_ref, va_ref,
                   y_ref, gb_ref, sz_ref, nw_ref,
                   oT_ref, og_ref,
                   bias_ref, addm_ref, qh_ref, lg_ref, moff_ref,
                   u_ref, wq_ref, a_ref, kd_ref, gl_ref, st_ref,
                   *, nblk, nbatch, tiles_per_seq, nsteps):
    step = pl.program_id(0)
    BLK = MOBA_BLOCK
    half = nblk // 2
    pair = jnp.minimum(step, nsteps - 2)
    hp = pair // (nbatch * (half // 2))
    b = (pair // (half // 2)) % nbatch
    scoring = step < nsteps - 1
    new_kv = (pair % (half // 2)) == 0

    solve_stages, recurrence_stages = _gdn_stages(
        y_ref, gb_ref, sz_ref, nw_ref, og_ref,
        u_ref, wq_ref, a_ref, kd_ref, gl_ref, st_ref)
    seq_start = ((step - 1) % tiles_per_seq) == 0

    @pl.when((b == 0) & new_kv & scoring)
    def _():
        kk = lax.broadcasted_iota(jnp.int32, (BLK, BLK), 0)
        qq = lax.broadcasted_iota(jnp.int32, (BLK, BLK), 1)
        for hh in range(2):
            h = 2 * hp + hh
            for kind in range(2):
                d = qq - kk + kind * BLK
                val = jnp.full((BLK, BLK), relb_ref[h, REL_BUCKETS - 1], F32)
                for bkt in range(REL_BUCKETS - 2, -1, -1):
                    val = jnp.where(d < BUCKET_LOWER[bkt + 1], relb_ref[h, bkt], val)
                val = val * LOG2E
                if kind == 0:
                    val = jnp.where(d >= 0, val, NEG)
                bias_ref[hh, kind] = val

    def key_means():
        per_tile = INPROJ_TILE // BLK
        ks = ksum_ref[...]
        km = jnp.concatenate([ks[j // per_tile, j % per_tile:j % per_tile + 1, :] for j in range(nblk)],
                             axis=0) * (1.0 / BLK)
        lane = lax.broadcasted_iota(jnp.int32, (nblk, LANES), 1)
        return jnp.concatenate([jnp.where(lane < HEAD_DIM, km, 0.0),
                                jnp.where(lane >= HEAD_DIM, km, 0.0)], axis=0)

    def item_tiles(t):
        i_hi = nblk - 1 - t
        tiles = [(0, t, "own"), (1, i_hi, "own"), (1, i_hi - 1, "prev")]
        if t >= 1:
            tiles.append((0, t - 1, "prev"))
        tiles += [(0, j, "far") for j in range(t - 1)]
        tiles += [(1, j, "far") for j in range(i_hi - 1)]
        assert len(tiles) == nblk + 1
        return tiles

    def score_stages(parity):
        ridx = lax.broadcasted_iota(jnp.int32, (nblk, BLK), 0)
        sub = lax.broadcasted_iota(jnp.int32, (LANES, BLK), 0)
        scale = HEAD_DIM ** -0.5 * LOG2E
        km = key_means()
        for e in range(2):
            t = 2 * parity + e
            slot = 2 * parity + e
            q_of = ((qlo_ref, e, t), (qhi_ref, 1 - e, nblk - 1 - t))
            for s, (q_ref, w, qi) in enumerate(q_of):
                qT = q_ref[0, w]
                gT = jnp.dot(km, qT, precision=HI, preferred_element_type=F32)
                past = ridx < qi
                for hh in range(2):
                    gm = jnp.where(past, gT[nblk * hh:nblk * (hh + 1)], -jnp.inf)
                    cnt = jnp.zeros((nblk, BLK), F32)
                    for jp in range(nblk):
                        row = gm[jp:jp + 1, :]
                        beats = (row > gm) | ((row == gm) & (ridx > jp))
                        cnt = cnt + jnp.where(beats, 1.0, 0.0)
                    visible = past & (cnt < MOBA_TOPK)
                    addm_ref[e, s, nblk * hh:nblk * (hh + 1), :] = jnp.where(visible, 0.0, NEG)
                    in_head = (sub >= HEAD_DIM * hh) & (sub < HEAD_DIM * (hh + 1))
                    qh_ref[e, s, hh] = jnp.where(in_head, qT * scale, 0.0).astype(BF16)
            yield
            tiles = item_tiles(t)
            for hh in range(2):
                cmax = {0: [], 1: []}
                offs = []
                for n, (s, kblk, cls) in enumerate(tiles):
                    lg = jnp.dot(kb_ref[0, kblk * BLK:(kblk + 1) * BLK, :], qh_ref[e, s, hh],
                                 preferred_element_type=F32)
                    if cls != "far":
                        lg = lg + bias_ref[hh, 0 if cls == "own" else 1]
                    lg_ref[slot, hh, n] = lg
                    cm = jnp.max(lg, axis=0, keepdims=True)
                    off = None
                    if cls != "own":
                        off = addm_ref[e, s, nblk * hh + kblk:nblk * hh + kblk + 1, :]
                        if cls == "far":
                            off = off + relb_ref[2 * hp + hh, REL_BUCKETS - 1] * LOG2E
                        cm = cm + off
                    cmax[s].append(cm)
                    offs.append(off)
                    yield
                m = {s: functools.reduce(jnp.maximum, cmax[s]) for s in (0, 1)}
                for n, (s, _, _) in enumerate(tiles):
                    moff_ref[slot, hh, n:n + 1, :] = m[s] if offs[n] is None else m[s] - offs[n]

    def softmax_pv_stages(parity):
        for e in range(2):
            t = 2 * parity + e
            slot = 2 * parity + e
            tiles = item_tiles(t)
            for hh in range(2):
                acc = {0: None, 1: None}
                for n, (s, kblk, _) in enumerate(tiles):
                    p = jnp.exp2(lg_ref[slot, hh, n] - moff_ref[slot, hh, n:n + 1, :])
                    pvn = jnp.dot(va_ref[0, kblk, hh], p.astype(BF16),
                                  preferred_element_type=F32)
                    acc[s] = pvn if acc[s] is None else acc[s] + pvn
                    yield
                for s in (0, 1):
                    oT_ref[0, 2 * e + s, HEAD_DIM * hh:HEAD_DIM * (hh + 1), :] = (
                        acc[s][0:HEAD_DIM] / acc[s][HEAD_DIM:HEAD_DIM + 1])

    ATT, GDN = 2, 1

    @pl.when(step == 0)
    def _():
        st_ref[...] = jnp.zeros(st_ref.shape, F32)
        _interleave_weighted((score_stages(0), ATT), (solve_stages(0), GDN))

    for parity in range(2):
        @pl.when((step > 0) & (step < nsteps - 1) & (step % 2 == parity))
        def _(parity=parity):
            _interleave_weighted((recurrence_stages(1 - parity, seq_start), GDN),
                                 (score_stages(parity), ATT),
                                 (solve_stages(parity), GDN),
                                 (softmax_pv_stages(1 - parity), ATT))

    @pl.when(step == nsteps - 1)
    def _():
        last = (nsteps - 2) % 2
        _interleave_weighted((recurrence_stages(last, seq_start), GDN), (softmax_pv_stages(last), ATT))


def _token_mixers(rel_bias, qT, k3, ksum, va, gqkv, sz, gb, nw_row, B, S):
    T = B * S
    W = GDN_WIDTH
    TILE = GDN_TILE
    nblk = S // MOBA_BLOCK
    half = nblk // 2
    assert TILE == GDN_HALF and S % TILE == 0
    assert BUCKET_LOWER[REL_BUCKETS - 1] <= MOBA_BLOCK + 1
    assert nblk == 8 and nblk + 1 <= 2 * SUBLANES
    nchunk = TILE // GDN_CHUNK
    ntiles = T // TILE
    npairs = (ATT_HEADS // 2) * B * (half // 2)
    assert npairs == ntiles
    nsteps = ntiles + 1

    def scored(s):
        p = jnp.minimum(s, npairs - 1)
        return p // (B * (half // 2)), (p // (half // 2)) % B, p % (half // 2)

    def done(s):
        p = jnp.maximum(s - 1, 0)
        return p // (B * (half // 2)), (p // (half // 2)) % B, p % (half // 2)

    def q_lo(s):
        hp, b, m = scored(s)
        return (b, m, hp, 0)

    def q_hi(s):
        hp, b, m = scored(s)
        return (b, half - 1 - m, hp, 0)

    def k_blk(s):
        hp, b, _ = scored(s)
        return (b, 0, hp)

    def v_blk(s):
        hp, b, _ = done(s)
        return (b, 0, hp, 0, 0)

    def o_blk(s):
        hp, b, m = done(s)
        return (b, m, hp, 0)

    cur_tile = lambda off: (lambda s: (jnp.minimum(s, ntiles - 1), off))
    prev_tile = lambda s: (jnp.maximum(s - 1, 0), 0)
    return pl.pallas_call(
        functools.partial(_mixers_kernel, nblk=nblk, nbatch=B, tiles_per_seq=S // TILE, nsteps=nsteps),
        grid=(nsteps,),
        in_specs=[
            pl.BlockSpec(memory_space=pltpu.SMEM),
            pl.BlockSpec((1, 2, LANES, MOBA_BLOCK), q_lo),
            pl.BlockSpec((1, 2, LANES, MOBA_BLOCK), q_hi),
            pl.BlockSpec((1, S, LANES), k_blk),
            pl.BlockSpec((S // INPROJ_TILE, SUBLANES, LANES), k_blk),
            pl.BlockSpec((1, nblk, 2, V_ROWS, MOBA_BLOCK), v_blk),
            pl.BlockSpec((TILE, 3 * W), cur_tile(0)),
            pl.BlockSpec((TILE, LANES), cur_tile(0)),
            pl.BlockSpec((TILE, W), prev_tile),
            pl.BlockSpec((1, W), lambda s: (0, 0)),
        ],
        out_specs=[
            pl.BlockSpec((1, 4, LANES, MOBA_BLOCK), o_blk),
            pl.BlockSpec((TILE, W), prev_tile),
        ],
        out_shape=[
            jax.ShapeDtypeStruct((B, nblk, ATT_WIDTH, MOBA_BLOCK), F32),
            jax.ShapeDtypeStruct((T, GDN_WIDTH), F32),
        ],
        scratch_shapes=[
            pltpu.VMEM((2, 2, MOBA_BLOCK, MOBA_BLOCK), F32),
            pltpu.VMEM((2, 2, 2 * nblk, MOBA_BLOCK), F32),
            pltpu.VMEM((2, 2, 2, LANES, MOBA_BLOCK), BF16),
            pltpu.VMEM((4, 2, nblk + 1, MOBA_BLOCK, MOBA_BLOCK), F32),
            pltpu.VMEM((4, 2, 2 * SUBLANES, MOBA_BLOCK), F32),
            pltpu.VMEM((2, TILE, W), F32),
            pltpu.VMEM((2, nchunk, 2 * GDN_CHUNK, W), BF16),
            pltpu.VMEM((2, TILE, W), BF16),
            pltpu.VMEM((2, TILE, W), BF16),
            pltpu.VMEM((2, nchunk * SUBLANES, W), F32),
            pltpu.VMEM((W // LANES, LANES, LANES), F32),
        ],
        compiler_params=pltpu.CompilerParams(
            dimension_semantics=("arbitrary",), vmem_limit_bytes=VMEM_LIMIT),
        name="token_mixers",
    )(rel_bias, qT, qT, k3, ksum, va, gqkv, gb, sz, nw_row)


def _out_mlp_kernel(x_ref, oTa_ref, oTb_ref, og_ref, woa_ref, wog_ref, pmn_ref, pre_ref, post_ref,
                    wup_ref, wdn_ref, out_ref):
    oT = jnp.concatenate([oTa_ref[0, 0], oTb_ref[0, 0]], axis=1)
    o_att = oT.T.astype(BF16)
    mix = jnp.dot(o_att, woa_ref[...], preferred_element_type=F32)
    mix = mix + jnp.dot(og_ref[...].astype(BF16), wog_ref[...], preferred_element_type=F32)
    x1 = x_ref[...] + _rms(mix, pmn_ref[...])
    h = _rms(x1, pre_ref[...]).astype(BF16)
    acc = jnp.zeros((ROW_TILE, D_MODEL), F32)
    for c in range(D_FF // FF_TILE):
        up = jnp.dot(h, wup_ref[:, c * FF_TILE:(c + 1) * FF_TILE], preferred_element_type=F32)
        act = jnp.square(jnp.maximum(up, 0.0)).astype(BF16)
        acc = acc + jnp.dot(act, wdn_ref[c * FF_TILE:(c + 1) * FF_TILE, :], preferred_element_type=F32)
    out_ref[...] = x1 + _rms(acc, post_ref[...])


def _out_mlp(xf, oT, og, woa, wog, pmn, pre, post, wup, wdn, B, S):
    T = B * S
    nblk = S // MOBA_BLOCK
    tiles_per_seq = S // ROW_TILE
    assert ROW_TILE == 2 * MOBA_BLOCK
    const = lambda i: (0, 0)
    row = lambda i: (i, 0)

    def att_block(which):
        def index(i):
            blk = 2 * (i % tiles_per_seq) + which
            return (i // tiles_per_seq, _paired_pos(blk, nblk), 0, 0)
        return index

    single = dict(pipeline_mode=pl.Buffered(1))
    return pl.pallas_call(
        _out_mlp_kernel,
        grid=(T // ROW_TILE,),
        in_specs=[
            pl.BlockSpec((ROW_TILE, D_MODEL), row),
            pl.BlockSpec((1, 1, ATT_WIDTH, MOBA_BLOCK), att_block(0)),
            pl.BlockSpec((1, 1, ATT_WIDTH, MOBA_BLOCK), att_block(1)),
            pl.BlockSpec((ROW_TILE, GDN_WIDTH), row),
            pl.BlockSpec(woa.shape, const, **single),
            pl.BlockSpec(wog.shape, const, **single),
            pl.BlockSpec((1, D_MODEL), const),
            pl.BlockSpec((1, D_MODEL), const),
            pl.BlockSpec((1, D_MODEL), const),
            pl.BlockSpec(wup.shape, const, **single),
            pl.BlockSpec(wdn.shape, const, **single),
        ],
        out_specs=pl.BlockSpec((ROW_TILE, D_MODEL), row),
        out_shape=jax.ShapeDtypeStruct((T, D_MODEL), F32),
        compiler_params=pltpu.CompilerParams(
            dimension_semantics=("arbitrary",), vmem_limit_bytes=VMEM_LIMIT),
        name="out_mlp",
    )(xf, oT, oT, og, woa, wog, pmn, pre, post, wup, wdn)


def kernel(x, w_in, w_out, conv_w, A_log, dt_bias, gdn_norm_w, rel_bias, pre_mix_norm,
           post_mix_norm, pre_mlp_norm, post_mlp_norm, w_up, w_down):
    B, S, D = x.shape
    assert D == D_MODEL and S % ROW_TILE == 0 and S % MOBA_BLOCK == 0
    T = B * S
    depth = w_in.shape[0]
    xf = x.reshape(T, D)
    o0, o1, o2, o3, o4 = 0, ATT_WIDTH, 2 * ATT_WIDTH, 3 * ATT_WIDTH, 3 * ATT_WIDTH + 3 * GDN_WIDTH
    o5 = o4 + GDN_WIDTH
    for l in range(depth):
        wi = w_in[l]
        wqT = wi[:, o0:o1].T.astype(BF16)
        wk = wi[:, o1:o2].astype(BF16)
        wvT = wi[:, o2:o3].T.astype(BF16)
        wg = wi[:, o3:o4].astype(BF16)
        wz = wi[:, o4:o5].astype(BF16)
        wab = jnp.pad(wi[:, o5:], ((0, 0), (0, LANES - 2 * GDN_HEADS))).astype(BF16)
        pad8 = lambda v: jnp.pad(v.astype(F32), (0, LANES - GDN_HEADS))[None, :]
        qT, k, ksum, va, gqkv, sz, gb = _inproj(xf, pre_mix_norm[l][None, :], wqT, wk, wvT, wg, wz, wab,
                                                conv_w[l], pad8(A_log[l]), pad8(dt_bias[l]), B, S)
        oT, og = _token_mixers(rel_bias.astype(F32), qT, k.reshape(B, S, ATT_WIDTH), ksum, va, gqkv, sz,
                               gb, jnp.tile(gdn_norm_w[l], GDN_HEADS)[None, :], B, S)
        wo = w_out[l].astype(BF16)
        xf = _out_mlp(xf, oT, og, wo[:ATT_WIDTH], wo[ATT_WIDTH:], post_mix_norm[l][None, :],
                      pre_mlp_norm[l][None, :], post_mlp_norm[l][None, :],
                      w_up[l].astype(BF16), w_down[l].astype(BF16), B, S)
    return xf.reshape(B, S, D)
```
